```python
import math
import jax, jax.numpy as jnp
from jax import lax
import numpy as np

D_MODEL = 1024
BATCH = 16
SEQ = 2048
DEPTH = 1

HEAD_DIM = 64
SWA_Q_HEADS = 8
SWA_KV_HEADS = 2
SWA_WINDOW = 128
DIL_HEADS = 4
DIL_PAIRS = ((128, 1), (512, 4), (2048, 16))
MEM_HEADS = 4
MEM_LEN = 256

BLOCK = 128
ROPE_THETA = 10000.0
LN_EPS = 1e-5
RMS_EPS = 1e-6
DEEPNORM_ALPHA = (2 * DEPTH) ** 0.25
DEEPNORM_BETA = (8 * DEPTH) ** -0.25

W_A = SWA_Q_HEADS * HEAD_DIM
W_KV_A = SWA_KV_HEADS * HEAD_DIM
W_B = DIL_HEADS * HEAD_DIM
W_C = MEM_HEADS * HEAD_DIM
D_MIX = W_A + W_B + W_C
IN_SPLITS = (W_A, W_KV_A, W_KV_A, W_B, W_B, W_B, W_C, D_MIX)
D_IN = sum(IN_SPLITS)

kernel_name = "hymba_swa_sink_dilated_mem_deepnorm"


def rope_tables(seq_len):
    pos = jnp.arange(seq_len, dtype=jnp.float32)
    inv = ROPE_THETA ** (-jnp.arange(0, HEAD_DIM, 2, dtype=jnp.float32) / HEAD_DIM)
    ang = pos[:, None] * inv[None, :]
    ang = jnp.concatenate([ang, ang], axis=-1)
    return jnp.cos(ang), jnp.sin(ang)


def apply_rope(t, cos, sin):
    tf = t.astype(jnp.float32)
    half = HEAD_DIM // 2
    rot = jnp.concatenate([-tf[..., half:], tf[..., :half]], axis=-1)
    return (tf * cos[None, :, None, :] + rot * sin[None, :, None, :]).astype(t.dtype)


def banded_causal_attention(q, k, v, max_dist, sinks=None):
    N, L, H, D = q.shape
    Hkv = k.shape[2]
    G = H // Hkv
    n_blk = -(-L // BLOCK)
    pad = n_blk * BLOCK - L
    if pad:
        q, k, v = [jnp.pad(t, ((0, 0), (0, pad), (0, 0), (0, 0))) for t in (q, k, v)]
    qb = q.reshape(N, n_blk, BLOCK, Hkv, G, D)
    kb = k.reshape(N, n_blk, BLOCK, Hkv, D)
    vb = v.reshape(N, n_blk, BLOCK, Hkv, D)

    def with_prev(t):
        prev = jnp.pad(t, ((0, 0), (1, 0), (0, 0), (0, 0), (0, 0)))[:, :-1]
        return jnp.concatenate([prev, t], axis=2)

    kk, vv = with_prev(kb), with_prev(vb)
    s = jnp.einsum('nbqhgd,nbkhd->nbhgqk', qb, kk,
                   preferred_element_type=jnp.float32) * (D ** -0.5)
    qi = jnp.arange(BLOCK)[:, None]
    kj = jnp.arange(2 * BLOCK)[None, :]
    dist = qi + BLOCK - kj
    band = (dist >= 0) & (dist <= max_dist)
    blk = jnp.arange(n_blk)[:, None, None]
    mask = band[None] & ((blk > 0) | (kj >= BLOCK)[None])
    s = jnp.where(mask[None, :, None, None], s, -jnp.inf)
    m = s.max(axis=-1)
    if sinks is not None:
        sink = sinks.astype(jnp.float32).reshape(Hkv, G)[None, None, :, :, None]
        m = jnp.maximum(m, sink)
    p = jnp.exp(s - m[..., None])
    denom = p.sum(axis=-1)
    if sinks is not None:
        denom = denom + jnp.exp(sink - m)
    o = jnp.einsum('nbhgqk,nbkhd->nbqhgd', (p / denom[..., None]).astype(v.dtype), vv)
    lse = m + jnp.log(denom)
    o = o.reshape(N, n_blk * BLOCK, H, D)[:, :L]
    lse = lse.transpose(0, 1, 4, 2, 3).reshape(N, n_blk * BLOCK, H)[:, :L]
    return o, lse


def dilated_attention(q, k, v):
    B, S, H, D = q.shape
    outs, lses = [], []
    for window, dil in DIL_PAIRS:
        Ls = S // dil

        def to_stream(t):
            return t.reshape(B, Ls, dil, H, D).transpose(0, 2, 1, 3, 4).reshape(B * dil, Ls, H, D)

        o, lse = banded_causal_attention(to_stream(q), to_stream(k), to_stream(v), window // dil)
        outs.append(o.reshape(B, dil, Ls, H, D).transpose(0, 2, 1, 3, 4).reshape(B, S, H, D))
        lses.append(lse.reshape(B, dil, Ls, H).transpose(0, 2, 1, 3).reshape(B, S, H))
    w = jax.nn.softmax(jnp.stack(lses), axis=0)
    o = jnp.einsum('pbsh,pbshd->bshd', w, jnp.stack(outs).astype(jnp.float32))
    return o.astype(q.dtype)


def memory_attention(q, mk, mv):
    s = jnp.einsum('bshd,bmhd->bhsm', q, mk, preferred_element_type=jnp.float32) * (HEAD_DIM ** -0.5)
    p = jax.nn.softmax(s, axis=-1)
    return jnp.einsum('bhsm,bmhd->bshd', p.astype(mv.dtype), mv)


def rms_normalize(t):
    tf = t.astype(jnp.float32)
    return tf * lax.rsqrt(jnp.mean(tf * tf, axis=-1, keepdims=True) + RMS_EPS)


def layer_norm(t, gain, bias):
    tf = t.astype(jnp.float32)
    mu = jnp.mean(tf, axis=-1, keepdims=True)
    var = jnp.mean(jnp.square(tf - mu), axis=-1, keepdims=True)
    return (tf - mu) * lax.rsqrt(var + LN_EPS) * gain.astype(jnp.float32) + bias.astype(jnp.float32)


def _fwd_setup_inputs(seed: int = 0) -> dict:
    key = jax.random.key(seed)
    ks = jax.random.split(key, 10)
    x = jax.random.normal(ks[0], (BATCH, SEQ, D_MODEL), jnp.float32)
    mem = jax.random.normal(ks[1], (BATCH, MEM_LEN, D_MODEL), jnp.float32)
    w_in = jax.random.normal(ks[2], (DEPTH, D_MODEL, D_IN), jnp.float32) * D_MODEL ** -0.5
    b_in = 0.02 * jax.random.normal(ks[3], (DEPTH, D_IN), jnp.float32)
    w_mem = jax.random.normal(ks[4], (DEPTH, D_MODEL, 2 * W_C), jnp.float32) * D_MODEL ** -0.5
    attn_sinks = 0.5 * jax.random.normal(ks[5], (DEPTH, SWA_Q_HEADS), jnp.float32)
    g_branch = 1.0 + 0.05 * jax.random.normal(ks[6], (DEPTH, D_MIX), jnp.float32)
    w_out = (jax.random.normal(ks[7], (DEPTH, D_MIX, D_MODEL), jnp.float32)
             * D_MIX ** -0.5 * DEEPNORM_BETA)
    ln_gain = 1.0 + 0.05 * jax.random.normal(ks[8], (DEPTH, D_MODEL), jnp.float32)
    ln_bias = 0.02 * jax.random.normal(ks[9], (DEPTH, D_MODEL), jnp.float32)
    return {"x": x, "mem": mem, "w_in": w_in, "b_in": b_in, "w_mem": w_mem,
            "attn_sinks": attn_sinks, "g_branch": g_branch, "w_out": w_out,
            "ln_gain": ln_gain, "ln_bias": ln_bias}


def _fwd_reference(x, mem, w_in, b_in, w_mem, attn_sinks, g_branch, w_out, ln_gain, ln_bias):
    B, S, _ = x.shape
    M = mem.shape[1]
    cos, sin = rope_tables(S)
    split_idx = [int(i) for i in np.cumsum(IN_SPLITS)[:-1]]

    def heads(t, n):
        return t.reshape(t.shape[0], t.shape[1], n, HEAD_DIM)

    for l in range(DEPTH):
        h = jnp.einsum('bsd,de->bse', x, w_in[l]) + b_in[l]
        qa, ka, va, qb, kb, vb, qc, z = jnp.split(h, split_idx, axis=-1)

        oa, _ = banded_causal_attention(apply_rope(heads(qa, SWA_Q_HEADS), cos, sin),
                                        apply_rope(heads(ka, SWA_KV_HEADS), cos, sin),
                                        heads(va, SWA_KV_HEADS),
                                        SWA_WINDOW - 1, attn_sinks[l])
        ob = dilated_attention(apply_rope(heads(qb, DIL_HEADS), cos, sin),
                               apply_rope(heads(kb, DIL_HEADS), cos, sin),
                               heads(vb, DIL_HEADS))
        mkv = jnp.einsum('bmd,de->bme', mem, w_mem[l])
        mk, mv = jnp.split(mkv, 2, axis=-1)
        oc = memory_attention(heads(qc, MEM_HEADS), heads(mk, MEM_HEADS), heads(mv, MEM_HEADS))

        y = jnp.concatenate([rms_normalize(oa.reshape(B, S, W_A)),
                             rms_normalize(ob.reshape(B, S, W_B)),
                             rms_normalize(oc.reshape(B, S, W_C))], axis=-1)
        y = y * g_branch[l].astype(jnp.float32) * jax.nn.silu(z.astype(jnp.float32))
        y = jnp.einsum('bse,ed->bsd', y.astype(x.dtype), w_out[l])

        x = layer_norm(DEEPNORM_ALPHA * x + y, ln_gain[l], ln_bias[l]).astype(x.dtype)
    return x


import jax as _jax
import jax.numpy as _jnp

TWIN_FORMAT = 'train_step'
FWD_PARAMS = ['x', 'mem', 'w_in', 'b_in', 'w_mem', 'attn_sinks', 'g_branch', 'w_out', 'ln_gain', 'ln_bias']
TWIN_WEIGHTS = ['w_in', 'b_in', 'w_mem', 'attn_sinks', 'g_branch', 'w_out', 'ln_gain', 'ln_bias']
TWIN_DIFF_INPUT = 'x'
TWIN_INPUTS = ['x', 'mem', 'w_in', 'b_in', 'w_mem', 'attn_sinks', 'g_branch', 'w_out', 'ln_gain', 'ln_bias', 'loss_target', 'm_w_in', 'm_b_in', 'm_w_mem', 'm_attn_sinks', 'm_g_branch', 'm_w_out', 'm_ln_gain', 'm_ln_bias', 'v_w_in', 'v_b_in', 'v_w_mem', 'v_attn_sinks', 'v_g_branch', 'v_w_out', 'v_ln_gain', 'v_ln_bias']
TWIN_OUTPUTS = ['loss', 'grad_x', 'grad_w_in', 'grad_b_in', 'grad_w_mem', 'grad_attn_sinks', 'grad_g_branch', 'grad_w_out', 'grad_ln_gain', 'grad_ln_bias', 'delta_w_in', 'delta_b_in', 'delta_w_mem', 'delta_attn_sinks', 'delta_g_branch', 'delta_w_out', 'delta_ln_gain', 'delta_ln_bias', 'new_m_w_in', 'new_m_b_in', 'new_m_w_mem', 'new_m_attn_sinks', 'new_m_g_branch', 'new_m_w_out', 'new_m_ln_gain', 'new_m_ln_bias', 'new_v_w_in', 'new_v_b_in', 'new_v_w_mem', 'new_v_attn_sinks', 'new_v_g_branch', 'new_v_w_out', 'new_v_ln_gain', 'new_v_ln_bias']
TWIN_LEAF_KINDS = {'loss': 'loss', 'grad_x': 'grad_x', 'grad_w_in': 'grad_w', 'grad_b_in': 'grad_w', 'grad_w_mem': 'grad_w', 'grad_attn_sinks': 'grad_w', 'grad_g_branch': 'grad_w', 'grad_w_out': 'grad_w', 'grad_ln_gain': 'grad_w', 'grad_ln_bias': 'grad_w', 'delta_w_in': 'delta_w', 'delta_b_in': 'delta_w', 'delta_w_mem': 'delta_w', 'delta_attn_sinks': 'delta_w', 'delta_g_branch': 'delta_w', 'delta_w_out': 'delta_w', 'delta_ln_gain': 'delta_w', 'delta_ln_bias': 'delta_w', 'new_m_w_in': 'new_m', 'new_m_b_in': 'new_m', 'new_m_w_mem': 'new_m', 'new_m_attn_sinks': 'new_m', 'new_m_g_branch': 'new_m', 'new_m_w_out': 'new_m', 'new_m_ln_gain': 'new_m', 'new_m_ln_bias': 'new_m', 'new_v_w_in': 'new_v', 'new_v_b_in': 'new_v', 'new_v_w_mem': 'new_v', 'new_v_attn_sinks': 'new_v', 'new_v_g_branch': 'new_v', 'new_v_w_out': 'new_v', 'new_v_ln_gain': 'new_v', 'new_v_ln_bias': 'new_v'}


def _forward(args):
    return _fwd_reference(*[args[k] for k in FWD_PARAMS])


def _output_shape():
    out = _jax.eval_shape(lambda: _forward(_fwd_setup_inputs(0)))
    return out.shape, out.dtype

N_MICROBATCH = 1
ADAM_LR = 0.001
ADAM_B1 = 0.9
ADAM_B2 = 0.999
ADAM_EPS = 1e-08
ADAM_WD = 0.01
ADAM_STEP = 10
PER_EXAMPLE_BATCH_AXIS = {'x': 0, 'mem': 0, 'loss_target': 0}
SHARED_INPUTS = []
_WEIGHT_DTYPES = {'w_in': _jnp.float32, 'b_in': _jnp.float32, 'w_mem': _jnp.float32, 'attn_sinks': _jnp.float32, 'g_branch': _jnp.float32, 'w_out': _jnp.float32, 'ln_gain': _jnp.float32, 'ln_bias': _jnp.float32}
MOMENT_SCALE = {'w_in': 5.742092e-02, 'b_in': 3.556622e-01, 'w_mem': 5.025809e-02, 'attn_sinks': 1.886379e-02, 'g_branch': 5.412709e-02, 'w_out': 8.704725e-02, 'ln_gain': 3.205578e+01, 'ln_bias': 6.372856e-01}


def _to_microbatches(a, axis):
    t = _jnp.moveaxis(a, axis, 0)
    t = t.reshape((N_MICROBATCH, t.shape[0] // N_MICROBATCH) + t.shape[1:])
    return _jnp.moveaxis(t, 1, axis + 1)


def setup_inputs(seed: int = 0) -> dict:
    inp = _fwd_setup_inputs(seed)
    key = _jax.random.fold_in(_jax.random.key(seed), 7919)
    shape, _ = _output_shape()
    out = dict(inp)
    out["loss_target"] = _jax.random.normal(_jax.random.fold_in(key, 0), shape, _jnp.float32)
    for i, name in enumerate(TWIN_WEIGHTS):
        w = inp[name].astype(_jnp.float32)
        if MOMENT_SCALE is None:
            s = _jnp.sqrt(_jnp.mean(_jnp.square(w)) + 1e-30)
        else:
            s = MOMENT_SCALE[name]
        km, kv = _jax.random.split(_jax.random.fold_in(key, i + 1))
        out[name] = w
        out["m_" + name] = s * _jax.random.normal(km, w.shape, _jnp.float32)
        out["v_" + name] = (s * s) * _jax.random.uniform(kv, w.shape, _jnp.float32, 0.5, 1.5)
    if N_MICROBATCH > 1:
        for name, axis in PER_EXAMPLE_BATCH_AXIS.items():
            out[name] = _to_microbatches(out[name], axis)
    return {'x': out['x'], 'mem': out['mem'], 'w_in': out['w_in'], 'b_in': out['b_in'], 'w_mem': out['w_mem'], 'attn_sinks': out['attn_sinks'], 'g_branch': out['g_branch'], 'w_out': out['w_out'], 'ln_gain': out['ln_gain'], 'ln_bias': out['ln_bias'], 'loss_target': out['loss_target'], 'm_w_in': out['m_w_in'], 'm_b_in': out['m_b_in'], 'm_w_mem': out['m_w_mem'], 'm_attn_sinks': out['m_attn_sinks'], 'm_g_branch': out['m_g_branch'], 'm_w_out': out['m_w_out'], 'm_ln_gain': out['m_ln_gain'], 'm_ln_bias': out['m_ln_bias'], 'v_w_in': out['v_w_in'], 'v_b_in': out['v_b_in'], 'v_w_mem': out['v_w_mem'], 'v_attn_sinks': out['v_attn_sinks'], 'v_g_branch': out['v_g_branch'], 'v_w_out': out['v_w_out'], 'v_ln_gain': out['v_ln_gain'], 'v_ln_bias': out['v_ln_bias']}


def _loss(weights, diff, rest, loss_target):
    with _jax.named_scope("forward"):
        args = {**rest, TWIN_DIFF_INPUT: diff, **{k: w.astype(_WEIGHT_DTYPES[k]) for k, w in weights.items()}}
        y = _forward(args)
    with _jax.named_scope("loss_head"):
        err = _jnp.square(y.astype(_jnp.float32) - loss_target)
        return 0.5 * _jnp.sum(_jnp.mean(err, axis=-1)) if err.ndim else 0.5 * err


def _adamw(w, g, m, v):
    m = ADAM_B1 * m + (1.0 - ADAM_B1) * g
    v = ADAM_B2 * v + (1.0 - ADAM_B2) * _jnp.square(g)
    m_hat = m / (1.0 - ADAM_B1 ** ADAM_STEP)
    v_hat = v / (1.0 - ADAM_B2 ** ADAM_STEP)
    delta = -ADAM_LR * (m_hat / (_jnp.sqrt(v_hat) + ADAM_EPS) + ADAM_WD * w)
    return delta, m, v


def reference(x, mem, w_in, b_in, w_mem, attn_sinks, g_branch, w_out, ln_gain, ln_bias, loss_target, m_w_in, m_b_in, m_w_mem, m_attn_sinks, m_g_branch, m_w_out, m_ln_gain, m_ln_bias, v_w_in, v_b_in, v_w_mem, v_attn_sinks, v_g_branch, v_w_out, v_ln_gain, v_ln_bias):
    given = dict(x=x, mem=mem, w_in=w_in, b_in=b_in, w_mem=w_mem, attn_sinks=attn_sinks, g_branch=g_branch, w_out=w_out, ln_gain=ln_gain, ln_bias=ln_bias, loss_target=loss_target, m_w_in=m_w_in, m_b_in=m_b_in, m_w_mem=m_w_mem, m_attn_sinks=m_attn_sinks, m_g_branch=m_g_branch, m_w_out=m_w_out, m_ln_gain=m_ln_gain, m_ln_bias=m_ln_bias, v_w_in=v_w_in, v_b_in=v_b_in, v_w_mem=v_w_mem, v_attn_sinks=v_attn_sinks, v_g_branch=v_g_branch, v_w_out=v_w_out, v_ln_gain=v_ln_gain, v_ln_bias=v_ln_bias)
    weights = {n: given[n] for n in TWIN_WEIGHTS}
    shared = {n: given[n] for n in SHARED_INPUTS}
    per_example = {n: given[n] for n in ['x', 'mem']}
    grad_fn = _jax.value_and_grad(_loss, argnums=(0, 1))

    def one_microbatch(ex, loss_target):
        ex = dict(ex)
        diff = ex.pop(TWIN_DIFF_INPUT)
        return grad_fn(weights, diff, {**shared, **ex}, loss_target)

    if N_MICROBATCH == 1:
        loss, (grad_w, grad_x) = one_microbatch(per_example, given["loss_target"])
    else:
        def body(carry, xs):
            loss_sum, grad_sum = carry
            l_k, (gw_k, gx_k) = one_microbatch(xs[0], xs[1])
            with _jax.named_scope("update"):
                return (loss_sum + l_k, _jax.tree.map(_jnp.add, grad_sum, gw_k)), gx_k

        init = (_jnp.zeros((), _jnp.float32), _jax.tree.map(_jnp.zeros_like, weights))
        (loss, grad_w), grad_x = _jax.lax.scan(body, init, (per_example, given["loss_target"]))
    with _jax.named_scope("update"):
        delta_w, new_m, new_v = {}, {}, {}
        for n in TWIN_WEIGHTS:
            delta_w[n], new_m[n], new_v[n] = _adamw(weights[n], grad_w[n], given["m_" + n], given["v_" + n])
    return (loss, grad_x, *[grad_w[n] for n in TWIN_WEIGHTS], *[delta_w[n] for n in TWIN_WEIGHTS],
            *[new_m[n] for n in TWIN_WEIGHTS], *[new_v[n] for n in TWIN_WEIGHTS])
```

```python
import functools
import math

import jax
import jax.numpy as jnp
from jax import lax
from jax.experimental import pallas as pl
from jax.experimental.pallas import tpu as pltpu

F32 = jnp.float32
MXU = jnp.bfloat16

D_MODEL = 1024
SEQ = 2048
HEAD = 64
BLK = 128
SWA_Q, SWA_KV = 8, 2
DIL_H = 4
MEM_H = 4
MEM_LEN = 256
W_A, W_KVA, W_B, W_C = 512, 128, 256, 256
D_MIX = 1024
D_IN = 2816
N_DEV = 8
COLS_PER_DEV = D_IN // N_DEV
ROWS_PER_DEV = D_MODEL // N_DEV
ROPE_THETA = 10000.0
LN_EPS = 1e-5
RMS_EPS = 1e-6
ALPHA = 2.0 ** 0.25
Q_SCALE = HEAD ** -0.5
NEG = -1e30
SMALL_ROWS = 48
VMEM_LIMIT = 56 * 1024 * 1024

ADAM_LR = 0.001
ADAM_B1 = 0.9
ADAM_B2 = 0.999
ADAM_EPS = 1e-08
ADAM_WD = 0.01
ADAM_STEP = 10

MESH = pl.DeviceIdType.MESH


def _cparams(sem=None):
    return pltpu.CompilerParams(dimension_semantics=sem, vmem_limit_bytes=VMEM_LIMIT)


def _dot(a, b):
    return jnp.dot(a, b, preferred_element_type=F32)


def _dot_nt(a, b):
    return lax.dot_general(a, b, (((1,), (1,)), ((), ())), preferred_element_type=F32)


def _dot_tn(a, b):
    return jnp.dot(a.T.astype(MXU), b, preferred_element_type=F32)


def _rope(t, tab, sign):
    cos, sa, sb = tab[:, 0:128], tab[:, 128:256], tab[:, 256:384]
    outs = []
    for c in range(t.shape[1] // 128):
        tc = t[:, c * 128:(c + 1) * 128]
        r = pltpu.roll(tc, 96, 1) * sa + pltpu.roll(tc, 32, 1) * sb
        outs.append(tc * cos + r if sign > 0 else tc * cos - r)
    return outs[0] if len(outs) == 1 else jnp.concatenate(outs, axis=1)


def _rope_table():
    pos = jnp.arange(SEQ, dtype=F32)
    inv = ROPE_THETA ** (-jnp.arange(0, HEAD, 2, dtype=F32) / HEAD)
    ang = pos[:, None] * inv[None, :]
    ang = jnp.concatenate([ang, ang], axis=-1)
    cos, sin = jnp.cos(ang), jnp.sin(ang)
    lane = jnp.arange(HEAD)[None, :]
    sa = jnp.where(lane < HEAD // 2, -sin, 0.0)
    sb = jnp.where(lane >= HEAD // 2, sin, 0.0)
    two = lambda t: jnp.concatenate([t, t], axis=-1)
    return jnp.concatenate([two(cos), two(sa), two(sb)], axis=-1).astype(F32)


def _in_proj(x2, w_in, b_in, tab):
    T = x2.shape[0]
    tm = 512

    def body(x_ref, w_ref, b_ref, tab_ref, qkva_ref, qkvb_ref, qc_ref, z_ref):
        xb = x_ref[...].astype(MXU)
        tab = tab_ref[...]

        def seg(c0, c1):
            return _dot(xb, w_ref[:, c0:c1]) + b_ref[:, c0:c1]

        qkva_ref[:, 0:512] = (_rope(seg(0, 512), tab, 1) * Q_SCALE).astype(MXU)
        qkva_ref[:, 512:640] = _rope(seg(512, 640), tab, 1).astype(MXU)
        qkva_ref[:, 640:768] = seg(640, 768).astype(MXU)
        qkvb_ref[:, 0:256] = (_rope(seg(768, 1024), tab, 1) * Q_SCALE).astype(MXU)
        qkvb_ref[:, 256:512] = _rope(seg(1024, 1280), tab, 1).astype(MXU)
        qkvb_ref[:, 512:768] = seg(1280, 1536).astype(MXU)
        qc_ref[...] = (seg(1536, 1792) * Q_SCALE).astype(MXU)
        z_ref[...] = seg(1792, 2816)

    nt = SEQ // tm
    return pl.pallas_call(
        body, name="in_proj_fwd",
        grid=(T // tm,),
        in_specs=[pl.BlockSpec((tm, D_MODEL), lambda i: (i, 0)),
                  pl.BlockSpec((D_MODEL, D_IN), lambda i: (0, 0)),
                  pl.BlockSpec((1, D_IN), lambda i: (0, 0)),
                  pl.BlockSpec((tm, 384), lambda i: (i % nt, 0))],
        out_specs=[pl.BlockSpec((tm, 768), lambda i: (i, 0)),
                   pl.BlockSpec((tm, 768), lambda i: (i, 0)),
                   pl.BlockSpec((tm, 256), lambda i: (i, 0)),
                   pl.BlockSpec((tm, D_MIX), lambda i: (i, 0))],
        out_shape=[jax.ShapeDtypeStruct((T, 768), MXU), jax.ShapeDtypeStruct((T, 768), MXU),
                   jax.ShapeDtypeStruct((T, 256), MXU), jax.ShapeDtypeStruct((T, D_MIX), F32)],
        compiler_params=_cparams(("arbitrary",)),
    )(x2, w_in, b_in, tab)


def _band_masks(max_dist):
    qi = lax.broadcasted_iota(jnp.int32, (BLK, 2 * BLK), 0)
    kj = lax.broadcasted_iota(jnp.int32, (BLK, 2 * BLK), 1)
    dist = qi + BLK - kj
    band = (dist >= 0) & (dist <= max_dist)
    q1 = lax.broadcasted_iota(jnp.int32, (BLK, BLK), 0)
    k1 = lax.broadcasted_iota(jnp.int32, (BLK, BLK), 1)
    first = (q1 - k1 >= 0) & (q1 - k1 <= max_dist)
    return band, first


def _attn_fwd(qkv, sinks, *, d, hq, hkv, max_dist, name):
    B, Ls, wall = qkv.shape
    qw, kw = hq * HEAD, hkv * HEAD
    wtot = qw + 2 * kw
    assert wall == d * wtot and Ls % BLK == 0
    G = hq // hkv
    nblk = Ls // BLK
    use_sink = sinks is not None

    def body(*refs):
        if use_sink:
            sink_ref, qkv_ref, o_ref = refs
        else:
            qkv_ref, o_ref = refs
        band, first = _band_masks(max_dist)
        for h in range(hq):
            hk = h // G
            qc = slice(h * HEAD, (h + 1) * HEAD)
            kc = slice(qw + hk * HEAD, qw + (hk + 1) * HEAD)
            vc = slice(qw + kw + hk * HEAD, qw + kw + (hk + 1) * HEAD)
            lc = slice(qw + h * HEAD, qw + (h + 1) * HEAD)
            sink = sink_ref[h] if use_sink else None

            def block(rows_q, rows_k, mask, qc=qc, kc=kc, vc=vc, lc=lc, sink=sink):
                q = qkv_ref[rows_q, qc]
                kk = qkv_ref[rows_k, kc]
                vv = qkv_ref[rows_k, vc]
                s = jnp.where(mask, _dot_nt(q, kk), NEG)
                m = jnp.max(s, axis=1, keepdims=True)
                if use_sink:
                    m = jnp.maximum(m, sink)
                p = jnp.exp(s - m)
                l = jnp.sum(p, axis=1, keepdims=True)
                if use_sink:
                    l = l + jnp.exp(sink - m)
                o_ref[rows_q, qc] = _dot((p / l).astype(MXU), vv)
                o_ref[rows_q, lc] = jnp.broadcast_to(m + jnp.log(l), (BLK, HEAD))

            block(pl.ds(0, BLK), pl.ds(0, BLK), first)
            if nblk > 1:
                def loop(i, carry, block=block):
                    r0 = pl.multiple_of(i * BLK, BLK)
                    rk = pl.multiple_of(i * BLK - BLK, BLK)
                    block(pl.ds(r0, BLK), pl.ds(rk, 2 * BLK), band)
                    return carry
                lax.fori_loop(1, nblk, loop, 0)

    in_specs = [pl.BlockSpec((None, Ls, wtot), lambda b, r: (b, 0, r))]
    args = [qkv]
    if use_sink:
        in_specs = [pl.BlockSpec(memory_space=pltpu.SMEM)] + in_specs
        args = [sinks] + args
    return pl.pallas_call(
        body, name=name, grid=(B, d), in_specs=in_specs,
        out_specs=pl.BlockSpec((None, Ls, 2 * qw), lambda b, r: (b, 0, r)),
        out_shape=jax.ShapeDtypeStruct((B, Ls, d * 2 * qw), F32),
        compiler_params=_cparams(("arbitrary", "arbitrary")),
    )(*args)


def _attn_bwd(qkv, dob, sinks, *, d, hq, hkv, max_dist, name):
    B, Ls, wall = qkv.shape
    qw, kw = hq * HEAD, hkv * HEAD
    wtot = qw + 2 * kw
    assert wall == d * wtot and dob.shape == (B, Ls, d * 3 * qw)
    G = hq // hkv
    nblk = Ls // BLK
    use_sink = sinks is not None

    def body(*refs):
        if use_sink:
            sink_ref, qkv_ref, dob_ref, dq_ref, dsink_ref = refs
        else:
            qkv_ref, dob_ref, dq_ref = refs
        band, first = _band_masks(max_dist)
        dq_ref[:, qw:wtot] = jnp.zeros((Ls, 2 * kw), F32)
        if use_sink:
            @pl.when((pl.program_id(0) == 0) & (pl.program_id(1) == 0))
            def _():
                dsink_ref[...] = jnp.zeros_like(dsink_ref)
        for h in range(hq):
            hk = h // G
            qc = slice(h * HEAD, (h + 1) * HEAD)
            kc = slice(qw + hk * HEAD, qw + (hk + 1) * HEAD)
            vc = slice(qw + kw + hk * HEAD, qw + kw + (hk + 1) * HEAD)
            lc = slice(qw + h * HEAD, qw + (h + 1) * HEAD)
            dc = slice(2 * qw + h * HEAD, 2 * qw + (h + 1) * HEAD)
            sink = sink_ref[h] if use_sink else None

            def block(rows_q, rows_k, mask, acc, qc=qc, kc=kc, vc=vc, lc=lc, dc=dc, sink=sink):
                q = qkv_ref[rows_q, qc]
                kk = qkv_ref[rows_k, kc]
                vv = qkv_ref[rows_k, vc]
                do = dob_ref[rows_q, qc].astype(MXU)
                lse = dob_ref[rows_q, lc][:, 0:1]
                delta = dob_ref[rows_q, dc][:, 0:1]
                p = jnp.where(mask, jnp.exp(_dot_nt(q, kk) - lse), 0.0)
                ds = p * (_dot_nt(do, vv) - delta)
                dq_ref[rows_q, qc] = _dot(ds.astype(MXU), kk)
                dq_ref[rows_k, kc] += _dot_tn(ds, q)
                dq_ref[rows_k, vc] += _dot_tn(p, do)
                if use_sink:
                    acc = acc - jnp.sum(jnp.exp(sink - lse) * delta, axis=0, keepdims=True)
                return acc

            acc = block(pl.ds(0, BLK), pl.ds(0, BLK), first, jnp.zeros((1, 1), F32))
            if nblk > 1:
                def loop(i, acc, block=block):
                    r0 = pl.multiple_of(i * BLK, BLK)
                    rk = pl.multiple_of(i * BLK - BLK, BLK)
                    return block(pl.ds(r0, BLK), pl.ds(rk, 2 * BLK), band, acc)
                acc = lax.fori_loop(1, nblk, loop, acc)
            if use_sink:
                dsink_ref[h:h + 1, :] += jnp.broadcast_to(acc, (1, 128))

    in_specs = [pl.BlockSpec((None, Ls, wtot), lambda b, r: (b, 0, r)),
                pl.BlockSpec((None, Ls, 3 * qw), lambda b, r: (b, 0, r))]
    out_specs = [pl.BlockSpec((None, Ls, wtot), lambda b, r: (b, 0, r))]
    out_shape = [jax.ShapeDtypeStruct((B, Ls, d * wtot), F32)]
    args = [qkv, dob]
    if use_sink:
        in_specs = [pl.BlockSpec(memory_space=pltpu.SMEM)] + in_specs
        args = [sinks] + args
        out_specs.append(pl.BlockSpec((8, 128), lambda b, r: (0, 0)))
        out_shape.append(jax.ShapeDtypeStruct((8, 128), F32))
    return pl.pallas_call(
        body, name=name, grid=(B, d), in_specs=in_specs, out_specs=out_specs, out_shape=out_shape,
        compiler_params=_cparams(("arbitrary", "arbitrary")),
    )(*args)


MEM_QROWS = 512


def _mem_attn_fwd(qc, mem, w_mem):
    B = qc.shape[0]

    def body(q_ref, mem_ref, w_ref, o_ref, mkv_ref):
        mkv = _dot(mem_ref[...].astype(MXU), w_ref[...]).astype(MXU)
        mkv_ref[...] = mkv
        for h in range(MEM_H):
            mk = mkv[:, h * HEAD:(h + 1) * HEAD]
            mv = mkv[:, W_C + h * HEAD:W_C + (h + 1) * HEAD]
            qcs = slice(h * HEAD, (h + 1) * HEAD)
            lcs = slice(W_C + h * HEAD, W_C + (h + 1) * HEAD)

            def loop(i, carry, mk=mk, mv=mv, qcs=qcs, lcs=lcs):
                rows = pl.ds(pl.multiple_of(i * MEM_QROWS, MEM_QROWS), MEM_QROWS)
                s = _dot_nt(q_ref[rows, qcs], mk)
                m = jnp.max(s, axis=1, keepdims=True)
                p = jnp.exp(s - m)
                l = jnp.sum(p, axis=1, keepdims=True)
                o_ref[rows, qcs] = _dot((p / l).astype(MXU), mv)
                o_ref[rows, lcs] = jnp.broadcast_to(m + jnp.log(l), (MEM_QROWS, HEAD))
                return carry
            lax.fori_loop(0, SEQ // MEM_QROWS, loop, 0)

    return pl.pallas_call(
        body, name="mem_attn_fwd", grid=(B,),
        in_specs=[pl.BlockSpec((None, SEQ, W_C), lambda b: (b, 0, 0)),
                  pl.BlockSpec((None, MEM_LEN, D_MODEL), lambda b: (b, 0, 0)),
                  pl.BlockSpec((D_MODEL, 2 * W_C), lambda b: (0, 0))],
        out_specs=[pl.BlockSpec((None, SEQ, 2 * W_C), lambda b: (b, 0, 0)),
                   pl.BlockSpec((None, MEM_LEN, 2 * W_C), lambda b: (b, 0, 0))],
        out_shape=[jax.ShapeDtypeStruct((B, SEQ, 2 * W_C), F32),
                   jax.ShapeDtypeStruct((B, MEM_LEN, 2 * W_C), MXU)],
        compiler_params=_cparams(("arbitrary",)),
    )(qc, mem, w_mem)


def _mem_attn_bwd(qc, mkv, dob, mem):
    B = qc.shape[0]

    def body(q_ref, mkv_ref, dob_ref, mem_ref, dq_ref, dw_ref, dmkv_ref):
        @pl.when(pl.program_id(0) == 0)
        def _():
            dw_ref[...] = jnp.zeros_like(dw_ref)
        for h in range(MEM_H):
            kcs = slice(h * HEAD, (h + 1) * HEAD)
            vcs = slice(W_C + h * HEAD, W_C + (h + 1) * HEAD)
            dcs = slice(2 * W_C + h * HEAD, 2 * W_C + (h + 1) * HEAD)
            mk = mkv_ref[:, kcs]
            mv = mkv_ref[:, vcs]

            def loop(i, carry, mk=mk, mv=mv, kcs=kcs, vcs=vcs, dcs=dcs):
                dmk, dmv = carry
                rows = pl.ds(pl.multiple_of(i * MEM_QROWS, MEM_QROWS), MEM_QROWS)
                q = q_ref[rows, kcs]
                do = dob_ref[rows, kcs].astype(MXU)
                lse = dob_ref[rows, vcs][:, 0:1]
                delta = dob_ref[rows, dcs][:, 0:1]
                p = jnp.exp(_dot_nt(q, mk) - lse)
                ds = p * (_dot_nt(do, mv) - delta)
                dq_ref[rows, kcs] = _dot(ds.astype(MXU), mk)
                return dmk + _dot_tn(ds, q), dmv + _dot_tn(p, do)
            z = jnp.zeros((MEM_LEN, HEAD), F32)
            dmk, dmv = lax.fori_loop(0, SEQ // MEM_QROWS, loop, (z, z))
            dmkv_ref[:, kcs] = dmk
            dmkv_ref[:, vcs] = dmv
        dw_ref[...] += _dot_tn(mem_ref[...], dmkv_ref[...].astype(MXU))

    return pl.pallas_call(
        body, name="mem_attn_bwd", grid=(B,),
        in_specs=[pl.BlockSpec((None, SEQ, W_C), lambda b: (b, 0, 0)),
                  pl.BlockSpec((None, MEM_LEN, 2 * W_C), lambda b: (b, 0, 0)),
                  pl.BlockSpec((None, SEQ, 3 * W_C), lambda b: (b, 0, 0)),
                  pl.BlockSpec((None, MEM_LEN, D_MODEL), lambda b: (b, 0, 0))],
        out_specs=[pl.BlockSpec((None, SEQ, W_C), lambda b: (b, 0, 0)),
                   pl.BlockSpec((D_MODEL, 2 * W_C), lambda b: (0, 0))],
        out_shape=[jax.ShapeDtypeStruct((B, SEQ, W_C), F32),
                   jax.ShapeDtypeStruct((D_MODEL, 2 * W_C), F32)],
        scratch_shapes=[pltpu.VMEM((MEM_LEN, 2 * W_C), F32)],
        compiler_params=_cparams(("arbitrary",)),
    )(qc, mkv, dob, mem)


def _headsum(t, e):
    if MXU == F32:
        return _dot(t, e)
    hi = t.astype(MXU)
    lo = (t - hi.astype(F32)).astype(MXU)
    return _dot(hi, e) + _dot(lo, e)


def _post(olse_a, olse_b1, olse_b4, olse_b16, olse_c, z, x2, tgt, g, gain, bias, w_out, hsum):
    T = x2.shape[0]
    tm = 256

    def body(oa_ref, b1_ref, b4_ref, b16_ref, oc_ref, z_ref, x_ref, t_ref, g_ref, gain_ref, bias_ref, w_ref,
             e_ref, gx_ref, doba_ref, dobb_ref, dobc_ref, dz_ref, dw_ref, small_ref, loss_ref):
        @pl.when(pl.program_id(0) == 0)
        def _():
            dw_ref[...] = jnp.zeros_like(dw_ref)
            small_ref[...] = jnp.zeros_like(small_ref)
            loss_ref[...] = jnp.zeros_like(loss_ref)

        oa = oa_ref[:, 0:W_A]
        lse_a = oa_ref[:, W_A:2 * W_A]
        o1, l1 = b1_ref[:, 0:W_B], b1_ref[:, W_B:2 * W_B]
        o4, l4 = b4_ref[:, 0:W_B], b4_ref[:, W_B:2 * W_B]
        o16, l16 = b16_ref[:, 0:W_B], b16_ref[:, W_B:2 * W_B]
        mx = jnp.maximum(jnp.maximum(l1, l4), l16)
        e1, e4, e16 = jnp.exp(l1 - mx), jnp.exp(l4 - mx), jnp.exp(l16 - mx)
        den = e1 + e4 + e16
        ob = (e1 * o1 + e4 * o4 + e16 * o16) / den
        lse_b = mx + jnp.log(den)
        oc = oc_ref[:, 0:W_C]
        lse_c = oc_ref[:, W_C:2 * W_C]

        def rms(o):
            rr = lax.rsqrt(jnp.mean(o * o, axis=1, keepdims=True) + RMS_EPS)
            return o * rr, rr

        na, ra = rms(oa)
        nb, rb = rms(ob)
        nc, rc = rms(oc)
        n = jnp.concatenate([na, nb, nc], axis=1)
        zz = z_ref[...]
        sig = 1.0 / (1.0 + jnp.exp(-zz))
        sz = zz * sig
        gg = g_ref[...]
        u = n * gg * sz
        ub = u.astype(MXU)
        w = w_ref[...]
        r = ALPHA * x_ref[...] + _dot(ub, w)
        mu = jnp.mean(r, axis=1, keepdims=True)
        rc0 = r - mu
        rstd = lax.rsqrt(jnp.mean(rc0 * rc0, axis=1, keepdims=True) + LN_EPS)
        xhat = rc0 * rstd
        gain_v = gain_ref[...]
        err = xhat * gain_v + bias_ref[...] - t_ref[...]
        loss_ref[...] += 0.5 * jnp.sum(err * err) * (1.0 / D_MODEL)

        dout = err * (1.0 / D_MODEL)
        small_ref[0:1, :] += jnp.sum(dout * xhat, axis=0, keepdims=True)
        small_ref[1:2, :] += jnp.sum(dout, axis=0, keepdims=True)
        dxh = dout * gain_v
        dr = rstd * (dxh - jnp.mean(dxh, axis=1, keepdims=True)
                     - xhat * jnp.mean(dxh * xhat, axis=1, keepdims=True))
        gx_ref[...] = ALPHA * dr
        drb = dr.astype(MXU)
        du = _dot_nt(drb, w)
        dw_ref[...] += _dot_tn(u, drb)
        small_ref[2:3, :] += jnp.sum(du * n * sz, axis=0, keepdims=True)
        dz = du * n * gg * (sig * (1.0 + zz * (1.0 - sig)))
        dz_ref[...] = dz.astype(MXU)
        small_ref[3:4, :] += jnp.sum(dz, axis=0, keepdims=True)
        dn = du * gg * sz

        def branch(lo, hi, o, nbr, rr, lse, out_ref):
            wd = hi - lo
            dnb = dn[:, lo:hi]
            dob = rr * (dnb - nbr * jnp.mean(dnb * nbr, axis=1, keepdims=True))
            out_ref[:, 0:wd] = dob
            out_ref[:, wd:2 * wd] = lse
            out_ref[:, 2 * wd:3 * wd] = _headsum(dob * o, e_ref[0:wd, 0:wd])

        branch(0, W_A, oa, na, ra, lse_a, doba_ref)
        branch(W_A, W_A + W_B, ob, nb, rb, lse_b, dobb_ref)
        branch(W_A + W_B, D_MIX, oc, nc, rc, lse_c, dobc_ref)

    row = lambda w: pl.BlockSpec((tm, w), lambda i: (i, 0))
    full = lambda a, b: pl.BlockSpec((a, b), lambda i: (0, 0))
    return pl.pallas_call(
        body, name="post_fwd_bwd", grid=(T // tm,),
        in_specs=[row(2 * W_A), row(2 * W_B), row(2 * W_B), row(2 * W_B), row(2 * W_C),
                  row(D_MIX), row(D_MODEL), row(D_MODEL),
                  full(1, D_MIX), full(1, D_MODEL), full(1, D_MODEL), full(D_MIX, D_MODEL), full(W_A, W_A)],
        out_specs=[row(D_MODEL), row(3 * W_A), row(3 * W_B), row(3 * W_C), row(D_MIX),
                   full(D_MIX, D_MODEL), full(8, D_MODEL), full(8, 128)],
        out_shape=[jax.ShapeDtypeStruct((T, D_MODEL), F32),
                   jax.ShapeDtypeStruct((T, 3 * W_A), F32),
                   jax.ShapeDtypeStruct((T, 3 * W_B), F32),
                   jax.ShapeDtypeStruct((T, 3 * W_C), F32),
                   jax.ShapeDtypeStruct((T, D_MIX), MXU),
                   jax.ShapeDtypeStruct((D_MIX, D_MODEL), F32),
                   jax.ShapeDtypeStruct((8, D_MODEL), F32),
                   jax.ShapeDtypeStruct((8, 128), F32)],
        compiler_params=_cparams(("arbitrary",)),
    )(olse_a, olse_b1, olse_b4, olse_b16, olse_c, z, x2, tgt, g, gain, bias, w_out, hsum)


def _in_proj_bwd_dx(dqkva, dqkvb1, dqkvb4, dqkvb16, dqc, dz, gx1, w_in, tab):
    T = gx1.shape[0]
    tm = 256
    nt = SEQ // tm
    HQ = D_IN - D_MIX

    def body(da_ref, d1_ref, d4_ref, d16_ref, dqc_ref, dz_ref, gx_ref, w_ref, tab_ref, dx_ref, dh_ref, db_ref):
        @pl.when(pl.program_id(0) == 0)
        def _():
            db_ref[...] = jnp.zeros_like(db_ref)
        tab = tab_ref[...]
        dbs = d1_ref[...] + d4_ref[...] + d16_ref[...]
        parts = [_rope(da_ref[:, 0:512], tab, -1) * Q_SCALE,
                 _rope(da_ref[:, 512:640], tab, -1),
                 da_ref[:, 640:768],
                 _rope(dbs[:, 0:256], tab, -1) * Q_SCALE,
                 _rope(dbs[:, 256:512], tab, -1),
                 dbs[:, 512:768],
                 dqc_ref[...] * Q_SCALE]
        dhq = jnp.concatenate(parts, axis=1)
        db_ref[0:1, :] += jnp.sum(dhq, axis=0, keepdims=True)
        dhq_b = dhq.astype(MXU)
        dzb = dz_ref[...]
        dh_ref[:, 0:HQ] = dhq_b
        dh_ref[:, HQ:D_IN] = dzb
        dx = _dot_nt(dhq_b, w_ref[:, 0:HQ]) + _dot_nt(dzb, w_ref[:, HQ:D_IN])
        dx_ref[...] = dx + gx_ref[...]

    row = lambda w: pl.BlockSpec((tm, w), lambda i: (i, 0))
    return pl.pallas_call(
        body, name="in_proj_bwd_dx", grid=(T // tm,),
        in_specs=[row(768), row(768), row(768), row(768), row(256), row(D_MIX), row(D_MODEL),
                  pl.BlockSpec((D_MODEL, D_IN), lambda i: (0, 0)),
                  pl.BlockSpec((tm, 384), lambda i: (i % nt, 0))],
        out_specs=[row(D_MODEL), row(D_IN), pl.BlockSpec((8, HQ), lambda i: (0, 0))],
        out_shape=[jax.ShapeDtypeStruct((T, D_MODEL), F32), jax.ShapeDtypeStruct((T, D_IN), MXU),
                   jax.ShapeDtypeStruct((8, HQ), F32)],
        compiler_params=_cparams(("arbitrary",)),
    )(dqkva, dqkvb1, dqkvb4, dqkvb16, dqc, dz, gx1, w_in, tab)


def _in_proj_bwd_dw(x2, dh):
    T = x2.shape[0]
    tk = 512
    tn = 1408

    def body(x_ref, dh_ref, dw_ref):
        @pl.when(pl.program_id(1) == 0)
        def _():
            dw_ref[...] = jnp.zeros_like(dw_ref)
        dw_ref[...] += _dot_tn(x_ref[...], dh_ref[...])

    return pl.pallas_call(
        body, name="in_proj_bwd_dw", grid=(D_IN // tn, T // tk),
        in_specs=[pl.BlockSpec((tk, D_MODEL), lambda j, k: (k, 0)),
                  pl.BlockSpec((tk, tn), lambda j, k: (k, j))],
        out_specs=pl.BlockSpec((D_MODEL, tn), lambda j, k: (0, j)),
        out_shape=jax.ShapeDtypeStruct((D_MODEL, D_IN), F32),
        compiler_params=_cparams(("arbitrary", "arbitrary")),
    )(x2, dh)


def _local_step(x, mem, w_in, w_mem, w_out, b_in, sinks, g, gain, bias, tgt):
    B = x.shape[0]
    T = B * SEQ
    x2 = x.reshape(T, D_MODEL)
    t2 = tgt.reshape(T, D_MODEL)
    tab = _rope_table()
    lane = jnp.arange(W_A)
    hsum = (lane[:, None] // HEAD == lane[None, :] // HEAD).astype(MXU)

    qkva, qkvb, qc, z = _in_proj(x2, w_in, b_in, tab)

    def view(a, d):
        return a.reshape(B, SEQ // d, d * a.shape[-1])

    def unview(a, w):
        return a.reshape(T, w)

    qkva3 = view(qkva, 1)
    qkvb_v = {d: view(qkvb, d) for d in (1, 4, 16)}
    qc3 = view(qc, 1)

    olse_a = _attn_fwd(qkva3, sinks, d=1, hq=SWA_Q, hkv=SWA_KV, max_dist=BLK - 1, name="swa_fwd")
    olse_b = {d: _attn_fwd(qkvb_v[d], None, d=d, hq=DIL_H, hkv=DIL_H, max_dist=BLK, name=f"dil{d}_fwd")
              for d in (1, 4, 16)}
    olse_c, mkv = _mem_attn_fwd(qc3, mem, w_mem)

    gx1, doba, dobb, dobc, dz, dw_out, small, loss = _post(
        unview(olse_a, 2 * W_A), unview(olse_b[1], 2 * W_B), unview(olse_b[4], 2 * W_B),
        unview(olse_b[16], 2 * W_B), unview(olse_c, 2 * W_C), z, x2, t2, g, gain, bias, w_out, hsum)

    dqkva, dsink = _attn_bwd(qkva3, view(doba, 1), sinks, d=1, hq=SWA_Q, hkv=SWA_KV, max_dist=BLK - 1,
                             name="swa_bwd")
    dqkvb = {d: _attn_bwd(qkvb_v[d], view(dobb, d), None, d=d, hq=DIL_H, hkv=DIL_H, max_dist=BLK,
                          name=f"dil{d}_bwd")[0] for d in (1, 4, 16)}
    dqc, dw_mem = _mem_attn_bwd(qc3, mkv, view(dobc, 1), mem)

    grad_x, dh, dbq = _in_proj_bwd_dx(unview(dqkva, 768), unview(dqkvb[1], 768), unview(dqkvb[4], 768),
                                      unview(dqkvb[16], 768), unview(dqc, W_C), dz, gx1, w_in, tab)
    dw_in = _in_proj_bwd_dw(x2, dh)

    db_in = jnp.concatenate([dbq[0], small[3]])
    grads_small = dict(b_in=db_in, sinks=dsink[:, 0], g=small[2], gain=small[0], bias=small[1])
    return loss[0, 0], grad_x.reshape(B, SEQ, D_MODEL), dw_in, dw_mem, dw_out, grads_small


def _my_pos():
    return lax.axis_index("x"), lax.axis_index("y"), lax.axis_index("c")


def _gather_weights(w_in_s, w_mem_s, w_out_s):
    shards = (w_in_s, w_mem_s, w_out_s)
    n_arr = len(shards)

    def body(*refs):
        ins, outs = refs[0:n_arr], refs[n_arr:2 * n_arr]
        send_sems, recv_sems, local_sems = refs[2 * n_arr:]
        x, y, c = _my_pos()
        me, sibling = (x, y, c), (x, y, 1 - c)
        chips = [(1 - x, y), (x, 1 - y), (1 - x, 1 - y)]

        def slot(a, pos):
            return outs[a].at[4 * pos[0] + 2 * pos[1] + pos[2]]

        def copy(a, k, block, to, src=None):
            return pltpu.make_async_remote_copy(
                src_ref=slot(a, block) if src is None else src, dst_ref=slot(a, block),
                send_sem=send_sems.at[a, k], recv_sem=recv_sems.at[a, k],
                device_id=to, device_id_type=MESH)

        mine = [pltpu.make_async_copy(ins[a], slot(a, me), local_sems.at[a]) for a in range(n_arr)]
        for cp in mine:
            cp.start()
        first = []
        for a in range(n_arr):
            first.append(copy(a, 0, me, sibling, src=ins[a]))
            first += [copy(a, 1 + j, me, (*chip, c), src=ins[a]) for j, chip in enumerate(chips)]
        for cp in first:
            cp.start()
        passed = []
        for j, chip in enumerate(chips):
            for a in range(n_arr):
                copy(a, 1 + j, (*chip, c), me).wait_recv()
                fwd = copy(a, 4 + j, (*chip, c), sibling)
                fwd.start()
                passed.append(fwd)
        for a in range(n_arr):
            copy(a, 0, sibling, me).wait_recv()
            for j, chip in enumerate(chips):
                copy(a, 4 + j, (*chip, 1 - c), me).wait_recv()
        for cp in first + passed:
            cp.wait_send()
        for cp in mine:
            cp.wait()

    any_spec = pl.BlockSpec(memory_space=pl.ANY)
    return pl.pallas_call(
        body, name="gather_weights",
        in_specs=[any_spec] * n_arr, out_specs=[any_spec] * n_arr,
        out_shape=[jax.ShapeDtypeStruct((N_DEV,) + s.shape, s.dtype) for s in shards],
        scratch_shapes=[pltpu.SemaphoreType.DMA((n_arr, 7)), pltpu.SemaphoreType.DMA((n_arr, 7)),
                        pltpu.SemaphoreType.DMA((n_arr,))],
    )(*shards)


def _adamw(w, g, m, v):
    m = ADAM_B1 * m + (1.0 - ADAM_B1) * g
    v = ADAM_B2 * v + (1.0 - ADAM_B2) * (g * g)
    m_hat = m / (1.0 - ADAM_B1 ** ADAM_STEP)
    v_hat = v / (1.0 - ADAM_B2 ** ADAM_STEP)
    delta = -ADAM_LR * (m_hat / (jnp.sqrt(v_hat) + ADAM_EPS) + ADAM_WD * w)
    return delta, m, v


def _reduce_adam(sends, owns, small_g, params):
    n_big = 3
    shapes = [o.shape for o in owns] + [small_g.shape]
    chunk_rows = [128, 128, 128, SMALL_ROWS]

    def body(*refs):
        it = iter(refs)
        send_refs = [next(it) for _ in range(n_big)]
        own_refs = [next(it) for _ in range(n_big)]
        sg_ref = next(it)
        wmv_refs = [[next(it) for _ in range(3)] for _ in range(n_big + 1)]
        out_refs = [[next(it) for _ in range(4)] for _ in range(n_big + 1)]
        land_refs = [next(it) for _ in range(n_big + 1)]
        send_sems, recv_sems = next(it), next(it)

        x, y, c = _my_pos()
        me = 4 * x + 2 * y + c
        copies = []
        for k in range(1, N_DEV):
            px = 1 - x if k & 4 else x
            py = 1 - y if k & 2 else y
            pc = 1 - c if k & 1 else c
            peer = 4 * px + 2 * py + pc
            for a in range(n_big + 1):
                src = send_refs[a].at[peer] if a < n_big else sg_ref
                cp = pltpu.make_async_remote_copy(
                    src_ref=src, dst_ref=land_refs[a].at[me],
                    send_sem=send_sems.at[a, k - 1], recv_sem=recv_sems.at[a, k - 1],
                    device_id=(px, py, pc), device_id_type=MESH)
                cp.start()
                copies.append(cp)
        for a in range(n_big):
            land_refs[a][me] = jnp.zeros(shapes[a], land_refs[a].dtype)
        land_refs[n_big][me] = sg_ref[...]
        for cp in copies:
            cp.wait_recv()
        for cp in copies:
            cp.wait_send()

        for a in range(n_big + 1):
            rows, cr = shapes[a][0], chunk_rows[a]
            w_ref, m_ref, v_ref = wmv_refs[a]
            g_ref, d_ref, nm_ref, nv_ref = out_refs[a]
            land = land_refs[a]

            def chunk(i, carry, a=a, cr=cr, w_ref=w_ref, m_ref=m_ref, v_ref=v_ref, g_ref=g_ref,
                      d_ref=d_ref, nm_ref=nm_ref, nv_ref=nv_ref, land=land):
                rs = pl.ds(pl.multiple_of(i * cr, cr), cr)
                if a < n_big:
                    gsum = own_refs[a][rs, :]
                    for j in range(N_DEV):
                        gsum = gsum + land[j, rs, :].astype(F32)
                else:
                    gsum = land[0, rs, :]
                    for j in range(1, N_DEV):
                        gsum = gsum + land[j, rs, :]
                delta, nm, nv = _adamw(w_ref[rs, :], gsum, m_ref[rs, :], v_ref[rs, :])
                g_ref[rs, :] = gsum
                d_ref[rs, :] = delta
                nm_ref[rs, :] = nm
                nv_ref[rs, :] = nv
                return carry
            lax.fori_loop(0, rows // cr, chunk, 0)

    any_spec = pl.BlockSpec(memory_space=pl.ANY)
    vmem = pl.BlockSpec(memory_space=pltpu.VMEM)
    flat_params = [p for grp in params for p in grp]
    out_shape = [jax.ShapeDtypeStruct(s, F32) for s in shapes for _ in range(4)]
    scratch = [pltpu.VMEM((N_DEV,) + shapes[a], sends[a].dtype) for a in range(n_big)]
    scratch.append(pltpu.VMEM((N_DEV,) + shapes[n_big], F32))
    scratch += [pltpu.SemaphoreType.DMA((n_big + 1, N_DEV - 1)), pltpu.SemaphoreType.DMA((n_big + 1, N_DEV - 1))]
    outs = pl.pallas_call(
        body, name="reduce_adam",
        in_specs=[any_spec] * n_big + [vmem] * (n_big + 1 + len(flat_params)),
        out_specs=[vmem] * len(out_shape), out_shape=out_shape, scratch_shapes=scratch,
        compiler_params=pltpu.CompilerParams(vmem_limit_bytes=VMEM_LIMIT),
    )(*sends, *owns, small_g, *flat_params)
    return [outs[4 * a:4 * a + 4] for a in range(n_big + 1)]


_SMALL_SIZES = (("b_in", D_IN), ("g", D_MIX), ("gain", D_MODEL), ("bias", D_MODEL), ("sinks", SWA_Q))


def _pack_small(d):
    flat = jnp.concatenate([d[k].reshape(-1).astype(F32) for k, _ in _SMALL_SIZES])
    flat = jnp.pad(flat, (0, SMALL_ROWS * 128 - flat.shape[0]))
    return flat.reshape(SMALL_ROWS, 128)


def _unpack_small(p):
    flat = p.reshape(-1)
    out, off = {}, 0
    for k, n in _SMALL_SIZES:
        out[k] = flat[off:off + n].reshape(1, n)
        off += n
    return out


def kernel(x, mem, w_in, b_in, w_mem, attn_sinks, g_branch, w_out, ln_gain, ln_bias, loss_target, m_w_in, m_b_in, m_w_mem, m_attn_sinks, m_g_branch, m_w_out, m_ln_gain, m_ln_bias, v_w_in, v_b_in, v_w_mem, v_attn_sinks, v_g_branch, v_w_out, v_ln_gain, v_ln_bias):
    xi, yi, ci = _my_pos()
    me = 4 * xi + 2 * yi + ci

    w_in_all, w_mem_all, w_out_all = _gather_weights(
        w_in[0].astype(MXU), w_mem[0].astype(MXU), w_out[0].astype(MXU))
    w_in_f = w_in_all.transpose(1, 0, 2).reshape(D_MODEL, D_IN)
    w_mem_f = w_mem_all.reshape(D_MODEL, 2 * W_C)
    w_out_f = w_out_all.reshape(D_MIX, D_MODEL)

    loss, grad_x, dw_in, dw_mem, dw_out, gsmall = _local_step(
        x, mem, w_in_f, w_mem_f, w_out_f, b_in, attn_sinks[0], g_branch, ln_gain, ln_bias, loss_target)
    loss = lax.psum(loss, ("x", "y", "c"))

    dw_in_blocks = dw_in.reshape(D_MODEL, N_DEV, COLS_PER_DEV).transpose(1, 0, 2)
    dw_mem_blocks = dw_mem.reshape(N_DEV, ROWS_PER_DEV, 2 * W_C)
    dw_out_blocks = dw_out.reshape(N_DEV, ROWS_PER_DEV, D_MODEL)
    blocks = (dw_in_blocks, dw_mem_blocks, dw_out_blocks)
    sends = [b.astype(MXU) for b in blocks]
    owns = [lax.dynamic_index_in_dim(b, me, axis=0, keepdims=False) for b in blocks]
    small_w = _pack_small(dict(b_in=b_in, g=g_branch, gain=ln_gain, bias=ln_bias, sinks=attn_sinks))
    small_m = _pack_small(dict(b_in=m_b_in, g=m_g_branch, gain=m_ln_gain, bias=m_ln_bias, sinks=m_attn_sinks))
    small_v = _pack_small(dict(b_in=v_b_in, g=v_g_branch, gain=v_ln_gain, bias=v_ln_bias, sinks=v_attn_sinks))
    params = [(w_in[0], m_w_in[0], v_w_in[0]), (w_mem[0], m_w_mem[0], v_w_mem[0]),
              (w_out[0], m_w_out[0], v_w_out[0]), (small_w, small_m, small_v)]
    res = _reduce_adam(sends, owns, _pack_small(gsmall), params)

    big = [[r[None] for r in res[a]] for a in range(3)]
    sm = [_unpack_small(r) for r in res[3]]

    def group(i):
        return (big[0][i], sm[i]["b_in"], big[1][i], sm[i]["sinks"], sm[i]["g"], big[2][i],
                sm[i]["gain"], sm[i]["bias"])

    return (loss, grad_x, *group(0), *group(1), *group(2), *group(3))
```

```python
import functools
import math

import jax
import jax.numpy as jnp
from jax import lax
from jax.experimental import pallas as pl
from jax.experimental.pallas import tpu as pltpu

F32 = jnp.float32
MXU = jnp.bfloat16

D_MODEL = 1024
SEQ = 2048
HEAD = 64
BLK = 128
SWA_Q, SWA_KV = 8, 2
DIL_H = 4
MEM_H = 4
MEM_LEN = 256
W_A, W_KVA, W_B, W_C = 512, 128, 256, 256
D_MIX = 1024
D_IN = 2816
N_DEV = 8
COLS_PER_DEV = D_IN // N_DEV
ROWS_PER_DEV = D_MODEL // N_DEV
ROPE_THETA = 10000.0
LN_EPS = 1e-5
RMS_EPS = 1e-6
ALPHA = 2.0 ** 0.25
Q_SCALE = HEAD ** -0.5
NEG = -1e30
SMALL_ROWS = 48
VMEM_LIMIT = 56 * 1024 * 1024

ADAM_LR = 0.001
ADAM_B1 = 0.9
ADAM_B2 = 0.999
ADAM_EPS = 1e-08
ADAM_WD = 0.01
ADAM_STEP = 10

MESH = pl.DeviceIdType.MESH


def _cparams(sem=None):
    return pltpu.CompilerParams(dimension_semantics=sem, vmem_limit_bytes=VMEM_LIMIT)


def _dot(a, b):
    return jnp.dot(a, b, preferred_element_type=F32)


def _dot_nt(a, b):
    return lax.dot_general(a, b, (((1,), (1,)), ((), ())), preferred_element_type=F32)


def _dot_t0(a, b):
    return lax.dot_general(a, b, (((0,), (0,)), ((), ())), preferred_element_type=F32)


def _dot_tn(a, b):
    return jnp.dot(a.T.astype(MXU), b, preferred_element_type=F32)


def _rope(t, tab, sign):
    cos, sa, sb = tab[:, 0:128], tab[:, 128:256], tab[:, 256:384]
    outs = []
    for c in range(t.shape[1] // 128):
        tc = t[:, c * 128:(c + 1) * 128]
        r = pltpu.roll(tc, 96, 1) * sa + pltpu.roll(tc, 32, 1) * sb
        outs.append(tc * cos + r if sign > 0 else tc * cos - r)
    return outs[0] if len(outs) == 1 else jnp.concatenate(outs, axis=1)


def _rope_table():
    pos = jnp.arange(SEQ, dtype=F32)
    inv = ROPE_THETA ** (-jnp.arange(0, HEAD, 2, dtype=F32) / HEAD)
    ang = pos[:, None] * inv[None, :]
    ang = jnp.concatenate([ang, ang], axis=-1)
    cos, sin = jnp.cos(ang), jnp.sin(ang)
    lane = jnp.arange(HEAD)[None, :]
    sa = jnp.where(lane < HEAD // 2, -sin, 0.0)
    sb = jnp.where(lane >= HEAD // 2, sin, 0.0)
    two = lambda t: jnp.concatenate([t, t], axis=-1)
    return jnp.concatenate([two(cos), two(sa), two(sb)], axis=-1).astype(F32)


def _in_proj(x2, w_in, b_in, tab):
    T = x2.shape[0]
    tm = 512

    def body(x_ref, w_ref, b_ref, tab_ref, qkva_ref, qkvb_ref, qc_ref, z_ref):
        xb = x_ref[...].astype(MXU)
        tab = tab_ref[...]

        def seg(c0, c1):
            return _dot(xb, w_ref[:, c0:c1]) + b_ref[:, c0:c1]

        qkva_ref[:, 0:512] = (_rope(seg(0, 512), tab, 1) * Q_SCALE).astype(MXU)
        qkva_ref[:, 512:640] = _rope(seg(512, 640), tab, 1).astype(MXU)
        qkva_ref[:, 640:768] = seg(640, 768).astype(MXU)
        qkvb_ref[:, 0:256] = (_rope(seg(768, 1024), tab, 1) * Q_SCALE).astype(MXU)
        qkvb_ref[:, 256:512] = _rope(seg(1024, 1280), tab, 1).astype(MXU)
        qkvb_ref[:, 512:768] = seg(1280, 1536).astype(MXU)
        qc_ref[...] = (seg(1536, 1792) * Q_SCALE).astype(MXU)
        z_ref[...] = seg(1792, 2816)

    nt = SEQ // tm
    return pl.pallas_call(
        body, name="in_proj_fwd",
        grid=(T // tm,),
        in_specs=[pl.BlockSpec((tm, D_MODEL), lambda i: (i, 0)),
                  pl.BlockSpec((D_MODEL, D_IN), lambda i: (0, 0)),
                  pl.BlockSpec((1, D_IN), lambda i: (0, 0)),
                  pl.BlockSpec((tm, 384), lambda i: (i % nt, 0))],
        out_specs=[pl.BlockSpec((tm, 768), lambda i: (i, 0)),
                   pl.BlockSpec((tm, 768), lambda i: (i, 0)),
                   pl.BlockSpec((tm, 256), lambda i: (i, 0)),
                   pl.BlockSpec((tm, D_MIX), lambda i: (i, 0))],
        out_shape=[jax.ShapeDtypeStruct((T, 768), MXU), jax.ShapeDtypeStruct((T, 768), MXU),
                   jax.ShapeDtypeStruct((T, 256), MXU), jax.ShapeDtypeStruct((T, D_MIX), F32)],
        compiler_params=_cparams(("arbitrary",)),
    )(x2, w_in, b_in, tab)


CHAIN = 4


def _band_bias(max_dist):
    kj = lax.broadcasted_iota(jnp.int32, (2 * BLK, BLK), 0)
    qi = lax.broadcasted_iota(jnp.int32, (2 * BLK, BLK), 1)
    dist = qi + BLK - kj
    band = jnp.where((dist >= 0) & (dist <= max_dist), 0.0, NEG).astype(F32)
    k1 = lax.broadcasted_iota(jnp.int32, (BLK, BLK), 0)
    q1 = lax.broadcasted_iota(jnp.int32, (BLK, BLK), 1)
    first = jnp.where((q1 - k1 >= 0) & (q1 - k1 <= max_dist), 0.0, NEG).astype(F32)
    return jnp.concatenate([band] * CHAIN, axis=1), jnp.concatenate([first] * CHAIN, axis=1)


def _lanes(parts):
    return jnp.concatenate(parts, axis=1)


def _col(off, h):
    return slice(off + h * HEAD, off + (h + 1) * HEAD)


def _pair_rows(ref, rows, off, h0):
    t = ref[rows, slice(off + h0 * HEAD, off + (h0 + 2) * HEAD)].T
    return t[0:1, :], t[HEAD:HEAD + 1, :]


def _store_o_lse(o_ref, rows, oT, lse, ocols, lcols):
    n = len(ocols)
    res = jnp.concatenate([oT, jnp.broadcast_to(lse, (HEAD, lse.shape[1]))], axis=0).T
    rq = res.shape[0] // n
    for i in range(n):
        o_ref[rows[i], ocols[i]] = res[i * rq:(i + 1) * rq, 0:HEAD]
        o_ref[rows[i], lcols[i]] = res[i * rq:(i + 1) * rq, HEAD:2 * HEAD]


def _attn_geometry(qkv, d, spg, hq, hkv):
    B, Ls, wall = qkv.shape
    qw, kw = hq * HEAD, hkv * HEAD
    wtot = qw + 2 * kw
    assert wall == d * wtot and Ls % BLK == 0 and d % spg == 0 and hq % CHAIN == 0
    G = hq // hkv
    assert G in (1, CHAIN)
    return B, Ls, qw, kw, wtot, G, Ls // BLK


def _attn_fwd(qkv, sinks, *, d, spg, unroll, hq, hkv, max_dist, name):
    B, Ls, qw, kw, wtot, G, nblk = _attn_geometry(qkv, d, spg, hq, hkv)
    use_sink = sinks is not None
    assert (nblk - 1) % unroll == 0
    trips = (nblk - 1) // unroll

    def body(*refs):
        if use_sink:
            sink_ref, qkv_ref, o_ref = refs
        else:
            qkv_ref, o_ref = refs
        band, first = _band_bias(max_dist)
        for st in range(spg):
            ib, ob = st * wtot, st * 2 * qw
            chains = []
            for ch in range(hq // CHAIN):
                heads = list(range(ch * CHAIN, (ch + 1) * CHAIN))
                sinkrow = _lanes([jnp.full((1, BLK), sink_ref[h], F32) for h in heads]) if use_sink else None

                def block(rows_q, rows_k, bias, heads=heads, ib=ib, ob=ob, sinkrow=sinkrow):
                    qs = [qkv_ref[rows_q, _col(ib, h)] for h in heads]
                    if G > 1:
                        hk = heads[0] // G
                        kk = qkv_ref[rows_k, _col(ib + qw, hk)]
                        vv = qkv_ref[rows_k, _col(ib + qw + kw, hk)]
                        sT = _dot_nt(kk, jnp.concatenate(qs, axis=0))
                    else:
                        vvs = [qkv_ref[rows_k, _col(ib + qw + kw, h)] for h in heads]
                        sT = _lanes([_dot_nt(qkv_ref[rows_k, _col(ib + qw, h)], qs[i])
                                     for i, h in enumerate(heads)])
                    sT = sT + bias
                    m = jnp.max(sT, axis=0, keepdims=True)
                    if use_sink:
                        m = jnp.maximum(m, sinkrow)
                    pT = jnp.exp(sT - m)
                    l = jnp.sum(pT, axis=0, keepdims=True)
                    if use_sink:
                        l = l + jnp.exp(sinkrow - m)
                    pnT = (pT * (1.0 / l)).astype(MXU)
                    if G > 1:
                        oT = _dot_t0(vv, pnT)
                    else:
                        oT = _lanes([_dot_t0(vvs[i], pnT[:, i * BLK:(i + 1) * BLK]) for i in range(CHAIN)])
                    _store_o_lse(o_ref, [rows_q] * CHAIN, oT, m + jnp.log(l),
                                 [_col(ob, h) for h in heads], [_col(ob + qw, h) for h in heads])

                chains.append(block)

            for block in chains:
                block(pl.ds(0, BLK), pl.ds(0, BLK), first)

            def blocks_at(i0, chains=chains):
                for u in range(unroll):
                    r0 = pl.multiple_of((i0 + u) * BLK, BLK)
                    rk = pl.multiple_of((i0 + u) * BLK - BLK, BLK)
                    for block in chains:
                        block(pl.ds(r0, BLK), pl.ds(rk, 2 * BLK), band)

            if trips == 1:
                blocks_at(1)
            elif trips > 1:
                def loop(j, carry, blocks_at=blocks_at):
                    blocks_at(1 + j * unroll)
                    return carry
                lax.fori_loop(0, trips, loop, 0)

    in_specs = [pl.BlockSpec((None, Ls, spg * wtot), lambda b, r: (b, 0, r))]
    args = [qkv]
    if use_sink:
        in_specs = [pl.BlockSpec(memory_space=pltpu.SMEM)] + in_specs
        args = [sinks] + args
    return pl.pallas_call(
        body, name=name, grid=(B, d // spg), in_specs=in_specs,
        out_specs=pl.BlockSpec((None, Ls, spg * 2 * qw), lambda b, r: (b, 0, r)),
        out_shape=jax.ShapeDtypeStruct((B, Ls, d * 2 * qw), F32),
        compiler_params=_cparams(("arbitrary", "arbitrary")),
    )(*args)


def _attn_bwd(qkv, dob, sinks, *, d, spg, unroll, hq, hkv, max_dist, name):
    B, Ls, qw, kw, wtot, G, nblk = _attn_geometry(qkv, d, spg, hq, hkv)
    assert dob.shape == (B, Ls, d * 3 * qw) and (nblk - 1) % unroll == 0
    trips = (nblk - 1) // unroll
    use_sink = sinks is not None

    def body(*refs):
        if use_sink:
            sink_ref, qkv_ref, dob_ref, dq_ref, dsink_ref = refs
        else:
            qkv_ref, dob_ref, dq_ref = refs
        band, first = _band_bias(max_dist)
        if use_sink:
            @pl.when((pl.program_id(0) == 0) & (pl.program_id(1) == 0))
            def _():
                dsink_ref[...] = jnp.zeros_like(dsink_ref)
        for st in range(spg):
            ib, db = st * wtot, st * 3 * qw
            dq_ref[:, ib + qw:ib + wtot] = jnp.zeros((Ls, 2 * kw), F32)
            chains = []
            for ch in range(hq // CHAIN):
                heads = list(range(ch * CHAIN, (ch + 1) * CHAIN))
                sinkrow = _lanes([jnp.full((1, BLK), sink_ref[h], F32) for h in heads]) if use_sink else None

                def block(rows_q, rows_k, bias, acc, heads=heads, ib=ib, db=db, sinkrow=sinkrow):
                    qs = [qkv_ref[rows_q, _col(ib, h)] for h in heads]
                    dos = [dob_ref[rows_q, _col(db, h)].astype(MXU) for h in heads]
                    lse = _lanes([r for j in range(0, CHAIN, 2)
                                  for r in _pair_rows(dob_ref, rows_q, db + qw, heads[j])])
                    delta = _lanes([r for j in range(0, CHAIN, 2)
                                    for r in _pair_rows(dob_ref, rows_q, db + 2 * qw, heads[j])])
                    if G > 1:
                        hk = heads[0] // G
                        kc, vc = _col(ib + qw, hk), _col(ib + qw + kw, hk)
                        kk, vv = qkv_ref[rows_k, kc], qkv_ref[rows_k, vc]
                        qst, dost = jnp.concatenate(qs, axis=0), jnp.concatenate(dos, axis=0)
                        sT, dpT = _dot_nt(kk, qst), _dot_nt(vv, dost)
                    else:
                        kks = [qkv_ref[rows_k, _col(ib + qw, h)] for h in heads]
                        vvs = [qkv_ref[rows_k, _col(ib + qw + kw, h)] for h in heads]
                        sT = _lanes([_dot_nt(kks[i], qs[i]) for i in range(CHAIN)])
                        dpT = _lanes([_dot_nt(vvs[i], dos[i]) for i in range(CHAIN)])
                    pT = jnp.exp(sT + bias - lse)
                    dsT = pT * (dpT - delta)
                    dsb, pb = dsT.astype(MXU), pT.astype(MXU)
                    if G > 1:
                        dq_ref[rows_k, kc] += _dot(dsb, qst)
                        dq_ref[rows_k, vc] += _dot(pb, dost)
                        dq = _dot_t0(kk, dsb).T
                        for i, h in enumerate(heads):
                            dq_ref[rows_q, _col(ib, h)] = dq[i * BLK:(i + 1) * BLK]
                    else:
                        dqT = []
                        for i, h in enumerate(heads):
                            ls = slice(i * BLK, (i + 1) * BLK)
                            dq_ref[rows_k, _col(ib + qw, h)] += _dot(dsb[:, ls], qs[i])
                            dq_ref[rows_k, _col(ib + qw + kw, h)] += _dot(pb[:, ls], dos[i])
                            dqT.append(_dot_t0(kks[i], dsb[:, ls]))
                        dq = _lanes(dqT).T
                        for i, h in enumerate(heads):
                            dq_ref[rows_q, _col(ib, h)] = dq[i * BLK:(i + 1) * BLK]
                    if use_sink:
                        acc = acc - jnp.exp(sinkrow - lse) * delta
                    return acc

                chains.append(block)

            accs = tuple(block(pl.ds(0, BLK), pl.ds(0, BLK), first, jnp.zeros((1, CHAIN * BLK), F32))
                         for block in chains)

            def blocks_at(i0, accs, chains=chains):
                for u in range(unroll):
                    r0 = pl.multiple_of((i0 + u) * BLK, BLK)
                    rk = pl.multiple_of((i0 + u) * BLK - BLK, BLK)
                    accs = tuple(block(pl.ds(r0, BLK), pl.ds(rk, 2 * BLK), band, acc)
                                 for block, acc in zip(chains, accs))
                return accs

            if trips == 1:
                accs = blocks_at(1, accs)
            elif trips > 1:
                def loop(j, accs, blocks_at=blocks_at):
                    return blocks_at(1 + j * unroll, accs)
                accs = lax.fori_loop(0, trips, loop, accs)
            if use_sink:
                for ch, acc in enumerate(accs):
                    for i in range(CHAIN):
                        h = ch * CHAIN + i
                        tot = jnp.sum(acc[:, i * BLK:(i + 1) * BLK], axis=1, keepdims=True)
                        dsink_ref[h:h + 1, :] += jnp.broadcast_to(tot, (1, 128))

    in_specs = [pl.BlockSpec((None, Ls, spg * wtot), lambda b, r: (b, 0, r)),
                pl.BlockSpec((None, Ls, spg * 3 * qw), lambda b, r: (b, 0, r))]
    out_specs = [pl.BlockSpec((None, Ls, spg * wtot), lambda b, r: (b, 0, r))]
    out_shape = [jax.ShapeDtypeStruct((B, Ls, d * wtot), F32)]
    args = [qkv, dob]
    if use_sink:
        in_specs = [pl.BlockSpec(memory_space=pltpu.SMEM)] + in_specs
        args = [sinks] + args
        out_specs.append(pl.BlockSpec((8, 128), lambda b, r: (0, 0)))
        out_shape.append(jax.ShapeDtypeStruct((8, 128), F32))
    return pl.pallas_call(
        body, name=name, grid=(B, d // spg), in_specs=in_specs, out_specs=out_specs, out_shape=out_shape,
        compiler_params=_cparams(("arbitrary", "arbitrary")),
    )(*args)


MEM_QROWS = 512


def _mem_attn_fwd(qc, mem, w_mem):
    B = qc.shape[0]

    def body(q_ref, mem_ref, w_ref, o_ref, mkv_ref):
        mkv_ref[...] = _dot(mem_ref[...].astype(MXU), w_ref[...]).astype(MXU)

        def loop(i, carry):
            rows = pl.ds(pl.multiple_of(i * MEM_QROWS, MEM_QROWS), MEM_QROWS)
            for h in range(MEM_H):
                sT = _dot_nt(mkv_ref[:, _col(0, h)], q_ref[rows, _col(0, h)])
                m = jnp.max(sT, axis=0, keepdims=True)
                pT = jnp.exp(sT - m)
                l = jnp.sum(pT, axis=0, keepdims=True)
                oT = _dot_t0(mkv_ref[:, _col(W_C, h)], (pT * (1.0 / l)).astype(MXU))
                _store_o_lse(o_ref, [rows], oT, m + jnp.log(l), [_col(0, h)], [_col(W_C, h)])
            return carry
        lax.fori_loop(0, SEQ // MEM_QROWS, loop, 0)

    return pl.pallas_call(
        body, name="mem_attn_fwd", grid=(B,),
        in_specs=[pl.BlockSpec((None, SEQ, W_C), lambda b: (b, 0, 0)),
                  pl.BlockSpec((None, MEM_LEN, D_MODEL), lambda b: (b, 0, 0)),
                  pl.BlockSpec((D_MODEL, 2 * W_C), lambda b: (0, 0))],
        out_specs=[pl.BlockSpec((None, SEQ, 2 * W_C), lambda b: (b, 0, 0)),
                   pl.BlockSpec((None, MEM_LEN, 2 * W_C), lambda b: (b, 0, 0))],
        out_shape=[jax.ShapeDtypeStruct((B, SEQ, 2 * W_C), F32),
                   jax.ShapeDtypeStruct((B, MEM_LEN, 2 * W_C), MXU)],
        compiler_params=_cparams(("arbitrary",)),
    )(qc, mem, w_mem)


def _mem_attn_bwd(qc, mkv, dob, mem):
    B = qc.shape[0]

    def body(q_ref, mkv_ref, dob_ref, mem_ref, dq_ref, dw_ref, dmkv_ref):
        @pl.when(pl.program_id(0) == 0)
        def _():
            dw_ref[...] = jnp.zeros_like(dw_ref)
        dmkv_ref[...] = jnp.zeros_like(dmkv_ref)

        def loop(i, carry):
            rows = pl.ds(pl.multiple_of(i * MEM_QROWS, MEM_QROWS), MEM_QROWS)
            for h0 in range(0, MEM_H, 2):
                lses = _pair_rows(dob_ref, rows, W_C, h0)
                deltas = _pair_rows(dob_ref, rows, 2 * W_C, h0)
                dqT = []
                for j in range(2):
                    h = h0 + j
                    q = q_ref[rows, _col(0, h)]
                    do = dob_ref[rows, _col(0, h)].astype(MXU)
                    mk, mv = mkv_ref[:, _col(0, h)], mkv_ref[:, _col(W_C, h)]
                    pT = jnp.exp(_dot_nt(mk, q) - lses[j])
                    dsT = pT * (_dot_nt(mv, do) - deltas[j])
                    dsb = dsT.astype(MXU)
                    dmkv_ref[:, _col(0, h)] += _dot(dsb, q)
                    dmkv_ref[:, _col(W_C, h)] += _dot(pT.astype(MXU), do)
                    dqT.append(_dot_t0(mk, dsb))
                dq_ref[rows, slice(h0 * HEAD, (h0 + 2) * HEAD)] = jnp.concatenate(dqT, axis=0).T
            return carry
        lax.fori_loop(0, SEQ // MEM_QROWS, loop, 0)
        dw_ref[...] += _dot_tn(mem_ref[...], dmkv_ref[...].astype(MXU))

    return pl.pallas_call(
        body, name="mem_attn_bwd", grid=(B,),
        in_specs=[pl.BlockSpec((None, SEQ, W_C), lambda b: (b, 0, 0)),
                  pl.BlockSpec((None, MEM_LEN, 2 * W_C), lambda b: (b, 0, 0)),
                  pl.BlockSpec((None, SEQ, 3 * W_C), lambda b: (b, 0, 0)),
                  pl.BlockSpec((None, MEM_LEN, D_MODEL), lambda b: (b, 0, 0))],
        out_specs=[pl.BlockSpec((None, SEQ, W_C), lambda b: (b, 0, 0)),
                   pl.BlockSpec((D_MODEL, 2 * W_C), lambda b: (0, 0))],
        out_shape=[jax.ShapeDtypeStruct((B, SEQ, W_C), F32),
                   jax.ShapeDtypeStruct((D_MODEL, 2 * W_C), F32)],
        scratch_shapes=[pltpu.VMEM((MEM_LEN, 2 * W_C), F32)],
        compiler_params=_cparams(("arbitrary",)),
    )(qc, mkv, dob, mem)


def _headsum(t, e):
    if MXU == F32:
        return _dot(t, e)
    hi = t.astype(MXU)
    lo = (t - hi.astype(F32)).astype(MXU)
    return _dot(hi, e) + _dot(lo, e)


def _post(olse_a, olse_b1, olse_b4, olse_b16, olse_c, z, x2, tgt, g, gain, bias, w_out, hsum):
    T = x2.shape[0]
    tm = 256

    def body(oa_ref, b1_ref, b4_ref, b16_ref, oc_ref, z_ref, x_ref, t_ref, g_ref, gain_ref, bias_ref, w_ref,
             e_ref, gx_ref, doba_ref, dobb_ref, dobc_ref, dz_ref, dw_ref, small_ref, loss_ref):
        @pl.when(pl.program_id(0) == 0)
        def _():
            dw_ref[...] = jnp.zeros_like(dw_ref)
            small_ref[...] = jnp.zeros_like(small_ref)
            loss_ref[...] = jnp.zeros_like(loss_ref)

        oa = oa_ref[:, 0:W_A]
        lse_a = oa_ref[:, W_A:2 * W_A]
        o1, l1 = b1_ref[:, 0:W_B], b1_ref[:, W_B:2 * W_B]
        o4, l4 = b4_ref[:, 0:W_B], b4_ref[:, W_B:2 * W_B]
        o16, l16 = b16_ref[:, 0:W_B], b16_ref[:, W_B:2 * W_B]
        mx = jnp.maximum(jnp.maximum(l1, l4), l16)
        e1, e4, e16 = jnp.exp(l1 - mx), jnp.exp(l4 - mx), jnp.exp(l16 - mx)
        den = e1 + e4 + e16
        ob = (e1 * o1 + e4 * o4 + e16 * o16) / den
        lse_b = mx + jnp.log(den)
        oc = oc_ref[:, 0:W_C]
        lse_c = oc_ref[:, W_C:2 * W_C]

        def rms(o):
            rr = lax.rsqrt(jnp.mean(o * o, axis=1, keepdims=True) + RMS_EPS)
            return o * rr, rr

        na, ra = rms(oa)
        nb, rb = rms(ob)
        nc, rc = rms(oc)
        n = jnp.concatenate([na, nb, nc], axis=1)
        zz = z_ref[...]
        sig = 1.0 / (1.0 + jnp.exp(-zz))
        sz = zz * sig
        gg = g_ref[...]
        u = n * gg * sz
        ub = u.astype(MXU)
        w = w_ref[...]
        r = ALPHA * x_ref[...] + _dot(ub, w)
        mu = jnp.mean(r, axis=1, keepdims=True)
        rc0 = r - mu
        rstd = lax.rsqrt(jnp.mean(rc0 * rc0, axis=1, keepdims=True) + LN_EPS)
        xhat = rc0 * rstd
        gain_v = gain_ref[...]
        err = xhat * gain_v + bias_ref[...] - t_ref[...]
        loss_ref[...] += 0.5 * jnp.sum(err * err) * (1.0 / D_MODEL)

        dout = err * (1.0 / D_MODEL)
        small_ref[0:1, :] += jnp.sum(dout * xhat, axis=0, keepdims=True)
        small_ref[1:2, :] += jnp.sum(dout, axis=0, keepdims=True)
        dxh = dout * gain_v
        dr = rstd * (dxh - jnp.mean(dxh, axis=1, keepdims=True)
                     - xhat * jnp.mean(dxh * xhat, axis=1, keepdims=True))
        gx_ref[...] = ALPHA * dr
        drb = dr.astype(MXU)
        du = _dot_nt(drb, w)
        dw_ref[...] += _dot_tn(u, drb)
        small_ref[2:3, :] += jnp.sum(du * n * sz, axis=0, keepdims=True)
        dz = du * n * gg * (sig * (1.0 + zz * (1.0 - sig)))
        dz_ref[...] = dz.astype(MXU)
        small_ref[3:4, :] += jnp.sum(dz, axis=0, keepdims=True)
        dn = du * gg * sz

        def branch(lo, hi, o, nbr, rr, lse, out_ref):
            wd = hi - lo
            dnb = dn[:, lo:hi]
            dob = rr * (dnb - nbr * jnp.mean(dnb * nbr, axis=1, keepdims=True))
            out_ref[:, 0:wd] = dob
            out_ref[:, wd:2 * wd] = lse
            out_ref[:, 2 * wd:3 * wd] = _headsum(dob * o, e_ref[0:wd, 0:wd])

        branch(0, W_A, oa, na, ra, lse_a, doba_ref)
        branch(W_A, W_A + W_B, ob, nb, rb, lse_b, dobb_ref)
        branch(W_A + W_B, D_MIX, oc, nc, rc, lse_c, dobc_ref)

    row = lambda w: pl.BlockSpec((tm, w), lambda i: (i, 0))
    full = lambda a, b: pl.BlockSpec((a, b), lambda i: (0, 0))
    return pl.pallas_call(
        body, name="post_fwd_bwd", grid=(T // tm,),
        in_specs=[row(2 * W_A), row(2 * W_B), row(2 * W_B), row(2 * W_B), row(2 * W_C),
                  row(D_MIX), row(D_MODEL), row(D_MODEL),
                  full(1, D_MIX), full(1, D_MODEL), full(1, D_MODEL), full(D_MIX, D_MODEL), full(W_A, W_A)],
        out_specs=[row(D_MODEL), row(3 * W_A), row(3 * W_B), row(3 * W_C), row(D_MIX),
                   full(D_MIX, D_MODEL), full(8, D_MODEL), full(8, 128)],
        out_shape=[jax.ShapeDtypeStruct((T, D_MODEL), F32),
                   jax.ShapeDtypeStruct((T, 3 * W_A), F32),
                   jax.ShapeDtypeStruct((T, 3 * W_B), F32),
                   jax.ShapeDtypeStruct((T, 3 * W_C), F32),
                   jax.ShapeDtypeStruct((T, D_MIX), MXU),
                   jax.ShapeDtypeStruct((D_MIX, D_MODEL), F32),
                   jax.ShapeDtypeStruct((8, D_MODEL), F32),
                   jax.ShapeDtypeStruct((8, 128), F32)],
        compiler_params=_cparams(("arbitrary",)),
    )(olse_a, olse_b1, olse_b4, olse_b16, olse_c, z, x2, tgt, g, gain, bias, w_out, hsum)


def _in_proj_bwd_dx(dqkva, dqkvb1, dqkvb4, dqkvb16, dqc, dz, gx1, w_in, tab):
    T = gx1.shape[0]
    tm = 256
    nt = SEQ // tm
    HQ = D_IN - D_MIX

    def body(da_ref, d1_ref, d4_ref, d16_ref, dqc_ref, dz_ref, gx_ref, w_ref, tab_ref, dx_ref, dh_ref, db_ref):
        @pl.when(pl.program_id(0) == 0)
        def _():
            db_ref[...] = jnp.zeros_like(db_ref)
        tab = tab_ref[...]
        dbs = d1_ref[...] + d4_ref[...] + d16_ref[...]
        parts = [_rope(da_ref[:, 0:512], tab, -1) * Q_SCALE,
                 _rope(da_ref[:, 512:640], tab, -1),
                 da_ref[:, 640:768],
                 _rope(dbs[:, 0:256], tab, -1) * Q_SCALE,
                 _rope(dbs[:, 256:512], tab, -1),
                 dbs[:, 512:768],
                 dqc_ref[...] * Q_SCALE]
        dhq = jnp.concatenate(parts, axis=1)
        db_ref[0:1, :] += jnp.sum(dhq, axis=0, keepdims=True)
        dhq_b = dhq.astype(MXU)
        dzb = dz_ref[...]
        dh_ref[:, 0:HQ] = dhq_b
        dh_ref[:, HQ:D_IN] = dzb
        dx = _dot_nt(dhq_b, w_ref[:, 0:HQ]) + _dot_nt(dzb, w_ref[:, HQ:D_IN])
        dx_ref[...] = dx + gx_ref[...]

    row = lambda w: pl.BlockSpec((tm, w), lambda i: (i, 0))
    return pl.pallas_call(
        body, name="in_proj_bwd_dx", grid=(T // tm,),
        in_specs=[row(768), row(768), row(768), row(768), row(256), row(D_MIX), row(D_MODEL),
                  pl.BlockSpec((D_MODEL, D_IN), lambda i: (0, 0)),
                  pl.BlockSpec((tm, 384), lambda i: (i % nt, 0))],
        out_specs=[row(D_MODEL), row(D_IN), pl.BlockSpec((8, HQ), lambda i: (0, 0))],
        out_shape=[jax.ShapeDtypeStruct((T, D_MODEL), F32), jax.ShapeDtypeStruct((T, D_IN), MXU),
                   jax.ShapeDtypeStruct((8, HQ), F32)],
        compiler_params=_cparams(("arbitrary",)),
    )(dqkva, dqkvb1, dqkvb4, dqkvb16, dqc, dz, gx1, w_in, tab)


def _in_proj_bwd_dw(x2, dh):
    T = x2.shape[0]
    tk = 512
    tn = 1408

    def body(x_ref, dh_ref, dw_ref):
        @pl.when(pl.program_id(1) == 0)
        def _():
            dw_ref[...] = jnp.zeros_like(dw_ref)
        dw_ref[...] += _dot_tn(x_ref[...], dh_ref[...])

    return pl.pallas_call(
        body, name="in_proj_bwd_dw", grid=(D_IN // tn, T // tk),
        in_specs=[pl.BlockSpec((tk, D_MODEL), lambda j, k: (k, 0)),
                  pl.BlockSpec((tk, tn), lambda j, k: (k, j))],
        out_specs=pl.BlockSpec((D_MODEL, tn), lambda j, k: (0, j)),
        out_shape=jax.ShapeDtypeStruct((D_MODEL, D_IN), F32),
        compiler_params=_cparams(("arbitrary", "arbitrary")),
    )(x2, dh)


def _local_step(x, mem, w_in, w_mem, w_out, b_in, sinks, g, gain, bias, tgt):
    B = x.shape[0]
    T = B * SEQ
    x2 = x.reshape(T, D_MODEL)
    t2 = tgt.reshape(T, D_MODEL)
    tab = _rope_table()
    lane = jnp.arange(W_A)
    hsum = (lane[:, None] // HEAD == lane[None, :] // HEAD).astype(MXU)

    qkva, qkvb, qc, z = _in_proj(x2, w_in, b_in, tab)

    def view(a, d):
        return a.reshape(B, SEQ // d, d * a.shape[-1])

    def unview(a, w):
        return a.reshape(T, w)

    qkva3 = view(qkva, 1)
    qkvb_v = {d: view(qkvb, d) for d in (1, 4, 16)}
    qc3 = view(qc, 1)

    spg = {1: 1, 4: 2, 16: 4}
    unr = {1: 3, 4: 3, 16: 1}
    olse_a = _attn_fwd(qkva3, sinks, d=1, spg=1, unroll=1, hq=SWA_Q, hkv=SWA_KV, max_dist=BLK - 1,
                       name="swa_fwd")
    olse_b = {d: _attn_fwd(qkvb_v[d], None, d=d, spg=spg[d], unroll=unr[d], hq=DIL_H, hkv=DIL_H, max_dist=BLK,
                           name=f"dil{d}_fwd") for d in (1, 4, 16)}
    olse_c, mkv = _mem_attn_fwd(qc3, mem, w_mem)

    gx1, doba, dobb, dobc, dz, dw_out, small, loss = _post(
        unview(olse_a, 2 * W_A), unview(olse_b[1], 2 * W_B), unview(olse_b[4], 2 * W_B),
        unview(olse_b[16], 2 * W_B), unview(olse_c, 2 * W_C), z, x2, t2, g, gain, bias, w_out, hsum)

    dqkva, dsink = _attn_bwd(qkva3, view(doba, 1), sinks, d=1, spg=1, unroll=1, hq=SWA_Q, hkv=SWA_KV,
                             max_dist=BLK - 1, name="swa_bwd")
    dqkvb = {d: _attn_bwd(qkvb_v[d], view(dobb, d), None, d=d, spg=spg[d], unroll=unr[d], hq=DIL_H, hkv=DIL_H,
                          max_dist=BLK, name=f"dil{d}_bwd")[0] for d in (1, 4, 16)}
    dqc, dw_mem = _mem_attn_bwd(qc3, mkv, view(dobc, 1), mem)

    grad_x, dh, dbq = _in_proj_bwd_dx(unview(dqkva, 768), unview(dqkvb[1], 768), unview(dqkvb[4], 768),
                                      unview(dqkvb[16], 768), unview(dqc, W_C), dz, gx1, w_in, tab)
    dw_in = _in_proj_bwd_dw(x2, dh)

    db_in = jnp.concatenate([dbq[0], small[3]])
    grads_small = dict(b_in=db_in, sinks=dsink[:, 0], g=small[2], gain=small[0], bias=small[1])
    return loss[0, 0], grad_x.reshape(B, SEQ, D_MODEL), dw_in, dw_mem, dw_out, grads_small


def _my_pos():
    return lax.axis_index("x"), lax.axis_index("y"), lax.axis_index("c")


def _gather_weights(w_in_s, w_mem_s, w_out_s):
    shards = (w_in_s, w_mem_s, w_out_s)
    n_arr = len(shards)

    def body(*refs):
        ins, outs = refs[0:n_arr], refs[n_arr:2 * n_arr]
        send_sems, recv_sems, local_sems = refs[2 * n_arr:]
        x, y, c = _my_pos()
        me, sibling = (x, y, c), (x, y, 1 - c)
        chips = [(1 - x, y), (x, 1 - y), (1 - x, 1 - y)]

        def slot(a, pos):
            return outs[a].at[4 * pos[0] + 2 * pos[1] + pos[2]]

        def copy(a, k, block, to, src=None):
            return pltpu.make_async_remote_copy(
                src_ref=slot(a, block) if src is None else src, dst_ref=slot(a, block),
                send_sem=send_sems.at[a, k], recv_sem=recv_sems.at[a, k],
                device_id=to, device_id_type=MESH)

        mine = [pltpu.make_async_copy(ins[a], slot(a, me), local_sems.at[a]) for a in range(n_arr)]
        for cp in mine:
            cp.start()
        first = []
        for a in range(n_arr):
            first.append(copy(a, 0, me, sibling, src=ins[a]))
            first += [copy(a, 1 + j, me, (*chip, c), src=ins[a]) for j, chip in enumerate(chips)]
        for cp in first:
            cp.start()
        passed = []
        for j, chip in enumerate(chips):
            for a in range(n_arr):
                copy(a, 1 + j, (*chip, c), me).wait_recv()
                fwd = copy(a, 4 + j, (*chip, c), sibling)
                fwd.start()
                passed.append(fwd)
        for a in range(n_arr):
            copy(a, 0, sibling, me).wait_recv()
            for j, chip in enumerate(chips):
                copy(a, 4 + j, (*chip, 1 - c), me).wait_recv()
        for cp in first + passed:
            cp.wait_send()
        for cp in mine:
            cp.wait()

    any_spec = pl.BlockSpec(memory_space=pl.ANY)
    return pl.pallas_call(
        body, name="gather_weights",
        in_specs=[any_spec] * n_arr, out_specs=[any_spec] * n_arr,
        out_shape=[jax.ShapeDtypeStruct((N_DEV,) + s.shape, s.dtype) for s in shards],
        scratch_shapes=[pltpu.SemaphoreType.DMA((n_arr, 7)), pltpu.SemaphoreType.DMA((n_arr, 7)),
                        pltpu.SemaphoreType.DMA((n_arr,))],
    )(*shards)


def _adamw(w, g, m, v):
    m = ADAM_B1 * m + (1.0 - ADAM_B1) * g
    v = ADAM_B2 * v + (1.0 - ADAM_B2) * (g * g)
    m_hat = m / (1.0 - ADAM_B1 ** ADAM_STEP)
    v_hat = v / (1.0 - ADAM_B2 ** ADAM_STEP)
    delta = -ADAM_LR * (m_hat / (jnp.sqrt(v_hat) + ADAM_EPS) + ADAM_WD * w)
    return delta, m, v


def _reduce_adam(sends, owns, small_g, params):
    n_big = 3
    shapes = [o.shape for o in owns] + [small_g.shape]
    chunk_rows = [128, 128, 128, SMALL_ROWS]

    def body(*refs):
        it = iter(refs)
        send_refs = [next(it) for _ in range(n_big)]
        own_refs = [next(it) for _ in range(n_big)]
        sg_ref = next(it)
        wmv_refs = [[next(it) for _ in range(3)] for _ in range(n_big + 1)]
        out_refs = [[next(it) for _ in range(4)] for _ in range(n_big + 1)]
        land_refs = [next(it) for _ in range(n_big + 1)]
        send_sems, recv_sems = next(it), next(it)

        x, y, c = _my_pos()
        me = 4 * x + 2 * y + c
        copies = []
        for k in range(1, N_DEV):
            px = 1 - x if k & 4 else x
            py = 1 - y if k & 2 else y
            pc = 1 - c if k & 1 else c
            peer = 4 * px + 2 * py + pc
            for a in range(n_big + 1):
                src = send_refs[a].at[peer] if a < n_big else sg_ref
                cp = pltpu.make_async_remote_copy(
                    src_ref=src, dst_ref=land_refs[a].at[me],
                    send_sem=send_sems.at[a, k - 1], recv_sem=recv_sems.at[a, k - 1],
                    device_id=(px, py, pc), device_id_type=MESH)
                cp.start()
                copies.append(cp)
        for a in range(n_big):
            land_refs[a][me] = jnp.zeros(shapes[a], land_refs[a].dtype)
        land_refs[n_big][me] = sg_ref[...]
        for cp in copies:
            cp.wait_recv()
        for cp in copies:
            cp.wait_send()

        for a in range(n_big + 1):
            rows, cr = shapes[a][0], chunk_rows[a]
            w_ref, m_ref, v_ref = wmv_refs[a]
            g_ref, d_ref, nm_ref, nv_ref = out_refs[a]
            land = land_refs[a]

            def chunk(i, carry, a=a, cr=cr, w_ref=w_ref, m_ref=m_ref, v_ref=v_ref, g_ref=g_ref,
                      d_ref=d_ref, nm_ref=nm_ref, nv_ref=nv_ref, land=land):
                rs = pl.ds(pl.multiple_of(i * cr, cr), cr)
                if a < n_big:
                    gsum = own_refs[a][rs, :]
                    for j in range(N_DEV):
                        gsum = gsum + land[j, rs, :].astype(F32)
                else:
                    gsum = land[0, rs, :]
                    for j in range(1, N_DEV):
                        gsum = gsum + land[j, rs, :]
                delta, nm, nv = _adamw(w_ref[rs, :], gsum, m_ref[rs, :], v_ref[rs, :])
                g_ref[rs, :] = gsum
                d_ref[rs, :] = delta
                nm_ref[rs, :] = nm
                nv_ref[rs, :] = nv
                return carry
            lax.fori_loop(0, rows // cr, chunk, 0)

    any_spec = pl.BlockSpec(memory_space=pl.ANY)
    vmem = pl.BlockSpec(memory_space=pltpu.VMEM)
    flat_params = [p for grp in params for p in grp]
    out_shape = [jax.ShapeDtypeStruct(s, F32) for s in shapes for _ in range(4)]
    scratch = [pltpu.VMEM((N_DEV,) + shapes[a], sends[a].dtype) for a in range(n_big)]
    scratch.append(pltpu.VMEM((N_DEV,) + shapes[n_big], F32))
    scratch += [pltpu.SemaphoreType.DMA((n_big + 1, N_DEV - 1)), pltpu.SemaphoreType.DMA((n_big + 1, N_DEV - 1))]
    outs = pl.pallas_call(
        body, name="reduce_adam",
        in_specs=[any_spec] * n_big + [vmem] * (n_big + 1 + len(flat_params)),
        out_specs=[vmem] * len(out_shape), out_shape=out_shape, scratch_shapes=scratch,
        compiler_params=pltpu.CompilerParams(vmem_limit_bytes=VMEM_LIMIT),
    )(*sends, *owns, small_g, *flat_params)
    return [outs[4 * a:4 * a + 4] for a in range(n_big + 1)]


_SMALL_SIZES = (("b_in", D_IN), ("g", D_MIX), ("gain", D_MODEL), ("bias", D_MODEL), ("sinks", SWA_Q))


def _pack_small(d):
    flat = jnp.concatenate([d[k].reshape(-1).astype(F32) for k, _ in _SMALL_SIZES])
    flat = jnp.pad(flat, (0, SMALL_ROWS * 128 - flat.shape[0]))
    return flat.reshape(SMALL_ROWS, 128)


def _unpack_small(p):
    flat = p.reshape(-1)
    out, off = {}, 0
    for k, n in _SMALL_SIZES:
        out[k] = flat[off:off + n].reshape(1, n)
        off += n
    return out


def kernel(x, mem, w_in, b_in, w_mem, attn_sinks, g_branch, w_out, ln_gain, ln_bias, loss_target, m_w_in, m_b_in, m_w_mem, m_attn_sinks, m_g_branch, m_w_out, m_ln_gain, m_ln_bias, v_w_in, v_b_in, v_w_mem, v_attn_sinks, v_g_branch, v_w_out, v_ln_gain, v_ln_bias):
    xi, yi, ci = _my_pos()
    me = 4 * xi + 2 * yi + ci

    w_in_all, w_mem_all, w_out_all = _gather_weights(
        w_in[0].astype(MXU), w_mem[0].astype(MXU), w_out[0].astype(MXU))
    w_in_f = w_in_all.transpose(1, 0, 2).reshape(D_MODEL, D_IN)
    w_mem_f = w_mem_all.reshape(D_MODEL, 2 * W_C)
    w_out_f = w_out_all.reshape(D_MIX, D_MODEL)

    loss, grad_x, dw_in, dw_mem, dw_out, gsmall = _local_step(
        x, mem, w_in_f, w_mem_f, w_out_f, b_in, attn_sinks[0], g_branch, ln_gain, ln_bias, loss_target)
    loss = lax.psum(loss, ("x", "y", "c"))

    dw_in_blocks = dw_in.reshape(D_MODEL, N_DEV, COLS_PER_DEV).transpose(1, 0, 2)
    dw_mem_blocks = dw_mem.reshape(N_DEV, ROWS_PER_DEV, 2 * W_C)
    dw_out_blocks = dw_out.reshape(N_DEV, ROWS_PER_DEV, D_MODEL)
    blocks = (dw_in_blocks, dw_mem_blocks, dw_out_blocks)
    sends = [b.astype(MXU) for b in blocks]
    owns = [lax.dynamic_index_in_dim(b, me, axis=0, keepdims=False) for b in blocks]
    small_w = _pack_small(dict(b_in=b_in, g=g_branch, gain=ln_gain, bias=ln_bias, sinks=attn_sinks))
    small_m = _pack_small(dict(b_in=m_b_in, g=m_g_branch, gain=m_ln_gain, bias=m_ln_bias, sinks=m_attn_sinks))
    small_v = _pack_small(dict(b_in=v_b_in, g=v_g_branch, gain=v_ln_gain, bias=v_ln_bias, sinks=v_attn_sinks))
    params = [(w_in[0], m_w_in[0], v_w_in[0]), (w_mem[0], m_w_mem[0], v_w_mem[0]),
              (w_out[0], m_w_out[0], v_w_out[0]), (small_w, small_m, small_v)]
    res = _reduce_adam(sends, owns, _pack_small(gsmall), params)

    big = [[r[None] for r in res[a]] for a in range(3)]
    sm = [_unpack_small(r) for r in res[3]]

    def group(i):
        return (big[0][i], sm[i]["b_in"], big[1][i], sm[i]["sinks"], sm[i]["g"], big[2][i],
                sm[i]["gain"], sm[i]["bias"])

    return (loss, grad_x, *group(0), *group(1), *group(2), *group(3))
```

```python
import functools
import math

import jax
import jax.numpy as jnp
from jax import lax
from jax.experimental import pallas as pl
from jax.experimental.pallas import tpu as pltpu

F32 = jnp.float32
MXU = jnp.bfloat16

D_MODEL = 1024
SEQ = 2048
HEAD = 64
BLK = 128
SWA_Q, SWA_KV = 8, 2
DIL_H = 4
MEM_H = 4
MEM_LEN = 256
W_A, W_KVA, W_B, W_C = 512, 128, 256, 256
D_MIX = 1024
D_IN = 2816
N_DEV = 8
COLS_PER_DEV = D_IN // N_DEV
ROWS_PER_DEV = D_MODEL // N_DEV
ROPE_THETA = 10000.0
LN_EPS = 1e-5
RMS_EPS = 1e-6
ALPHA = 2.0 ** 0.25
Q_SCALE = HEAD ** -0.5
NEG = -1e30
SMALL_ROWS = 48
VMEM_LIMIT = 56 * 1024 * 1024

ADAM_LR = 0.001
ADAM_B1 = 0.9
ADAM_B2 = 0.999
ADAM_EPS = 1e-08
ADAM_WD = 0.01
ADAM_STEP = 10

MESH = pl.DeviceIdType.MESH


def _cparams(sem=None):
    return pltpu.CompilerParams(dimension_semantics=sem, vmem_limit_bytes=VMEM_LIMIT)


def _dot(a, b):
    return jnp.dot(a, b, preferred_element_type=F32)


def _dot_nt(a, b):
    return lax.dot_general(a, b, (((1,), (1,)), ((), ())), preferred_element_type=F32)


def _dot_t0(a, b):
    return lax.dot_general(a, b, (((0,), (0,)), ((), ())), preferred_element_type=F32)


def _dot_tn(a, b):
    return jnp.dot(a.T.astype(MXU), b, preferred_element_type=F32)


def _rope(t, tab, sign):
    cos, sa, sb = tab[:, 0:128], tab[:, 128:256], tab[:, 256:384]
    outs = []
    for c in range(t.shape[1] // 128):
        tc = t[:, c * 128:(c + 1) * 128]
        r = pltpu.roll(tc, 96, 1) * sa + pltpu.roll(tc, 32, 1) * sb
        outs.append(tc * cos + r if sign > 0 else tc * cos - r)
    return outs[0] if len(outs) == 1 else jnp.concatenate(outs, axis=1)


def _rope_table():
    pos = jnp.arange(SEQ, dtype=F32)
    inv = ROPE_THETA ** (-jnp.arange(0, HEAD, 2, dtype=F32) / HEAD)
    ang = pos[:, None] * inv[None, :]
    ang = jnp.concatenate([ang, ang], axis=-1)
    cos, sin = jnp.cos(ang), jnp.sin(ang)
    lane = jnp.arange(HEAD)[None, :]
    sa = jnp.where(lane < HEAD // 2, -sin, 0.0)
    sb = jnp.where(lane >= HEAD // 2, sin, 0.0)
    two = lambda t: jnp.concatenate([t, t], axis=-1)
    return jnp.concatenate([two(cos), two(sa), two(sb)], axis=-1).astype(F32)


def _in_proj(x2, w_in, b_in, tab):
    T = x2.shape[0]
    tm = 512

    def body(x_ref, w_ref, b_ref, tab_ref, qkva_ref, qkvb_ref, qc_ref, z_ref):
        xb = x_ref[...].astype(MXU)
        tab = tab_ref[...]

        def seg(c0, c1):
            return _dot(xb, w_ref[:, c0:c1]) + b_ref[:, c0:c1]

        qkva_ref[:, 0:512] = (_rope(seg(0, 512), tab, 1) * Q_SCALE).astype(MXU)
        qkva_ref[:, 512:640] = _rope(seg(512, 640), tab, 1).astype(MXU)
        qkva_ref[:, 640:768] = seg(640, 768).astype(MXU)
        qkvb = (_rope(seg(768, 1024), tab, 1) * Q_SCALE, _rope(seg(1024, 1280), tab, 1), seg(1280, 1536))
        for j, t in enumerate(qkvb):
            for c in range(2):
                qkvb_ref[2 * j + c] = t[:, c * 128:(c + 1) * 128]
        qc_ref[...] = (seg(1536, 1792) * Q_SCALE).astype(MXU)
        z_ref[...] = seg(1792, 2816)

    nt = SEQ // tm
    return pl.pallas_call(
        body, name="in_proj_fwd",
        grid=(T // tm,),
        in_specs=[pl.BlockSpec((tm, D_MODEL), lambda i: (i, 0)),
                  pl.BlockSpec((D_MODEL, D_IN), lambda i: (0, 0)),
                  pl.BlockSpec((1, D_IN), lambda i: (0, 0)),
                  pl.BlockSpec((tm, 384), lambda i: (i % nt, 0))],
        out_specs=[pl.BlockSpec((tm, 768), lambda i: (i, 0)),
                   pl.BlockSpec((None, 6, tm, 128), lambda i: (i // nt, 0, i % nt, 0)),
                   pl.BlockSpec((tm, 256), lambda i: (i, 0)),
                   pl.BlockSpec((tm, D_MIX), lambda i: (i, 0))],
        out_shape=[jax.ShapeDtypeStruct((T, 768), MXU), jax.ShapeDtypeStruct((T // SEQ, 6, SEQ, 128), F32),
                   jax.ShapeDtypeStruct((T, 256), MXU), jax.ShapeDtypeStruct((T, D_MIX), F32)],
        compiler_params=_cparams(("arbitrary",)),
    )(x2, w_in, b_in, tab)


CHAIN = 4


def _band_bias(max_dist):
    kj = lax.broadcasted_iota(jnp.int32, (2 * BLK, BLK), 0)
    qi = lax.broadcasted_iota(jnp.int32, (2 * BLK, BLK), 1)
    dist = qi + BLK - kj
    band = jnp.where((dist >= 0) & (dist <= max_dist), 0.0, NEG).astype(F32)
    k1 = lax.broadcasted_iota(jnp.int32, (BLK, BLK), 0)
    q1 = lax.broadcasted_iota(jnp.int32, (BLK, BLK), 1)
    first = jnp.where((q1 - k1 >= 0) & (q1 - k1 <= max_dist), 0.0, NEG).astype(F32)
    return jnp.concatenate([band] * CHAIN, axis=1), jnp.concatenate([first] * CHAIN, axis=1)


def _lanes(parts):
    return jnp.concatenate(parts, axis=1)


def _col(off, h):
    return slice(off + h * HEAD, off + (h + 1) * HEAD)


def _pair_rows(ref, rows, off, h0):
    t = ref[rows, slice(off + h0 * HEAD, off + (h0 + 2) * HEAD)].T
    return t[0:1, :], t[HEAD:HEAD + 1, :]


def _store_o_lse(o_ref, rows, oT, lse, ocols, lcols):
    n = len(ocols)
    res = jnp.concatenate([oT, jnp.broadcast_to(lse, (HEAD, lse.shape[1]))], axis=0).T
    rq = res.shape[0] // n
    for i in range(n):
        o_ref[rows[i], ocols[i]] = res[i * rq:(i + 1) * rq, 0:HEAD]
        o_ref[rows[i], lcols[i]] = res[i * rq:(i + 1) * rq, HEAD:2 * HEAD]


def _attn_geometry(qkv, d, spg, hq, hkv):
    B, Ls, wall = qkv.shape
    qw, kw = hq * HEAD, hkv * HEAD
    wtot = qw + 2 * kw
    assert wall == d * wtot and Ls % BLK == 0 and d % spg == 0 and hq % CHAIN == 0
    G = hq // hkv
    assert G in (1, CHAIN)
    return B, Ls, qw, kw, wtot, G, Ls // BLK


def _attn_fwd(qkv, sinks, *, d, spg, unroll, hq, hkv, max_dist, name):
    B, Ls, qw, kw, wtot, G, nblk = _attn_geometry(qkv, d, spg, hq, hkv)
    use_sink = sinks is not None
    assert (nblk - 1) % unroll == 0
    trips = (nblk - 1) // unroll

    def body(*refs):
        if use_sink:
            sink_ref, qkv_ref, o_ref = refs
        else:
            qkv_ref, o_ref = refs
        band, first = _band_bias(max_dist)
        for st in range(spg):
            ib, ob = st * wtot, st * 2 * qw
            chains = []
            for ch in range(hq // CHAIN):
                heads = list(range(ch * CHAIN, (ch + 1) * CHAIN))
                sinkrow = _lanes([jnp.full((1, BLK), sink_ref[h], F32) for h in heads]) if use_sink else None

                def block(rows_q, rows_k, bias, heads=heads, ib=ib, ob=ob, sinkrow=sinkrow):
                    qs = [qkv_ref[rows_q, _col(ib, h)] for h in heads]
                    if G > 1:
                        hk = heads[0] // G
                        kk = qkv_ref[rows_k, _col(ib + qw, hk)]
                        vv = qkv_ref[rows_k, _col(ib + qw + kw, hk)]
                        sT = _dot_nt(kk, jnp.concatenate(qs, axis=0))
                    else:
                        vvs = [qkv_ref[rows_k, _col(ib + qw + kw, h)] for h in heads]
                        sT = _lanes([_dot_nt(qkv_ref[rows_k, _col(ib + qw, h)], qs[i])
                                     for i, h in enumerate(heads)])
                    sT = sT + bias
                    m = jnp.max(sT, axis=0, keepdims=True)
                    if use_sink:
                        m = jnp.maximum(m, sinkrow)
                    pT = jnp.exp(sT - m)
                    l = jnp.sum(pT, axis=0, keepdims=True)
                    if use_sink:
                        l = l + jnp.exp(sinkrow - m)
                    pnT = (pT * (1.0 / l)).astype(MXU)
                    if G > 1:
                        oT = _dot_t0(vv, pnT)
                    else:
                        oT = _lanes([_dot_t0(vvs[i], pnT[:, i * BLK:(i + 1) * BLK]) for i in range(CHAIN)])
                    _store_o_lse(o_ref, [rows_q] * CHAIN, oT, m + jnp.log(l),
                                 [_col(ob, h) for h in heads], [_col(ob + qw, h) for h in heads])

                chains.append(block)

            for block in chains:
                block(pl.ds(0, BLK), pl.ds(0, BLK), first)

            def blocks_at(i0, chains=chains):
                for u in range(unroll):
                    r0 = pl.multiple_of((i0 + u) * BLK, BLK)
                    rk = pl.multiple_of((i0 + u) * BLK - BLK, BLK)
                    for block in chains:
                        block(pl.ds(r0, BLK), pl.ds(rk, 2 * BLK), band)

            if trips == 1:
                blocks_at(1)
            elif trips > 1:
                def loop(j, carry, blocks_at=blocks_at):
                    blocks_at(1 + j * unroll)
                    return carry
                lax.fori_loop(0, trips, loop, 0)

    in_specs = [pl.BlockSpec((None, Ls, spg * wtot), lambda b, r: (b, 0, r))]
    args = [qkv]
    if use_sink:
        in_specs = [pl.BlockSpec(memory_space=pltpu.SMEM)] + in_specs
        args = [sinks] + args
    return pl.pallas_call(
        body, name=name, grid=(B, d // spg), in_specs=in_specs,
        out_specs=pl.BlockSpec((None, Ls, spg * 2 * qw), lambda b, r: (b, 0, r)),
        out_shape=jax.ShapeDtypeStruct((B, Ls, d * 2 * qw), F32),
        compiler_params=_cparams(("arbitrary", "arbitrary")),
    )(*args)


def _attn_bwd(qkv, dob, sinks, *, d, spg, unroll, hq, hkv, max_dist, name):
    B, Ls, qw, kw, wtot, G, nblk = _attn_geometry(qkv, d, spg, hq, hkv)
    assert dob.shape == (B, Ls, d * 3 * qw) and (nblk - 1) % unroll == 0
    trips = (nblk - 1) // unroll
    use_sink = sinks is not None

    def body(*refs):
        if use_sink:
            sink_ref, qkv_ref, dob_ref, dq_ref, dsink_ref = refs
        else:
            qkv_ref, dob_ref, dq_ref = refs
        band, first = _band_bias(max_dist)
        if use_sink:
            @pl.when((pl.program_id(0) == 0) & (pl.program_id(1) == 0))
            def _():
                dsink_ref[...] = jnp.zeros_like(dsink_ref)
        for st in range(spg):
            ib, db = st * wtot, st * 3 * qw
            dq_ref[:, ib + qw:ib + wtot] = jnp.zeros((Ls, 2 * kw), F32)
            chains = []
            for ch in range(hq // CHAIN):
                heads = list(range(ch * CHAIN, (ch + 1) * CHAIN))
                sinkrow = _lanes([jnp.full((1, BLK), sink_ref[h], F32) for h in heads]) if use_sink else None

                def block(rows_q, rows_k, bias, acc, heads=heads, ib=ib, db=db, sinkrow=sinkrow):
                    qs = [qkv_ref[rows_q, _col(ib, h)] for h in heads]
                    dos = [dob_ref[rows_q, _col(db, h)].astype(MXU) for h in heads]
                    lse = _lanes([r for j in range(0, CHAIN, 2)
                                  for r in _pair_rows(dob_ref, rows_q, db + qw, heads[j])])
                    delta = _lanes([r for j in range(0, CHAIN, 2)
                                    for r in _pair_rows(dob_ref, rows_q, db + 2 * qw, heads[j])])
                    if G > 1:
                        hk = heads[0] // G
                        kc, vc = _col(ib + qw, hk), _col(ib + qw + kw, hk)
                        kk, vv = qkv_ref[rows_k, kc], qkv_ref[rows_k, vc]
                        qst, dost = jnp.concatenate(qs, axis=0), jnp.concatenate(dos, axis=0)
                        sT, dpT = _dot_nt(kk, qst), _dot_nt(vv, dost)
                    else:
                        kks = [qkv_ref[rows_k, _col(ib + qw, h)] for h in heads]
                        vvs = [qkv_ref[rows_k, _col(ib + qw + kw, h)] for h in heads]
                        sT = _lanes([_dot_nt(kks[i], qs[i]) for i in range(CHAIN)])
                        dpT = _lanes([_dot_nt(vvs[i], dos[i]) for i in range(CHAIN)])
                    pT = jnp.exp(sT + bias - lse)
                    dsT = pT * (dpT - delta)
                    dsb, pb = dsT.astype(MXU), pT.astype(MXU)
                    if G > 1:
                        dq_ref[rows_k, kc] += _dot(dsb, qst)
                        dq_ref[rows_k, vc] += _dot(pb, dost)
                        dq = _dot_t0(kk, dsb).T
                        for i, h in enumerate(heads):
                            dq_ref[rows_q, _col(ib, h)] = dq[i * BLK:(i + 1) * BLK]
                    else:
                        dqT = []
                        for i, h in enumerate(heads):
                            ls = slice(i * BLK, (i + 1) * BLK)
                            dq_ref[rows_k, _col(ib + qw, h)] += _dot(dsb[:, ls], qs[i])
                            dq_ref[rows_k, _col(ib + qw + kw, h)] += _dot(pb[:, ls], dos[i])
                            dqT.append(_dot_t0(kks[i], dsb[:, ls]))
                        dq = _lanes(dqT).T
                        for i, h in enumerate(heads):
                            dq_ref[rows_q, _col(ib, h)] = dq[i * BLK:(i + 1) * BLK]
                    if use_sink:
                        acc = acc - jnp.exp(sinkrow - lse) * delta
                    return acc

                chains.append(block)

            accs = tuple(block(pl.ds(0, BLK), pl.ds(0, BLK), first, jnp.zeros((1, CHAIN * BLK), F32))
                         for block in chains)

            def blocks_at(i0, accs, chains=chains):
                for u in range(unroll):
                    r0 = pl.multiple_of((i0 + u) * BLK, BLK)
                    rk = pl.multiple_of((i0 + u) * BLK - BLK, BLK)
                    accs = tuple(block(pl.ds(r0, BLK), pl.ds(rk, 2 * BLK), band, acc)
                                 for block, acc in zip(chains, accs))
                return accs

            if trips == 1:
                accs = blocks_at(1, accs)
            elif trips > 1:
                def loop(j, accs, blocks_at=blocks_at):
                    return blocks_at(1 + j * unroll, accs)
                accs = lax.fori_loop(0, trips, loop, accs)
            if use_sink:
                for ch, acc in enumerate(accs):
                    for i in range(CHAIN):
                        h = ch * CHAIN + i
                        tot = jnp.sum(acc[:, i * BLK:(i + 1) * BLK], axis=1, keepdims=True)
                        dsink_ref[h:h + 1, :] += jnp.broadcast_to(tot, (1, 128))

    in_specs = [pl.BlockSpec((None, Ls, spg * wtot), lambda b, r: (b, 0, r)),
                pl.BlockSpec((None, Ls, spg * 3 * qw), lambda b, r: (b, 0, r))]
    out_specs = [pl.BlockSpec((None, Ls, spg * wtot), lambda b, r: (b, 0, r))]
    out_shape = [jax.ShapeDtypeStruct((B, Ls, d * wtot), F32)]
    args = [qkv, dob]
    if use_sink:
        in_specs = [pl.BlockSpec(memory_space=pltpu.SMEM)] + in_specs
        args = [sinks] + args
        out_specs.append(pl.BlockSpec((8, 128), lambda b, r: (0, 0)))
        out_shape.append(jax.ShapeDtypeStruct((8, 128), F32))
    return pl.pallas_call(
        body, name=name, grid=(B, d // spg), in_specs=in_specs, out_specs=out_specs, out_shape=out_shape,
        compiler_params=_cparams(("arbitrary", "arbitrary")),
    )(*args)


DILATIONS = (1, 4, 16)
DIL_PAIRS_H = DIL_H // 2


def _stream_rows(d, r, i, n):
    if d == 1:
        return pl.ds(pl.multiple_of(i * BLK, BLK), n)
    return pl.ds(r + i * (BLK * d), n, stride=d)


def _heads_of(ref, base, rows, dtype):
    out = []
    for c in range(DIL_PAIRS_H):
        t = ref.at[base + c][rows, :].astype(dtype)
        out += [t[:, 0:HEAD], t[:, HEAD:2 * HEAD]]
    return out


def _pair_to_tokens(a, b):
    return jnp.concatenate([a, b], axis=0).T


def _dil_schedule(body_first, body_next):
    for p, d in enumerate(DILATIONS):
        nblk = SEQ // d // BLK
        if d == 1:
            carry = body_first(p, d, 0)
            def loop(j, c, p=p, d=d):
                for u in range(3):
                    body_next(p, d, 0, 1 + 3 * j + u)
                return c
            lax.fori_loop(0, (nblk - 1) // 3, loop, 0)
        elif nblk > 1:
            def loop(r, c, p=p, d=d, nblk=nblk):
                body_first(p, d, r)
                for i in range(1, nblk):
                    body_next(p, d, r, i)
                return c
            lax.fori_loop(0, d, loop, 0)
        else:
            def loop(j, c, p=p, d=d):
                for u in range(4):
                    body_first(p, d, 4 * j + u)
                return c
            lax.fori_loop(0, d // 4, loop, 0)


def _dil_fwd(qkvb):
    B = qkvb.shape[0]

    def body(qkv_ref, o_ref):
        band, first = _band_bias(BLK)

        def block(p, d, rows_q, rows_k, bias):
            qs = _heads_of(qkv_ref, 0, rows_q, MXU)
            kks = _heads_of(qkv_ref, DIL_PAIRS_H, rows_k, MXU)
            vvs = _heads_of(qkv_ref, 2 * DIL_PAIRS_H, rows_k, MXU)
            sT = _lanes([_dot_nt(kks[h], qs[h]) for h in range(DIL_H)]) + bias
            m = jnp.max(sT, axis=0, keepdims=True)
            pT = jnp.exp(sT - m)
            l = jnp.sum(pT, axis=0, keepdims=True)
            pnT = (pT * (1.0 / l)).astype(MXU)
            lse = m + jnp.log(l)
            for c in range(DIL_PAIRS_H):
                oT = [_dot_t0(vvs[h], pnT[:, h * BLK:(h + 1) * BLK]) for h in (2 * c, 2 * c + 1)]
                ls = [jnp.broadcast_to(lse[:, h * BLK:(h + 1) * BLK], (HEAD, BLK)) for h in (2 * c, 2 * c + 1)]
                o_ref.at[p, c][rows_q, :] = _pair_to_tokens(*oT)
                o_ref.at[p, DIL_PAIRS_H + c][rows_q, :] = _pair_to_tokens(*ls)

        def body_first(p, d, r):
            rows = _stream_rows(d, r, 0, BLK)
            block(p, d, rows, rows, first)

        def body_next(p, d, r, i):
            block(p, d, _stream_rows(d, r, i, BLK), _stream_rows(d, r, i - 1, 2 * BLK), band)

        _dil_schedule(body_first, body_next)

    return pl.pallas_call(
        body, name="dil_fwd", grid=(B,),
        in_specs=[pl.BlockSpec((None, 6, SEQ, 128), lambda b: (b, 0, 0, 0))],
        out_specs=pl.BlockSpec((None, 3, 4, SEQ, 128), lambda b: (b, 0, 0, 0, 0)),
        out_shape=jax.ShapeDtypeStruct((B, 3, 4, SEQ, 128), F32),
        compiler_params=_cparams(("arbitrary",)),
    )(qkvb)


def _dil_bwd(qkvb, dobb):
    B = qkvb.shape[0]

    def body(qkv_ref, dob_ref, dq_ref):
        band, first = _band_bias(BLK)
        dq_ref[...] = jnp.zeros_like(dq_ref)

        def rowvec(base, rows):
            parts = []
            for c in range(DIL_PAIRS_H):
                t = dob_ref.at[base + c][rows, :].T
                parts += [t[0:1, :], t[HEAD:HEAD + 1, :]]
            return _lanes(parts)

        def block(p, d, rows_q, rows_k, bias):
            qs = _heads_of(qkv_ref, 0, rows_q, MXU)
            kks = _heads_of(qkv_ref, DIL_PAIRS_H, rows_k, MXU)
            vvs = _heads_of(qkv_ref, 2 * DIL_PAIRS_H, rows_k, MXU)
            dos = _heads_of(dob_ref, 0, rows_q, MXU)
            lse = rowvec(DIL_PAIRS_H, rows_q)
            delta = rowvec(2 * DIL_PAIRS_H, rows_q)
            sT = _lanes([_dot_nt(kks[h], qs[h]) for h in range(DIL_H)])
            dpT = _lanes([_dot_nt(vvs[h], dos[h]) for h in range(DIL_H)])
            pT = jnp.exp(sT + bias - lse)
            dsT = pT * (dpT - delta)
            dsb, pb = dsT.astype(MXU), pT.astype(MXU)
            for c in range(DIL_PAIRS_H):
                hs = (2 * c, 2 * c + 1)
                dq_ref.at[c][rows_q, :] += _pair_to_tokens(
                    *[_dot_t0(kks[h], dsb[:, h * BLK:(h + 1) * BLK]) for h in hs])
                dq_ref.at[DIL_PAIRS_H + c][rows_k, :] += _lanes(
                    [_dot(dsb[:, h * BLK:(h + 1) * BLK], qs[h]) for h in hs])
                dq_ref.at[2 * DIL_PAIRS_H + c][rows_k, :] += _lanes(
                    [_dot(pb[:, h * BLK:(h + 1) * BLK], dos[h]) for h in hs])

        def body_first(p, d, r):
            rows = _stream_rows(d, r, 0, BLK)
            block(p, d, rows, rows, first)

        def body_next(p, d, r, i):
            block(p, d, _stream_rows(d, r, i, BLK), _stream_rows(d, r, i - 1, 2 * BLK), band)

        _dil_schedule(body_first, body_next)

    spec = pl.BlockSpec((None, 6, SEQ, 128), lambda b: (b, 0, 0, 0))
    return pl.pallas_call(
        body, name="dil_bwd", grid=(B,), in_specs=[spec, spec], out_specs=spec,
        out_shape=jax.ShapeDtypeStruct((B, 6, SEQ, 128), F32),
        compiler_params=_cparams(("arbitrary",)),
    )(qkvb, dobb)


MEM_QROWS = 512


def _mem_attn_fwd(qc, mem, w_mem):
    B = qc.shape[0]

    def body(q_ref, mem_ref, w_ref, o_ref, mkv_ref):
        mkv_ref[...] = _dot(mem_ref[...].astype(MXU), w_ref[...]).astype(MXU)

        def loop(i, carry):
            rows = pl.ds(pl.multiple_of(i * MEM_QROWS, MEM_QROWS), MEM_QROWS)
            for h in range(MEM_H):
                sT = _dot_nt(mkv_ref[:, _col(0, h)], q_ref[rows, _col(0, h)])
                m = jnp.max(sT, axis=0, keepdims=True)
                pT = jnp.exp(sT - m)
                l = jnp.sum(pT, axis=0, keepdims=True)
                oT = _dot_t0(mkv_ref[:, _col(W_C, h)], (pT * (1.0 / l)).astype(MXU))
                _store_o_lse(o_ref, [rows], oT, m + jnp.log(l), [_col(0, h)], [_col(W_C, h)])
            return carry
        lax.fori_loop(0, SEQ // MEM_QROWS, loop, 0)

    return pl.pallas_call(
        body, name="mem_attn_fwd", grid=(B,),
        in_specs=[pl.BlockSpec((None, SEQ, W_C), lambda b: (b, 0, 0)),
                  pl.BlockSpec((None, MEM_LEN, D_MODEL), lambda b: (b, 0, 0)),
                  pl.BlockSpec((D_MODEL, 2 * W_C), lambda b: (0, 0))],
        out_specs=[pl.BlockSpec((None, SEQ, 2 * W_C), lambda b: (b, 0, 0)),
                   pl.BlockSpec((None, MEM_LEN, 2 * W_C), lambda b: (b, 0, 0))],
        out_shape=[jax.ShapeDtypeStruct((B, SEQ, 2 * W_C), F32),
                   jax.ShapeDtypeStruct((B, MEM_LEN, 2 * W_C), MXU)],
        compiler_params=_cparams(("arbitrary",)),
    )(qc, mem, w_mem)


def _mem_attn_bwd(qc, mkv, dob, mem):
    B = qc.shape[0]

    def body(q_ref, mkv_ref, dob_ref, mem_ref, dq_ref, dw_ref, dmkv_ref):
        @pl.when(pl.program_id(0) == 0)
        def _():
            dw_ref[...] = jnp.zeros_like(dw_ref)
        dmkv_ref[...] = jnp.zeros_like(dmkv_ref)

        def loop(i, carry):
            rows = pl.ds(pl.multiple_of(i * MEM_QROWS, MEM_QROWS), MEM_QROWS)
            for h0 in range(0, MEM_H, 2):
                lses = _pair_rows(dob_ref, rows, W_C, h0)
                deltas = _pair_rows(dob_ref, rows, 2 * W_C, h0)
                dqT = []
                for j in range(2):
                    h = h0 + j
                    q = q_ref[rows, _col(0, h)]
                    do = dob_ref[rows, _col(0, h)].astype(MXU)
                    mk, mv = mkv_ref[:, _col(0, h)], mkv_ref[:, _col(W_C, h)]
                    pT = jnp.exp(_dot_nt(mk, q) - lses[j])
                    dsT = pT * (_dot_nt(mv, do) - deltas[j])
                    dsb = dsT.astype(MXU)
                    dmkv_ref[:, _col(0, h)] += _dot(dsb, q)
                    dmkv_ref[:, _col(W_C, h)] += _dot(pT.astype(MXU), do)
                    dqT.append(_dot_t0(mk, dsb))
                dq_ref[rows, slice(h0 * HEAD, (h0 + 2) * HEAD)] = jnp.concatenate(dqT, axis=0).T
            return carry
        lax.fori_loop(0, SEQ // MEM_QROWS, loop, 0)
        dw_ref[...] += _dot_tn(mem_ref[...], dmkv_ref[...].astype(MXU))

    return pl.pallas_call(
        body, name="mem_attn_bwd", grid=(B,),
        in_specs=[pl.BlockSpec((None, SEQ, W_C), lambda b: (b, 0, 0)),
                  pl.BlockSpec((None, MEM_LEN, 2 * W_C), lambda b: (b, 0, 0)),
                  pl.BlockSpec((None, SEQ, 3 * W_C), lambda b: (b, 0, 0)),
                  pl.BlockSpec((None, MEM_LEN, D_MODEL), lambda b: (b, 0, 0))],
        out_specs=[pl.BlockSpec((None, SEQ, W_C), lambda b: (b, 0, 0)),
                   pl.BlockSpec((D_MODEL, 2 * W_C), lambda b: (0, 0))],
        out_shape=[jax.ShapeDtypeStruct((B, SEQ, W_C), F32),
                   jax.ShapeDtypeStruct((D_MODEL, 2 * W_C), F32)],
        scratch_shapes=[pltpu.VMEM((MEM_LEN, 2 * W_C), F32)],
        compiler_params=_cparams(("arbitrary",)),
    )(qc, mkv, dob, mem)


def _headsum(t, e):
    if MXU == F32:
        return _dot(t, e)
    hi = t.astype(MXU)
    lo = (t - hi.astype(F32)).astype(MXU)
    return _dot(hi, e) + _dot(lo, e)


def _post(olse_a, olse_b, olse_c, z, x2, tgt, g, gain, bias, w_out, hsum):
    T = x2.shape[0]
    tm = 256
    nt = SEQ // tm

    def body(oa_ref, ob_ref, oc_ref, z_ref, x_ref, t_ref, g_ref, gain_ref, bias_ref, w_ref,
             e_ref, gx_ref, doba_ref, dobb_ref, dobc_ref, dz_ref, dw_ref, small_ref, loss_ref):
        @pl.when(pl.program_id(0) == 0)
        def _():
            dw_ref[...] = jnp.zeros_like(dw_ref)
            small_ref[...] = jnp.zeros_like(small_ref)
            loss_ref[...] = jnp.zeros_like(loss_ref)

        oa = oa_ref[:, 0:W_A]
        lse_a = oa_ref[:, W_A:2 * W_A]
        (o1, l1), (o4, l4), (o16, l16) = [
            (_lanes([ob_ref[p, 0], ob_ref[p, 1]]), _lanes([ob_ref[p, 2], ob_ref[p, 3]])) for p in range(3)]
        mx = jnp.maximum(jnp.maximum(l1, l4), l16)
        e1, e4, e16 = jnp.exp(l1 - mx), jnp.exp(l4 - mx), jnp.exp(l16 - mx)
        den = e1 + e4 + e16
        ob = (e1 * o1 + e4 * o4 + e16 * o16) / den
        lse_b = mx + jnp.log(den)
        oc = oc_ref[:, 0:W_C]
        lse_c = oc_ref[:, W_C:2 * W_C]

        def rms(o):
            rr = lax.rsqrt(jnp.mean(o * o, axis=1, keepdims=True) + RMS_EPS)
            return o * rr, rr

        na, ra = rms(oa)
        nb, rb = rms(ob)
        nc, rc = rms(oc)
        n = jnp.concatenate([na, nb, nc], axis=1)
        zz = z_ref[...]
        sig = 1.0 / (1.0 + jnp.exp(-zz))
        sz = zz * sig
        gg = g_ref[...]
        u = n * gg * sz
        ub = u.astype(MXU)
        w = w_ref[...]
        r = ALPHA * x_ref[...] + _dot(ub, w)
        mu = jnp.mean(r, axis=1, keepdims=True)
        rc0 = r - mu
        rstd = lax.rsqrt(jnp.mean(rc0 * rc0, axis=1, keepdims=True) + LN_EPS)
        xhat = rc0 * rstd
        gain_v = gain_ref[...]
        err = xhat * gain_v + bias_ref[...] - t_ref[...]
        loss_ref[...] += 0.5 * jnp.sum(err * err) * (1.0 / D_MODEL)

        dout = err * (1.0 / D_MODEL)
        small_ref[0:1, :] += jnp.sum(dout * xhat, axis=0, keepdims=True)
        small_ref[1:2, :] += jnp.sum(dout, axis=0, keepdims=True)
        dxh = dout * gain_v
        dr = rstd * (dxh - jnp.mean(dxh, axis=1, keepdims=True)
                     - xhat * jnp.mean(dxh * xhat, axis=1, keepdims=True))
        gx_ref[...] = ALPHA * dr
        drb = dr.astype(MXU)
        du = _dot_nt(drb, w)
        dw_ref[...] += _dot_tn(u, drb)
        small_ref[2:3, :] += jnp.sum(du * n * sz, axis=0, keepdims=True)
        dz = du * n * gg * (sig * (1.0 + zz * (1.0 - sig)))
        dz_ref[...] = dz.astype(MXU)
        small_ref[3:4, :] += jnp.sum(dz, axis=0, keepdims=True)
        dn = du * gg * sz

        def branch(lo, hi, o, nbr, rr, lse, out_ref):
            wd = hi - lo
            dnb = dn[:, lo:hi]
            dob = rr * (dnb - nbr * jnp.mean(dnb * nbr, axis=1, keepdims=True))
            parts = (dob, lse, _headsum(dob * o, e_ref[0:wd, 0:wd]))
            if len(out_ref.shape) == 3:
                for j, t in enumerate(parts):
                    for c in range(wd // 128):
                        out_ref[j * (wd // 128) + c] = t[:, c * 128:(c + 1) * 128]
            else:
                for j, t in enumerate(parts):
                    out_ref[:, j * wd:(j + 1) * wd] = t

        branch(0, W_A, oa, na, ra, lse_a, doba_ref)
        branch(W_A, W_A + W_B, ob, nb, rb, lse_b, dobb_ref)
        branch(W_A + W_B, D_MIX, oc, nc, rc, lse_c, dobc_ref)

    row = lambda w: pl.BlockSpec((tm, w), lambda i: (i, 0))
    full = lambda a, b: pl.BlockSpec((a, b), lambda i: (0, 0))
    return pl.pallas_call(
        body, name="post_fwd_bwd", grid=(T // tm,),
        in_specs=[row(2 * W_A), pl.BlockSpec((None, 3, 4, tm, 128), lambda i: (i // nt, 0, 0, i % nt, 0)),
                  row(2 * W_C),
                  row(D_MIX), row(D_MODEL), row(D_MODEL),
                  full(1, D_MIX), full(1, D_MODEL), full(1, D_MODEL), full(D_MIX, D_MODEL), full(W_A, W_A)],
        out_specs=[row(D_MODEL), row(3 * W_A), pl.BlockSpec((None, 6, tm, 128), lambda i: (i // nt, 0, i % nt, 0)),
                   row(3 * W_C), row(D_MIX),
                   full(D_MIX, D_MODEL), full(8, D_MODEL), full(8, 128)],
        out_shape=[jax.ShapeDtypeStruct((T, D_MODEL), F32),
                   jax.ShapeDtypeStruct((T, 3 * W_A), F32),
                   jax.ShapeDtypeStruct((T // SEQ, 6, SEQ, 128), F32),
                   jax.ShapeDtypeStruct((T, 3 * W_C), F32),
                   jax.ShapeDtypeStruct((T, D_MIX), MXU),
                   jax.ShapeDtypeStruct((D_MIX, D_MODEL), F32),
                   jax.ShapeDtypeStruct((8, D_MODEL), F32),
                   jax.ShapeDtypeStruct((8, 128), F32)],
        compiler_params=_cparams(("arbitrary",)),
    )(olse_a, olse_b, olse_c, z, x2, tgt, g, gain, bias, w_out, hsum)


def _in_proj_bwd_dx(dqkva, dqkvb, dqc, dz, gx1, w_in, tab):
    T = gx1.shape[0]
    tm = 256
    nt = SEQ // tm
    HQ = D_IN - D_MIX

    def body(da_ref, db6_ref, dqc_ref, dz_ref, gx_ref, w_ref, tab_ref, dx_ref, dh_ref, db_ref):
        @pl.when(pl.program_id(0) == 0)
        def _():
            db_ref[...] = jnp.zeros_like(db_ref)
        tab = tab_ref[...]
        parts = [_rope(da_ref[:, 0:512], tab, -1) * Q_SCALE,
                 _rope(da_ref[:, 512:640], tab, -1),
                 da_ref[:, 640:768],
                 _rope(_lanes([db6_ref[0], db6_ref[1]]), tab, -1) * Q_SCALE,
                 _rope(_lanes([db6_ref[2], db6_ref[3]]), tab, -1),
                 _lanes([db6_ref[4], db6_ref[5]]),
                 dqc_ref[...] * Q_SCALE]
        dhq = jnp.concatenate(parts, axis=1)
        db_ref[0:1, :] += jnp.sum(dhq, axis=0, keepdims=True)
        dhq_b = dhq.astype(MXU)
        dzb = dz_ref[...]
        dh_ref[:, 0:HQ] = dhq_b
        dh_ref[:, HQ:D_IN] = dzb
        dx = _dot_nt(dhq_b, w_ref[:, 0:HQ]) + _dot_nt(dzb, w_ref[:, HQ:D_IN])
        dx_ref[...] = dx + gx_ref[...]

    row = lambda w: pl.BlockSpec((tm, w), lambda i: (i, 0))
    return pl.pallas_call(
        body, name="in_proj_bwd_dx", grid=(T // tm,),
        in_specs=[row(768), pl.BlockSpec((None, 6, tm, 128), lambda i: (i // nt, 0, i % nt, 0)),
                  row(256), row(D_MIX), row(D_MODEL),
                  pl.BlockSpec((D_MODEL, D_IN), lambda i: (0, 0)),
                  pl.BlockSpec((tm, 384), lambda i: (i % nt, 0))],
        out_specs=[row(D_MODEL), row(D_IN), pl.BlockSpec((8, HQ), lambda i: (0, 0))],
        out_shape=[jax.ShapeDtypeStruct((T, D_MODEL), F32), jax.ShapeDtypeStruct((T, D_IN), MXU),
                   jax.ShapeDtypeStruct((8, HQ), F32)],
        compiler_params=_cparams(("arbitrary",)),
    )(dqkva, dqkvb, dqc, dz, gx1, w_in, tab)


def _in_proj_bwd_dw(x2, dh):
    T = x2.shape[0]
    tk = 512
    tn = 1408

    def body(x_ref, dh_ref, dw_ref):
        @pl.when(pl.program_id(1) == 0)
        def _():
            dw_ref[...] = jnp.zeros_like(dw_ref)
        dw_ref[...] += _dot_tn(x_ref[...], dh_ref[...])

    return pl.pallas_call(
        body, name="in_proj_bwd_dw", grid=(D_IN // tn, T // tk),
        in_specs=[pl.BlockSpec((tk, D_MODEL), lambda j, k: (k, 0)),
                  pl.BlockSpec((tk, tn), lambda j, k: (k, j))],
        out_specs=pl.BlockSpec((D_MODEL, tn), lambda j, k: (0, j)),
        out_shape=jax.ShapeDtypeStruct((D_MODEL, D_IN), F32),
        compiler_params=_cparams(("arbitrary", "arbitrary")),
    )(x2, dh)


def _local_step(x, mem, w_in, w_mem, w_out, b_in, sinks, g, gain, bias, tgt):
    B = x.shape[0]
    T = B * SEQ
    x2 = x.reshape(T, D_MODEL)
    t2 = tgt.reshape(T, D_MODEL)
    tab = _rope_table()
    lane = jnp.arange(W_A)
    hsum = (lane[:, None] // HEAD == lane[None, :] // HEAD).astype(MXU)

    qkva, qkvb, qc, z = _in_proj(x2, w_in, b_in, tab)

    def per_ex(a):
        return a.reshape(B, SEQ, a.shape[-1])

    def flat(a):
        return a.reshape(T, a.shape[-1])

    qkva3, qc3 = per_ex(qkva), per_ex(qc)
    olse_a = _attn_fwd(qkva3, sinks, d=1, spg=1, unroll=1, hq=SWA_Q, hkv=SWA_KV, max_dist=BLK - 1,
                       name="swa_fwd")
    olse_b = _dil_fwd(qkvb)
    olse_c, mkv = _mem_attn_fwd(qc3, mem, w_mem)

    gx1, doba, dobb, dobc, dz, dw_out, small, loss = _post(
        flat(olse_a), olse_b, flat(olse_c), z, x2, t2, g, gain, bias, w_out, hsum)

    dqkva, dsink = _attn_bwd(qkva3, per_ex(doba), sinks, d=1, spg=1, unroll=1, hq=SWA_Q, hkv=SWA_KV,
                             max_dist=BLK - 1, name="swa_bwd")
    dqkvb = _dil_bwd(qkvb, dobb)
    dqc, dw_mem = _mem_attn_bwd(qc3, mkv, per_ex(dobc), mem)

    grad_x, dh, dbq = _in_proj_bwd_dx(flat(dqkva), dqkvb, flat(dqc), dz, gx1, w_in, tab)
    dw_in = _in_proj_bwd_dw(x2, dh)

    db_in = jnp.concatenate([dbq[0], small[3]])
    grads_small = dict(b_in=db_in, sinks=dsink[:, 0], g=small[2], gain=small[0], bias=small[1])
    return loss[0, 0], grad_x.reshape(B, SEQ, D_MODEL), dw_in, dw_mem, dw_out, grads_small


def _my_pos():
    return lax.axis_index("x"), lax.axis_index("y"), lax.axis_index("c")


def _gather_weights(w_in_s, w_mem_s, w_out_s):
    shards = (w_in_s, w_mem_s, w_out_s)
    n_arr = len(shards)

    def body(*refs):
        ins, outs = refs[0:n_arr], refs[n_arr:2 * n_arr]
        send_sems, recv_sems, local_sems = refs[2 * n_arr:]
        x, y, c = _my_pos()
        me, sibling = (x, y, c), (x, y, 1 - c)
        chips = [(1 - x, y), (x, 1 - y), (1 - x, 1 - y)]

        def slot(a, pos):
            return outs[a].at[4 * pos[0] + 2 * pos[1] + pos[2]]

        def copy(a, k, block, to, src=None):
            return pltpu.make_async_remote_copy(
                src_ref=slot(a, block) if src is None else src, dst_ref=slot(a, block),
                send_sem=send_sems.at[a, k], recv_sem=recv_sems.at[a, k],
                device_id=to, device_id_type=MESH)

        mine = [pltpu.make_async_copy(ins[a], slot(a, me), local_sems.at[a]) for a in range(n_arr)]
        for cp in mine:
            cp.start()
        first = []
        for a in range(n_arr):
            first.append(copy(a, 0, me, sibling, src=ins[a]))
            first += [copy(a, 1 + j, me, (*chip, c), src=ins[a]) for j, chip in enumerate(chips)]
        for cp in first:
            cp.start()
        passed = []
        for j, chip in enumerate(chips):
            for a in range(n_arr):
                copy(a, 1 + j, (*chip, c), me).wait_recv()
                fwd = copy(a, 4 + j, (*chip, c), sibling)
                fwd.start()
                passed.append(fwd)
        for a in range(n_arr):
            copy(a, 0, sibling, me).wait_recv()
            for j, chip in enumerate(chips):
                copy(a, 4 + j, (*chip, 1 - c), me).wait_recv()
        for cp in first + passed:
            cp.wait_send()
        for cp in mine:
            cp.wait()

    any_spec = pl.BlockSpec(memory_space=pl.ANY)
    return pl.pallas_call(
        body, name="gather_weights",
        in_specs=[any_spec] * n_arr, out_specs=[any_spec] * n_arr,
        out_shape=[jax.ShapeDtypeStruct((N_DEV,) + s.shape, s.dtype) for s in shards],
        scratch_shapes=[pltpu.SemaphoreType.DMA((n_arr, 7)), pltpu.SemaphoreType.DMA((n_arr, 7)),
                        pltpu.SemaphoreType.DMA((n_arr,))],
    )(*shards)


def _adamw(w, g, m, v):
    m = ADAM_B1 * m + (1.0 - ADAM_B1) * g
    v = ADAM_B2 * v + (1.0 - ADAM_B2) * (g * g)
    m_hat = m / (1.0 - ADAM_B1 ** ADAM_STEP)
    v_hat = v / (1.0 - ADAM_B2 ** ADAM_STEP)
    delta = -ADAM_LR * (m_hat / (jnp.sqrt(v_hat) + ADAM_EPS) + ADAM_WD * w)
    return delta, m, v


def _reduce_adam(sends, owns, small_g, params):
    n_big = 3
    shapes = [o.shape for o in owns] + [small_g.shape]
    chunk_rows = [128, 128, 128, SMALL_ROWS]

    def body(*refs):
        it = iter(refs)
        send_refs = [next(it) for _ in range(n_big)]
        own_refs = [next(it) for _ in range(n_big)]
        sg_ref = next(it)
        wmv_refs = [[next(it) for _ in range(3)] for _ in range(n_big + 1)]
        out_refs = [[next(it) for _ in range(4)] for _ in range(n_big + 1)]
        land_refs = [next(it) for _ in range(n_big + 1)]
        send_sems, recv_sems = next(it), next(it)

        x, y, c = _my_pos()
        me = 4 * x + 2 * y + c
        copies = []
        for k in range(1, N_DEV):
            px = 1 - x if k & 4 else x
            py = 1 - y if k & 2 else y
            pc = 1 - c if k & 1 else c
            peer = 4 * px + 2 * py + pc
            for a in range(n_big + 1):
                src = send_refs[a].at[peer] if a < n_big else sg_ref
                cp = pltpu.make_async_remote_copy(
                    src_ref=src, dst_ref=land_refs[a].at[me],
                    send_sem=send_sems.at[a, k - 1], recv_sem=recv_sems.at[a, k - 1],
                    device_id=(px, py, pc), device_id_type=MESH)
                cp.start()
                copies.append(cp)
        for a in range(n_big):
            land_refs[a][me] = jnp.zeros(shapes[a], land_refs[a].dtype)
        land_refs[n_big][me] = sg_ref[...]
        for cp in copies:
            cp.wait_recv()
        for cp in copies:
            cp.wait_send()

        for a in range(n_big + 1):
            rows, cr = shapes[a][0], chunk_rows[a]
            w_ref, m_ref, v_ref = wmv_refs[a]
            g_ref, d_ref, nm_ref, nv_ref = out_refs[a]
            land = land_refs[a]

            def chunk(i, carry, a=a, cr=cr, w_ref=w_ref, m_ref=m_ref, v_ref=v_ref, g_ref=g_ref,
                      d_ref=d_ref, nm_ref=nm_ref, nv_ref=nv_ref, land=land):
                rs = pl.ds(pl.multiple_of(i * cr, cr), cr)
                if a < n_big:
                    gsum = own_refs[a][rs, :]
                    for j in range(N_DEV):
                        gsum = gsum + land[j, rs, :].astype(F32)
                else:
                    gsum = land[0, rs, :]
                    for j in range(1, N_DEV):
                        gsum = gsum + land[j, rs, :]
                delta, nm, nv = _adamw(w_ref[rs, :], gsum, m_ref[rs, :], v_ref[rs, :])
                g_ref[rs, :] = gsum
                d_ref[rs, :] = delta
                nm_ref[rs, :] = nm
                nv_ref[rs, :] = nv
                return carry
            lax.fori_loop(0, rows // cr, chunk, 0)

    any_spec = pl.BlockSpec(memory_space=pl.ANY)
    vmem = pl.BlockSpec(memory_space=pltpu.VMEM)
    flat_params = [p for grp in params for p in grp]
    out_shape = [jax.ShapeDtypeStruct(s, F32) for s in shapes for _ in range(4)]
    scratch = [pltpu.VMEM((N_DEV,) + shapes[a], sends[a].dtype) for a in range(n_big)]
    scratch.append(pltpu.VMEM((N_DEV,) + shapes[n_big], F32))
    scratch += [pltpu.SemaphoreType.DMA((n_big + 1, N_DEV - 1)), pltpu.SemaphoreType.DMA((n_big + 1, N_DEV - 1))]
    outs = pl.pallas_call(
        body, name="reduce_adam",
        in_specs=[any_spec] * n_big + [vmem] * (n_big + 1 + len(flat_params)),
        out_specs=[vmem] * len(out_shape), out_shape=out_shape, scratch_shapes=scratch,
        compiler_params=pltpu.CompilerParams(vmem_limit_bytes=VMEM_LIMIT),
    )(*sends, *owns, small_g, *flat_params)
    return [outs[4 * a:4 * a + 4] for a in range(n_big + 1)]


_SMALL_SIZES = (("b_in", D_IN), ("g", D_MIX), ("gain", D_MODEL), ("bias", D_MODEL), ("sinks", SWA_Q))


def _pack_small(d):
    flat = jnp.concatenate([d[k].reshape(-1).astype(F32) for k, _ in _SMALL_SIZES])
    flat = jnp.pad(flat, (0, SMALL_ROWS * 128 - flat.shape[0]))
    return flat.reshape(SMALL_ROWS, 128)


def _unpack_small(p):
    flat = p.reshape(-1)
    out, off = {}, 0
    for k, n in _SMALL_SIZES:
        out[k] = flat[off:off + n].reshape(1, n)
        off += n
    return out


def kernel(x, mem, w_in, b_in, w_mem, attn_sinks, g_branch, w_out, ln_gain, ln_bias, loss_target, m_w_in, m_b_in, m_w_mem, m_attn_sinks, m_g_branch, m_w_out, m_ln_gain, m_ln_bias, v_w_in, v_b_in, v_w_mem, v_attn_sinks, v_g_branch, v_w_out, v_ln_gain, v_ln_bias):
    xi, yi, ci = _my_pos()
    me = 4 * xi + 2 * yi + ci

    w_in_all, w_mem_all, w_out_all = _gather_weights(
        w_in[0].astype(MXU), w_mem[0].astype(MXU), w_out[0].astype(MXU))
    w_in_f = w_in_all.transpose(1, 0, 2).reshape(D_MODEL, D_IN)
    w_mem_f = w_mem_all.reshape(D_MODEL, 2 * W_C)
    w_out_f = w_out_all.reshape(D_MIX, D_MODEL)

    loss, grad_x, dw_in, dw_mem, dw_out, gsmall = _local_step(
        x, mem, w_in_f, w_mem_f, w_out_f, b_in, attn_sinks[0], g_branch, ln_gain, ln_bias, loss_target)
    loss = lax.psum(loss, ("x", "y", "c"))

    dw_in_blocks = dw_in.reshape(D_MODEL, N_DEV, COLS_PER_DEV).transpose(1, 0, 2)
    dw_mem_blocks = dw_mem.reshape(N_DEV, ROWS_PER_DEV, 2 * W_C)
    dw_out_blocks = dw_out.reshape(N_DEV, ROWS_PER_DEV, D_MODEL)
    blocks = (dw_in_blocks, dw_mem_blocks, dw_out_blocks)
    sends = [b.astype(MXU) for b in blocks]
    owns = [lax.dynamic_index_in_dim(b, me, axis=0, keepdims=False) for b in blocks]
    small_w = _pack_small(dict(b_in=b_in, g=g_branch, gain=ln_gain, bias=ln_bias, sinks=attn_sinks))
    small_m = _pack_small(dict(b_in=m_b_in, g=m_g_branch, gain=m_ln_gain, bias=m_ln_bias, sinks=m_attn_sinks))
    small_v = _pack_small(dict(b_in=v_b_in, g=v_g_branch, gain=v_ln_gain, bias=v_ln_bias, sinks=v_attn_sinks))
    params = [(w_in[0], m_w_in[0], v_w_in[0]), (w_mem[0], m_w_mem[0], v_w_mem[0]),
              (w_out[0], m_w_out[0], v_w_out[0]), (small_w, small_m, small_v)]
    res = _reduce_adam(sends, owns, _pack_small(gsmall), params)

    big = [[r[None] for r in res[a]] for a in range(3)]
    sm = [_unpack_small(r) for r in res[3]]

    def group(i):
        return (big[0][i], sm[i]["b_in"], big[1][i], sm[i]["sinks"], sm[i]["g"], big[2][i],
                sm[i]["gain"], sm[i]["bias"])

    return (loss, grad_x, *group(0), *group(1), *group(2), *group(3))
```

```python
import functools
import math

import jax
import jax.numpy as jnp
from jax import lax
from jax.experimental import pallas as pl
from jax.experimental.pallas import tpu as pltpu

F32 = jnp.float32
MXU = jnp.bfloat16

D_MODEL = 1024
SEQ = 2048
HEAD = 64
BLK = 128
SWA_Q, SWA_KV = 8, 2
DIL_H = 4
MEM_H = 4
MEM_LEN = 256
W_A, W_KVA, W_B, W_C = 512, 128, 256, 256
D_MIX = 1024
D_IN = 2816
N_DEV = 8
COLS_PER_DEV = D_IN // N_DEV
ROWS_PER_DEV = D_MODEL // N_DEV
ROPE_THETA = 10000.0
LN_EPS = 1e-5
RMS_EPS = 1e-6
ALPHA = 2.0 ** 0.25
Q_SCALE = HEAD ** -0.5
NEG = -1e30
SMALL_ROWS = 48
VMEM_LIMIT = 56 * 1024 * 1024

ADAM_LR = 0.001
ADAM_B1 = 0.9
ADAM_B2 = 0.999
ADAM_EPS = 1e-08
ADAM_WD = 0.01
ADAM_STEP = 10

MESH = pl.DeviceIdType.MESH


def _cparams(sem=None):
    return pltpu.CompilerParams(dimension_semantics=sem, vmem_limit_bytes=VMEM_LIMIT)


def _dot(a, b):
    return jnp.dot(a, b, preferred_element_type=F32)


def _dot_nt(a, b):
    return lax.dot_general(a, b, (((1,), (1,)), ((), ())), preferred_element_type=F32)


def _dot_t0(a, b):
    return lax.dot_general(a, b, (((0,), (0,)), ((), ())), preferred_element_type=F32)


def _dot_tn(a, b):
    return jnp.dot(a.T.astype(MXU), b, preferred_element_type=F32)


def _rope(t, tab, sign):
    cos, sa, sb = tab[:, 0:128], tab[:, 128:256], tab[:, 256:384]
    outs = []
    for c in range(t.shape[1] // 128):
        tc = t[:, c * 128:(c + 1) * 128]
        r = pltpu.roll(tc, 96, 1) * sa + pltpu.roll(tc, 32, 1) * sb
        outs.append(tc * cos + r if sign > 0 else tc * cos - r)
    return outs[0] if len(outs) == 1 else jnp.concatenate(outs, axis=1)


def _rope_table():
    pos = jnp.arange(SEQ, dtype=F32)
    inv = ROPE_THETA ** (-jnp.arange(0, HEAD, 2, dtype=F32) / HEAD)
    ang = pos[:, None] * inv[None, :]
    ang = jnp.concatenate([ang, ang], axis=-1)
    cos, sin = jnp.cos(ang), jnp.sin(ang)
    lane = jnp.arange(HEAD)[None, :]
    sa = jnp.where(lane < HEAD // 2, -sin, 0.0)
    sb = jnp.where(lane >= HEAD // 2, sin, 0.0)
    two = lambda t: jnp.concatenate([t, t], axis=-1)
    return jnp.concatenate([two(cos), two(sa), two(sb)], axis=-1).astype(F32)


def _dev_coords(j):
    return (j >> 2, (j >> 1) & 1, j & 1)


def _in_proj(x2, w_all, b_in, tab, late_shards):
    T = x2.shape[0]
    tm = 512
    n_late = len(late_shards)

    def body(x_ref, wall_ref, b_ref, tab_ref, *rest):
        late_in, rest = rest[:n_late], rest[n_late:]
        qkva_ref, qkvb_ref, qc_ref, z_ref, w_ref = rest[:5]
        late_out = rest[5:5 + n_late]
        send_sems, recv_sems, local_sems = rest[5 + n_late:]
        step, last = pl.program_id(0), pl.num_programs(0) - 1
        x, y, c = _my_pos()
        me = 4 * x + 2 * y + c

        def to_peer(a, j):
            return pltpu.make_async_remote_copy(
                src_ref=late_in[a], dst_ref=late_out[a].at[me], send_sem=send_sems.at[a, j],
                recv_sem=recv_sems.at[a, me], device_id=_dev_coords(j), device_id_type=MESH)

        def from_peer(a, m):
            return pltpu.make_async_remote_copy(
                src_ref=late_out[a].at[m], dst_ref=late_out[a].at[m], send_sem=send_sems.at[a, m],
                recv_sem=recv_sems.at[a, m], device_id=_dev_coords(m), device_id_type=MESH)

        def mine(a):
            return pltpu.make_async_copy(late_in[a], late_out[a].at[me], local_sems.at[a])

        @pl.when(step == 0)
        def _():
            for a in range(n_late):
                mine(a).start()
                for j in range(N_DEV):
                    pl.when(me != j)(to_peer(a, j).start)
            for j in range(N_DEV):
                w_ref[:, j * COLS_PER_DEV:(j + 1) * COLS_PER_DEV] = wall_ref[j]

        xb = x_ref[...].astype(MXU)
        tab = tab_ref[...]

        def seg(c0, c1):
            return _dot(xb, w_ref[:, c0:c1]) + b_ref[:, c0:c1]

        qkva_ref[:, 0:512] = (_rope(seg(0, 512), tab, 1) * Q_SCALE).astype(MXU)
        qkva_ref[:, 512:640] = _rope(seg(512, 640), tab, 1).astype(MXU)
        qkva_ref[:, 640:768] = seg(640, 768).astype(MXU)
        qkvb = (_rope(seg(768, 1024), tab, 1) * Q_SCALE, _rope(seg(1024, 1280), tab, 1), seg(1280, 1536))
        for j, t in enumerate(qkvb):
            for c in range(2):
                qkvb_ref[2 * j + c] = t[:, c * 128:(c + 1) * 128]
        qc_ref[...] = (seg(1536, 1792) * Q_SCALE).astype(MXU)
        z_ref[...] = seg(1792, 2816)

        @pl.when(step == last)
        def _():
            for a in range(n_late):
                mine(a).wait()
                for m in range(N_DEV):
                    pl.when(me != m)(from_peer(a, m).wait_recv)
                for j in range(N_DEV):
                    pl.when(me != j)(to_peer(a, j).wait_send)

    nt = SEQ // tm
    any_spec = pl.BlockSpec(memory_space=pl.ANY)
    return pl.pallas_call(
        body, name="in_proj_fwd",
        grid=(T // tm,),
        in_specs=[pl.BlockSpec((tm, D_MODEL), lambda i: (i, 0)),
                  pl.BlockSpec((N_DEV, D_MODEL, COLS_PER_DEV), lambda i: (0, 0, 0)),
                  pl.BlockSpec((1, D_IN), lambda i: (0, 0)),
                  pl.BlockSpec((tm, 384), lambda i: (i % nt, 0))] + [any_spec] * n_late,
        out_specs=[pl.BlockSpec((tm, 768), lambda i: (i, 0)),
                   pl.BlockSpec((None, 6, tm, 128), lambda i: (i // nt, 0, i % nt, 0)),
                   pl.BlockSpec((tm, 256), lambda i: (i, 0)),
                   pl.BlockSpec((tm, D_MIX), lambda i: (i, 0)),
                   pl.BlockSpec((D_MODEL, D_IN), lambda i: (0, 0))] + [any_spec] * n_late,
        out_shape=[jax.ShapeDtypeStruct((T, 768), MXU), jax.ShapeDtypeStruct((T // SEQ, 6, SEQ, 128), F32),
                   jax.ShapeDtypeStruct((T, 256), MXU), jax.ShapeDtypeStruct((T, D_MIX), F32),
                   jax.ShapeDtypeStruct((D_MODEL, D_IN), w_all.dtype)]
        + [jax.ShapeDtypeStruct((N_DEV,) + s.shape, s.dtype) for s in late_shards],
        scratch_shapes=[pltpu.SemaphoreType.DMA((n_late, N_DEV)), pltpu.SemaphoreType.DMA((n_late, N_DEV)),
                        pltpu.SemaphoreType.DMA((n_late,))],
        compiler_params=_cparams(("arbitrary",)),
    )(x2, w_all, b_in, tab, *late_shards)


CHAIN = 4


def _band_bias(max_dist):
    kj = lax.broadcasted_iota(jnp.int32, (2 * BLK, BLK), 0)
    qi = lax.broadcasted_iota(jnp.int32, (2 * BLK, BLK), 1)
    dist = qi + BLK - kj
    band = jnp.where((dist >= 0) & (dist <= max_dist), 0.0, NEG).astype(F32)
    k1 = lax.broadcasted_iota(jnp.int32, (BLK, BLK), 0)
    q1 = lax.broadcasted_iota(jnp.int32, (BLK, BLK), 1)
    first = jnp.where((q1 - k1 >= 0) & (q1 - k1 <= max_dist), 0.0, NEG).astype(F32)
    return jnp.concatenate([band] * CHAIN, axis=1), jnp.concatenate([first] * CHAIN, axis=1)


def _lanes(parts):
    return jnp.concatenate(parts, axis=1)


def _col(off, h):
    return slice(off + h * HEAD, off + (h + 1) * HEAD)


def _pair_rows(ref, rows, off, h0):
    t = ref[rows, slice(off + h0 * HEAD, off + (h0 + 2) * HEAD)].T
    return t[0:1, :], t[HEAD:HEAD + 1, :]


def _store_o_lse(o_ref, rows, oT, lse, ocols, lcols):
    n = len(ocols)
    res = jnp.concatenate([oT, jnp.broadcast_to(lse, (HEAD, lse.shape[1]))], axis=0).T
    rq = res.shape[0] // n
    for i in range(n):
        o_ref[rows[i], ocols[i]] = res[i * rq:(i + 1) * rq, 0:HEAD]
        o_ref[rows[i], lcols[i]] = res[i * rq:(i + 1) * rq, HEAD:2 * HEAD]


def _attn_geometry(qkv, d, spg, hq, hkv):
    B, Ls, wall = qkv.shape
    qw, kw = hq * HEAD, hkv * HEAD
    wtot = qw + 2 * kw
    assert wall == d * wtot and Ls % BLK == 0 and d % spg == 0 and hq % CHAIN == 0
    G = hq // hkv
    assert G in (1, CHAIN)
    return B, Ls, qw, kw, wtot, G, Ls // BLK


def _attn_fwd(qkv, sinks, *, d, spg, unroll, hq, hkv, max_dist, name):
    B, Ls, qw, kw, wtot, G, nblk = _attn_geometry(qkv, d, spg, hq, hkv)
    use_sink = sinks is not None
    assert (nblk - 1) % unroll == 0
    trips = (nblk - 1) // unroll

    def body(*refs):
        if use_sink:
            sink_ref, qkv_ref, o_ref = refs
        else:
            qkv_ref, o_ref = refs
        band, first = _band_bias(max_dist)
        for st in range(spg):
            ib, ob = st * wtot, st * 2 * qw
            chains = []
            for ch in range(hq // CHAIN):
                heads = list(range(ch * CHAIN, (ch + 1) * CHAIN))
                sinkrow = _lanes([jnp.full((1, BLK), sink_ref[h], F32) for h in heads]) if use_sink else None

                def block(rows_q, rows_k, bias, heads=heads, ib=ib, ob=ob, sinkrow=sinkrow):
                    qs = [qkv_ref[rows_q, _col(ib, h)] for h in heads]
                    if G > 1:
                        hk = heads[0] // G
                        kk = qkv_ref[rows_k, _col(ib + qw, hk)]
                        vv = qkv_ref[rows_k, _col(ib + qw + kw, hk)]
                        sT = _dot_nt(kk, jnp.concatenate(qs, axis=0))
                    else:
                        vvs = [qkv_ref[rows_k, _col(ib + qw + kw, h)] for h in heads]
                        sT = _lanes([_dot_nt(qkv_ref[rows_k, _col(ib + qw, h)], qs[i])
                                     for i, h in enumerate(heads)])
                    sT = sT + bias
                    m = jnp.max(sT, axis=0, keepdims=True)
                    if use_sink:
                        m = jnp.maximum(m, sinkrow)
                    pT = jnp.exp(sT - m)
                    l = jnp.sum(pT, axis=0, keepdims=True)
                    if use_sink:
                        l = l + jnp.exp(sinkrow - m)
                    pnT = (pT * (1.0 / l)).astype(MXU)
                    if G > 1:
                        oT = _dot_t0(vv, pnT)
                    else:
                        oT = _lanes([_dot_t0(vvs[i], pnT[:, i * BLK:(i + 1) * BLK]) for i in range(CHAIN)])
                    _store_o_lse(o_ref, [rows_q] * CHAIN, oT, m + jnp.log(l),
                                 [_col(ob, h) for h in heads], [_col(ob + qw, h) for h in heads])

                chains.append(block)

            for block in chains:
                block(pl.ds(0, BLK), pl.ds(0, BLK), first)

            def blocks_at(i0, chains=chains):
                for u in range(unroll):
                    r0 = pl.multiple_of((i0 + u) * BLK, BLK)
                    rk = pl.multiple_of((i0 + u) * BLK - BLK, BLK)
                    for block in chains:
                        block(pl.ds(r0, BLK), pl.ds(rk, 2 * BLK), band)

            if trips == 1:
                blocks_at(1)
            elif trips > 1:
                def loop(j, carry, blocks_at=blocks_at):
                    blocks_at(1 + j * unroll)
                    return carry
                lax.fori_loop(0, trips, loop, 0)

    in_specs = [pl.BlockSpec((None, Ls, spg * wtot), lambda b, r: (b, 0, r))]
    args = [qkv]
    if use_sink:
        in_specs = [pl.BlockSpec(memory_space=pltpu.SMEM)] + in_specs
        args = [sinks] + args
    return pl.pallas_call(
        body, name=name, grid=(B, d // spg), in_specs=in_specs,
        out_specs=pl.BlockSpec((None, Ls, spg * 2 * qw), lambda b, r: (b, 0, r)),
        out_shape=jax.ShapeDtypeStruct((B, Ls, d * 2 * qw), F32),
        compiler_params=_cparams(("arbitrary", "arbitrary")),
    )(*args)


def _attn_bwd(qkv, dob, sinks, *, d, spg, unroll, hq, hkv, max_dist, name):
    B, Ls, qw, kw, wtot, G, nblk = _attn_geometry(qkv, d, spg, hq, hkv)
    assert dob.shape == (B, Ls, d * 3 * qw) and (nblk - 1) % unroll == 0
    trips = (nblk - 1) // unroll
    use_sink = sinks is not None

    def body(*refs):
        if use_sink:
            sink_ref, qkv_ref, dob_ref, dq_ref, dsink_ref = refs
        else:
            qkv_ref, dob_ref, dq_ref = refs
        band, first = _band_bias(max_dist)
        if use_sink:
            @pl.when((pl.program_id(0) == 0) & (pl.program_id(1) == 0))
            def _():
                dsink_ref[...] = jnp.zeros_like(dsink_ref)
        for st in range(spg):
            ib, db = st * wtot, st * 3 * qw
            dq_ref[:, ib + qw:ib + wtot] = jnp.zeros((Ls, 2 * kw), F32)
            chains = []
            for ch in range(hq // CHAIN):
                heads = list(range(ch * CHAIN, (ch + 1) * CHAIN))
                sinkrow = _lanes([jnp.full((1, BLK), sink_ref[h], F32) for h in heads]) if use_sink else None

                def block(rows_q, rows_k, bias, acc, heads=heads, ib=ib, db=db, sinkrow=sinkrow):
                    qs = [qkv_ref[rows_q, _col(ib, h)] for h in heads]
                    dos = [dob_ref[rows_q, _col(db, h)].astype(MXU) for h in heads]
                    lse = _lanes([r for j in range(0, CHAIN, 2)
                                  for r in _pair_rows(dob_ref, rows_q, db + qw, heads[j])])
                    delta = _lanes([r for j in range(0, CHAIN, 2)
                                    for r in _pair_rows(dob_ref, rows_q, db + 2 * qw, heads[j])])
                    if G > 1:
                        hk = heads[0] // G
                        kc, vc = _col(ib + qw, hk), _col(ib + qw + kw, hk)
                        kk, vv = qkv_ref[rows_k, kc], qkv_ref[rows_k, vc]
                        qst, dost = jnp.concatenate(qs, axis=0), jnp.concatenate(dos, axis=0)
                        sT, dpT = _dot_nt(kk, qst), _dot_nt(vv, dost)
                    else:
                        kks = [qkv_ref[rows_k, _col(ib + qw, h)] for h in heads]
                        vvs = [qkv_ref[rows_k, _col(ib + qw + kw, h)] for h in heads]
                        sT = _lanes([_dot_nt(kks[i], qs[i]) for i in range(CHAIN)])
                        dpT = _lanes([_dot_nt(vvs[i], dos[i]) for i in range(CHAIN)])
                    pT = jnp.exp(sT + bias - lse)
                    dsT = pT * (dpT - delta)
                    dsb, pb = dsT.astype(MXU), pT.astype(MXU)
                    if G > 1:
                        dq_ref[rows_k, kc] += _dot(dsb, qst)
                        dq_ref[rows_k, vc] += _dot(pb, dost)
                        dq = _dot_t0(kk, dsb).T
                        for i, h in enumerate(heads):
                            dq_ref[rows_q, _col(ib, h)] = dq[i * BLK:(i + 1) * BLK]
                    else:
                        dqT = []
                        for i, h in enumerate(heads):
                            ls = slice(i * BLK, (i + 1) * BLK)
                            dq_ref[rows_k, _col(ib + qw, h)] += _dot(dsb[:, ls], qs[i])
                            dq_ref[rows_k, _col(ib + qw + kw, h)] += _dot(pb[:, ls], dos[i])
                            dqT.append(_dot_t0(kks[i], dsb[:, ls]))
                        dq = _lanes(dqT).T
                        for i, h in enumerate(heads):
                            dq_ref[rows_q, _col(ib, h)] = dq[i * BLK:(i + 1) * BLK]
                    if use_sink:
                        acc = acc - jnp.exp(sinkrow - lse) * delta
                    return acc

                chains.append(block)

            accs = tuple(block(pl.ds(0, BLK), pl.ds(0, BLK), first, jnp.zeros((1, CHAIN * BLK), F32))
                         for block in chains)

            def blocks_at(i0, accs, chains=chains):
                for u in range(unroll):
                    r0 = pl.multiple_of((i0 + u) * BLK, BLK)
                    rk = pl.multiple_of((i0 + u) * BLK - BLK, BLK)
                    accs = tuple(block(pl.ds(r0, BLK), pl.ds(rk, 2 * BLK), band, acc)
                                 for block, acc in zip(chains, accs))
                return accs

            if trips == 1:
                accs = blocks_at(1, accs)
            elif trips > 1:
                def loop(j, accs, blocks_at=blocks_at):
                    return blocks_at(1 + j * unroll, accs)
                accs = lax.fori_loop(0, trips, loop, accs)
            if use_sink:
                for ch, acc in enumerate(accs):
                    for i in range(CHAIN):
                        h = ch * CHAIN + i
                        tot = jnp.sum(acc[:, i * BLK:(i + 1) * BLK], axis=1, keepdims=True)
                        dsink_ref[h:h + 1, :] += jnp.broadcast_to(tot, (1, 128))

    in_specs = [pl.BlockSpec((None, Ls, spg * wtot), lambda b, r: (b, 0, r)),
                pl.BlockSpec((None, Ls, spg * 3 * qw), lambda b, r: (b, 0, r))]
    out_specs = [pl.BlockSpec((None, Ls, spg * wtot), lambda b, r: (b, 0, r))]
    out_shape = [jax.ShapeDtypeStruct((B, Ls, d * wtot), F32)]
    args = [qkv, dob]
    if use_sink:
        in_specs = [pl.BlockSpec(memory_space=pltpu.SMEM)] + in_specs
        args = [sinks] + args
        out_specs.append(pl.BlockSpec((8, 128), lambda b, r: (0, 0)))
        out_shape.append(jax.ShapeDtypeStruct((8, 128), F32))
    return pl.pallas_call(
        body, name=name, grid=(B, d // spg), in_specs=in_specs, out_specs=out_specs, out_shape=out_shape,
        compiler_params=_cparams(("arbitrary", "arbitrary")),
    )(*args)


DILATIONS = (1, 4, 16)
DIL_PAIRS_H = DIL_H // 2


def _stream_rows(d, r, i, n):
    if d == 1:
        return pl.ds(pl.multiple_of(i * BLK, BLK), n)
    return pl.ds(r + i * (BLK * d), n, stride=d)


def _heads_of(ref, base, rows, dtype):
    out = []
    for c in range(DIL_PAIRS_H):
        t = ref.at[base + c][rows, :].astype(dtype)
        out += [t[:, 0:HEAD], t[:, HEAD:2 * HEAD]]
    return out


def _pair_to_tokens(a, b):
    return jnp.concatenate([a, b], axis=0).T


def _dil_schedule(body_first, body_next):
    for p, d in enumerate(DILATIONS):
        nblk = SEQ // d // BLK
        if d == 1:
            carry = body_first(p, d, 0)
            def loop(j, c, p=p, d=d):
                for u in range(3):
                    body_next(p, d, 0, 1 + 3 * j + u)
                return c
            lax.fori_loop(0, (nblk - 1) // 3, loop, 0)
        elif nblk > 1:
            def loop(r, c, p=p, d=d, nblk=nblk):
                body_first(p, d, r)
                for i in range(1, nblk):
                    body_next(p, d, r, i)
                return c
            lax.fori_loop(0, d, loop, 0)
        else:
            def loop(j, c, p=p, d=d):
                for u in range(4):
                    body_first(p, d, 4 * j + u)
                return c
            lax.fori_loop(0, d // 4, loop, 0)


def _dil_fwd(qkvb):
    B = qkvb.shape[0]

    def body(qkv_ref, o_ref):
        band, first = _band_bias(BLK)

        def block(p, d, rows_q, rows_k, bias):
            qs = _heads_of(qkv_ref, 0, rows_q, MXU)
            kks = _heads_of(qkv_ref, DIL_PAIRS_H, rows_k, MXU)
            vvs = _heads_of(qkv_ref, 2 * DIL_PAIRS_H, rows_k, MXU)
            sT = _lanes([_dot_nt(kks[h], qs[h]) for h in range(DIL_H)]) + bias
            m = jnp.max(sT, axis=0, keepdims=True)
            pT = jnp.exp(sT - m)
            l = jnp.sum(pT, axis=0, keepdims=True)
            pnT = (pT * (1.0 / l)).astype(MXU)
            lse = m + jnp.log(l)
            for c in range(DIL_PAIRS_H):
                oT = [_dot_t0(vvs[h], pnT[:, h * BLK:(h + 1) * BLK]) for h in (2 * c, 2 * c + 1)]
                ls = [jnp.broadcast_to(lse[:, h * BLK:(h + 1) * BLK], (HEAD, BLK)) for h in (2 * c, 2 * c + 1)]
                o_ref.at[p, c][rows_q, :] = _pair_to_tokens(*oT)
                o_ref.at[p, DIL_PAIRS_H + c][rows_q, :] = _pair_to_tokens(*ls)

        def body_first(p, d, r):
            rows = _stream_rows(d, r, 0, BLK)
            block(p, d, rows, rows, first)

        def body_next(p, d, r, i):
            block(p, d, _stream_rows(d, r, i, BLK), _stream_rows(d, r, i - 1, 2 * BLK), band)

        _dil_schedule(body_first, body_next)

    return pl.pallas_call(
        body, name="dil_fwd", grid=(B,),
        in_specs=[pl.BlockSpec((None, 6, SEQ, 128), lambda b: (b, 0, 0, 0))],
        out_specs=pl.BlockSpec((None, 3, 4, SEQ, 128), lambda b: (b, 0, 0, 0, 0)),
        out_shape=jax.ShapeDtypeStruct((B, 3, 4, SEQ, 128), F32),
        compiler_params=_cparams(("arbitrary",)),
    )(qkvb)


def _dil_bwd(qkvb, dobb):
    B = qkvb.shape[0]

    def body(qkv_ref, dob_ref, dq_ref):
        band, first = _band_bias(BLK)
        dq_ref[...] = jnp.zeros_like(dq_ref)

        def rowvec(base, rows):
            parts = []
            for c in range(DIL_PAIRS_H):
                t = dob_ref.at[base + c][rows, :].T
                parts += [t[0:1, :], t[HEAD:HEAD + 1, :]]
            return _lanes(parts)

        def block(p, d, rows_q, rows_k, bias):
            qs = _heads_of(qkv_ref, 0, rows_q, MXU)
            kks = _heads_of(qkv_ref, DIL_PAIRS_H, rows_k, MXU)
            vvs = _heads_of(qkv_ref, 2 * DIL_PAIRS_H, rows_k, MXU)
            dos = _heads_of(dob_ref, 0, rows_q, MXU)
            lse = rowvec(DIL_PAIRS_H, rows_q)
            delta = rowvec(2 * DIL_PAIRS_H, rows_q)
            sT = _lanes([_dot_nt(kks[h], qs[h]) for h in range(DIL_H)])
            dpT = _lanes([_dot_nt(vvs[h], dos[h]) for h in range(DIL_H)])
            pT = jnp.exp(sT + bias - lse)
            dsT = pT * (dpT - delta)
            dsb, pb = dsT.astype(MXU), pT.astype(MXU)
            for c in range(DIL_PAIRS_H):
                hs = (2 * c, 2 * c + 1)
                dq_ref.at[c][rows_q, :] += _pair_to_tokens(
                    *[_dot_t0(kks[h], dsb[:, h * BLK:(h + 1) * BLK]) for h in hs])
                dq_ref.at[DIL_PAIRS_H + c][rows_k, :] += _lanes(
                    [_dot(dsb[:, h * BLK:(h + 1) * BLK], qs[h]) for h in hs])
                dq_ref.at[2 * DIL_PAIRS_H + c][rows_k, :] += _lanes(
                    [_dot(pb[:, h * BLK:(h + 1) * BLK], dos[h]) for h in hs])

        def body_first(p, d, r):
            rows = _stream_rows(d, r, 0, BLK)
            block(p, d, rows, rows, first)

        def body_next(p, d, r, i):
            block(p, d, _stream_rows(d, r, i, BLK), _stream_rows(d, r, i - 1, 2 * BLK), band)

        _dil_schedule(body_first, body_next)

    spec = pl.BlockSpec((None, 6, SEQ, 128), lambda b: (b, 0, 0, 0))
    return pl.pallas_call(
        body, name="dil_bwd", grid=(B,), in_specs=[spec, spec], out_specs=spec,
        out_shape=jax.ShapeDtypeStruct((B, 6, SEQ, 128), F32),
        compiler_params=_cparams(("arbitrary",)),
    )(qkvb, dobb)


MEM_QROWS = 512


def _mem_attn_fwd(qc, mem, w_mem):
    B = qc.shape[0]

    def body(q_ref, mem_ref, w_ref, o_ref, mkv_ref):
        mkv_ref[...] = _dot(mem_ref[...].astype(MXU), w_ref[...]).astype(MXU)

        def loop(i, carry):
            rows = pl.ds(pl.multiple_of(i * MEM_QROWS, MEM_QROWS), MEM_QROWS)
            for h in range(MEM_H):
                sT = _dot_nt(mkv_ref[:, _col(0, h)], q_ref[rows, _col(0, h)])
                m = jnp.max(sT, axis=0, keepdims=True)
                pT = jnp.exp(sT - m)
                l = jnp.sum(pT, axis=0, keepdims=True)
                oT = _dot_t0(mkv_ref[:, _col(W_C, h)], (pT * (1.0 / l)).astype(MXU))
                _store_o_lse(o_ref, [rows], oT, m + jnp.log(l), [_col(0, h)], [_col(W_C, h)])
            return carry
        lax.fori_loop(0, SEQ // MEM_QROWS, loop, 0)

    return pl.pallas_call(
        body, name="mem_attn_fwd", grid=(B,),
        in_specs=[pl.BlockSpec((None, SEQ, W_C), lambda b: (b, 0, 0)),
                  pl.BlockSpec((None, MEM_LEN, D_MODEL), lambda b: (b, 0, 0)),
                  pl.BlockSpec((D_MODEL, 2 * W_C), lambda b: (0, 0))],
        out_specs=[pl.BlockSpec((None, SEQ, 2 * W_C), lambda b: (b, 0, 0)),
                   pl.BlockSpec((None, MEM_LEN, 2 * W_C), lambda b: (b, 0, 0))],
        out_shape=[jax.ShapeDtypeStruct((B, SEQ, 2 * W_C), F32),
                   jax.ShapeDtypeStruct((B, MEM_LEN, 2 * W_C), MXU)],
        compiler_params=_cparams(("arbitrary",)),
    )(qc, mem, w_mem)


def _mem_attn_bwd(qc, mkv, dob, mem):
    B = qc.shape[0]

    def body(q_ref, mkv_ref, dob_ref, mem_ref, dq_ref, dw_ref, dmkv_ref):
        @pl.when(pl.program_id(0) == 0)
        def _():
            dw_ref[...] = jnp.zeros_like(dw_ref)
        dmkv_ref[...] = jnp.zeros_like(dmkv_ref)

        def loop(i, carry):
            rows = pl.ds(pl.multiple_of(i * MEM_QROWS, MEM_QROWS), MEM_QROWS)
            for h0 in range(0, MEM_H, 2):
                lses = _pair_rows(dob_ref, rows, W_C, h0)
                deltas = _pair_rows(dob_ref, rows, 2 * W_C, h0)
                dqT = []
                for j in range(2):
                    h = h0 + j
                    q = q_ref[rows, _col(0, h)]
                    do = dob_ref[rows, _col(0, h)].astype(MXU)
                    mk, mv = mkv_ref[:, _col(0, h)], mkv_ref[:, _col(W_C, h)]
                    pT = jnp.exp(_dot_nt(mk, q) - lses[j])
                    dsT = pT * (_dot_nt(mv, do) - deltas[j])
                    dsb = dsT.astype(MXU)
                    dmkv_ref[:, _col(0, h)] += _dot(dsb, q)
                    dmkv_ref[:, _col(W_C, h)] += _dot(pT.astype(MXU), do)
                    dqT.append(_dot_t0(mk, dsb))
                dq_ref[rows, slice(h0 * HEAD, (h0 + 2) * HEAD)] = jnp.concatenate(dqT, axis=0).T
            return carry
        lax.fori_loop(0, SEQ // MEM_QROWS, loop, 0)
        dw_ref[...] += _dot_tn(mem_ref[...], dmkv_ref[...].astype(MXU))

    return pl.pallas_call(
        body, name="mem_attn_bwd", grid=(B,),
        in_specs=[pl.BlockSpec((None, SEQ, W_C), lambda b: (b, 0, 0)),
                  pl.BlockSpec((None, MEM_LEN, 2 * W_C), lambda b: (b, 0, 0)),
                  pl.BlockSpec((None, SEQ, 3 * W_C), lambda b: (b, 0, 0)),
                  pl.BlockSpec((None, MEM_LEN, D_MODEL), lambda b: (b, 0, 0))],
        out_specs=[pl.BlockSpec((None, SEQ, W_C), lambda b: (b, 0, 0)),
                   pl.BlockSpec((D_MODEL, 2 * W_C), lambda b: (0, 0))],
        out_shape=[jax.ShapeDtypeStruct((B, SEQ, W_C), F32),
                   jax.ShapeDtypeStruct((D_MODEL, 2 * W_C), F32)],
        scratch_shapes=[pltpu.VMEM((MEM_LEN, 2 * W_C), F32)],
        compiler_params=_cparams(("arbitrary",)),
    )(qc, mkv, dob, mem)


def _headsum(t, e):
    if MXU == F32:
        return _dot(t, e)
    hi = t.astype(MXU)
    lo = (t - hi.astype(F32)).astype(MXU)
    return _dot(hi, e) + _dot(lo, e)


def _post(olse_a, olse_b, olse_c, z, x2, tgt, g, gain, bias, w_out, hsum):
    T = x2.shape[0]
    tm = 256
    nt = SEQ // tm

    def body(oa_ref, ob_ref, oc_ref, z_ref, x_ref, t_ref, g_ref, gain_ref, bias_ref, w_ref,
             e_ref, gx_ref, doba_ref, dobb_ref, dobc_ref, dz_ref, dw_ref, small_ref, loss_ref):
        @pl.when(pl.program_id(0) == 0)
        def _():
            dw_ref[...] = jnp.zeros_like(dw_ref)
            small_ref[...] = jnp.zeros_like(small_ref)
            loss_ref[...] = jnp.zeros_like(loss_ref)

        oa = oa_ref[:, 0:W_A]
        lse_a = oa_ref[:, W_A:2 * W_A]
        (o1, l1), (o4, l4), (o16, l16) = [
            (_lanes([ob_ref[p, 0], ob_ref[p, 1]]), _lanes([ob_ref[p, 2], ob_ref[p, 3]])) for p in range(3)]
        mx = jnp.maximum(jnp.maximum(l1, l4), l16)
        e1, e4, e16 = jnp.exp(l1 - mx), jnp.exp(l4 - mx), jnp.exp(l16 - mx)
        den = e1 + e4 + e16
        ob = (e1 * o1 + e4 * o4 + e16 * o16) / den
        lse_b = mx + jnp.log(den)
        oc = oc_ref[:, 0:W_C]
        lse_c = oc_ref[:, W_C:2 * W_C]

        def rms(o):
            rr = lax.rsqrt(jnp.mean(o * o, axis=1, keepdims=True) + RMS_EPS)
            return o * rr, rr

        na, ra = rms(oa)
        nb, rb = rms(ob)
        nc, rc = rms(oc)
        n = jnp.concatenate([na, nb, nc], axis=1)
        zz = z_ref[...]
        sig = 1.0 / (1.0 + jnp.exp(-zz))
        sz = zz * sig
        gg = g_ref[...]
        u = n * gg * sz
        ub = u.astype(MXU)
        w = w_ref[...]
        r = ALPHA * x_ref[...] + _dot(ub, w)
        mu = jnp.mean(r, axis=1, keepdims=True)
        rc0 = r - mu
        rstd = lax.rsqrt(jnp.mean(rc0 * rc0, axis=1, keepdims=True) + LN_EPS)
        xhat = rc0 * rstd
        gain_v = gain_ref[...]
        err = xhat * gain_v + bias_ref[...] - t_ref[...]
        loss_ref[...] += 0.5 * jnp.sum(err * err) * (1.0 / D_MODEL)

        dout = err * (1.0 / D_MODEL)
        small_ref[0:1, :] += jnp.sum(dout * xhat, axis=0, keepdims=True)
        small_ref[1:2, :] += jnp.sum(dout, axis=0, keepdims=True)
        dxh = dout * gain_v
        dr = rstd * (dxh - jnp.mean(dxh, axis=1, keepdims=True)
                     - xhat * jnp.mean(dxh * xhat, axis=1, keepdims=True))
        gx_ref[...] = ALPHA * dr
        drb = dr.astype(MXU)
        du = _dot_nt(drb, w)
        dw_ref[...] += _dot_tn(u, drb)
        small_ref[2:3, :] += jnp.sum(du * n * sz, axis=0, keepdims=True)
        dz = du * n * gg * (sig * (1.0 + zz * (1.0 - sig)))
        dz_ref[...] = dz.astype(MXU)
        small_ref[3:4, :] += jnp.sum(dz, axis=0, keepdims=True)
        dn = du * gg * sz

        def branch(lo, hi, o, nbr, rr, lse, out_ref):
            wd = hi - lo
            dnb = dn[:, lo:hi]
            dob = rr * (dnb - nbr * jnp.mean(dnb * nbr, axis=1, keepdims=True))
            parts = (dob, lse, _headsum(dob * o, e_ref[0:wd, 0:wd]))
            if len(out_ref.shape) == 3:
                for j, t in enumerate(parts):
                    for c in range(wd // 128):
                        out_ref[j * (wd // 128) + c] = t[:, c * 128:(c + 1) * 128]
            else:
                for j, t in enumerate(parts):
                    out_ref[:, j * wd:(j + 1) * wd] = t

        branch(0, W_A, oa, na, ra, lse_a, doba_ref)
        branch(W_A, W_A + W_B, ob, nb, rb, lse_b, dobb_ref)
        branch(W_A + W_B, D_MIX, oc, nc, rc, lse_c, dobc_ref)

    row = lambda w: pl.BlockSpec((tm, w), lambda i: (i, 0))
    full = lambda a, b: pl.BlockSpec((a, b), lambda i: (0, 0))
    return pl.pallas_call(
        body, name="post_fwd_bwd", grid=(T // tm,),
        in_specs=[row(2 * W_A), pl.BlockSpec((None, 3, 4, tm, 128), lambda i: (i // nt, 0, 0, i % nt, 0)),
                  row(2 * W_C),
                  row(D_MIX), row(D_MODEL), row(D_MODEL),
                  full(1, D_MIX), full(1, D_MODEL), full(1, D_MODEL), full(D_MIX, D_MODEL), full(W_A, W_A)],
        out_specs=[row(D_MODEL), row(3 * W_A), pl.BlockSpec((None, 6, tm, 128), lambda i: (i // nt, 0, i % nt, 0)),
                   row(3 * W_C), row(D_MIX),
                   full(D_MIX, D_MODEL), full(8, D_MODEL), full(8, 128)],
        out_shape=[jax.ShapeDtypeStruct((T, D_MODEL), F32),
                   jax.ShapeDtypeStruct((T, 3 * W_A), F32),
                   jax.ShapeDtypeStruct((T // SEQ, 6, SEQ, 128), F32),
                   jax.ShapeDtypeStruct((T, 3 * W_C), F32),
                   jax.ShapeDtypeStruct((T, D_MIX), MXU),
                   jax.ShapeDtypeStruct((D_MIX, D_MODEL), F32),
                   jax.ShapeDtypeStruct((8, D_MODEL), F32),
                   jax.ShapeDtypeStruct((8, 128), F32)],
        compiler_params=_cparams(("arbitrary",)),
    )(olse_a, olse_b, olse_c, z, x2, tgt, g, gain, bias, w_out, hsum)


def _dh_build(dqkva, dqkvb, dqc, dz, tab):
    T = dz.shape[0]
    tm = 512
    nt = SEQ // tm
    HQ = D_IN - D_MIX

    def body(da_ref, db6_ref, dqc_ref, dz_ref, tab_ref, dh_ref, db_ref):
        @pl.when(pl.program_id(0) == 0)
        def _():
            db_ref[...] = jnp.zeros_like(db_ref)
        tab = tab_ref[...]
        parts = [_rope(da_ref[:, 0:512], tab, -1) * Q_SCALE,
                 _rope(da_ref[:, 512:640], tab, -1),
                 da_ref[:, 640:768],
                 _rope(_lanes([db6_ref[0], db6_ref[1]]), tab, -1) * Q_SCALE,
                 _rope(_lanes([db6_ref[2], db6_ref[3]]), tab, -1),
                 _lanes([db6_ref[4], db6_ref[5]]),
                 dqc_ref[...] * Q_SCALE]
        dhq = jnp.concatenate(parts, axis=1)
        db_ref[0:1, :] += jnp.sum(dhq, axis=0, keepdims=True)
        dh_ref[:, 0:HQ] = dhq.astype(MXU)
        dh_ref[:, HQ:D_IN] = dz_ref[...]

    row = lambda w: pl.BlockSpec((tm, w), lambda i: (i, 0))
    return pl.pallas_call(
        body, name="dh_build", grid=(T // tm,),
        in_specs=[row(768), pl.BlockSpec((None, 6, tm, 128), lambda i: (i // nt, 0, i % nt, 0)),
                  row(256), row(D_MIX), pl.BlockSpec((tm, 384), lambda i: (i % nt, 0))],
        out_specs=[row(D_IN), pl.BlockSpec((8, HQ), lambda i: (0, 0))],
        out_shape=[jax.ShapeDtypeStruct((T, D_IN), MXU), jax.ShapeDtypeStruct((8, HQ), F32)],
        compiler_params=_cparams(("arbitrary",)),
    )(dqkva, dqkvb, dqc, dz, tab)


TAIL_TK = 512
TAIL_TN = D_IN // 2
TAIL_TM = 256
REDUCE_ROWS = 128


def _tail(x2, dh, gx1, w_in, sends, owns, small_g):
    T = x2.shape[0]
    kt = T // TAIL_TK
    ndw = (D_IN // TAIL_TN) * kt
    nsteps = ndw + T // TAIL_TM
    per_pass = TAIL_TN // COLS_PER_DEV
    pay = dh.dtype
    shapes = [(D_MODEL, COLS_PER_DEV), owns[0].shape, owns[1].shape, small_g.shape]
    n_arr = len(shapes)

    def body(x_ref, dh1_ref, dh2_ref, gx_ref, w_hbm, smem_ref, sout_ref, omem_ref, oout_ref, sg_ref,
             dx_ref, gin_ref, gmem_ref, gout_ref, gsm_ref,
             acc_ref, w_ref, stage_ref, ownin_ref, lin_ref, lmem_ref, lout_ref, lsm_ref,
             send_sems, recv_sems, w_sem):
        s = pl.program_id(0)
        x, y, c = _my_pos()
        me = 4 * x + 2 * y + c
        lands = (lin_ref, lmem_ref, lout_ref, lsm_ref)

        def src_of(a, j):
            return (stage_ref.at[j], smem_ref.at[j], sout_ref.at[j], sg_ref)[a]

        def to_peer(a, j):
            return pltpu.make_async_remote_copy(
                src_ref=src_of(a, j), dst_ref=lands[a].at[me], send_sem=send_sems.at[a, j],
                recv_sem=recv_sems.at[a, me], device_id=_dev_coords(j), device_id_type=MESH)

        def from_peer(a, m):
            return pltpu.make_async_remote_copy(
                src_ref=lands[a].at[m], dst_ref=lands[a].at[m], send_sem=send_sems.at[a, m],
                recv_sem=recv_sems.at[a, m], device_id=_dev_coords(m), device_id_type=MESH)

        w_copy = pltpu.make_async_copy(w_hbm, w_ref, w_sem)

        @pl.when(s == 0)
        def _():
            w_copy.start()
            for j in range(N_DEV):
                @pl.when(me != j)
                def _(j=j):
                    for a in range(1, n_arr):
                        to_peer(a, j).start()
            for a in range(n_arr - 1):
                lands[a][me] = jnp.zeros(shapes[a], lands[a].dtype)
            lsm_ref[me] = sg_ref[...]

        @pl.when(s < ndw)
        def _():
            k = s % kt

            @pl.when(k == 0)
            def _():
                acc_ref[...] = jnp.zeros_like(acc_ref)
            acc_ref[...] += _dot_tn(x_ref[...], dh1_ref[...])

            for p in range(D_IN // TAIL_TN):
                @pl.when(s == p * kt + kt - 1)
                def _(p=p):
                    for jj in range(per_pass):
                        j = p * per_pass + jj
                        blk = acc_ref[:, jj * COLS_PER_DEV:(jj + 1) * COLS_PER_DEV]
                        stage_ref[j] = blk.astype(pay)

                        @pl.when(me == j)
                        def _(blk=blk):
                            ownin_ref[...] = blk
                        pl.when(me != j)(to_peer(0, j).start)

        @pl.when(s >= ndw)
        def _():
            pl.when(s == ndw)(w_copy.wait)
            dx_ref[...] = _dot_nt(dh2_ref[...], w_ref[...]) + gx_ref[...]

        @pl.when(s == nsteps - 1)
        def _():
            for m in range(N_DEV):
                @pl.when(me != m)
                def _(m=m):
                    for a in range(n_arr):
                        from_peer(a, m).wait_recv()
            for j in range(N_DEV):
                @pl.when(me != j)
                def _(j=j):
                    for a in range(n_arr):
                        to_peer(a, j).wait_send()
            for a, (own, out) in enumerate(((ownin_ref, gin_ref), (omem_ref, gmem_ref), (oout_ref, gout_ref))):
                def chunk(i, carry, a=a, own=own, out=out):
                    rs = pl.ds(pl.multiple_of(i * REDUCE_ROWS, REDUCE_ROWS), REDUCE_ROWS)
                    g = own[rs, :]
                    for m in range(N_DEV):
                        g = g + lands[a][m, rs, :].astype(F32)
                    out[rs, :] = g
                    return carry
                lax.fori_loop(0, shapes[a][0] // REDUCE_ROWS, chunk, 0)
            g = lsm_ref[0]
            for m in range(1, N_DEV):
                g = g + lsm_ref[m]
            gsm_ref[...] = g

    dw_step = lambda s: jnp.minimum(s, ndw - 1)
    dx_step = lambda s: jnp.maximum(s - ndw, 0)
    any_spec = pl.BlockSpec(memory_space=pl.ANY)
    vmem = pl.BlockSpec(memory_space=pltpu.VMEM)
    scratch = [pltpu.VMEM((D_MODEL, TAIL_TN), F32), pltpu.VMEM((D_MODEL, D_IN), w_in.dtype),
               pltpu.VMEM((N_DEV,) + shapes[0], pay), pltpu.VMEM(shapes[0], F32)]
    scratch += [pltpu.VMEM((N_DEV,) + shapes[a], pay) for a in range(n_arr - 1)]
    scratch += [pltpu.VMEM((N_DEV,) + shapes[-1], F32),
                pltpu.SemaphoreType.DMA((n_arr, N_DEV)), pltpu.SemaphoreType.DMA((n_arr, N_DEV)),
                pltpu.SemaphoreType.DMA]
    return pl.pallas_call(
        body, name="tail_dw_dx_reduce", grid=(nsteps,),
        in_specs=[pl.BlockSpec((TAIL_TK, D_MODEL), lambda s: (dw_step(s) % kt, 0)),
                  pl.BlockSpec((TAIL_TK, TAIL_TN), lambda s: (dw_step(s) % kt, dw_step(s) // kt)),
                  pl.BlockSpec((TAIL_TM, D_IN), lambda s: (dx_step(s), 0)),
                  pl.BlockSpec((TAIL_TM, D_MODEL), lambda s: (dx_step(s), 0)),
                  any_spec, any_spec, any_spec, vmem, vmem, vmem],
        out_specs=[pl.BlockSpec((TAIL_TM, D_MODEL), lambda s: (dx_step(s), 0)), vmem, vmem, vmem, vmem],
        out_shape=[jax.ShapeDtypeStruct((T, D_MODEL), F32)] + [jax.ShapeDtypeStruct(sh, F32) for sh in shapes],
        scratch_shapes=scratch,
        compiler_params=_cparams(("arbitrary",)),
    )(x2, dh, dh, gx1, w_in, *sends, *owns, small_g)


def _adam_update(grads, params):
    n = len(grads)

    def body(*refs):
        g_refs, p_refs, o_refs = refs[:n], refs[n:4 * n], refs[4 * n:]
        for a in range(n):
            rows = g_refs[a].shape[0]
            cr = REDUCE_ROWS if rows % REDUCE_ROWS == 0 else rows
            w_ref, m_ref, v_ref = p_refs[3 * a:3 * a + 3]
            d_ref, nm_ref, nv_ref = o_refs[3 * a:3 * a + 3]

            def chunk(i, carry, cr=cr, g_ref=g_refs[a], w_ref=w_ref, m_ref=m_ref, v_ref=v_ref,
                      d_ref=d_ref, nm_ref=nm_ref, nv_ref=nv_ref):
                rs = pl.ds(pl.multiple_of(i * cr, cr), cr)
                d_ref[rs, :], nm_ref[rs, :], nv_ref[rs, :] = _adamw(w_ref[rs, :], g_ref[rs, :], m_ref[rs, :],
                                                                    v_ref[rs, :])
                return carry
            lax.fori_loop(0, rows // cr, chunk, 0)

    vmem = pl.BlockSpec(memory_space=pltpu.VMEM)
    flat = [p for grp in params for p in grp]
    outs = pl.pallas_call(
        body, name="adamw", in_specs=[vmem] * (4 * n), out_specs=[vmem] * (3 * n),
        out_shape=[jax.ShapeDtypeStruct(g.shape, F32) for g in grads for _ in range(3)],
        compiler_params=pltpu.CompilerParams(vmem_limit_bytes=VMEM_LIMIT),
    )(*grads, *flat)
    return [outs[3 * a:3 * a + 3] for a in range(n)]


def _step(x, mem, w_in_s, w_mem_s, w_out_s, b_in, sinks, g, gain, bias, tgt):
    B = x.shape[0]
    T = B * SEQ
    x2 = x.reshape(T, D_MODEL)
    t2 = tgt.reshape(T, D_MODEL)
    tab = _rope_table()
    lane = jnp.arange(W_A)
    hsum = (lane[:, None] // HEAD == lane[None, :] // HEAD).astype(MXU)
    me = 4 * lax.axis_index("x") + 2 * lax.axis_index("y") + lax.axis_index("c")

    (w_in_all,) = _gather_weights([w_in_s])
    qkva, qkvb, qc, z, w_in, w_mem_all, w_out_all = _in_proj(x2, w_in_all, b_in, tab, [w_mem_s, w_out_s])
    w_mem = w_mem_all.reshape(D_MODEL, 2 * W_C)
    w_out = w_out_all.reshape(D_MIX, D_MODEL)

    def per_ex(a):
        return a.reshape(B, SEQ, a.shape[-1])

    def flat(a):
        return a.reshape(T, a.shape[-1])

    qkva3, qc3 = per_ex(qkva), per_ex(qc)
    olse_a = _attn_fwd(qkva3, sinks, d=1, spg=1, unroll=3, hq=SWA_Q, hkv=SWA_KV, max_dist=BLK - 1,
                       name="swa_fwd")
    olse_b = _dil_fwd(qkvb)
    olse_c, mkv = _mem_attn_fwd(qc3, mem, w_mem)

    gx1, doba, dobb, dobc, dz, dw_out, small, loss = _post(
        flat(olse_a), olse_b, flat(olse_c), z, x2, t2, g, gain, bias, w_out, hsum)

    dqkva, dsink = _attn_bwd(qkva3, per_ex(doba), sinks, d=1, spg=1, unroll=3, hq=SWA_Q, hkv=SWA_KV,
                             max_dist=BLK - 1, name="swa_bwd")
    dqkvb = _dil_bwd(qkvb, dobb)
    dqc, dw_mem = _mem_attn_bwd(qc3, mkv, per_ex(dobc), mem)
    dh, dbq = _dh_build(flat(dqkva), dqkvb, flat(dqc), dz, tab)

    small_g = _pack_small(dict(b_in=jnp.concatenate([dbq[0], small[3]]), sinks=dsink[:, 0], g=small[2],
                               gain=small[0], bias=small[1], loss=loss[0, 0]))
    blocks = [dw_mem.reshape(N_DEV, ROWS_PER_DEV, 2 * W_C), dw_out.reshape(N_DEV, ROWS_PER_DEV, D_MODEL)]
    sends = [b.astype(MXU) for b in blocks]
    owns = [lax.dynamic_index_in_dim(b, me, axis=0, keepdims=False) for b in blocks]
    grad_x, g_in, g_mem, g_out, g_small = _tail(x2, dh, gx1, w_in, sends, owns, small_g)
    return grad_x.reshape(B, SEQ, D_MODEL), g_in, g_mem, g_out, g_small


def _my_pos():
    return lax.axis_index("x"), lax.axis_index("y"), lax.axis_index("c")


def _gather_weights(shards):
    n_arr = len(shards)

    def body(*refs):
        ins, outs = refs[0:n_arr], refs[n_arr:2 * n_arr]
        send_sems, recv_sems, local_sems = refs[2 * n_arr:]
        x, y, c = _my_pos()
        me, sibling = (x, y, c), (x, y, 1 - c)
        chips = [(1 - x, y), (x, 1 - y), (1 - x, 1 - y)]

        def slot(a, pos):
            return outs[a].at[4 * pos[0] + 2 * pos[1] + pos[2]]

        def copy(a, k, block, to, src=None):
            return pltpu.make_async_remote_copy(
                src_ref=slot(a, block) if src is None else src, dst_ref=slot(a, block),
                send_sem=send_sems.at[a, k], recv_sem=recv_sems.at[a, k],
                device_id=to, device_id_type=MESH)

        mine = [pltpu.make_async_copy(ins[a], slot(a, me), local_sems.at[a]) for a in range(n_arr)]
        for cp in mine:
            cp.start()
        first = []
        for a in range(n_arr):
            first.append(copy(a, 0, me, sibling, src=ins[a]))
            first += [copy(a, 1 + j, me, (*chip, c), src=ins[a]) for j, chip in enumerate(chips)]
        for cp in first:
            cp.start()
        passed = []
        for j, chip in enumerate(chips):
            for a in range(n_arr):
                copy(a, 1 + j, (*chip, c), me).wait_recv()
                fwd = copy(a, 4 + j, (*chip, c), sibling)
                fwd.start()
                passed.append(fwd)
        for a in range(n_arr):
            copy(a, 0, sibling, me).wait_recv()
            for j, chip in enumerate(chips):
                copy(a, 4 + j, (*chip, 1 - c), me).wait_recv()
        for cp in first + passed:
            cp.wait_send()
        for cp in mine:
            cp.wait()

    any_spec = pl.BlockSpec(memory_space=pl.ANY)
    return pl.pallas_call(
        body, name="gather_weights",
        in_specs=[any_spec] * n_arr, out_specs=[any_spec] * n_arr,
        out_shape=[jax.ShapeDtypeStruct((N_DEV,) + s.shape, s.dtype) for s in shards],
        scratch_shapes=[pltpu.SemaphoreType.DMA((n_arr, 7)), pltpu.SemaphoreType.DMA((n_arr, 7)),
                        pltpu.SemaphoreType.DMA((n_arr,))],
    )(*shards)


def _adamw(w, g, m, v):
    m = ADAM_B1 * m + (1.0 - ADAM_B1) * g
    v = ADAM_B2 * v + (1.0 - ADAM_B2) * (g * g)
    m_hat = m / (1.0 - ADAM_B1 ** ADAM_STEP)
    v_hat = v / (1.0 - ADAM_B2 ** ADAM_STEP)
    delta = -ADAM_LR * (m_hat / (jnp.sqrt(v_hat) + ADAM_EPS) + ADAM_WD * w)
    return delta, m, v


_SMALL_SIZES = (("b_in", D_IN), ("g", D_MIX), ("gain", D_MODEL), ("bias", D_MODEL), ("sinks", SWA_Q), ("loss", 1))


def _pack_small(d):
    flat = jnp.concatenate([jnp.reshape(d[k], (-1,)).astype(F32) if k in d else jnp.zeros((n,), F32)
                            for k, n in _SMALL_SIZES])
    flat = jnp.pad(flat, (0, SMALL_ROWS * 128 - flat.shape[0]))
    return flat.reshape(SMALL_ROWS, 128)


def _unpack_small(p):
    flat = p.reshape(-1)
    out, off = {}, 0
    for k, n in _SMALL_SIZES:
        out[k] = flat[off:off + n].reshape(1, n)
        off += n
    return out


def kernel(x, mem, w_in, b_in, w_mem, attn_sinks, g_branch, w_out, ln_gain, ln_bias, loss_target, m_w_in, m_b_in, m_w_mem, m_attn_sinks, m_g_branch, m_w_out, m_ln_gain, m_ln_bias, v_w_in, v_b_in, v_w_mem, v_attn_sinks, v_g_branch, v_w_out, v_ln_gain, v_ln_bias):
    grad_x, g_in, g_mem, g_out, g_small = _step(
        x, mem, w_in[0].astype(MXU), w_mem[0].astype(MXU), w_out[0].astype(MXU), b_in, attn_sinks[0],
        g_branch, ln_gain, ln_bias, loss_target)

    small_w = _pack_small(dict(b_in=b_in, g=g_branch, gain=ln_gain, bias=ln_bias, sinks=attn_sinks))
    small_m = _pack_small(dict(b_in=m_b_in, g=m_g_branch, gain=m_ln_gain, bias=m_ln_bias, sinks=m_attn_sinks))
    small_v = _pack_small(dict(b_in=v_b_in, g=v_g_branch, gain=v_ln_gain, bias=v_ln_bias, sinks=v_attn_sinks))
    grads = [g_in, g_mem, g_out, g_small]
    params = [(w_in[0], m_w_in[0], v_w_in[0]), (w_mem[0], m_w_mem[0], v_w_mem[0]),
              (w_out[0], m_w_out[0], v_w_out[0]), (small_w, small_m, small_v)]
    upd = _adam_update(grads, params)

    res = [[grads[a]] + list(upd[a]) for a in range(4)]
    big = [[r[None] for r in res[a]] for a in range(3)]
    sm = [_unpack_small(r) for r in res[3]]

    def group(i):
        return (big[0][i], sm[i]["b_in"], big[1][i], sm[i]["sinks"], sm[i]["g"], big[2][i],
                sm[i]["gain"], sm[i]["bias"])

    loss = sm[0]["loss"].reshape(())
    return (loss, grad_x, *group(0), *group(1), *group(2), *group(3))
```

```python
import functools
import math

import jax
import jax.numpy as jnp
from jax import lax
from jax.experimental import pallas as pl
from jax.experimental.pallas import tpu as pltpu

F32 = jnp.float32
MXU = jnp.bfloat16

D_MODEL = 1024
SEQ = 2048
HEAD = 64
BLK = 128
SWA_Q, SWA_KV = 8, 2
DIL_H = 4
MEM_H = 4
MEM_LEN = 256
W_A, W_KVA, W_B, W_C = 512, 128, 256, 256
D_MIX = 1024
D_IN = 2816
N_DEV = 8
COLS_PER_DEV = D_IN // N_DEV
ROWS_PER_DEV = D_MODEL // N_DEV
ROPE_THETA = 10000.0
LN_EPS = 1e-5
RMS_EPS = 1e-6
ALPHA = 2.0 ** 0.25
Q_SCALE = HEAD ** -0.5
NEG = -1e30
SMALL_ROWS = 48
VMEM_LIMIT = 56 * 1024 * 1024

ADAM_LR = 0.001
ADAM_B1 = 0.9
ADAM_B2 = 0.999
ADAM_EPS = 1e-08
ADAM_WD = 0.01
ADAM_STEP = 10

MESH = pl.DeviceIdType.MESH


def _cparams(sem=None):
    return pltpu.CompilerParams(dimension_semantics=sem, vmem_limit_bytes=VMEM_LIMIT)


def _dot(a, b):
    return jnp.dot(a, b, preferred_element_type=F32)


def _dot_nt(a, b):
    return lax.dot_general(a, b, (((1,), (1,)), ((), ())), preferred_element_type=F32)


def _dot_t0(a, b):
    return lax.dot_general(a, b, (((0,), (0,)), ((), ())), preferred_element_type=F32)


def _dot_tn(a, b):
    return jnp.dot(a.T.astype(MXU), b, preferred_element_type=F32)


def _rope(t, tab, sign):
    cos, sa, sb = tab[:, 0:128], tab[:, 128:256], tab[:, 256:384]
    outs = []
    for c in range(t.shape[1] // 128):
        tc = t[:, c * 128:(c + 1) * 128]
        r = pltpu.roll(tc, 96, 1) * sa + pltpu.roll(tc, 32, 1) * sb
        outs.append(tc * cos + r if sign > 0 else tc * cos - r)
    return outs[0] if len(outs) == 1 else jnp.concatenate(outs, axis=1)


def _rope_table():
    pos = jnp.arange(SEQ, dtype=F32)
    inv = ROPE_THETA ** (-jnp.arange(0, HEAD, 2, dtype=F32) / HEAD)
    ang = pos[:, None] * inv[None, :]
    ang = jnp.concatenate([ang, ang], axis=-1)
    cos, sin = jnp.cos(ang), jnp.sin(ang)
    lane = jnp.arange(HEAD)[None, :]
    sa = jnp.where(lane < HEAD // 2, -sin, 0.0)
    sb = jnp.where(lane >= HEAD // 2, sin, 0.0)
    two = lambda t: jnp.concatenate([t, t], axis=-1)
    return jnp.concatenate([two(cos), two(sa), two(sb)], axis=-1).astype(F32)


def _dev_coords(j):
    return (j >> 2, (j >> 1) & 1, j & 1)


def _owner_order_pos(j):
    return (j % 2) * (N_DEV // 2) + j // 2


def _in_proj(x2, w_all, b_in, tab, late_shards):
    T = x2.shape[0]
    tm = 512
    n_late = len(late_shards)

    def body(x_ref, wall_ref, b_ref, tab_ref, *rest):
        late_in, rest = rest[:n_late], rest[n_late:]
        qkva_ref, qkvb_ref, qc_ref, z_ref, wperm_ref = rest[:5]
        late_out = rest[5:5 + n_late]
        w_ref, send_sems, recv_sems, local_sems = rest[5 + n_late:]
        step, last = pl.program_id(0), pl.num_programs(0) - 1
        x, y, c = _my_pos()
        me = 4 * x + 2 * y + c

        def to_peer(a, j):
            return pltpu.make_async_remote_copy(
                src_ref=late_in[a], dst_ref=late_out[a].at[me], send_sem=send_sems.at[a, j],
                recv_sem=recv_sems.at[a, me], device_id=_dev_coords(j), device_id_type=MESH)

        def from_peer(a, m):
            return pltpu.make_async_remote_copy(
                src_ref=late_out[a].at[m], dst_ref=late_out[a].at[m], send_sem=send_sems.at[a, m],
                recv_sem=recv_sems.at[a, m], device_id=_dev_coords(m), device_id_type=MESH)

        def mine(a):
            return pltpu.make_async_copy(late_in[a], late_out[a].at[me], local_sems.at[a])

        @pl.when(step == 0)
        def _():
            for a in range(n_late):
                mine(a).start()
                for j in range(N_DEV):
                    pl.when(me != j)(to_peer(a, j).start)
            for j in range(N_DEV):
                w_ref[:, j * COLS_PER_DEV:(j + 1) * COLS_PER_DEV] = wall_ref[j]
                q = _owner_order_pos(j)
                wperm_ref[:, q * COLS_PER_DEV:(q + 1) * COLS_PER_DEV] = wall_ref[j]

        xb = x_ref[...].astype(MXU)
        tab = tab_ref[...]

        def seg(c0, c1):
            return _dot(xb, w_ref[:, c0:c1]) + b_ref[:, c0:c1]

        qkva_ref[:, 0:512] = (_rope(seg(0, 512), tab, 1) * Q_SCALE).astype(MXU)
        qkva_ref[:, 512:640] = _rope(seg(512, 640), tab, 1).astype(MXU)
        qkva_ref[:, 640:768] = seg(640, 768).astype(MXU)
        qkvb = (_rope(seg(768, 1024), tab, 1) * Q_SCALE, _rope(seg(1024, 1280), tab, 1), seg(1280, 1536))
        for j, t in enumerate(qkvb):
            for c in range(2):
                qkvb_ref[2 * j + c] = t[:, c * 128:(c + 1) * 128]
        qc_ref[...] = (seg(1536, 1792) * Q_SCALE).astype(MXU)
        z_ref[...] = seg(1792, 2816)

        @pl.when(step == last)
        def _():
            for a in range(n_late):
                mine(a).wait()
                for m in range(N_DEV):
                    pl.when(me != m)(from_peer(a, m).wait_recv)
                for j in range(N_DEV):
                    pl.when(me != j)(to_peer(a, j).wait_send)

    nt = SEQ // tm
    any_spec = pl.BlockSpec(memory_space=pl.ANY)
    return pl.pallas_call(
        body, name="in_proj_fwd",
        grid=(T // tm,),
        in_specs=[pl.BlockSpec((tm, D_MODEL), lambda i: (i, 0)),
                  pl.BlockSpec((N_DEV, D_MODEL, COLS_PER_DEV), lambda i: (0, 0, 0)),
                  pl.BlockSpec((1, D_IN), lambda i: (0, 0)),
                  pl.BlockSpec((tm, 384), lambda i: (i % nt, 0))] + [any_spec] * n_late,
        out_specs=[pl.BlockSpec((tm, 768), lambda i: (i, 0)),
                   pl.BlockSpec((None, 6, tm, 128), lambda i: (i // nt, 0, i % nt, 0)),
                   pl.BlockSpec((tm, 256), lambda i: (i, 0)),
                   pl.BlockSpec((tm, D_MIX), lambda i: (i, 0)),
                   pl.BlockSpec((D_MODEL, D_IN), lambda i: (0, 0))] + [any_spec] * n_late,
        out_shape=[jax.ShapeDtypeStruct((T, 768), MXU), jax.ShapeDtypeStruct((T // SEQ, 6, SEQ, 128), F32),
                   jax.ShapeDtypeStruct((T, 256), MXU), jax.ShapeDtypeStruct((T, D_MIX), F32),
                   jax.ShapeDtypeStruct((D_MODEL, D_IN), w_all.dtype)]
        + [jax.ShapeDtypeStruct((N_DEV,) + s.shape, s.dtype) for s in late_shards],
        scratch_shapes=[pltpu.VMEM((D_MODEL, D_IN), w_all.dtype),
                        pltpu.SemaphoreType.DMA((n_late, N_DEV)), pltpu.SemaphoreType.DMA((n_late, N_DEV)),
                        pltpu.SemaphoreType.DMA((n_late,))],
        compiler_params=_cparams(("arbitrary",)),
    )(x2, w_all, b_in, tab, *late_shards)


CHAIN = 4


def _band_bias(max_dist):
    kj = lax.broadcasted_iota(jnp.int32, (2 * BLK, BLK), 0)
    qi = lax.broadcasted_iota(jnp.int32, (2 * BLK, BLK), 1)
    dist = qi + BLK - kj
    band = jnp.where((dist >= 0) & (dist <= max_dist), 0.0, NEG).astype(F32)
    k1 = lax.broadcasted_iota(jnp.int32, (BLK, BLK), 0)
    q1 = lax.broadcasted_iota(jnp.int32, (BLK, BLK), 1)
    first = jnp.where((q1 - k1 >= 0) & (q1 - k1 <= max_dist), 0.0, NEG).astype(F32)
    return jnp.concatenate([band] * CHAIN, axis=1), jnp.concatenate([first] * CHAIN, axis=1)


def _lanes(parts):
    return jnp.concatenate(parts, axis=1)


def _col(off, h):
    return slice(off + h * HEAD, off + (h + 1) * HEAD)


def _pair_rows(ref, rows, off, h0):
    t = ref[rows, slice(off + h0 * HEAD, off + (h0 + 2) * HEAD)].T
    return t[0:1, :], t[HEAD:HEAD + 1, :]


def _store_o_lse(o_ref, rows, oT, lse, ocols, lcols):
    n = len(ocols)
    res = jnp.concatenate([oT, jnp.broadcast_to(lse, (HEAD, lse.shape[1]))], axis=0).T
    rq = res.shape[0] // n
    for i in range(n):
        o_ref[rows[i], ocols[i]] = res[i * rq:(i + 1) * rq, 0:HEAD]
        o_ref[rows[i], lcols[i]] = res[i * rq:(i + 1) * rq, HEAD:2 * HEAD]


def _attn_geometry(qkv, d, spg, hq, hkv):
    B, Ls, wall = qkv.shape
    qw, kw = hq * HEAD, hkv * HEAD
    wtot = qw + 2 * kw
    assert wall == d * wtot and Ls % BLK == 0 and d % spg == 0 and hq % CHAIN == 0
    G = hq // hkv
    assert G in (1, CHAIN)
    return B, Ls, qw, kw, wtot, G, Ls // BLK


def _attn_fwd(qkv, sinks, *, d, spg, unroll, hq, hkv, max_dist, name):
    B, Ls, qw, kw, wtot, G, nblk = _attn_geometry(qkv, d, spg, hq, hkv)
    use_sink = sinks is not None
    assert (nblk - 1) % unroll == 0
    trips = (nblk - 1) // unroll

    def body(*refs):
        if use_sink:
            sink_ref, qkv_ref, o_ref = refs
        else:
            qkv_ref, o_ref = refs
        band, first = _band_bias(max_dist)
        for st in range(spg):
            ib, ob = st * wtot, st * 2 * qw
            chains = []
            for ch in range(hq // CHAIN):
                heads = list(range(ch * CHAIN, (ch + 1) * CHAIN))
                sinkrow = _lanes([jnp.full((1, BLK), sink_ref[h], F32) for h in heads]) if use_sink else None

                def block(rows_q, rows_k, bias, heads=heads, ib=ib, ob=ob, sinkrow=sinkrow):
                    qs = [qkv_ref[rows_q, _col(ib, h)] for h in heads]
                    if G > 1:
                        hk = heads[0] // G
                        kk = qkv_ref[rows_k, _col(ib + qw, hk)]
                        vv = qkv_ref[rows_k, _col(ib + qw + kw, hk)]
                        sT = _dot_nt(kk, jnp.concatenate(qs, axis=0))
                    else:
                        vvs = [qkv_ref[rows_k, _col(ib + qw + kw, h)] for h in heads]
                        sT = _lanes([_dot_nt(qkv_ref[rows_k, _col(ib + qw, h)], qs[i])
                                     for i, h in enumerate(heads)])
                    sT = sT + bias
                    m = jnp.max(sT, axis=0, keepdims=True)
                    if use_sink:
                        m = jnp.maximum(m, sinkrow)
                    pT = jnp.exp(sT - m)
                    l = jnp.sum(pT, axis=0, keepdims=True)
                    if use_sink:
                        l = l + jnp.exp(sinkrow - m)
                    pnT = (pT * (1.0 / l)).astype(MXU)
                    if G > 1:
                        oT = _dot_t0(vv, pnT)
                    else:
                        oT = _lanes([_dot_t0(vvs[i], pnT[:, i * BLK:(i + 1) * BLK]) for i in range(CHAIN)])
                    _store_o_lse(o_ref, [rows_q] * CHAIN, oT, m + jnp.log(l),
                                 [_col(ob, h) for h in heads], [_col(ob + qw, h) for h in heads])

                chains.append(block)

            for block in chains:
                block(pl.ds(0, BLK), pl.ds(0, BLK), first)

            def blocks_at(i0, chains=chains):
                for u in range(unroll):
                    r0 = pl.multiple_of((i0 + u) * BLK, BLK)
                    rk = pl.multiple_of((i0 + u) * BLK - BLK, BLK)
                    for block in chains:
                        block(pl.ds(r0, BLK), pl.ds(rk, 2 * BLK), band)

            if trips == 1:
                blocks_at(1)
            elif trips > 1:
                def loop(j, carry, blocks_at=blocks_at):
                    blocks_at(1 + j * unroll)
                    return carry
                lax.fori_loop(0, trips, loop, 0)

    in_specs = [pl.BlockSpec((None, Ls, spg * wtot), lambda b, r: (b, 0, r))]
    args = [qkv]
    if use_sink:
        in_specs = [pl.BlockSpec(memory_space=pltpu.SMEM)] + in_specs
        args = [sinks] + args
    return pl.pallas_call(
        body, name=name, grid=(B, d // spg), in_specs=in_specs,
        out_specs=pl.BlockSpec((None, Ls, spg * 2 * qw), lambda b, r: (b, 0, r)),
        out_shape=jax.ShapeDtypeStruct((B, Ls, d * 2 * qw), F32),
        compiler_params=_cparams(("arbitrary", "arbitrary")),
    )(*args)


def _attn_bwd(qkv, dob, sinks, *, d, spg, unroll, hq, hkv, max_dist, name):
    B, Ls, qw, kw, wtot, G, nblk = _attn_geometry(qkv, d, spg, hq, hkv)
    assert dob.shape == (B, Ls, d * 3 * qw) and (nblk - 1) % unroll == 0
    trips = (nblk - 1) // unroll
    use_sink = sinks is not None

    def body(*refs):
        if use_sink:
            sink_ref, qkv_ref, dob_ref, dq_ref, dsink_ref = refs
        else:
            qkv_ref, dob_ref, dq_ref = refs
        band, first = _band_bias(max_dist)
        if use_sink:
            @pl.when((pl.program_id(0) == 0) & (pl.program_id(1) == 0))
            def _():
                dsink_ref[...] = jnp.zeros_like(dsink_ref)
        for st in range(spg):
            ib, db = st * wtot, st * 3 * qw
            dq_ref[:, ib + qw:ib + wtot] = jnp.zeros((Ls, 2 * kw), F32)
            chains = []
            for ch in range(hq // CHAIN):
                heads = list(range(ch * CHAIN, (ch + 1) * CHAIN))
                sinkrow = _lanes([jnp.full((1, BLK), sink_ref[h], F32) for h in heads]) if use_sink else None

                def block(rows_q, rows_k, bias, acc, heads=heads, ib=ib, db=db, sinkrow=sinkrow):
                    qs = [qkv_ref[rows_q, _col(ib, h)] for h in heads]
                    dos = [dob_ref[rows_q, _col(db, h)].astype(MXU) for h in heads]
                    lse = _lanes([r for j in range(0, CHAIN, 2)
                                  for r in _pair_rows(dob_ref, rows_q, db + qw, heads[j])])
                    delta = _lanes([r for j in range(0, CHAIN, 2)
                                    for r in _pair_rows(dob_ref, rows_q, db + 2 * qw, heads[j])])
                    if G > 1:
                        hk = heads[0] // G
                        kc, vc = _col(ib + qw, hk), _col(ib + qw + kw, hk)
                        kk, vv = qkv_ref[rows_k, kc], qkv_ref[rows_k, vc]
                        qst, dost = jnp.concatenate(qs, axis=0), jnp.concatenate(dos, axis=0)
                        sT, dpT = _dot_nt(kk, qst), _dot_nt(vv, dost)
                    else:
                        kks = [qkv_ref[rows_k, _col(ib + qw, h)] for h in heads]
                        vvs = [qkv_ref[rows_k, _col(ib + qw + kw, h)] for h in heads]
                        sT = _lanes([_dot_nt(kks[i], qs[i]) for i in range(CHAIN)])
                        dpT = _lanes([_dot_nt(vvs[i], dos[i]) for i in range(CHAIN)])
                    pT = jnp.exp(sT + bias - lse)
                    dsT = pT * (dpT - delta)
                    dsb, pb = dsT.astype(MXU), pT.astype(MXU)
                    if G > 1:
                        dq_ref[rows_k, kc] += _dot(dsb, qst)
                        dq_ref[rows_k, vc] += _dot(pb, dost)
                        dq = _dot_t0(kk, dsb).T
                        for i, h in enumerate(heads):
                            dq_ref[rows_q, _col(ib, h)] = dq[i * BLK:(i + 1) * BLK]
                    else:
                        dqT = []
                        for i, h in enumerate(heads):
                            ls = slice(i * BLK, (i + 1) * BLK)
                            dq_ref[rows_k, _col(ib + qw, h)] += _dot(dsb[:, ls], qs[i])
                            dq_ref[rows_k, _col(ib + qw + kw, h)] += _dot(pb[:, ls], dos[i])
                            dqT.append(_dot_t0(kks[i], dsb[:, ls]))
                        dq = _lanes(dqT).T
                        for i, h in enumerate(heads):
                            dq_ref[rows_q, _col(ib, h)] = dq[i * BLK:(i + 1) * BLK]
                    if use_sink:
                        acc = acc - jnp.exp(sinkrow - lse) * delta
                    return acc

                chains.append(block)

            accs = tuple(block(pl.ds(0, BLK), pl.ds(0, BLK), first, jnp.zeros((1, CHAIN * BLK), F32))
                         for block in chains)

            def blocks_at(i0, accs, chains=chains):
                for u in range(unroll):
                    r0 = pl.multiple_of((i0 + u) * BLK, BLK)
                    rk = pl.multiple_of((i0 + u) * BLK - BLK, BLK)
                    accs = tuple(block(pl.ds(r0, BLK), pl.ds(rk, 2 * BLK), band, acc)
                                 for block, acc in zip(chains, accs))
                return accs

            if trips == 1:
                accs = blocks_at(1, accs)
            elif trips > 1:
                def loop(j, accs, blocks_at=blocks_at):
                    return blocks_at(1 + j * unroll, accs)
                accs = lax.fori_loop(0, trips, loop, accs)
            if use_sink:
                for ch, acc in enumerate(accs):
                    for i in range(CHAIN):
                        h = ch * CHAIN + i
                        tot = jnp.sum(acc[:, i * BLK:(i + 1) * BLK], axis=1, keepdims=True)
                        dsink_ref[h:h + 1, :] += jnp.broadcast_to(tot, (1, 128))

    in_specs = [pl.BlockSpec((None, Ls, spg * wtot), lambda b, r: (b, 0, r)),
                pl.BlockSpec((None, Ls, spg * 3 * qw), lambda b, r: (b, 0, r))]
    out_specs = [pl.BlockSpec((None, Ls, spg * wtot), lambda b, r: (b, 0, r))]
    out_shape = [jax.ShapeDtypeStruct((B, Ls, d * wtot), F32)]
    args = [qkv, dob]
    if use_sink:
        in_specs = [pl.BlockSpec(memory_space=pltpu.SMEM)] + in_specs
        args = [sinks] + args
        out_specs.append(pl.BlockSpec((8, 128), lambda b, r: (0, 0)))
        out_shape.append(jax.ShapeDtypeStruct((8, 128), F32))
    return pl.pallas_call(
        body, name=name, grid=(B, d // spg), in_specs=in_specs, out_specs=out_specs, out_shape=out_shape,
        compiler_params=_cparams(("arbitrary", "arbitrary")),
    )(*args)


DILATIONS = (1, 4, 16)
DIL_PAIRS_H = DIL_H // 2


def _stream_rows(d, r, i, n):
    if d == 1:
        return pl.ds(pl.multiple_of(i * BLK, BLK), n)
    return pl.ds(r + i * (BLK * d), n, stride=d)


def _heads_of(ref, base, rows, dtype):
    out = []
    for c in range(DIL_PAIRS_H):
        t = ref.at[base + c][rows, :].astype(dtype)
        out += [t[:, 0:HEAD], t[:, HEAD:2 * HEAD]]
    return out


def _pair_to_tokens(a, b):
    return jnp.concatenate([a, b], axis=0).T


def _dil_schedule(body_first, body_next):
    for p, d in enumerate(DILATIONS):
        nblk = SEQ // d // BLK
        if d == 1:
            carry = body_first(p, d, 0)
            def loop(j, c, p=p, d=d):
                for u in range(3):
                    body_next(p, d, 0, 1 + 3 * j + u)
                return c
            lax.fori_loop(0, (nblk - 1) // 3, loop, 0)
        elif nblk > 1:
            def loop(r, c, p=p, d=d, nblk=nblk):
                body_first(p, d, r)
                for i in range(1, nblk):
                    body_next(p, d, r, i)
                return c
            lax.fori_loop(0, d, loop, 0)
        else:
            def loop(j, c, p=p, d=d):
                for u in range(4):
                    body_first(p, d, 4 * j + u)
                return c
            lax.fori_loop(0, d // 4, loop, 0)


def _dil_fwd(qkvb):
    B = qkvb.shape[0]

    def body(qkv_ref, o_ref):
        band, first = _band_bias(BLK)

        def block(p, d, rows_q, rows_k, bias):
            qs = _heads_of(qkv_ref, 0, rows_q, MXU)
            kks = _heads_of(qkv_ref, DIL_PAIRS_H, rows_k, MXU)
            vvs = _heads_of(qkv_ref, 2 * DIL_PAIRS_H, rows_k, MXU)
            sT = _lanes([_dot_nt(kks[h], qs[h]) for h in range(DIL_H)]) + bias
            m = jnp.max(sT, axis=0, keepdims=True)
            pT = jnp.exp(sT - m)
            l = jnp.sum(pT, axis=0, keepdims=True)
            pnT = (pT * (1.0 / l)).astype(MXU)
            lse = m + jnp.log(l)
            for c in range(DIL_PAIRS_H):
                oT = [_dot_t0(vvs[h], pnT[:, h * BLK:(h + 1) * BLK]) for h in (2 * c, 2 * c + 1)]
                ls = [jnp.broadcast_to(lse[:, h * BLK:(h + 1) * BLK], (HEAD, BLK)) for h in (2 * c, 2 * c + 1)]
                o_ref.at[p, c][rows_q, :] = _pair_to_tokens(*oT)
                o_ref.at[p, DIL_PAIRS_H + c][rows_q, :] = _pair_to_tokens(*ls)

        def body_first(p, d, r):
            rows = _stream_rows(d, r, 0, BLK)
            block(p, d, rows, rows, first)

        def body_next(p, d, r, i):
            block(p, d, _stream_rows(d, r, i, BLK), _stream_rows(d, r, i - 1, 2 * BLK), band)

        _dil_schedule(body_first, body_next)

    return pl.pallas_call(
        body, name="dil_fwd", grid=(B,),
        in_specs=[pl.BlockSpec((None, 6, SEQ, 128), lambda b: (b, 0, 0, 0))],
        out_specs=pl.BlockSpec((None, 3, 4, SEQ, 128), lambda b: (b, 0, 0, 0, 0)),
        out_shape=jax.ShapeDtypeStruct((B, 3, 4, SEQ, 128), F32),
        compiler_params=_cparams(("arbitrary",)),
    )(qkvb)


def _dil_bwd(qkvb, dobb):
    B = qkvb.shape[0]

    def body(qkv_ref, dob_ref, dq_ref):
        band, first = _band_bias(BLK)
        dq_ref[...] = jnp.zeros_like(dq_ref)

        def rowvec(base, rows):
            parts = []
            for c in range(DIL_PAIRS_H):
                t = dob_ref.at[base + c][rows, :].T
                parts += [t[0:1, :], t[HEAD:HEAD + 1, :]]
            return _lanes(parts)

        def block(p, d, rows_q, rows_k, bias):
            qs = _heads_of(qkv_ref, 0, rows_q, MXU)
            kks = _heads_of(qkv_ref, DIL_PAIRS_H, rows_k, MXU)
            vvs = _heads_of(qkv_ref, 2 * DIL_PAIRS_H, rows_k, MXU)
            dos = _heads_of(dob_ref, 0, rows_q, MXU)
            lse = rowvec(DIL_PAIRS_H, rows_q)
            delta = rowvec(2 * DIL_PAIRS_H, rows_q)
            sT = _lanes([_dot_nt(kks[h], qs[h]) for h in range(DIL_H)])
            dpT = _lanes([_dot_nt(vvs[h], dos[h]) for h in range(DIL_H)])
            pT = jnp.exp(sT + bias - lse)
            dsT = pT * (dpT - delta)
            dsb, pb = dsT.astype(MXU), pT.astype(MXU)
            for c in range(DIL_PAIRS_H):
                hs = (2 * c, 2 * c + 1)
                dq_ref.at[c][rows_q, :] += _pair_to_tokens(
                    *[_dot_t0(kks[h], dsb[:, h * BLK:(h + 1) * BLK]) for h in hs])
                dq_ref.at[DIL_PAIRS_H + c][rows_k, :] += _lanes(
                    [_dot(dsb[:, h * BLK:(h + 1) * BLK], qs[h]) for h in hs])
                dq_ref.at[2 * DIL_PAIRS_H + c][rows_k, :] += _lanes(
                    [_dot(pb[:, h * BLK:(h + 1) * BLK], dos[h]) for h in hs])

        def body_first(p, d, r):
            rows = _stream_rows(d, r, 0, BLK)
            block(p, d, rows, rows, first)

        def body_next(p, d, r, i):
            block(p, d, _stream_rows(d, r, i, BLK), _stream_rows(d, r, i - 1, 2 * BLK), band)

        _dil_schedule(body_first, body_next)

    spec = pl.BlockSpec((None, 6, SEQ, 128), lambda b: (b, 0, 0, 0))
    return pl.pallas_call(
        body, name="dil_bwd", grid=(B,), in_specs=[spec, spec], out_specs=spec,
        out_shape=jax.ShapeDtypeStruct((B, 6, SEQ, 128), F32),
        compiler_params=_cparams(("arbitrary",)),
    )(qkvb, dobb)


MEM_QROWS = 512


def _mem_attn_fwd(qc, mem, w_mem):
    B = qc.shape[0]

    def body(q_ref, mem_ref, w_ref, o_ref, mkv_ref):
        mkv_ref[...] = _dot(mem_ref[...].astype(MXU), w_ref[...]).astype(MXU)

        def loop(i, carry):
            rows = pl.ds(pl.multiple_of(i * MEM_QROWS, MEM_QROWS), MEM_QROWS)
            for h in range(MEM_H):
                sT = _dot_nt(mkv_ref[:, _col(0, h)], q_ref[rows, _col(0, h)])
                m = jnp.max(sT, axis=0, keepdims=True)
                pT = jnp.exp(sT - m)
                l = jnp.sum(pT, axis=0, keepdims=True)
                oT = _dot_t0(mkv_ref[:, _col(W_C, h)], (pT * (1.0 / l)).astype(MXU))
                _store_o_lse(o_ref, [rows], oT, m + jnp.log(l), [_col(0, h)], [_col(W_C, h)])
            return carry
        lax.fori_loop(0, SEQ // MEM_QROWS, loop, 0)

    return pl.pallas_call(
        body, name="mem_attn_fwd", grid=(B,),
        in_specs=[pl.BlockSpec((None, SEQ, W_C), lambda b: (b, 0, 0)),
                  pl.BlockSpec((None, MEM_LEN, D_MODEL), lambda b: (b, 0, 0)),
                  pl.BlockSpec((D_MODEL, 2 * W_C), lambda b: (0, 0))],
        out_specs=[pl.BlockSpec((None, SEQ, 2 * W_C), lambda b: (b, 0, 0)),
                   pl.BlockSpec((None, MEM_LEN, 2 * W_C), lambda b: (b, 0, 0))],
        out_shape=[jax.ShapeDtypeStruct((B, SEQ, 2 * W_C), F32),
                   jax.ShapeDtypeStruct((B, MEM_LEN, 2 * W_C), MXU)],
        compiler_params=_cparams(("arbitrary",)),
    )(qc, mem, w_mem)


def _mem_attn_bwd(qc, mkv, dob, mem):
    B = qc.shape[0]

    def body(q_ref, mkv_ref, dob_ref, mem_ref, dq_ref, dw_ref, dmkv_ref):
        @pl.when(pl.program_id(0) == 0)
        def _():
            dw_ref[...] = jnp.zeros_like(dw_ref)
        dmkv_ref[...] = jnp.zeros_like(dmkv_ref)

        def loop(i, carry):
            rows = pl.ds(pl.multiple_of(i * MEM_QROWS, MEM_QROWS), MEM_QROWS)
            for h0 in range(0, MEM_H, 2):
                lses = _pair_rows(dob_ref, rows, W_C, h0)
                deltas = _pair_rows(dob_ref, rows, 2 * W_C, h0)
                dqT = []
                for j in range(2):
                    h = h0 + j
                    q = q_ref[rows, _col(0, h)]
                    do = dob_ref[rows, _col(0, h)].astype(MXU)
                    mk, mv = mkv_ref[:, _col(0, h)], mkv_ref[:, _col(W_C, h)]
                    pT = jnp.exp(_dot_nt(mk, q) - lses[j])
                    dsT = pT * (_dot_nt(mv, do) - deltas[j])
                    dsb = dsT.astype(MXU)
                    dmkv_ref[:, _col(0, h)] += _dot(dsb, q)
                    dmkv_ref[:, _col(W_C, h)] += _dot(pT.astype(MXU), do)
                    dqT.append(_dot_t0(mk, dsb))
                dq_ref[rows, slice(h0 * HEAD, (h0 + 2) * HEAD)] = jnp.concatenate(dqT, axis=0).T
            return carry
        lax.fori_loop(0, SEQ // MEM_QROWS, loop, 0)
        dw_ref[...] += _dot_tn(mem_ref[...], dmkv_ref[...].astype(MXU))

    return pl.pallas_call(
        body, name="mem_attn_bwd", grid=(B,),
        in_specs=[pl.BlockSpec((None, SEQ, W_C), lambda b: (b, 0, 0)),
                  pl.BlockSpec((None, MEM_LEN, 2 * W_C), lambda b: (b, 0, 0)),
                  pl.BlockSpec((None, SEQ, 3 * W_C), lambda b: (b, 0, 0)),
                  pl.BlockSpec((None, MEM_LEN, D_MODEL), lambda b: (b, 0, 0))],
        out_specs=[pl.BlockSpec((None, SEQ, W_C), lambda b: (b, 0, 0)),
                   pl.BlockSpec((D_MODEL, 2 * W_C), lambda b: (0, 0))],
        out_shape=[jax.ShapeDtypeStruct((B, SEQ, W_C), F32),
                   jax.ShapeDtypeStruct((D_MODEL, 2 * W_C), F32)],
        scratch_shapes=[pltpu.VMEM((MEM_LEN, 2 * W_C), F32)],
        compiler_params=_cparams(("arbitrary",)),
    )(qc, mkv, dob, mem)


def _headsum(t, e):
    if MXU == F32:
        return _dot(t, e)
    hi = t.astype(MXU)
    lo = (t - hi.astype(F32)).astype(MXU)
    return _dot(hi, e) + _dot(lo, e)


def _post(olse_a, olse_b, olse_c, z, x2, tgt, g, gain, bias, w_out, hsum):
    T = x2.shape[0]
    tm = 256
    nt = SEQ // tm

    def body(oa_ref, ob_ref, oc_ref, z_ref, x_ref, t_ref, g_ref, gain_ref, bias_ref, w_ref,
             e_ref, gx_ref, doba_ref, dobb_ref, dobc_ref, dz_ref, dw_ref, small_ref, loss_ref):
        @pl.when(pl.program_id(0) == 0)
        def _():
            dw_ref[...] = jnp.zeros_like(dw_ref)
            small_ref[...] = jnp.zeros_like(small_ref)
            loss_ref[...] = jnp.zeros_like(loss_ref)

        oa = oa_ref[:, 0:W_A]
        lse_a = oa_ref[:, W_A:2 * W_A]
        (o1, l1), (o4, l4), (o16, l16) = [
            (_lanes([ob_ref[p, 0], ob_ref[p, 1]]), _lanes([ob_ref[p, 2], ob_ref[p, 3]])) for p in range(3)]
        mx = jnp.maximum(jnp.maximum(l1, l4), l16)
        e1, e4, e16 = jnp.exp(l1 - mx), jnp.exp(l4 - mx), jnp.exp(l16 - mx)
        den = e1 + e4 + e16
        ob = (e1 * o1 + e4 * o4 + e16 * o16) / den
        lse_b = mx + jnp.log(den)
        oc = oc_ref[:, 0:W_C]
        lse_c = oc_ref[:, W_C:2 * W_C]

        def rms(o):
            rr = lax.rsqrt(jnp.mean(o * o, axis=1, keepdims=True) + RMS_EPS)
            return o * rr, rr

        na, ra = rms(oa)
        nb, rb = rms(ob)
        nc, rc = rms(oc)
        n = jnp.concatenate([na, nb, nc], axis=1)
        zz = z_ref[...]
        sig = 1.0 / (1.0 + jnp.exp(-zz))
        sz = zz * sig
        gg = g_ref[...]
        u = n * gg * sz
        ub = u.astype(MXU)
        w = w_ref[...]
        r = ALPHA * x_ref[...] + _dot(ub, w)
        mu = jnp.mean(r, axis=1, keepdims=True)
        rc0 = r - mu
        rstd = lax.rsqrt(jnp.mean(rc0 * rc0, axis=1, keepdims=True) + LN_EPS)
        xhat = rc0 * rstd
        gain_v = gain_ref[...]
        err = xhat * gain_v + bias_ref[...] - t_ref[...]
        loss_ref[...] += 0.5 * jnp.sum(err * err) * (1.0 / D_MODEL)

        dout = err * (1.0 / D_MODEL)
        small_ref[0:1, :] += jnp.sum(dout * xhat, axis=0, keepdims=True)
        small_ref[1:2, :] += jnp.sum(dout, axis=0, keepdims=True)
        dxh = dout * gain_v
        dr = rstd * (dxh - jnp.mean(dxh, axis=1, keepdims=True)
                     - xhat * jnp.mean(dxh * xhat, axis=1, keepdims=True))
        gx_ref[...] = ALPHA * dr
        drb = dr.astype(MXU)
        du = _dot_nt(drb, w)
        dw_ref[...] += _dot_tn(u, drb)
        small_ref[2:3, :] += jnp.sum(du * n * sz, axis=0, keepdims=True)
        dz = du * n * gg * (sig * (1.0 + zz * (1.0 - sig)))
        dz_ref[...] = dz.astype(MXU)
        small_ref[3:4, :] += jnp.sum(dz, axis=0, keepdims=True)
        dn = du * gg * sz

        def branch(lo, hi, o, nbr, rr, lse, out_ref):
            wd = hi - lo
            dnb = dn[:, lo:hi]
            dob = rr * (dnb - nbr * jnp.mean(dnb * nbr, axis=1, keepdims=True))
            parts = (dob, lse, _headsum(dob * o, e_ref[0:wd, 0:wd]))
            if len(out_ref.shape) == 3:
                for j, t in enumerate(parts):
                    for c in range(wd // 128):
                        out_ref[j * (wd // 128) + c] = t[:, c * 128:(c + 1) * 128]
            else:
                for j, t in enumerate(parts):
                    out_ref[:, j * wd:(j + 1) * wd] = t

        branch(0, W_A, oa, na, ra, lse_a, doba_ref)
        branch(W_A, W_A + W_B, ob, nb, rb, lse_b, dobb_ref)
        branch(W_A + W_B, D_MIX, oc, nc, rc, lse_c, dobc_ref)

    row = lambda w: pl.BlockSpec((tm, w), lambda i: (i, 0))
    full = lambda a, b: pl.BlockSpec((a, b), lambda i: (0, 0))
    return pl.pallas_call(
        body, name="post_fwd_bwd", grid=(T // tm,),
        in_specs=[row(2 * W_A), pl.BlockSpec((None, 3, 4, tm, 128), lambda i: (i // nt, 0, 0, i % nt, 0)),
                  row(2 * W_C),
                  row(D_MIX), row(D_MODEL), row(D_MODEL),
                  full(1, D_MIX), full(1, D_MODEL), full(1, D_MODEL), full(D_MIX, D_MODEL), full(W_A, W_A)],
        out_specs=[row(D_MODEL), row(3 * W_A), pl.BlockSpec((None, 6, tm, 128), lambda i: (i // nt, 0, i % nt, 0)),
                   row(3 * W_C), row(D_MIX),
                   full(D_MIX, D_MODEL), full(8, D_MODEL), full(8, 128)],
        out_shape=[jax.ShapeDtypeStruct((T, D_MODEL), F32),
                   jax.ShapeDtypeStruct((T, 3 * W_A), F32),
                   jax.ShapeDtypeStruct((T // SEQ, 6, SEQ, 128), F32),
                   jax.ShapeDtypeStruct((T, 3 * W_C), F32),
                   jax.ShapeDtypeStruct((T, D_MIX), MXU),
                   jax.ShapeDtypeStruct((D_MIX, D_MODEL), F32),
                   jax.ShapeDtypeStruct((8, D_MODEL), F32),
                   jax.ShapeDtypeStruct((8, 128), F32)],
        compiler_params=_cparams(("arbitrary",)),
    )(olse_a, olse_b, olse_c, z, x2, tgt, g, gain, bias, w_out, hsum)


def _dh_build(dqkva, dqkvb, dqc, dz, tab):
    T = dz.shape[0]
    tm = 512
    nt = SEQ // tm
    HQ = D_IN - D_MIX

    def body(da_ref, db6_ref, dqc_ref, dz_ref, tab_ref, dh_ref, db_ref, nat_ref):
        @pl.when(pl.program_id(0) == 0)
        def _():
            db_ref[...] = jnp.zeros_like(db_ref)
        tab = tab_ref[...]
        parts = [_rope(da_ref[:, 0:512], tab, -1) * Q_SCALE,
                 _rope(da_ref[:, 512:640], tab, -1),
                 da_ref[:, 640:768],
                 _rope(_lanes([db6_ref[0], db6_ref[1]]), tab, -1) * Q_SCALE,
                 _rope(_lanes([db6_ref[2], db6_ref[3]]), tab, -1),
                 _lanes([db6_ref[4], db6_ref[5]]),
                 dqc_ref[...] * Q_SCALE]
        dhq = jnp.concatenate(parts, axis=1)
        db_ref[0:1, :] += jnp.sum(dhq, axis=0, keepdims=True)
        nat_ref[:, 0:HQ] = dhq.astype(MXU)
        nat_ref[:, HQ:D_IN] = dz_ref[...]
        for j in range(N_DEV):
            q = _owner_order_pos(j)
            dh_ref[:, q * COLS_PER_DEV:(q + 1) * COLS_PER_DEV] = nat_ref[:, j * COLS_PER_DEV:(j + 1) * COLS_PER_DEV]

    row = lambda w: pl.BlockSpec((tm, w), lambda i: (i, 0))
    return pl.pallas_call(
        body, name="dh_build", grid=(T // tm,),
        in_specs=[row(768), pl.BlockSpec((None, 6, tm, 128), lambda i: (i // nt, 0, i % nt, 0)),
                  row(256), row(D_MIX), pl.BlockSpec((tm, 384), lambda i: (i % nt, 0))],
        out_specs=[row(D_IN), pl.BlockSpec((8, HQ), lambda i: (0, 0))],
        out_shape=[jax.ShapeDtypeStruct((T, D_IN), MXU), jax.ShapeDtypeStruct((8, HQ), F32)],
        scratch_shapes=[pltpu.VMEM((tm, D_IN), MXU)],
        compiler_params=_cparams(("arbitrary",)),
    )(dqkva, dqkvb, dqc, dz, tab)


TAIL_TK = 512
TAIL_TN = D_IN // 2
TAIL_TM = 256
REDUCE_ROWS = 128


def _tail(x2, dh, gx1, w_in, sends, owns, small_g):
    T = x2.shape[0]
    kt = T // TAIL_TK
    ndw = (D_IN // TAIL_TN) * kt
    nsteps = ndw + T // TAIL_TM
    per_pass = TAIL_TN // COLS_PER_DEV
    pay = dh.dtype
    shapes = [(D_MODEL, COLS_PER_DEV), owns[0].shape, owns[1].shape, small_g.shape]
    n_arr = len(shapes)

    def body(x_ref, dh1_ref, dh2_ref, gx_ref, w_hbm, smem_ref, sout_ref, omem_ref, oout_ref, sg_ref,
             dx_ref, gin_ref, gmem_ref, gout_ref, gsm_ref,
             acc_ref, w_ref, stage_ref, ownin_ref, lin_ref, lmem_ref, lout_ref, lsm_ref,
             send_sems, recv_sems, w_sem):
        s = pl.program_id(0)
        x, y, c = _my_pos()
        me = 4 * x + 2 * y + c
        lands = (lin_ref, lmem_ref, lout_ref, lsm_ref)

        def src_of(a, j):
            return (stage_ref.at[j], smem_ref.at[j], sout_ref.at[j], sg_ref)[a]

        def to_peer(a, j):
            return pltpu.make_async_remote_copy(
                src_ref=src_of(a, j), dst_ref=lands[a].at[me], send_sem=send_sems.at[a, j],
                recv_sem=recv_sems.at[a, me], device_id=_dev_coords(j), device_id_type=MESH)

        def from_peer(a, m):
            return pltpu.make_async_remote_copy(
                src_ref=lands[a].at[m], dst_ref=lands[a].at[m], send_sem=send_sems.at[a, m],
                recv_sem=recv_sems.at[a, m], device_id=_dev_coords(m), device_id_type=MESH)

        w_copy = pltpu.make_async_copy(w_hbm, w_ref, w_sem)

        @pl.when(s == 0)
        def _():
            w_copy.start()
            for j in range(N_DEV):
                @pl.when(me != j)
                def _(j=j):
                    for a in range(1, n_arr):
                        to_peer(a, j).start()
            for a in range(n_arr - 1):
                lands[a][me] = jnp.zeros(shapes[a], lands[a].dtype)
            lsm_ref[me] = sg_ref[...]

        @pl.when(s < ndw)
        def _():
            k = s % kt

            @pl.when(k == 0)
            def _():
                acc_ref[...] = jnp.zeros_like(acc_ref)
            acc_ref[...] += _dot_tn(x_ref[...], dh1_ref[...])

            for p in range(D_IN // TAIL_TN):
                @pl.when(s == p * kt + kt - 1)
                def _(p=p):
                    for jj in range(per_pass):
                        j = [o for o in range(N_DEV) if _owner_order_pos(o) == p * per_pass + jj][0]
                        blk = acc_ref[:, jj * COLS_PER_DEV:(jj + 1) * COLS_PER_DEV]
                        stage_ref[j] = blk.astype(pay)

                        @pl.when(me == j)
                        def _(blk=blk):
                            ownin_ref[...] = blk
                        pl.when(me != j)(to_peer(0, j).start)

        @pl.when(s >= ndw)
        def _():
            pl.when(s == ndw)(w_copy.wait)
            dx_ref[...] = _dot_nt(dh2_ref[...], w_ref[...]) + gx_ref[...]

        @pl.when(s == nsteps - 1)
        def _():
            for m in range(N_DEV):
                @pl.when(me != m)
                def _(m=m):
                    for a in range(n_arr):
                        from_peer(a, m).wait_recv()
            for j in range(N_DEV):
                @pl.when(me != j)
                def _(j=j):
                    for a in range(n_arr):
                        to_peer(a, j).wait_send()
            for a, (own, out) in enumerate(((ownin_ref, gin_ref), (omem_ref, gmem_ref), (oout_ref, gout_ref))):
                def chunk(i, carry, a=a, own=own, out=out):
                    rs = pl.ds(pl.multiple_of(i * REDUCE_ROWS, REDUCE_ROWS), REDUCE_ROWS)
                    g = own[rs, :]
                    for m in range(N_DEV):
                        g = g + lands[a][m, rs, :].astype(F32)
                    out[rs, :] = g
                    return carry
                lax.fori_loop(0, shapes[a][0] // REDUCE_ROWS, chunk, 0)
            g = lsm_ref[0]
            for m in range(1, N_DEV):
                g = g + lsm_ref[m]
            gsm_ref[...] = g

    dw_step = lambda s: jnp.minimum(s, ndw - 1)
    dx_step = lambda s: jnp.maximum(s - ndw, 0)
    any_spec = pl.BlockSpec(memory_space=pl.ANY)
    vmem = pl.BlockSpec(memory_space=pltpu.VMEM)
    scratch = [pltpu.VMEM((D_MODEL, TAIL_TN), F32), pltpu.VMEM((D_MODEL, D_IN), w_in.dtype),
               pltpu.VMEM((N_DEV,) + shapes[0], pay), pltpu.VMEM(shapes[0], F32)]
    scratch += [pltpu.VMEM((N_DEV,) + shapes[a], pay) for a in range(n_arr - 1)]
    scratch += [pltpu.VMEM((N_DEV,) + shapes[-1], F32),
                pltpu.SemaphoreType.DMA((n_arr, N_DEV)), pltpu.SemaphoreType.DMA((n_arr, N_DEV)),
                pltpu.SemaphoreType.DMA]
    return pl.pallas_call(
        body, name="tail_dw_dx_reduce", grid=(nsteps,),
        in_specs=[pl.BlockSpec((TAIL_TK, D_MODEL), lambda s: (dw_step(s) % kt, 0)),
                  pl.BlockSpec((TAIL_TK, TAIL_TN), lambda s: (dw_step(s) % kt, dw_step(s) // kt)),
                  pl.BlockSpec((TAIL_TM, D_IN), lambda s: (dx_step(s), 0)),
                  pl.BlockSpec((TAIL_TM, D_MODEL), lambda s: (dx_step(s), 0)),
                  any_spec, any_spec, any_spec, vmem, vmem, vmem],
        out_specs=[pl.BlockSpec((TAIL_TM, D_MODEL), lambda s: (dx_step(s), 0)), vmem, vmem, vmem, vmem],
        out_shape=[jax.ShapeDtypeStruct((T, D_MODEL), F32)] + [jax.ShapeDtypeStruct(sh, F32) for sh in shapes],
        scratch_shapes=scratch,
        compiler_params=_cparams(("arbitrary",)),
    )(x2, dh, dh, gx1, w_in, *sends, *owns, small_g)


def _adam_update(grads, params, carried):
    n = len(grads)

    def body(*refs):
        g_refs, p_refs, o_refs = refs[1:1 + n], refs[1 + n:1 + 4 * n], refs[2 + 4 * n:]
        for a in range(n):
            rows = g_refs[a].shape[0]
            cr = REDUCE_ROWS if rows % REDUCE_ROWS == 0 else rows
            w_ref, m_ref, v_ref = p_refs[3 * a:3 * a + 3]
            go_ref, d_ref, nm_ref, nv_ref = o_refs[4 * a:4 * a + 4]

            def chunk(i, carry, cr=cr, g_ref=g_refs[a], w_ref=w_ref, m_ref=m_ref, v_ref=v_ref,
                      go_ref=go_ref, d_ref=d_ref, nm_ref=nm_ref, nv_ref=nv_ref):
                rs = pl.ds(pl.multiple_of(i * cr, cr), cr)
                g = g_ref[rs, :]
                go_ref[rs, :] = g
                d_ref[rs, :], nm_ref[rs, :], nv_ref[rs, :] = _adamw(w_ref[rs, :], g, m_ref[rs, :], v_ref[rs, :])
                return carry
            lax.fori_loop(0, rows // cr, chunk, 0)

    vmem = pl.BlockSpec(memory_space=pltpu.VMEM)
    any_spec = pl.BlockSpec(memory_space=pl.ANY)
    flat = [p for grp in params for p in grp]
    outs = pl.pallas_call(
        body, name="adamw", in_specs=[any_spec] + [vmem] * (4 * n), out_specs=[any_spec] + [vmem] * (4 * n),
        out_shape=[jax.ShapeDtypeStruct(carried.shape, carried.dtype)]
        + [jax.ShapeDtypeStruct(g.shape, F32) for g in grads for _ in range(4)],
        input_output_aliases={0: 0},
        compiler_params=pltpu.CompilerParams(vmem_limit_bytes=VMEM_LIMIT),
    )(carried, *grads, *flat)
    return [outs[1 + 4 * a:5 + 4 * a] for a in range(n)], outs[0]


def _step(x, mem, w_in_s, w_mem_s, w_out_s, b_in, sinks, g, gain, bias, tgt):
    B = x.shape[0]
    T = B * SEQ
    x2 = x.reshape(T, D_MODEL)
    t2 = tgt.reshape(T, D_MODEL)
    tab = _rope_table()
    lane = jnp.arange(W_A)
    hsum = (lane[:, None] // HEAD == lane[None, :] // HEAD).astype(MXU)
    me = 4 * lax.axis_index("x") + 2 * lax.axis_index("y") + lax.axis_index("c")

    (w_in_all,) = _gather_weights([w_in_s])
    qkva, qkvb, qc, z, w_in_perm, w_mem_all, w_out_all = _in_proj(x2, w_in_all, b_in, tab, [w_mem_s, w_out_s])
    w_mem = w_mem_all.reshape(D_MODEL, 2 * W_C)
    w_out = w_out_all.reshape(D_MIX, D_MODEL)

    def per_ex(a):
        return a.reshape(B, SEQ, a.shape[-1])

    def flat(a):
        return a.reshape(T, a.shape[-1])

    qkva3, qc3 = per_ex(qkva), per_ex(qc)
    olse_a = _attn_fwd(qkva3, sinks, d=1, spg=1, unroll=3, hq=SWA_Q, hkv=SWA_KV, max_dist=BLK - 1,
                       name="swa_fwd")
    olse_b = _dil_fwd(qkvb)
    olse_c, mkv = _mem_attn_fwd(qc3, mem, w_mem)

    gx1, doba, dobb, dobc, dz, dw_out, small, loss = _post(
        flat(olse_a), olse_b, flat(olse_c), z, x2, t2, g, gain, bias, w_out, hsum)

    dqkva, dsink = _attn_bwd(qkva3, per_ex(doba), sinks, d=1, spg=1, unroll=3, hq=SWA_Q, hkv=SWA_KV,
                             max_dist=BLK - 1, name="swa_bwd")
    dqkvb = _dil_bwd(qkvb, dobb)
    dqc, dw_mem = _mem_attn_bwd(qc3, mkv, per_ex(dobc), mem)
    dh, dbq = _dh_build(flat(dqkva), dqkvb, flat(dqc), dz, tab)

    small_g = _pack_small(dict(b_in=jnp.concatenate([dbq[0], small[3]]), sinks=dsink[:, 0], g=small[2],
                               gain=small[0], bias=small[1], loss=loss[0, 0]))
    blocks = [dw_mem.reshape(N_DEV, ROWS_PER_DEV, 2 * W_C), dw_out.reshape(N_DEV, ROWS_PER_DEV, D_MODEL)]
    sends = [b.astype(MXU) for b in blocks]
    owns = [lax.dynamic_index_in_dim(b, me, axis=0, keepdims=False) for b in blocks]
    grad_x, g_in, g_mem, g_out, g_small = _tail(x2, dh, gx1, w_in_perm, sends, owns, small_g)
    return grad_x.reshape(B, SEQ, D_MODEL), g_in, g_mem, g_out, g_small


def _my_pos():
    return lax.axis_index("x"), lax.axis_index("y"), lax.axis_index("c")


def _gather_weights(shards):
    n_arr = len(shards)

    def body(*refs):
        ins, outs = refs[0:n_arr], refs[n_arr:2 * n_arr]
        send_sems, recv_sems, local_sems = refs[2 * n_arr:]
        x, y, c = _my_pos()
        me, sibling = (x, y, c), (x, y, 1 - c)
        chips = [(1 - x, y), (x, 1 - y), (1 - x, 1 - y)]

        def slot(a, pos):
            return outs[a].at[4 * pos[0] + 2 * pos[1] + pos[2]]

        def copy(a, k, block, to, src=None):
            return pltpu.make_async_remote_copy(
                src_ref=slot(a, block) if src is None else src, dst_ref=slot(a, block),
                send_sem=send_sems.at[a, k], recv_sem=recv_sems.at[a, k],
                device_id=to, device_id_type=MESH)

        mine = [pltpu.make_async_copy(ins[a], slot(a, me), local_sems.at[a]) for a in range(n_arr)]
        for cp in mine:
            cp.start()
        first = []
        for a in range(n_arr):
            first.append(copy(a, 0, me, sibling, src=ins[a]))
            first += [copy(a, 1 + j, me, (*chip, c), src=ins[a]) for j, chip in enumerate(chips)]
        for cp in first:
            cp.start()
        passed = []
        for j, chip in enumerate(chips):
            for a in range(n_arr):
                copy(a, 1 + j, (*chip, c), me).wait_recv()
                fwd = copy(a, 4 + j, (*chip, c), sibling)
                fwd.start()
                passed.append(fwd)
        for a in range(n_arr):
            copy(a, 0, sibling, me).wait_recv()
            for j, chip in enumerate(chips):
                copy(a, 4 + j, (*chip, 1 - c), me).wait_recv()
        for cp in first + passed:
            cp.wait_send()
        for cp in mine:
            cp.wait()

    any_spec = pl.BlockSpec(memory_space=pl.ANY)
    return pl.pallas_call(
        body, name="gather_weights",
        in_specs=[any_spec] * n_arr, out_specs=[any_spec] * n_arr,
        out_shape=[jax.ShapeDtypeStruct((N_DEV,) + s.shape, s.dtype) for s in shards],
        scratch_shapes=[pltpu.SemaphoreType.DMA((n_arr, 7)), pltpu.SemaphoreType.DMA((n_arr, 7)),
                        pltpu.SemaphoreType.DMA((n_arr,))],
    )(*shards)


def _adamw(w, g, m, v):
    m = ADAM_B1 * m + (1.0 - ADAM_B1) * g
    v = ADAM_B2 * v + (1.0 - ADAM_B2) * (g * g)
    m_hat = m / (1.0 - ADAM_B1 ** ADAM_STEP)
    v_hat = v / (1.0 - ADAM_B2 ** ADAM_STEP)
    delta = -ADAM_LR * (m_hat / (jnp.sqrt(v_hat) + ADAM_EPS) + ADAM_WD * w)
    return delta, m, v


_SMALL_SIZES = (("b_in", D_IN), ("g", D_MIX), ("gain", D_MODEL), ("bias", D_MODEL), ("sinks", SWA_Q), ("loss", 1))


def _pack_small(d):
    flat = jnp.concatenate([jnp.reshape(d[k], (-1,)).astype(F32) if k in d else jnp.zeros((n,), F32)
                            for k, n in _SMALL_SIZES])
    flat = jnp.pad(flat, (0, SMALL_ROWS * 128 - flat.shape[0]))
    return flat.reshape(SMALL_ROWS, 128)


def _unpack_small(p):
    flat = p.reshape(-1)
    out, off = {}, 0
    for k, n in _SMALL_SIZES:
        out[k] = flat[off:off + n].reshape(1, n)
        off += n
    return out


def kernel(x, mem, w_in, b_in, w_mem, attn_sinks, g_branch, w_out, ln_gain, ln_bias, loss_target, m_w_in, m_b_in, m_w_mem, m_attn_sinks, m_g_branch, m_w_out, m_ln_gain, m_ln_bias, v_w_in, v_b_in, v_w_mem, v_attn_sinks, v_g_branch, v_w_out, v_ln_gain, v_ln_bias):
    grad_x, g_in, g_mem, g_out, g_small = _step(
        x, mem, w_in[0].astype(MXU), w_mem[0].astype(MXU), w_out[0].astype(MXU), b_in, attn_sinks[0],
        g_branch, ln_gain, ln_bias, loss_target)

    small_w = _pack_small(dict(b_in=b_in, g=g_branch, gain=ln_gain, bias=ln_bias, sinks=attn_sinks))
    small_m = _pack_small(dict(b_in=m_b_in, g=m_g_branch, gain=m_ln_gain, bias=m_ln_bias, sinks=m_attn_sinks))
    small_v = _pack_small(dict(b_in=v_b_in, g=v_g_branch, gain=v_ln_gain, bias=v_ln_bias, sinks=v_attn_sinks))
    grads = [g_in, g_mem, g_out, g_small]
    params = [(w_in[0], m_w_in[0], v_w_in[0]), (w_mem[0], m_w_mem[0], v_w_mem[0]),
              (w_out[0], m_w_out[0], v_w_out[0]), (small_w, small_m, small_v)]
    res, grad_x = _adam_update(grads, params, grad_x)
    big = [[r[None] for r in res[a]] for a in range(3)]
    sm = [_unpack_small(r) for r in res[3]]

    def group(i):
        return (big[0][i], sm[i]["b_in"], big[1][i], sm[i]["sinks"], sm[i]["g"], big[2][i],
                sm[i]["gain"], sm[i]["bias"])

    loss = sm[0]["loss"].reshape(())
    return (loss, grad_x, *group(0), *group(1), *group(2), *group(3))
```

```python
import functools
import math

import jax
import jax.numpy as jnp
from jax import lax
from jax.experimental import pallas as pl
from jax.experimental.pallas import tpu as pltpu

F32 = jnp.float32
MXU = jnp.bfloat16

D_MODEL = 1024
SEQ = 2048
HEAD = 64
BLK = 128
SWA_Q, SWA_KV = 8, 2
DIL_H = 4
MEM_H = 4
MEM_LEN = 256
W_A, W_KVA, W_B, W_C = 512, 128, 256, 256
D_MIX = 1024
D_IN = 2816
N_DEV = 8
COLS_PER_DEV = D_IN // N_DEV
ROWS_PER_DEV = D_MODEL // N_DEV
ROPE_THETA = 10000.0
LN_EPS = 1e-5
RMS_EPS = 1e-6
ALPHA = 2.0 ** 0.25
Q_SCALE = HEAD ** -0.5
NEG = -1e30
SMALL_ROWS = 48
VMEM_LIMIT = 56 * 1024 * 1024

ADAM_LR = 0.001
ADAM_B1 = 0.9
ADAM_B2 = 0.999
ADAM_EPS = 1e-08
ADAM_WD = 0.01
ADAM_STEP = 10

MESH = pl.DeviceIdType.MESH


def _cparams(sem=None):
    return pltpu.CompilerParams(dimension_semantics=sem, vmem_limit_bytes=VMEM_LIMIT)


def _dot(a, b):
    return jnp.dot(a, b, preferred_element_type=F32)


def _dot_nt(a, b):
    return lax.dot_general(a, b, (((1,), (1,)), ((), ())), preferred_element_type=F32)


def _dot_t0(a, b):
    return lax.dot_general(a, b, (((0,), (0,)), ((), ())), preferred_element_type=F32)


def _dot_tn(a, b):
    return jnp.dot(a.T.astype(MXU), b, preferred_element_type=F32)


def _rope(t, tab, sign):
    cos, sa, sb = tab[:, 0:128], tab[:, 128:256], tab[:, 256:384]
    outs = []
    for c in range(t.shape[1] // 128):
        tc = t[:, c * 128:(c + 1) * 128]
        r = pltpu.roll(tc, 96, 1) * sa + pltpu.roll(tc, 32, 1) * sb
        outs.append(tc * cos + r if sign > 0 else tc * cos - r)
    return outs[0] if len(outs) == 1 else jnp.concatenate(outs, axis=1)


def _rope_table():
    pos = jnp.arange(SEQ, dtype=F32)
    inv = ROPE_THETA ** (-jnp.arange(0, HEAD, 2, dtype=F32) / HEAD)
    ang = pos[:, None] * inv[None, :]
    ang = jnp.concatenate([ang, ang], axis=-1)
    cos, sin = jnp.cos(ang), jnp.sin(ang)
    lane = jnp.arange(HEAD)[None, :]
    sa = jnp.where(lane < HEAD // 2, -sin, 0.0)
    sb = jnp.where(lane >= HEAD // 2, sin, 0.0)
    two = lambda t: jnp.concatenate([t, t], axis=-1)
    return jnp.concatenate([two(cos), two(sa), two(sb)], axis=-1).astype(F32)


def _dev_coords(j):
    return (j >> 2, (j >> 1) & 1, j & 1)


def _owner_order_pos(j):
    return (j % 2) * (N_DEV // 2) + j // 2


def _in_proj(x2, w_all, b_in, tab, late_shards):
    T = x2.shape[0]
    tm = 512
    n_late = len(late_shards)

    def body(x_ref, wall_ref, b_ref, tab_ref, *rest):
        late_in, rest = rest[:n_late], rest[n_late:]
        qkva_ref, qkvb_ref, qc_ref, z_ref, wperm_ref, xt_ref = rest[:6]
        late_out = rest[6:6 + n_late]
        w_ref, send_sems, recv_sems, local_sems = rest[6 + n_late:]
        step, last = pl.program_id(0), pl.num_programs(0) - 1
        x, y, c = _my_pos()
        me = 4 * x + 2 * y + c

        def to_peer(a, j):
            return pltpu.make_async_remote_copy(
                src_ref=late_in[a], dst_ref=late_out[a].at[me], send_sem=send_sems.at[a, j],
                recv_sem=recv_sems.at[a, me], device_id=_dev_coords(j), device_id_type=MESH)

        def from_peer(a, m):
            return pltpu.make_async_remote_copy(
                src_ref=late_out[a].at[m], dst_ref=late_out[a].at[m], send_sem=send_sems.at[a, m],
                recv_sem=recv_sems.at[a, m], device_id=_dev_coords(m), device_id_type=MESH)

        def mine(a):
            return pltpu.make_async_copy(late_in[a], late_out[a].at[me], local_sems.at[a])

        @pl.when(step == 0)
        def _():
            for a in range(n_late):
                mine(a).start()
                for j in range(N_DEV):
                    pl.when(me != j)(to_peer(a, j).start)
            for j in range(N_DEV):
                w_ref[:, j * COLS_PER_DEV:(j + 1) * COLS_PER_DEV] = wall_ref[j]
                q = _owner_order_pos(j)
                wperm_ref[:, q * COLS_PER_DEV:(q + 1) * COLS_PER_DEV] = wall_ref[j]

        xb = x_ref[...].astype(MXU)
        xt_ref[...] = x_ref[...].T.astype(MXU)
        tab = tab_ref[...]

        def seg(c0, c1):
            return _dot(xb, w_ref[:, c0:c1]) + b_ref[:, c0:c1]

        qkva_ref[:, 0:512] = (_rope(seg(0, 512), tab, 1) * Q_SCALE).astype(MXU)
        qkva_ref[:, 512:640] = _rope(seg(512, 640), tab, 1).astype(MXU)
        qkva_ref[:, 640:768] = seg(640, 768).astype(MXU)
        qkvb = (_rope(seg(768, 1024), tab, 1) * Q_SCALE, _rope(seg(1024, 1280), tab, 1), seg(1280, 1536))
        for j, t in enumerate(qkvb):
            for c in range(2):
                qkvb_ref[2 * j + c] = t[:, c * 128:(c + 1) * 128]
        qc_ref[...] = (seg(1536, 1792) * Q_SCALE).astype(MXU)
        z_ref[...] = seg(1792, 2816)

        @pl.when(step == last)
        def _():
            for a in range(n_late):
                mine(a).wait()
                for m in range(N_DEV):
                    pl.when(me != m)(from_peer(a, m).wait_recv)
                for j in range(N_DEV):
                    pl.when(me != j)(to_peer(a, j).wait_send)

    nt = SEQ // tm
    any_spec = pl.BlockSpec(memory_space=pl.ANY)
    return pl.pallas_call(
        body, name="in_proj_fwd",
        grid=(T // tm,),
        in_specs=[pl.BlockSpec((tm, D_MODEL), lambda i: (i, 0)),
                  pl.BlockSpec((N_DEV, D_MODEL, COLS_PER_DEV), lambda i: (0, 0, 0)),
                  pl.BlockSpec((1, D_IN), lambda i: (0, 0)),
                  pl.BlockSpec((tm, 384), lambda i: (i % nt, 0))] + [any_spec] * n_late,
        out_specs=[pl.BlockSpec((tm, 768), lambda i: (i, 0)),
                   pl.BlockSpec((None, 6, tm, 128), lambda i: (i // nt, 0, i % nt, 0)),
                   pl.BlockSpec((tm, 256), lambda i: (i, 0)),
                   pl.BlockSpec((tm, D_MIX), lambda i: (i, 0)),
                   pl.BlockSpec((D_MODEL, D_IN), lambda i: (0, 0)),
                   pl.BlockSpec((D_MODEL, tm), lambda i: (0, i))] + [any_spec] * n_late,
        out_shape=[jax.ShapeDtypeStruct((T, 768), MXU), jax.ShapeDtypeStruct((T // SEQ, 6, SEQ, 128), F32),
                   jax.ShapeDtypeStruct((T, 256), MXU), jax.ShapeDtypeStruct((T, D_MIX), F32),
                   jax.ShapeDtypeStruct((D_MODEL, D_IN), w_all.dtype),
                   jax.ShapeDtypeStruct((D_MODEL, T), MXU)]
        + [jax.ShapeDtypeStruct((N_DEV,) + s.shape, s.dtype) for s in late_shards],
        scratch_shapes=[pltpu.VMEM((D_MODEL, D_IN), w_all.dtype),
                        pltpu.SemaphoreType.DMA((n_late, N_DEV)), pltpu.SemaphoreType.DMA((n_late, N_DEV)),
                        pltpu.SemaphoreType.DMA((n_late,))],
        compiler_params=_cparams(("arbitrary",)),
    )(x2, w_all, b_in, tab, *late_shards)


CHAIN = 4


def _band_bias(max_dist):
    kj = lax.broadcasted_iota(jnp.int32, (2 * BLK, BLK), 0)
    qi = lax.broadcasted_iota(jnp.int32, (2 * BLK, BLK), 1)
    dist = qi + BLK - kj
    band = jnp.where((dist >= 0) & (dist <= max_dist), 0.0, NEG).astype(F32)
    k1 = lax.broadcasted_iota(jnp.int32, (BLK, BLK), 0)
    q1 = lax.broadcasted_iota(jnp.int32, (BLK, BLK), 1)
    first = jnp.where((q1 - k1 >= 0) & (q1 - k1 <= max_dist), 0.0, NEG).astype(F32)
    return jnp.concatenate([band] * CHAIN, axis=1), jnp.concatenate([first] * CHAIN, axis=1)


def _lanes(parts):
    return jnp.concatenate(parts, axis=1)


def _col(off, h):
    return slice(off + h * HEAD, off + (h + 1) * HEAD)


def _pair_rows(ref, rows, off, h0):
    t = ref[rows, slice(off + h0 * HEAD, off + (h0 + 2) * HEAD)].T
    return t[0:1, :], t[HEAD:HEAD + 1, :]


def _store_o_lse(o_ref, rows, oT, lse, ocols, lcols):
    n = len(ocols)
    res = jnp.concatenate([oT, jnp.broadcast_to(lse, (HEAD, lse.shape[1]))], axis=0).T
    rq = res.shape[0] // n
    for i in range(n):
        o_ref[rows[i], ocols[i]] = res[i * rq:(i + 1) * rq, 0:HEAD]
        o_ref[rows[i], lcols[i]] = res[i * rq:(i + 1) * rq, HEAD:2 * HEAD]


def _attn_geometry(qkv, d, spg, hq, hkv):
    B, Ls, wall = qkv.shape
    qw, kw = hq * HEAD, hkv * HEAD
    wtot = qw + 2 * kw
    assert wall == d * wtot and Ls % BLK == 0 and d % spg == 0 and hq % CHAIN == 0
    G = hq // hkv
    assert G in (1, CHAIN)
    return B, Ls, qw, kw, wtot, G, Ls // BLK


def _attn_fwd(qkv, sinks, *, d, spg, unroll, hq, hkv, max_dist, name):
    B, Ls, qw, kw, wtot, G, nblk = _attn_geometry(qkv, d, spg, hq, hkv)
    use_sink = sinks is not None
    assert (nblk - 1) % unroll == 0
    trips = (nblk - 1) // unroll

    def body(*refs):
        if use_sink:
            sink_ref, qkv_ref, o_ref = refs
        else:
            qkv_ref, o_ref = refs
        band, first = _band_bias(max_dist)
        for st in range(spg):
            ib, ob = st * wtot, st * 2 * qw
            chains = []
            for ch in range(hq // CHAIN):
                heads = list(range(ch * CHAIN, (ch + 1) * CHAIN))
                sinkrow = _lanes([jnp.full((1, BLK), sink_ref[h], F32) for h in heads]) if use_sink else None

                def block(rows_q, rows_k, bias, heads=heads, ib=ib, ob=ob, sinkrow=sinkrow):
                    qs = [qkv_ref[rows_q, _col(ib, h)] for h in heads]
                    if G > 1:
                        hk = heads[0] // G
                        kk = qkv_ref[rows_k, _col(ib + qw, hk)]
                        vv = qkv_ref[rows_k, _col(ib + qw + kw, hk)]
                        sT = _dot_nt(kk, jnp.concatenate(qs, axis=0))
                    else:
                        vvs = [qkv_ref[rows_k, _col(ib + qw + kw, h)] for h in heads]
                        sT = _lanes([_dot_nt(qkv_ref[rows_k, _col(ib + qw, h)], qs[i])
                                     for i, h in enumerate(heads)])
                    sT = sT + bias
                    m = jnp.max(sT, axis=0, keepdims=True)
                    if use_sink:
                        m = jnp.maximum(m, sinkrow)
                    pT = jnp.exp(sT - m)
                    l = jnp.sum(pT, axis=0, keepdims=True)
                    if use_sink:
                        l = l + jnp.exp(sinkrow - m)
                    pnT = (pT * (1.0 / l)).astype(MXU)
                    if G > 1:
                        oT = _dot_t0(vv, pnT)
                    else:
                        oT = _lanes([_dot_t0(vvs[i], pnT[:, i * BLK:(i + 1) * BLK]) for i in range(CHAIN)])
                    _store_o_lse(o_ref, [rows_q] * CHAIN, oT, m + jnp.log(l),
                                 [_col(ob, h) for h in heads], [_col(ob + qw, h) for h in heads])

                chains.append(block)

            for block in chains:
                block(pl.ds(0, BLK), pl.ds(0, BLK), first)

            def blocks_at(i0, chains=chains):
                for u in range(unroll):
                    r0 = pl.multiple_of((i0 + u) * BLK, BLK)
                    rk = pl.multiple_of((i0 + u) * BLK - BLK, BLK)
                    for block in chains:
                        block(pl.ds(r0, BLK), pl.ds(rk, 2 * BLK), band)

            if trips == 1:
                blocks_at(1)
            elif trips > 1:
                def loop(j, carry, blocks_at=blocks_at):
                    blocks_at(1 + j * unroll)
                    return carry
                lax.fori_loop(0, trips, loop, 0)

    in_specs = [pl.BlockSpec((None, Ls, spg * wtot), lambda b, r: (b, 0, r))]
    args = [qkv]
    if use_sink:
        in_specs = [pl.BlockSpec(memory_space=pltpu.SMEM)] + in_specs
        args = [sinks] + args
    return pl.pallas_call(
        body, name=name, grid=(B, d // spg), in_specs=in_specs,
        out_specs=pl.BlockSpec((None, Ls, spg * 2 * qw), lambda b, r: (b, 0, r)),
        out_shape=jax.ShapeDtypeStruct((B, Ls, d * 2 * qw), F32),
        compiler_params=_cparams(("arbitrary", "arbitrary")),
    )(*args)


def _attn_bwd(qkv, dob, sinks, *, d, spg, unroll, hq, hkv, max_dist, name):
    B, Ls, qw, kw, wtot, G, nblk = _attn_geometry(qkv, d, spg, hq, hkv)
    assert dob.shape == (B, Ls, d * 3 * qw) and (nblk - 1) % unroll == 0
    trips = (nblk - 1) // unroll
    use_sink = sinks is not None

    def body(*refs):
        if use_sink:
            sink_ref, qkv_ref, dob_ref, dq_ref, dsink_ref = refs
        else:
            qkv_ref, dob_ref, dq_ref = refs
        band, first = _band_bias(max_dist)
        if use_sink:
            @pl.when((pl.program_id(0) == 0) & (pl.program_id(1) == 0))
            def _():
                dsink_ref[...] = jnp.zeros_like(dsink_ref)
        for st in range(spg):
            ib, db = st * wtot, st * 3 * qw
            dq_ref[:, ib + qw:ib + wtot] = jnp.zeros((Ls, 2 * kw), F32)
            chains = []
            for ch in range(hq // CHAIN):
                heads = list(range(ch * CHAIN, (ch + 1) * CHAIN))
                sinkrow = _lanes([jnp.full((1, BLK), sink_ref[h], F32) for h in heads]) if use_sink else None

                def block(rows_q, rows_k, bias, acc, heads=heads, ib=ib, db=db, sinkrow=sinkrow):
                    qs = [qkv_ref[rows_q, _col(ib, h)] for h in heads]
                    dos = [dob_ref[rows_q, _col(db, h)].astype(MXU) for h in heads]
                    lse = _lanes([r for j in range(0, CHAIN, 2)
                                  for r in _pair_rows(dob_ref, rows_q, db + qw, heads[j])])
                    delta = _lanes([r for j in range(0, CHAIN, 2)
                                    for r in _pair_rows(dob_ref, rows_q, db + 2 * qw, heads[j])])
                    if G > 1:
                        hk = heads[0] // G
                        kc, vc = _col(ib + qw, hk), _col(ib + qw + kw, hk)
                        kk, vv = qkv_ref[rows_k, kc], qkv_ref[rows_k, vc]
                        qst, dost = jnp.concatenate(qs, axis=0), jnp.concatenate(dos, axis=0)
                        sT, dpT = _dot_nt(kk, qst), _dot_nt(vv, dost)
                    else:
                        kks = [qkv_ref[rows_k, _col(ib + qw, h)] for h in heads]
                        vvs = [qkv_ref[rows_k, _col(ib + qw + kw, h)] for h in heads]
                        sT = _lanes([_dot_nt(kks[i], qs[i]) for i in range(CHAIN)])
                        dpT = _lanes([_dot_nt(vvs[i], dos[i]) for i in range(CHAIN)])
                    pT = jnp.exp(sT + bias - lse)
                    dsT = pT * (dpT - delta)
                    dsb, pb = dsT.astype(MXU), pT.astype(MXU)
                    if G > 1:
                        dq_ref[rows_k, kc] += _dot(dsb, qst)
                        dq_ref[rows_k, vc] += _dot(pb, dost)
                        dq = _dot_t0(kk, dsb).T
                        for i, h in enumerate(heads):
                            dq_ref[rows_q, _col(ib, h)] = dq[i * BLK:(i + 1) * BLK]
                    else:
                        dqT = []
                        for i, h in enumerate(heads):
                            ls = slice(i * BLK, (i + 1) * BLK)
                            dq_ref[rows_k, _col(ib + qw, h)] += _dot(dsb[:, ls], qs[i])
                            dq_ref[rows_k, _col(ib + qw + kw, h)] += _dot(pb[:, ls], dos[i])
                            dqT.append(_dot_t0(kks[i], dsb[:, ls]))
                        dq = _lanes(dqT).T
                        for i, h in enumerate(heads):
                            dq_ref[rows_q, _col(ib, h)] = dq[i * BLK:(i + 1) * BLK]
                    if use_sink:
                        acc = acc - jnp.exp(sinkrow - lse) * delta
                    return acc

                chains.append(block)

            accs = tuple(block(pl.ds(0, BLK), pl.ds(0, BLK), first, jnp.zeros((1, CHAIN * BLK), F32))
                         for block in chains)

            def blocks_at(i0, accs, chains=chains):
                for u in range(unroll):
                    r0 = pl.multiple_of((i0 + u) * BLK, BLK)
                    rk = pl.multiple_of((i0 + u) * BLK - BLK, BLK)
                    accs = tuple(block(pl.ds(r0, BLK), pl.ds(rk, 2 * BLK), band, acc)
                                 for block, acc in zip(chains, accs))
                return accs

            if trips == 1:
                accs = blocks_at(1, accs)
            elif trips > 1:
                def loop(j, accs, blocks_at=blocks_at):
                    return blocks_at(1 + j * unroll, accs)
                accs = lax.fori_loop(0, trips, loop, accs)
            if use_sink:
                for ch, acc in enumerate(accs):
                    for i in range(CHAIN):
                        h = ch * CHAIN + i
                        tot = jnp.sum(acc[:, i * BLK:(i + 1) * BLK], axis=1, keepdims=True)
                        dsink_ref[h:h + 1, :] += jnp.broadcast_to(tot, (1, 128))

    in_specs = [pl.BlockSpec((None, Ls, spg * wtot), lambda b, r: (b, 0, r)),
                pl.BlockSpec((None, Ls, spg * 3 * qw), lambda b, r: (b, 0, r))]
    out_specs = [pl.BlockSpec((None, Ls, spg * wtot), lambda b, r: (b, 0, r))]
    out_shape = [jax.ShapeDtypeStruct((B, Ls, d * wtot), F32)]
    args = [qkv, dob]
    if use_sink:
        in_specs = [pl.BlockSpec(memory_space=pltpu.SMEM)] + in_specs
        args = [sinks] + args
        out_specs.append(pl.BlockSpec((8, 128), lambda b, r: (0, 0)))
        out_shape.append(jax.ShapeDtypeStruct((8, 128), F32))
    return pl.pallas_call(
        body, name=name, grid=(B, d // spg), in_specs=in_specs, out_specs=out_specs, out_shape=out_shape,
        compiler_params=_cparams(("arbitrary", "arbitrary")),
    )(*args)


DILATIONS = (1, 4, 16)
DIL_PAIRS_H = DIL_H // 2


def _stream_rows(d, r, i, n):
    if d == 1:
        return pl.ds(pl.multiple_of(i * BLK, BLK), n)
    return pl.ds(r + i * (BLK * d), n, stride=d)


PICK_ROWS = 16


def _stack_pair(t):
    lo = (lax.broadcasted_iota(jnp.int32, t.shape, 1) < HEAD).astype(F32)
    return jnp.concatenate([t * lo, t * (1.0 - lo)], axis=0).astype(MXU)


def _split3(t):
    if MXU == F32:
        return (t,)
    hi = t.astype(MXU)
    r = t - hi.astype(F32)
    mid = r.astype(MXU)
    return hi, mid, (r - mid.astype(F32)).astype(MXU)


def _spread_matrix():
    row = lax.broadcasted_iota(jnp.int32, (PICK_ROWS, 128), 0)
    lane = lax.broadcasted_iota(jnp.int32, (PICK_ROWS, 128), 1)
    return ((row < 6) & ((row % 2 == 1) == (lane >= HEAD))).astype(MXU)


def _lanes_to_tokens(v0, v1, spread):
    n = v0.shape[1]
    row = lax.broadcasted_iota(jnp.int32, (PICK_ROWS, n), 0)
    a = jnp.zeros((PICK_ROWS, n), F32)
    for i, (p0, p1) in enumerate(zip(_split3(v0), _split3(v1))):
        a = jnp.where(row == 2 * i, p0.astype(F32), a)
        a = jnp.where(row == 2 * i + 1, p1.astype(F32), a)
    return _dot_t0(a.astype(MXU), spread)


def _tokens_to_lanes(t):
    r = t.T
    return r[0:1, :], r[HEAD:HEAD + 1, :]


def _interleave(tiles):
    tiles = list(tiles)
    while tiles:
        for t in list(tiles):
            try:
                next(t)
            except StopIteration:
                tiles.remove(t)


def _dil_schedule(body_first, body_next):
    for p, d in enumerate(DILATIONS):
        nblk = SEQ // d // BLK
        if d == 1:
            _interleave([body_first(p, d, 0)])
            def loop(j, c, p=p, d=d):
                _interleave([body_next(p, d, 0, 1 + 3 * j + u) for u in range(3)])
                return c
            lax.fori_loop(0, (nblk - 1) // 3, loop, 0)
        elif nblk > 1:
            def loop(r, c, p=p, d=d, nblk=nblk):
                _interleave([body_first(p, d, r)] + [body_next(p, d, r, i) for i in range(1, nblk)])
                return c
            lax.fori_loop(0, d, loop, 0)
        else:
            def loop(j, c, p=p, d=d):
                _interleave([body_first(p, d, 4 * j + u) for u in range(4)])
                return c
            lax.fori_loop(0, d // 4, loop, 0)


def _dil_fwd(qkvb):
    B = qkvb.shape[0]

    def body(qkv_ref, o_ref):
        band, first = _band_bias(BLK)
        spread = _spread_matrix()

        def block(p, d, rows_q, rows_k, bias):
            nk = bias.shape[0]
            sT = []
            for c in range(DIL_PAIRS_H):
                qc = qkv_ref.at[c][rows_q, :].astype(MXU)
                s2 = _dot_nt(_stack_pair(qkv_ref.at[DIL_PAIRS_H + c][rows_k, :]), qc)
                sT += [s2[0:nk], s2[nk:2 * nk]]
            yield
            sT = _lanes(sT) + bias
            m = jnp.max(sT, axis=0, keepdims=True)
            pT = jnp.exp(sT - m)
            l = jnp.sum(pT, axis=0, keepdims=True)
            pnT = (pT * (1.0 / l)).astype(MXU)
            lse = m + jnp.log(l)
            yield
            for c in range(DIL_PAIRS_H):
                q0, q1 = slice(2 * c * BLK, (2 * c + 1) * BLK), slice((2 * c + 1) * BLK, (2 * c + 2) * BLK)
                p2 = jnp.concatenate([pnT[:, q0], pnT[:, q1]], axis=0)
                o_ref.at[p, c][rows_q, :] = _dot_t0(p2, _stack_pair(qkv_ref.at[2 * DIL_PAIRS_H + c][rows_k, :]))
                o_ref.at[p, DIL_PAIRS_H + c][rows_q, :] = _lanes_to_tokens(lse[:, q0], lse[:, q1], spread)

        def body_first(p, d, r):
            rows = _stream_rows(d, r, 0, BLK)
            return block(p, d, rows, rows, first)

        def body_next(p, d, r, i):
            return block(p, d, _stream_rows(d, r, i, BLK), _stream_rows(d, r, i - 1, 2 * BLK), band)

        _dil_schedule(body_first, body_next)

    return pl.pallas_call(
        body, name="dil_fwd", grid=(B,),
        in_specs=[pl.BlockSpec((None, 6, SEQ, 128), lambda b: (b, 0, 0, 0))],
        out_specs=pl.BlockSpec((None, 3, 4, SEQ, 128), lambda b: (b, 0, 0, 0, 0)),
        out_shape=jax.ShapeDtypeStruct((B, 3, 4, SEQ, 128), F32),
        compiler_params=_cparams(("arbitrary",)),
    )(qkvb)


def _dil_bwd(qkvb, dobb):
    B = qkvb.shape[0]

    def body(qkv_ref, dob_ref, dq_ref):
        band, first = _band_bias(BLK)
        dq_ref[...] = jnp.zeros_like(dq_ref)

        def block(p, d, rows_q, rows_k, bias):
            nk = bias.shape[0]
            lo = lax.broadcasted_iota(jnp.int32, (nk, 128), 1) < HEAD
            qcs, docs, k2s, sT, dpT, lse, delta = [], [], [], [], [], [], []
            for c in range(DIL_PAIRS_H):
                qc = qkv_ref.at[c][rows_q, :].astype(MXU)
                doc = dob_ref.at[c][rows_q, :].astype(MXU)
                k2 = _stack_pair(qkv_ref.at[DIL_PAIRS_H + c][rows_k, :])
                s2 = _dot_nt(k2, qc)
                dp2 = _dot_nt(_stack_pair(qkv_ref.at[2 * DIL_PAIRS_H + c][rows_k, :]), doc)
                sT += [s2[0:nk], s2[nk:2 * nk]]
                dpT += [dp2[0:nk], dp2[nk:2 * nk]]
                lse += _tokens_to_lanes(dob_ref.at[DIL_PAIRS_H + c][rows_q, :])
                delta += _tokens_to_lanes(dob_ref.at[2 * DIL_PAIRS_H + c][rows_q, :])
                qcs.append(qc)
                docs.append(doc)
                k2s.append(k2)
            yield
            pT = jnp.exp(_lanes(sT) + bias - _lanes(lse))
            dsT = pT * (_lanes(dpT) - _lanes(delta))
            dsb, pb = dsT.astype(MXU), pT.astype(MXU)
            yield
            for c in range(DIL_PAIRS_H):
                q0, q1 = slice(2 * c * BLK, (2 * c + 1) * BLK), slice((2 * c + 1) * BLK, (2 * c + 2) * BLK)
                ds2 = jnp.concatenate([dsb[:, q0], dsb[:, q1]], axis=0)
                p2 = jnp.concatenate([pb[:, q0], pb[:, q1]], axis=0)
                dq_ref.at[c][rows_q, :] += _dot_t0(ds2, k2s[c])
                dk2, dv2 = _dot(ds2, qcs[c]), _dot(p2, docs[c])
                dq_ref.at[DIL_PAIRS_H + c][rows_k, :] += jnp.where(lo, dk2[0:nk], dk2[nk:2 * nk])
                dq_ref.at[2 * DIL_PAIRS_H + c][rows_k, :] += jnp.where(lo, dv2[0:nk], dv2[nk:2 * nk])

        def body_first(p, d, r):
            rows = _stream_rows(d, r, 0, BLK)
            return block(p, d, rows, rows, first)

        def body_next(p, d, r, i):
            return block(p, d, _stream_rows(d, r, i, BLK), _stream_rows(d, r, i - 1, 2 * BLK), band)

        _dil_schedule(body_first, body_next)

    spec = pl.BlockSpec((None, 6, SEQ, 128), lambda b: (b, 0, 0, 0))
    return pl.pallas_call(
        body, name="dil_bwd", grid=(B,), in_specs=[spec, spec], out_specs=spec,
        out_shape=jax.ShapeDtypeStruct((B, 6, SEQ, 128), F32),
        compiler_params=_cparams(("arbitrary",)),
    )(qkvb, dobb)


MEM_QROWS = 512


def _mem_attn_fwd(qc, mem, w_mem):
    B = qc.shape[0]

    def body(q_ref, mem_ref, w_ref, o_ref, mkv_ref):
        mkv_ref[...] = _dot(mem_ref[...].astype(MXU), w_ref[...]).astype(MXU)

        def loop(i, carry):
            rows = pl.ds(pl.multiple_of(i * MEM_QROWS, MEM_QROWS), MEM_QROWS)
            for h in range(MEM_H):
                sT = _dot_nt(mkv_ref[:, _col(0, h)], q_ref[rows, _col(0, h)])
                m = jnp.max(sT, axis=0, keepdims=True)
                pT = jnp.exp(sT - m)
                l = jnp.sum(pT, axis=0, keepdims=True)
                oT = _dot_t0(mkv_ref[:, _col(W_C, h)], (pT * (1.0 / l)).astype(MXU))
                _store_o_lse(o_ref, [rows], oT, m + jnp.log(l), [_col(0, h)], [_col(W_C, h)])
            return carry
        lax.fori_loop(0, SEQ // MEM_QROWS, loop, 0)

    return pl.pallas_call(
        body, name="mem_attn_fwd", grid=(B,),
        in_specs=[pl.BlockSpec((None, SEQ, W_C), lambda b: (b, 0, 0)),
                  pl.BlockSpec((None, MEM_LEN, D_MODEL), lambda b: (b, 0, 0)),
                  pl.BlockSpec((D_MODEL, 2 * W_C), lambda b: (0, 0))],
        out_specs=[pl.BlockSpec((None, SEQ, 2 * W_C), lambda b: (b, 0, 0)),
                   pl.BlockSpec((None, MEM_LEN, 2 * W_C), lambda b: (b, 0, 0))],
        out_shape=[jax.ShapeDtypeStruct((B, SEQ, 2 * W_C), F32),
                   jax.ShapeDtypeStruct((B, MEM_LEN, 2 * W_C), MXU)],
        compiler_params=_cparams(("arbitrary",)),
    )(qc, mem, w_mem)


def _mem_attn_bwd(qc, mkv, dob, mem):
    B = qc.shape[0]

    def body(q_ref, mkv_ref, dob_ref, mem_ref, dq_ref, dw_ref, dmkv_ref):
        @pl.when(pl.program_id(0) == 0)
        def _():
            dw_ref[...] = jnp.zeros_like(dw_ref)
        dmkv_ref[...] = jnp.zeros_like(dmkv_ref)

        def loop(i, carry):
            rows = pl.ds(pl.multiple_of(i * MEM_QROWS, MEM_QROWS), MEM_QROWS)
            for h0 in range(0, MEM_H, 2):
                lses = _pair_rows(dob_ref, rows, W_C, h0)
                deltas = _pair_rows(dob_ref, rows, 2 * W_C, h0)
                dqT = []
                for j in range(2):
                    h = h0 + j
                    q = q_ref[rows, _col(0, h)]
                    do = dob_ref[rows, _col(0, h)].astype(MXU)
                    mk, mv = mkv_ref[:, _col(0, h)], mkv_ref[:, _col(W_C, h)]
                    pT = jnp.exp(_dot_nt(mk, q) - lses[j])
                    dsT = pT * (_dot_nt(mv, do) - deltas[j])
                    dsb = dsT.astype(MXU)
                    dmkv_ref[:, _col(0, h)] += _dot(dsb, q)
                    dmkv_ref[:, _col(W_C, h)] += _dot(pT.astype(MXU), do)
                    dqT.append(_dot_t0(mk, dsb))
                dq_ref[rows, slice(h0 * HEAD, (h0 + 2) * HEAD)] = jnp.concatenate(dqT, axis=0).T
            return carry
        lax.fori_loop(0, SEQ // MEM_QROWS, loop, 0)
        dw_ref[...] += _dot_tn(mem_ref[...], dmkv_ref[...].astype(MXU))

    return pl.pallas_call(
        body, name="mem_attn_bwd", grid=(B,),
        in_specs=[pl.BlockSpec((None, SEQ, W_C), lambda b: (b, 0, 0)),
                  pl.BlockSpec((None, MEM_LEN, 2 * W_C), lambda b: (b, 0, 0)),
                  pl.BlockSpec((None, SEQ, 3 * W_C), lambda b: (b, 0, 0)),
                  pl.BlockSpec((None, MEM_LEN, D_MODEL), lambda b: (b, 0, 0))],
        out_specs=[pl.BlockSpec((None, SEQ, W_C), lambda b: (b, 0, 0)),
                   pl.BlockSpec((D_MODEL, 2 * W_C), lambda b: (0, 0))],
        out_shape=[jax.ShapeDtypeStruct((B, SEQ, W_C), F32),
                   jax.ShapeDtypeStruct((D_MODEL, 2 * W_C), F32)],
        scratch_shapes=[pltpu.VMEM((MEM_LEN, 2 * W_C), F32)],
        compiler_params=_cparams(("arbitrary",)),
    )(qc, mkv, dob, mem)


def _headsum(t, e):
    if MXU == F32:
        return _dot(t, e)
    hi = t.astype(MXU)
    lo = (t - hi.astype(F32)).astype(MXU)
    return _dot(hi, e) + _dot(lo, e)


POST_ROWS = 256


def _post(olse_a, olse_b, olse_c, z, x2, tgt, g, gain, bias, w_out, hsum):
    T = x2.shape[0]
    tm = 256
    nt = SEQ // tm

    def body(oa_ref, ob_ref, oc_ref, z_ref, x_ref, t_ref, g_ref, gain_ref, bias_ref, w_ref,
             e_ref, gx_ref, doba_ref, dobb_ref, dobc_ref, dz_ref, dw_ref, small_ref, loss_ref):
        @pl.when(pl.program_id(0) == 0)
        def _():
            dw_ref[...] = jnp.zeros_like(dw_ref)
            small_ref[...] = jnp.zeros_like(small_ref)
            loss_ref[...] = jnp.zeros_like(loss_ref)

        gg = g_ref[...]
        gain_v = gain_ref[...]
        gain_s = gain_v * (1.0 / D_MODEL)
        bias_v = bias_ref[...]
        w = w_ref[...]

        def rms(o):
            rr = lax.rsqrt(jnp.mean(o * o, axis=1, keepdims=True) + RMS_EPS)
            return o * rr, rr

        def rows_of(rs):
            oa = oa_ref[rs, 0:W_A]
            lse_a = oa_ref[rs, W_A:2 * W_A]
            (o1, l1), (o4, l4), (o16, l16) = [
                (_lanes([ob_ref.at[p, 0][rs, :], ob_ref.at[p, 1][rs, :]]),
                 _lanes([ob_ref.at[p, 2][rs, :], ob_ref.at[p, 3][rs, :]])) for p in range(3)]
            mx = jnp.maximum(jnp.maximum(l1, l4), l16)
            e1, e4, e16 = jnp.exp(l1 - mx), jnp.exp(l4 - mx), jnp.exp(l16 - mx)
            den = e1 + e4 + e16
            ob = (e1 * o1 + e4 * o4 + e16 * o16) / den
            lse_b = mx + jnp.log(den)
            oc = oc_ref[rs, 0:W_C]
            lse_c = oc_ref[rs, W_C:2 * W_C]
            na, ra = rms(oa)
            nb, rb = rms(ob)
            nc, rc = rms(oc)
            n = jnp.concatenate([na, nb, nc], axis=1)
            zz = z_ref[rs, :]
            sig = 0.5 * jnp.tanh(0.5 * zz) + 0.5
            sz = zz * sig
            gs = gg * sz
            u = n * gs
            r = ALPHA * x_ref[rs, :] + _dot(u.astype(MXU), w)
            rc0 = r - jnp.mean(r, axis=1, keepdims=True)
            rstd = lax.rsqrt(jnp.mean(rc0 * rc0, axis=1, keepdims=True) + LN_EPS)
            xhat = rc0 * rstd
            err = xhat * gain_v + bias_v - t_ref[rs, :]
            dxh = err * gain_s
            dr = rstd * (dxh - jnp.mean(dxh, axis=1, keepdims=True)
                         - xhat * jnp.mean(dxh * xhat, axis=1, keepdims=True))
            gx_ref[rs, :] = ALPHA * dr
            drb = dr.astype(MXU)
            du = _dot_nt(drb, w)
            dun = du * n
            dz = dun * (gg * (sig + sz * (1.0 - sig)))
            dz_ref[rs, :] = dz.astype(MXU)
            dn = du * gs

            def branch(lo, hi, o, nbr, rr, lse, out_ref):
                wd = hi - lo
                dnb = dn[:, lo:hi]
                dob = rr * (dnb - nbr * jnp.mean(dnb * nbr, axis=1, keepdims=True))
                parts = (dob, lse, _headsum(dob * o, e_ref[0:wd, 0:wd]))
                for j, t in enumerate(parts):
                    if len(out_ref.shape) == 3:
                        for c in range(wd // 128):
                            out_ref.at[j * (wd // 128) + c][rs, :] = t[:, c * 128:(c + 1) * 128]
                    else:
                        out_ref[rs, j * wd:(j + 1) * wd] = t

            branch(0, W_A, oa, na, ra, lse_a, doba_ref)
            branch(W_A, W_A + W_B, ob, nb, rb, lse_b, dobb_ref)
            branch(W_A + W_B, D_MIX, oc, nc, rc, lse_c, dobc_ref)
            csum = lambda t: jnp.sum(t, axis=0, keepdims=True)
            return (u, drb, jnp.sum(err * err), csum(err * xhat), csum(err), csum(dun * sz), csum(dz))

        parts = [rows_of(slice(k * POST_ROWS, (k + 1) * POST_ROWS)) for k in range(tm // POST_ROWS)]
        tot = [sum(p[i] for p in parts) for i in range(2, 7)]
        dw_ref[...] += _dot_tn(jnp.concatenate([p[0] for p in parts], axis=0),
                               jnp.concatenate([p[1] for p in parts], axis=0))
        loss_ref[...] += 0.5 * tot[0] * (1.0 / D_MODEL)
        small_ref[0:1, :] += tot[1] * (1.0 / D_MODEL)
        small_ref[1:2, :] += tot[2] * (1.0 / D_MODEL)
        small_ref[2:3, :] += tot[3]
        small_ref[3:4, :] += tot[4]

    row = lambda w: pl.BlockSpec((tm, w), lambda i: (i, 0))
    full = lambda a, b: pl.BlockSpec((a, b), lambda i: (0, 0))
    return pl.pallas_call(
        body, name="post_fwd_bwd", grid=(T // tm,),
        in_specs=[row(2 * W_A), pl.BlockSpec((None, 3, 4, tm, 128), lambda i: (i // nt, 0, 0, i % nt, 0)),
                  row(2 * W_C),
                  row(D_MIX), row(D_MODEL), row(D_MODEL),
                  full(1, D_MIX), full(1, D_MODEL), full(1, D_MODEL), full(D_MIX, D_MODEL), full(W_A, W_A)],
        out_specs=[row(D_MODEL), row(3 * W_A), pl.BlockSpec((None, 6, tm, 128), lambda i: (i // nt, 0, i % nt, 0)),
                   row(3 * W_C), row(D_MIX),
                   full(D_MIX, D_MODEL), full(8, D_MODEL), full(8, 128)],
        out_shape=[jax.ShapeDtypeStruct((T, D_MODEL), F32),
                   jax.ShapeDtypeStruct((T, 3 * W_A), F32),
                   jax.ShapeDtypeStruct((T // SEQ, 6, SEQ, 128), F32),
                   jax.ShapeDtypeStruct((T, 3 * W_C), F32),
                   jax.ShapeDtypeStruct((T, D_MIX), MXU),
                   jax.ShapeDtypeStruct((D_MIX, D_MODEL), F32),
                   jax.ShapeDtypeStruct((8, D_MODEL), F32),
                   jax.ShapeDtypeStruct((8, 128), F32)],
        compiler_params=_cparams(("arbitrary",)),
    )(olse_a, olse_b, olse_c, z, x2, tgt, g, gain, bias, w_out, hsum)


def _dh_build(dqkva, dqkvb, dqc, dz, tab):
    T = dz.shape[0]
    tm = 512
    nt = SEQ // tm
    HQ = D_IN - D_MIX

    def body(da_ref, db6_ref, dqc_ref, dz_ref, tab_ref, dh_ref, db_ref, nat_ref):
        @pl.when(pl.program_id(0) == 0)
        def _():
            db_ref[...] = jnp.zeros_like(db_ref)
        tab = tab_ref[...]
        parts = [_rope(da_ref[:, 0:512], tab, -1) * Q_SCALE,
                 _rope(da_ref[:, 512:640], tab, -1),
                 da_ref[:, 640:768],
                 _rope(_lanes([db6_ref[0], db6_ref[1]]), tab, -1) * Q_SCALE,
                 _rope(_lanes([db6_ref[2], db6_ref[3]]), tab, -1),
                 _lanes([db6_ref[4], db6_ref[5]]),
                 dqc_ref[...] * Q_SCALE]
        dhq = jnp.concatenate(parts, axis=1)
        db_ref[0:1, :] += jnp.sum(dhq, axis=0, keepdims=True)
        nat_ref[:, 0:HQ] = dhq.astype(MXU)
        nat_ref[:, HQ:D_IN] = dz_ref[...]
        for j in range(N_DEV):
            q = _owner_order_pos(j)
            dh_ref[:, q * COLS_PER_DEV:(q + 1) * COLS_PER_DEV] = nat_ref[:, j * COLS_PER_DEV:(j + 1) * COLS_PER_DEV]

    row = lambda w: pl.BlockSpec((tm, w), lambda i: (i, 0))
    return pl.pallas_call(
        body, name="dh_build", grid=(T // tm,),
        in_specs=[row(768), pl.BlockSpec((None, 6, tm, 128), lambda i: (i // nt, 0, i % nt, 0)),
                  row(256), row(D_MIX), pl.BlockSpec((tm, 384), lambda i: (i % nt, 0))],
        out_specs=[row(D_IN), pl.BlockSpec((8, HQ), lambda i: (0, 0))],
        out_shape=[jax.ShapeDtypeStruct((T, D_IN), MXU), jax.ShapeDtypeStruct((8, HQ), F32)],
        scratch_shapes=[pltpu.VMEM((tm, D_IN), MXU)],
        compiler_params=_cparams(("arbitrary",)),
    )(dqkva, dqkvb, dqc, dz, tab)


TAIL_TK = 512
TAIL_TN = D_IN // 2
TAIL_TM = 256
REDUCE_ROWS = 128


def _tail(xt, dh, gx1, w_in, sends, owns, small_g):
    T = xt.shape[1]
    kt = T // TAIL_TK
    ndw = (D_IN // TAIL_TN) * kt
    nsteps = ndw + T // TAIL_TM
    per_pass = TAIL_TN // COLS_PER_DEV
    pay = dh.dtype
    shapes = [(D_MODEL, COLS_PER_DEV), owns[0].shape, owns[1].shape, small_g.shape]
    n_arr = len(shapes)

    def body(xt_ref, dh1_ref, dh2_ref, gx_ref, w_hbm, smem_ref, sout_ref, omem_ref, oout_ref, sg_ref,
             dx_ref, gin_ref, gmem_ref, gout_ref, gsm_ref,
             acc_ref, w_ref, stage_ref, ownin_ref, lin_ref, lmem_ref, lout_ref, lsm_ref,
             send_sems, recv_sems, w_sem):
        s = pl.program_id(0)
        x, y, c = _my_pos()
        me = 4 * x + 2 * y + c
        lands = (lin_ref, lmem_ref, lout_ref, lsm_ref)

        def src_of(a, j):
            return (stage_ref.at[j], smem_ref.at[j], sout_ref.at[j], sg_ref)[a]

        def to_peer(a, j):
            return pltpu.make_async_remote_copy(
                src_ref=src_of(a, j), dst_ref=lands[a].at[me], send_sem=send_sems.at[a, j],
                recv_sem=recv_sems.at[a, me], device_id=_dev_coords(j), device_id_type=MESH)

        def from_peer(a, m):
            return pltpu.make_async_remote_copy(
                src_ref=lands[a].at[m], dst_ref=lands[a].at[m], send_sem=send_sems.at[a, m],
                recv_sem=recv_sems.at[a, m], device_id=_dev_coords(m), device_id_type=MESH)

        w_copy = pltpu.make_async_copy(w_hbm, w_ref, w_sem)

        @pl.when(s == 0)
        def _():
            w_copy.start()
            for j in range(N_DEV):
                @pl.when(me != j)
                def _(j=j):
                    for a in range(1, n_arr):
                        to_peer(a, j).start()
            for a in range(n_arr - 1):
                lands[a][me] = jnp.zeros(shapes[a], lands[a].dtype)
            lsm_ref[me] = sg_ref[...]

        @pl.when(s < ndw)
        def _():
            k = s % kt

            @pl.when(k == 0)
            def _():
                acc_ref[...] = jnp.zeros_like(acc_ref)
            acc_ref[...] += _dot(xt_ref[...], dh1_ref[...])

            for p in range(D_IN // TAIL_TN):
                @pl.when(s == p * kt + kt - 1)
                def _(p=p):
                    for jj in range(per_pass):
                        j = [o for o in range(N_DEV) if _owner_order_pos(o) == p * per_pass + jj][0]
                        blk = acc_ref[:, jj * COLS_PER_DEV:(jj + 1) * COLS_PER_DEV]
                        stage_ref[j] = blk.astype(pay)

                        @pl.when(me == j)
                        def _(blk=blk):
                            ownin_ref[...] = blk
                        pl.when(me != j)(to_peer(0, j).start)

        @pl.when(s >= ndw)
        def _():
            pl.when(s == ndw)(w_copy.wait)
            dx_ref[...] = _dot_nt(dh2_ref[...], w_ref[...]) + gx_ref[...]

        @pl.when(s == nsteps - 1)
        def _():
            for m in range(N_DEV):
                @pl.when(me != m)
                def _(m=m):
                    for a in range(n_arr):
                        from_peer(a, m).wait_recv()
            for j in range(N_DEV):
                @pl.when(me != j)
                def _(j=j):
                    for a in range(n_arr):
                        to_peer(a, j).wait_send()
            for a, (own, out) in enumerate(((ownin_ref, gin_ref), (omem_ref, gmem_ref), (oout_ref, gout_ref))):
                def chunk(i, carry, a=a, own=own, out=out):
                    rs = pl.ds(pl.multiple_of(i * REDUCE_ROWS, REDUCE_ROWS), REDUCE_ROWS)
                    g = own[rs, :]
                    for m in range(N_DEV):
                        g = g + lands[a][m, rs, :].astype(F32)
                    out[rs, :] = g
                    return carry
                lax.fori_loop(0, shapes[a][0] // REDUCE_ROWS, chunk, 0)
            g = lsm_ref[0]
            for m in range(1, N_DEV):
                g = g + lsm_ref[m]
            gsm_ref[...] = g

    dw_step = lambda s: jnp.minimum(s, ndw - 1)
    dx_step = lambda s: jnp.maximum(s - ndw, 0)
    any_spec = pl.BlockSpec(memory_space=pl.ANY)
    vmem = pl.BlockSpec(memory_space=pltpu.VMEM)
    scratch = [pltpu.VMEM((D_MODEL, TAIL_TN), F32), pltpu.VMEM((D_MODEL, D_IN), w_in.dtype),
               pltpu.VMEM((N_DEV,) + shapes[0], pay), pltpu.VMEM(shapes[0], F32)]
    scratch += [pltpu.VMEM((N_DEV,) + shapes[a], pay) for a in range(n_arr - 1)]
    scratch += [pltpu.VMEM((N_DEV,) + shapes[-1], F32),
                pltpu.SemaphoreType.DMA((n_arr, N_DEV)), pltpu.SemaphoreType.DMA((n_arr, N_DEV)),
                pltpu.SemaphoreType.DMA]
    return pl.pallas_call(
        body, name="tail_dw_dx_reduce", grid=(nsteps,),
        in_specs=[pl.BlockSpec((D_MODEL, TAIL_TK), lambda s: (0, dw_step(s) % kt)),
                  pl.BlockSpec((TAIL_TK, TAIL_TN), lambda s: (dw_step(s) % kt, dw_step(s) // kt)),
                  pl.BlockSpec((TAIL_TM, D_IN), lambda s: (dx_step(s), 0)),
                  pl.BlockSpec((TAIL_TM, D_MODEL), lambda s: (dx_step(s), 0)),
                  any_spec, any_spec, any_spec, vmem, vmem, vmem],
        out_specs=[pl.BlockSpec((TAIL_TM, D_MODEL), lambda s: (dx_step(s), 0)), vmem, vmem, vmem, vmem],
        out_shape=[jax.ShapeDtypeStruct((T, D_MODEL), F32)] + [jax.ShapeDtypeStruct(sh, F32) for sh in shapes],
        scratch_shapes=scratch,
        compiler_params=_cparams(("arbitrary",)),
    )(xt, dh, dh, gx1, w_in, *sends, *owns, small_g)


def _adam_update(grads, params, carried):
    n = len(grads)

    def body(*refs):
        g_refs, p_refs, o_refs = refs[1:1 + n], refs[1 + n:1 + 4 * n], refs[2 + 4 * n:]
        for a in range(n):
            rows = g_refs[a].shape[0]
            cr = REDUCE_ROWS if rows % REDUCE_ROWS == 0 else rows
            w_ref, m_ref, v_ref = p_refs[3 * a:3 * a + 3]
            go_ref, d_ref, nm_ref, nv_ref = o_refs[4 * a:4 * a + 4]

            def chunk(i, carry, cr=cr, g_ref=g_refs[a], w_ref=w_ref, m_ref=m_ref, v_ref=v_ref,
                      go_ref=go_ref, d_ref=d_ref, nm_ref=nm_ref, nv_ref=nv_ref):
                rs = pl.ds(pl.multiple_of(i * cr, cr), cr)
                g = g_ref[rs, :]
                go_ref[rs, :] = g
                d_ref[rs, :], nm_ref[rs, :], nv_ref[rs, :] = _adamw(w_ref[rs, :], g, m_ref[rs, :], v_ref[rs, :])
                return carry
            lax.fori_loop(0, rows // cr, chunk, 0)

    vmem = pl.BlockSpec(memory_space=pltpu.VMEM)
    any_spec = pl.BlockSpec(memory_space=pl.ANY)
    flat = [p for grp in params for p in grp]
    outs = pl.pallas_call(
        body, name="adamw", in_specs=[any_spec] + [vmem] * (4 * n), out_specs=[any_spec] + [vmem] * (4 * n),
        out_shape=[jax.ShapeDtypeStruct(carried.shape, carried.dtype)]
        + [jax.ShapeDtypeStruct(g.shape, F32) for g in grads for _ in range(4)],
        input_output_aliases={0: 0},
        compiler_params=pltpu.CompilerParams(vmem_limit_bytes=VMEM_LIMIT),
    )(carried, *grads, *flat)
    return [outs[1 + 4 * a:5 + 4 * a] for a in range(n)], outs[0]


def _step(x, mem, w_in_s, w_mem_s, w_out_s, b_in, sinks, g, gain, bias, tgt):
    B = x.shape[0]
    T = B * SEQ
    x2 = x.reshape(T, D_MODEL)
    t2 = tgt.reshape(T, D_MODEL)
    tab = _rope_table()
    lane = jnp.arange(W_A)
    hsum = (lane[:, None] // HEAD == lane[None, :] // HEAD).astype(MXU)
    me = 4 * lax.axis_index("x") + 2 * lax.axis_index("y") + lax.axis_index("c")

    (w_in_all,) = _gather_weights([w_in_s])
    qkva, qkvb, qc, z, w_in_perm, xt, w_mem_all, w_out_all = _in_proj(x2, w_in_all, b_in, tab, [w_mem_s, w_out_s])
    w_mem = w_mem_all.reshape(D_MODEL, 2 * W_C)
    w_out = w_out_all.reshape(D_MIX, D_MODEL)

    def per_ex(a):
        return a.reshape(B, SEQ, a.shape[-1])

    def flat(a):
        return a.reshape(T, a.shape[-1])

    qkva3, qc3 = per_ex(qkva), per_ex(qc)
    olse_a = _attn_fwd(qkva3, sinks, d=1, spg=1, unroll=3, hq=SWA_Q, hkv=SWA_KV, max_dist=BLK - 1,
                       name="swa_fwd")
    olse_b = _dil_fwd(qkvb)
    olse_c, mkv = _mem_attn_fwd(qc3, mem, w_mem)

    gx1, doba, dobb, dobc, dz, dw_out, small, loss = _post(
        flat(olse_a), olse_b, flat(olse_c), z, x2, t2, g, gain, bias, w_out, hsum)

    dqkva, dsink = _attn_bwd(qkva3, per_ex(doba), sinks, d=1, spg=1, unroll=3, hq=SWA_Q, hkv=SWA_KV,
                             max_dist=BLK - 1, name="swa_bwd")
    dqkvb = _dil_bwd(qkvb, dobb)
    dqc, dw_mem = _mem_attn_bwd(qc3, mkv, per_ex(dobc), mem)
    dh, dbq = _dh_build(flat(dqkva), dqkvb, flat(dqc), dz, tab)

    small_g = _pack_small(dict(b_in=jnp.concatenate([dbq[0], small[3]]), sinks=dsink[:, 0], g=small[2],
                               gain=small[0], bias=small[1], loss=loss[0, 0]))
    blocks = [dw_mem.reshape(N_DEV, ROWS_PER_DEV, 2 * W_C), dw_out.reshape(N_DEV, ROWS_PER_DEV, D_MODEL)]
    sends = [b.astype(MXU) for b in blocks]
    owns = [lax.dynamic_index_in_dim(b, me, axis=0, keepdims=False) for b in blocks]
    grad_x, g_in, g_mem, g_out, g_small = _tail(xt, dh, gx1, w_in_perm, sends, owns, small_g)
    return grad_x.reshape(B, SEQ, D_MODEL), g_in, g_mem, g_out, g_small


def _my_pos():
    return lax.axis_index("x"), lax.axis_index("y"), lax.axis_index("c")


def _gather_weights(shards):
    n_arr = len(shards)

    def body(*refs):
        ins, outs = refs[0:n_arr], refs[n_arr:2 * n_arr]
        send_sems, recv_sems, local_sems = refs[2 * n_arr:]
        x, y, c = _my_pos()
        me, sibling = (x, y, c), (x, y, 1 - c)
        chips = [(1 - x, y), (x, 1 - y), (1 - x, 1 - y)]

        def slot(a, pos):
            return outs[a].at[4 * pos[0] + 2 * pos[1] + pos[2]]

        def copy(a, k, block, to, src=None):
            return pltpu.make_async_remote_copy(
                src_ref=slot(a, block) if src is None else src, dst_ref=slot(a, block),
                send_sem=send_sems.at[a, k], recv_sem=recv_sems.at[a, k],
                device_id=to, device_id_type=MESH)

        mine = [pltpu.make_async_copy(ins[a], slot(a, me), local_sems.at[a]) for a in range(n_arr)]
        for cp in mine:
            cp.start()
        first = []
        for a in range(n_arr):
            first.append(copy(a, 0, me, sibling, src=ins[a]))
            first += [copy(a, 1 + j, me, (*chip, c), src=ins[a]) for j, chip in enumerate(chips)]
        for cp in first:
            cp.start()
        passed = []
        for j, chip in enumerate(chips):
            for a in range(n_arr):
                copy(a, 1 + j, (*chip, c), me).wait_recv()
                fwd = copy(a, 4 + j, (*chip, c), sibling)
                fwd.start()
                passed.append(fwd)
        for a in range(n_arr):
            copy(a, 0, sibling, me).wait_recv()
            for j, chip in enumerate(chips):
                copy(a, 4 + j, (*chip, 1 - c), me).wait_recv()
        for cp in first + passed:
            cp.wait_send()
        for cp in mine:
            cp.wait()

    any_spec = pl.BlockSpec(memory_space=pl.ANY)
    return pl.pallas_call(
        body, name="gather_weights",
        in_specs=[any_spec] * n_arr, out_specs=[any_spec] * n_arr,
        out_shape=[jax.ShapeDtypeStruct((N_DEV,) + s.shape, s.dtype) for s in shards],
        scratch_shapes=[pltpu.SemaphoreType.DMA((n_arr, 7)), pltpu.SemaphoreType.DMA((n_arr, 7)),
                        pltpu.SemaphoreType.DMA((n_arr,))],
    )(*shards)


def _adamw(w, g, m, v):
    m = ADAM_B1 * m + (1.0 - ADAM_B1) * g
    v = ADAM_B2 * v + (1.0 - ADAM_B2) * (g * g)
    m_hat = m / (1.0 - ADAM_B1 ** ADAM_STEP)
    v_hat = v / (1.0 - ADAM_B2 ** ADAM_STEP)
    delta = -ADAM_LR * (m_hat / (jnp.sqrt(v_hat) + ADAM_EPS) + ADAM_WD * w)
    return delta, m, v


_SMALL_SIZES = (("b_in", D_IN), ("g", D_MIX), ("gain", D_MODEL), ("bias", D_MODEL), ("sinks", SWA_Q), ("loss", 1))


def _pack_small(d):
    flat = jnp.concatenate([jnp.reshape(d[k], (-1,)).astype(F32) if k in d else jnp.zeros((n,), F32)
                            for k, n in _SMALL_SIZES])
    flat = jnp.pad(flat, (0, SMALL_ROWS * 128 - flat.shape[0]))
    return flat.reshape(SMALL_ROWS, 128)


def _unpack_small(p):
    flat = p.reshape(-1)
    out, off = {}, 0
    for k, n in _SMALL_SIZES:
        out[k] = flat[off:off + n].reshape(1, n)
        off += n
    return out


def kernel(x, mem, w_in, b_in, w_mem, attn_sinks, g_branch, w_out, ln_gain, ln_bias, loss_target, m_w_in, m_b_in, m_w_mem, m_attn_sinks, m_g_branch, m_w_out, m_ln_gain, m_ln_bias, v_w_in, v_b_in, v_w_mem, v_attn_sinks, v_g_branch, v_w_out, v_ln_gain, v_ln_bias):
    grad_x, g_in, g_mem, g_out, g_small = _step(
        x, mem, w_in[0].astype(MXU), w_mem[0].astype(MXU), w_out[0].astype(MXU), b_in, attn_sinks[0],
        g_branch, ln_gain, ln_bias, loss_target)

    small_w = _pack_small(dict(b_in=b_in, g=g_branch, gain=ln_gain, bias=ln_bias, sinks=attn_sinks))
    small_m = _pack_small(dict(b_in=m_b_in, g=m_g_branch, gain=m_ln_gain, bias=m_ln_bias, sinks=m_attn_sinks))
    small_v = _pack_small(dict(b_in=v_b_in, g=v_g_branch, gain=v_ln_gain, bias=v_ln_bias, sinks=v_attn_sinks))
    grads = [g_in, g_mem, g_out, g_small]
    params = [(w_in[0], m_w_in[0], v_w_in[0]), (w_mem[0], m_w_mem[0], v_w_mem[0]),
              (w_out[0], m_w_out[0], v_w_out[0]), (small_w, small_m, small_v)]
    res, grad_x = _adam_update(grads, params, grad_x)
    big = [[r[None] for r in res[a]] for a in range(3)]
    sm = [_unpack_small(r) for r in res[3]]

    def group(i):
        return (big[0][i], sm[i]["b_in"], big[1][i], sm[i]["sinks"], sm[i]["g"], big[2][i],
                sm[i]["gain"], sm[i]["bias"])

    loss = sm[0]["loss"].reshape(())
    return (loss, grad_x, *group(0), *group(1), *group(2), *group(3))
```

```python
import functools
import math

import jax
import jax.numpy as jnp
from jax import lax
from jax.experimental import pallas as pl
from jax.experimental.pallas import tpu as pltpu

F32 = jnp.float32
MXU = jnp.bfloat16

D_MODEL = 1024
SEQ = 2048
HEAD = 64
BLK = 128
SWA_Q, SWA_KV = 8, 2
DIL_H = 4
MEM_H = 4
MEM_LEN = 256
W_A, W_KVA, W_B, W_C = 512, 128, 256, 256
D_MIX = 1024
D_IN = 2816
N_DEV = 8
COLS_PER_DEV = D_IN // N_DEV
ROWS_PER_DEV = D_MODEL // N_DEV
ROPE_THETA = 10000.0
LN_EPS = 1e-5
RMS_EPS = 1e-6
ALPHA = 2.0 ** 0.25
Q_SCALE = HEAD ** -0.5
NEG = -1e30
SMALL_ROWS = 48
VMEM_LIMIT = 56 * 1024 * 1024

ADAM_LR = 0.001
ADAM_B1 = 0.9
ADAM_B2 = 0.999
ADAM_EPS = 1e-08
ADAM_WD = 0.01
ADAM_STEP = 10

MESH = pl.DeviceIdType.MESH


def _cparams(sem=None):
    return pltpu.CompilerParams(dimension_semantics=sem, vmem_limit_bytes=VMEM_LIMIT)


def _dot(a, b):
    return jnp.dot(a, b, preferred_element_type=F32)


def _dot_nt(a, b):
    return lax.dot_general(a, b, (((1,), (1,)), ((), ())), preferred_element_type=F32)


def _dot_t0(a, b):
    return lax.dot_general(a, b, (((0,), (0,)), ((), ())), preferred_element_type=F32)


def _dot_tn(a, b):
    return jnp.dot(a.T.astype(MXU), b, preferred_element_type=F32)


def _rope(t, tab, sign):
    cos, sa, sb = tab[:, 0:128], tab[:, 128:256], tab[:, 256:384]
    outs = []
    for c in range(t.shape[1] // 128):
        tc = t[:, c * 128:(c + 1) * 128]
        r = pltpu.roll(tc, 96, 1) * sa + pltpu.roll(tc, 32, 1) * sb
        outs.append(tc * cos + r if sign > 0 else tc * cos - r)
    return outs[0] if len(outs) == 1 else jnp.concatenate(outs, axis=1)


def _rope_table():
    pos = jnp.arange(SEQ, dtype=F32)
    inv = ROPE_THETA ** (-jnp.arange(0, HEAD, 2, dtype=F32) / HEAD)
    ang = pos[:, None] * inv[None, :]
    ang = jnp.concatenate([ang, ang], axis=-1)
    cos, sin = jnp.cos(ang), jnp.sin(ang)
    lane = jnp.arange(HEAD)[None, :]
    sa = jnp.where(lane < HEAD // 2, -sin, 0.0)
    sb = jnp.where(lane >= HEAD // 2, sin, 0.0)
    two = lambda t: jnp.concatenate([t, t], axis=-1)
    return jnp.concatenate([two(cos), two(sa), two(sb)], axis=-1).astype(F32)


def _dev_coords(j):
    return (j >> 2, (j >> 1) & 1, j & 1)


def _in_proj(x2, w_all, b_in, tab, late_shards):
    T = x2.shape[0]
    tm = 512
    n_late = len(late_shards)

    def body(x_ref, wall_ref, b_ref, tab_ref, *rest):
        late_in, rest = rest[:n_late], rest[n_late:]
        qkva_ref, qkvb_ref, qc_ref, z_ref, w_ref, xt_ref = rest[:6]
        late_out = rest[6:6 + n_late]
        send_sems, recv_sems, local_sems = rest[6 + n_late:]
        step, last = pl.program_id(0), pl.num_programs(0) - 1
        x, y, c = _my_pos()
        me = 4 * x + 2 * y + c

        def to_peer(a, j):
            return pltpu.make_async_remote_copy(
                src_ref=late_in[a], dst_ref=late_out[a].at[me], send_sem=send_sems.at[a, j],
                recv_sem=recv_sems.at[a, me], device_id=_dev_coords(j), device_id_type=MESH)

        def from_peer(a, m):
            return pltpu.make_async_remote_copy(
                src_ref=late_out[a].at[m], dst_ref=late_out[a].at[m], send_sem=send_sems.at[a, m],
                recv_sem=recv_sems.at[a, m], device_id=_dev_coords(m), device_id_type=MESH)

        def mine(a):
            return pltpu.make_async_copy(late_in[a], late_out[a].at[me], local_sems.at[a])

        @pl.when(step == 0)
        def _():
            for a in range(n_late):
                mine(a).start()
                for j in range(N_DEV):
                    pl.when(me != j)(to_peer(a, j).start)
            for j in range(N_DEV):
                w_ref[:, j * COLS_PER_DEV:(j + 1) * COLS_PER_DEV] = wall_ref[j]

        xb = x_ref[...].astype(MXU)
        xt_ref[...] = x_ref[...].T.astype(MXU)
        tab = tab_ref[...]

        def seg(c0, c1):
            return _dot(xb, w_ref[:, c0:c1]) + b_ref[:, c0:c1]

        qa = (_rope(seg(0, 512), tab, 1) * Q_SCALE).astype(MXU)
        for c in range(SWA_Q // 2):
            qkva_ref[c] = qa[:, c * 128:(c + 1) * 128]
        lo = lax.broadcasted_iota(jnp.int32, (tm, 128), 1) < HEAD
        for j, t in enumerate((_rope(seg(512, 640), tab, 1), seg(640, 768))):
            other = pltpu.roll(t, HEAD, 1)
            qkva_ref[4 + 2 * j] = jnp.where(lo, t, other).astype(MXU)
            qkva_ref[5 + 2 * j] = jnp.where(lo, other, t).astype(MXU)
        qkvb = (_rope(seg(768, 1024), tab, 1) * Q_SCALE, _rope(seg(1024, 1280), tab, 1), seg(1280, 1536))
        for j, t in enumerate(qkvb):
            for c in range(2):
                qkvb_ref[2 * j + c] = t[:, c * 128:(c + 1) * 128]
        qc = (seg(1536, 1792) * Q_SCALE).astype(MXU)
        for c in range(MEM_H // 2):
            qc_ref[c] = qc[:, c * 128:(c + 1) * 128]
        z_ref[...] = seg(1792, 2816)

        @pl.when(step == last)
        def _():
            for a in range(n_late):
                mine(a).wait()
                for m in range(N_DEV):
                    pl.when(me != m)(from_peer(a, m).wait_recv)
                for j in range(N_DEV):
                    pl.when(me != j)(to_peer(a, j).wait_send)

    nt = SEQ // tm
    any_spec = pl.BlockSpec(memory_space=pl.ANY)
    chunked = lambda n: pl.BlockSpec((None, n, tm, 128), lambda i: (i // nt, 0, i % nt, 0))
    return pl.pallas_call(
        body, name="in_proj_fwd",
        grid=(T // tm,),
        in_specs=[pl.BlockSpec((tm, D_MODEL), lambda i: (i, 0)),
                  pl.BlockSpec((N_DEV, D_MODEL, COLS_PER_DEV), lambda i: (0, 0, 0)),
                  pl.BlockSpec((1, D_IN), lambda i: (0, 0)),
                  pl.BlockSpec((tm, 384), lambda i: (i % nt, 0))] + [any_spec] * n_late,
        out_specs=[chunked(SWA_CHUNKS), chunked(6), chunked(MEM_H // 2),
                   pl.BlockSpec((tm, D_MIX), lambda i: (i, 0)),
                   pl.BlockSpec((D_MODEL, D_IN), lambda i: (0, 0)),
                   pl.BlockSpec((D_MODEL, tm), lambda i: (0, i))] + [any_spec] * n_late,
        out_shape=[jax.ShapeDtypeStruct((T // SEQ, SWA_CHUNKS, SEQ, 128), MXU),
                   jax.ShapeDtypeStruct((T // SEQ, 6, SEQ, 128), F32),
                   jax.ShapeDtypeStruct((T // SEQ, MEM_H // 2, SEQ, 128), MXU),
                   jax.ShapeDtypeStruct((T, D_MIX), F32),
                   jax.ShapeDtypeStruct((D_MODEL, D_IN), w_all.dtype),
                   jax.ShapeDtypeStruct((D_MODEL, T), MXU)]
        + [jax.ShapeDtypeStruct((N_DEV,) + s.shape, s.dtype) for s in late_shards],
        scratch_shapes=[pltpu.SemaphoreType.DMA((n_late, N_DEV)), pltpu.SemaphoreType.DMA((n_late, N_DEV)),
                        pltpu.SemaphoreType.DMA((n_late,))],
        compiler_params=_cparams(("arbitrary",)),
    )(x2, w_all, b_in, tab, *late_shards)


CHAIN = 4


def _band_bias(max_dist):
    kj = lax.broadcasted_iota(jnp.int32, (2 * BLK, BLK), 0)
    qi = lax.broadcasted_iota(jnp.int32, (2 * BLK, BLK), 1)
    dist = qi + BLK - kj
    band = jnp.where((dist >= 0) & (dist <= max_dist), 0.0, NEG).astype(F32)
    k1 = lax.broadcasted_iota(jnp.int32, (BLK, BLK), 0)
    q1 = lax.broadcasted_iota(jnp.int32, (BLK, BLK), 1)
    first = jnp.where((q1 - k1 >= 0) & (q1 - k1 <= max_dist), 0.0, NEG).astype(F32)
    return jnp.concatenate([band] * CHAIN, axis=1), jnp.concatenate([first] * CHAIN, axis=1)


def _lanes(parts):
    return jnp.concatenate(parts, axis=1)


PICK_ROWS = 16


def _stack_pair(t):
    lo = (lax.broadcasted_iota(jnp.int32, t.shape, 1) < HEAD).astype(F32)
    return jnp.concatenate([t * lo, t * (1.0 - lo)], axis=0).astype(MXU)


def _pair_rows(x, n):
    lo = lax.broadcasted_iota(jnp.int32, (n, 128), 1) < HEAD
    return jnp.where(lo, x[0:n], x[n:2 * n])


def _split3(t):
    if MXU == F32:
        return (t,)
    hi = t.astype(MXU)
    r = t - hi.astype(F32)
    mid = r.astype(MXU)
    return hi, mid, (r - mid.astype(F32)).astype(MXU)


def _interleave(tiles):
    tiles = list(tiles)
    while tiles:
        for t in list(tiles):
            try:
                next(t)
            except StopIteration:
                tiles.remove(t)


def _softmax_cols(sT, sinkrow=None):
    m = jnp.max(sT, axis=0, keepdims=True)
    if sinkrow is not None:
        m = jnp.maximum(m, sinkrow)
    pT = jnp.exp(sT - m)
    l = jnp.sum(pT, axis=0, keepdims=True)
    if sinkrow is not None:
        l = l + jnp.exp(sinkrow - m)
    return (pT * (1.0 / l)).astype(MXU), m + jnp.log(l)


SWA_CHUNKS = 8
SWA_UNROLL = 3
N_QBLK = SEQ // BLK


def _swa_fwd(qkva, sinks):
    B = qkva.shape[0]
    G = SWA_Q // SWA_KV

    def body(sink_ref, qkv_ref, o_ref, lse_ref):
        band, first = _band_bias(BLK - 1)
        sinkrows = [_lanes([jnp.full((1, BLK), sink_ref[G * hk + j], F32) for j in range(G)])
                    for hk in range(SWA_KV)]

        def tile(hk, blk, rows_q, rows_k, bias):
            nk = bias.shape[0]
            k2 = _stack_pair(qkv_ref.at[4 + hk][rows_k, :])
            sT = []
            for c in (2 * hk, 2 * hk + 1):
                s2 = _dot_nt(k2, qkv_ref.at[c][rows_q, :])
                sT += [s2[0:nk], s2[nk:2 * nk]]
            yield
            pnT, lse = _softmax_cols(_lanes(sT) + bias, sinkrows[hk])
            yield
            v2 = _stack_pair(qkv_ref.at[6 + hk][rows_k, :])
            for j, c in enumerate((2 * hk, 2 * hk + 1)):
                p2 = jnp.concatenate([pnT[:, 2 * j * BLK:(2 * j + 1) * BLK],
                                      pnT[:, (2 * j + 1) * BLK:(2 * j + 2) * BLK]], axis=0)
                o_ref.at[c][rows_q, :] = _dot_t0(p2, v2)
            for j in range(G):
                lse_ref.at[blk][G * hk + j:G * hk + j + 1, :] = lse[:, j * BLK:(j + 1) * BLK]

        def tiles_at(i):
            r0 = pl.multiple_of(i * BLK, BLK)
            rk = pl.multiple_of(i * BLK - BLK, BLK)
            return [tile(hk, i, pl.ds(r0, BLK), pl.ds(rk, 2 * BLK), band) for hk in range(SWA_KV)]

        _interleave([tile(hk, 0, pl.ds(0, BLK), pl.ds(0, BLK), first) for hk in range(SWA_KV)])

        def loop(j, carry):
            _interleave([t for u in range(SWA_UNROLL) for t in tiles_at(1 + j * SWA_UNROLL + u)])
            return carry
        lax.fori_loop(0, (N_QBLK - 1) // SWA_UNROLL, loop, 0)

    return pl.pallas_call(
        body, name="swa_fwd", grid=(B,),
        in_specs=[pl.BlockSpec(memory_space=pltpu.SMEM),
                  pl.BlockSpec((None, SWA_CHUNKS, SEQ, 128), lambda b: (b, 0, 0, 0))],
        out_specs=[pl.BlockSpec((None, SWA_Q // 2, SEQ, 128), lambda b: (b, 0, 0, 0)),
                   pl.BlockSpec((None, N_QBLK, 8, 128), lambda b: (b, 0, 0, 0))],
        out_shape=[jax.ShapeDtypeStruct((B, SWA_Q // 2, SEQ, 128), F32),
                   jax.ShapeDtypeStruct((B, N_QBLK, 8, 128), F32)],
        compiler_params=_cparams(("arbitrary",)),
    )(sinks, qkva)


def _swa_bwd(qkva, do, lse, delta, sinks):
    B = qkva.shape[0]
    G = SWA_Q // SWA_KV

    def body(sink_ref, qkv_ref, do_ref, lse_ref, delta_ref, dq_ref, dsink_ref):
        band, first = _band_bias(BLK - 1)
        sinkrows = [_lanes([jnp.full((1, BLK), sink_ref[G * hk + j], F32) for j in range(G)])
                    for hk in range(SWA_KV)]

        @pl.when(pl.program_id(0) == 0)
        def _():
            dsink_ref[...] = jnp.zeros_like(dsink_ref)
        for c in range(4, SWA_CHUNKS):
            dq_ref[c] = jnp.zeros((SEQ, 128), F32)

        def tile(hk, blk, rows_q, rows_k, bias, accs):
            nk = bias.shape[0]
            k2 = _stack_pair(qkv_ref.at[4 + hk][rows_k, :])
            v2 = _stack_pair(qkv_ref.at[6 + hk][rows_k, :])
            qcs, docs, sT, dpT = [], [], [], []
            for c in (2 * hk, 2 * hk + 1):
                qc, doc = qkv_ref.at[c][rows_q, :], do_ref.at[c][rows_q, :]
                s2, dp2 = _dot_nt(k2, qc), _dot_nt(v2, doc)
                sT += [s2[0:nk], s2[nk:2 * nk]]
                dpT += [dp2[0:nk], dp2[nk:2 * nk]]
                qcs.append(qc)
                docs.append(doc)
            heads = slice(G * hk, G * hk + G)
            lse_r = _lanes([lse_ref.at[blk][h:h + 1, :] for h in range(G * hk, G * hk + G)])
            delta_r = _lanes([delta_ref.at[blk][h:h + 1, :] for h in range(G * hk, G * hk + G)])
            yield
            pT = jnp.exp(_lanes(sT) + bias - lse_r)
            dsT = pT * (_lanes(dpT) - delta_r)
            dsb, pb = dsT.astype(MXU), pT.astype(MXU)
            accs[hk] = accs[hk] - jnp.exp(sinkrows[hk] - lse_r) * delta_r
            yield
            dk2 = dv2 = None
            for j, c in enumerate((2 * hk, 2 * hk + 1)):
                q0, q1 = slice(2 * j * BLK, (2 * j + 1) * BLK), slice((2 * j + 1) * BLK, (2 * j + 2) * BLK)
                ds2 = jnp.concatenate([dsb[:, q0], dsb[:, q1]], axis=0)
                p2 = jnp.concatenate([pb[:, q0], pb[:, q1]], axis=0)
                dq_ref.at[c][rows_q, :] = _dot_t0(ds2, k2)
                dk2 = _dot(ds2, qcs[j]) if dk2 is None else dk2 + _dot(ds2, qcs[j])
                dv2 = _dot(p2, docs[j]) if dv2 is None else dv2 + _dot(p2, docs[j])
            dq_ref.at[4 + hk][rows_k, :] += _pair_rows(dk2, nk)
            dq_ref.at[6 + hk][rows_k, :] += _pair_rows(dv2, nk)

        def run(tiles_of, accs):
            accs = list(accs)
            _interleave(tiles_of(accs))
            return tuple(accs)

        zero = jnp.zeros((1, G * BLK), F32)
        accs = run(lambda a: [tile(hk, 0, pl.ds(0, BLK), pl.ds(0, BLK), first, a) for hk in range(SWA_KV)],
                   (zero,) * SWA_KV)

        def loop(j, accs):
            def tiles_of(a):
                out = []
                for u in range(SWA_UNROLL):
                    i = 1 + j * SWA_UNROLL + u
                    r0 = pl.multiple_of(i * BLK, BLK)
                    rk = pl.multiple_of(i * BLK - BLK, BLK)
                    out += [tile(hk, i, pl.ds(r0, BLK), pl.ds(rk, 2 * BLK), band, a) for hk in range(SWA_KV)]
                return out
            return run(tiles_of, accs)
        accs = lax.fori_loop(0, (N_QBLK - 1) // SWA_UNROLL, loop, accs)
        for hk in range(SWA_KV):
            for j in range(G):
                tot = jnp.sum(accs[hk][:, j * BLK:(j + 1) * BLK], axis=1, keepdims=True)
                dsink_ref[G * hk + j:G * hk + j + 1, :] += jnp.broadcast_to(tot, (1, 128))

    stat = pl.BlockSpec((None, N_QBLK, 8, 128), lambda b: (b, 0, 0, 0))
    return pl.pallas_call(
        body, name="swa_bwd", grid=(B,),
        in_specs=[pl.BlockSpec(memory_space=pltpu.SMEM),
                  pl.BlockSpec((None, SWA_CHUNKS, SEQ, 128), lambda b: (b, 0, 0, 0)),
                  pl.BlockSpec((None, SWA_Q // 2, SEQ, 128), lambda b: (b, 0, 0, 0)), stat, stat],
        out_specs=[pl.BlockSpec((None, SWA_CHUNKS, SEQ, 128), lambda b: (b, 0, 0, 0)),
                   pl.BlockSpec((8, 128), lambda b: (0, 0))],
        out_shape=[jax.ShapeDtypeStruct((B, SWA_CHUNKS, SEQ, 128), F32), jax.ShapeDtypeStruct((8, 128), F32)],
        compiler_params=_cparams(("arbitrary",)),
    )(sinks, qkva, do, lse, delta)


DILATIONS = (1, 4, 16)
DIL_PAIRS_H = DIL_H // 2


def _stream_rows(d, r, i, n):
    if d == 1:
        return pl.ds(pl.multiple_of(i * BLK, BLK), n)
    return pl.ds(r + i * (BLK * d), n, stride=d)


def _spread_matrix():
    row = lax.broadcasted_iota(jnp.int32, (PICK_ROWS, 128), 0)
    lane = lax.broadcasted_iota(jnp.int32, (PICK_ROWS, 128), 1)
    return ((row < 6) & ((row % 2 == 1) == (lane >= HEAD))).astype(MXU)


def _lanes_to_tokens(v0, v1, spread):
    n = v0.shape[1]
    row = lax.broadcasted_iota(jnp.int32, (PICK_ROWS, n), 0)
    a = jnp.zeros((PICK_ROWS, n), F32)
    for i, (p0, p1) in enumerate(zip(_split3(v0), _split3(v1))):
        a = jnp.where(row == 2 * i, p0.astype(F32), a)
        a = jnp.where(row == 2 * i + 1, p1.astype(F32), a)
    return _dot_t0(a.astype(MXU), spread)


def _tokens_to_lanes(t):
    r = t.T
    return r[0:1, :], r[HEAD:HEAD + 1, :]


def _dil_schedule(body_first, body_next):
    for p, d in enumerate(DILATIONS):
        nblk = SEQ // d // BLK
        if d == 1:
            _interleave([body_first(p, d, 0)])
            def loop(j, c, p=p, d=d):
                _interleave([body_next(p, d, 0, 1 + 3 * j + u) for u in range(3)])
                return c
            lax.fori_loop(0, (nblk - 1) // 3, loop, 0)
        elif nblk > 1:
            def loop(r, c, p=p, d=d, nblk=nblk):
                _interleave([body_first(p, d, r)] + [body_next(p, d, r, i) for i in range(1, nblk)])
                return c
            lax.fori_loop(0, d, loop, 0)
        else:
            def loop(j, c, p=p, d=d):
                _interleave([body_first(p, d, 4 * j + u) for u in range(4)])
                return c
            lax.fori_loop(0, d // 4, loop, 0)


def _dil_fwd(qkvb):
    B = qkvb.shape[0]

    def body(qkv_ref, o_ref):
        band, first = _band_bias(BLK)
        spread = _spread_matrix()

        def block(p, d, rows_q, rows_k, bias):
            nk = bias.shape[0]
            sT = []
            for c in range(DIL_PAIRS_H):
                qc = qkv_ref.at[c][rows_q, :].astype(MXU)
                s2 = _dot_nt(_stack_pair(qkv_ref.at[DIL_PAIRS_H + c][rows_k, :]), qc)
                sT += [s2[0:nk], s2[nk:2 * nk]]
            yield
            sT = _lanes(sT) + bias
            m = jnp.max(sT, axis=0, keepdims=True)
            pT = jnp.exp(sT - m)
            l = jnp.sum(pT, axis=0, keepdims=True)
            pnT = (pT * (1.0 / l)).astype(MXU)
            lse = m + jnp.log(l)
            yield
            for c in range(DIL_PAIRS_H):
                q0, q1 = slice(2 * c * BLK, (2 * c + 1) * BLK), slice((2 * c + 1) * BLK, (2 * c + 2) * BLK)
                p2 = jnp.concatenate([pnT[:, q0], pnT[:, q1]], axis=0)
                o_ref.at[p, c][rows_q, :] = _dot_t0(p2, _stack_pair(qkv_ref.at[2 * DIL_PAIRS_H + c][rows_k, :]))
                o_ref.at[p, DIL_PAIRS_H + c][rows_q, :] = _lanes_to_tokens(lse[:, q0], lse[:, q1], spread)

        def body_first(p, d, r):
            rows = _stream_rows(d, r, 0, BLK)
            return block(p, d, rows, rows, first)

        def body_next(p, d, r, i):
            return block(p, d, _stream_rows(d, r, i, BLK), _stream_rows(d, r, i - 1, 2 * BLK), band)

        _dil_schedule(body_first, body_next)

    return pl.pallas_call(
        body, name="dil_fwd", grid=(B,),
        in_specs=[pl.BlockSpec((None, 6, SEQ, 128), lambda b: (b, 0, 0, 0))],
        out_specs=pl.BlockSpec((None, 3, 4, SEQ, 128), lambda b: (b, 0, 0, 0, 0)),
        out_shape=jax.ShapeDtypeStruct((B, 3, 4, SEQ, 128), F32),
        compiler_params=_cparams(("arbitrary",)),
    )(qkvb)


def _dil_bwd(qkvb, dobb):
    B = qkvb.shape[0]

    def body(qkv_ref, dob_ref, dq_ref):
        band, first = _band_bias(BLK)
        dq_ref[...] = jnp.zeros_like(dq_ref)

        def block(p, d, rows_q, rows_k, bias):
            nk = bias.shape[0]
            lo = lax.broadcasted_iota(jnp.int32, (nk, 128), 1) < HEAD
            qcs, docs, k2s, sT, dpT, lse, delta = [], [], [], [], [], [], []
            for c in range(DIL_PAIRS_H):
                qc = qkv_ref.at[c][rows_q, :].astype(MXU)
                doc = dob_ref.at[c][rows_q, :].astype(MXU)
                k2 = _stack_pair(qkv_ref.at[DIL_PAIRS_H + c][rows_k, :])
                s2 = _dot_nt(k2, qc)
                dp2 = _dot_nt(_stack_pair(qkv_ref.at[2 * DIL_PAIRS_H + c][rows_k, :]), doc)
                sT += [s2[0:nk], s2[nk:2 * nk]]
                dpT += [dp2[0:nk], dp2[nk:2 * nk]]
                lse += _tokens_to_lanes(dob_ref.at[DIL_PAIRS_H + c][rows_q, :])
                delta += _tokens_to_lanes(dob_ref.at[2 * DIL_PAIRS_H + c][rows_q, :])
                qcs.append(qc)
                docs.append(doc)
                k2s.append(k2)
            yield
            pT = jnp.exp(_lanes(sT) + bias - _lanes(lse))
            dsT = pT * (_lanes(dpT) - _lanes(delta))
            dsb, pb = dsT.astype(MXU), pT.astype(MXU)
            yield
            for c in range(DIL_PAIRS_H):
                q0, q1 = slice(2 * c * BLK, (2 * c + 1) * BLK), slice((2 * c + 1) * BLK, (2 * c + 2) * BLK)
                ds2 = jnp.concatenate([dsb[:, q0], dsb[:, q1]], axis=0)
                p2 = jnp.concatenate([pb[:, q0], pb[:, q1]], axis=0)
                dq_ref.at[c][rows_q, :] += _dot_t0(ds2, k2s[c])
                dk2, dv2 = _dot(ds2, qcs[c]), _dot(p2, docs[c])
                dq_ref.at[DIL_PAIRS_H + c][rows_k, :] += jnp.where(lo, dk2[0:nk], dk2[nk:2 * nk])
                dq_ref.at[2 * DIL_PAIRS_H + c][rows_k, :] += jnp.where(lo, dv2[0:nk], dv2[nk:2 * nk])

        def body_first(p, d, r):
            rows = _stream_rows(d, r, 0, BLK)
            return block(p, d, rows, rows, first)

        def body_next(p, d, r, i):
            return block(p, d, _stream_rows(d, r, i, BLK), _stream_rows(d, r, i - 1, 2 * BLK), band)

        _dil_schedule(body_first, body_next)

    spec = pl.BlockSpec((None, 6, SEQ, 128), lambda b: (b, 0, 0, 0))
    return pl.pallas_call(
        body, name="dil_bwd", grid=(B,), in_specs=[spec, spec], out_specs=spec,
        out_shape=jax.ShapeDtypeStruct((B, 6, SEQ, 128), F32),
        compiler_params=_cparams(("arbitrary",)),
    )(qkvb, dobb)


MEM_UNROLL = 4
MEM_PAIRS = MEM_H // 2


def _mem_attn_fwd(qc, mem, w_mem):
    B = qc.shape[0]

    def body(q_ref, mem_ref, w_ref, o_ref, lse_ref, mkv_ref, k2_ref, v2_ref):
        mkv = _dot(mem_ref[...].astype(MXU), w_ref[...])
        mkv_ref[...] = mkv.astype(MXU)
        for c in range(MEM_PAIRS):
            k2_ref[c] = _stack_pair(mkv[:, c * 128:(c + 1) * 128])
            v2_ref[c] = _stack_pair(mkv[:, W_C + c * 128:W_C + (c + 1) * 128])
        lse_ref[...] = jnp.zeros_like(lse_ref)

        def tile(blk):
            rows = pl.ds(pl.multiple_of(blk * BLK, BLK), BLK)
            sT = []
            for c in range(MEM_PAIRS):
                s2 = _dot_nt(k2_ref[c], q_ref.at[c][rows, :])
                sT += [s2[0:MEM_LEN], s2[MEM_LEN:2 * MEM_LEN]]
            yield
            pnT, lse = _softmax_cols(_lanes(sT))
            yield
            for c in range(MEM_PAIRS):
                p2 = jnp.concatenate([pnT[:, 2 * c * BLK:(2 * c + 1) * BLK],
                                      pnT[:, (2 * c + 1) * BLK:(2 * c + 2) * BLK]], axis=0)
                o_ref.at[c][rows, :] = _dot_t0(p2, v2_ref[c])
            for h in range(MEM_H):
                lse_ref.at[blk][h:h + 1, :] = lse[:, h * BLK:(h + 1) * BLK]

        def loop(j, carry):
            _interleave([tile(j * MEM_UNROLL + u) for u in range(MEM_UNROLL)])
            return carry
        lax.fori_loop(0, N_QBLK // MEM_UNROLL, loop, 0)

    return pl.pallas_call(
        body, name="mem_attn_fwd", grid=(B,),
        in_specs=[pl.BlockSpec((None, MEM_PAIRS, SEQ, 128), lambda b: (b, 0, 0, 0)),
                  pl.BlockSpec((None, MEM_LEN, D_MODEL), lambda b: (b, 0, 0)),
                  pl.BlockSpec((D_MODEL, 2 * W_C), lambda b: (0, 0))],
        out_specs=[pl.BlockSpec((None, MEM_PAIRS, SEQ, 128), lambda b: (b, 0, 0, 0)),
                   pl.BlockSpec((None, N_QBLK, 8, 128), lambda b: (b, 0, 0, 0)),
                   pl.BlockSpec((None, MEM_LEN, 2 * W_C), lambda b: (b, 0, 0))],
        out_shape=[jax.ShapeDtypeStruct((B, MEM_PAIRS, SEQ, 128), F32),
                   jax.ShapeDtypeStruct((B, N_QBLK, 8, 128), F32),
                   jax.ShapeDtypeStruct((B, MEM_LEN, 2 * W_C), MXU)],
        scratch_shapes=[pltpu.VMEM((MEM_PAIRS, 2 * MEM_LEN, 128), MXU), pltpu.VMEM((MEM_PAIRS, 2 * MEM_LEN, 128), MXU)],
        compiler_params=_cparams(("arbitrary",)),
    )(qc, mem, w_mem)


def _mem_attn_bwd(qc, mkv, do, lse, delta, mem):
    B = qc.shape[0]

    def body(q_ref, mkv_ref, do_ref, lse_ref, delta_ref, mem_ref, dq_ref, dw_ref, dmkv_ref, k2_ref, v2_ref):
        @pl.when(pl.program_id(0) == 0)
        def _():
            dw_ref[...] = jnp.zeros_like(dw_ref)
        dmkv_ref[...] = jnp.zeros_like(dmkv_ref)
        for c in range(MEM_PAIRS):
            k2_ref[c] = _stack_pair(mkv_ref[:, c * 128:(c + 1) * 128])
            v2_ref[c] = _stack_pair(mkv_ref[:, W_C + c * 128:W_C + (c + 1) * 128])

        def tile(blk):
            rows = pl.ds(pl.multiple_of(blk * BLK, BLK), BLK)
            qcs, docs, sT, dpT = [], [], [], []
            for c in range(MEM_PAIRS):
                qc_, doc = q_ref.at[c][rows, :], do_ref.at[c][rows, :]
                s2, dp2 = _dot_nt(k2_ref[c], qc_), _dot_nt(v2_ref[c], doc)
                sT += [s2[0:MEM_LEN], s2[MEM_LEN:2 * MEM_LEN]]
                dpT += [dp2[0:MEM_LEN], dp2[MEM_LEN:2 * MEM_LEN]]
                qcs.append(qc_)
                docs.append(doc)
            lse_r = _lanes([lse_ref.at[blk][h:h + 1, :] for h in range(MEM_H)])
            delta_r = _lanes([delta_ref.at[blk][h:h + 1, :] for h in range(MEM_H)])
            yield
            pT = jnp.exp(_lanes(sT) - lse_r)
            dsT = pT * (_lanes(dpT) - delta_r)
            dsb, pb = dsT.astype(MXU), pT.astype(MXU)
            yield
            for c in range(MEM_PAIRS):
                q0, q1 = slice(2 * c * BLK, (2 * c + 1) * BLK), slice((2 * c + 1) * BLK, (2 * c + 2) * BLK)
                ds2 = jnp.concatenate([dsb[:, q0], dsb[:, q1]], axis=0)
                p2 = jnp.concatenate([pb[:, q0], pb[:, q1]], axis=0)
                dq_ref.at[c][rows, :] = _dot_t0(ds2, k2_ref[c])
                dmkv_ref[:, c * 128:(c + 1) * 128] += _pair_rows(_dot(ds2, qcs[c]), MEM_LEN)
                dmkv_ref[:, W_C + c * 128:W_C + (c + 1) * 128] += _pair_rows(_dot(p2, docs[c]), MEM_LEN)

        def loop(j, carry):
            _interleave([tile(j * MEM_UNROLL + u) for u in range(MEM_UNROLL)])
            return carry
        lax.fori_loop(0, N_QBLK // MEM_UNROLL, loop, 0)
        dw_ref[...] += _dot_tn(mem_ref[...], dmkv_ref[...].astype(MXU))

    stat = pl.BlockSpec((None, N_QBLK, 8, 128), lambda b: (b, 0, 0, 0))
    pairs = pl.BlockSpec((None, MEM_PAIRS, SEQ, 128), lambda b: (b, 0, 0, 0))
    return pl.pallas_call(
        body, name="mem_attn_bwd", grid=(B,),
        in_specs=[pairs, pl.BlockSpec((None, MEM_LEN, 2 * W_C), lambda b: (b, 0, 0)), pairs, stat, stat,
                  pl.BlockSpec((None, MEM_LEN, D_MODEL), lambda b: (b, 0, 0))],
        out_specs=[pairs, pl.BlockSpec((D_MODEL, 2 * W_C), lambda b: (0, 0))],
        out_shape=[jax.ShapeDtypeStruct((B, MEM_PAIRS, SEQ, 128), F32),
                   jax.ShapeDtypeStruct((D_MODEL, 2 * W_C), F32)],
        scratch_shapes=[pltpu.VMEM((MEM_LEN, 2 * W_C), F32),
                        pltpu.VMEM((MEM_PAIRS, 2 * MEM_LEN, 128), MXU), pltpu.VMEM((MEM_PAIRS, 2 * MEM_LEN, 128), MXU)],
        compiler_params=_cparams(("arbitrary",)),
    )(qc, mkv, do, lse, delta, mem)


def _headsum(t, e):
    if MXU == F32:
        return _dot(t, e)
    hi = t.astype(MXU)
    lo = (t - hi.astype(F32)).astype(MXU)
    return _dot(hi, e) + _dot(lo, e)


def _heads_to_rows(t, e):
    return sum(_dot_nt(e, part) for part in _split3(t))


POST_ROWS = 256


def _post(o_a, olse_b, o_c, z, x2, tgt, g, gain, bias, w_out, hsum, hrows):
    T = x2.shape[0]
    tm = 256
    nt = SEQ // tm

    def body(oa_ref, ob_ref, oc_ref, z_ref, x_ref, t_ref, g_ref, gain_ref, bias_ref, w_ref, e_ref, er_ref,
             gx_ref, doa_ref, dela_ref, dobb_ref, doc_ref, delc_ref, dz_ref, dw_ref, small_ref, loss_ref):
        @pl.when(pl.program_id(0) == 0)
        def _():
            dw_ref[...] = jnp.zeros_like(dw_ref)
            small_ref[...] = jnp.zeros_like(small_ref)
            loss_ref[...] = jnp.zeros_like(loss_ref)

        gg = g_ref[...]
        gain_v = gain_ref[...]
        gain_s = gain_v * (1.0 / D_MODEL)
        bias_v = bias_ref[...]
        w = w_ref[...]

        def rms(o):
            rr = lax.rsqrt(jnp.mean(o * o, axis=1, keepdims=True) + RMS_EPS)
            return o * rr, rr

        def rows_of(rs):
            oa = _lanes([oa_ref.at[c][rs, :] for c in range(SWA_Q // 2)])
            (o1, l1), (o4, l4), (o16, l16) = [
                (_lanes([ob_ref.at[p, 0][rs, :], ob_ref.at[p, 1][rs, :]]),
                 _lanes([ob_ref.at[p, 2][rs, :], ob_ref.at[p, 3][rs, :]])) for p in range(3)]
            mx = jnp.maximum(jnp.maximum(l1, l4), l16)
            e1, e4, e16 = jnp.exp(l1 - mx), jnp.exp(l4 - mx), jnp.exp(l16 - mx)
            den = e1 + e4 + e16
            ob = (e1 * o1 + e4 * o4 + e16 * o16) / den
            lse_b = mx + jnp.log(den)
            oc = _lanes([oc_ref.at[c][rs, :] for c in range(MEM_PAIRS)])
            na, ra = rms(oa)
            nb, rb = rms(ob)
            nc, rc = rms(oc)
            n = jnp.concatenate([na, nb, nc], axis=1)
            zz = z_ref[rs, :]
            sig = 0.5 * jnp.tanh(0.5 * zz) + 0.5
            sz = zz * sig
            gs = gg * sz
            u = n * gs
            r = ALPHA * x_ref[rs, :] + _dot(u.astype(MXU), w)
            rc0 = r - jnp.mean(r, axis=1, keepdims=True)
            rstd = lax.rsqrt(jnp.mean(rc0 * rc0, axis=1, keepdims=True) + LN_EPS)
            xhat = rc0 * rstd
            err = xhat * gain_v + bias_v - t_ref[rs, :]
            dxh = err * gain_s
            dr = rstd * (dxh - jnp.mean(dxh, axis=1, keepdims=True)
                         - xhat * jnp.mean(dxh * xhat, axis=1, keepdims=True))
            gx_ref[rs, :] = ALPHA * dr
            drb = dr.astype(MXU)
            du = _dot_nt(drb, w)
            dun = du * n
            dz = dun * (gg * (sig + sz * (1.0 - sig)))
            dz_ref[rs, :] = dz.astype(MXU)
            dn = du * gs

            def branch(lo, hi, nbr, rr):
                dnb = dn[:, lo:hi]
                return rr * (dnb - nbr * jnp.mean(dnb * nbr, axis=1, keepdims=True))

            def to_kernel(dob, o, do_ref, delta_ref):
                wd = dob.shape[1]
                for c in range(wd // 128):
                    do_ref.at[c][rs, :] = dob[:, c * 128:(c + 1) * 128].astype(do_ref.dtype)
                dT = _heads_to_rows(dob * o, er_ref[:, 0:wd])
                for jb in range((rs.stop - rs.start) // BLK):
                    delta_ref[rs.start // BLK + jb] = dT[0:8, jb * BLK:(jb + 1) * BLK]

            to_kernel(branch(0, W_A, na, ra), oa, doa_ref, dela_ref)
            to_kernel(branch(W_A + W_B, D_MIX, nc, rc), oc, doc_ref, delc_ref)
            dob = branch(W_A, W_A + W_B, nb, rb)
            for j, t in enumerate((dob, lse_b, _headsum(dob * ob, e_ref[...]))):
                for c in range(W_B // 128):
                    dobb_ref.at[j * (W_B // 128) + c][rs, :] = t[:, c * 128:(c + 1) * 128]
            csum = lambda t: jnp.sum(t, axis=0, keepdims=True)
            return (u, drb, jnp.sum(err * err), csum(err * xhat), csum(err), csum(dun * sz), csum(dz))

        parts = [rows_of(slice(k * POST_ROWS, (k + 1) * POST_ROWS)) for k in range(tm // POST_ROWS)]
        tot = [sum(p[i] for p in parts) for i in range(2, 7)]
        dw_ref[...] += _dot_tn(jnp.concatenate([p[0] for p in parts], axis=0),
                               jnp.concatenate([p[1] for p in parts], axis=0))
        loss_ref[...] += 0.5 * tot[0] * (1.0 / D_MODEL)
        small_ref[0:1, :] += tot[1] * (1.0 / D_MODEL)
        small_ref[1:2, :] += tot[2] * (1.0 / D_MODEL)
        small_ref[2:3, :] += tot[3]
        small_ref[3:4, :] += tot[4]

    B = T // SEQ
    row = lambda w: pl.BlockSpec((tm, w), lambda i: (i, 0))
    full = lambda a, b: pl.BlockSpec((a, b), lambda i: (0, 0))
    chunked = lambda n: pl.BlockSpec((None, n, tm, 128), lambda i: (i // nt, 0, i % nt, 0))
    stat = pl.BlockSpec((None, tm // BLK, 8, 128), lambda i: (i // nt, i % nt, 0, 0))
    return pl.pallas_call(
        body, name="post_fwd_bwd", grid=(T // tm,),
        in_specs=[chunked(SWA_Q // 2), pl.BlockSpec((None, 3, 4, tm, 128), lambda i: (i // nt, 0, 0, i % nt, 0)),
                  chunked(MEM_PAIRS),
                  row(D_MIX), row(D_MODEL), row(D_MODEL),
                  full(1, D_MIX), full(1, D_MODEL), full(1, D_MODEL), full(D_MIX, D_MODEL), full(W_B, W_B),
                  full(PICK_ROWS, W_A)],
        out_specs=[row(D_MODEL), chunked(SWA_Q // 2), stat, chunked(6), chunked(MEM_PAIRS), stat, row(D_MIX),
                   full(D_MIX, D_MODEL), full(8, D_MODEL), full(8, 128)],
        out_shape=[jax.ShapeDtypeStruct((T, D_MODEL), F32),
                   jax.ShapeDtypeStruct((B, SWA_Q // 2, SEQ, 128), MXU),
                   jax.ShapeDtypeStruct((B, N_QBLK, 8, 128), F32),
                   jax.ShapeDtypeStruct((B, 6, SEQ, 128), F32),
                   jax.ShapeDtypeStruct((B, MEM_PAIRS, SEQ, 128), MXU),
                   jax.ShapeDtypeStruct((B, N_QBLK, 8, 128), F32),
                   jax.ShapeDtypeStruct((T, D_MIX), MXU),
                   jax.ShapeDtypeStruct((D_MIX, D_MODEL), F32),
                   jax.ShapeDtypeStruct((8, D_MODEL), F32),
                   jax.ShapeDtypeStruct((8, 128), F32)],
        compiler_params=_cparams(("arbitrary",)),
    )(o_a, olse_b, o_c, z, x2, tgt, g, gain, bias, w_out, hsum, hrows)


def _dh_build(dqkva, dqkvb, dqc, dz, tab):
    T = dz.shape[0]
    tm = 512
    nt = SEQ // tm
    HQ = D_IN - D_MIX

    def body(da_ref, db6_ref, dqc_ref, dz_ref, tab_ref, dh_ref, db_ref):
        @pl.when(pl.program_id(0) == 0)
        def _():
            db_ref[...] = jnp.zeros_like(db_ref)
        tab = tab_ref[...]
        lo = lax.broadcasted_iota(jnp.int32, (tm, 128), 1) < HEAD

        def kv_grad(c):
            g0, g1 = da_ref[c], da_ref[c + 1]
            return jnp.where(lo, g0 + pltpu.roll(g0, HEAD, 1), g1 + pltpu.roll(g1, HEAD, 1))

        parts = [_rope(_lanes([da_ref[c] for c in range(SWA_Q // 2)]), tab, -1) * Q_SCALE,
                 _rope(kv_grad(4), tab, -1),
                 kv_grad(6),
                 _rope(_lanes([db6_ref[0], db6_ref[1]]), tab, -1) * Q_SCALE,
                 _rope(_lanes([db6_ref[2], db6_ref[3]]), tab, -1),
                 _lanes([db6_ref[4], db6_ref[5]]),
                 _lanes([dqc_ref[c] for c in range(MEM_PAIRS)]) * Q_SCALE]
        dhq = jnp.concatenate(parts, axis=1)
        db_ref[0:1, :] += jnp.sum(dhq, axis=0, keepdims=True)
        dh_ref[:, 0:HQ] = dhq.astype(MXU)
        dh_ref[:, HQ:D_IN] = dz_ref[...]

    row = lambda w: pl.BlockSpec((tm, w), lambda i: (i, 0))
    chunked = lambda n: pl.BlockSpec((None, n, tm, 128), lambda i: (i // nt, 0, i % nt, 0))
    return pl.pallas_call(
        body, name="dh_build", grid=(T // tm,),
        in_specs=[chunked(SWA_CHUNKS), chunked(6), chunked(MEM_PAIRS), row(D_MIX),
                  pl.BlockSpec((tm, 384), lambda i: (i % nt, 0))],
        out_specs=[row(D_IN), pl.BlockSpec((8, HQ), lambda i: (0, 0))],
        out_shape=[jax.ShapeDtypeStruct((T, D_IN), MXU), jax.ShapeDtypeStruct((8, HQ), F32)],
        compiler_params=_cparams(("arbitrary",)),
    )(dqkva, dqkvb, dqc, dz, tab)


TAIL_TK = 512
TAIL_TN = D_IN // 2
TAIL_TM = 256
REDUCE_ROWS = 128


def _tail(xt, dh, gx1, w_in, sends, owns, small_g):
    T = xt.shape[1]
    kt = T // TAIL_TK
    ndw = (D_IN // TAIL_TN) * kt
    nsteps = ndw + T // TAIL_TM
    per_pass = TAIL_TN // COLS_PER_DEV
    pay = dh.dtype
    shapes = [(D_MODEL, COLS_PER_DEV), owns[0].shape, owns[1].shape, small_g.shape]
    n_arr = len(shapes)

    def body(xt_ref, dh1_ref, dh2_ref, gx_ref, w_hbm, smem_ref, sout_ref, omem_ref, oout_ref, sg_ref,
             dx_ref, gin_ref, gmem_ref, gout_ref, gsm_ref,
             acc_ref, w_ref, stage_ref, ownin_ref, lin_ref, lmem_ref, lout_ref, lsm_ref,
             send_sems, recv_sems, w_sem):
        s = pl.program_id(0)
        x, y, c = _my_pos()
        me = 4 * x + 2 * y + c
        lands = (lin_ref, lmem_ref, lout_ref, lsm_ref)

        def src_of(a, j):
            return (stage_ref.at[j], smem_ref.at[j], sout_ref.at[j], sg_ref)[a]

        def to_peer(a, j):
            return pltpu.make_async_remote_copy(
                src_ref=src_of(a, j), dst_ref=lands[a].at[me], send_sem=send_sems.at[a, j],
                recv_sem=recv_sems.at[a, me], device_id=_dev_coords(j), device_id_type=MESH)

        def from_peer(a, m):
            return pltpu.make_async_remote_copy(
                src_ref=lands[a].at[m], dst_ref=lands[a].at[m], send_sem=send_sems.at[a, m],
                recv_sem=recv_sems.at[a, m], device_id=_dev_coords(m), device_id_type=MESH)

        w_copy = pltpu.make_async_copy(w_hbm, w_ref, w_sem)

        @pl.when(s == 0)
        def _():
            w_copy.start()
            for j in range(N_DEV):
                @pl.when(me != j)
                def _(j=j):
                    for a in range(1, n_arr):
                        to_peer(a, j).start()
            for a in range(n_arr - 1):
                lands[a][me] = jnp.zeros(shapes[a], lands[a].dtype)
            lsm_ref[me] = sg_ref[...]

        @pl.when(s < ndw)
        def _():
            k = s % kt

            @pl.when(k == 0)
            def _():
                acc_ref[...] = jnp.zeros_like(acc_ref)
            acc_ref[...] += _dot(xt_ref[...], dh1_ref[...])

            for p in range(D_IN // TAIL_TN):
                @pl.when(s == p * kt + kt - 1)
                def _(p=p):
                    for jj in range(per_pass):
                        j = p * per_pass + jj
                        blk = acc_ref[:, jj * COLS_PER_DEV:(jj + 1) * COLS_PER_DEV]
                        stage_ref[j] = blk.astype(pay)

                        @pl.when(me == j)
                        def _(blk=blk):
                            ownin_ref[...] = blk
                        pl.when(me != j)(to_peer(0, j).start)

        @pl.when(s >= ndw)
        def _():
            pl.when(s == ndw)(w_copy.wait)
            dx_ref[...] = _dot_nt(dh2_ref[...], w_ref[...]) + gx_ref[...]

        @pl.when(s == nsteps - 1)
        def _():
            for m in range(N_DEV):
                @pl.when(me != m)
                def _(m=m):
                    for a in range(n_arr):
                        from_peer(a, m).wait_recv()
            for j in range(N_DEV):
                @pl.when(me != j)
                def _(j=j):
                    for a in range(n_arr):
                        to_peer(a, j).wait_send()
            for a, (own, out) in enumerate(((ownin_ref, gin_ref), (omem_ref, gmem_ref), (oout_ref, gout_ref))):
                def chunk(i, carry, a=a, own=own, out=out):
                    rs = pl.ds(pl.multiple_of(i * REDUCE_ROWS, REDUCE_ROWS), REDUCE_ROWS)
                    g = own[rs, :]
                    for m in range(N_DEV):
                        g = g + lands[a][m, rs, :].astype(F32)
                    out[rs, :] = g
                    return carry
                lax.fori_loop(0, shapes[a][0] // REDUCE_ROWS, chunk, 0)
            g = lsm_ref[0]
            for m in range(1, N_DEV):
                g = g + lsm_ref[m]
            gsm_ref[...] = g

    dw_step = lambda s: jnp.minimum(s, ndw - 1)
    dx_step = lambda s: jnp.maximum(s - ndw, 0)
    any_spec = pl.BlockSpec(memory_space=pl.ANY)
    vmem = pl.BlockSpec(memory_space=pltpu.VMEM)
    scratch = [pltpu.VMEM((D_MODEL, TAIL_TN), F32), pltpu.VMEM((D_MODEL, D_IN), w_in.dtype),
               pltpu.VMEM((N_DEV,) + shapes[0], pay), pltpu.VMEM(shapes[0], F32)]
    scratch += [pltpu.VMEM((N_DEV,) + shapes[a], pay) for a in range(n_arr - 1)]
    scratch += [pltpu.VMEM((N_DEV,) + shapes[-1], F32),
                pltpu.SemaphoreType.DMA((n_arr, N_DEV)), pltpu.SemaphoreType.DMA((n_arr, N_DEV)),
                pltpu.SemaphoreType.DMA]
    return pl.pallas_call(
        body, name="tail_dw_dx_reduce", grid=(nsteps,),
        in_specs=[pl.BlockSpec((D_MODEL, TAIL_TK), lambda s: (0, dw_step(s) % kt)),
                  pl.BlockSpec((TAIL_TK, TAIL_TN), lambda s: (dw_step(s) % kt, dw_step(s) // kt)),
                  pl.BlockSpec((TAIL_TM, D_IN), lambda s: (dx_step(s), 0)),
                  pl.BlockSpec((TAIL_TM, D_MODEL), lambda s: (dx_step(s), 0)),
                  any_spec, any_spec, any_spec, vmem, vmem, vmem],
        out_specs=[pl.BlockSpec((TAIL_TM, D_MODEL), lambda s: (dx_step(s), 0)), vmem, vmem, vmem, vmem],
        out_shape=[jax.ShapeDtypeStruct((T, D_MODEL), F32)] + [jax.ShapeDtypeStruct(sh, F32) for sh in shapes],
        scratch_shapes=scratch,
        compiler_params=_cparams(("arbitrary",)),
    )(xt, dh, dh, gx1, w_in, *sends, *owns, small_g)


def _adam_update(grads, params, carried):
    n = len(grads)

    def body(*refs):
        g_refs, p_refs, o_refs = refs[1:1 + n], refs[1 + n:1 + 4 * n], refs[2 + 4 * n:]
        for a in range(n):
            rows = g_refs[a].shape[0]
            cr = REDUCE_ROWS if rows % REDUCE_ROWS == 0 else rows
            w_ref, m_ref, v_ref = p_refs[3 * a:3 * a + 3]
            go_ref, d_ref, nm_ref, nv_ref = o_refs[4 * a:4 * a + 4]

            def chunk(i, carry, cr=cr, g_ref=g_refs[a], w_ref=w_ref, m_ref=m_ref, v_ref=v_ref,
                      go_ref=go_ref, d_ref=d_ref, nm_ref=nm_ref, nv_ref=nv_ref):
                rs = pl.ds(pl.multiple_of(i * cr, cr), cr)
                g = g_ref[rs, :]
                go_ref[rs, :] = g
                d_ref[rs, :], nm_ref[rs, :], nv_ref[rs, :] = _adamw(w_ref[rs, :], g, m_ref[rs, :], v_ref[rs, :])
                return carry
            lax.fori_loop(0, rows // cr, chunk, 0)

    vmem = pl.BlockSpec(memory_space=pltpu.VMEM)
    any_spec = pl.BlockSpec(memory_space=pl.ANY)
    flat = [p for grp in params for p in grp]
    outs = pl.pallas_call(
        body, name="adamw", in_specs=[any_spec] + [vmem] * (4 * n), out_specs=[any_spec] + [vmem] * (4 * n),
        out_shape=[jax.ShapeDtypeStruct(carried.shape, carried.dtype)]
        + [jax.ShapeDtypeStruct(g.shape, F32) for g in grads for _ in range(4)],
        input_output_aliases={0: 0},
        compiler_params=pltpu.CompilerParams(vmem_limit_bytes=VMEM_LIMIT),
    )(carried, *grads, *flat)
    return [outs[1 + 4 * a:5 + 4 * a] for a in range(n)], outs[0]


def _step(x, mem, w_in_s, w_mem_s, w_out_s, b_in, sinks, g, gain, bias, tgt):
    B = x.shape[0]
    T = B * SEQ
    x2 = x.reshape(T, D_MODEL)
    t2 = tgt.reshape(T, D_MODEL)
    tab = _rope_table()
    lane = jnp.arange(W_A)
    hsum = (lane[:W_B, None] // HEAD == lane[None, :W_B] // HEAD).astype(MXU)
    hrows = (jnp.arange(PICK_ROWS)[:, None] == lane[None, :] // HEAD).astype(MXU)
    me = 4 * lax.axis_index("x") + 2 * lax.axis_index("y") + lax.axis_index("c")

    (w_in_all,) = _gather_weights([w_in_s])
    qkva, qkvb, qc, z, w_in, xt, w_mem_all, w_out_all = _in_proj(x2, w_in_all, b_in, tab, [w_mem_s, w_out_s])
    w_mem = w_mem_all.reshape(D_MODEL, 2 * W_C)
    w_out = w_out_all.reshape(D_MIX, D_MODEL)

    o_a, lse_a = _swa_fwd(qkva, sinks)
    olse_b = _dil_fwd(qkvb)
    o_c, lse_c, mkv = _mem_attn_fwd(qc, mem, w_mem)

    gx1, do_a, delta_a, dobb, do_c, delta_c, dz, dw_out, small, loss = _post(
        o_a, olse_b, o_c, z, x2, t2, g, gain, bias, w_out, hsum, hrows)

    dqkva, dsink = _swa_bwd(qkva, do_a, lse_a, delta_a, sinks)
    dqkvb = _dil_bwd(qkvb, dobb)
    dqc, dw_mem = _mem_attn_bwd(qc, mkv, do_c, lse_c, delta_c, mem)
    dh, dbq = _dh_build(dqkva, dqkvb, dqc, dz, tab)

    small_g = _pack_small(dict(b_in=jnp.concatenate([dbq[0], small[3]]), sinks=dsink[:, 0], g=small[2],
                               gain=small[0], bias=small[1], loss=loss[0, 0]))
    blocks = [dw_mem.reshape(N_DEV, ROWS_PER_DEV, 2 * W_C), dw_out.reshape(N_DEV, ROWS_PER_DEV, D_MODEL)]
    sends = [b.astype(MXU) for b in blocks]
    owns = [lax.dynamic_index_in_dim(b, me, axis=0, keepdims=False) for b in blocks]
    grad_x, g_in, g_mem, g_out, g_small = _tail(xt, dh, gx1, w_in, sends, owns, small_g)
    return grad_x.reshape(B, SEQ, D_MODEL), g_in, g_mem, g_out, g_small


def _my_pos():
    return lax.axis_index("x"), lax.axis_index("y"), lax.axis_index("c")


def _gather_weights(shards):
    n_arr = len(shards)

    def body(*refs):
        ins, outs = refs[0:n_arr], refs[n_arr:2 * n_arr]
        send_sems, recv_sems, local_sems = refs[2 * n_arr:]
        x, y, c = _my_pos()
        me, sibling = (x, y, c), (x, y, 1 - c)
        chips = [(1 - x, y), (x, 1 - y), (1 - x, 1 - y)]

        def slot(a, pos):
            return outs[a].at[4 * pos[0] + 2 * pos[1] + pos[2]]

        def copy(a, k, block, to, src=None):
            return pltpu.make_async_remote_copy(
                src_ref=slot(a, block) if src is None else src, dst_ref=slot(a, block),
                send_sem=send_sems.at[a, k], recv_sem=recv_sems.at[a, k],
                device_id=to, device_id_type=MESH)

        mine = [pltpu.make_async_copy(ins[a], slot(a, me), local_sems.at[a]) for a in range(n_arr)]
        for cp in mine:
            cp.start()
        first = []
        for a in range(n_arr):
            first.append(copy(a, 0, me, sibling, src=ins[a]))
            first += [copy(a, 1 + j, me, (*chip, c), src=ins[a]) for j, chip in enumerate(chips)]
        for cp in first:
            cp.start()
        passed = []
        for j, chip in enumerate(chips):
            for a in range(n_arr):
                copy(a, 1 + j, (*chip, c), me).wait_recv()
                fwd = copy(a, 4 + j, (*chip, c), sibling)
                fwd.start()
                passed.append(fwd)
        for a in range(n_arr):
            copy(a, 0, sibling, me).wait_recv()
            for j, chip in enumerate(chips):
                copy(a, 4 + j, (*chip, 1 - c), me).wait_recv()
        for cp in first + passed:
            cp.wait_send()
        for cp in mine:
            cp.wait()

    any_spec = pl.BlockSpec(memory_space=pl.ANY)
    return pl.pallas_call(
        body, name="gather_weights",
        in_specs=[any_spec] * n_arr, out_specs=[any_spec] * n_arr,
        out_shape=[jax.ShapeDtypeStruct((N_DEV,) + s.shape, s.dtype) for s in shards],
        scratch_shapes=[pltpu.SemaphoreType.DMA((n_arr, 7)), pltpu.SemaphoreType.DMA((n_arr, 7)),
                        pltpu.SemaphoreType.DMA((n_arr,))],
    )(*shards)


def _adamw(w, g, m, v):
    m = ADAM_B1 * m + (1.0 - ADAM_B1) * g
    v = ADAM_B2 * v + (1.0 - ADAM_B2) * (g * g)
    m_hat = m / (1.0 - ADAM_B1 ** ADAM_STEP)
    v_hat = v / (1.0 - ADAM_B2 ** ADAM_STEP)
    delta = -ADAM_LR * (m_hat / (jnp.sqrt(v_hat) + ADAM_EPS) + ADAM_WD * w)
    return delta, m, v


_SMALL_SIZES = (("b_in", D_IN), ("g", D_MIX), ("gain", D_MODEL), ("bias", D_MODEL), ("sinks", SWA_Q), ("loss", 1))


def _pack_small(d):
    flat = jnp.concatenate([jnp.reshape(d[k], (-1,)).astype(F32) if k in d else jnp.zeros((n,), F32)
                            for k, n in _SMALL_SIZES])
    flat = jnp.pad(flat, (0, SMALL_ROWS * 128 - flat.shape[0]))
    return flat.reshape(SMALL_ROWS, 128)


def _unpack_small(p):
    flat = p.reshape(-1)
    out, off = {}, 0
    for k, n in _SMALL_SIZES:
        out[k] = flat[off:off + n].reshape(1, n)
        off += n
    return out


def kernel(x, mem, w_in, b_in, w_mem, attn_sinks, g_branch, w_out, ln_gain, ln_bias, loss_target, m_w_in, m_b_in, m_w_mem, m_attn_sinks, m_g_branch, m_w_out, m_ln_gain, m_ln_bias, v_w_in, v_b_in, v_w_mem, v_attn_sinks, v_g_branch, v_w_out, v_ln_gain, v_ln_bias):
    grad_x, g_in, g_mem, g_out, g_small = _step(
        x, mem, w_in[0].astype(MXU), w_mem[0].astype(MXU), w_out[0].astype(MXU), b_in, attn_sinks[0],
        g_branch, ln_gain, ln_bias, loss_target)

    small_w = _pack_small(dict(b_in=b_in, g=g_branch, gain=ln_gain, bias=ln_bias, sinks=attn_sinks))
    small_m = _pack_small(dict(b_in=m_b_in, g=m_g_branch, gain=m_ln_gain, bias=m_ln_bias, sinks=m_attn_sinks))
    small_v = _pack_small(dict(b_in=v_b_in, g=v_g_branch, gain=v_ln_gain, bias=v_ln_bias, sinks=v_attn_sinks))
    grads = [g_in, g_mem, g_out, g_small]
    params = [(w_in[0], m_w_in[0], v_w_in[0]), (w_mem[0], m_w_mem[0], v_w_mem[0]),
              (w_out[0], m_w_out[0], v_w_out[0]), (small_w, small_m, small_v)]
    res, grad_x = _adam_update(grads, params, grad_x)
    big = [[r[None] for r in res[a]] for a in range(3)]
    sm = [_unpack_small(r) for r in res[3]]

    def group(i):
        return (big[0][i], sm[i]["b_in"], big[1][i], sm[i]["sinks"], sm[i]["g"], big[2][i],
                sm[i]["gain"], sm[i]["bias"])

    loss = sm[0]["loss"].reshape(())
    return (loss, grad_x, *group(0), *group(1), *group(2), *group(3))
```

```python
import functools
import math

import jax
import jax.numpy as jnp
from jax import lax
from jax.experimental import pallas as pl
from jax.experimental.pallas import tpu as pltpu

F32 = jnp.float32
MXU = jnp.bfloat16

D_MODEL = 1024
SEQ = 2048
HEAD = 64
BLK = 128
SWA_Q, SWA_KV = 8, 2
DIL_H = 4
MEM_H = 4
MEM_LEN = 256
W_A, W_KVA, W_B, W_C = 512, 128, 256, 256
D_MIX = 1024
D_IN = 2816
N_DEV = 8
COLS_PER_DEV = D_IN // N_DEV
ROWS_PER_DEV = D_MODEL // N_DEV
ROPE_THETA = 10000.0
LN_EPS = 1e-5
RMS_EPS = 1e-6
ALPHA = 2.0 ** 0.25
Q_SCALE = HEAD ** -0.5
NEG = -1e30
SMALL_ROWS = 48
VMEM_LIMIT = 56 * 1024 * 1024

ADAM_LR = 0.001
ADAM_B1 = 0.9
ADAM_B2 = 0.999
ADAM_EPS = 1e-08
ADAM_WD = 0.01
ADAM_STEP = 10

MESH = pl.DeviceIdType.MESH


def _cparams(sem=None):
    return pltpu.CompilerParams(dimension_semantics=sem, vmem_limit_bytes=VMEM_LIMIT)


def _dot(a, b):
    return jnp.dot(a, b, preferred_element_type=F32)


def _dot_nt(a, b):
    return lax.dot_general(a, b, (((1,), (1,)), ((), ())), preferred_element_type=F32)


def _dot_t0(a, b):
    return lax.dot_general(a, b, (((0,), (0,)), ((), ())), preferred_element_type=F32)


def _dot_tn(a, b):
    return jnp.dot(a.T.astype(MXU), b, preferred_element_type=F32)


def _rope(t, tab, sign):
    cos, sa, sb = tab[:, 0:128], tab[:, 128:256], tab[:, 256:384]
    outs = []
    for c in range(t.shape[1] // 128):
        tc = t[:, c * 128:(c + 1) * 128]
        r = pltpu.roll(tc, 96, 1) * sa + pltpu.roll(tc, 32, 1) * sb
        outs.append(tc * cos + r if sign > 0 else tc * cos - r)
    return outs[0] if len(outs) == 1 else jnp.concatenate(outs, axis=1)


def _rope_table():
    pos = jnp.arange(SEQ, dtype=F32)
    inv = ROPE_THETA ** (-jnp.arange(0, HEAD, 2, dtype=F32) / HEAD)
    ang = pos[:, None] * inv[None, :]
    ang = jnp.concatenate([ang, ang], axis=-1)
    cos, sin = jnp.cos(ang), jnp.sin(ang)
    lane = jnp.arange(HEAD)[None, :]
    sa = jnp.where(lane < HEAD // 2, -sin, 0.0)
    sb = jnp.where(lane >= HEAD // 2, sin, 0.0)
    two = lambda t: jnp.concatenate([t, t], axis=-1)
    return jnp.concatenate([two(cos), two(sa), two(sb)], axis=-1).astype(F32)


def _dev_coords(j):
    return (j >> 2, (j >> 1) & 1, j & 1)


def _in_proj(x2, w_all, b_in, tab, late_shards):
    T = x2.shape[0]
    tm = 512
    n_late = len(late_shards)

    def body(x_ref, wall_ref, b_ref, tab_ref, *rest):
        late_in, rest = rest[:n_late], rest[n_late:]
        qkva_ref, qkvb_ref, qc_ref, z_ref, w_ref, xt_ref = rest[:6]
        late_out = rest[6:6 + n_late]
        send_sems, recv_sems, local_sems = rest[6 + n_late:]
        step, last = pl.program_id(0), pl.num_programs(0) - 1
        x, y, c = _my_pos()
        me = 4 * x + 2 * y + c

        def to_peer(a, j):
            return pltpu.make_async_remote_copy(
                src_ref=late_in[a], dst_ref=late_out[a].at[me], send_sem=send_sems.at[a, j],
                recv_sem=recv_sems.at[a, me], device_id=_dev_coords(j), device_id_type=MESH)

        def from_peer(a, m):
            return pltpu.make_async_remote_copy(
                src_ref=late_out[a].at[m], dst_ref=late_out[a].at[m], send_sem=send_sems.at[a, m],
                recv_sem=recv_sems.at[a, m], device_id=_dev_coords(m), device_id_type=MESH)

        def mine(a):
            return pltpu.make_async_copy(late_in[a], late_out[a].at[me], local_sems.at[a])

        @pl.when(step == 0)
        def _():
            for a in range(n_late):
                mine(a).start()
                for j in range(N_DEV):
                    pl.when(me != j)(to_peer(a, j).start)
            for j in range(N_DEV):
                w_ref[:, j * COLS_PER_DEV:(j + 1) * COLS_PER_DEV] = wall_ref[j]

        xb = x_ref[...].astype(MXU)
        xt_ref[...] = x_ref[...].T.astype(MXU)
        tab = tab_ref[...]

        def seg(c0, c1):
            return _dot(xb, w_ref[:, c0:c1]) + b_ref[:, c0:c1]

        qa = (_rope(seg(0, 512), tab, 1) * Q_SCALE).astype(MXU)
        for c in range(SWA_Q // 2):
            qkva_ref[c] = qa[:, c * 128:(c + 1) * 128]
        lo = lax.broadcasted_iota(jnp.int32, (tm, 128), 1) < HEAD
        for j, t in enumerate((_rope(seg(512, 640), tab, 1), seg(640, 768))):
            other = pltpu.roll(t, HEAD, 1)
            qkva_ref[4 + 2 * j] = jnp.where(lo, t, other).astype(MXU)
            qkva_ref[5 + 2 * j] = jnp.where(lo, other, t).astype(MXU)
        qkvb = (_rope(seg(768, 1024), tab, 1) * Q_SCALE, _rope(seg(1024, 1280), tab, 1), seg(1280, 1536))
        for j, t in enumerate(qkvb):
            for c in range(2):
                qkvb_ref[2 * j + c] = t[:, c * 128:(c + 1) * 128]
        qc = (seg(1536, 1792) * Q_SCALE).astype(MXU)
        for c in range(MEM_H // 2):
            qc_ref[c] = qc[:, c * 128:(c + 1) * 128]
        z_ref[...] = seg(1792, 2816)

        @pl.when(step == last)
        def _():
            for a in range(n_late):
                mine(a).wait()
                for m in range(N_DEV):
                    pl.when(me != m)(from_peer(a, m).wait_recv)
                for j in range(N_DEV):
                    pl.when(me != j)(to_peer(a, j).wait_send)

    nt = SEQ // tm
    any_spec = pl.BlockSpec(memory_space=pl.ANY)
    chunked = lambda n: pl.BlockSpec((None, n, tm, 128), lambda i: (i // nt, 0, i % nt, 0))
    return pl.pallas_call(
        body, name="in_proj_fwd",
        grid=(T // tm,),
        in_specs=[pl.BlockSpec((tm, D_MODEL), lambda i: (i, 0)),
                  pl.BlockSpec((N_DEV, D_MODEL, COLS_PER_DEV), lambda i: (0, 0, 0)),
                  pl.BlockSpec((1, D_IN), lambda i: (0, 0)),
                  pl.BlockSpec((tm, 384), lambda i: (i % nt, 0))] + [any_spec] * n_late,
        out_specs=[chunked(SWA_CHUNKS), chunked(6), chunked(MEM_H // 2),
                   pl.BlockSpec((tm, D_MIX), lambda i: (i, 0)),
                   pl.BlockSpec((D_MODEL, D_IN), lambda i: (0, 0)),
                   pl.BlockSpec((D_MODEL, tm), lambda i: (0, i))] + [any_spec] * n_late,
        out_shape=[jax.ShapeDtypeStruct((T // SEQ, SWA_CHUNKS, SEQ, 128), MXU),
                   jax.ShapeDtypeStruct((T // SEQ, 6, SEQ, 128), F32),
                   jax.ShapeDtypeStruct((T // SEQ, MEM_H // 2, SEQ, 128), MXU),
                   jax.ShapeDtypeStruct((T, D_MIX), F32),
                   jax.ShapeDtypeStruct((D_MODEL, D_IN), w_all.dtype),
                   jax.ShapeDtypeStruct((D_MODEL, T), MXU)]
        + [jax.ShapeDtypeStruct((N_DEV,) + s.shape, s.dtype) for s in late_shards],
        scratch_shapes=[pltpu.SemaphoreType.DMA((n_late, N_DEV)), pltpu.SemaphoreType.DMA((n_late, N_DEV)),
                        pltpu.SemaphoreType.DMA((n_late,))],
        compiler_params=_cparams(("arbitrary",)),
    )(x2, w_all, b_in, tab, *late_shards)


CHAIN = 4


def _band_bias(max_dist):
    kj = lax.broadcasted_iota(jnp.int32, (2 * BLK, BLK), 0)
    qi = lax.broadcasted_iota(jnp.int32, (2 * BLK, BLK), 1)
    dist = qi + BLK - kj
    band = jnp.where((dist >= 0) & (dist <= max_dist), 0.0, NEG).astype(F32)
    k1 = lax.broadcasted_iota(jnp.int32, (BLK, BLK), 0)
    q1 = lax.broadcasted_iota(jnp.int32, (BLK, BLK), 1)
    first = jnp.where((q1 - k1 >= 0) & (q1 - k1 <= max_dist), 0.0, NEG).astype(F32)
    return jnp.concatenate([band] * CHAIN, axis=1), jnp.concatenate([first] * CHAIN, axis=1)


def _lanes(parts):
    return jnp.concatenate(parts, axis=1)


PICK_ROWS = 16


def _stack_pair(t):
    lo = (lax.broadcasted_iota(jnp.int32, t.shape, 1) < HEAD).astype(F32)
    return jnp.concatenate([t * lo, t * (1.0 - lo)], axis=0).astype(MXU)


def _pair_rows(x, n):
    lo = lax.broadcasted_iota(jnp.int32, (n, 128), 1) < HEAD
    return jnp.where(lo, x[0:n], x[n:2 * n])


def _split3(t):
    if MXU == F32:
        return (t,)
    hi = t.astype(MXU)
    r = t - hi.astype(F32)
    mid = r.astype(MXU)
    return hi, mid, (r - mid.astype(F32)).astype(MXU)


def _interleave(tiles):
    tiles = list(tiles)
    while tiles:
        for t in list(tiles):
            try:
                next(t)
            except StopIteration:
                tiles.remove(t)


def _softmax_cols(sT, sinkrow=None):
    m = jnp.max(sT, axis=0, keepdims=True)
    if sinkrow is not None:
        m = jnp.maximum(m, sinkrow)
    pT = jnp.exp(sT - m)
    l = jnp.sum(pT, axis=0, keepdims=True)
    if sinkrow is not None:
        l = l + jnp.exp(sinkrow - m)
    return (pT * (1.0 / l)).astype(MXU), m + jnp.log(l)


SWA_CHUNKS = 8
SWA_UNROLL = 3
N_QBLK = SEQ // BLK


def _swa_fwd(qkva, sinks):
    B = qkva.shape[0]
    G = SWA_Q // SWA_KV

    def body(sink_ref, qkv_ref, o_ref, lse_ref):
        band, first = _band_bias(BLK - 1)
        sinkrows = [_lanes([jnp.full((1, BLK), sink_ref[G * hk + j], F32) for j in range(G)])
                    for hk in range(SWA_KV)]

        def tile(hk, blk, rows_q, rows_k, bias):
            nk = bias.shape[0]
            k2 = _stack_pair(qkv_ref.at[4 + hk][rows_k, :])
            sT = []
            for c in (2 * hk, 2 * hk + 1):
                s2 = _dot_nt(k2, qkv_ref.at[c][rows_q, :])
                sT += [s2[0:nk], s2[nk:2 * nk]]
            yield
            pnT, lse = _softmax_cols(_lanes(sT) + bias, sinkrows[hk])
            yield
            v2 = _stack_pair(qkv_ref.at[6 + hk][rows_k, :])
            for j, c in enumerate((2 * hk, 2 * hk + 1)):
                p2 = jnp.concatenate([pnT[:, 2 * j * BLK:(2 * j + 1) * BLK],
                                      pnT[:, (2 * j + 1) * BLK:(2 * j + 2) * BLK]], axis=0)
                o_ref.at[c][rows_q, :] = _dot_t0(p2, v2)
            for j in range(G):
                lse_ref.at[blk][G * hk + j:G * hk + j + 1, :] = lse[:, j * BLK:(j + 1) * BLK]

        def tiles_at(i):
            r0 = pl.multiple_of(i * BLK, BLK)
            rk = pl.multiple_of(i * BLK - BLK, BLK)
            return [tile(hk, i, pl.ds(r0, BLK), pl.ds(rk, 2 * BLK), band) for hk in range(SWA_KV)]

        _interleave([tile(hk, 0, pl.ds(0, BLK), pl.ds(0, BLK), first) for hk in range(SWA_KV)])

        def loop(j, carry):
            _interleave([t for u in range(SWA_UNROLL) for t in tiles_at(1 + j * SWA_UNROLL + u)])
            return carry
        lax.fori_loop(0, (N_QBLK - 1) // SWA_UNROLL, loop, 0)

    return pl.pallas_call(
        body, name="swa_fwd", grid=(B,),
        in_specs=[pl.BlockSpec(memory_space=pltpu.SMEM),
                  pl.BlockSpec((None, SWA_CHUNKS, SEQ, 128), lambda b: (b, 0, 0, 0))],
        out_specs=[pl.BlockSpec((None, SWA_Q // 2, SEQ, 128), lambda b: (b, 0, 0, 0)),
                   pl.BlockSpec((None, N_QBLK, 8, 128), lambda b: (b, 0, 0, 0))],
        out_shape=[jax.ShapeDtypeStruct((B, SWA_Q // 2, SEQ, 128), F32),
                   jax.ShapeDtypeStruct((B, N_QBLK, 8, 128), F32)],
        compiler_params=_cparams(("arbitrary",)),
    )(sinks, qkva)


def _swa_bwd(qkva, do, lse, delta, sinks):
    B = qkva.shape[0]
    G = SWA_Q // SWA_KV

    def body(sink_ref, qkv_ref, do_ref, lse_ref, delta_ref, dq_ref, dsink_ref):
        band, first = _band_bias(BLK - 1)
        sinkrows = [_lanes([jnp.full((1, BLK), sink_ref[G * hk + j], F32) for j in range(G)])
                    for hk in range(SWA_KV)]

        @pl.when(pl.program_id(0) == 0)
        def _():
            dsink_ref[...] = jnp.zeros_like(dsink_ref)
        for c in range(4, SWA_CHUNKS):
            dq_ref[c] = jnp.zeros((SEQ, 128), F32)

        def tile(hk, blk, rows_q, rows_k, bias, accs):
            nk = bias.shape[0]
            k2 = _stack_pair(qkv_ref.at[4 + hk][rows_k, :])
            v2 = _stack_pair(qkv_ref.at[6 + hk][rows_k, :])
            qcs, docs, sT, dpT = [], [], [], []
            for c in (2 * hk, 2 * hk + 1):
                qc, doc = qkv_ref.at[c][rows_q, :], do_ref.at[c][rows_q, :]
                s2, dp2 = _dot_nt(k2, qc), _dot_nt(v2, doc)
                sT += [s2[0:nk], s2[nk:2 * nk]]
                dpT += [dp2[0:nk], dp2[nk:2 * nk]]
                qcs.append(qc)
                docs.append(doc)
            heads = slice(G * hk, G * hk + G)
            lse_r = _lanes([lse_ref.at[blk][h:h + 1, :] for h in range(G * hk, G * hk + G)])
            delta_r = _lanes([delta_ref.at[blk][h:h + 1, :] for h in range(G * hk, G * hk + G)])
            yield
            pT = jnp.exp(_lanes(sT) + bias - lse_r)
            dsT = pT * (_lanes(dpT) - delta_r)
            dsb, pb = dsT.astype(MXU), pT.astype(MXU)
            accs[hk] = accs[hk] - jnp.exp(sinkrows[hk] - lse_r) * delta_r
            yield
            dk2 = dv2 = None
            for j, c in enumerate((2 * hk, 2 * hk + 1)):
                q0, q1 = slice(2 * j * BLK, (2 * j + 1) * BLK), slice((2 * j + 1) * BLK, (2 * j + 2) * BLK)
                ds2 = jnp.concatenate([dsb[:, q0], dsb[:, q1]], axis=0)
                p2 = jnp.concatenate([pb[:, q0], pb[:, q1]], axis=0)
                dq_ref.at[c][rows_q, :] = _dot_t0(ds2, k2)
                dk2 = _dot(ds2, qcs[j]) if dk2 is None else dk2 + _dot(ds2, qcs[j])
                dv2 = _dot(p2, docs[j]) if dv2 is None else dv2 + _dot(p2, docs[j])
            dq_ref.at[4 + hk][rows_k, :] += _pair_rows(dk2, nk)
            dq_ref.at[6 + hk][rows_k, :] += _pair_rows(dv2, nk)

        def run(tiles_of, accs):
            accs = list(accs)
            _interleave(tiles_of(accs))
            return tuple(accs)

        zero = jnp.zeros((1, G * BLK), F32)
        accs = run(lambda a: [tile(hk, 0, pl.ds(0, BLK), pl.ds(0, BLK), first, a) for hk in range(SWA_KV)],
                   (zero,) * SWA_KV)

        def loop(j, accs):
            def tiles_of(a):
                out = []
                for u in range(SWA_UNROLL):
                    i = 1 + j * SWA_UNROLL + u
                    r0 = pl.multiple_of(i * BLK, BLK)
                    rk = pl.multiple_of(i * BLK - BLK, BLK)
                    out += [tile(hk, i, pl.ds(r0, BLK), pl.ds(rk, 2 * BLK), band, a) for hk in range(SWA_KV)]
                return out
            return run(tiles_of, accs)
        accs = lax.fori_loop(0, (N_QBLK - 1) // SWA_UNROLL, loop, accs)
        for hk in range(SWA_KV):
            for j in range(G):
                tot = jnp.sum(accs[hk][:, j * BLK:(j + 1) * BLK], axis=1, keepdims=True)
                dsink_ref[G * hk + j:G * hk + j + 1, :] += jnp.broadcast_to(tot, (1, 128))

    stat = pl.BlockSpec((None, N_QBLK, 8, 128), lambda b: (b, 0, 0, 0))
    return pl.pallas_call(
        body, name="swa_bwd", grid=(B,),
        in_specs=[pl.BlockSpec(memory_space=pltpu.SMEM),
                  pl.BlockSpec((None, SWA_CHUNKS, SEQ, 128), lambda b: (b, 0, 0, 0)),
                  pl.BlockSpec((None, SWA_Q // 2, SEQ, 128), lambda b: (b, 0, 0, 0)), stat, stat],
        out_specs=[pl.BlockSpec((None, SWA_CHUNKS, SEQ, 128), lambda b: (b, 0, 0, 0)),
                   pl.BlockSpec((8, 128), lambda b: (0, 0))],
        out_shape=[jax.ShapeDtypeStruct((B, SWA_CHUNKS, SEQ, 128), F32), jax.ShapeDtypeStruct((8, 128), F32)],
        compiler_params=_cparams(("arbitrary",)),
    )(sinks, qkva, do, lse, delta)


DILATIONS = (1, 4, 16)
DIL_PAIRS_H = DIL_H // 2


def _stream_rows(d, r, i, n):
    if d == 1:
        return pl.ds(pl.multiple_of(i * BLK, BLK), n)
    return pl.ds(r + i * (BLK * d), n, stride=d)


def _spread_matrix():
    row = lax.broadcasted_iota(jnp.int32, (PICK_ROWS, 128), 0)
    lane = lax.broadcasted_iota(jnp.int32, (PICK_ROWS, 128), 1)
    return ((row < 6) & ((row % 2 == 1) == (lane >= HEAD))).astype(MXU)


def _lanes_to_tokens(v0, v1, spread):
    n = v0.shape[1]
    row = lax.broadcasted_iota(jnp.int32, (PICK_ROWS, n), 0)
    a = jnp.zeros((PICK_ROWS, n), F32)
    for i, (p0, p1) in enumerate(zip(_split3(v0), _split3(v1))):
        a = jnp.where(row == 2 * i, p0.astype(F32), a)
        a = jnp.where(row == 2 * i + 1, p1.astype(F32), a)
    return _dot_t0(a.astype(MXU), spread)


def _tokens_to_lanes(t):
    r = t.T
    return r[0:1, :], r[HEAD:HEAD + 1, :]


def _dil_schedule(body_first, body_next):
    for p, d in enumerate(DILATIONS):
        nblk = SEQ // d // BLK
        if d == 1:
            _interleave([body_first(p, d, 0)])
            def loop(j, c, p=p, d=d):
                _interleave([body_next(p, d, 0, 1 + 3 * j + u) for u in range(3)])
                return c
            lax.fori_loop(0, (nblk - 1) // 3, loop, 0)
        elif nblk > 1:
            def loop(r, c, p=p, d=d, nblk=nblk):
                _interleave([body_first(p, d, r)] + [body_next(p, d, r, i) for i in range(1, nblk)])
                return c
            lax.fori_loop(0, d, loop, 0)
        else:
            def loop(j, c, p=p, d=d):
                _interleave([body_first(p, d, 4 * j + u) for u in range(4)])
                return c
            lax.fori_loop(0, d // 4, loop, 0)


def _dil_fwd(qkvb):
    B = qkvb.shape[0]

    def body(qkv_ref, o_ref):
        band, first = _band_bias(BLK)
        spread = _spread_matrix()

        def block(p, d, rows_q, rows_k, bias):
            nk = bias.shape[0]
            sT = []
            for c in range(DIL_PAIRS_H):
                qc = qkv_ref.at[c][rows_q, :].astype(MXU)
                s2 = _dot_nt(_stack_pair(qkv_ref.at[DIL_PAIRS_H + c][rows_k, :]), qc)
                sT += [s2[0:nk], s2[nk:2 * nk]]
            yield
            sT = _lanes(sT) + bias
            m = jnp.max(sT, axis=0, keepdims=True)
            pT = jnp.exp(sT - m)
            l = jnp.sum(pT, axis=0, keepdims=True)
            pnT = (pT * (1.0 / l)).astype(MXU)
            lse = m + jnp.log(l)
            yield
            for c in range(DIL_PAIRS_H):
                q0, q1 = slice(2 * c * BLK, (2 * c + 1) * BLK), slice((2 * c + 1) * BLK, (2 * c + 2) * BLK)
                p2 = jnp.concatenate([pnT[:, q0], pnT[:, q1]], axis=0)
                o_ref.at[p, c][rows_q, :] = _dot_t0(p2, _stack_pair(qkv_ref.at[2 * DIL_PAIRS_H + c][rows_k, :]))
                o_ref.at[p, DIL_PAIRS_H + c][rows_q, :] = _lanes_to_tokens(lse[:, q0], lse[:, q1], spread)

        def body_first(p, d, r):
            rows = _stream_rows(d, r, 0, BLK)
            return block(p, d, rows, rows, first)

        def body_next(p, d, r, i):
            return block(p, d, _stream_rows(d, r, i, BLK), _stream_rows(d, r, i - 1, 2 * BLK), band)

        _dil_schedule(body_first, body_next)

    return pl.pallas_call(
        body, name="dil_fwd", grid=(B,),
        in_specs=[pl.BlockSpec((None, 6, SEQ, 128), lambda b: (b, 0, 0, 0))],
        out_specs=pl.BlockSpec((None, 3, 4, SEQ, 128), lambda b: (b, 0, 0, 0, 0)),
        out_shape=jax.ShapeDtypeStruct((B, 3, 4, SEQ, 128), F32),
        compiler_params=_cparams(("arbitrary",)),
    )(qkvb)


def _reduce_scatter_ops(send_refs, land_refs, send_sems, recv_sems):
    x, y, c = _my_pos()
    me = 4 * x + 2 * y + c
    n = len(send_refs)

    def to_peer(a, j):
        return pltpu.make_async_remote_copy(
            src_ref=send_refs[a].at[j], dst_ref=land_refs[a].at[me], send_sem=send_sems.at[a, j],
            recv_sem=recv_sems.at[a, me], device_id=_dev_coords(j), device_id_type=MESH)

    def from_peer(a, m):
        return pltpu.make_async_remote_copy(
            src_ref=land_refs[a].at[m], dst_ref=land_refs[a].at[m], send_sem=send_sems.at[a, m],
            recv_sem=recv_sems.at[a, m], device_id=_dev_coords(m), device_id_type=MESH)

    def start():
        for j in range(N_DEV):
            @pl.when(me != j)
            def _(j=j):
                for a in range(n):
                    to_peer(a, j).start()
        for a in range(n):
            land_refs[a][me] = jnp.zeros(land_refs[a].shape[1:], land_refs[a].dtype)

    def finish(own_refs, out_refs):
        for m in range(N_DEV):
            @pl.when(me != m)
            def _(m=m):
                for a in range(n):
                    from_peer(a, m).wait_recv()
        for j in range(N_DEV):
            @pl.when(me != j)
            def _(j=j):
                for a in range(n):
                    to_peer(a, j).wait_send()
        for a in range(n):
            def chunk(i, carry, a=a):
                rs = pl.ds(pl.multiple_of(i * REDUCE_ROWS, REDUCE_ROWS), REDUCE_ROWS)
                g = own_refs[a][rs, :]
                for m in range(N_DEV):
                    g = g + land_refs[a][m, rs, :].astype(F32)
                out_refs[a][rs, :] = g
                return carry
            lax.fori_loop(0, own_refs[a].shape[0] // REDUCE_ROWS, chunk, 0)

    return start, finish


def _dil_bwd(qkvb, dobb, sends, owns):
    B = qkvb.shape[0]
    n_rs = len(sends)

    def body(qkv_ref, dob_ref, *rest):
        send_refs, own_refs = rest[:n_rs], rest[n_rs:2 * n_rs]
        dq_ref = rest[2 * n_rs]
        out_refs = rest[2 * n_rs + 1:3 * n_rs + 1]
        land_refs = rest[3 * n_rs + 1:4 * n_rs + 1]
        send_sems, recv_sems = rest[4 * n_rs + 1:]
        rs_start, rs_finish = _reduce_scatter_ops(send_refs, land_refs, send_sems, recv_sems)
        pl.when(pl.program_id(0) == 0)(rs_start)

        band, first = _band_bias(BLK)
        dq_ref[...] = jnp.zeros_like(dq_ref)

        def block(p, d, rows_q, rows_k, bias):
            nk = bias.shape[0]
            lo = lax.broadcasted_iota(jnp.int32, (nk, 128), 1) < HEAD
            qcs, docs, k2s, sT, dpT, lse, delta = [], [], [], [], [], [], []
            for c in range(DIL_PAIRS_H):
                qc = qkv_ref.at[c][rows_q, :].astype(MXU)
                doc = dob_ref.at[c][rows_q, :].astype(MXU)
                k2 = _stack_pair(qkv_ref.at[DIL_PAIRS_H + c][rows_k, :])
                s2 = _dot_nt(k2, qc)
                dp2 = _dot_nt(_stack_pair(qkv_ref.at[2 * DIL_PAIRS_H + c][rows_k, :]), doc)
                sT += [s2[0:nk], s2[nk:2 * nk]]
                dpT += [dp2[0:nk], dp2[nk:2 * nk]]
                lse += _tokens_to_lanes(dob_ref.at[DIL_PAIRS_H + c][rows_q, :])
                delta += _tokens_to_lanes(dob_ref.at[2 * DIL_PAIRS_H + c][rows_q, :])
                qcs.append(qc)
                docs.append(doc)
                k2s.append(k2)
            yield
            pT = jnp.exp(_lanes(sT) + bias - _lanes(lse))
            dsT = pT * (_lanes(dpT) - _lanes(delta))
            dsb, pb = dsT.astype(MXU), pT.astype(MXU)
            yield
            for c in range(DIL_PAIRS_H):
                q0, q1 = slice(2 * c * BLK, (2 * c + 1) * BLK), slice((2 * c + 1) * BLK, (2 * c + 2) * BLK)
                ds2 = jnp.concatenate([dsb[:, q0], dsb[:, q1]], axis=0)
                p2 = jnp.concatenate([pb[:, q0], pb[:, q1]], axis=0)
                dq_ref.at[c][rows_q, :] += _dot_t0(ds2, k2s[c])
                dk2, dv2 = _dot(ds2, qcs[c]), _dot(p2, docs[c])
                dq_ref.at[DIL_PAIRS_H + c][rows_k, :] += jnp.where(lo, dk2[0:nk], dk2[nk:2 * nk])
                dq_ref.at[2 * DIL_PAIRS_H + c][rows_k, :] += jnp.where(lo, dv2[0:nk], dv2[nk:2 * nk])

        def body_first(p, d, r):
            rows = _stream_rows(d, r, 0, BLK)
            return block(p, d, rows, rows, first)

        def body_next(p, d, r, i):
            return block(p, d, _stream_rows(d, r, i, BLK), _stream_rows(d, r, i - 1, 2 * BLK), band)

        _dil_schedule(body_first, body_next)

        @pl.when(pl.program_id(0) == pl.num_programs(0) - 1)
        def _():
            rs_finish(own_refs, out_refs)

    spec = pl.BlockSpec((None, 6, SEQ, 128), lambda b: (b, 0, 0, 0))
    any_spec = pl.BlockSpec(memory_space=pl.ANY)
    vmem = pl.BlockSpec(memory_space=pltpu.VMEM)
    outs = pl.pallas_call(
        body, name="dil_bwd", grid=(B,),
        in_specs=[spec, spec] + [any_spec] * n_rs + [vmem] * n_rs, out_specs=[spec] + [vmem] * n_rs,
        out_shape=[jax.ShapeDtypeStruct((B, 6, SEQ, 128), F32)] + [jax.ShapeDtypeStruct(o.shape, F32) for o in owns],
        scratch_shapes=[pltpu.VMEM(s.shape, s.dtype) for s in sends]
        + [pltpu.SemaphoreType.DMA((n_rs, N_DEV)), pltpu.SemaphoreType.DMA((n_rs, N_DEV))],
        compiler_params=_cparams(("arbitrary",)),
    )(qkvb, dobb, *sends, *owns)
    return outs[0], outs[1:]


MEM_UNROLL = 4
MEM_PAIRS = MEM_H // 2


def _mem_attn_fwd(qc, mem, w_mem):
    B = qc.shape[0]

    def body(q_ref, mem_ref, w_ref, o_ref, lse_ref, mkv_ref, k2_ref, v2_ref):
        mkv = _dot(mem_ref[...].astype(MXU), w_ref[...])
        mkv_ref[...] = mkv.astype(MXU)
        for c in range(MEM_PAIRS):
            k2_ref[c] = _stack_pair(mkv[:, c * 128:(c + 1) * 128])
            v2_ref[c] = _stack_pair(mkv[:, W_C + c * 128:W_C + (c + 1) * 128])
        lse_ref[...] = jnp.zeros_like(lse_ref)

        def tile(blk):
            rows = pl.ds(pl.multiple_of(blk * BLK, BLK), BLK)
            sT = []
            for c in range(MEM_PAIRS):
                s2 = _dot_nt(k2_ref[c], q_ref.at[c][rows, :])
                sT += [s2[0:MEM_LEN], s2[MEM_LEN:2 * MEM_LEN]]
            yield
            pnT, lse = _softmax_cols(_lanes(sT))
            yield
            for c in range(MEM_PAIRS):
                p2 = jnp.concatenate([pnT[:, 2 * c * BLK:(2 * c + 1) * BLK],
                                      pnT[:, (2 * c + 1) * BLK:(2 * c + 2) * BLK]], axis=0)
                o_ref.at[c][rows, :] = _dot_t0(p2, v2_ref[c])
            for h in range(MEM_H):
                lse_ref.at[blk][h:h + 1, :] = lse[:, h * BLK:(h + 1) * BLK]

        def loop(j, carry):
            _interleave([tile(j * MEM_UNROLL + u) for u in range(MEM_UNROLL)])
            return carry
        lax.fori_loop(0, N_QBLK // MEM_UNROLL, loop, 0)

    return pl.pallas_call(
        body, name="mem_attn_fwd", grid=(B,),
        in_specs=[pl.BlockSpec((None, MEM_PAIRS, SEQ, 128), lambda b: (b, 0, 0, 0)),
                  pl.BlockSpec((None, MEM_LEN, D_MODEL), lambda b: (b, 0, 0)),
                  pl.BlockSpec((D_MODEL, 2 * W_C), lambda b: (0, 0))],
        out_specs=[pl.BlockSpec((None, MEM_PAIRS, SEQ, 128), lambda b: (b, 0, 0, 0)),
                   pl.BlockSpec((None, N_QBLK, 8, 128), lambda b: (b, 0, 0, 0)),
                   pl.BlockSpec((None, MEM_LEN, 2 * W_C), lambda b: (b, 0, 0))],
        out_shape=[jax.ShapeDtypeStruct((B, MEM_PAIRS, SEQ, 128), F32),
                   jax.ShapeDtypeStruct((B, N_QBLK, 8, 128), F32),
                   jax.ShapeDtypeStruct((B, MEM_LEN, 2 * W_C), MXU)],
        scratch_shapes=[pltpu.VMEM((MEM_PAIRS, 2 * MEM_LEN, 128), MXU), pltpu.VMEM((MEM_PAIRS, 2 * MEM_LEN, 128), MXU)],
        compiler_params=_cparams(("arbitrary",)),
    )(qc, mem, w_mem)


def _mem_attn_bwd(qc, mkv, do, lse, delta, mem):
    B = qc.shape[0]

    def body(q_ref, mkv_ref, do_ref, lse_ref, delta_ref, mem_ref, dq_ref, dw_ref, dmkv_ref, k2_ref, v2_ref):
        @pl.when(pl.program_id(0) == 0)
        def _():
            dw_ref[...] = jnp.zeros_like(dw_ref)
        dmkv_ref[...] = jnp.zeros_like(dmkv_ref)
        for c in range(MEM_PAIRS):
            k2_ref[c] = _stack_pair(mkv_ref[:, c * 128:(c + 1) * 128])
            v2_ref[c] = _stack_pair(mkv_ref[:, W_C + c * 128:W_C + (c + 1) * 128])

        def tile(blk):
            rows = pl.ds(pl.multiple_of(blk * BLK, BLK), BLK)
            qcs, docs, sT, dpT = [], [], [], []
            for c in range(MEM_PAIRS):
                qc_, doc = q_ref.at[c][rows, :], do_ref.at[c][rows, :]
                s2, dp2 = _dot_nt(k2_ref[c], qc_), _dot_nt(v2_ref[c], doc)
                sT += [s2[0:MEM_LEN], s2[MEM_LEN:2 * MEM_LEN]]
                dpT += [dp2[0:MEM_LEN], dp2[MEM_LEN:2 * MEM_LEN]]
                qcs.append(qc_)
                docs.append(doc)
            lse_r = _lanes([lse_ref.at[blk][h:h + 1, :] for h in range(MEM_H)])
            delta_r = _lanes([delta_ref.at[blk][h:h + 1, :] for h in range(MEM_H)])
            yield
            pT = jnp.exp(_lanes(sT) - lse_r)
            dsT = pT * (_lanes(dpT) - delta_r)
            dsb, pb = dsT.astype(MXU), pT.astype(MXU)
            yield
            for c in range(MEM_PAIRS):
                q0, q1 = slice(2 * c * BLK, (2 * c + 1) * BLK), slice((2 * c + 1) * BLK, (2 * c + 2) * BLK)
                ds2 = jnp.concatenate([dsb[:, q0], dsb[:, q1]], axis=0)
                p2 = jnp.concatenate([pb[:, q0], pb[:, q1]], axis=0)
                dq_ref.at[c][rows, :] = _dot_t0(ds2, k2_ref[c])
                dmkv_ref[:, c * 128:(c + 1) * 128] += _pair_rows(_dot(ds2, qcs[c]), MEM_LEN)
                dmkv_ref[:, W_C + c * 128:W_C + (c + 1) * 128] += _pair_rows(_dot(p2, docs[c]), MEM_LEN)

        def loop(j, carry):
            _interleave([tile(j * MEM_UNROLL + u) for u in range(MEM_UNROLL)])
            return carry
        lax.fori_loop(0, N_QBLK // MEM_UNROLL, loop, 0)
        dw_ref[...] += _dot_tn(mem_ref[...], dmkv_ref[...].astype(MXU))

    stat = pl.BlockSpec((None, N_QBLK, 8, 128), lambda b: (b, 0, 0, 0))
    pairs = pl.BlockSpec((None, MEM_PAIRS, SEQ, 128), lambda b: (b, 0, 0, 0))
    return pl.pallas_call(
        body, name="mem_attn_bwd", grid=(B,),
        in_specs=[pairs, pl.BlockSpec((None, MEM_LEN, 2 * W_C), lambda b: (b, 0, 0)), pairs, stat, stat,
                  pl.BlockSpec((None, MEM_LEN, D_MODEL), lambda b: (b, 0, 0))],
        out_specs=[pairs, pl.BlockSpec((D_MODEL, 2 * W_C), lambda b: (0, 0))],
        out_shape=[jax.ShapeDtypeStruct((B, MEM_PAIRS, SEQ, 128), F32),
                   jax.ShapeDtypeStruct((D_MODEL, 2 * W_C), F32)],
        scratch_shapes=[pltpu.VMEM((MEM_LEN, 2 * W_C), F32),
                        pltpu.VMEM((MEM_PAIRS, 2 * MEM_LEN, 128), MXU), pltpu.VMEM((MEM_PAIRS, 2 * MEM_LEN, 128), MXU)],
        compiler_params=_cparams(("arbitrary",)),
    )(qc, mkv, do, lse, delta, mem)


def _headsum(t, e):
    if MXU == F32:
        return _dot(t, e)
    hi = t.astype(MXU)
    lo = (t - hi.astype(F32)).astype(MXU)
    return _dot(hi, e) + _dot(lo, e)


def _heads_to_rows(t, e):
    return sum(_dot_nt(e, part) for part in _split3(t))


POST_ROWS = 256


def _post(o_a, olse_b, o_c, z, x2, tgt, g, gain, bias, w_out, hsum, hrows):
    T = x2.shape[0]
    tm = 256
    nt = SEQ // tm

    def body(oa_ref, ob_ref, oc_ref, z_ref, x_ref, t_ref, g_ref, gain_ref, bias_ref, w_ref, e_ref, er_ref,
             gx_ref, doa_ref, dela_ref, dobb_ref, doc_ref, delc_ref, dz_ref, dw_ref, small_ref, loss_ref):
        @pl.when(pl.program_id(0) == 0)
        def _():
            dw_ref[...] = jnp.zeros_like(dw_ref)
            small_ref[...] = jnp.zeros_like(small_ref)
            loss_ref[...] = jnp.zeros_like(loss_ref)

        gg = g_ref[...]
        gain_v = gain_ref[...]
        gain_s = gain_v * (1.0 / D_MODEL)
        bias_v = bias_ref[...]
        w = w_ref[...]

        def rms(o):
            rr = lax.rsqrt(jnp.mean(o * o, axis=1, keepdims=True) + RMS_EPS)
            return o * rr, rr

        def rows_of(rs):
            oa = _lanes([oa_ref.at[c][rs, :] for c in range(SWA_Q // 2)])
            (o1, l1), (o4, l4), (o16, l16) = [
                (_lanes([ob_ref.at[p, 0][rs, :], ob_ref.at[p, 1][rs, :]]),
                 _lanes([ob_ref.at[p, 2][rs, :], ob_ref.at[p, 3][rs, :]])) for p in range(3)]
            mx = jnp.maximum(jnp.maximum(l1, l4), l16)
            e1, e4, e16 = jnp.exp(l1 - mx), jnp.exp(l4 - mx), jnp.exp(l16 - mx)
            den = e1 + e4 + e16
            ob = (e1 * o1 + e4 * o4 + e16 * o16) / den
            lse_b = mx + jnp.log(den)
            oc = _lanes([oc_ref.at[c][rs, :] for c in range(MEM_PAIRS)])
            na, ra = rms(oa)
            nb, rb = rms(ob)
            nc, rc = rms(oc)
            n = jnp.concatenate([na, nb, nc], axis=1)
            zz = z_ref[rs, :]
            sig = 0.5 * jnp.tanh(0.5 * zz) + 0.5
            sz = zz * sig
            gs = gg * sz
            u = n * gs
            r = ALPHA * x_ref[rs, :] + _dot(u.astype(MXU), w)
            rc0 = r - jnp.mean(r, axis=1, keepdims=True)
            rstd = lax.rsqrt(jnp.mean(rc0 * rc0, axis=1, keepdims=True) + LN_EPS)
            xhat = rc0 * rstd
            err = xhat * gain_v + bias_v - t_ref[rs, :]
            dxh = err * gain_s
            dr = rstd * (dxh - jnp.mean(dxh, axis=1, keepdims=True)
                         - xhat * jnp.mean(dxh * xhat, axis=1, keepdims=True))
            gx_ref[rs, :] = ALPHA * dr
            drb = dr.astype(MXU)
            du = _dot_nt(drb, w)
            dun = du * n
            dz = dun * (gg * (sig + sz * (1.0 - sig)))
            dz_ref[rs, :] = dz.astype(MXU)
            dn = du * gs

            def branch(lo, hi, nbr, rr):
                dnb = dn[:, lo:hi]
                return rr * (dnb - nbr * jnp.mean(dnb * nbr, axis=1, keepdims=True))

            def to_kernel(dob, o, do_ref, delta_ref):
                wd = dob.shape[1]
                for c in range(wd // 128):
                    do_ref.at[c][rs, :] = dob[:, c * 128:(c + 1) * 128].astype(do_ref.dtype)
                dT = _heads_to_rows(dob * o, er_ref[:, 0:wd])
                for jb in range((rs.stop - rs.start) // BLK):
                    delta_ref[rs.start // BLK + jb] = dT[0:8, jb * BLK:(jb + 1) * BLK]

            to_kernel(branch(0, W_A, na, ra), oa, doa_ref, dela_ref)
            to_kernel(branch(W_A + W_B, D_MIX, nc, rc), oc, doc_ref, delc_ref)
            dob = branch(W_A, W_A + W_B, nb, rb)
            for j, t in enumerate((dob, lse_b, _headsum(dob * ob, e_ref[...]))):
                for c in range(W_B // 128):
                    dobb_ref.at[j * (W_B // 128) + c][rs, :] = t[:, c * 128:(c + 1) * 128]
            csum = lambda t: jnp.sum(t, axis=0, keepdims=True)
            return (u, drb, jnp.sum(err * err), csum(err * xhat), csum(err), csum(dun * sz), csum(dz))

        parts = [rows_of(slice(k * POST_ROWS, (k + 1) * POST_ROWS)) for k in range(tm // POST_ROWS)]
        tot = [sum(p[i] for p in parts) for i in range(2, 7)]
        dw_ref[...] += _dot_tn(jnp.concatenate([p[0] for p in parts], axis=0),
                               jnp.concatenate([p[1] for p in parts], axis=0))
        loss_ref[...] += 0.5 * tot[0] * (1.0 / D_MODEL)
        small_ref[0:1, :] += tot[1] * (1.0 / D_MODEL)
        small_ref[1:2, :] += tot[2] * (1.0 / D_MODEL)
        small_ref[2:3, :] += tot[3]
        small_ref[3:4, :] += tot[4]

    B = T // SEQ
    row = lambda w: pl.BlockSpec((tm, w), lambda i: (i, 0))
    full = lambda a, b: pl.BlockSpec((a, b), lambda i: (0, 0))
    chunked = lambda n: pl.BlockSpec((None, n, tm, 128), lambda i: (i // nt, 0, i % nt, 0))
    stat = pl.BlockSpec((None, tm // BLK, 8, 128), lambda i: (i // nt, i % nt, 0, 0))
    return pl.pallas_call(
        body, name="post_fwd_bwd", grid=(T // tm,),
        in_specs=[chunked(SWA_Q // 2), pl.BlockSpec((None, 3, 4, tm, 128), lambda i: (i // nt, 0, 0, i % nt, 0)),
                  chunked(MEM_PAIRS),
                  row(D_MIX), row(D_MODEL), row(D_MODEL),
                  full(1, D_MIX), full(1, D_MODEL), full(1, D_MODEL), full(D_MIX, D_MODEL), full(W_B, W_B),
                  full(PICK_ROWS, W_A)],
        out_specs=[row(D_MODEL), chunked(SWA_Q // 2), stat, chunked(6), chunked(MEM_PAIRS), stat, row(D_MIX),
                   full(D_MIX, D_MODEL), full(8, D_MODEL), full(8, 128)],
        out_shape=[jax.ShapeDtypeStruct((T, D_MODEL), F32),
                   jax.ShapeDtypeStruct((B, SWA_Q // 2, SEQ, 128), MXU),
                   jax.ShapeDtypeStruct((B, N_QBLK, 8, 128), F32),
                   jax.ShapeDtypeStruct((B, 6, SEQ, 128), F32),
                   jax.ShapeDtypeStruct((B, MEM_PAIRS, SEQ, 128), MXU),
                   jax.ShapeDtypeStruct((B, N_QBLK, 8, 128), F32),
                   jax.ShapeDtypeStruct((T, D_MIX), MXU),
                   jax.ShapeDtypeStruct((D_MIX, D_MODEL), F32),
                   jax.ShapeDtypeStruct((8, D_MODEL), F32),
                   jax.ShapeDtypeStruct((8, 128), F32)],
        compiler_params=_cparams(("arbitrary",)),
    )(o_a, olse_b, o_c, z, x2, tgt, g, gain, bias, w_out, hsum, hrows)


def _dh_build(dqkva, dqkvb, dqc, dz, tab):
    T = dz.shape[0]
    tm = 512
    nt = SEQ // tm
    HQ = D_IN - D_MIX

    def body(da_ref, db6_ref, dqc_ref, dz_ref, tab_ref, dh_ref, db_ref):
        @pl.when(pl.program_id(0) == 0)
        def _():
            db_ref[...] = jnp.zeros_like(db_ref)
        tab = tab_ref[...]
        lo = lax.broadcasted_iota(jnp.int32, (tm, 128), 1) < HEAD

        def kv_grad(c):
            g0, g1 = da_ref[c], da_ref[c + 1]
            return jnp.where(lo, g0 + pltpu.roll(g0, HEAD, 1), g1 + pltpu.roll(g1, HEAD, 1))

        parts = [_rope(_lanes([da_ref[c] for c in range(SWA_Q // 2)]), tab, -1) * Q_SCALE,
                 _rope(kv_grad(4), tab, -1),
                 kv_grad(6),
                 _rope(_lanes([db6_ref[0], db6_ref[1]]), tab, -1) * Q_SCALE,
                 _rope(_lanes([db6_ref[2], db6_ref[3]]), tab, -1),
                 _lanes([db6_ref[4], db6_ref[5]]),
                 _lanes([dqc_ref[c] for c in range(MEM_PAIRS)]) * Q_SCALE]
        dhq = jnp.concatenate(parts, axis=1)
        db_ref[0:1, :] += jnp.sum(dhq, axis=0, keepdims=True)
        dh_ref[:, 0:HQ] = dhq.astype(MXU)
        dh_ref[:, HQ:D_IN] = dz_ref[...]

    row = lambda w: pl.BlockSpec((tm, w), lambda i: (i, 0))
    chunked = lambda n: pl.BlockSpec((None, n, tm, 128), lambda i: (i // nt, 0, i % nt, 0))
    return pl.pallas_call(
        body, name="dh_build", grid=(T // tm,),
        in_specs=[chunked(SWA_CHUNKS), chunked(6), chunked(MEM_PAIRS), row(D_MIX),
                  pl.BlockSpec((tm, 384), lambda i: (i % nt, 0))],
        out_specs=[row(D_IN), pl.BlockSpec((8, HQ), lambda i: (0, 0))],
        out_shape=[jax.ShapeDtypeStruct((T, D_IN), MXU), jax.ShapeDtypeStruct((8, HQ), F32)],
        compiler_params=_cparams(("arbitrary",)),
    )(dqkva, dqkvb, dqc, dz, tab)


TAIL_TK = 512
TAIL_TN = D_IN // 2
TAIL_TM = 256
REDUCE_ROWS = 128


def _tail(xt, dh, gx1, w_in, small_g):
    T = xt.shape[1]
    kt = T // TAIL_TK
    ndw = (D_IN // TAIL_TN) * kt
    nsteps = ndw + T // TAIL_TM
    per_pass = TAIL_TN // COLS_PER_DEV
    pay = dh.dtype
    shapes = [(D_MODEL, COLS_PER_DEV), small_g.shape]
    n_arr = len(shapes)

    def body(xt_ref, dh1_ref, dh2_ref, gx_ref, w_hbm, sg_ref, dx_ref, gin_ref, gsm_ref,
             acc_ref, w_ref, stage_ref, ownin_ref, lin_ref, lsm_ref, send_sems, recv_sems, w_sem):
        s = pl.program_id(0)
        x, y, c = _my_pos()
        me = 4 * x + 2 * y + c
        lands = (lin_ref, lsm_ref)

        def src_of(a, j):
            return (stage_ref.at[j], sg_ref)[a]

        def to_peer(a, j):
            return pltpu.make_async_remote_copy(
                src_ref=src_of(a, j), dst_ref=lands[a].at[me], send_sem=send_sems.at[a, j],
                recv_sem=recv_sems.at[a, me], device_id=_dev_coords(j), device_id_type=MESH)

        def from_peer(a, m):
            return pltpu.make_async_remote_copy(
                src_ref=lands[a].at[m], dst_ref=lands[a].at[m], send_sem=send_sems.at[a, m],
                recv_sem=recv_sems.at[a, m], device_id=_dev_coords(m), device_id_type=MESH)

        w_copy = pltpu.make_async_copy(w_hbm, w_ref, w_sem)

        @pl.when(s == 0)
        def _():
            w_copy.start()
            for j in range(N_DEV):
                @pl.when(me != j)
                def _(j=j):
                    for a in range(1, n_arr):
                        to_peer(a, j).start()
            for a in range(n_arr - 1):
                lands[a][me] = jnp.zeros(shapes[a], lands[a].dtype)
            lsm_ref[me] = sg_ref[...]

        @pl.when(s < ndw)
        def _():
            k = s % kt

            @pl.when(k == 0)
            def _():
                acc_ref[...] = jnp.zeros_like(acc_ref)
            acc_ref[...] += _dot(xt_ref[...], dh1_ref[...])

            for p in range(D_IN // TAIL_TN):
                @pl.when(s == p * kt + kt - 1)
                def _(p=p):
                    for jj in range(per_pass):
                        j = p * per_pass + jj
                        blk = acc_ref[:, jj * COLS_PER_DEV:(jj + 1) * COLS_PER_DEV]
                        stage_ref[j] = blk.astype(pay)

                        @pl.when(me == j)
                        def _(blk=blk):
                            ownin_ref[...] = blk
                        pl.when(me != j)(to_peer(0, j).start)

        @pl.when(s >= ndw)
        def _():
            pl.when(s == ndw)(w_copy.wait)
            dx_ref[...] = _dot_nt(dh2_ref[...], w_ref[...]) + gx_ref[...]

        @pl.when(s == nsteps - 1)
        def _():
            for m in range(N_DEV):
                @pl.when(me != m)
                def _(m=m):
                    for a in range(n_arr):
                        from_peer(a, m).wait_recv()
            for j in range(N_DEV):
                @pl.when(me != j)
                def _(j=j):
                    for a in range(n_arr):
                        to_peer(a, j).wait_send()
            def chunk(i, carry):
                rs = pl.ds(pl.multiple_of(i * REDUCE_ROWS, REDUCE_ROWS), REDUCE_ROWS)
                g = ownin_ref[rs, :]
                for m in range(N_DEV):
                    g = g + lin_ref[m, rs, :].astype(F32)
                gin_ref[rs, :] = g
                return carry
            lax.fori_loop(0, D_MODEL // REDUCE_ROWS, chunk, 0)
            g = lsm_ref[0]
            for m in range(1, N_DEV):
                g = g + lsm_ref[m]
            gsm_ref[...] = g

    dw_step = lambda s: jnp.minimum(s, ndw - 1)
    dx_step = lambda s: jnp.maximum(s - ndw, 0)
    any_spec = pl.BlockSpec(memory_space=pl.ANY)
    vmem = pl.BlockSpec(memory_space=pltpu.VMEM)
    scratch = [pltpu.VMEM((D_MODEL, TAIL_TN), F32), pltpu.VMEM((D_MODEL, D_IN), w_in.dtype),
               pltpu.VMEM((N_DEV,) + shapes[0], pay), pltpu.VMEM(shapes[0], F32)]
    scratch += [pltpu.VMEM((N_DEV,) + shapes[a], pay) for a in range(n_arr - 1)]
    scratch += [pltpu.VMEM((N_DEV,) + shapes[-1], F32),
                pltpu.SemaphoreType.DMA((n_arr, N_DEV)), pltpu.SemaphoreType.DMA((n_arr, N_DEV)),
                pltpu.SemaphoreType.DMA]
    return pl.pallas_call(
        body, name="tail_dw_dx_reduce", grid=(nsteps,),
        in_specs=[pl.BlockSpec((D_MODEL, TAIL_TK), lambda s: (0, dw_step(s) % kt)),
                  pl.BlockSpec((TAIL_TK, TAIL_TN), lambda s: (dw_step(s) % kt, dw_step(s) // kt)),
                  pl.BlockSpec((TAIL_TM, D_IN), lambda s: (dx_step(s), 0)),
                  pl.BlockSpec((TAIL_TM, D_MODEL), lambda s: (dx_step(s), 0)),
                  any_spec, vmem],
        out_specs=[pl.BlockSpec((TAIL_TM, D_MODEL), lambda s: (dx_step(s), 0)), vmem, vmem],
        out_shape=[jax.ShapeDtypeStruct((T, D_MODEL), F32)] + [jax.ShapeDtypeStruct(sh, F32) for sh in shapes],
        scratch_shapes=scratch,
        compiler_params=_cparams(("arbitrary",)),
    )(xt, dh, dh, gx1, w_in, small_g)


def _adam_update(grads, params, carried):
    n = len(grads)

    def body(*refs):
        g_refs, p_refs, o_refs = refs[1:1 + n], refs[1 + n:1 + 4 * n], refs[2 + 4 * n:]
        for a in range(n):
            rows = g_refs[a].shape[0]
            cr = REDUCE_ROWS if rows % REDUCE_ROWS == 0 else rows
            w_ref, m_ref, v_ref = p_refs[3 * a:3 * a + 3]
            go_ref, d_ref, nm_ref, nv_ref = o_refs[4 * a:4 * a + 4]

            def chunk(i, carry, cr=cr, g_ref=g_refs[a], w_ref=w_ref, m_ref=m_ref, v_ref=v_ref,
                      go_ref=go_ref, d_ref=d_ref, nm_ref=nm_ref, nv_ref=nv_ref):
                rs = pl.ds(pl.multiple_of(i * cr, cr), cr)
                g = g_ref[rs, :]
                go_ref[rs, :] = g
                d_ref[rs, :], nm_ref[rs, :], nv_ref[rs, :] = _adamw(w_ref[rs, :], g, m_ref[rs, :], v_ref[rs, :])
                return carry
            lax.fori_loop(0, rows // cr, chunk, 0)

    vmem = pl.BlockSpec(memory_space=pltpu.VMEM)
    any_spec = pl.BlockSpec(memory_space=pl.ANY)
    flat = [p for grp in params for p in grp]
    outs = pl.pallas_call(
        body, name="adamw", in_specs=[any_spec] + [vmem] * (4 * n), out_specs=[any_spec] + [vmem] * (4 * n),
        out_shape=[jax.ShapeDtypeStruct(carried.shape, carried.dtype)]
        + [jax.ShapeDtypeStruct(g.shape, F32) for g in grads for _ in range(4)],
        input_output_aliases={0: 0},
        compiler_params=pltpu.CompilerParams(vmem_limit_bytes=VMEM_LIMIT),
    )(carried, *grads, *flat)
    return [outs[1 + 4 * a:5 + 4 * a] for a in range(n)], outs[0]


def _step(x, mem, w_in_s, w_mem_s, w_out_s, b_in, sinks, g, gain, bias, tgt):
    B = x.shape[0]
    T = B * SEQ
    x2 = x.reshape(T, D_MODEL)
    t2 = tgt.reshape(T, D_MODEL)
    tab = _rope_table()
    lane = jnp.arange(W_A)
    hsum = (lane[:W_B, None] // HEAD == lane[None, :W_B] // HEAD).astype(MXU)
    hrows = (jnp.arange(PICK_ROWS)[:, None] == lane[None, :] // HEAD).astype(MXU)
    me = 4 * lax.axis_index("x") + 2 * lax.axis_index("y") + lax.axis_index("c")

    (w_in_all,) = _gather_weights([w_in_s])
    qkva, qkvb, qc, z, w_in, xt, w_mem_all, w_out_all = _in_proj(x2, w_in_all, b_in, tab, [w_mem_s, w_out_s])
    w_mem = w_mem_all.reshape(D_MODEL, 2 * W_C)
    w_out = w_out_all.reshape(D_MIX, D_MODEL)

    o_a, lse_a = _swa_fwd(qkva, sinks)
    olse_b = _dil_fwd(qkvb)
    o_c, lse_c, mkv = _mem_attn_fwd(qc, mem, w_mem)

    gx1, do_a, delta_a, dobb, do_c, delta_c, dz, dw_out, small, loss = _post(
        o_a, olse_b, o_c, z, x2, t2, g, gain, bias, w_out, hsum, hrows)

    dqc, dw_mem = _mem_attn_bwd(qc, mkv, do_c, lse_c, delta_c, mem)
    blocks = [dw_mem.reshape(N_DEV, ROWS_PER_DEV, 2 * W_C), dw_out.reshape(N_DEV, ROWS_PER_DEV, D_MODEL)]
    sends = [b.astype(MXU) for b in blocks]
    owns = [lax.dynamic_index_in_dim(b, me, axis=0, keepdims=False) for b in blocks]
    dqkvb, (g_mem, g_out) = _dil_bwd(qkvb, dobb, sends, owns)
    dqkva, dsink = _swa_bwd(qkva, do_a, lse_a, delta_a, sinks)
    dh, dbq = _dh_build(dqkva, dqkvb, dqc, dz, tab)

    small_g = _pack_small(dict(b_in=jnp.concatenate([dbq[0], small[3]]), sinks=dsink[:, 0], g=small[2],
                               gain=small[0], bias=small[1], loss=loss[0, 0]))
    grad_x, g_in, g_small = _tail(xt, dh, gx1, w_in, small_g)
    return grad_x.reshape(B, SEQ, D_MODEL), g_in, g_mem, g_out, g_small


def _my_pos():
    return lax.axis_index("x"), lax.axis_index("y"), lax.axis_index("c")


def _gather_weights(shards):
    n_arr = len(shards)

    def body(*refs):
        ins, outs = refs[0:n_arr], refs[n_arr:2 * n_arr]
        send_sems, recv_sems, local_sems = refs[2 * n_arr:]
        x, y, c = _my_pos()
        me, sibling = (x, y, c), (x, y, 1 - c)
        chips = [(1 - x, y), (x, 1 - y), (1 - x, 1 - y)]

        def slot(a, pos):
            return outs[a].at[4 * pos[0] + 2 * pos[1] + pos[2]]

        def copy(a, k, block, to, src=None):
            return pltpu.make_async_remote_copy(
                src_ref=slot(a, block) if src is None else src, dst_ref=slot(a, block),
                send_sem=send_sems.at[a, k], recv_sem=recv_sems.at[a, k],
                device_id=to, device_id_type=MESH)

        mine = [pltpu.make_async_copy(ins[a], slot(a, me), local_sems.at[a]) for a in range(n_arr)]
        for cp in mine:
            cp.start()
        first = []
        for a in range(n_arr):
            first.append(copy(a, 0, me, sibling, src=ins[a]))
            first += [copy(a, 1 + j, me, (*chip, c), src=ins[a]) for j, chip in enumerate(chips)]
        for cp in first:
            cp.start()
        passed = []
        for j, chip in enumerate(chips):
            for a in range(n_arr):
                copy(a, 1 + j, (*chip, c), me).wait_recv()
                fwd = copy(a, 4 + j, (*chip, c), sibling)
                fwd.start()
                passed.append(fwd)
        for a in range(n_arr):
            copy(a, 0, sibling, me).wait_recv()
            for j, chip in enumerate(chips):
                copy(a, 4 + j, (*chip, 1 - c), me).wait_recv()
        for cp in first + passed:
            cp.wait_send()
        for cp in mine:
            cp.wait()

    any_spec = pl.BlockSpec(memory_space=pl.ANY)
    return pl.pallas_call(
        body, name="gather_weights",
        in_specs=[any_spec] * n_arr, out_specs=[any_spec] * n_arr,
        out_shape=[jax.ShapeDtypeStruct((N_DEV,) + s.shape, s.dtype) for s in shards],
        scratch_shapes=[pltpu.SemaphoreType.DMA((n_arr, 7)), pltpu.SemaphoreType.DMA((n_arr, 7)),
                        pltpu.SemaphoreType.DMA((n_arr,))],
    )(*shards)


def _adamw(w, g, m, v):
    m = ADAM_B1 * m + (1.0 - ADAM_B1) * g
    v = ADAM_B2 * v + (1.0 - ADAM_B2) * (g * g)
    m_hat = m / (1.0 - ADAM_B1 ** ADAM_STEP)
    v_hat = v / (1.0 - ADAM_B2 ** ADAM_STEP)
    delta = -ADAM_LR * (m_hat / (jnp.sqrt(v_hat) + ADAM_EPS) + ADAM_WD * w)
    return delta, m, v


_SMALL_SIZES = (("b_in", D_IN), ("g", D_MIX), ("gain", D_MODEL), ("bias", D_MODEL), ("sinks", SWA_Q), ("loss", 1))


def _pack_small(d):
    flat = jnp.concatenate([jnp.reshape(d[k], (-1,)).astype(F32) if k in d else jnp.zeros((n,), F32)
                            for k, n in _SMALL_SIZES])
    flat = jnp.pad(flat, (0, SMALL_ROWS * 128 - flat.shape[0]))
    return flat.reshape(SMALL_ROWS, 128)


def _unpack_small(p):
    flat = p.reshape(-1)
    out, off = {}, 0
    for k, n in _SMALL_SIZES:
        out[k] = flat[off:off + n].reshape(1, n)
        off += n
    return out


def kernel(x, mem, w_in, b_in, w_mem, attn_sinks, g_branch, w_out, ln_gain, ln_bias, loss_target, m_w_in, m_b_in, m_w_mem, m_attn_sinks, m_g_branch, m_w_out, m_ln_gain, m_ln_bias, v_w_in, v_b_in, v_w_mem, v_attn_sinks, v_g_branch, v_w_out, v_ln_gain, v_ln_bias):
    grad_x, g_in, g_mem, g_out, g_small = _step(
        x, mem, w_in[0].astype(MXU), w_mem[0].astype(MXU), w_out[0].astype(MXU), b_in, attn_sinks[0],
        g_branch, ln_gain, ln_bias, loss_target)

    small_w = _pack_small(dict(b_in=b_in, g=g_branch, gain=ln_gain, bias=ln_bias, sinks=attn_sinks))
    small_m = _pack_small(dict(b_in=m_b_in, g=m_g_branch, gain=m_ln_gain, bias=m_ln_bias, sinks=m_attn_sinks))
    small_v = _pack_small(dict(b_in=v_b_in, g=v_g_branch, gain=v_ln_gain, bias=v_ln_bias, sinks=v_attn_sinks))
    grads = [g_in, g_mem, g_out, g_small]
    params = [(w_in[0], m_w_in[0], v_w_in[0]), (w_mem[0], m_w_mem[0], v_w_mem[0]),
              (w_out[0], m_w_out[0], v_w_out[0]), (small_w, small_m, small_v)]
    res, grad_x = _adam_update(grads, params, grad_x)
    big = [[r[None] for r in res[a]] for a in range(3)]
    sm = [_unpack_small(r) for r in res[3]]

    def group(i):
        return (big[0][i], sm[i]["b_in"], big[1][i], sm[i]["sinks"], sm[i]["g"], big[2][i],
                sm[i]["gain"], sm[i]["bias"])

    loss = sm[0]["loss"].reshape(())
    return (loss, grad_x, *group(0), *group(1), *group(2), *group(3))
```

```python
import functools
import math

import jax
import jax.numpy as jnp
from jax import lax
from jax.experimental import pallas as pl
from jax.experimental.pallas import tpu as pltpu

F32 = jnp.float32
MXU = jnp.bfloat16

D_MODEL = 1024
SEQ = 2048
HEAD = 64
BLK = 128
SWA_Q, SWA_KV = 8, 2
DIL_H = 4
MEM_H = 4
MEM_LEN = 256
W_A, W_KVA, W_B, W_C = 512, 128, 256, 256
D_MIX = 1024
D_IN = 2816
N_DEV = 8
COLS_PER_DEV = D_IN // N_DEV
ROWS_PER_DEV = D_MODEL // N_DEV
ROPE_THETA = 10000.0
LN_EPS = 1e-5
RMS_EPS = 1e-6
ALPHA = 2.0 ** 0.25
Q_SCALE = HEAD ** -0.5
NEG = -1e30
SMALL_ROWS = 48
VMEM_LIMIT = 56 * 1024 * 1024

ADAM_LR = 0.001
ADAM_B1 = 0.9
ADAM_B2 = 0.999
ADAM_EPS = 1e-08
ADAM_WD = 0.01
ADAM_STEP = 10

MESH = pl.DeviceIdType.MESH


def _cparams(sem=None):
    return pltpu.CompilerParams(dimension_semantics=sem, vmem_limit_bytes=VMEM_LIMIT)


def _dot(a, b):
    return jnp.dot(a, b, preferred_element_type=F32)


def _dot_nt(a, b):
    return lax.dot_general(a, b, (((1,), (1,)), ((), ())), preferred_element_type=F32)


def _dot_t0(a, b):
    return lax.dot_general(a, b, (((0,), (0,)), ((), ())), preferred_element_type=F32)


def _dot_tn(a, b):
    return jnp.dot(a.T.astype(MXU), b, preferred_element_type=F32)


def _rope(t, tab, sign):
    cos, sa, sb = tab[:, 0:128], tab[:, 128:256], tab[:, 256:384]
    outs = []
    for c in range(t.shape[1] // 128):
        tc = t[:, c * 128:(c + 1) * 128]
        r = pltpu.roll(tc, 96, 1) * sa + pltpu.roll(tc, 32, 1) * sb
        outs.append(tc * cos + r if sign > 0 else tc * cos - r)
    return outs[0] if len(outs) == 1 else jnp.concatenate(outs, axis=1)


def _rope_table():
    pos = jnp.arange(SEQ, dtype=F32)
    inv = ROPE_THETA ** (-jnp.arange(0, HEAD, 2, dtype=F32) / HEAD)
    ang = pos[:, None] * inv[None, :]
    ang = jnp.concatenate([ang, ang], axis=-1)
    cos, sin = jnp.cos(ang), jnp.sin(ang)
    lane = jnp.arange(HEAD)[None, :]
    sa = jnp.where(lane < HEAD // 2, -sin, 0.0)
    sb = jnp.where(lane >= HEAD // 2, sin, 0.0)
    two = lambda t: jnp.concatenate([t, t], axis=-1)
    return jnp.concatenate([two(cos), two(sa), two(sb)], axis=-1).astype(F32)


def _dev_coords(j):
    return (j >> 2, (j >> 1) & 1, j & 1)


def _in_proj(x2, w_all, b_in, tab, late_shards):
    T = x2.shape[0]
    tm = 512
    n_late = len(late_shards)

    def body(x_ref, wall_ref, b_ref, tab_ref, *rest):
        late_in, rest = rest[:n_late], rest[n_late:]
        qkva_ref, qkvb_ref, qc_ref, z_ref, w_ref, xt_ref = rest[:6]
        late_out = rest[6:6 + n_late]
        send_sems, recv_sems, local_sems = rest[6 + n_late:]
        step, last = pl.program_id(0), pl.num_programs(0) - 1
        x, y, c = _my_pos()
        me = 4 * x + 2 * y + c

        def to_peer(a, j):
            return pltpu.make_async_remote_copy(
                src_ref=late_in[a], dst_ref=late_out[a].at[me], send_sem=send_sems.at[a, j],
                recv_sem=recv_sems.at[a, me], device_id=_dev_coords(j), device_id_type=MESH)

        def from_peer(a, m):
            return pltpu.make_async_remote_copy(
                src_ref=late_out[a].at[m], dst_ref=late_out[a].at[m], send_sem=send_sems.at[a, m],
                recv_sem=recv_sems.at[a, m], device_id=_dev_coords(m), device_id_type=MESH)

        def mine(a):
            return pltpu.make_async_copy(late_in[a], late_out[a].at[me], local_sems.at[a])

        @pl.when(step == 0)
        def _():
            for a in range(n_late):
                mine(a).start()
                for j in range(N_DEV):
                    pl.when(me != j)(to_peer(a, j).start)
            for j in range(N_DEV):
                w_ref[:, j * COLS_PER_DEV:(j + 1) * COLS_PER_DEV] = wall_ref[j]

        xb = x_ref[...].astype(MXU)
        xt_ref[...] = x_ref[...].T.astype(MXU)
        tab = tab_ref[...]

        def seg(c0, c1):
            return _dot(xb, w_ref[:, c0:c1]) + b_ref[:, c0:c1]

        qa = (_rope(seg(0, 512), tab, 1) * Q_SCALE).astype(MXU)
        for c in range(SWA_Q // 2):
            qkva_ref[c] = qa[:, c * 128:(c + 1) * 128]
        lo = lax.broadcasted_iota(jnp.int32, (tm, 128), 1) < HEAD
        for j, t in enumerate((_rope(seg(512, 640), tab, 1), seg(640, 768))):
            other = pltpu.roll(t, HEAD, 1)
            qkva_ref[4 + 2 * j] = jnp.where(lo, t, other).astype(MXU)
            qkva_ref[5 + 2 * j] = jnp.where(lo, other, t).astype(MXU)
        qkvb = (_rope(seg(768, 1024), tab, 1) * Q_SCALE, _rope(seg(1024, 1280), tab, 1), seg(1280, 1536))
        for j, t in enumerate(qkvb):
            for c in range(2):
                qkvb_ref[2 * j + c] = t[:, c * 128:(c + 1) * 128]
        qc = (seg(1536, 1792) * Q_SCALE).astype(MXU)
        for c in range(MEM_H // 2):
            qc_ref[c] = qc[:, c * 128:(c + 1) * 128]
        z_ref[...] = seg(1792, 2816)

        @pl.when(step == last)
        def _():
            for a in range(n_late):
                mine(a).wait()
                for m in range(N_DEV):
                    pl.when(me != m)(from_peer(a, m).wait_recv)
                for j in range(N_DEV):
                    pl.when(me != j)(to_peer(a, j).wait_send)

    nt = SEQ // tm
    any_spec = pl.BlockSpec(memory_space=pl.ANY)
    chunked = lambda n: pl.BlockSpec((None, n, tm, 128), lambda i: (i // nt, 0, i % nt, 0))
    return pl.pallas_call(
        body, name="in_proj_fwd",
        grid=(T // tm,),
        in_specs=[pl.BlockSpec((tm, D_MODEL), lambda i: (i, 0)),
                  pl.BlockSpec((N_DEV, D_MODEL, COLS_PER_DEV), lambda i: (0, 0, 0)),
                  pl.BlockSpec((1, D_IN), lambda i: (0, 0)),
                  pl.BlockSpec((tm, 384), lambda i: (i % nt, 0))] + [any_spec] * n_late,
        out_specs=[chunked(SWA_CHUNKS), chunked(6), chunked(MEM_H // 2),
                   pl.BlockSpec((tm, D_MIX), lambda i: (i, 0)),
                   pl.BlockSpec((D_MODEL, D_IN), lambda i: (0, 0)),
                   pl.BlockSpec((D_MODEL, tm), lambda i: (0, i))] + [any_spec] * n_late,
        out_shape=[jax.ShapeDtypeStruct((T // SEQ, SWA_CHUNKS, SEQ, 128), MXU),
                   jax.ShapeDtypeStruct((T // SEQ, 6, SEQ, 128), F32),
                   jax.ShapeDtypeStruct((T // SEQ, MEM_H // 2, SEQ, 128), MXU),
                   jax.ShapeDtypeStruct((T, D_MIX), F32),
                   jax.ShapeDtypeStruct((D_MODEL, D_IN), w_all.dtype),
                   jax.ShapeDtypeStruct((D_MODEL, T), MXU)]
        + [jax.ShapeDtypeStruct((N_DEV,) + s.shape, s.dtype) for s in late_shards],
        scratch_shapes=[pltpu.SemaphoreType.DMA((n_late, N_DEV)), pltpu.SemaphoreType.DMA((n_late, N_DEV)),
                        pltpu.SemaphoreType.DMA((n_late,))],
        compiler_params=_cparams(("arbitrary",)),
    )(x2, w_all, b_in, tab, *late_shards)


CHAIN = 4


def _band_bias(max_dist):
    kj = lax.broadcasted_iota(jnp.int32, (2 * BLK, BLK), 0)
    qi = lax.broadcasted_iota(jnp.int32, (2 * BLK, BLK), 1)
    dist = qi + BLK - kj
    band = jnp.where((dist >= 0) & (dist <= max_dist), 0.0, NEG).astype(F32)
    k1 = lax.broadcasted_iota(jnp.int32, (BLK, BLK), 0)
    q1 = lax.broadcasted_iota(jnp.int32, (BLK, BLK), 1)
    first = jnp.where((q1 - k1 >= 0) & (q1 - k1 <= max_dist), 0.0, NEG).astype(F32)
    return jnp.concatenate([band] * CHAIN, axis=1), jnp.concatenate([first] * CHAIN, axis=1)


def _lanes(parts):
    return jnp.concatenate(parts, axis=1)


PICK_ROWS = 16


def _stack_pair(t):
    lo = (lax.broadcasted_iota(jnp.int32, t.shape, 1) < HEAD).astype(F32)
    return jnp.concatenate([t * lo, t * (1.0 - lo)], axis=0).astype(MXU)


def _pair_rows(x, n):
    lo = lax.broadcasted_iota(jnp.int32, (n, 128), 1) < HEAD
    return jnp.where(lo, x[0:n], x[n:2 * n])


def _split3(t):
    if MXU == F32:
        return (t,)
    hi = t.astype(MXU)
    r = t - hi.astype(F32)
    mid = r.astype(MXU)
    return hi, mid, (r - mid.astype(F32)).astype(MXU)


def _interleave(tiles):
    tiles = list(tiles)
    while tiles:
        for t in list(tiles):
            try:
                next(t)
            except StopIteration:
                tiles.remove(t)


def _softmax_cols(sT, sinkrow=None):
    m = jnp.max(sT, axis=0, keepdims=True)
    if sinkrow is not None:
        m = jnp.maximum(m, sinkrow)
    pT = jnp.exp(sT - m)
    l = jnp.sum(pT, axis=0, keepdims=True)
    if sinkrow is not None:
        l = l + jnp.exp(sinkrow - m)
    return (pT * (1.0 / l)).astype(MXU), m + jnp.log(l)


SWA_CHUNKS = 8
SWA_UNROLL = 3
N_QBLK = SEQ // BLK


def _swa_fwd(qkva, sinks):
    B = qkva.shape[0]
    G = SWA_Q // SWA_KV

    def body(sink_ref, qkv_ref, o_ref, lse_ref):
        band, first = _band_bias(BLK - 1)
        sinkrows = [_lanes([jnp.full((1, BLK), sink_ref[G * hk + j], F32) for j in range(G)])
                    for hk in range(SWA_KV)]

        def tile(hk, blk, rows_q, rows_k, bias):
            nk = bias.shape[0]
            k2 = _stack_pair(qkv_ref.at[4 + hk][rows_k, :])
            sT = []
            for c in (2 * hk, 2 * hk + 1):
                s2 = _dot_nt(k2, qkv_ref.at[c][rows_q, :])
                sT += [s2[0:nk], s2[nk:2 * nk]]
            yield
            pnT, lse = _softmax_cols(_lanes(sT) + bias, sinkrows[hk])
            yield
            v2 = _stack_pair(qkv_ref.at[6 + hk][rows_k, :])
            for j, c in enumerate((2 * hk, 2 * hk + 1)):
                p2 = jnp.concatenate([pnT[:, 2 * j * BLK:(2 * j + 1) * BLK],
                                      pnT[:, (2 * j + 1) * BLK:(2 * j + 2) * BLK]], axis=0)
                o_ref.at[c][rows_q, :] = _dot_t0(p2, v2)
            for j in range(G):
                lse_ref.at[blk][G * hk + j:G * hk + j + 1, :] = lse[:, j * BLK:(j + 1) * BLK]

        def tiles_at(i):
            r0 = pl.multiple_of(i * BLK, BLK)
            rk = pl.multiple_of(i * BLK - BLK, BLK)
            return [tile(hk, i, pl.ds(r0, BLK), pl.ds(rk, 2 * BLK), band) for hk in range(SWA_KV)]

        _interleave([tile(hk, 0, pl.ds(0, BLK), pl.ds(0, BLK), first) for hk in range(SWA_KV)])

        def loop(j, carry):
            _interleave([t for u in range(SWA_UNROLL) for t in tiles_at(1 + j * SWA_UNROLL + u)])
            return carry
        lax.fori_loop(0, (N_QBLK - 1) // SWA_UNROLL, loop, 0)

    return pl.pallas_call(
        body, name="swa_fwd", grid=(B,),
        in_specs=[pl.BlockSpec(memory_space=pltpu.SMEM),
                  pl.BlockSpec((None, SWA_CHUNKS, SEQ, 128), lambda b: (b, 0, 0, 0))],
        out_specs=[pl.BlockSpec((None, SWA_Q // 2, SEQ, 128), lambda b: (b, 0, 0, 0)),
                   pl.BlockSpec((None, N_QBLK, 8, 128), lambda b: (b, 0, 0, 0))],
        out_shape=[jax.ShapeDtypeStruct((B, SWA_Q // 2, SEQ, 128), F32),
                   jax.ShapeDtypeStruct((B, N_QBLK, 8, 128), F32)],
        compiler_params=_cparams(("arbitrary",)),
    )(sinks, qkva)


def _swa_bwd(qkva, do, lse, delta, sinks):
    B = qkva.shape[0]
    G = SWA_Q // SWA_KV

    def body(sink_ref, qkv_ref, do_ref, lse_ref, delta_ref, dq_ref, dsink_ref):
        band, first = _band_bias(BLK - 1)
        sinkrows = [_lanes([jnp.full((1, BLK), sink_ref[G * hk + j], F32) for j in range(G)])
                    for hk in range(SWA_KV)]

        @pl.when(pl.program_id(0) == 0)
        def _():
            dsink_ref[...] = jnp.zeros_like(dsink_ref)
        for c in range(4, SWA_CHUNKS):
            dq_ref[c] = jnp.zeros((SEQ, 128), F32)

        def tile(hk, blk, rows_q, rows_k, bias, accs):
            nk = bias.shape[0]
            k2 = _stack_pair(qkv_ref.at[4 + hk][rows_k, :])
            v2 = _stack_pair(qkv_ref.at[6 + hk][rows_k, :])
            qcs, docs, sT, dpT = [], [], [], []
            for c in (2 * hk, 2 * hk + 1):
                qc, doc = qkv_ref.at[c][rows_q, :], do_ref.at[c][rows_q, :]
                s2, dp2 = _dot_nt(k2, qc), _dot_nt(v2, doc)
                sT += [s2[0:nk], s2[nk:2 * nk]]
                dpT += [dp2[0:nk], dp2[nk:2 * nk]]
                qcs.append(qc)
                docs.append(doc)
            heads = slice(G * hk, G * hk + G)
            lse_r = _lanes([lse_ref.at[blk][h:h + 1, :] for h in range(G * hk, G * hk + G)])
            delta_r = _lanes([delta_ref.at[blk][h:h + 1, :] for h in range(G * hk, G * hk + G)])
            yield
            pT = jnp.exp(_lanes(sT) + bias - lse_r)
            dsT = pT * (_lanes(dpT) - delta_r)
            dsb, pb = dsT.astype(MXU), pT.astype(MXU)
            accs[hk] = accs[hk] - jnp.exp(sinkrows[hk] - lse_r) * delta_r
            yield
            dk2 = dv2 = None
            for j, c in enumerate((2 * hk, 2 * hk + 1)):
                q0, q1 = slice(2 * j * BLK, (2 * j + 1) * BLK), slice((2 * j + 1) * BLK, (2 * j + 2) * BLK)
                ds2 = jnp.concatenate([dsb[:, q0], dsb[:, q1]], axis=0)
                p2 = jnp.concatenate([pb[:, q0], pb[:, q1]], axis=0)
                dq_ref.at[c][rows_q, :] = _dot_t0(ds2, k2)
                dk2 = _dot(ds2, qcs[j]) if dk2 is None else dk2 + _dot(ds2, qcs[j])
                dv2 = _dot(p2, docs[j]) if dv2 is None else dv2 + _dot(p2, docs[j])
            dq_ref.at[4 + hk][rows_k, :] += _pair_rows(dk2, nk)
            dq_ref.at[6 + hk][rows_k, :] += _pair_rows(dv2, nk)

        def run(tiles_of, accs):
            accs = list(accs)
            _interleave(tiles_of(accs))
            return tuple(accs)

        zero = jnp.zeros((1, G * BLK), F32)
        accs = run(lambda a: [tile(hk, 0, pl.ds(0, BLK), pl.ds(0, BLK), first, a) for hk in range(SWA_KV)],
                   (zero,) * SWA_KV)

        def loop(j, accs):
            def tiles_of(a):
                out = []
                for u in range(SWA_UNROLL):
                    i = 1 + j * SWA_UNROLL + u
                    r0 = pl.multiple_of(i * BLK, BLK)
                    rk = pl.multiple_of(i * BLK - BLK, BLK)
                    out += [tile(hk, i, pl.ds(r0, BLK), pl.ds(rk, 2 * BLK), band, a) for hk in range(SWA_KV)]
                return out
            return run(tiles_of, accs)
        accs = lax.fori_loop(0, (N_QBLK - 1) // SWA_UNROLL, loop, accs)
        for hk in range(SWA_KV):
            for j in range(G):
                tot = jnp.sum(accs[hk][:, j * BLK:(j + 1) * BLK], axis=1, keepdims=True)
                dsink_ref[G * hk + j:G * hk + j + 1, :] += jnp.broadcast_to(tot, (1, 128))

    stat = pl.BlockSpec((None, N_QBLK, 8, 128), lambda b: (b, 0, 0, 0))
    return pl.pallas_call(
        body, name="swa_bwd", grid=(B,),
        in_specs=[pl.BlockSpec(memory_space=pltpu.SMEM),
                  pl.BlockSpec((None, SWA_CHUNKS, SEQ, 128), lambda b: (b, 0, 0, 0)),
                  pl.BlockSpec((None, SWA_Q // 2, SEQ, 128), lambda b: (b, 0, 0, 0)), stat, stat],
        out_specs=[pl.BlockSpec((None, SWA_CHUNKS, SEQ, 128), lambda b: (b, 0, 0, 0)),
                   pl.BlockSpec((8, 128), lambda b: (0, 0))],
        out_shape=[jax.ShapeDtypeStruct((B, SWA_CHUNKS, SEQ, 128), F32), jax.ShapeDtypeStruct((8, 128), F32)],
        compiler_params=_cparams(("arbitrary",)),
    )(sinks, qkva, do, lse, delta)


DILATIONS = (1, 4, 16)
DIL_PAIRS_H = DIL_H // 2


def _stream_rows(d, r, i, n):
    if d == 1:
        return pl.ds(pl.multiple_of(i * BLK, BLK), n)
    return pl.ds(r + i * (BLK * d), n, stride=d)


def _spread_matrix():
    row = lax.broadcasted_iota(jnp.int32, (PICK_ROWS, 128), 0)
    lane = lax.broadcasted_iota(jnp.int32, (PICK_ROWS, 128), 1)
    return ((row < 6) & ((row % 2 == 1) == (lane >= HEAD))).astype(MXU)


def _lanes_to_tokens(v0, v1, spread):
    n = v0.shape[1]
    row = lax.broadcasted_iota(jnp.int32, (PICK_ROWS, n), 0)
    a = jnp.zeros((PICK_ROWS, n), F32)
    for i, (p0, p1) in enumerate(zip(_split3(v0), _split3(v1))):
        a = jnp.where(row == 2 * i, p0.astype(F32), a)
        a = jnp.where(row == 2 * i + 1, p1.astype(F32), a)
    return _dot_t0(a.astype(MXU), spread)


def _tokens_to_lanes(t):
    r = t.T
    return r[0:1, :], r[HEAD:HEAD + 1, :]


def _dil_schedule(body_first, body_next):
    for p, d in enumerate(DILATIONS):
        nblk = SEQ // d // BLK
        if d == 1:
            _interleave([body_first(p, d, 0)])
            def loop(j, c, p=p, d=d):
                _interleave([body_next(p, d, 0, 1 + 3 * j + u) for u in range(3)])
                return c
            lax.fori_loop(0, (nblk - 1) // 3, loop, 0)
        elif nblk > 1:
            def loop(r, c, p=p, d=d, nblk=nblk):
                _interleave([body_first(p, d, r)] + [body_next(p, d, r, i) for i in range(1, nblk)])
                return c
            lax.fori_loop(0, d, loop, 0)
        else:
            def loop(j, c, p=p, d=d):
                _interleave([body_first(p, d, 4 * j + u) for u in range(4)])
                return c
            lax.fori_loop(0, d // 4, loop, 0)


def _dil_fwd(qkvb):
    B = qkvb.shape[0]

    def body(qkv_ref, o_ref):
        band, first = _band_bias(BLK)
        spread = _spread_matrix()

        def block(p, d, rows_q, rows_k, bias):
            nk = bias.shape[0]
            sT = []
            for c in range(DIL_PAIRS_H):
                qc = qkv_ref.at[c][rows_q, :].astype(MXU)
                s2 = _dot_nt(_stack_pair(qkv_ref.at[DIL_PAIRS_H + c][rows_k, :]), qc)
                sT += [s2[0:nk], s2[nk:2 * nk]]
            yield
            sT = _lanes(sT) + bias
            m = jnp.max(sT, axis=0, keepdims=True)
            pT = jnp.exp(sT - m)
            l = jnp.sum(pT, axis=0, keepdims=True)
            pnT = (pT * (1.0 / l)).astype(MXU)
            lse = m + jnp.log(l)
            yield
            for c in range(DIL_PAIRS_H):
                q0, q1 = slice(2 * c * BLK, (2 * c + 1) * BLK), slice((2 * c + 1) * BLK, (2 * c + 2) * BLK)
                p2 = jnp.concatenate([pnT[:, q0], pnT[:, q1]], axis=0)
                o_ref.at[p, c][rows_q, :] = _dot_t0(p2, _stack_pair(qkv_ref.at[2 * DIL_PAIRS_H + c][rows_k, :]))
                o_ref.at[p, DIL_PAIRS_H + c][rows_q, :] = _lanes_to_tokens(lse[:, q0], lse[:, q1], spread)

        def body_first(p, d, r):
            rows = _stream_rows(d, r, 0, BLK)
            return block(p, d, rows, rows, first)

        def body_next(p, d, r, i):
            return block(p, d, _stream_rows(d, r, i, BLK), _stream_rows(d, r, i - 1, 2 * BLK), band)

        _dil_schedule(body_first, body_next)

    return pl.pallas_call(
        body, name="dil_fwd", grid=(B,),
        in_specs=[pl.BlockSpec((None, 6, SEQ, 128), lambda b: (b, 0, 0, 0))],
        out_specs=pl.BlockSpec((None, 3, 4, SEQ, 128), lambda b: (b, 0, 0, 0, 0)),
        out_shape=jax.ShapeDtypeStruct((B, 3, 4, SEQ, 128), F32),
        compiler_params=_cparams(("arbitrary",)),
    )(qkvb)


def _reduce_scatter_ops(send_refs, land_refs, send_sems, recv_sems):
    x, y, c = _my_pos()
    me = 4 * x + 2 * y + c
    n = len(send_refs)

    def to_peer(a, j):
        return pltpu.make_async_remote_copy(
            src_ref=send_refs[a].at[j], dst_ref=land_refs[a].at[me], send_sem=send_sems.at[a, j],
            recv_sem=recv_sems.at[a, me], device_id=_dev_coords(j), device_id_type=MESH)

    def from_peer(a, m):
        return pltpu.make_async_remote_copy(
            src_ref=land_refs[a].at[m], dst_ref=land_refs[a].at[m], send_sem=send_sems.at[a, m],
            recv_sem=recv_sems.at[a, m], device_id=_dev_coords(m), device_id_type=MESH)

    def start():
        for j in range(N_DEV):
            @pl.when(me != j)
            def _(j=j):
                for a in range(n):
                    to_peer(a, j).start()
        for a in range(n):
            land_refs[a][me] = jnp.zeros(land_refs[a].shape[1:], land_refs[a].dtype)

    def finish(own_refs, out_refs):
        for m in range(N_DEV):
            @pl.when(me != m)
            def _(m=m):
                for a in range(n):
                    from_peer(a, m).wait_recv()
        for j in range(N_DEV):
            @pl.when(me != j)
            def _(j=j):
                for a in range(n):
                    to_peer(a, j).wait_send()
        for a in range(n):
            def chunk(i, carry, a=a):
                rs = pl.ds(pl.multiple_of(i * REDUCE_ROWS, REDUCE_ROWS), REDUCE_ROWS)
                g = own_refs[a][rs, :]
                for m in range(N_DEV):
                    g = g + land_refs[a][m, rs, :].astype(F32)
                out_refs[a][rs, :] = g
                return carry
            lax.fori_loop(0, own_refs[a].shape[0] // REDUCE_ROWS, chunk, 0)

    return start, finish


def _dil_bwd(qkvb, dobb, sends, owns):
    B = qkvb.shape[0]
    n_rs = len(sends)

    def body(qkv_ref, dob_ref, *rest):
        send_refs, own_refs = rest[:n_rs], rest[n_rs:2 * n_rs]
        dq_ref = rest[2 * n_rs]
        out_refs = rest[2 * n_rs + 1:3 * n_rs + 1]
        land_refs = rest[3 * n_rs + 1:4 * n_rs + 1]
        send_sems, recv_sems = rest[4 * n_rs + 1:]
        rs_start, rs_finish = _reduce_scatter_ops(send_refs, land_refs, send_sems, recv_sems)
        pl.when(pl.program_id(0) == 0)(rs_start)

        band, first = _band_bias(BLK)
        dq_ref[...] = jnp.zeros_like(dq_ref)

        def block(p, d, rows_q, rows_k, bias):
            nk = bias.shape[0]
            lo = lax.broadcasted_iota(jnp.int32, (nk, 128), 1) < HEAD
            qcs, docs, k2s, sT, dpT, lse, delta = [], [], [], [], [], [], []
            for c in range(DIL_PAIRS_H):
                qc = qkv_ref.at[c][rows_q, :].astype(MXU)
                doc = dob_ref.at[c][rows_q, :].astype(MXU)
                k2 = _stack_pair(qkv_ref.at[DIL_PAIRS_H + c][rows_k, :])
                s2 = _dot_nt(k2, qc)
                dp2 = _dot_nt(_stack_pair(qkv_ref.at[2 * DIL_PAIRS_H + c][rows_k, :]), doc)
                sT += [s2[0:nk], s2[nk:2 * nk]]
                dpT += [dp2[0:nk], dp2[nk:2 * nk]]
                lse += _tokens_to_lanes(dob_ref.at[DIL_PAIRS_H + c][rows_q, :])
                delta += _tokens_to_lanes(dob_ref.at[2 * DIL_PAIRS_H + c][rows_q, :])
                qcs.append(qc)
                docs.append(doc)
                k2s.append(k2)
            yield
            pT = jnp.exp(_lanes(sT) + bias - _lanes(lse))
            dsT = pT * (_lanes(dpT) - _lanes(delta))
            dsb, pb = dsT.astype(MXU), pT.astype(MXU)
            yield
            for c in range(DIL_PAIRS_H):
                q0, q1 = slice(2 * c * BLK, (2 * c + 1) * BLK), slice((2 * c + 1) * BLK, (2 * c + 2) * BLK)
                ds2 = jnp.concatenate([dsb[:, q0], dsb[:, q1]], axis=0)
                p2 = jnp.concatenate([pb[:, q0], pb[:, q1]], axis=0)
                dq_ref.at[c][rows_q, :] += _dot_t0(ds2, k2s[c])
                dk2, dv2 = _dot(ds2, qcs[c]), _dot(p2, docs[c])
                dq_ref.at[DIL_PAIRS_H + c][rows_k, :] += jnp.where(lo, dk2[0:nk], dk2[nk:2 * nk])
                dq_ref.at[2 * DIL_PAIRS_H + c][rows_k, :] += jnp.where(lo, dv2[0:nk], dv2[nk:2 * nk])

        def body_first(p, d, r):
            rows = _stream_rows(d, r, 0, BLK)
            return block(p, d, rows, rows, first)

        def body_next(p, d, r, i):
            return block(p, d, _stream_rows(d, r, i, BLK), _stream_rows(d, r, i - 1, 2 * BLK), band)

        _dil_schedule(body_first, body_next)

        @pl.when(pl.program_id(0) == pl.num_programs(0) - 1)
        def _():
            rs_finish(own_refs, out_refs)

    spec = pl.BlockSpec((None, 6, SEQ, 128), lambda b: (b, 0, 0, 0))
    any_spec = pl.BlockSpec(memory_space=pl.ANY)
    vmem = pl.BlockSpec(memory_space=pltpu.VMEM)
    outs = pl.pallas_call(
        body, name="dil_bwd", grid=(B,),
        in_specs=[spec, spec] + [any_spec] * n_rs + [vmem] * n_rs, out_specs=[spec] + [vmem] * n_rs,
        out_shape=[jax.ShapeDtypeStruct((B, 6, SEQ, 128), F32)] + [jax.ShapeDtypeStruct(o.shape, F32) for o in owns],
        scratch_shapes=[pltpu.VMEM(s.shape, s.dtype) for s in sends]
        + [pltpu.SemaphoreType.DMA((n_rs, N_DEV)), pltpu.SemaphoreType.DMA((n_rs, N_DEV))],
        compiler_params=_cparams(("arbitrary",)),
    )(qkvb, dobb, *sends, *owns)
    return outs[0], outs[1:]


MEM_UNROLL = 4
MEM_PAIRS = MEM_H // 2


def _mem_attn_fwd(qc, mem, w_mem):
    B = qc.shape[0]

    def body(q_ref, mem_ref, w_ref, o_ref, lse_ref, mkv_ref, k2_ref, v2_ref):
        mkv = _dot(mem_ref[...].astype(MXU), w_ref[...])
        mkv_ref[...] = mkv.astype(MXU)
        for c in range(MEM_PAIRS):
            k2_ref[c] = _stack_pair(mkv[:, c * 128:(c + 1) * 128])
            v2_ref[c] = _stack_pair(mkv[:, W_C + c * 128:W_C + (c + 1) * 128])
        lse_ref[...] = jnp.zeros_like(lse_ref)

        def tile(blk):
            rows = pl.ds(pl.multiple_of(blk * BLK, BLK), BLK)
            sT = []
            for c in range(MEM_PAIRS):
                s2 = _dot_nt(k2_ref[c], q_ref.at[c][rows, :])
                sT += [s2[0:MEM_LEN], s2[MEM_LEN:2 * MEM_LEN]]
            yield
            pnT, lse = _softmax_cols(_lanes(sT))
            yield
            for c in range(MEM_PAIRS):
                p2 = jnp.concatenate([pnT[:, 2 * c * BLK:(2 * c + 1) * BLK],
                                      pnT[:, (2 * c + 1) * BLK:(2 * c + 2) * BLK]], axis=0)
                o_ref.at[c][rows, :] = _dot_t0(p2, v2_ref[c])
            for h in range(MEM_H):
                lse_ref.at[blk][h:h + 1, :] = lse[:, h * BLK:(h + 1) * BLK]

        def loop(j, carry):
            _interleave([tile(j * MEM_UNROLL + u) for u in range(MEM_UNROLL)])
            return carry
        lax.fori_loop(0, N_QBLK // MEM_UNROLL, loop, 0)

    return pl.pallas_call(
        body, name="mem_attn_fwd", grid=(B,),
        in_specs=[pl.BlockSpec((None, MEM_PAIRS, SEQ, 128), lambda b: (b, 0, 0, 0)),
                  pl.BlockSpec((None, MEM_LEN, D_MODEL), lambda b: (b, 0, 0)),
                  pl.BlockSpec((D_MODEL, 2 * W_C), lambda b: (0, 0))],
        out_specs=[pl.BlockSpec((None, MEM_PAIRS, SEQ, 128), lambda b: (b, 0, 0, 0)),
                   pl.BlockSpec((None, N_QBLK, 8, 128), lambda b: (b, 0, 0, 0)),
                   pl.BlockSpec((None, MEM_LEN, 2 * W_C), lambda b: (b, 0, 0))],
        out_shape=[jax.ShapeDtypeStruct((B, MEM_PAIRS, SEQ, 128), F32),
                   jax.ShapeDtypeStruct((B, N_QBLK, 8, 128), F32),
                   jax.ShapeDtypeStruct((B, MEM_LEN, 2 * W_C), MXU)],
        scratch_shapes=[pltpu.VMEM((MEM_PAIRS, 2 * MEM_LEN, 128), MXU), pltpu.VMEM((MEM_PAIRS, 2 * MEM_LEN, 128), MXU)],
        compiler_params=_cparams(("arbitrary",)),
    )(qc, mem, w_mem)


def _mem_attn_bwd(qc, mkv, do, lse, delta, mem):
    B = qc.shape[0]

    def body(q_ref, mkv_ref, do_ref, lse_ref, delta_ref, mem_ref, dq_ref, dw_ref, dmkv_ref, k2_ref, v2_ref):
        @pl.when(pl.program_id(0) == 0)
        def _():
            dw_ref[...] = jnp.zeros_like(dw_ref)
        dmkv_ref[...] = jnp.zeros_like(dmkv_ref)
        for c in range(MEM_PAIRS):
            k2_ref[c] = _stack_pair(mkv_ref[:, c * 128:(c + 1) * 128])
            v2_ref[c] = _stack_pair(mkv_ref[:, W_C + c * 128:W_C + (c + 1) * 128])

        def tile(blk):
            rows = pl.ds(pl.multiple_of(blk * BLK, BLK), BLK)
            qcs, docs, sT, dpT = [], [], [], []
            for c in range(MEM_PAIRS):
                qc_, doc = q_ref.at[c][rows, :], do_ref.at[c][rows, :]
                s2, dp2 = _dot_nt(k2_ref[c], qc_), _dot_nt(v2_ref[c], doc)
                sT += [s2[0:MEM_LEN], s2[MEM_LEN:2 * MEM_LEN]]
                dpT += [dp2[0:MEM_LEN], dp2[MEM_LEN:2 * MEM_LEN]]
                qcs.append(qc_)
                docs.append(doc)
            lse_r = _lanes([lse_ref.at[blk][h:h + 1, :] for h in range(MEM_H)])
            delta_r = _lanes([delta_ref.at[blk][h:h + 1, :] for h in range(MEM_H)])
            yield
            pT = jnp.exp(_lanes(sT) - lse_r)
            dsT = pT * (_lanes(dpT) - delta_r)
            dsb, pb = dsT.astype(MXU), pT.astype(MXU)
            yield
            for c in range(MEM_PAIRS):
                q0, q1 = slice(2 * c * BLK, (2 * c + 1) * BLK), slice((2 * c + 1) * BLK, (2 * c + 2) * BLK)
                ds2 = jnp.concatenate([dsb[:, q0], dsb[:, q1]], axis=0)
                p2 = jnp.concatenate([pb[:, q0], pb[:, q1]], axis=0)
                dq_ref.at[c][rows, :] = _dot_t0(ds2, k2_ref[c])
                dmkv_ref[:, c * 128:(c + 1) * 128] += _pair_rows(_dot(ds2, qcs[c]), MEM_LEN)
                dmkv_ref[:, W_C + c * 128:W_C + (c + 1) * 128] += _pair_rows(_dot(p2, docs[c]), MEM_LEN)

        def loop(j, carry):
            _interleave([tile(j * MEM_UNROLL + u) for u in range(MEM_UNROLL)])
            return carry
        lax.fori_loop(0, N_QBLK // MEM_UNROLL, loop, 0)
        dw_ref[...] += _dot_tn(mem_ref[...], dmkv_ref[...].astype(MXU))

    stat = pl.BlockSpec((None, N_QBLK, 8, 128), lambda b: (b, 0, 0, 0))
    pairs = pl.BlockSpec((None, MEM_PAIRS, SEQ, 128), lambda b: (b, 0, 0, 0))
    return pl.pallas_call(
        body, name="mem_attn_bwd", grid=(B,),
        in_specs=[pairs, pl.BlockSpec((None, MEM_LEN, 2 * W_C), lambda b: (b, 0, 0)), pairs, stat, stat,
                  pl.BlockSpec((None, MEM_LEN, D_MODEL), lambda b: (b, 0, 0))],
        out_specs=[pairs, pl.BlockSpec((D_MODEL, 2 * W_C), lambda b: (0, 0))],
        out_shape=[jax.ShapeDtypeStruct((B, MEM_PAIRS, SEQ, 128), F32),
                   jax.ShapeDtypeStruct((D_MODEL, 2 * W_C), F32)],
        scratch_shapes=[pltpu.VMEM((MEM_LEN, 2 * W_C), F32),
                        pltpu.VMEM((MEM_PAIRS, 2 * MEM_LEN, 128), MXU), pltpu.VMEM((MEM_PAIRS, 2 * MEM_LEN, 128), MXU)],
        compiler_params=_cparams(("arbitrary",)),
    )(qc, mkv, do, lse, delta, mem)


def _headsum(t, e):
    if MXU == F32:
        return _dot(t, e)
    hi = t.astype(MXU)
    lo = (t - hi.astype(F32)).astype(MXU)
    return _dot(hi, e) + _dot(lo, e)


def _heads_to_rows(t, e):
    return sum(_dot_nt(e, part) for part in _split3(t))


POST_ROWS = 256


def _post(o_a, olse_b, o_c, z, x2, tgt, g, gain, bias, w_out, hsum, hrows):
    T = x2.shape[0]
    tm = 256
    nt = SEQ // tm

    def body(oa_ref, ob_ref, oc_ref, z_ref, x_ref, t_ref, g_ref, gain_ref, bias_ref, w_ref, e_ref, er_ref,
             gx_ref, doa_ref, dela_ref, dobb_ref, doc_ref, delc_ref, dz_ref, dw_ref, small_ref, loss_ref):
        @pl.when(pl.program_id(0) == 0)
        def _():
            dw_ref[...] = jnp.zeros_like(dw_ref)
            small_ref[...] = jnp.zeros_like(small_ref)
            loss_ref[...] = jnp.zeros_like(loss_ref)

        gg = g_ref[...]
        gain_v = gain_ref[...]
        gain_s = gain_v * (1.0 / D_MODEL)
        bias_v = bias_ref[...]
        w = w_ref[...]

        def rms(o):
            rr = lax.rsqrt(jnp.mean(o * o, axis=1, keepdims=True) + RMS_EPS)
            return o * rr, rr

        def rows_of(rs):
            oa = _lanes([oa_ref.at[c][rs, :] for c in range(SWA_Q // 2)])
            (o1, l1), (o4, l4), (o16, l16) = [
                (_lanes([ob_ref.at[p, 0][rs, :], ob_ref.at[p, 1][rs, :]]),
                 _lanes([ob_ref.at[p, 2][rs, :], ob_ref.at[p, 3][rs, :]])) for p in range(3)]
            mx = jnp.maximum(jnp.maximum(l1, l4), l16)
            e1, e4, e16 = jnp.exp(l1 - mx), jnp.exp(l4 - mx), jnp.exp(l16 - mx)
            den = e1 + e4 + e16
            ob = (e1 * o1 + e4 * o4 + e16 * o16) / den
            lse_b = mx + jnp.log(den)
            oc = _lanes([oc_ref.at[c][rs, :] for c in range(MEM_PAIRS)])
            na, ra = rms(oa)
            nb, rb = rms(ob)
            nc, rc = rms(oc)
            n = jnp.concatenate([na, nb, nc], axis=1)
            zz = z_ref[rs, :]
            sig = 0.5 * jnp.tanh(0.5 * zz) + 0.5
            sz = zz * sig
            gs = gg * sz
            u = n * gs
            r = ALPHA * x_ref[rs, :] + _dot(u.astype(MXU), w)
            rc0 = r - jnp.mean(r, axis=1, keepdims=True)
            rstd = lax.rsqrt(jnp.mean(rc0 * rc0, axis=1, keepdims=True) + LN_EPS)
            xhat = rc0 * rstd
            err = xhat * gain_v + bias_v - t_ref[rs, :]
            dxh = err * gain_s
            dr = rstd * (dxh - jnp.mean(dxh, axis=1, keepdims=True)
                         - xhat * jnp.mean(dxh * xhat, axis=1, keepdims=True))
            gx_ref[rs, :] = ALPHA * dr
            drb = dr.astype(MXU)
            du = _dot_nt(drb, w)
            dun = du * n
            dz = dun * (gg * (sig + sz * (1.0 - sig)))
            dz_ref[rs, :] = dz.astype(MXU)
            dn = du * gs

            def branch(lo, hi, nbr, rr):
                dnb = dn[:, lo:hi]
                return rr * (dnb - nbr * jnp.mean(dnb * nbr, axis=1, keepdims=True))

            def to_kernel(dob, o, do_ref, delta_ref):
                wd = dob.shape[1]
                for c in range(wd // 128):
                    do_ref.at[c][rs, :] = dob[:, c * 128:(c + 1) * 128].astype(do_ref.dtype)
                dT = _heads_to_rows(dob * o, er_ref[:, 0:wd])
                for jb in range((rs.stop - rs.start) // BLK):
                    delta_ref[rs.start // BLK + jb] = dT[0:8, jb * BLK:(jb + 1) * BLK]

            to_kernel(branch(0, W_A, na, ra), oa, doa_ref, dela_ref)
            to_kernel(branch(W_A + W_B, D_MIX, nc, rc), oc, doc_ref, delc_ref)
            dob = branch(W_A, W_A + W_B, nb, rb)
            for j, t in enumerate((dob, lse_b, _headsum(dob * ob, e_ref[...]))):
                for c in range(W_B // 128):
                    dobb_ref.at[j * (W_B // 128) + c][rs, :] = t[:, c * 128:(c + 1) * 128]
            csum = lambda t: jnp.sum(t, axis=0, keepdims=True)
            return (u, drb, jnp.sum(err * err), csum(err * xhat), csum(err), csum(dun * sz), csum(dz))

        parts = [rows_of(slice(k * POST_ROWS, (k + 1) * POST_ROWS)) for k in range(tm // POST_ROWS)]
        tot = [sum(p[i] for p in parts) for i in range(2, 7)]
        dw_ref[...] += _dot_tn(jnp.concatenate([p[0] for p in parts], axis=0),
                               jnp.concatenate([p[1] for p in parts], axis=0))
        loss_ref[...] += 0.5 * tot[0] * (1.0 / D_MODEL)
        small_ref[0:1, :] += tot[1] * (1.0 / D_MODEL)
        small_ref[1:2, :] += tot[2] * (1.0 / D_MODEL)
        small_ref[2:3, :] += tot[3]
        small_ref[3:4, :] += tot[4]

    B = T // SEQ
    row = lambda w: pl.BlockSpec((tm, w), lambda i: (i, 0))
    full = lambda a, b: pl.BlockSpec((a, b), lambda i: (0, 0))
    chunked = lambda n: pl.BlockSpec((None, n, tm, 128), lambda i: (i // nt, 0, i % nt, 0))
    stat = pl.BlockSpec((None, tm // BLK, 8, 128), lambda i: (i // nt, i % nt, 0, 0))
    return pl.pallas_call(
        body, name="post_fwd_bwd", grid=(T // tm,),
        in_specs=[chunked(SWA_Q // 2), pl.BlockSpec((None, 3, 4, tm, 128), lambda i: (i // nt, 0, 0, i % nt, 0)),
                  chunked(MEM_PAIRS),
                  row(D_MIX), row(D_MODEL), row(D_MODEL),
                  full(1, D_MIX), full(1, D_MODEL), full(1, D_MODEL), full(D_MIX, D_MODEL), full(W_B, W_B),
                  full(PICK_ROWS, W_A)],
        out_specs=[row(D_MODEL), chunked(SWA_Q // 2), stat, chunked(6), chunked(MEM_PAIRS), stat, row(D_MIX),
                   full(D_MIX, D_MODEL), full(8, D_MODEL), full(8, 128)],
        out_shape=[jax.ShapeDtypeStruct((T, D_MODEL), F32),
                   jax.ShapeDtypeStruct((B, SWA_Q // 2, SEQ, 128), MXU),
                   jax.ShapeDtypeStruct((B, N_QBLK, 8, 128), F32),
                   jax.ShapeDtypeStruct((B, 6, SEQ, 128), F32),
                   jax.ShapeDtypeStruct((B, MEM_PAIRS, SEQ, 128), MXU),
                   jax.ShapeDtypeStruct((B, N_QBLK, 8, 128), F32),
                   jax.ShapeDtypeStruct((T, D_MIX), MXU),
                   jax.ShapeDtypeStruct((D_MIX, D_MODEL), F32),
                   jax.ShapeDtypeStruct((8, D_MODEL), F32),
                   jax.ShapeDtypeStruct((8, 128), F32)],
        compiler_params=_cparams(("arbitrary",)),
    )(o_a, olse_b, o_c, z, x2, tgt, g, gain, bias, w_out, hsum, hrows)


def _dh_build(dqkva, dqkvb, dqc, dz, tab):
    T = dz.shape[0]
    tm = 512
    nt = SEQ // tm
    HQ = D_IN - D_MIX

    def body(da_ref, db6_ref, dqc_ref, dz_ref, tab_ref, dh_ref, db_ref):
        @pl.when(pl.program_id(0) == 0)
        def _():
            db_ref[...] = jnp.zeros_like(db_ref)
        tab = tab_ref[...]
        lo = lax.broadcasted_iota(jnp.int32, (tm, 128), 1) < HEAD

        def kv_grad(c):
            g0, g1 = da_ref[c], da_ref[c + 1]
            return jnp.where(lo, g0 + pltpu.roll(g0, HEAD, 1), g1 + pltpu.roll(g1, HEAD, 1))

        parts = [_rope(_lanes([da_ref[c] for c in range(SWA_Q // 2)]), tab, -1) * Q_SCALE,
                 _rope(kv_grad(4), tab, -1),
                 kv_grad(6),
                 _rope(_lanes([db6_ref[0], db6_ref[1]]), tab, -1) * Q_SCALE,
                 _rope(_lanes([db6_ref[2], db6_ref[3]]), tab, -1),
                 _lanes([db6_ref[4], db6_ref[5]]),
                 _lanes([dqc_ref[c] for c in range(MEM_PAIRS)]) * Q_SCALE]
        dhq = jnp.concatenate(parts, axis=1)
        db_ref[0:1, :] += jnp.sum(dhq, axis=0, keepdims=True)
        dh_ref[:, 0:HQ] = dhq.astype(MXU)
        dh_ref[:, HQ:D_IN] = dz_ref[...]

    row = lambda w: pl.BlockSpec((tm, w), lambda i: (i, 0))
    chunked = lambda n: pl.BlockSpec((None, n, tm, 128), lambda i: (i // nt, 0, i % nt, 0))
    return pl.pallas_call(
        body, name="dh_build", grid=(T // tm,),
        in_specs=[chunked(SWA_CHUNKS), chunked(6), chunked(MEM_PAIRS), row(D_MIX),
                  pl.BlockSpec((tm, 384), lambda i: (i % nt, 0))],
        out_specs=[row(D_IN), pl.BlockSpec((8, HQ), lambda i: (0, 0))],
        out_shape=[jax.ShapeDtypeStruct((T, D_IN), MXU), jax.ShapeDtypeStruct((8, HQ), F32)],
        compiler_params=_cparams(("arbitrary",)),
    )(dqkva, dqkvb, dqc, dz, tab)


TAIL_TK = 1024
TAIL_TN = D_IN // 4
TAIL_DH_COLS = D_IN // 2
TAIL_TM = 256
REDUCE_ROWS = 128


def _tail(xt, dh, gx1, w_in, small_g):
    T = xt.shape[1]
    kt = T // TAIL_TK
    ndw = (D_IN // TAIL_TN) * kt
    nsteps = ndw + T // TAIL_TM
    per_pass = TAIL_TN // COLS_PER_DEV
    pay = dh.dtype
    shapes = [(D_MODEL, COLS_PER_DEV), small_g.shape]
    n_arr = len(shapes)

    def body(xt_ref, dh1_ref, dh2_ref, gx_ref, w_hbm, sg_ref, dx_ref, gin_ref, gsm_ref,
             acc_ref, w_ref, stage_ref, ownin_ref, lin_ref, lsm_ref, send_sems, recv_sems, w_sem):
        s = pl.program_id(0)
        x, y, c = _my_pos()
        me = 4 * x + 2 * y + c
        lands = (lin_ref, lsm_ref)

        def src_of(a, j):
            return (stage_ref.at[j], sg_ref)[a]

        def to_peer(a, j):
            return pltpu.make_async_remote_copy(
                src_ref=src_of(a, j), dst_ref=lands[a].at[me], send_sem=send_sems.at[a, j],
                recv_sem=recv_sems.at[a, me], device_id=_dev_coords(j), device_id_type=MESH)

        def from_peer(a, m):
            return pltpu.make_async_remote_copy(
                src_ref=lands[a].at[m], dst_ref=lands[a].at[m], send_sem=send_sems.at[a, m],
                recv_sem=recv_sems.at[a, m], device_id=_dev_coords(m), device_id_type=MESH)

        w_copy = pltpu.make_async_copy(w_hbm, w_ref, w_sem)

        @pl.when(s == 0)
        def _():
            w_copy.start()
            for j in range(N_DEV):
                @pl.when(me != j)
                def _(j=j):
                    for a in range(1, n_arr):
                        to_peer(a, j).start()
            for a in range(n_arr - 1):
                lands[a][me] = jnp.zeros(shapes[a], lands[a].dtype)
            lsm_ref[me] = sg_ref[...]

        @pl.when(s < ndw)
        def _():
            k = s % kt

            @pl.when(k == 0)
            def _():
                acc_ref[...] = jnp.zeros_like(acc_ref)
            for half in range(TAIL_DH_COLS // TAIL_TN):
                @pl.when((s // kt) % (TAIL_DH_COLS // TAIL_TN) == half)
                def _(half=half):
                    acc_ref[...] += _dot(xt_ref[...], dh1_ref[:, half * TAIL_TN:(half + 1) * TAIL_TN])

            for p in range(D_IN // TAIL_TN):
                @pl.when(s == p * kt + kt - 1)
                def _(p=p):
                    for jj in range(per_pass):
                        j = p * per_pass + jj
                        blk = acc_ref[:, jj * COLS_PER_DEV:(jj + 1) * COLS_PER_DEV]
                        stage_ref[j] = blk.astype(pay)

                        @pl.when(me == j)
                        def _(blk=blk):
                            ownin_ref[...] = blk
                        pl.when(me != j)(to_peer(0, j).start)

        @pl.when(s >= ndw)
        def _():
            pl.when(s == ndw)(w_copy.wait)
            dx_ref[...] = _dot_nt(dh2_ref[...], w_ref[...]) + gx_ref[...]

        @pl.when(s == nsteps - 1)
        def _():
            for m in range(N_DEV):
                @pl.when(me != m)
                def _(m=m):
                    for a in range(n_arr):
                        from_peer(a, m).wait_recv()
            for j in range(N_DEV):
                @pl.when(me != j)
                def _(j=j):
                    for a in range(n_arr):
                        to_peer(a, j).wait_send()
            def chunk(i, carry):
                rs = pl.ds(pl.multiple_of(i * REDUCE_ROWS, REDUCE_ROWS), REDUCE_ROWS)
                g = ownin_ref[rs, :]
                for m in range(N_DEV):
                    g = g + lin_ref[m, rs, :].astype(F32)
                gin_ref[rs, :] = g
                return carry
            lax.fori_loop(0, D_MODEL // REDUCE_ROWS, chunk, 0)
            g = lsm_ref[0]
            for m in range(1, N_DEV):
                g = g + lsm_ref[m]
            gsm_ref[...] = g

    dw_step = lambda s: jnp.minimum(s, ndw - 1)
    dx_step = lambda s: jnp.maximum(s - ndw, 0)
    any_spec = pl.BlockSpec(memory_space=pl.ANY)
    vmem = pl.BlockSpec(memory_space=pltpu.VMEM)
    scratch = [pltpu.VMEM((D_MODEL, TAIL_TN), F32), pltpu.VMEM((D_MODEL, D_IN), w_in.dtype),
               pltpu.VMEM((N_DEV,) + shapes[0], pay), pltpu.VMEM(shapes[0], F32)]
    scratch += [pltpu.VMEM((N_DEV,) + shapes[a], pay) for a in range(n_arr - 1)]
    scratch += [pltpu.VMEM((N_DEV,) + shapes[-1], F32),
                pltpu.SemaphoreType.DMA((n_arr, N_DEV)), pltpu.SemaphoreType.DMA((n_arr, N_DEV)),
                pltpu.SemaphoreType.DMA]
    return pl.pallas_call(
        body, name="tail_dw_dx_reduce", grid=(nsteps,),
        in_specs=[pl.BlockSpec((D_MODEL, TAIL_TK), lambda s: (0, dw_step(s) % kt)),
                  pl.BlockSpec((TAIL_TK, TAIL_DH_COLS),
                               lambda s: (dw_step(s) % kt, dw_step(s) // (kt * (TAIL_DH_COLS // TAIL_TN)))),
                  pl.BlockSpec((TAIL_TM, D_IN), lambda s: (dx_step(s), 0)),
                  pl.BlockSpec((TAIL_TM, D_MODEL), lambda s: (dx_step(s), 0)),
                  any_spec, vmem],
        out_specs=[pl.BlockSpec((TAIL_TM, D_MODEL), lambda s: (dx_step(s), 0)), vmem, vmem],
        out_shape=[jax.ShapeDtypeStruct((T, D_MODEL), F32)] + [jax.ShapeDtypeStruct(sh, F32) for sh in shapes],
        scratch_shapes=scratch,
        compiler_params=_cparams(("arbitrary",)),
    )(xt, dh, dh, gx1, w_in, small_g)


def _adam_update(grads, params, carried):
    n = len(grads)

    def body(*refs):
        g_refs, p_refs, o_refs = refs[1:1 + n], refs[1 + n:1 + 4 * n], refs[2 + 4 * n:]
        for a in range(n):
            rows = g_refs[a].shape[0]
            cr = REDUCE_ROWS if rows % REDUCE_ROWS == 0 else rows
            w_ref, m_ref, v_ref = p_refs[3 * a:3 * a + 3]
            go_ref, d_ref, nm_ref, nv_ref = o_refs[4 * a:4 * a + 4]

            def chunk(i, carry, cr=cr, g_ref=g_refs[a], w_ref=w_ref, m_ref=m_ref, v_ref=v_ref,
                      go_ref=go_ref, d_ref=d_ref, nm_ref=nm_ref, nv_ref=nv_ref):
                rs = pl.ds(pl.multiple_of(i * cr, cr), cr)
                g = g_ref[rs, :]
                go_ref[rs, :] = g
                d_ref[rs, :], nm_ref[rs, :], nv_ref[rs, :] = _adamw(w_ref[rs, :], g, m_ref[rs, :], v_ref[rs, :])
                return carry
            lax.fori_loop(0, rows // cr, chunk, 0)

    vmem = pl.BlockSpec(memory_space=pltpu.VMEM)
    any_spec = pl.BlockSpec(memory_space=pl.ANY)
    flat = [p for grp in params for p in grp]
    outs = pl.pallas_call(
        body, name="adamw", in_specs=[any_spec] + [vmem] * (4 * n), out_specs=[any_spec] + [vmem] * (4 * n),
        out_shape=[jax.ShapeDtypeStruct(carried.shape, carried.dtype)]
        + [jax.ShapeDtypeStruct(g.shape, F32) for g in grads for _ in range(4)],
        input_output_aliases={0: 0},
        compiler_params=pltpu.CompilerParams(vmem_limit_bytes=VMEM_LIMIT),
    )(carried, *grads, *flat)
    return [outs[1 + 4 * a:5 + 4 * a] for a in range(n)], outs[0]


def _step(x, mem, w_in_s, w_mem_s, w_out_s, b_in, sinks, g, gain, bias, tgt):
    B = x.shape[0]
    T = B * SEQ
    x2 = x.reshape(T, D_MODEL)
    t2 = tgt.reshape(T, D_MODEL)
    tab = _rope_table()
    lane = jnp.arange(W_A)
    hsum = (lane[:W_B, None] // HEAD == lane[None, :W_B] // HEAD).astype(MXU)
    hrows = (jnp.arange(PICK_ROWS)[:, None] == lane[None, :] // HEAD).astype(MXU)
    me = 4 * lax.axis_index("x") + 2 * lax.axis_index("y") + lax.axis_index("c")

    (w_in_all,) = _gather_weights([w_in_s])
    qkva, qkvb, qc, z, w_in, xt, w_mem_all, w_out_all = _in_proj(x2, w_in_all, b_in, tab, [w_mem_s, w_out_s])
    w_mem = w_mem_all.reshape(D_MODEL, 2 * W_C)
    w_out = w_out_all.reshape(D_MIX, D_MODEL)

    o_a, lse_a = _swa_fwd(qkva, sinks)
    olse_b = _dil_fwd(qkvb)
    o_c, lse_c, mkv = _mem_attn_fwd(qc, mem, w_mem)

    gx1, do_a, delta_a, dobb, do_c, delta_c, dz, dw_out, small, loss = _post(
        o_a, olse_b, o_c, z, x2, t2, g, gain, bias, w_out, hsum, hrows)

    dqc, dw_mem = _mem_attn_bwd(qc, mkv, do_c, lse_c, delta_c, mem)
    blocks = [dw_mem.reshape(N_DEV, ROWS_PER_DEV, 2 * W_C), dw_out.reshape(N_DEV, ROWS_PER_DEV, D_MODEL)]
    sends = [b.astype(MXU) for b in blocks]
    owns = [lax.dynamic_index_in_dim(b, me, axis=0, keepdims=False) for b in blocks]
    dqkvb, (g_mem, g_out) = _dil_bwd(qkvb, dobb, sends, owns)
    dqkva, dsink = _swa_bwd(qkva, do_a, lse_a, delta_a, sinks)
    dh, dbq = _dh_build(dqkva, dqkvb, dqc, dz, tab)

    small_g = _pack_small(dict(b_in=jnp.concatenate([dbq[0], small[3]]), sinks=dsink[:, 0], g=small[2],
                               gain=small[0], bias=small[1], loss=loss[0, 0]))
    grad_x, g_in, g_small = _tail(xt, dh, gx1, w_in, small_g)
    return grad_x.reshape(B, SEQ, D_MODEL), g_in, g_mem, g_out, g_small


def _my_pos():
    return lax.axis_index("x"), lax.axis_index("y"), lax.axis_index("c")


def _gather_weights(shards):
    n_arr = len(shards)

    def body(*refs):
        ins, outs = refs[0:n_arr], refs[n_arr:2 * n_arr]
        send_sems, recv_sems, local_sems = refs[2 * n_arr:]
        x, y, c = _my_pos()
        me, sibling = (x, y, c), (x, y, 1 - c)
        chips = [(1 - x, y), (x, 1 - y), (1 - x, 1 - y)]

        def slot(a, pos):
            return outs[a].at[4 * pos[0] + 2 * pos[1] + pos[2]]

        def copy(a, k, block, to, src=None):
            return pltpu.make_async_remote_copy(
                src_ref=slot(a, block) if src is None else src, dst_ref=slot(a, block),
                send_sem=send_sems.at[a, k], recv_sem=recv_sems.at[a, k],
                device_id=to, device_id_type=MESH)

        mine = [pltpu.make_async_copy(ins[a], slot(a, me), local_sems.at[a]) for a in range(n_arr)]
        for cp in mine:
            cp.start()
        first = []
        for a in range(n_arr):
            first.append(copy(a, 0, me, sibling, src=ins[a]))
            first += [copy(a, 1 + j, me, (*chip, c), src=ins[a]) for j, chip in enumerate(chips)]
        for cp in first:
            cp.start()
        passed = []
        for j, chip in enumerate(chips):
            for a in range(n_arr):
                copy(a, 1 + j, (*chip, c), me).wait_recv()
                fwd = copy(a, 4 + j, (*chip, c), sibling)
                fwd.start()
                passed.append(fwd)
        for a in range(n_arr):
            copy(a, 0, sibling, me).wait_recv()
            for j, chip in enumerate(chips):
                copy(a, 4 + j, (*chip, 1 - c), me).wait_recv()
        for cp in first + passed:
            cp.wait_send()
        for cp in mine:
            cp.wait()

    any_spec = pl.BlockSpec(memory_space=pl.ANY)
    return pl.pallas_call(
        body, name="gather_weights",
        in_specs=[any_spec] * n_arr, out_specs=[any_spec] * n_arr,
        out_shape=[jax.ShapeDtypeStruct((N_DEV,) + s.shape, s.dtype) for s in shards],
        scratch_shapes=[pltpu.SemaphoreType.DMA((n_arr, 7)), pltpu.SemaphoreType.DMA((n_arr, 7)),
                        pltpu.SemaphoreType.DMA((n_arr,))],
    )(*shards)


def _adamw(w, g, m, v):
    m = ADAM_B1 * m + (1.0 - ADAM_B1) * g
    v = ADAM_B2 * v + (1.0 - ADAM_B2) * (g * g)
    m_hat = m / (1.0 - ADAM_B1 ** ADAM_STEP)
    v_hat = v / (1.0 - ADAM_B2 ** ADAM_STEP)
    delta = -ADAM_LR * (m_hat / (jnp.sqrt(v_hat) + ADAM_EPS) + ADAM_WD * w)
    return delta, m, v


_SMALL_SIZES = (("b_in", D_IN), ("g", D_MIX), ("gain", D_MODEL), ("bias", D_MODEL), ("sinks", SWA_Q), ("loss", 1))


def _pack_small(d):
    flat = jnp.concatenate([jnp.reshape(d[k], (-1,)).astype(F32) if k in d else jnp.zeros((n,), F32)
                            for k, n in _SMALL_SIZES])
    flat = jnp.pad(flat, (0, SMALL_ROWS * 128 - flat.shape[0]))
    return flat.reshape(SMALL_ROWS, 128)


def _unpack_small(p):
    flat = p.reshape(-1)
    out, off = {}, 0
    for k, n in _SMALL_SIZES:
        out[k] = flat[off:off + n].reshape(1, n)
        off += n
    return out


def kernel(x, mem, w_in, b_in, w_mem, attn_sinks, g_branch, w_out, ln_gain, ln_bias, loss_target, m_w_in, m_b_in, m_w_mem, m_attn_sinks, m_g_branch, m_w_out, m_ln_gain, m_ln_bias, v_w_in, v_b_in, v_w_mem, v_attn_sinks, v_g_branch, v_w_out, v_ln_gain, v_ln_bias):
    grad_x, g_in, g_mem, g_out, g_small = _step(
        x, mem, w_in[0].astype(MXU), w_mem[0].astype(MXU), w_out[0].astype(MXU), b_in, attn_sinks[0],
        g_branch, ln_gain, ln_bias, loss_target)

    small_w = _pack_small(dict(b_in=b_in, g=g_branch, gain=ln_gain, bias=ln_bias, sinks=attn_sinks))
    small_m = _pack_small(dict(b_in=m_b_in, g=m_g_branch, gain=m_ln_gain, bias=m_ln_bias, sinks=m_attn_sinks))
    small_v = _pack_small(dict(b_in=v_b_in, g=v_g_branch, gain=v_ln_gain, bias=v_ln_bias, sinks=v_attn_sinks))
    grads = [g_in, g_mem, g_out, g_small]
    params = [(w_in[0], m_w_in[0], v_w_in[0]), (w_mem[0], m_w_mem[0], v_w_mem[0]),
              (w_out[0], m_w_out[0], v_w_out[0]), (small_w, small_m, small_v)]
    res, grad_x = _adam_update(grads, params, grad_x)
    big = [[r[None] for r in res[a]] for a in range(3)]
    sm = [_unpack_small(r) for r in res[3]]

    def group(i):
        return (big[0][i], sm[i]["b_in"], big[1][i], sm[i]["sinks"], sm[i]["g"], big[2][i],
                sm[i]["gain"], sm[i]["bias"])

    loss = sm[0]["loss"].reshape(())
    return (loss, grad_x, *group(0), *group(1), *group(2), *group(3))
```

```python
import functools
import math

import jax
import jax.numpy as jnp
from jax import lax
from jax.experimental import pallas as pl
from jax.experimental.pallas import tpu as pltpu

F32 = jnp.float32
MXU = jnp.bfloat16

D_MODEL = 1024
SEQ = 2048
HEAD = 64
BLK = 128
SWA_Q, SWA_KV = 8, 2
DIL_H = 4
MEM_H = 4
MEM_LEN = 256
W_A, W_KVA, W_B, W_C = 512, 128, 256, 256
D_MIX = 1024
D_IN = 2816
N_DEV = 8
COLS_PER_DEV = D_IN // N_DEV
ROWS_PER_DEV = D_MODEL // N_DEV
ROPE_THETA = 10000.0
LN_EPS = 1e-5
RMS_EPS = 1e-6
ALPHA = 2.0 ** 0.25
Q_SCALE = HEAD ** -0.5
NEG = -1e30
SMALL_ROWS = 48
VMEM_LIMIT = 56 * 1024 * 1024

ADAM_LR = 0.001
ADAM_B1 = 0.9
ADAM_B2 = 0.999
ADAM_EPS = 1e-08
ADAM_WD = 0.01
ADAM_STEP = 10

MESH = pl.DeviceIdType.MESH


def _cparams(sem=None):
    return pltpu.CompilerParams(dimension_semantics=sem, vmem_limit_bytes=VMEM_LIMIT)


def _dot(a, b):
    return jnp.dot(a, b, preferred_element_type=F32)


def _dot_nt(a, b):
    return lax.dot_general(a, b, (((1,), (1,)), ((), ())), preferred_element_type=F32)


def _dot_t0(a, b):
    return lax.dot_general(a, b, (((0,), (0,)), ((), ())), preferred_element_type=F32)


def _dot_tn(a, b):
    return jnp.dot(a.T.astype(MXU), b, preferred_element_type=F32)


def _rope(t, tab, sign):
    cos, sa, sb = tab[:, 0:128], tab[:, 128:256], tab[:, 256:384]
    outs = []
    for c in range(t.shape[1] // 128):
        tc = t[:, c * 128:(c + 1) * 128]
        r = pltpu.roll(tc, 96, 1) * sa + pltpu.roll(tc, 32, 1) * sb
        outs.append(tc * cos + r if sign > 0 else tc * cos - r)
    return outs[0] if len(outs) == 1 else jnp.concatenate(outs, axis=1)


def _rope_table():
    pos = jnp.arange(SEQ, dtype=F32)
    inv = ROPE_THETA ** (-jnp.arange(0, HEAD, 2, dtype=F32) / HEAD)
    ang = pos[:, None] * inv[None, :]
    ang = jnp.concatenate([ang, ang], axis=-1)
    cos, sin = jnp.cos(ang), jnp.sin(ang)
    lane = jnp.arange(HEAD)[None, :]
    sa = jnp.where(lane < HEAD // 2, -sin, 0.0)
    sb = jnp.where(lane >= HEAD // 2, sin, 0.0)
    two = lambda t: jnp.concatenate([t, t], axis=-1)
    return jnp.concatenate([two(cos), two(sa), two(sb)], axis=-1).astype(F32)


def _dev_coords(j):
    return (j >> 2, (j >> 1) & 1, j & 1)


def _in_proj(x2, w_all, b_in, tab, late_shards):
    T = x2.shape[0]
    tm = 512
    n_late = len(late_shards)

    def body(x_ref, wall_ref, b_ref, tab_ref, *rest):
        late_in, rest = rest[:n_late], rest[n_late:]
        qkva_ref, qkvb_ref, qc_ref, z_ref, w_ref, xt_ref = rest[:6]
        late_out = rest[6:6 + n_late]
        send_sems, recv_sems, local_sems = rest[6 + n_late:]
        step, last = pl.program_id(0), pl.num_programs(0) - 1
        x, y, c = _my_pos()
        me = 4 * x + 2 * y + c

        def to_peer(a, j):
            return pltpu.make_async_remote_copy(
                src_ref=late_in[a], dst_ref=late_out[a].at[me], send_sem=send_sems.at[a, j],
                recv_sem=recv_sems.at[a, me], device_id=_dev_coords(j), device_id_type=MESH)

        def from_peer(a, m):
            return pltpu.make_async_remote_copy(
                src_ref=late_out[a].at[m], dst_ref=late_out[a].at[m], send_sem=send_sems.at[a, m],
                recv_sem=recv_sems.at[a, m], device_id=_dev_coords(m), device_id_type=MESH)

        def mine(a):
            return pltpu.make_async_copy(late_in[a], late_out[a].at[me], local_sems.at[a])

        @pl.when(step == 0)
        def _():
            for a in range(n_late):
                mine(a).start()
                for j in range(N_DEV):
                    pl.when(me != j)(to_peer(a, j).start)
            for j in range(N_DEV):
                w_ref[:, j * COLS_PER_DEV:(j + 1) * COLS_PER_DEV] = wall_ref[j]

        xb = x_ref[...].astype(MXU)
        xt_ref[...] = x_ref[...].T.astype(MXU)
        tab = tab_ref[...]

        def seg(c0, c1):
            return _dot(xb, w_ref[:, c0:c1]) + b_ref[:, c0:c1]

        qa = (_rope(seg(0, 512), tab, 1) * Q_SCALE).astype(MXU)
        for c in range(SWA_Q // 2):
            qkva_ref[c] = qa[:, c * 128:(c + 1) * 128]
        lo = lax.broadcasted_iota(jnp.int32, (tm, 128), 1) < HEAD
        for j, t in enumerate((_rope(seg(512, 640), tab, 1), seg(640, 768))):
            other = pltpu.roll(t, HEAD, 1)
            qkva_ref[4 + 2 * j] = jnp.where(lo, t, other).astype(MXU)
            qkva_ref[5 + 2 * j] = jnp.where(lo, other, t).astype(MXU)
        qkvb = (_rope(seg(768, 1024), tab, 1) * Q_SCALE, _rope(seg(1024, 1280), tab, 1), seg(1280, 1536))
        for j, t in enumerate(qkvb):
            for c in range(2):
                qkvb_ref[2 * j + c] = t[:, c * 128:(c + 1) * 128]
        qc = (seg(1536, 1792) * Q_SCALE).astype(MXU)
        for c in range(MEM_H // 2):
            qc_ref[c] = qc[:, c * 128:(c + 1) * 128]
        z_ref[...] = seg(1792, 2816)

        @pl.when(step == last)
        def _():
            for a in range(n_late):
                mine(a).wait()
                for m in range(N_DEV):
                    pl.when(me != m)(from_peer(a, m).wait_recv)
                for j in range(N_DEV):
                    pl.when(me != j)(to_peer(a, j).wait_send)

    nt = SEQ // tm
    any_spec = pl.BlockSpec(memory_space=pl.ANY)
    chunked = lambda n: pl.BlockSpec((None, n, tm, 128), lambda i: (i // nt, 0, i % nt, 0))
    return pl.pallas_call(
        body, name="in_proj_fwd",
        grid=(T // tm,),
        in_specs=[pl.BlockSpec((tm, D_MODEL), lambda i: (i, 0)),
                  pl.BlockSpec((N_DEV, D_MODEL, COLS_PER_DEV), lambda i: (0, 0, 0)),
                  pl.BlockSpec((1, D_IN), lambda i: (0, 0)),
                  pl.BlockSpec((tm, 384), lambda i: (i % nt, 0))] + [any_spec] * n_late,
        out_specs=[chunked(SWA_CHUNKS), chunked(6), chunked(MEM_H // 2),
                   pl.BlockSpec((tm, D_MIX), lambda i: (i, 0)),
                   pl.BlockSpec((D_MODEL, D_IN), lambda i: (0, 0)),
                   pl.BlockSpec((D_MODEL, tm), lambda i: (0, i))] + [any_spec] * n_late,
        out_shape=[jax.ShapeDtypeStruct((T // SEQ, SWA_CHUNKS, SEQ, 128), MXU),
                   jax.ShapeDtypeStruct((T // SEQ, 6, SEQ, 128), F32),
                   jax.ShapeDtypeStruct((T // SEQ, MEM_H // 2, SEQ, 128), MXU),
                   jax.ShapeDtypeStruct((T, D_MIX), F32),
                   jax.ShapeDtypeStruct((D_MODEL, D_IN), w_all.dtype),
                   jax.ShapeDtypeStruct((D_MODEL, T), MXU)]
        + [jax.ShapeDtypeStruct((N_DEV,) + s.shape, s.dtype) for s in late_shards],
        scratch_shapes=[pltpu.SemaphoreType.DMA((n_late, N_DEV)), pltpu.SemaphoreType.DMA((n_late, N_DEV)),
                        pltpu.SemaphoreType.DMA((n_late,))],
        compiler_params=_cparams(("arbitrary",)),
    )(x2, w_all, b_in, tab, *late_shards)


CHAIN = 4


def _band_bias(max_dist):
    kj = lax.broadcasted_iota(jnp.int32, (2 * BLK, BLK), 0)
    qi = lax.broadcasted_iota(jnp.int32, (2 * BLK, BLK), 1)
    dist = qi + BLK - kj
    band = jnp.where((dist >= 0) & (dist <= max_dist), 0.0, NEG).astype(F32)
    k1 = lax.broadcasted_iota(jnp.int32, (BLK, BLK), 0)
    q1 = lax.broadcasted_iota(jnp.int32, (BLK, BLK), 1)
    first = jnp.where((q1 - k1 >= 0) & (q1 - k1 <= max_dist), 0.0, NEG).astype(F32)
    return jnp.concatenate([band] * CHAIN, axis=1), jnp.concatenate([first] * CHAIN, axis=1)


def _lanes(parts):
    return jnp.concatenate(parts, axis=1)


PICK_ROWS = 16


def _stack_pair(t):
    lo = (lax.broadcasted_iota(jnp.int32, t.shape, 1) < HEAD).astype(F32)
    return jnp.concatenate([t * lo, t * (1.0 - lo)], axis=0).astype(MXU)


def _pair_rows(x, n):
    lo = lax.broadcasted_iota(jnp.int32, (n, 128), 1) < HEAD
    return jnp.where(lo, x[0:n], x[n:2 * n])


def _split3(t):
    if MXU == F32:
        return (t,)
    hi = t.astype(MXU)
    r = t - hi.astype(F32)
    mid = r.astype(MXU)
    return hi, mid, (r - mid.astype(F32)).astype(MXU)


def _interleave(tiles):
    tiles = list(tiles)
    while tiles:
        for t in list(tiles):
            try:
                next(t)
            except StopIteration:
                tiles.remove(t)


def _softmax_cols(sT, sinkrow=None):
    m = jnp.max(sT, axis=0, keepdims=True)
    if sinkrow is not None:
        m = jnp.maximum(m, sinkrow)
    pT = jnp.exp(sT - m)
    l = jnp.sum(pT, axis=0, keepdims=True)
    if sinkrow is not None:
        l = l + jnp.exp(sinkrow - m)
    return (pT * (1.0 / l)).astype(MXU), m + jnp.log(l)


SWA_CHUNKS = 8
SWA_UNROLL = 3
N_QBLK = SEQ // BLK


def _swa_fwd(qkva, sinks):
    B = qkva.shape[0]
    G = SWA_Q // SWA_KV

    def body(sink_ref, qkv_ref, o_ref, lse_ref):
        band, first = _band_bias(BLK - 1)
        sinkrows = [_lanes([jnp.full((1, BLK), sink_ref[G * hk + j], F32) for j in range(G)])
                    for hk in range(SWA_KV)]

        def tile(hk, blk, rows_q, rows_k, bias):
            nk = bias.shape[0]
            k2 = _stack_pair(qkv_ref.at[4 + hk][rows_k, :])
            sT = []
            for c in (2 * hk, 2 * hk + 1):
                s2 = _dot_nt(k2, qkv_ref.at[c][rows_q, :])
                sT += [s2[0:nk], s2[nk:2 * nk]]
            yield
            pnT, lse = _softmax_cols(_lanes(sT) + bias, sinkrows[hk])
            yield
            v2 = _stack_pair(qkv_ref.at[6 + hk][rows_k, :])
            for j, c in enumerate((2 * hk, 2 * hk + 1)):
                p2 = jnp.concatenate([pnT[:, 2 * j * BLK:(2 * j + 1) * BLK],
                                      pnT[:, (2 * j + 1) * BLK:(2 * j + 2) * BLK]], axis=0)
                o_ref.at[c][rows_q, :] = _dot_t0(p2, v2)
            for j in range(G):
                lse_ref.at[blk][G * hk + j:G * hk + j + 1, :] = lse[:, j * BLK:(j + 1) * BLK]

        def tiles_at(i):
            r0 = pl.multiple_of(i * BLK, BLK)
            rk = pl.multiple_of(i * BLK - BLK, BLK)
            return [tile(hk, i, pl.ds(r0, BLK), pl.ds(rk, 2 * BLK), band) for hk in range(SWA_KV)]

        _interleave([tile(hk, 0, pl.ds(0, BLK), pl.ds(0, BLK), first) for hk in range(SWA_KV)])

        def loop(j, carry):
            _interleave([t for u in range(SWA_UNROLL) for t in tiles_at(1 + j * SWA_UNROLL + u)])
            return carry
        lax.fori_loop(0, (N_QBLK - 1) // SWA_UNROLL, loop, 0)

    return pl.pallas_call(
        body, name="swa_fwd", grid=(B,),
        in_specs=[pl.BlockSpec(memory_space=pltpu.SMEM),
                  pl.BlockSpec((None, SWA_CHUNKS, SEQ, 128), lambda b: (b, 0, 0, 0))],
        out_specs=[pl.BlockSpec((None, SWA_Q // 2, SEQ, 128), lambda b: (b, 0, 0, 0)),
                   pl.BlockSpec((None, N_QBLK, 8, 128), lambda b: (b, 0, 0, 0))],
        out_shape=[jax.ShapeDtypeStruct((B, SWA_Q // 2, SEQ, 128), F32),
                   jax.ShapeDtypeStruct((B, N_QBLK, 8, 128), F32)],
        compiler_params=_cparams(("arbitrary",)),
    )(sinks, qkva)


def _swa_bwd(qkva, do, lse, delta, sinks):
    B = qkva.shape[0]
    G = SWA_Q // SWA_KV

    def body(sink_ref, qkv_ref, do_ref, lse_ref, delta_ref, dq_ref, dsink_ref):
        band, first = _band_bias(BLK - 1)
        sinkrows = [_lanes([jnp.full((1, BLK), sink_ref[G * hk + j], F32) for j in range(G)])
                    for hk in range(SWA_KV)]

        @pl.when(pl.program_id(0) == 0)
        def _():
            dsink_ref[...] = jnp.zeros_like(dsink_ref)
        for c in range(4, SWA_CHUNKS):
            dq_ref[c] = jnp.zeros((SEQ, 128), F32)

        def tile(hk, blk, rows_q, rows_k, bias, accs):
            nk = bias.shape[0]
            k2 = _stack_pair(qkv_ref.at[4 + hk][rows_k, :])
            v2 = _stack_pair(qkv_ref.at[6 + hk][rows_k, :])
            qcs, docs, sT, dpT = [], [], [], []
            for c in (2 * hk, 2 * hk + 1):
                qc, doc = qkv_ref.at[c][rows_q, :], do_ref.at[c][rows_q, :]
                s2, dp2 = _dot_nt(k2, qc), _dot_nt(v2, doc)
                sT += [s2[0:nk], s2[nk:2 * nk]]
                dpT += [dp2[0:nk], dp2[nk:2 * nk]]
                qcs.append(qc)
                docs.append(doc)
            heads = slice(G * hk, G * hk + G)
            lse_r = _lanes([lse_ref.at[blk][h:h + 1, :] for h in range(G * hk, G * hk + G)])
            delta_r = _lanes([delta_ref.at[blk][h:h + 1, :] for h in range(G * hk, G * hk + G)])
            yield
            pT = jnp.exp(_lanes(sT) + bias - lse_r)
            dsT = pT * (_lanes(dpT) - delta_r)
            dsb, pb = dsT.astype(MXU), pT.astype(MXU)
            accs[hk] = accs[hk] - jnp.exp(sinkrows[hk] - lse_r) * delta_r
            yield
            dk2 = dv2 = None
            for j, c in enumerate((2 * hk, 2 * hk + 1)):
                q0, q1 = slice(2 * j * BLK, (2 * j + 1) * BLK), slice((2 * j + 1) * BLK, (2 * j + 2) * BLK)
                ds2 = jnp.concatenate([dsb[:, q0], dsb[:, q1]], axis=0)
                p2 = jnp.concatenate([pb[:, q0], pb[:, q1]], axis=0)
                dq_ref.at[c][rows_q, :] = _dot_t0(ds2, k2)
                dk2 = _dot(ds2, qcs[j]) if dk2 is None else dk2 + _dot(ds2, qcs[j])
                dv2 = _dot(p2, docs[j]) if dv2 is None else dv2 + _dot(p2, docs[j])
            dq_ref.at[4 + hk][rows_k, :] += _pair_rows(dk2, nk)
            dq_ref.at[6 + hk][rows_k, :] += _pair_rows(dv2, nk)

        def run(tiles_of, accs):
            accs = list(accs)
            _interleave(tiles_of(accs))
            return tuple(accs)

        zero = jnp.zeros((1, G * BLK), F32)
        accs = run(lambda a: [tile(hk, 0, pl.ds(0, BLK), pl.ds(0, BLK), first, a) for hk in range(SWA_KV)],
                   (zero,) * SWA_KV)

        def loop(j, accs):
            def tiles_of(a):
                out = []
                for u in range(SWA_UNROLL):
                    i = 1 + j * SWA_UNROLL + u
                    r0 = pl.multiple_of(i * BLK, BLK)
                    rk = pl.multiple_of(i * BLK - BLK, BLK)
                    out += [tile(hk, i, pl.ds(r0, BLK), pl.ds(rk, 2 * BLK), band, a) for hk in range(SWA_KV)]
                return out
            return run(tiles_of, accs)
        accs = lax.fori_loop(0, (N_QBLK - 1) // SWA_UNROLL, loop, accs)
        for hk in range(SWA_KV):
            for j in range(G):
                tot = jnp.sum(accs[hk][:, j * BLK:(j + 1) * BLK], axis=1, keepdims=True)
                dsink_ref[G * hk + j:G * hk + j + 1, :] += jnp.broadcast_to(tot, (1, 128))

    stat = pl.BlockSpec((None, N_QBLK, 8, 128), lambda b: (b, 0, 0, 0))
    return pl.pallas_call(
        body, name="swa_bwd", grid=(B,),
        in_specs=[pl.BlockSpec(memory_space=pltpu.SMEM),
                  pl.BlockSpec((None, SWA_CHUNKS, SEQ, 128), lambda b: (b, 0, 0, 0)),
                  pl.BlockSpec((None, SWA_Q // 2, SEQ, 128), lambda b: (b, 0, 0, 0)), stat, stat],
        out_specs=[pl.BlockSpec((None, SWA_CHUNKS, SEQ, 128), lambda b: (b, 0, 0, 0)),
                   pl.BlockSpec((8, 128), lambda b: (0, 0))],
        out_shape=[jax.ShapeDtypeStruct((B, SWA_CHUNKS, SEQ, 128), F32), jax.ShapeDtypeStruct((8, 128), F32)],
        compiler_params=_cparams(("arbitrary",)),
    )(sinks, qkva, do, lse, delta)


DILATIONS = (1, 4, 16)
DIL_PAIRS_H = DIL_H // 2


def _stream_rows(d, r, i, n):
    if d == 1:
        return pl.ds(pl.multiple_of(i * BLK, BLK), n)
    return pl.ds(r + i * (BLK * d), n, stride=d)


def _spread_matrix():
    row = lax.broadcasted_iota(jnp.int32, (PICK_ROWS, 128), 0)
    lane = lax.broadcasted_iota(jnp.int32, (PICK_ROWS, 128), 1)
    return ((row < 6) & ((row % 2 == 1) == (lane >= HEAD))).astype(MXU)


def _lanes_to_tokens(v0, v1, spread):
    n = v0.shape[1]
    row = lax.broadcasted_iota(jnp.int32, (PICK_ROWS, n), 0)
    a = jnp.zeros((PICK_ROWS, n), F32)
    for i, (p0, p1) in enumerate(zip(_split3(v0), _split3(v1))):
        a = jnp.where(row == 2 * i, p0.astype(F32), a)
        a = jnp.where(row == 2 * i + 1, p1.astype(F32), a)
    return _dot_t0(a.astype(MXU), spread)


def _tokens_to_lanes(t):
    r = t.T
    return r[0:1, :], r[HEAD:HEAD + 1, :]


def _dil_schedule(body_first, body_next):
    for p, d in enumerate(DILATIONS):
        nblk = SEQ // d // BLK
        if d == 1:
            _interleave([body_first(p, d, 0)])
            def loop(j, c, p=p, d=d):
                _interleave([body_next(p, d, 0, 1 + 3 * j + u) for u in range(3)])
                return c
            lax.fori_loop(0, (nblk - 1) // 3, loop, 0)
        elif nblk > 1:
            def loop(r, c, p=p, d=d, nblk=nblk):
                _interleave([body_first(p, d, r)] + [body_next(p, d, r, i) for i in range(1, nblk)])
                return c
            lax.fori_loop(0, d, loop, 0)
        else:
            def loop(j, c, p=p, d=d):
                _interleave([body_first(p, d, 4 * j + u) for u in range(4)])
                return c
            lax.fori_loop(0, d // 4, loop, 0)


def _dil_fwd(qkvb):
    B = qkvb.shape[0]

    def body(qkv_ref, o_ref):
        band, first = _band_bias(BLK)
        spread = _spread_matrix()

        def block(p, d, rows_q, rows_k, bias):
            nk = bias.shape[0]
            sT = []
            for c in range(DIL_PAIRS_H):
                qc = qkv_ref.at[c][rows_q, :].astype(MXU)
                s2 = _dot_nt(_stack_pair(qkv_ref.at[DIL_PAIRS_H + c][rows_k, :]), qc)
                sT += [s2[0:nk], s2[nk:2 * nk]]
            yield
            sT = _lanes(sT) + bias
            m = jnp.max(sT, axis=0, keepdims=True)
            pT = jnp.exp(sT - m)
            l = jnp.sum(pT, axis=0, keepdims=True)
            pnT = (pT * (1.0 / l)).astype(MXU)
            lse = m + jnp.log(l)
            yield
            for c in range(DIL_PAIRS_H):
                q0, q1 = slice(2 * c * BLK, (2 * c + 1) * BLK), slice((2 * c + 1) * BLK, (2 * c + 2) * BLK)
                p2 = jnp.concatenate([pnT[:, q0], pnT[:, q1]], axis=0)
                o_ref.at[p, c][rows_q, :] = _dot_t0(p2, _stack_pair(qkv_ref.at[2 * DIL_PAIRS_H + c][rows_k, :]))
                o_ref.at[p, DIL_PAIRS_H + c][rows_q, :] = _lanes_to_tokens(lse[:, q0], lse[:, q1], spread)

        def body_first(p, d, r):
            rows = _stream_rows(d, r, 0, BLK)
            return block(p, d, rows, rows, first)

        def body_next(p, d, r, i):
            return block(p, d, _stream_rows(d, r, i, BLK), _stream_rows(d, r, i - 1, 2 * BLK), band)

        _dil_schedule(body_first, body_next)

    return pl.pallas_call(
        body, name="dil_fwd", grid=(B,),
        in_specs=[pl.BlockSpec((None, 6, SEQ, 128), lambda b: (b, 0, 0, 0))],
        out_specs=pl.BlockSpec((None, 3, 4, SEQ, 128), lambda b: (b, 0, 0, 0, 0)),
        out_shape=jax.ShapeDtypeStruct((B, 3, 4, SEQ, 128), F32),
        compiler_params=_cparams(("arbitrary",)),
    )(qkvb)


def _reduce_scatter_ops(send_refs, land_refs, send_sems, recv_sems):
    x, y, c = _my_pos()
    me = 4 * x + 2 * y + c
    n = len(send_refs)

    def to_peer(a, j):
        return pltpu.make_async_remote_copy(
            src_ref=send_refs[a].at[j], dst_ref=land_refs[a].at[me], send_sem=send_sems.at[a, j],
            recv_sem=recv_sems.at[a, me], device_id=_dev_coords(j), device_id_type=MESH)

    def from_peer(a, m):
        return pltpu.make_async_remote_copy(
            src_ref=land_refs[a].at[m], dst_ref=land_refs[a].at[m], send_sem=send_sems.at[a, m],
            recv_sem=recv_sems.at[a, m], device_id=_dev_coords(m), device_id_type=MESH)

    def start():
        for j in range(N_DEV):
            @pl.when(me != j)
            def _(j=j):
                for a in range(n):
                    to_peer(a, j).start()
        for a in range(n):
            land_refs[a][me] = jnp.zeros(land_refs[a].shape[1:], land_refs[a].dtype)

    def finish(own_refs, out_refs):
        for m in range(N_DEV):
            @pl.when(me != m)
            def _(m=m):
                for a in range(n):
                    from_peer(a, m).wait_recv()
        for j in range(N_DEV):
            @pl.when(me != j)
            def _(j=j):
                for a in range(n):
                    to_peer(a, j).wait_send()
        for a in range(n):
            def chunk(i, carry, a=a):
                rs = pl.ds(pl.multiple_of(i * REDUCE_ROWS, REDUCE_ROWS), REDUCE_ROWS)
                g = own_refs[a][rs, :]
                for m in range(N_DEV):
                    g = g + land_refs[a][m, rs, :].astype(F32)
                out_refs[a][rs, :] = g
                return carry
            lax.fori_loop(0, own_refs[a].shape[0] // REDUCE_ROWS, chunk, 0)

    return start, finish


def _dil_bwd(qkvb, dobb, sends, owns):
    B = qkvb.shape[0]
    n_rs = len(sends)

    def body(qkv_ref, dob_ref, *rest):
        send_refs, own_refs = rest[:n_rs], rest[n_rs:2 * n_rs]
        dq_ref = rest[2 * n_rs]
        out_refs = rest[2 * n_rs + 1:3 * n_rs + 1]
        land_refs = rest[3 * n_rs + 1:4 * n_rs + 1]
        send_sems, recv_sems = rest[4 * n_rs + 1:]
        rs_start, rs_finish = _reduce_scatter_ops(send_refs, land_refs, send_sems, recv_sems)
        pl.when(pl.program_id(0) == 0)(rs_start)

        band, first = _band_bias(BLK)
        dq_ref[...] = jnp.zeros_like(dq_ref)

        def block(p, d, rows_q, rows_k, bias):
            nk = bias.shape[0]
            lo = lax.broadcasted_iota(jnp.int32, (nk, 128), 1) < HEAD
            qcs, docs, k2s, sT, dpT, lse, delta = [], [], [], [], [], [], []
            for c in range(DIL_PAIRS_H):
                qc = qkv_ref.at[c][rows_q, :].astype(MXU)
                doc = dob_ref.at[c][rows_q, :].astype(MXU)
                k2 = _stack_pair(qkv_ref.at[DIL_PAIRS_H + c][rows_k, :])
                s2 = _dot_nt(k2, qc)
                dp2 = _dot_nt(_stack_pair(qkv_ref.at[2 * DIL_PAIRS_H + c][rows_k, :]), doc)
                sT += [s2[0:nk], s2[nk:2 * nk]]
                dpT += [dp2[0:nk], dp2[nk:2 * nk]]
                lse += _tokens_to_lanes(dob_ref.at[DIL_PAIRS_H + c][rows_q, :])
                delta += _tokens_to_lanes(dob_ref.at[2 * DIL_PAIRS_H + c][rows_q, :])
                qcs.append(qc)
                docs.append(doc)
                k2s.append(k2)
            yield
            pT = jnp.exp(_lanes(sT) + bias - _lanes(lse))
            dsT = pT * (_lanes(dpT) - _lanes(delta))
            dsb, pb = dsT.astype(MXU), pT.astype(MXU)
            yield
            for c in range(DIL_PAIRS_H):
                q0, q1 = slice(2 * c * BLK, (2 * c + 1) * BLK), slice((2 * c + 1) * BLK, (2 * c + 2) * BLK)
                ds2 = jnp.concatenate([dsb[:, q0], dsb[:, q1]], axis=0)
                p2 = jnp.concatenate([pb[:, q0], pb[:, q1]], axis=0)
                dq_ref.at[c][rows_q, :] += _dot_t0(ds2, k2s[c])
                dk2, dv2 = _dot(ds2, qcs[c]), _dot(p2, docs[c])
                dq_ref.at[DIL_PAIRS_H + c][rows_k, :] += jnp.where(lo, dk2[0:nk], dk2[nk:2 * nk])
                dq_ref.at[2 * DIL_PAIRS_H + c][rows_k, :] += jnp.where(lo, dv2[0:nk], dv2[nk:2 * nk])

        def body_first(p, d, r):
            rows = _stream_rows(d, r, 0, BLK)
            return block(p, d, rows, rows, first)

        def body_next(p, d, r, i):
            return block(p, d, _stream_rows(d, r, i, BLK), _stream_rows(d, r, i - 1, 2 * BLK), band)

        _dil_schedule(body_first, body_next)

        @pl.when(pl.program_id(0) == pl.num_programs(0) - 1)
        def _():
            rs_finish(own_refs, out_refs)

    spec = pl.BlockSpec((None, 6, SEQ, 128), lambda b: (b, 0, 0, 0))
    any_spec = pl.BlockSpec(memory_space=pl.ANY)
    vmem = pl.BlockSpec(memory_space=pltpu.VMEM)
    outs = pl.pallas_call(
        body, name="dil_bwd", grid=(B,),
        in_specs=[spec, spec] + [any_spec] * n_rs + [vmem] * n_rs, out_specs=[spec] + [vmem] * n_rs,
        out_shape=[jax.ShapeDtypeStruct((B, 6, SEQ, 128), F32)] + [jax.ShapeDtypeStruct(o.shape, F32) for o in owns],
        scratch_shapes=[pltpu.VMEM(s.shape, s.dtype) for s in sends]
        + [pltpu.SemaphoreType.DMA((n_rs, N_DEV)), pltpu.SemaphoreType.DMA((n_rs, N_DEV))],
        compiler_params=_cparams(("arbitrary",)),
    )(qkvb, dobb, *sends, *owns)
    return outs[0], outs[1:]


MEM_UNROLL = 4
MEM_PAIRS = MEM_H // 2


def _mem_attn_fwd(qc, mem, w_mem):
    B = qc.shape[0]

    def body(q_ref, mem_ref, w_ref, o_ref, lse_ref, mkv_ref, k2_ref, v2_ref):
        mkv = _dot(mem_ref[...].astype(MXU), w_ref[...])
        mkv_ref[...] = mkv.astype(MXU)
        for c in range(MEM_PAIRS):
            k2_ref[c] = _stack_pair(mkv[:, c * 128:(c + 1) * 128])
            v2_ref[c] = _stack_pair(mkv[:, W_C + c * 128:W_C + (c + 1) * 128])
        lse_ref[...] = jnp.zeros_like(lse_ref)

        def tile(blk):
            rows = pl.ds(pl.multiple_of(blk * BLK, BLK), BLK)
            sT = []
            for c in range(MEM_PAIRS):
                s2 = _dot_nt(k2_ref[c], q_ref.at[c][rows, :])
                sT += [s2[0:MEM_LEN], s2[MEM_LEN:2 * MEM_LEN]]
            yield
            pnT, lse = _softmax_cols(_lanes(sT))
            yield
            for c in range(MEM_PAIRS):
                p2 = jnp.concatenate([pnT[:, 2 * c * BLK:(2 * c + 1) * BLK],
                                      pnT[:, (2 * c + 1) * BLK:(2 * c + 2) * BLK]], axis=0)
                o_ref.at[c][rows, :] = _dot_t0(p2, v2_ref[c])
            for h in range(MEM_H):
                lse_ref.at[blk][h:h + 1, :] = lse[:, h * BLK:(h + 1) * BLK]

        def loop(j, carry):
            _interleave([tile(j * MEM_UNROLL + u) for u in range(MEM_UNROLL)])
            return carry
        lax.fori_loop(0, N_QBLK // MEM_UNROLL, loop, 0)

    return pl.pallas_call(
        body, name="mem_attn_fwd", grid=(B,),
        in_specs=[pl.BlockSpec((None, MEM_PAIRS, SEQ, 128), lambda b: (b, 0, 0, 0)),
                  pl.BlockSpec((None, MEM_LEN, D_MODEL), lambda b: (b, 0, 0)),
                  pl.BlockSpec((D_MODEL, 2 * W_C), lambda b: (0, 0))],
        out_specs=[pl.BlockSpec((None, MEM_PAIRS, SEQ, 128), lambda b: (b, 0, 0, 0)),
                   pl.BlockSpec((None, N_QBLK, 8, 128), lambda b: (b, 0, 0, 0)),
                   pl.BlockSpec((None, MEM_LEN, 2 * W_C), lambda b: (b, 0, 0))],
        out_shape=[jax.ShapeDtypeStruct((B, MEM_PAIRS, SEQ, 128), F32),
                   jax.ShapeDtypeStruct((B, N_QBLK, 8, 128), F32),
                   jax.ShapeDtypeStruct((B, MEM_LEN, 2 * W_C), MXU)],
        scratch_shapes=[pltpu.VMEM((MEM_PAIRS, 2 * MEM_LEN, 128), MXU), pltpu.VMEM((MEM_PAIRS, 2 * MEM_LEN, 128), MXU)],
        compiler_params=_cparams(("arbitrary",)),
    )(qc, mem, w_mem)


def _mem_attn_bwd(qc, mkv, do, lse, delta, mem):
    B = qc.shape[0]

    def body(q_ref, mkv_ref, do_ref, lse_ref, delta_ref, mem_ref, dq_ref, dw_ref, dmkv_ref, k2_ref, v2_ref):
        @pl.when(pl.program_id(0) == 0)
        def _():
            dw_ref[...] = jnp.zeros_like(dw_ref)
        dmkv_ref[...] = jnp.zeros_like(dmkv_ref)
        for c in range(MEM_PAIRS):
            k2_ref[c] = _stack_pair(mkv_ref[:, c * 128:(c + 1) * 128])
            v2_ref[c] = _stack_pair(mkv_ref[:, W_C + c * 128:W_C + (c + 1) * 128])

        def tile(blk):
            rows = pl.ds(pl.multiple_of(blk * BLK, BLK), BLK)
            qcs, docs, sT, dpT = [], [], [], []
            for c in range(MEM_PAIRS):
                qc_, doc = q_ref.at[c][rows, :], do_ref.at[c][rows, :]
                s2, dp2 = _dot_nt(k2_ref[c], qc_), _dot_nt(v2_ref[c], doc)
                sT += [s2[0:MEM_LEN], s2[MEM_LEN:2 * MEM_LEN]]
                dpT += [dp2[0:MEM_LEN], dp2[MEM_LEN:2 * MEM_LEN]]
                qcs.append(qc_)
                docs.append(doc)
            lse_r = _lanes([lse_ref.at[blk][h:h + 1, :] for h in range(MEM_H)])
            delta_r = _lanes([delta_ref.at[blk][h:h + 1, :] for h in range(MEM_H)])
            yield
            pT = jnp.exp(_lanes(sT) - lse_r)
            dsT = pT * (_lanes(dpT) - delta_r)
            dsb, pb = dsT.astype(MXU), pT.astype(MXU)
            yield
            for c in range(MEM_PAIRS):
                q0, q1 = slice(2 * c * BLK, (2 * c + 1) * BLK), slice((2 * c + 1) * BLK, (2 * c + 2) * BLK)
                ds2 = jnp.concatenate([dsb[:, q0], dsb[:, q1]], axis=0)
                p2 = jnp.concatenate([pb[:, q0], pb[:, q1]], axis=0)
                dq_ref.at[c][rows, :] = _dot_t0(ds2, k2_ref[c])
                dmkv_ref[:, c * 128:(c + 1) * 128] += _pair_rows(_dot(ds2, qcs[c]), MEM_LEN)
                dmkv_ref[:, W_C + c * 128:W_C + (c + 1) * 128] += _pair_rows(_dot(p2, docs[c]), MEM_LEN)

        def loop(j, carry):
            _interleave([tile(j * MEM_UNROLL + u) for u in range(MEM_UNROLL)])
            return carry
        lax.fori_loop(0, N_QBLK // MEM_UNROLL, loop, 0)
        dw_ref[...] += _dot_tn(mem_ref[...], dmkv_ref[...].astype(MXU))

    stat = pl.BlockSpec((None, N_QBLK, 8, 128), lambda b: (b, 0, 0, 0))
    pairs = pl.BlockSpec((None, MEM_PAIRS, SEQ, 128), lambda b: (b, 0, 0, 0))
    return pl.pallas_call(
        body, name="mem_attn_bwd", grid=(B,),
        in_specs=[pairs, pl.BlockSpec((None, MEM_LEN, 2 * W_C), lambda b: (b, 0, 0)), pairs, stat, stat,
                  pl.BlockSpec((None, MEM_LEN, D_MODEL), lambda b: (b, 0, 0))],
        out_specs=[pairs, pl.BlockSpec((D_MODEL, 2 * W_C), lambda b: (0, 0))],
        out_shape=[jax.ShapeDtypeStruct((B, MEM_PAIRS, SEQ, 128), F32),
                   jax.ShapeDtypeStruct((D_MODEL, 2 * W_C), F32)],
        scratch_shapes=[pltpu.VMEM((MEM_LEN, 2 * W_C), F32),
                        pltpu.VMEM((MEM_PAIRS, 2 * MEM_LEN, 128), MXU), pltpu.VMEM((MEM_PAIRS, 2 * MEM_LEN, 128), MXU)],
        compiler_params=_cparams(("arbitrary",)),
    )(qc, mkv, do, lse, delta, mem)


def _headsum(t, e):
    if MXU == F32:
        return _dot(t, e)
    hi = t.astype(MXU)
    lo = (t - hi.astype(F32)).astype(MXU)
    return _dot(hi, e) + _dot(lo, e)


def _heads_to_rows(t, e):
    return sum(_dot_nt(e, part) for part in _split3(t))


POST_ROWS = 256


def _post(o_a, olse_b, o_c, z, x2, tgt, g, gain, bias, w_out, hsum, hrows):
    T = x2.shape[0]
    tm = 256
    nt = SEQ // tm

    def body(oa_ref, ob_ref, oc_ref, z_ref, x_ref, t_ref, g_ref, gain_ref, bias_ref, w_ref, e_ref, er_ref,
             gx_ref, doa_ref, dela_ref, dobb_ref, doc_ref, delc_ref, dz_ref, dw_ref, small_ref, loss_ref):
        @pl.when(pl.program_id(0) == 0)
        def _():
            dw_ref[...] = jnp.zeros_like(dw_ref)
            small_ref[...] = jnp.zeros_like(small_ref)
            loss_ref[...] = jnp.zeros_like(loss_ref)

        gg = g_ref[...]
        gain_v = gain_ref[...]
        gain_s = gain_v * (1.0 / D_MODEL)
        bias_v = bias_ref[...]
        w = w_ref[...]

        def rms(o):
            rr = lax.rsqrt(jnp.mean(o * o, axis=1, keepdims=True) + RMS_EPS)
            return o * rr, rr

        def rows_of(rs):
            oa = _lanes([oa_ref.at[c][rs, :] for c in range(SWA_Q // 2)])
            (o1, l1), (o4, l4), (o16, l16) = [
                (_lanes([ob_ref.at[p, 0][rs, :], ob_ref.at[p, 1][rs, :]]),
                 _lanes([ob_ref.at[p, 2][rs, :], ob_ref.at[p, 3][rs, :]])) for p in range(3)]
            mx = jnp.maximum(jnp.maximum(l1, l4), l16)
            e1, e4, e16 = jnp.exp(l1 - mx), jnp.exp(l4 - mx), jnp.exp(l16 - mx)
            den = e1 + e4 + e16
            ob = (e1 * o1 + e4 * o4 + e16 * o16) / den
            lse_b = mx + jnp.log(den)
            oc = _lanes([oc_ref.at[c][rs, :] for c in range(MEM_PAIRS)])
            na, ra = rms(oa)
            nb, rb = rms(ob)
            nc, rc = rms(oc)
            n = jnp.concatenate([na, nb, nc], axis=1)
            zz = z_ref[rs, :]
            sig = 0.5 * jnp.tanh(0.5 * zz) + 0.5
            sz = zz * sig
            gs = gg * sz
            u = n * gs
            r = ALPHA * x_ref[rs, :] + _dot(u.astype(MXU), w)
            rc0 = r - jnp.mean(r, axis=1, keepdims=True)
            rstd = lax.rsqrt(jnp.mean(rc0 * rc0, axis=1, keepdims=True) + LN_EPS)
            xhat = rc0 * rstd
            err = xhat * gain_v + bias_v - t_ref[rs, :]
            dxh = err * gain_s
            dr = rstd * (dxh - jnp.mean(dxh, axis=1, keepdims=True)
                         - xhat * jnp.mean(dxh * xhat, axis=1, keepdims=True))
            gx_ref[rs, :] = ALPHA * dr
            drb = dr.astype(MXU)
            du = _dot_nt(drb, w)
            dun = du * n
            dz = dun * (gg * (sig + sz * (1.0 - sig)))
            dz_ref[rs, :] = dz.astype(MXU)
            dn = du * gs

            def branch(lo, hi, nbr, rr):
                dnb = dn[:, lo:hi]
                return rr * (dnb - nbr * jnp.mean(dnb * nbr, axis=1, keepdims=True))

            def to_kernel(dob, o, do_ref, delta_ref):
                wd = dob.shape[1]
                for c in range(wd // 128):
                    do_ref.at[c][rs, :] = dob[:, c * 128:(c + 1) * 128].astype(do_ref.dtype)
                dT = _heads_to_rows(dob * o, er_ref[:, 0:wd])
                for jb in range((rs.stop - rs.start) // BLK):
                    delta_ref[rs.start // BLK + jb] = dT[0:8, jb * BLK:(jb + 1) * BLK]

            to_kernel(branch(0, W_A, na, ra), oa, doa_ref, dela_ref)
            to_kernel(branch(W_A + W_B, D_MIX, nc, rc), oc, doc_ref, delc_ref)
            dob = branch(W_A, W_A + W_B, nb, rb)
            for j, t in enumerate((dob, lse_b, _headsum(dob * ob, e_ref[...]))):
                for c in range(W_B // 128):
                    dobb_ref.at[j * (W_B // 128) + c][rs, :] = t[:, c * 128:(c + 1) * 128]
            csum = lambda t: jnp.sum(t, axis=0, keepdims=True)
            return (u, drb, jnp.sum(err * err), csum(err * xhat), csum(err), csum(dun * sz), csum(dz))

        parts = [rows_of(slice(k * POST_ROWS, (k + 1) * POST_ROWS)) for k in range(tm // POST_ROWS)]
        tot = [sum(p[i] for p in parts) for i in range(2, 7)]
        dw_ref[...] += _dot_tn(jnp.concatenate([p[0] for p in parts], axis=0),
                               jnp.concatenate([p[1] for p in parts], axis=0))
        loss_ref[...] += 0.5 * tot[0] * (1.0 / D_MODEL)
        small_ref[0:1, :] += tot[1] * (1.0 / D_MODEL)
        small_ref[1:2, :] += tot[2] * (1.0 / D_MODEL)
        small_ref[2:3, :] += tot[3]
        small_ref[3:4, :] += tot[4]

    B = T // SEQ
    row = lambda w: pl.BlockSpec((tm, w), lambda i: (i, 0))
    full = lambda a, b: pl.BlockSpec((a, b), lambda i: (0, 0))
    chunked = lambda n: pl.BlockSpec((None, n, tm, 128), lambda i: (i // nt, 0, i % nt, 0))
    stat = pl.BlockSpec((None, tm // BLK, 8, 128), lambda i: (i // nt, i % nt, 0, 0))
    return pl.pallas_call(
        body, name="post_fwd_bwd", grid=(T // tm,),
        in_specs=[chunked(SWA_Q // 2), pl.BlockSpec((None, 3, 4, tm, 128), lambda i: (i // nt, 0, 0, i % nt, 0)),
                  chunked(MEM_PAIRS),
                  row(D_MIX), row(D_MODEL), row(D_MODEL),
                  full(1, D_MIX), full(1, D_MODEL), full(1, D_MODEL), full(D_MIX, D_MODEL), full(W_B, W_B),
                  full(PICK_ROWS, W_A)],
        out_specs=[row(D_MODEL), chunked(SWA_Q // 2), stat, chunked(6), chunked(MEM_PAIRS), stat, row(D_MIX),
                   full(D_MIX, D_MODEL), full(8, D_MODEL), full(8, 128)],
        out_shape=[jax.ShapeDtypeStruct((T, D_MODEL), F32),
                   jax.ShapeDtypeStruct((B, SWA_Q // 2, SEQ, 128), MXU),
                   jax.ShapeDtypeStruct((B, N_QBLK, 8, 128), F32),
                   jax.ShapeDtypeStruct((B, 6, SEQ, 128), F32),
                   jax.ShapeDtypeStruct((B, MEM_PAIRS, SEQ, 128), MXU),
                   jax.ShapeDtypeStruct((B, N_QBLK, 8, 128), F32),
                   jax.ShapeDtypeStruct((T, D_MIX), MXU),
                   jax.ShapeDtypeStruct((D_MIX, D_MODEL), F32),
                   jax.ShapeDtypeStruct((8, D_MODEL), F32),
                   jax.ShapeDtypeStruct((8, 128), F32)],
        compiler_params=_cparams(("arbitrary",)),
    )(o_a, olse_b, o_c, z, x2, tgt, g, gain, bias, w_out, hsum, hrows)


def _dh_build(dqkva, dqkvb, dqc, dz, tab):
    T = dz.shape[0]
    tm = 512
    nt = SEQ // tm
    HQ = D_IN - D_MIX

    def body(da_ref, db6_ref, dqc_ref, dz_ref, tab_ref, dh_ref, db_ref):
        @pl.when(pl.program_id(0) == 0)
        def _():
            db_ref[...] = jnp.zeros_like(db_ref)
        tab = tab_ref[...]
        lo = lax.broadcasted_iota(jnp.int32, (tm, 128), 1) < HEAD

        def kv_grad(c):
            g0, g1 = da_ref[c], da_ref[c + 1]
            return jnp.where(lo, g0 + pltpu.roll(g0, HEAD, 1), g1 + pltpu.roll(g1, HEAD, 1))

        parts = [_rope(_lanes([da_ref[c] for c in range(SWA_Q // 2)]), tab, -1) * Q_SCALE,
                 _rope(kv_grad(4), tab, -1),
                 kv_grad(6),
                 _rope(_lanes([db6_ref[0], db6_ref[1]]), tab, -1) * Q_SCALE,
                 _rope(_lanes([db6_ref[2], db6_ref[3]]), tab, -1),
                 _lanes([db6_ref[4], db6_ref[5]]),
                 _lanes([dqc_ref[c] for c in range(MEM_PAIRS)]) * Q_SCALE]
        dhq = jnp.concatenate(parts, axis=1)
        db_ref[0:1, :] += jnp.sum(dhq, axis=0, keepdims=True)
        dh_ref[:, 0:HQ] = dhq.astype(MXU)
        dh_ref[:, HQ:D_IN] = dz_ref[...]

    row = lambda w: pl.BlockSpec((tm, w), lambda i: (i, 0))
    chunked = lambda n: pl.BlockSpec((None, n, tm, 128), lambda i: (i // nt, 0, i % nt, 0))
    return pl.pallas_call(
        body, name="dh_build", grid=(T // tm,),
        in_specs=[chunked(SWA_CHUNKS), chunked(6), chunked(MEM_PAIRS), row(D_MIX),
                  pl.BlockSpec((tm, 384), lambda i: (i % nt, 0))],
        out_specs=[row(D_IN), pl.BlockSpec((8, HQ), lambda i: (0, 0))],
        out_shape=[jax.ShapeDtypeStruct((T, D_IN), MXU), jax.ShapeDtypeStruct((8, HQ), F32)],
        compiler_params=_cparams(("arbitrary",)),
    )(dqkva, dqkvb, dqc, dz, tab)


TAIL_TK = 512
TAIL_TN = D_IN // 2
TAIL_TM = 256
REDUCE_ROWS = 128


def _tail(xt, dh, gx1, w_in, small_g):
    T = xt.shape[1]
    kt = T // TAIL_TK
    ndw = (D_IN // TAIL_TN) * kt
    nsteps = ndw + T // TAIL_TM
    n_pass = D_IN // TAIL_TN
    assert n_pass == 2 and TAIL_TN == 4 * COLS_PER_DEV and kt >= 2
    pay = dh.dtype
    blk_shape = (D_MODEL, COLS_PER_DEV)
    n_half = 2 * n_pass
    n_chip = N_DEV // 2

    def body(xt_ref, dh1_ref, dh2_ref, gx_ref, w_hbm, sg_ref, dx_ref, gin_ref, gsm_ref,
             acc_ref, w_ref, mine_ref, stagea_ref, landa_ref, stageb_ref, landb_ref, own_ref, lsm_ref,
             sa_sems, ra_sems, sb_sems, rb_sems, ss_sems, rs_sems, w_sem):
        s = pl.program_id(0)
        x, y, c = _my_pos()
        me = 4 * x + 2 * y + c
        chip = 2 * x + y

        def to_sibling(q):
            return pltpu.make_async_remote_copy(
                src_ref=stagea_ref.at[q], dst_ref=landa_ref.at[q], send_sem=sa_sems.at[q], recv_sem=ra_sems.at[q],
                device_id=(x, y, 1 - c), device_id_type=MESH)

        def to_owner(q):
            return pltpu.make_async_remote_copy(
                src_ref=stageb_ref.at[q], dst_ref=landb_ref.at[chip], send_sem=sb_sems.at[q],
                recv_sem=rb_sems.at[chip], device_id=(q // 2, q % 2, c), device_id_type=MESH)

        def from_chip(m):
            return pltpu.make_async_remote_copy(
                src_ref=landb_ref.at[m], dst_ref=landb_ref.at[m], send_sem=sb_sems.at[m], recv_sem=rb_sems.at[m],
                device_id=(m // 2, m % 2, c), device_id_type=MESH)

        def is_me(q):
            return (x == q // 2) & (y == q % 2)

        def small_to(j):
            return pltpu.make_async_remote_copy(
                src_ref=sg_ref, dst_ref=lsm_ref.at[me], send_sem=ss_sems.at[j], recv_sem=rs_sems.at[me],
                device_id=_dev_coords(j), device_id_type=MESH)

        def small_from(m):
            return pltpu.make_async_remote_copy(
                src_ref=lsm_ref.at[m], dst_ref=lsm_ref.at[m], send_sem=ss_sems.at[m], recv_sem=rs_sems.at[m],
                device_id=_dev_coords(m), device_id_type=MESH)

        w_copy = pltpu.make_async_copy(w_hbm, w_ref, w_sem)

        @pl.when(s == 0)
        def _():
            w_copy.start()
            for j in range(N_DEV):
                pl.when(me != j)(small_to(j).start)
            lsm_ref[me] = sg_ref[...]
            landb_ref[chip] = jnp.zeros(blk_shape, pay)

        @pl.when(s < ndw)
        def _():
            @pl.when(s % kt == 0)
            def _():
                acc_ref[...] = jnp.zeros_like(acc_ref)
            acc_ref[...] += _dot(xt_ref[...], dh1_ref[...])

        for p in range(n_pass):
            @pl.when(s == p * kt + kt - 1)
            def _(p=p):
                for cc in range(2):
                    @pl.when(c == cc)
                    def _(cc=cc):
                        for yo in range(2):
                            q = 2 * p + yo
                            same, other = 2 * yo + cc, 2 * yo + 1 - cc
                            mine_ref[q] = acc_ref[:, same * COLS_PER_DEV:(same + 1) * COLS_PER_DEV]
                            stagea_ref[q] = acc_ref[:, other * COLS_PER_DEV:(other + 1) * COLS_PER_DEV].astype(pay)
                for yo in range(2):
                    to_sibling(2 * p + yo).start()

            @pl.when(s == (p + 1) * kt + 1)
            def _(p=p):
                for yo in range(2):
                    q = 2 * p + yo
                    to_sibling(q).wait_recv()

                    def chunk(i, carry, q=q):
                        rs = pl.ds(pl.multiple_of(i * REDUCE_ROWS, REDUCE_ROWS), REDUCE_ROWS)
                        tot = mine_ref[q, rs, :] + landa_ref[q, rs, :].astype(F32)

                        @pl.when(is_me(q))
                        def _():
                            own_ref[rs, :] = tot

                        @pl.when(jnp.logical_not(is_me(q)))
                        def _():
                            stageb_ref[q, rs, :] = tot.astype(pay)
                        return carry
                    lax.fori_loop(0, D_MODEL // REDUCE_ROWS, chunk, 0)
                    pl.when(jnp.logical_not(is_me(q)))(to_owner(q).start)

        @pl.when(s >= ndw)
        def _():
            pl.when(s == ndw)(w_copy.wait)
            dx_ref[...] = _dot_nt(dh2_ref[...], w_ref[...]) + gx_ref[...]

        @pl.when(s == nsteps - 1)
        def _():
            for m in range(n_chip):
                pl.when(m != chip)(from_chip(m).wait_recv)
            for m in range(N_DEV):
                pl.when(me != m)(small_from(m).wait_recv)
            for q in range(n_half):
                to_sibling(q).wait_send()
                pl.when(jnp.logical_not(is_me(q)))(to_owner(q).wait_send)
            for j in range(N_DEV):
                pl.when(me != j)(small_to(j).wait_send)

            def chunk(i, carry):
                rs = pl.ds(pl.multiple_of(i * REDUCE_ROWS, REDUCE_ROWS), REDUCE_ROWS)
                g = own_ref[rs, :]
                for m in range(n_chip):
                    g = g + landb_ref[m, rs, :].astype(F32)
                gin_ref[rs, :] = g
                return carry
            lax.fori_loop(0, D_MODEL // REDUCE_ROWS, chunk, 0)
            g = lsm_ref[0]
            for m in range(1, N_DEV):
                g = g + lsm_ref[m]
            gsm_ref[...] = g

    dw_step = lambda s: jnp.minimum(s, ndw - 1)
    dx_step = lambda s: jnp.maximum(s - ndw, 0)
    any_spec = pl.BlockSpec(memory_space=pl.ANY)
    vmem = pl.BlockSpec(memory_space=pltpu.VMEM)
    dma = pltpu.SemaphoreType.DMA
    scratch = [pltpu.VMEM((D_MODEL, TAIL_TN), F32), pltpu.VMEM((D_MODEL, D_IN), w_in.dtype),
               pltpu.VMEM((n_half,) + blk_shape, F32),
               pltpu.VMEM((n_half,) + blk_shape, pay), pltpu.VMEM((n_half,) + blk_shape, pay),
               pltpu.VMEM((n_half,) + blk_shape, pay), pltpu.VMEM((n_chip,) + blk_shape, pay),
               pltpu.VMEM(blk_shape, F32), pltpu.VMEM((N_DEV,) + small_g.shape, F32),
               dma((n_half,)), dma((n_half,)), dma((n_half,)), dma((n_chip,)), dma((N_DEV,)), dma((N_DEV,)), dma]
    return pl.pallas_call(
        body, name="tail_dw_dx_reduce", grid=(nsteps,),
        in_specs=[pl.BlockSpec((D_MODEL, TAIL_TK), lambda s: (0, dw_step(s) % kt)),
                  pl.BlockSpec((TAIL_TK, TAIL_TN), lambda s: (dw_step(s) % kt, dw_step(s) // kt)),
                  pl.BlockSpec((TAIL_TM, D_IN), lambda s: (dx_step(s), 0)),
                  pl.BlockSpec((TAIL_TM, D_MODEL), lambda s: (dx_step(s), 0)),
                  any_spec, vmem],
        out_specs=[pl.BlockSpec((TAIL_TM, D_MODEL), lambda s: (dx_step(s), 0)), vmem, vmem],
        out_shape=[jax.ShapeDtypeStruct((T, D_MODEL), F32), jax.ShapeDtypeStruct(blk_shape, F32),
                   jax.ShapeDtypeStruct(small_g.shape, F32)],
        scratch_shapes=scratch,
        compiler_params=_cparams(("arbitrary",)),
    )(xt, dh, dh, gx1, w_in, small_g)


def _adam_update(grads, params, carried):
    n = len(grads)

    def body(*refs):
        g_refs, p_refs, o_refs = refs[1:1 + n], refs[1 + n:1 + 4 * n], refs[2 + 4 * n:]
        for a in range(n):
            rows = g_refs[a].shape[0]
            cr = REDUCE_ROWS if rows % REDUCE_ROWS == 0 else rows
            w_ref, m_ref, v_ref = p_refs[3 * a:3 * a + 3]
            go_ref, d_ref, nm_ref, nv_ref = o_refs[4 * a:4 * a + 4]

            def chunk(i, carry, cr=cr, g_ref=g_refs[a], w_ref=w_ref, m_ref=m_ref, v_ref=v_ref,
                      go_ref=go_ref, d_ref=d_ref, nm_ref=nm_ref, nv_ref=nv_ref):
                rs = pl.ds(pl.multiple_of(i * cr, cr), cr)
                g = g_ref[rs, :]
                go_ref[rs, :] = g
                d_ref[rs, :], nm_ref[rs, :], nv_ref[rs, :] = _adamw(w_ref[rs, :], g, m_ref[rs, :], v_ref[rs, :])
                return carry
            lax.fori_loop(0, rows // cr, chunk, 0)

    vmem = pl.BlockSpec(memory_space=pltpu.VMEM)
    any_spec = pl.BlockSpec(memory_space=pl.ANY)
    flat = [p for grp in params for p in grp]
    outs = pl.pallas_call(
        body, name="adamw", in_specs=[any_spec] + [vmem] * (4 * n), out_specs=[any_spec] + [vmem] * (4 * n),
        out_shape=[jax.ShapeDtypeStruct(carried.shape, carried.dtype)]
        + [jax.ShapeDtypeStruct(g.shape, F32) for g in grads for _ in range(4)],
        input_output_aliases={0: 0},
        compiler_params=pltpu.CompilerParams(vmem_limit_bytes=VMEM_LIMIT),
    )(carried, *grads, *flat)
    return [outs[1 + 4 * a:5 + 4 * a] for a in range(n)], outs[0]


def _step(x, mem, w_in_s, w_mem_s, w_out_s, b_in, sinks, g, gain, bias, tgt):
    B = x.shape[0]
    T = B * SEQ
    x2 = x.reshape(T, D_MODEL)
    t2 = tgt.reshape(T, D_MODEL)
    tab = _rope_table()
    lane = jnp.arange(W_A)
    hsum = (lane[:W_B, None] // HEAD == lane[None, :W_B] // HEAD).astype(MXU)
    hrows = (jnp.arange(PICK_ROWS)[:, None] == lane[None, :] // HEAD).astype(MXU)
    me = 4 * lax.axis_index("x") + 2 * lax.axis_index("y") + lax.axis_index("c")

    (w_in_all,) = _gather_weights([w_in_s])
    qkva, qkvb, qc, z, w_in, xt, w_mem_all, w_out_all = _in_proj(x2, w_in_all, b_in, tab, [w_mem_s, w_out_s])
    w_mem = w_mem_all.reshape(D_MODEL, 2 * W_C)
    w_out = w_out_all.reshape(D_MIX, D_MODEL)

    o_a, lse_a = _swa_fwd(qkva, sinks)
    olse_b = _dil_fwd(qkvb)
    o_c, lse_c, mkv = _mem_attn_fwd(qc, mem, w_mem)

    gx1, do_a, delta_a, dobb, do_c, delta_c, dz, dw_out, small, loss = _post(
        o_a, olse_b, o_c, z, x2, t2, g, gain, bias, w_out, hsum, hrows)

    dqc, dw_mem = _mem_attn_bwd(qc, mkv, do_c, lse_c, delta_c, mem)
    blocks = [dw_mem.reshape(N_DEV, ROWS_PER_DEV, 2 * W_C), dw_out.reshape(N_DEV, ROWS_PER_DEV, D_MODEL)]
    sends = [b.astype(MXU) for b in blocks]
    owns = [lax.dynamic_index_in_dim(b, me, axis=0, keepdims=False) for b in blocks]
    dqkvb, (g_mem, g_out) = _dil_bwd(qkvb, dobb, sends, owns)
    dqkva, dsink = _swa_bwd(qkva, do_a, lse_a, delta_a, sinks)
    dh, dbq = _dh_build(dqkva, dqkvb, dqc, dz, tab)

    small_g = _pack_small(dict(b_in=jnp.concatenate([dbq[0], small[3]]), sinks=dsink[:, 0], g=small[2],
                               gain=small[0], bias=small[1], loss=loss[0, 0]))
    grad_x, g_in, g_small = _tail(xt, dh, gx1, w_in, small_g)
    return grad_x.reshape(B, SEQ, D_MODEL), g_in, g_mem, g_out, g_small


def _my_pos():
    return lax.axis_index("x"), lax.axis_index("y"), lax.axis_index("c")


def _gather_weights(shards):
    n_arr = len(shards)

    def body(*refs):
        ins, outs = refs[0:n_arr], refs[n_arr:2 * n_arr]
        send_sems, recv_sems, local_sems = refs[2 * n_arr:]
        x, y, c = _my_pos()
        me, sibling = (x, y, c), (x, y, 1 - c)
        chips = [(1 - x, y), (x, 1 - y), (1 - x, 1 - y)]

        def slot(a, pos):
            return outs[a].at[4 * pos[0] + 2 * pos[1] + pos[2]]

        def copy(a, k, block, to, src=None):
            return pltpu.make_async_remote_copy(
                src_ref=slot(a, block) if src is None else src, dst_ref=slot(a, block),
                send_sem=send_sems.at[a, k], recv_sem=recv_sems.at[a, k],
                device_id=to, device_id_type=MESH)

        mine = [pltpu.make_async_copy(ins[a], slot(a, me), local_sems.at[a]) for a in range(n_arr)]
        for cp in mine:
            cp.start()
        first = []
        for a in range(n_arr):
            first.append(copy(a, 0, me, sibling, src=ins[a]))
            first += [copy(a, 1 + j, me, (*chip, c), src=ins[a]) for j, chip in enumerate(chips)]
        for cp in first:
            cp.start()
        passed = []
        for j, chip in enumerate(chips):
            for a in range(n_arr):
                copy(a, 1 + j, (*chip, c), me).wait_recv()
                fwd = copy(a, 4 + j, (*chip, c), sibling)
                fwd.start()
                passed.append(fwd)
        for a in range(n_arr):
            copy(a, 0, sibling, me).wait_recv()
            for j, chip in enumerate(chips):
                copy(a, 4 + j, (*chip, 1 - c), me).wait_recv()
        for cp in first + passed:
            cp.wait_send()
        for cp in mine:
            cp.wait()

    any_spec = pl.BlockSpec(memory_space=pl.ANY)
    return pl.pallas_call(
        body, name="gather_weights",
        in_specs=[any_spec] * n_arr, out_specs=[any_spec] * n_arr,
        out_shape=[jax.ShapeDtypeStruct((N_DEV,) + s.shape, s.dtype) for s in shards],
        scratch_shapes=[pltpu.SemaphoreType.DMA((n_arr, 7)), pltpu.SemaphoreType.DMA((n_arr, 7)),
                        pltpu.SemaphoreType.DMA((n_arr,))],
    )(*shards)


def _adamw(w, g, m, v):
    m = ADAM_B1 * m + (1.0 - ADAM_B1) * g
    v = ADAM_B2 * v + (1.0 - ADAM_B2) * (g * g)
    m_hat = m / (1.0 - ADAM_B1 ** ADAM_STEP)
    v_hat = v / (1.0 - ADAM_B2 ** ADAM_STEP)
    delta = -ADAM_LR * (m_hat / (jnp.sqrt(v_hat) + ADAM_EPS) + ADAM_WD * w)
    return delta, m, v


_SMALL_SIZES = (("b_in", D_IN), ("g", D_MIX), ("gain", D_MODEL), ("bias", D_MODEL), ("sinks", SWA_Q), ("loss", 1))


def _pack_small(d):
    flat = jnp.concatenate([jnp.reshape(d[k], (-1,)).astype(F32) if k in d else jnp.zeros((n,), F32)
                            for k, n in _SMALL_SIZES])
    flat = jnp.pad(flat, (0, SMALL_ROWS * 128 - flat.shape[0]))
    return flat.reshape(SMALL_ROWS, 128)


def _unpack_small(p):
    flat = p.reshape(-1)
    out, off = {}, 0
    for k, n in _SMALL_SIZES:
        out[k] = flat[off:off + n].reshape(1, n)
        off += n
    return out


def kernel(x, mem, w_in, b_in, w_mem, attn_sinks, g_branch, w_out, ln_gain, ln_bias, loss_target, m_w_in, m_b_in, m_w_mem, m_attn_sinks, m_g_branch, m_w_out, m_ln_gain, m_ln_bias, v_w_in, v_b_in, v_w_mem, v_attn_sinks, v_g_branch, v_w_out, v_ln_gain, v_ln_bias):
    grad_x, g_in, g_mem, g_out, g_small = _step(
        x, mem, w_in[0].astype(MXU), w_mem[0].astype(MXU), w_out[0].astype(MXU), b_in, attn_sinks[0],
        g_branch, ln_gain, ln_bias, loss_target)

    small_w = _pack_small(dict(b_in=b_in, g=g_branch, gain=ln_gain, bias=ln_bias, sinks=attn_sinks))
    small_m = _pack_small(dict(b_in=m_b_in, g=m_g_branch, gain=m_ln_gain, bias=m_ln_bias, sinks=m_attn_sinks))
    small_v = _pack_small(dict(b_in=v_b_in, g=v_g_branch, gain=v_ln_gain, bias=v_ln_bias, sinks=v_attn_sinks))
    grads = [g_in, g_mem, g_out, g_small]
    params = [(w_in[0], m_w_in[0], v_w_in[0]), (w_mem[0], m_w_mem[0], v_w_mem[0]),
              (w_out[0], m_w_out[0], v_w_out[0]), (small_w, small_m, small_v)]
    res, grad_x = _adam_update(grads, params, grad_x)
    big = [[r[None] for r in res[a]] for a in range(3)]
    sm = [_unpack_small(r) for r in res[3]]

    def group(i):
        return (big[0][i], sm[i]["b_in"], big[1][i], sm[i]["sinks"], sm[i]["g"], big[2][i],
                sm[i]["gain"], sm[i]["bias"])

    loss = sm[0]["loss"].reshape(())
    return (loss, grad_x, *group(0), *group(1), *group(2), *group(3))
```

```python
import functools
import math

import jax
import jax.numpy as jnp
from jax import lax
from jax.experimental import pallas as pl
from jax.experimental.pallas import tpu as pltpu

F32 = jnp.float32
MXU = jnp.bfloat16

D_MODEL = 1024
SEQ = 2048
HEAD = 64
BLK = 128
SWA_Q, SWA_KV = 8, 2
DIL_H = 4
MEM_H = 4
MEM_LEN = 256
W_A, W_KVA, W_B, W_C = 512, 128, 256, 256
D_MIX = 1024
D_IN = 2816
N_DEV = 8
COLS_PER_DEV = D_IN // N_DEV
ROWS_PER_DEV = D_MODEL // N_DEV
ROPE_THETA = 10000.0
LN_EPS = 1e-5
RMS_EPS = 1e-6
ALPHA = 2.0 ** 0.25
Q_SCALE = HEAD ** -0.5
NEG = -1e30
SMALL_ROWS = 48
VMEM_LIMIT = 56 * 1024 * 1024

ADAM_LR = 0.001
ADAM_B1 = 0.9
ADAM_B2 = 0.999
ADAM_EPS = 1e-08
ADAM_WD = 0.01
ADAM_STEP = 10

MESH = pl.DeviceIdType.MESH


def _cparams(sem=None):
    return pltpu.CompilerParams(dimension_semantics=sem, vmem_limit_bytes=VMEM_LIMIT)


def _dot(a, b):
    return jnp.dot(a, b, preferred_element_type=F32)


def _dot_nt(a, b):
    return lax.dot_general(a, b, (((1,), (1,)), ((), ())), preferred_element_type=F32)


def _dot_t0(a, b):
    return lax.dot_general(a, b, (((0,), (0,)), ((), ())), preferred_element_type=F32)


def _dot_tn(a, b):
    return jnp.dot(a.T.astype(MXU), b, preferred_element_type=F32)


def _rope(t, tab, sign):
    cos, sa, sb = tab
    outs = []
    for c in range(t.shape[1] // 128):
        tc = t[:, c * 128:(c + 1) * 128]
        r = pltpu.roll(tc, 96, 1) * sa + pltpu.roll(tc, 32, 1) * sb
        outs.append(tc * cos + r if sign > 0 else tc * cos - r)
    return outs[0] if len(outs) == 1 else jnp.concatenate(outs, axis=1)


def _rope_inv():
    inv = ROPE_THETA ** (-jnp.arange(0, HEAD, 2, dtype=F32) / HEAD)
    return jnp.tile(inv, 2 * 128 // HEAD)[None, :]


def _rope_tab(pos0, rows, inv):
    pos = (lax.broadcasted_iota(jnp.int32, (rows, 128), 0) + pos0).astype(F32)
    ang = pos * inv
    cos, sin = jnp.cos(ang), jnp.sin(ang)
    first = lax.broadcasted_iota(jnp.int32, (rows, 128), 1) % HEAD < HEAD // 2
    return cos, jnp.where(first, -sin, 0.0), jnp.where(first, 0.0, sin)


def _dev_coords(j):
    return (j >> 2, (j >> 1) & 1, j & 1)


def _in_proj(x2, w_all, b_in, tab, late_shards):
    T = x2.shape[0]
    tm = 512
    n_late = len(late_shards)

    def body(x_ref, wall_ref, b_ref, tab_ref, *rest):
        late_in, rest = rest[:n_late], rest[n_late:]
        qkva_ref, qkvb_ref, qc_ref, z_ref, w_ref, xt_ref = rest[:6]
        late_out = rest[6:6 + n_late]
        send_sems, recv_sems, local_sems = rest[6 + n_late:]
        step, last = pl.program_id(0), pl.num_programs(0) - 1
        x, y, c = _my_pos()
        me = 4 * x + 2 * y + c

        def to_peer(a, j):
            return pltpu.make_async_remote_copy(
                src_ref=late_in[a], dst_ref=late_out[a].at[me], send_sem=send_sems.at[a, j],
                recv_sem=recv_sems.at[a, me], device_id=_dev_coords(j), device_id_type=MESH)

        def from_peer(a, m):
            return pltpu.make_async_remote_copy(
                src_ref=late_out[a].at[m], dst_ref=late_out[a].at[m], send_sem=send_sems.at[a, m],
                recv_sem=recv_sems.at[a, m], device_id=_dev_coords(m), device_id_type=MESH)

        def mine(a):
            return pltpu.make_async_copy(late_in[a], late_out[a].at[me], local_sems.at[a])

        @pl.when(step == 0)
        def _():
            for a in range(n_late):
                mine(a).start()
                for j in range(N_DEV):
                    pl.when(me != j)(to_peer(a, j).start)
            for j in range(N_DEV):
                w_ref[:, j * COLS_PER_DEV:(j + 1) * COLS_PER_DEV] = wall_ref[j]

        xb = x_ref[...].astype(MXU)
        xt_ref[...] = x_ref[...].T.astype(MXU)
        tab = _rope_tab((step % nt) * tm, tm, tab_ref[...])

        def seg(c0, c1):
            return _dot(xb, w_ref[:, c0:c1]) + b_ref[:, c0:c1]

        qa = (_rope(seg(0, 512), tab, 1) * Q_SCALE).astype(MXU)
        for c in range(SWA_Q // 2):
            qkva_ref[c] = qa[:, c * 128:(c + 1) * 128]
        lo = lax.broadcasted_iota(jnp.int32, (tm, 128), 1) < HEAD
        for j, t in enumerate((_rope(seg(512, 640), tab, 1), seg(640, 768))):
            other = pltpu.roll(t, HEAD, 1)
            qkva_ref[4 + 2 * j] = jnp.where(lo, t, other).astype(MXU)
            qkva_ref[5 + 2 * j] = jnp.where(lo, other, t).astype(MXU)
        qkvb = (_rope(seg(768, 1024), tab, 1) * Q_SCALE, _rope(seg(1024, 1280), tab, 1), seg(1280, 1536))
        for j, t in enumerate(qkvb):
            for c in range(2):
                qkvb_ref[2 * j + c] = t[:, c * 128:(c + 1) * 128]
        qc = (seg(1536, 1792) * Q_SCALE).astype(MXU)
        for c in range(MEM_H // 2):
            qc_ref[c] = qc[:, c * 128:(c + 1) * 128]
        z_ref[...] = seg(1792, 2816)

        @pl.when(step == last)
        def _():
            for a in range(n_late):
                mine(a).wait()
                for m in range(N_DEV):
                    pl.when(me != m)(from_peer(a, m).wait_recv)
                for j in range(N_DEV):
                    pl.when(me != j)(to_peer(a, j).wait_send)

    nt = SEQ // tm
    any_spec = pl.BlockSpec(memory_space=pl.ANY)
    chunked = lambda n: pl.BlockSpec((None, n, tm, 128), lambda i: (i // nt, 0, i % nt, 0))
    return pl.pallas_call(
        body, name="in_proj_fwd",
        grid=(T // tm,),
        in_specs=[pl.BlockSpec((tm, D_MODEL), lambda i: (i, 0)),
                  pl.BlockSpec((N_DEV, D_MODEL, COLS_PER_DEV), lambda i: (0, 0, 0)),
                  pl.BlockSpec((1, D_IN), lambda i: (0, 0)),
                  pl.BlockSpec((1, 128), lambda i: (0, 0))] + [any_spec] * n_late,
        out_specs=[chunked(SWA_CHUNKS), chunked(6), chunked(MEM_H // 2),
                   pl.BlockSpec((tm, D_MIX), lambda i: (i, 0)),
                   pl.BlockSpec((D_MODEL, D_IN), lambda i: (0, 0)),
                   pl.BlockSpec((D_MODEL, tm), lambda i: (0, i))] + [any_spec] * n_late,
        out_shape=[jax.ShapeDtypeStruct((T // SEQ, SWA_CHUNKS, SEQ, 128), MXU),
                   jax.ShapeDtypeStruct((T // SEQ, 6, SEQ, 128), F32),
                   jax.ShapeDtypeStruct((T // SEQ, MEM_H // 2, SEQ, 128), MXU),
                   jax.ShapeDtypeStruct((T, D_MIX), F32),
                   jax.ShapeDtypeStruct((D_MODEL, D_IN), w_all.dtype),
                   jax.ShapeDtypeStruct((D_MODEL, T), MXU)]
        + [jax.ShapeDtypeStruct((N_DEV,) + s.shape, s.dtype) for s in late_shards],
        scratch_shapes=[pltpu.SemaphoreType.DMA((n_late, N_DEV)), pltpu.SemaphoreType.DMA((n_late, N_DEV)),
                        pltpu.SemaphoreType.DMA((n_late,))],
        compiler_params=_cparams(("arbitrary",)),
    )(x2, w_all, b_in, tab, *late_shards)


CHAIN = 4


def _band_bias(max_dist):
    kj = lax.broadcasted_iota(jnp.int32, (2 * BLK, BLK), 0)
    qi = lax.broadcasted_iota(jnp.int32, (2 * BLK, BLK), 1)
    dist = qi + BLK - kj
    band = jnp.where((dist >= 0) & (dist <= max_dist), 0.0, NEG).astype(F32)
    k1 = lax.broadcasted_iota(jnp.int32, (BLK, BLK), 0)
    q1 = lax.broadcasted_iota(jnp.int32, (BLK, BLK), 1)
    first = jnp.where((q1 - k1 >= 0) & (q1 - k1 <= max_dist), 0.0, NEG).astype(F32)
    return jnp.concatenate([band] * CHAIN, axis=1), jnp.concatenate([first] * CHAIN, axis=1)


def _lanes(parts):
    return jnp.concatenate(parts, axis=1)


PICK_ROWS = 16


def _stack_pair(t):
    lo = (lax.broadcasted_iota(jnp.int32, t.shape, 1) < HEAD).astype(F32)
    return jnp.concatenate([t * lo, t * (1.0 - lo)], axis=0).astype(MXU)


def _pair_rows(x, n):
    lo = lax.broadcasted_iota(jnp.int32, (n, 128), 1) < HEAD
    return jnp.where(lo, x[0:n], x[n:2 * n])


def _split3(t):
    if MXU == F32:
        return (t,)
    hi = t.astype(MXU)
    r = t - hi.astype(F32)
    mid = r.astype(MXU)
    return hi, mid, (r - mid.astype(F32)).astype(MXU)


def _interleave(tiles):
    tiles = list(tiles)
    while tiles:
        for t in list(tiles):
            try:
                next(t)
            except StopIteration:
                tiles.remove(t)


def _softmax_cols(sT, sinkrow=None):
    m = jnp.max(sT, axis=0, keepdims=True)
    if sinkrow is not None:
        m = jnp.maximum(m, sinkrow)
    pT = jnp.exp(sT - m)
    l = jnp.sum(pT, axis=0, keepdims=True)
    if sinkrow is not None:
        l = l + jnp.exp(sinkrow - m)
    return (pT * (1.0 / l)).astype(MXU), m + jnp.log(l)


SWA_CHUNKS = 8
SWA_UNROLL = 3
N_QBLK = SEQ // BLK


def _swa_fwd(qkva, sinks):
    B = qkva.shape[0]
    G = SWA_Q // SWA_KV

    def body(sink_ref, qkv_ref, o_ref, lse_ref):
        band, first = _band_bias(BLK - 1)
        sinkrows = [_lanes([jnp.full((1, BLK), sink_ref[G * hk + j], F32) for j in range(G)])
                    for hk in range(SWA_KV)]

        def tile(hk, blk, rows_q, rows_k, bias):
            nk = bias.shape[0]
            k2 = _stack_pair(qkv_ref.at[4 + hk][rows_k, :])
            sT = []
            for c in (2 * hk, 2 * hk + 1):
                s2 = _dot_nt(k2, qkv_ref.at[c][rows_q, :])
                sT += [s2[0:nk], s2[nk:2 * nk]]
            yield
            pnT, lse = _softmax_cols(_lanes(sT) + bias, sinkrows[hk])
            yield
            v2 = _stack_pair(qkv_ref.at[6 + hk][rows_k, :])
            for j, c in enumerate((2 * hk, 2 * hk + 1)):
                p2 = jnp.concatenate([pnT[:, 2 * j * BLK:(2 * j + 1) * BLK],
                                      pnT[:, (2 * j + 1) * BLK:(2 * j + 2) * BLK]], axis=0)
                o_ref.at[c][rows_q, :] = _dot_t0(p2, v2)
            for j in range(G):
                lse_ref.at[blk][G * hk + j:G * hk + j + 1, :] = lse[:, j * BLK:(j + 1) * BLK]

        def tiles_at(i):
            r0 = pl.multiple_of(i * BLK, BLK)
            rk = pl.multiple_of(i * BLK - BLK, BLK)
            return [tile(hk, i, pl.ds(r0, BLK), pl.ds(rk, 2 * BLK), band) for hk in range(SWA_KV)]

        _interleave([tile(hk, 0, pl.ds(0, BLK), pl.ds(0, BLK), first) for hk in range(SWA_KV)])

        def loop(j, carry):
            _interleave([t for u in range(SWA_UNROLL) for t in tiles_at(1 + j * SWA_UNROLL + u)])
            return carry
        lax.fori_loop(0, (N_QBLK - 1) // SWA_UNROLL, loop, 0)

    return pl.pallas_call(
        body, name="swa_fwd", grid=(B,),
        in_specs=[pl.BlockSpec(memory_space=pltpu.SMEM),
                  pl.BlockSpec((None, SWA_CHUNKS, SEQ, 128), lambda b: (b, 0, 0, 0))],
        out_specs=[pl.BlockSpec((None, SWA_Q // 2, SEQ, 128), lambda b: (b, 0, 0, 0)),
                   pl.BlockSpec((None, N_QBLK, 8, 128), lambda b: (b, 0, 0, 0))],
        out_shape=[jax.ShapeDtypeStruct((B, SWA_Q // 2, SEQ, 128), F32),
                   jax.ShapeDtypeStruct((B, N_QBLK, 8, 128), F32)],
        compiler_params=_cparams(("arbitrary",)),
    )(sinks, qkva)


def _swa_bwd(qkva, do, lse, delta, sinks):
    B = qkva.shape[0]
    G = SWA_Q // SWA_KV

    def body(sink_ref, qkv_ref, do_ref, lse_ref, delta_ref, dq_ref, dsink_ref):
        band, first = _band_bias(BLK - 1)
        sinkrows = [_lanes([jnp.full((1, BLK), sink_ref[G * hk + j], F32) for j in range(G)])
                    for hk in range(SWA_KV)]

        @pl.when(pl.program_id(0) == 0)
        def _():
            dsink_ref[...] = jnp.zeros_like(dsink_ref)
        for c in range(4, SWA_CHUNKS):
            dq_ref[c] = jnp.zeros((SEQ, 128), F32)

        def tile(hk, blk, rows_q, rows_k, bias, accs):
            nk = bias.shape[0]
            k2 = _stack_pair(qkv_ref.at[4 + hk][rows_k, :])
            v2 = _stack_pair(qkv_ref.at[6 + hk][rows_k, :])
            qcs, docs, sT, dpT = [], [], [], []
            for c in (2 * hk, 2 * hk + 1):
                qc, doc = qkv_ref.at[c][rows_q, :], do_ref.at[c][rows_q, :]
                s2, dp2 = _dot_nt(k2, qc), _dot_nt(v2, doc)
                sT += [s2[0:nk], s2[nk:2 * nk]]
                dpT += [dp2[0:nk], dp2[nk:2 * nk]]
                qcs.append(qc)
                docs.append(doc)
            heads = slice(G * hk, G * hk + G)
            lse_r = _lanes([lse_ref.at[blk][h:h + 1, :] for h in range(G * hk, G * hk + G)])
            delta_r = _lanes([delta_ref.at[blk][h:h + 1, :] for h in range(G * hk, G * hk + G)])
            yield
            pT = jnp.exp(_lanes(sT) + bias - lse_r)
            dsT = pT * (_lanes(dpT) - delta_r)
            dsb, pb = dsT.astype(MXU), pT.astype(MXU)
            accs[hk] = accs[hk] - jnp.exp(sinkrows[hk] - lse_r) * delta_r
            yield
            dk2 = dv2 = None
            for j, c in enumerate((2 * hk, 2 * hk + 1)):
                q0, q1 = slice(2 * j * BLK, (2 * j + 1) * BLK), slice((2 * j + 1) * BLK, (2 * j + 2) * BLK)
                ds2 = jnp.concatenate([dsb[:, q0], dsb[:, q1]], axis=0)
                p2 = jnp.concatenate([pb[:, q0], pb[:, q1]], axis=0)
                dq_ref.at[c][rows_q, :] = _dot_t0(ds2, k2)
                dk2 = _dot(ds2, qcs[j]) if dk2 is None else dk2 + _dot(ds2, qcs[j])
                dv2 = _dot(p2, docs[j]) if dv2 is None else dv2 + _dot(p2, docs[j])
            dq_ref.at[4 + hk][rows_k, :] += _pair_rows(dk2, nk)
            dq_ref.at[6 + hk][rows_k, :] += _pair_rows(dv2, nk)

        def run(tiles_of, accs):
            accs = list(accs)
            _interleave(tiles_of(accs))
            return tuple(accs)

        zero = jnp.zeros((1, G * BLK), F32)
        accs = run(lambda a: [tile(hk, 0, pl.ds(0, BLK), pl.ds(0, BLK), first, a) for hk in range(SWA_KV)],
                   (zero,) * SWA_KV)

        def loop(j, accs):
            def tiles_of(a):
                out = []
                for u in range(SWA_UNROLL):
                    i = 1 + j * SWA_UNROLL + u
                    r0 = pl.multiple_of(i * BLK, BLK)
                    rk = pl.multiple_of(i * BLK - BLK, BLK)
                    out += [tile(hk, i, pl.ds(r0, BLK), pl.ds(rk, 2 * BLK), band, a) for hk in range(SWA_KV)]
                return out
            return run(tiles_of, accs)
        accs = lax.fori_loop(0, (N_QBLK - 1) // SWA_UNROLL, loop, accs)
        for hk in range(SWA_KV):
            for j in range(G):
                tot = jnp.sum(accs[hk][:, j * BLK:(j + 1) * BLK], axis=1, keepdims=True)
                dsink_ref[G * hk + j:G * hk + j + 1, :] += jnp.broadcast_to(tot, (1, 128))

    stat = pl.BlockSpec((None, N_QBLK, 8, 128), lambda b: (b, 0, 0, 0))
    return pl.pallas_call(
        body, name="swa_bwd", grid=(B,),
        in_specs=[pl.BlockSpec(memory_space=pltpu.SMEM),
                  pl.BlockSpec((None, SWA_CHUNKS, SEQ, 128), lambda b: (b, 0, 0, 0)),
                  pl.BlockSpec((None, SWA_Q // 2, SEQ, 128), lambda b: (b, 0, 0, 0)), stat, stat],
        out_specs=[pl.BlockSpec((None, SWA_CHUNKS, SEQ, 128), lambda b: (b, 0, 0, 0)),
                   pl.BlockSpec((8, 128), lambda b: (0, 0))],
        out_shape=[jax.ShapeDtypeStruct((B, SWA_CHUNKS, SEQ, 128), F32), jax.ShapeDtypeStruct((8, 128), F32)],
        compiler_params=_cparams(("arbitrary",)),
    )(sinks, qkva, do, lse, delta)


DILATIONS = (1, 4, 16)
DIL_PAIRS_H = DIL_H // 2


def _stream_rows(d, r, i, n):
    if d == 1:
        return pl.ds(pl.multiple_of(i * BLK, BLK), n)
    return pl.ds(r + i * (BLK * d), n, stride=d)


def _spread_matrix():
    row = lax.broadcasted_iota(jnp.int32, (PICK_ROWS, 128), 0)
    lane = lax.broadcasted_iota(jnp.int32, (PICK_ROWS, 128), 1)
    return ((row < 6) & ((row % 2 == 1) == (lane >= HEAD))).astype(MXU)


def _lanes_to_tokens(v0, v1, spread):
    n = v0.shape[1]
    row = lax.broadcasted_iota(jnp.int32, (PICK_ROWS, n), 0)
    a = jnp.zeros((PICK_ROWS, n), F32)
    for i, (p0, p1) in enumerate(zip(_split3(v0), _split3(v1))):
        a = jnp.where(row == 2 * i, p0.astype(F32), a)
        a = jnp.where(row == 2 * i + 1, p1.astype(F32), a)
    return _dot_t0(a.astype(MXU), spread)


def _tokens_to_lanes(t):
    r = t.T
    return r[0:1, :], r[HEAD:HEAD + 1, :]


def _dil_schedule(body_first, body_next):
    for p, d in enumerate(DILATIONS):
        nblk = SEQ // d // BLK
        if d == 1:
            _interleave([body_first(p, d, 0)])
            def loop(j, c, p=p, d=d):
                _interleave([body_next(p, d, 0, 1 + 3 * j + u) for u in range(3)])
                return c
            lax.fori_loop(0, (nblk - 1) // 3, loop, 0)
        elif nblk > 1:
            def loop(r, c, p=p, d=d, nblk=nblk):
                _interleave([body_first(p, d, r)] + [body_next(p, d, r, i) for i in range(1, nblk)])
                return c
            lax.fori_loop(0, d, loop, 0)
        else:
            def loop(j, c, p=p, d=d):
                _interleave([body_first(p, d, 4 * j + u) for u in range(4)])
                return c
            lax.fori_loop(0, d // 4, loop, 0)


def _dil_fwd(qkvb):
    B = qkvb.shape[0]

    def body(qkv_ref, o_ref):
        band, first = _band_bias(BLK)
        spread = _spread_matrix()

        def block(p, d, rows_q, rows_k, bias):
            nk = bias.shape[0]
            sT = []
            for c in range(DIL_PAIRS_H):
                qc = qkv_ref.at[c][rows_q, :].astype(MXU)
                s2 = _dot_nt(_stack_pair(qkv_ref.at[DIL_PAIRS_H + c][rows_k, :]), qc)
                sT += [s2[0:nk], s2[nk:2 * nk]]
            yield
            sT = _lanes(sT) + bias
            m = jnp.max(sT, axis=0, keepdims=True)
            pT = jnp.exp(sT - m)
            l = jnp.sum(pT, axis=0, keepdims=True)
            pnT = (pT * (1.0 / l)).astype(MXU)
            lse = m + jnp.log(l)
            yield
            for c in range(DIL_PAIRS_H):
                q0, q1 = slice(2 * c * BLK, (2 * c + 1) * BLK), slice((2 * c + 1) * BLK, (2 * c + 2) * BLK)
                p2 = jnp.concatenate([pnT[:, q0], pnT[:, q1]], axis=0)
                o_ref.at[p, c][rows_q, :] = _dot_t0(p2, _stack_pair(qkv_ref.at[2 * DIL_PAIRS_H + c][rows_k, :]))
                o_ref.at[p, DIL_PAIRS_H + c][rows_q, :] = _lanes_to_tokens(lse[:, q0], lse[:, q1], spread)

        def body_first(p, d, r):
            rows = _stream_rows(d, r, 0, BLK)
            return block(p, d, rows, rows, first)

        def body_next(p, d, r, i):
            return block(p, d, _stream_rows(d, r, i, BLK), _stream_rows(d, r, i - 1, 2 * BLK), band)

        _dil_schedule(body_first, body_next)

    return pl.pallas_call(
        body, name="dil_fwd", grid=(B,),
        in_specs=[pl.BlockSpec((None, 6, SEQ, 128), lambda b: (b, 0, 0, 0))],
        out_specs=pl.BlockSpec((None, 3, 4, SEQ, 128), lambda b: (b, 0, 0, 0, 0)),
        out_shape=jax.ShapeDtypeStruct((B, 3, 4, SEQ, 128), F32),
        compiler_params=_cparams(("arbitrary",)),
    )(qkvb)


def _reduce_scatter_ops(send_refs, land_refs, send_sems, recv_sems):
    x, y, c = _my_pos()
    me = 4 * x + 2 * y + c
    n = len(send_refs)

    def to_peer(a, j):
        return pltpu.make_async_remote_copy(
            src_ref=send_refs[a].at[j], dst_ref=land_refs[a].at[me], send_sem=send_sems.at[a, j],
            recv_sem=recv_sems.at[a, me], device_id=_dev_coords(j), device_id_type=MESH)

    def from_peer(a, m):
        return pltpu.make_async_remote_copy(
            src_ref=land_refs[a].at[m], dst_ref=land_refs[a].at[m], send_sem=send_sems.at[a, m],
            recv_sem=recv_sems.at[a, m], device_id=_dev_coords(m), device_id_type=MESH)

    def start():
        for j in range(N_DEV):
            @pl.when(me != j)
            def _(j=j):
                for a in range(n):
                    to_peer(a, j).start()
        for a in range(n):
            land_refs[a][me] = jnp.zeros(land_refs[a].shape[1:], land_refs[a].dtype)

    def finish(own_refs, out_refs):
        for m in range(N_DEV):
            @pl.when(me != m)
            def _(m=m):
                for a in range(n):
                    from_peer(a, m).wait_recv()
        for j in range(N_DEV):
            @pl.when(me != j)
            def _(j=j):
                for a in range(n):
                    to_peer(a, j).wait_send()
        for a in range(n):
            def chunk(i, carry, a=a):
                rs = pl.ds(pl.multiple_of(i * REDUCE_ROWS, REDUCE_ROWS), REDUCE_ROWS)
                g = own_refs[a][rs, :]
                for m in range(N_DEV):
                    g = g + land_refs[a][m, rs, :].astype(F32)
                out_refs[a][rs, :] = g
                return carry
            lax.fori_loop(0, own_refs[a].shape[0] // REDUCE_ROWS, chunk, 0)

    return start, finish


def _dil_bwd(qkvb, dobb, sends, owns):
    B = qkvb.shape[0]
    n_rs = len(sends)

    def body(qkv_ref, dob_ref, *rest):
        send_refs, own_refs = rest[:n_rs], rest[n_rs:2 * n_rs]
        dq_ref = rest[2 * n_rs]
        out_refs = rest[2 * n_rs + 1:3 * n_rs + 1]
        land_refs = rest[3 * n_rs + 1:4 * n_rs + 1]
        send_sems, recv_sems = rest[4 * n_rs + 1:]
        rs_start, rs_finish = _reduce_scatter_ops(send_refs, land_refs, send_sems, recv_sems)
        pl.when(pl.program_id(0) == 0)(rs_start)

        band, first = _band_bias(BLK)
        dq_ref[...] = jnp.zeros_like(dq_ref)

        def block(p, d, rows_q, rows_k, bias):
            nk = bias.shape[0]
            lo = lax.broadcasted_iota(jnp.int32, (nk, 128), 1) < HEAD
            qcs, docs, k2s, sT, dpT, lse, delta = [], [], [], [], [], [], []
            for c in range(DIL_PAIRS_H):
                qc = qkv_ref.at[c][rows_q, :].astype(MXU)
                doc = dob_ref.at[c][rows_q, :].astype(MXU)
                k2 = _stack_pair(qkv_ref.at[DIL_PAIRS_H + c][rows_k, :])
                s2 = _dot_nt(k2, qc)
                dp2 = _dot_nt(_stack_pair(qkv_ref.at[2 * DIL_PAIRS_H + c][rows_k, :]), doc)
                sT += [s2[0:nk], s2[nk:2 * nk]]
                dpT += [dp2[0:nk], dp2[nk:2 * nk]]
                lse += _tokens_to_lanes(dob_ref.at[DIL_PAIRS_H + c][rows_q, :])
                delta += _tokens_to_lanes(dob_ref.at[2 * DIL_PAIRS_H + c][rows_q, :])
                qcs.append(qc)
                docs.append(doc)
                k2s.append(k2)
            yield
            pT = jnp.exp(_lanes(sT) + bias - _lanes(lse))
            dsT = pT * (_lanes(dpT) - _lanes(delta))
            dsb, pb = dsT.astype(MXU), pT.astype(MXU)
            yield
            for c in range(DIL_PAIRS_H):
                q0, q1 = slice(2 * c * BLK, (2 * c + 1) * BLK), slice((2 * c + 1) * BLK, (2 * c + 2) * BLK)
                ds2 = jnp.concatenate([dsb[:, q0], dsb[:, q1]], axis=0)
                p2 = jnp.concatenate([pb[:, q0], pb[:, q1]], axis=0)
                dq_ref.at[c][rows_q, :] += _dot_t0(ds2, k2s[c])
                dk2, dv2 = _dot(ds2, qcs[c]), _dot(p2, docs[c])
                dq_ref.at[DIL_PAIRS_H + c][rows_k, :] += jnp.where(lo, dk2[0:nk], dk2[nk:2 * nk])
                dq_ref.at[2 * DIL_PAIRS_H + c][rows_k, :] += jnp.where(lo, dv2[0:nk], dv2[nk:2 * nk])

        def body_first(p, d, r):
            rows = _stream_rows(d, r, 0, BLK)
            return block(p, d, rows, rows, first)

        def body_next(p, d, r, i):
            return block(p, d, _stream_rows(d, r, i, BLK), _stream_rows(d, r, i - 1, 2 * BLK), band)

        _dil_schedule(body_first, body_next)

        @pl.when(pl.program_id(0) == pl.num_programs(0) - 1)
        def _():
            rs_finish(own_refs, out_refs)

    spec = pl.BlockSpec((None, 6, SEQ, 128), lambda b: (b, 0, 0, 0))
    any_spec = pl.BlockSpec(memory_space=pl.ANY)
    vmem = pl.BlockSpec(memory_space=pltpu.VMEM)
    outs = pl.pallas_call(
        body, name="dil_bwd", grid=(B,),
        in_specs=[spec, spec] + [any_spec] * n_rs + [vmem] * n_rs, out_specs=[spec] + [vmem] * n_rs,
        out_shape=[jax.ShapeDtypeStruct((B, 6, SEQ, 128), F32)] + [jax.ShapeDtypeStruct(o.shape, F32) for o in owns],
        scratch_shapes=[pltpu.VMEM(s.shape, s.dtype) for s in sends]
        + [pltpu.SemaphoreType.DMA((n_rs, N_DEV)), pltpu.SemaphoreType.DMA((n_rs, N_DEV))],
        compiler_params=_cparams(("arbitrary",)),
    )(qkvb, dobb, *sends, *owns)
    return outs[0], outs[1:]


MEM_UNROLL = 4
MEM_PAIRS = MEM_H // 2


def _mem_attn_fwd(qc, mem, w_mem):
    B = qc.shape[0]

    def body(q_ref, mem_ref, w_ref, o_ref, lse_ref, mkv_ref, k2_ref, v2_ref):
        mkv = _dot(mem_ref[...].astype(MXU), w_ref[...])
        mkv_ref[...] = mkv.astype(MXU)
        for c in range(MEM_PAIRS):
            k2_ref[c] = _stack_pair(mkv[:, c * 128:(c + 1) * 128])
            v2_ref[c] = _stack_pair(mkv[:, W_C + c * 128:W_C + (c + 1) * 128])
        lse_ref[...] = jnp.zeros_like(lse_ref)

        def tile(blk):
            rows = pl.ds(pl.multiple_of(blk * BLK, BLK), BLK)
            sT = []
            for c in range(MEM_PAIRS):
                s2 = _dot_nt(k2_ref[c], q_ref.at[c][rows, :])
                sT += [s2[0:MEM_LEN], s2[MEM_LEN:2 * MEM_LEN]]
            yield
            pnT, lse = _softmax_cols(_lanes(sT))
            yield
            for c in range(MEM_PAIRS):
                p2 = jnp.concatenate([pnT[:, 2 * c * BLK:(2 * c + 1) * BLK],
                                      pnT[:, (2 * c + 1) * BLK:(2 * c + 2) * BLK]], axis=0)
                o_ref.at[c][rows, :] = _dot_t0(p2, v2_ref[c])
            for h in range(MEM_H):
                lse_ref.at[blk][h:h + 1, :] = lse[:, h * BLK:(h + 1) * BLK]

        def loop(j, carry):
            _interleave([tile(j * MEM_UNROLL + u) for u in range(MEM_UNROLL)])
            return carry
        lax.fori_loop(0, N_QBLK // MEM_UNROLL, loop, 0)

    return pl.pallas_call(
        body, name="mem_attn_fwd", grid=(B,),
        in_specs=[pl.BlockSpec((None, MEM_PAIRS, SEQ, 128), lambda b: (b, 0, 0, 0)),
                  pl.BlockSpec((None, MEM_LEN, D_MODEL), lambda b: (b, 0, 0)),
                  pl.BlockSpec((D_MODEL, 2 * W_C), lambda b: (0, 0))],
        out_specs=[pl.BlockSpec((None, MEM_PAIRS, SEQ, 128), lambda b: (b, 0, 0, 0)),
                   pl.BlockSpec((None, N_QBLK, 8, 128), lambda b: (b, 0, 0, 0)),
                   pl.BlockSpec((None, MEM_LEN, 2 * W_C), lambda b: (b, 0, 0))],
        out_shape=[jax.ShapeDtypeStruct((B, MEM_PAIRS, SEQ, 128), F32),
                   jax.ShapeDtypeStruct((B, N_QBLK, 8, 128), F32),
                   jax.ShapeDtypeStruct((B, MEM_LEN, 2 * W_C), MXU)],
        scratch_shapes=[pltpu.VMEM((MEM_PAIRS, 2 * MEM_LEN, 128), MXU), pltpu.VMEM((MEM_PAIRS, 2 * MEM_LEN, 128), MXU)],
        compiler_params=_cparams(("arbitrary",)),
    )(qc, mem, w_mem)


def _mem_attn_bwd(qc, mkv, do, lse, delta, mem):
    B = qc.shape[0]

    def body(q_ref, mkv_ref, do_ref, lse_ref, delta_ref, mem_ref, dq_ref, dw_ref, dmkv_ref, k2_ref, v2_ref):
        @pl.when(pl.program_id(0) == 0)
        def _():
            dw_ref[...] = jnp.zeros_like(dw_ref)
        dmkv_ref[...] = jnp.zeros_like(dmkv_ref)
        for c in range(MEM_PAIRS):
            k2_ref[c] = _stack_pair(mkv_ref[:, c * 128:(c + 1) * 128])
            v2_ref[c] = _stack_pair(mkv_ref[:, W_C + c * 128:W_C + (c + 1) * 128])

        def tile(blk):
            rows = pl.ds(pl.multiple_of(blk * BLK, BLK), BLK)
            qcs, docs, sT, dpT = [], [], [], []
            for c in range(MEM_PAIRS):
                qc_, doc = q_ref.at[c][rows, :], do_ref.at[c][rows, :]
                s2, dp2 = _dot_nt(k2_ref[c], qc_), _dot_nt(v2_ref[c], doc)
                sT += [s2[0:MEM_LEN], s2[MEM_LEN:2 * MEM_LEN]]
                dpT += [dp2[0:MEM_LEN], dp2[MEM_LEN:2 * MEM_LEN]]
                qcs.append(qc_)
                docs.append(doc)
            lse_r = _lanes([lse_ref.at[blk][h:h + 1, :] for h in range(MEM_H)])
            delta_r = _lanes([delta_ref.at[blk][h:h + 1, :] for h in range(MEM_H)])
            yield
            pT = jnp.exp(_lanes(sT) - lse_r)
            dsT = pT * (_lanes(dpT) - delta_r)
            dsb, pb = dsT.astype(MXU), pT.astype(MXU)
            yield
            for c in range(MEM_PAIRS):
                q0, q1 = slice(2 * c * BLK, (2 * c + 1) * BLK), slice((2 * c + 1) * BLK, (2 * c + 2) * BLK)
                ds2 = jnp.concatenate([dsb[:, q0], dsb[:, q1]], axis=0)
                p2 = jnp.concatenate([pb[:, q0], pb[:, q1]], axis=0)
                dq_ref.at[c][rows, :] = _dot_t0(ds2, k2_ref[c])
                dmkv_ref[:, c * 128:(c + 1) * 128] += _pair_rows(_dot(ds2, qcs[c]), MEM_LEN)
                dmkv_ref[:, W_C + c * 128:W_C + (c + 1) * 128] += _pair_rows(_dot(p2, docs[c]), MEM_LEN)

        def loop(j, carry):
            _interleave([tile(j * MEM_UNROLL + u) for u in range(MEM_UNROLL)])
            return carry
        lax.fori_loop(0, N_QBLK // MEM_UNROLL, loop, 0)
        dw_ref[...] += _dot_tn(mem_ref[...], dmkv_ref[...].astype(MXU))

    stat = pl.BlockSpec((None, N_QBLK, 8, 128), lambda b: (b, 0, 0, 0))
    pairs = pl.BlockSpec((None, MEM_PAIRS, SEQ, 128), lambda b: (b, 0, 0, 0))
    return pl.pallas_call(
        body, name="mem_attn_bwd", grid=(B,),
        in_specs=[pairs, pl.BlockSpec((None, MEM_LEN, 2 * W_C), lambda b: (b, 0, 0)), pairs, stat, stat,
                  pl.BlockSpec((None, MEM_LEN, D_MODEL), lambda b: (b, 0, 0))],
        out_specs=[pairs, pl.BlockSpec((D_MODEL, 2 * W_C), lambda b: (0, 0))],
        out_shape=[jax.ShapeDtypeStruct((B, MEM_PAIRS, SEQ, 128), F32),
                   jax.ShapeDtypeStruct((D_MODEL, 2 * W_C), F32)],
        scratch_shapes=[pltpu.VMEM((MEM_LEN, 2 * W_C), F32),
                        pltpu.VMEM((MEM_PAIRS, 2 * MEM_LEN, 128), MXU), pltpu.VMEM((MEM_PAIRS, 2 * MEM_LEN, 128), MXU)],
        compiler_params=_cparams(("arbitrary",)),
    )(qc, mkv, do, lse, delta, mem)


def _headsum(t, e):
    if MXU == F32:
        return _dot(t, e)
    hi = t.astype(MXU)
    lo = (t - hi.astype(F32)).astype(MXU)
    return _dot(hi, e) + _dot(lo, e)


def _heads_to_rows(t, e):
    return sum(_dot_nt(e, part) for part in _split3(t))


POST_ROWS = 256


def _post(o_a, olse_b, o_c, z, x2, tgt, g, gain, bias, w_out, hsum, hrows):
    T = x2.shape[0]
    tm = 256
    nt = SEQ // tm

    def body(oa_ref, ob_ref, oc_ref, z_ref, x_ref, t_ref, g_ref, gain_ref, bias_ref, w_ref, e_ref, er_ref,
             gx_ref, doa_ref, dela_ref, dobb_ref, doc_ref, delc_ref, dz_ref, dw_ref, small_ref, loss_ref):
        @pl.when(pl.program_id(0) == 0)
        def _():
            dw_ref[...] = jnp.zeros_like(dw_ref)
            small_ref[...] = jnp.zeros_like(small_ref)
            loss_ref[...] = jnp.zeros_like(loss_ref)

        gg = g_ref[...]
        gain_v = gain_ref[...]
        gain_s = gain_v * (1.0 / D_MODEL)
        bias_v = bias_ref[...]
        w = w_ref[...]

        def rms(o):
            rr = lax.rsqrt(jnp.mean(o * o, axis=1, keepdims=True) + RMS_EPS)
            return o * rr, rr

        def rows_of(rs):
            oa = _lanes([oa_ref.at[c][rs, :] for c in range(SWA_Q // 2)])
            (o1, l1), (o4, l4), (o16, l16) = [
                (_lanes([ob_ref.at[p, 0][rs, :], ob_ref.at[p, 1][rs, :]]),
                 _lanes([ob_ref.at[p, 2][rs, :], ob_ref.at[p, 3][rs, :]])) for p in range(3)]
            mx = jnp.maximum(jnp.maximum(l1, l4), l16)
            e1, e4, e16 = jnp.exp(l1 - mx), jnp.exp(l4 - mx), jnp.exp(l16 - mx)
            den = e1 + e4 + e16
            ob = (e1 * o1 + e4 * o4 + e16 * o16) / den
            lse_b = mx + jnp.log(den)
            oc = _lanes([oc_ref.at[c][rs, :] for c in range(MEM_PAIRS)])
            na, ra = rms(oa)
            nb, rb = rms(ob)
            nc, rc = rms(oc)
            n = jnp.concatenate([na, nb, nc], axis=1)
            zz = z_ref[rs, :]
            sig = 0.5 * jnp.tanh(0.5 * zz) + 0.5
            sz = zz * sig
            gs = gg * sz
            u = n * gs
            r = ALPHA * x_ref[rs, :] + _dot(u.astype(MXU), w)
            rc0 = r - jnp.mean(r, axis=1, keepdims=True)
            rstd = lax.rsqrt(jnp.mean(rc0 * rc0, axis=1, keepdims=True) + LN_EPS)
            xhat = rc0 * rstd
            err = xhat * gain_v + bias_v - t_ref[rs, :]
            dxh = err * gain_s
            dr = rstd * (dxh - jnp.mean(dxh, axis=1, keepdims=True)
                         - xhat * jnp.mean(dxh * xhat, axis=1, keepdims=True))
            gx_ref[rs, :] = ALPHA * dr
            drb = dr.astype(MXU)
            du = _dot_nt(drb, w)
            dun = du * n
            dz = dun * (gg * (sig + sz * (1.0 - sig)))
            dz_ref[rs, :] = dz.astype(MXU)
            dn = du * gs

            def branch(lo, hi, nbr, rr):
                dnb = dn[:, lo:hi]
                return rr * (dnb - nbr * jnp.mean(dnb * nbr, axis=1, keepdims=True))

            def to_kernel(dob, o, do_ref, delta_ref):
                wd = dob.shape[1]
                for c in range(wd // 128):
                    do_ref.at[c][rs, :] = dob[:, c * 128:(c + 1) * 128].astype(do_ref.dtype)
                dT = _heads_to_rows(dob * o, er_ref[:, 0:wd])
                for jb in range((rs.stop - rs.start) // BLK):
                    delta_ref[rs.start // BLK + jb] = dT[0:8, jb * BLK:(jb + 1) * BLK]

            to_kernel(branch(0, W_A, na, ra), oa, doa_ref, dela_ref)
            to_kernel(branch(W_A + W_B, D_MIX, nc, rc), oc, doc_ref, delc_ref)
            dob = branch(W_A, W_A + W_B, nb, rb)
            for j, t in enumerate((dob, lse_b, _headsum(dob * ob, e_ref[...]))):
                for c in range(W_B // 128):
                    dobb_ref.at[j * (W_B // 128) + c][rs, :] = t[:, c * 128:(c + 1) * 128]
            csum = lambda t: jnp.sum(t, axis=0, keepdims=True)
            return (u, drb, jnp.sum(err * err), csum(err * xhat), csum(err), csum(dun * sz), csum(dz))

        parts = [rows_of(slice(k * POST_ROWS, (k + 1) * POST_ROWS)) for k in range(tm // POST_ROWS)]
        tot = [sum(p[i] for p in parts) for i in range(2, 7)]
        dw_ref[...] += _dot_tn(jnp.concatenate([p[0] for p in parts], axis=0),
                               jnp.concatenate([p[1] for p in parts], axis=0))
        loss_ref[...] += 0.5 * tot[0] * (1.0 / D_MODEL)
        small_ref[0:1, :] += tot[1] * (1.0 / D_MODEL)
        small_ref[1:2, :] += tot[2] * (1.0 / D_MODEL)
        small_ref[2:3, :] += tot[3]
        small_ref[3:4, :] += tot[4]

    B = T // SEQ
    row = lambda w: pl.BlockSpec((tm, w), lambda i: (i, 0))
    full = lambda a, b: pl.BlockSpec((a, b), lambda i: (0, 0))
    chunked = lambda n: pl.BlockSpec((None, n, tm, 128), lambda i: (i // nt, 0, i % nt, 0))
    stat = pl.BlockSpec((None, tm // BLK, 8, 128), lambda i: (i // nt, i % nt, 0, 0))
    return pl.pallas_call(
        body, name="post_fwd_bwd", grid=(T // tm,),
        in_specs=[chunked(SWA_Q // 2), pl.BlockSpec((None, 3, 4, tm, 128), lambda i: (i // nt, 0, 0, i % nt, 0)),
                  chunked(MEM_PAIRS),
                  row(D_MIX), row(D_MODEL), row(D_MODEL),
                  full(1, D_MIX), full(1, D_MODEL), full(1, D_MODEL), full(D_MIX, D_MODEL), full(W_B, W_B),
                  full(PICK_ROWS, W_A)],
        out_specs=[row(D_MODEL), chunked(SWA_Q // 2), stat, chunked(6), chunked(MEM_PAIRS), stat, row(D_MIX),
                   full(D_MIX, D_MODEL), full(8, D_MODEL), full(8, 128)],
        out_shape=[jax.ShapeDtypeStruct((T, D_MODEL), F32),
                   jax.ShapeDtypeStruct((B, SWA_Q // 2, SEQ, 128), MXU),
                   jax.ShapeDtypeStruct((B, N_QBLK, 8, 128), F32),
                   jax.ShapeDtypeStruct((B, 6, SEQ, 128), F32),
                   jax.ShapeDtypeStruct((B, MEM_PAIRS, SEQ, 128), MXU),
                   jax.ShapeDtypeStruct((B, N_QBLK, 8, 128), F32),
                   jax.ShapeDtypeStruct((T, D_MIX), MXU),
                   jax.ShapeDtypeStruct((D_MIX, D_MODEL), F32),
                   jax.ShapeDtypeStruct((8, D_MODEL), F32),
                   jax.ShapeDtypeStruct((8, 128), F32)],
        compiler_params=_cparams(("arbitrary",)),
    )(o_a, olse_b, o_c, z, x2, tgt, g, gain, bias, w_out, hsum, hrows)


def _dh_build(dqkva, dqkvb, dqc, dz, tab):
    T = dz.shape[0]
    tm = 512
    nt = SEQ // tm
    HQ = D_IN - D_MIX

    def body(da_ref, db6_ref, dqc_ref, dz_ref, tab_ref, dh_ref, db_ref):
        @pl.when(pl.program_id(0) == 0)
        def _():
            db_ref[...] = jnp.zeros_like(db_ref)
        tab = _rope_tab((pl.program_id(0) % nt) * tm, tm, tab_ref[...])
        lo = lax.broadcasted_iota(jnp.int32, (tm, 128), 1) < HEAD

        def kv_grad(c):
            g0, g1 = da_ref[c], da_ref[c + 1]
            return jnp.where(lo, g0 + pltpu.roll(g0, HEAD, 1), g1 + pltpu.roll(g1, HEAD, 1))

        parts = [_rope(_lanes([da_ref[c] for c in range(SWA_Q // 2)]), tab, -1) * Q_SCALE,
                 _rope(kv_grad(4), tab, -1),
                 kv_grad(6),
                 _rope(_lanes([db6_ref[0], db6_ref[1]]), tab, -1) * Q_SCALE,
                 _rope(_lanes([db6_ref[2], db6_ref[3]]), tab, -1),
                 _lanes([db6_ref[4], db6_ref[5]]),
                 _lanes([dqc_ref[c] for c in range(MEM_PAIRS)]) * Q_SCALE]
        dhq = jnp.concatenate(parts, axis=1)
        db_ref[0:1, :] += jnp.sum(dhq, axis=0, keepdims=True)
        dh_ref[:, 0:HQ] = dhq.astype(MXU)
        dh_ref[:, HQ:D_IN] = dz_ref[...]

    row = lambda w: pl.BlockSpec((tm, w), lambda i: (i, 0))
    chunked = lambda n: pl.BlockSpec((None, n, tm, 128), lambda i: (i // nt, 0, i % nt, 0))
    return pl.pallas_call(
        body, name="dh_build", grid=(T // tm,),
        in_specs=[chunked(SWA_CHUNKS), chunked(6), chunked(MEM_PAIRS), row(D_MIX),
                  pl.BlockSpec((1, 128), lambda i: (0, 0))],
        out_specs=[row(D_IN), pl.BlockSpec((8, HQ), lambda i: (0, 0))],
        out_shape=[jax.ShapeDtypeStruct((T, D_IN), MXU), jax.ShapeDtypeStruct((8, HQ), F32)],
        compiler_params=_cparams(("arbitrary",)),
    )(dqkva, dqkvb, dqc, dz, tab)


TAIL_TK = 512
TAIL_TN = D_IN // 2
TAIL_TM = 256
REDUCE_ROWS = 128


def _tail(xt, dh, gx1, w_in, small_g):
    T = xt.shape[1]
    kt = T // TAIL_TK
    ndw = (D_IN // TAIL_TN) * kt
    nsteps = ndw + T // TAIL_TM
    n_pass = D_IN // TAIL_TN
    assert n_pass == 2 and TAIL_TN == 4 * COLS_PER_DEV and kt >= 2
    pay = dh.dtype
    blk_shape = (D_MODEL, COLS_PER_DEV)
    n_half = 2 * n_pass
    n_chip = N_DEV // 2

    def body(xt_ref, dh1_ref, dh2_ref, gx_ref, w_hbm, sg_ref, dx_ref, gin_ref, gsm_ref,
             acc_ref, w_ref, mine_ref, stagea_ref, landa_ref, stageb_ref, landb_ref, own_ref, lsm_ref,
             sa_sems, ra_sems, sb_sems, rb_sems, ss_sems, rs_sems, w_sem):
        s = pl.program_id(0)
        x, y, c = _my_pos()
        me = 4 * x + 2 * y + c
        chip = 2 * x + y

        def to_sibling(q):
            return pltpu.make_async_remote_copy(
                src_ref=stagea_ref.at[q], dst_ref=landa_ref.at[q], send_sem=sa_sems.at[q], recv_sem=ra_sems.at[q],
                device_id=(x, y, 1 - c), device_id_type=MESH)

        def to_owner(q):
            return pltpu.make_async_remote_copy(
                src_ref=stageb_ref.at[q], dst_ref=landb_ref.at[chip], send_sem=sb_sems.at[q],
                recv_sem=rb_sems.at[chip], device_id=(q // 2, q % 2, c), device_id_type=MESH)

        def from_chip(m):
            return pltpu.make_async_remote_copy(
                src_ref=landb_ref.at[m], dst_ref=landb_ref.at[m], send_sem=sb_sems.at[m], recv_sem=rb_sems.at[m],
                device_id=(m // 2, m % 2, c), device_id_type=MESH)

        def is_me(q):
            return (x == q // 2) & (y == q % 2)

        def small_to(j):
            return pltpu.make_async_remote_copy(
                src_ref=sg_ref, dst_ref=lsm_ref.at[me], send_sem=ss_sems.at[j], recv_sem=rs_sems.at[me],
                device_id=_dev_coords(j), device_id_type=MESH)

        def small_from(m):
            return pltpu.make_async_remote_copy(
                src_ref=lsm_ref.at[m], dst_ref=lsm_ref.at[m], send_sem=ss_sems.at[m], recv_sem=rs_sems.at[m],
                device_id=_dev_coords(m), device_id_type=MESH)

        w_copy = pltpu.make_async_copy(w_hbm, w_ref, w_sem)

        @pl.when(s == 0)
        def _():
            w_copy.start()
            for j in range(N_DEV):
                pl.when(me != j)(small_to(j).start)
            lsm_ref[me] = sg_ref[...]
            landb_ref[chip] = jnp.zeros(blk_shape, pay)

        @pl.when(s < ndw)
        def _():
            @pl.when(s % kt == 0)
            def _():
                acc_ref[...] = jnp.zeros_like(acc_ref)
            acc_ref[...] += _dot(xt_ref[...], dh1_ref[...])

        for p in range(n_pass):
            @pl.when(s == p * kt + kt - 1)
            def _(p=p):
                for cc in range(2):
                    @pl.when(c == cc)
                    def _(cc=cc):
                        for yo in range(2):
                            q = 2 * p + yo
                            same, other = 2 * yo + cc, 2 * yo + 1 - cc
                            mine_ref[q] = acc_ref[:, same * COLS_PER_DEV:(same + 1) * COLS_PER_DEV]
                            stagea_ref[q] = acc_ref[:, other * COLS_PER_DEV:(other + 1) * COLS_PER_DEV].astype(pay)
                for yo in range(2):
                    to_sibling(2 * p + yo).start()

            @pl.when(s == (p + 1) * kt + 1)
            def _(p=p):
                for yo in range(2):
                    q = 2 * p + yo
                    to_sibling(q).wait_recv()

                    def chunk(i, carry, q=q):
                        rs = pl.ds(pl.multiple_of(i * REDUCE_ROWS, REDUCE_ROWS), REDUCE_ROWS)
                        tot = mine_ref[q, rs, :] + landa_ref[q, rs, :].astype(F32)

                        @pl.when(is_me(q))
                        def _():
                            own_ref[rs, :] = tot

                        @pl.when(jnp.logical_not(is_me(q)))
                        def _():
                            stageb_ref[q, rs, :] = tot.astype(pay)
                        return carry
                    lax.fori_loop(0, D_MODEL // REDUCE_ROWS, chunk, 0)
                    pl.when(jnp.logical_not(is_me(q)))(to_owner(q).start)

        @pl.when(s >= ndw)
        def _():
            pl.when(s == ndw)(w_copy.wait)
            dx_ref[...] = _dot_nt(dh2_ref[...], w_ref[...]) + gx_ref[...]

        @pl.when(s == nsteps - 1)
        def _():
            for m in range(n_chip):
                pl.when(m != chip)(from_chip(m).wait_recv)
            for m in range(N_DEV):
                pl.when(me != m)(small_from(m).wait_recv)
            for q in range(n_half):
                to_sibling(q).wait_send()
                pl.when(jnp.logical_not(is_me(q)))(to_owner(q).wait_send)
            for j in range(N_DEV):
                pl.when(me != j)(small_to(j).wait_send)

            def chunk(i, carry):
                rs = pl.ds(pl.multiple_of(i * REDUCE_ROWS, REDUCE_ROWS), REDUCE_ROWS)
                g = own_ref[rs, :]
                for m in range(n_chip):
                    g = g + landb_ref[m, rs, :].astype(F32)
                gin_ref[rs, :] = g
                return carry
            lax.fori_loop(0, D_MODEL // REDUCE_ROWS, chunk, 0)
            g = lsm_ref[0]
            for m in range(1, N_DEV):
                g = g + lsm_ref[m]
            gsm_ref[...] = g

    dw_step = lambda s: jnp.minimum(s, ndw - 1)
    dx_step = lambda s: jnp.maximum(s - ndw, 0)
    any_spec = pl.BlockSpec(memory_space=pl.ANY)
    vmem = pl.BlockSpec(memory_space=pltpu.VMEM)
    dma = pltpu.SemaphoreType.DMA
    scratch = [pltpu.VMEM((D_MODEL, TAIL_TN), F32), pltpu.VMEM((D_MODEL, D_IN), w_in.dtype),
               pltpu.VMEM((n_half,) + blk_shape, F32),
               pltpu.VMEM((n_half,) + blk_shape, pay), pltpu.VMEM((n_half,) + blk_shape, pay),
               pltpu.VMEM((n_half,) + blk_shape, pay), pltpu.VMEM((n_chip,) + blk_shape, pay),
               pltpu.VMEM(blk_shape, F32), pltpu.VMEM((N_DEV,) + small_g.shape, F32),
               dma((n_half,)), dma((n_half,)), dma((n_half,)), dma((n_chip,)), dma((N_DEV,)), dma((N_DEV,)), dma]
    return pl.pallas_call(
        body, name="tail_dw_dx_reduce", grid=(nsteps,),
        in_specs=[pl.BlockSpec((D_MODEL, TAIL_TK), lambda s: (0, dw_step(s) % kt)),
                  pl.BlockSpec((TAIL_TK, TAIL_TN), lambda s: (dw_step(s) % kt, dw_step(s) // kt)),
                  pl.BlockSpec((TAIL_TM, D_IN), lambda s: (dx_step(s), 0)),
                  pl.BlockSpec((TAIL_TM, D_MODEL), lambda s: (dx_step(s), 0)),
                  any_spec, vmem],
        out_specs=[pl.BlockSpec((TAIL_TM, D_MODEL), lambda s: (dx_step(s), 0)), vmem, vmem],
        out_shape=[jax.ShapeDtypeStruct((T, D_MODEL), F32), jax.ShapeDtypeStruct(blk_shape, F32),
                   jax.ShapeDtypeStruct(small_g.shape, F32)],
        scratch_shapes=scratch,
        compiler_params=_cparams(("arbitrary",)),
    )(xt, dh, dh, gx1, w_in, small_g)


def _adam_update(grads, params, carried):
    n = len(grads)

    def body(*refs):
        g_refs, p_refs, o_refs = refs[1:1 + n], refs[1 + n:1 + 4 * n], refs[2 + 4 * n:]
        for a in range(n):
            rows = g_refs[a].shape[0]
            cr = REDUCE_ROWS if rows % REDUCE_ROWS == 0 else rows
            flat2 = lambda r: r.at[0] if len(r.shape) == 3 else r
            w_ref, m_ref, v_ref = [flat2(r) for r in p_refs[3 * a:3 * a + 3]]
            go_ref, d_ref, nm_ref, nv_ref = [flat2(r) for r in o_refs[4 * a:4 * a + 4]]

            def chunk(i, carry, cr=cr, g_ref=g_refs[a], w_ref=w_ref, m_ref=m_ref, v_ref=v_ref,
                      go_ref=go_ref, d_ref=d_ref, nm_ref=nm_ref, nv_ref=nv_ref):
                rs = pl.ds(pl.multiple_of(i * cr, cr), cr)
                g = g_ref[rs, :]
                go_ref[rs, :] = g
                d_ref[rs, :], nm_ref[rs, :], nv_ref[rs, :] = _adamw(w_ref[rs, :], g, m_ref[rs, :], v_ref[rs, :])
                return carry
            lax.fori_loop(0, rows // cr, chunk, 0)

    vmem = pl.BlockSpec(memory_space=pltpu.VMEM)
    any_spec = pl.BlockSpec(memory_space=pl.ANY)
    flat = [p for grp in params for p in grp]
    outs = pl.pallas_call(
        body, name="adamw", in_specs=[any_spec] + [vmem] * (4 * n), out_specs=[any_spec] + [vmem] * (4 * n),
        out_shape=[jax.ShapeDtypeStruct(carried.shape, carried.dtype)]
        + [jax.ShapeDtypeStruct(grp[0].shape, F32) for grp in params for _ in range(4)],
        input_output_aliases={0: 0},
        compiler_params=pltpu.CompilerParams(vmem_limit_bytes=VMEM_LIMIT),
    )(carried, *grads, *flat)
    return [outs[1 + 4 * a:5 + 4 * a] for a in range(n)], outs[0]


def _step(x, mem, w_in_s, w_mem_s, w_out_s, b_in, sinks, g, gain, bias, tgt):
    B = x.shape[0]
    T = B * SEQ
    x2 = x.reshape(T, D_MODEL)
    t2 = tgt.reshape(T, D_MODEL)
    tab = _rope_inv()
    lane = jnp.arange(W_A)
    hsum = (lane[:W_B, None] // HEAD == lane[None, :W_B] // HEAD).astype(MXU)
    hrows = (jnp.arange(PICK_ROWS)[:, None] == lane[None, :] // HEAD).astype(MXU)
    me = 4 * lax.axis_index("x") + 2 * lax.axis_index("y") + lax.axis_index("c")

    (w_in_all,) = _gather_weights([w_in_s])
    qkva, qkvb, qc, z, w_in, xt, w_mem_all, w_out_all = _in_proj(x2, w_in_all, b_in, tab, [w_mem_s, w_out_s])
    w_mem = w_mem_all.reshape(D_MODEL, 2 * W_C)
    w_out = w_out_all.reshape(D_MIX, D_MODEL)

    o_a, lse_a = _swa_fwd(qkva, sinks)
    olse_b = _dil_fwd(qkvb)
    o_c, lse_c, mkv = _mem_attn_fwd(qc, mem, w_mem)

    gx1, do_a, delta_a, dobb, do_c, delta_c, dz, dw_out, small, loss = _post(
        o_a, olse_b, o_c, z, x2, t2, g, gain, bias, w_out, hsum, hrows)

    dqc, dw_mem = _mem_attn_bwd(qc, mkv, do_c, lse_c, delta_c, mem)
    blocks = [dw_mem.reshape(N_DEV, ROWS_PER_DEV, 2 * W_C), dw_out.reshape(N_DEV, ROWS_PER_DEV, D_MODEL)]
    sends = [b.astype(MXU) for b in blocks]
    owns = [lax.dynamic_index_in_dim(b, me, axis=0, keepdims=False) for b in blocks]
    dqkvb, (g_mem, g_out) = _dil_bwd(qkvb, dobb, sends, owns)
    dqkva, dsink = _swa_bwd(qkva, do_a, lse_a, delta_a, sinks)
    dh, dbq = _dh_build(dqkva, dqkvb, dqc, dz, tab)

    small_g = _pack_small(dict(b_in=jnp.concatenate([dbq[0], small[3]]), sinks=dsink[:, 0], g=small[2],
                               gain=small[0], bias=small[1], loss=loss[0, 0]))
    grad_x, g_in, g_small = _tail(xt, dh, gx1, w_in, small_g)
    return grad_x.reshape(B, SEQ, D_MODEL), g_in, g_mem, g_out, g_small


def _my_pos():
    return lax.axis_index("x"), lax.axis_index("y"), lax.axis_index("c")


def _gather_weights(shards):
    n_arr = len(shards)

    def body(*refs):
        ins, outs = refs[0:n_arr], refs[n_arr:2 * n_arr]
        send_sems, recv_sems, local_sems = refs[2 * n_arr:]
        x, y, c = _my_pos()
        me, sibling = (x, y, c), (x, y, 1 - c)
        chips = [(1 - x, y), (x, 1 - y), (1 - x, 1 - y)]

        def slot(a, pos):
            return outs[a].at[4 * pos[0] + 2 * pos[1] + pos[2]]

        def copy(a, k, block, to, src=None):
            return pltpu.make_async_remote_copy(
                src_ref=slot(a, block) if src is None else src, dst_ref=slot(a, block),
                send_sem=send_sems.at[a, k], recv_sem=recv_sems.at[a, k],
                device_id=to, device_id_type=MESH)

        mine = [pltpu.make_async_copy(ins[a], slot(a, me), local_sems.at[a]) for a in range(n_arr)]
        for cp in mine:
            cp.start()
        first = []
        for a in range(n_arr):
            first.append(copy(a, 0, me, sibling, src=ins[a]))
            first += [copy(a, 1 + j, me, (*chip, c), src=ins[a]) for j, chip in enumerate(chips)]
        for cp in first:
            cp.start()
        passed = []
        for j, chip in enumerate(chips):
            for a in range(n_arr):
                copy(a, 1 + j, (*chip, c), me).wait_recv()
                fwd = copy(a, 4 + j, (*chip, c), sibling)
                fwd.start()
                passed.append(fwd)
        for a in range(n_arr):
            copy(a, 0, sibling, me).wait_recv()
            for j, chip in enumerate(chips):
                copy(a, 4 + j, (*chip, 1 - c), me).wait_recv()
        for cp in first + passed:
            cp.wait_send()
        for cp in mine:
            cp.wait()

    any_spec = pl.BlockSpec(memory_space=pl.ANY)
    return pl.pallas_call(
        body, name="gather_weights",
        in_specs=[any_spec] * n_arr, out_specs=[any_spec] * n_arr,
        out_shape=[jax.ShapeDtypeStruct((N_DEV,) + s.shape, s.dtype) for s in shards],
        scratch_shapes=[pltpu.SemaphoreType.DMA((n_arr, 7)), pltpu.SemaphoreType.DMA((n_arr, 7)),
                        pltpu.SemaphoreType.DMA((n_arr,))],
    )(*shards)


def _adamw(w, g, m, v):
    m = ADAM_B1 * m + (1.0 - ADAM_B1) * g
    v = ADAM_B2 * v + (1.0 - ADAM_B2) * (g * g)
    m_hat = m / (1.0 - ADAM_B1 ** ADAM_STEP)
    v_hat = v / (1.0 - ADAM_B2 ** ADAM_STEP)
    delta = -ADAM_LR * (m_hat / (jnp.sqrt(v_hat) + ADAM_EPS) + ADAM_WD * w)
    return delta, m, v


_SMALL_SIZES = (("b_in", D_IN), ("g", D_MIX), ("gain", D_MODEL), ("bias", D_MODEL), ("sinks", SWA_Q), ("loss", 1))


def _pack_small(d):
    flat = jnp.concatenate([jnp.reshape(d[k], (-1,)).astype(F32) if k in d else jnp.zeros((n,), F32)
                            for k, n in _SMALL_SIZES])
    flat = jnp.pad(flat, (0, SMALL_ROWS * 128 - flat.shape[0]))
    return flat.reshape(SMALL_ROWS, 128)


def _unpack_small(p):
    flat = p.reshape(-1)
    out, off = {}, 0
    for k, n in _SMALL_SIZES:
        out[k] = flat[off:off + n].reshape(1, n)
        off += n
    return out


def kernel(x, mem, w_in, b_in, w_mem, attn_sinks, g_branch, w_out, ln_gain, ln_bias, loss_target, m_w_in, m_b_in, m_w_mem, m_attn_sinks, m_g_branch, m_w_out, m_ln_gain, m_ln_bias, v_w_in, v_b_in, v_w_mem, v_attn_sinks, v_g_branch, v_w_out, v_ln_gain, v_ln_bias):
    grad_x, g_in, g_mem, g_out, g_small = _step(
        x, mem, w_in[0].astype(MXU), w_mem[0].astype(MXU), w_out[0].astype(MXU), b_in, attn_sinks[0],
        g_branch, ln_gain, ln_bias, loss_target)

    small_w = _pack_small(dict(b_in=b_in, g=g_branch, gain=ln_gain, bias=ln_bias, sinks=attn_sinks))
    small_m = _pack_small(dict(b_in=m_b_in, g=m_g_branch, gain=m_ln_gain, bias=m_ln_bias, sinks=m_attn_sinks))
    small_v = _pack_small(dict(b_in=v_b_in, g=v_g_branch, gain=v_ln_gain, bias=v_ln_bias, sinks=v_attn_sinks))
    grads = [g_in, g_mem, g_out, g_small]
    params = [(w_in, m_w_in, v_w_in), (w_mem, m_w_mem, v_w_mem), (w_out, m_w_out, v_w_out),
              (small_w, small_m, small_v)]
    res, grad_x = _adam_update(grads, params, grad_x)
    big = res[:3]
    sm = [_unpack_small(r) for r in res[3]]

    def group(i):
        return (big[0][i], sm[i]["b_in"], big[1][i], sm[i]["sinks"], sm[i]["g"], big[2][i],
                sm[i]["gain"], sm[i]["bias"])

    loss = sm[0]["loss"].reshape(())
    return (loss, grad_x, *group(0), *group(1), *group(2), *group(3))
```

```python
import functools
import math

import jax
import jax.numpy as jnp
from jax import lax
from jax.experimental import pallas as pl
from jax.experimental.pallas import tpu as pltpu

F32 = jnp.float32
MXU = jnp.bfloat16

D_MODEL = 1024
SEQ = 2048
HEAD = 64
BLK = 128
SWA_Q, SWA_KV = 8, 2
DIL_H = 4
MEM_H = 4
MEM_LEN = 256
W_A, W_KVA, W_B, W_C = 512, 128, 256, 256
D_MIX = 1024
D_IN = 2816
N_DEV = 8
COLS_PER_DEV = D_IN // N_DEV
ROWS_PER_DEV = D_MODEL // N_DEV
ROPE_THETA = 10000.0
LN_EPS = 1e-5
RMS_EPS = 1e-6
ALPHA = 2.0 ** 0.25
Q_SCALE = HEAD ** -0.5
NEG = -1e30
SMALL_ROWS = 48
VMEM_LIMIT = 56 * 1024 * 1024

ADAM_LR = 0.001
ADAM_B1 = 0.9
ADAM_B2 = 0.999
ADAM_EPS = 1e-08
ADAM_WD = 0.01
ADAM_STEP = 10

MESH = pl.DeviceIdType.MESH


def _cparams(sem=None):
    return pltpu.CompilerParams(dimension_semantics=sem, vmem_limit_bytes=VMEM_LIMIT)


def _dot(a, b):
    return jnp.dot(a, b, preferred_element_type=F32)


def _dot_nt(a, b):
    return lax.dot_general(a, b, (((1,), (1,)), ((), ())), preferred_element_type=F32)


def _dot_t0(a, b):
    return lax.dot_general(a, b, (((0,), (0,)), ((), ())), preferred_element_type=F32)


def _dot_tn(a, b):
    return jnp.dot(a.T.astype(MXU), b, preferred_element_type=F32)


def _rope(t, tab, sign):
    cos, sa, sb = tab
    outs = []
    for c in range(t.shape[1] // 128):
        tc = t[:, c * 128:(c + 1) * 128]
        r = pltpu.roll(tc, 96, 1) * sa + pltpu.roll(tc, 32, 1) * sb
        outs.append(tc * cos + r if sign > 0 else tc * cos - r)
    return outs[0] if len(outs) == 1 else jnp.concatenate(outs, axis=1)


def _rope_inv():
    inv = ROPE_THETA ** (-jnp.arange(0, HEAD, 2, dtype=F32) / HEAD)
    return jnp.tile(inv, 2 * 128 // HEAD)[None, :]


def _rope_tab(pos0, rows, inv):
    pos = (lax.broadcasted_iota(jnp.int32, (rows, 128), 0) + pos0).astype(F32)
    ang = pos * inv
    cos, sin = jnp.cos(ang), jnp.sin(ang)
    first = lax.broadcasted_iota(jnp.int32, (rows, 128), 1) % HEAD < HEAD // 2
    return cos, jnp.where(first, -sin, 0.0), jnp.where(first, 0.0, sin)


def _dev_coords(j):
    return (j >> 2, (j >> 1) & 1, j & 1)


def _in_proj(x2, w_all, b_in, tab, late_shards):
    T = x2.shape[0]
    tm = 512
    n_late = len(late_shards)

    def body(x_ref, wall_ref, b_ref, tab_ref, *rest):
        late_in, rest = rest[:n_late], rest[n_late:]
        qkva_ref, qkvb_ref, qc_ref, z_ref, w_ref, xt_ref = rest[:6]
        late_out = rest[6:6 + n_late]
        send_sems, recv_sems, local_sems = rest[6 + n_late:]
        step, last = pl.program_id(0), pl.num_programs(0) - 1
        x, y, c = _my_pos()
        me = 4 * x + 2 * y + c

        def to_peer(a, j):
            return pltpu.make_async_remote_copy(
                src_ref=late_in[a], dst_ref=late_out[a].at[me], send_sem=send_sems.at[a, j],
                recv_sem=recv_sems.at[a, me], device_id=_dev_coords(j), device_id_type=MESH)

        def from_peer(a, m):
            return pltpu.make_async_remote_copy(
                src_ref=late_out[a].at[m], dst_ref=late_out[a].at[m], send_sem=send_sems.at[a, m],
                recv_sem=recv_sems.at[a, m], device_id=_dev_coords(m), device_id_type=MESH)

        def mine(a):
            return pltpu.make_async_copy(late_in[a], late_out[a].at[me], local_sems.at[a])

        @pl.when(step == 0)
        def _():
            for a in range(n_late):
                mine(a).start()
                for j in range(N_DEV):
                    pl.when(me != j)(to_peer(a, j).start)
            for j in range(N_DEV):
                w_ref[:, j * COLS_PER_DEV:(j + 1) * COLS_PER_DEV] = wall_ref[j]

        xb = x_ref[...].astype(MXU)
        xt_ref[...] = x_ref[...].T.astype(MXU)
        tab = _rope_tab((step % nt) * tm, tm, tab_ref[...])

        def seg(c0, c1):
            return _dot(xb, w_ref[:, c0:c1]) + b_ref[:, c0:c1]

        qa = (_rope(seg(0, 512), tab, 1) * Q_SCALE).astype(MXU)
        for c in range(SWA_Q // 2):
            qkva_ref[c] = qa[:, c * 128:(c + 1) * 128]
        lo = lax.broadcasted_iota(jnp.int32, (tm, 128), 1) < HEAD
        for j, t in enumerate((_rope(seg(512, 640), tab, 1), seg(640, 768))):
            other = pltpu.roll(t, HEAD, 1)
            qkva_ref[4 + 2 * j] = jnp.where(lo, t, other).astype(MXU)
            qkva_ref[5 + 2 * j] = jnp.where(lo, other, t).astype(MXU)
        qkvb = (_rope(seg(768, 1024), tab, 1) * Q_SCALE, _rope(seg(1024, 1280), tab, 1), seg(1280, 1536))
        for j, t in enumerate(qkvb):
            for c in range(2):
                qkvb_ref[2 * j + c] = t[:, c * 128:(c + 1) * 128]
        qc = (seg(1536, 1792) * Q_SCALE).astype(MXU)
        for c in range(MEM_H // 2):
            qc_ref[c] = qc[:, c * 128:(c + 1) * 128]
        z_ref[...] = seg(1792, 2816)

        @pl.when(step == last)
        def _():
            for a in range(n_late):
                mine(a).wait()
                for m in range(N_DEV):
                    pl.when(me != m)(from_peer(a, m).wait_recv)
                for j in range(N_DEV):
                    pl.when(me != j)(to_peer(a, j).wait_send)

    nt = SEQ // tm
    any_spec = pl.BlockSpec(memory_space=pl.ANY)
    chunked = lambda n: pl.BlockSpec((None, n, tm, 128), lambda i: (i // nt, 0, i % nt, 0))
    return pl.pallas_call(
        body, name="in_proj_fwd",
        grid=(T // tm,),
        in_specs=[pl.BlockSpec((tm, D_MODEL), lambda i: (i, 0)),
                  pl.BlockSpec((N_DEV, D_MODEL, COLS_PER_DEV), lambda i: (0, 0, 0)),
                  pl.BlockSpec((1, D_IN), lambda i: (0, 0)),
                  pl.BlockSpec((1, 128), lambda i: (0, 0))] + [any_spec] * n_late,
        out_specs=[chunked(SWA_CHUNKS), chunked(6), chunked(MEM_H // 2),
                   pl.BlockSpec((tm, D_MIX), lambda i: (i, 0)),
                   pl.BlockSpec((D_MODEL, D_IN), lambda i: (0, 0)),
                   pl.BlockSpec((D_MODEL, tm), lambda i: (0, i))] + [any_spec] * n_late,
        out_shape=[jax.ShapeDtypeStruct((T // SEQ, SWA_CHUNKS, SEQ, 128), MXU),
                   jax.ShapeDtypeStruct((T // SEQ, 6, SEQ, 128), F32),
                   jax.ShapeDtypeStruct((T // SEQ, MEM_H // 2, SEQ, 128), MXU),
                   jax.ShapeDtypeStruct((T, D_MIX), F32),
                   jax.ShapeDtypeStruct((D_MODEL, D_IN), w_all.dtype),
                   jax.ShapeDtypeStruct((D_MODEL, T), MXU)]
        + [jax.ShapeDtypeStruct((N_DEV,) + s.shape, s.dtype) for s in late_shards],
        scratch_shapes=[pltpu.SemaphoreType.DMA((n_late, N_DEV)), pltpu.SemaphoreType.DMA((n_late, N_DEV)),
                        pltpu.SemaphoreType.DMA((n_late,))],
        compiler_params=_cparams(("arbitrary",)),
    )(x2, w_all, b_in, tab, *late_shards)


CHAIN = 4


def _band_bias(max_dist):
    kj = lax.broadcasted_iota(jnp.int32, (2 * BLK, BLK), 0)
    qi = lax.broadcasted_iota(jnp.int32, (2 * BLK, BLK), 1)
    dist = qi + BLK - kj
    band = jnp.where((dist >= 0) & (dist <= max_dist), 0.0, NEG).astype(F32)
    k1 = lax.broadcasted_iota(jnp.int32, (BLK, BLK), 0)
    q1 = lax.broadcasted_iota(jnp.int32, (BLK, BLK), 1)
    first = jnp.where((q1 - k1 >= 0) & (q1 - k1 <= max_dist), 0.0, NEG).astype(F32)
    return jnp.concatenate([band] * CHAIN, axis=1), jnp.concatenate([first] * CHAIN, axis=1)


def _lanes(parts):
    return jnp.concatenate(parts, axis=1)


PICK_ROWS = 16


def _stack_pair(t):
    lo = (lax.broadcasted_iota(jnp.int32, t.shape, 1) < HEAD).astype(F32)
    return jnp.concatenate([t * lo, t * (1.0 - lo)], axis=0).astype(MXU)


def _pair_rows(x, n):
    lo = lax.broadcasted_iota(jnp.int32, (n, 128), 1) < HEAD
    return jnp.where(lo, x[0:n], x[n:2 * n])


def _split3(t):
    if MXU == F32:
        return (t,)
    hi = t.astype(MXU)
    r = t - hi.astype(F32)
    mid = r.astype(MXU)
    return hi, mid, (r - mid.astype(F32)).astype(MXU)


def _interleave(tiles):
    tiles = list(tiles)
    while tiles:
        for t in list(tiles):
            try:
                next(t)
            except StopIteration:
                tiles.remove(t)


def _softmax_cols(sT, sinkrow=None):
    m = jnp.max(sT, axis=0, keepdims=True)
    if sinkrow is not None:
        m = jnp.maximum(m, sinkrow)
    pT = jnp.exp(sT - m)
    l = jnp.sum(pT, axis=0, keepdims=True)
    if sinkrow is not None:
        l = l + jnp.exp(sinkrow - m)
    return (pT * (1.0 / l)).astype(MXU), m + jnp.log(l)


SWA_CHUNKS = 8
SWA_UNROLL = 3
N_QBLK = SEQ // BLK


def _swa_fwd(qkva, sinks):
    B = qkva.shape[0]
    G = SWA_Q // SWA_KV

    def body(sink_ref, qkv_ref, o_ref, lse_ref):
        band, first = _band_bias(BLK - 1)
        sinkrows = [_lanes([jnp.full((1, BLK), sink_ref[G * hk + j], F32) for j in range(G)])
                    for hk in range(SWA_KV)]

        def tile(hk, blk, rows_q, rows_k, bias):
            nk = bias.shape[0]
            k2 = _stack_pair(qkv_ref.at[4 + hk][rows_k, :])
            sT = []
            for c in (2 * hk, 2 * hk + 1):
                s2 = _dot_nt(k2, qkv_ref.at[c][rows_q, :])
                sT += [s2[0:nk], s2[nk:2 * nk]]
            yield
            pnT, lse = _softmax_cols(_lanes(sT) + bias, sinkrows[hk])
            yield
            v2 = _stack_pair(qkv_ref.at[6 + hk][rows_k, :])
            for j, c in enumerate((2 * hk, 2 * hk + 1)):
                p2 = jnp.concatenate([pnT[:, 2 * j * BLK:(2 * j + 1) * BLK],
                                      pnT[:, (2 * j + 1) * BLK:(2 * j + 2) * BLK]], axis=0)
                o_ref.at[c][rows_q, :] = _dot_t0(p2, v2)
            for j in range(G):
                lse_ref.at[blk][G * hk + j:G * hk + j + 1, :] = lse[:, j * BLK:(j + 1) * BLK]

        def tiles_at(i):
            r0 = pl.multiple_of(i * BLK, BLK)
            rk = pl.multiple_of(i * BLK - BLK, BLK)
            return [tile(hk, i, pl.ds(r0, BLK), pl.ds(rk, 2 * BLK), band) for hk in range(SWA_KV)]

        _interleave([tile(hk, 0, pl.ds(0, BLK), pl.ds(0, BLK), first) for hk in range(SWA_KV)])

        def loop(j, carry):
            _interleave([t for u in range(SWA_UNROLL) for t in tiles_at(1 + j * SWA_UNROLL + u)])
            return carry
        lax.fori_loop(0, (N_QBLK - 1) // SWA_UNROLL, loop, 0)

    return pl.pallas_call(
        body, name="swa_fwd", grid=(B,),
        in_specs=[pl.BlockSpec(memory_space=pltpu.SMEM),
                  pl.BlockSpec((None, SWA_CHUNKS, SEQ, 128), lambda b: (b, 0, 0, 0))],
        out_specs=[pl.BlockSpec((None, SWA_Q // 2, SEQ, 128), lambda b: (b, 0, 0, 0)),
                   pl.BlockSpec((None, N_QBLK, 8, 128), lambda b: (b, 0, 0, 0))],
        out_shape=[jax.ShapeDtypeStruct((B, SWA_Q // 2, SEQ, 128), F32),
                   jax.ShapeDtypeStruct((B, N_QBLK, 8, 128), F32)],
        compiler_params=_cparams(("arbitrary",)),
    )(sinks, qkva)


EP_ROWS = 256


def _dh_epilogue(part_of, dh_ref, db_ref, inv_ref):
    @pl.when(pl.program_id(0) == 0)
    def _():
        db_ref[...] = jnp.zeros_like(db_ref)

    def step(i, carry):
        r0 = pl.multiple_of(i * EP_ROWS, EP_ROWS)
        rs = pl.ds(r0, EP_ROWS)
        part = part_of(rs, None if inv_ref is None else _rope_tab(r0, EP_ROWS, inv_ref[...]))
        db_ref[0:1, :] += jnp.sum(part, axis=0, keepdims=True)
        dh_ref[rs, :] = part.astype(dh_ref.dtype)
        return carry
    lax.fori_loop(0, SEQ // EP_ROWS, step, 0)


def _swa_bwd(qkva, do, lse, delta, sinks, inv):
    B = qkva.shape[0]
    G = SWA_Q // SWA_KV

    def body(sink_ref, qkv_ref, do_ref, lse_ref, delta_ref, inv_ref, dh_ref, db_ref, dsink_ref, dq_ref):
        band, first = _band_bias(BLK - 1)
        sinkrows = [_lanes([jnp.full((1, BLK), sink_ref[G * hk + j], F32) for j in range(G)])
                    for hk in range(SWA_KV)]

        @pl.when(pl.program_id(0) == 0)
        def _():
            dsink_ref[...] = jnp.zeros_like(dsink_ref)
        for c in range(4, SWA_CHUNKS):
            dq_ref[c] = jnp.zeros((SEQ, 128), F32)

        def tile(hk, blk, rows_q, rows_k, bias, accs):
            nk = bias.shape[0]
            k2 = _stack_pair(qkv_ref.at[4 + hk][rows_k, :])
            v2 = _stack_pair(qkv_ref.at[6 + hk][rows_k, :])
            qcs, docs, sT, dpT = [], [], [], []
            for c in (2 * hk, 2 * hk + 1):
                qc, doc = qkv_ref.at[c][rows_q, :], do_ref.at[c][rows_q, :]
                s2, dp2 = _dot_nt(k2, qc), _dot_nt(v2, doc)
                sT += [s2[0:nk], s2[nk:2 * nk]]
                dpT += [dp2[0:nk], dp2[nk:2 * nk]]
                qcs.append(qc)
                docs.append(doc)
            heads = slice(G * hk, G * hk + G)
            lse_r = _lanes([lse_ref.at[blk][h:h + 1, :] for h in range(G * hk, G * hk + G)])
            delta_r = _lanes([delta_ref.at[blk][h:h + 1, :] for h in range(G * hk, G * hk + G)])
            yield
            pT = jnp.exp(_lanes(sT) + bias - lse_r)
            dsT = pT * (_lanes(dpT) - delta_r)
            dsb, pb = dsT.astype(MXU), pT.astype(MXU)
            accs[hk] = accs[hk] - jnp.exp(sinkrows[hk] - lse_r) * delta_r
            yield
            dk2 = dv2 = None
            for j, c in enumerate((2 * hk, 2 * hk + 1)):
                q0, q1 = slice(2 * j * BLK, (2 * j + 1) * BLK), slice((2 * j + 1) * BLK, (2 * j + 2) * BLK)
                ds2 = jnp.concatenate([dsb[:, q0], dsb[:, q1]], axis=0)
                p2 = jnp.concatenate([pb[:, q0], pb[:, q1]], axis=0)
                dq_ref.at[c][rows_q, :] = _dot_t0(ds2, k2)
                dk2 = _dot(ds2, qcs[j]) if dk2 is None else dk2 + _dot(ds2, qcs[j])
                dv2 = _dot(p2, docs[j]) if dv2 is None else dv2 + _dot(p2, docs[j])
            dq_ref.at[4 + hk][rows_k, :] += _pair_rows(dk2, nk)
            dq_ref.at[6 + hk][rows_k, :] += _pair_rows(dv2, nk)

        def run(tiles_of, accs):
            accs = list(accs)
            _interleave(tiles_of(accs))
            return tuple(accs)

        zero = jnp.zeros((1, G * BLK), F32)
        accs = run(lambda a: [tile(hk, 0, pl.ds(0, BLK), pl.ds(0, BLK), first, a) for hk in range(SWA_KV)],
                   (zero,) * SWA_KV)

        def loop(j, accs):
            def tiles_of(a):
                out = []
                for u in range(SWA_UNROLL):
                    i = 1 + j * SWA_UNROLL + u
                    r0 = pl.multiple_of(i * BLK, BLK)
                    rk = pl.multiple_of(i * BLK - BLK, BLK)
                    out += [tile(hk, i, pl.ds(r0, BLK), pl.ds(rk, 2 * BLK), band, a) for hk in range(SWA_KV)]
                return out
            return run(tiles_of, accs)
        accs = lax.fori_loop(0, (N_QBLK - 1) // SWA_UNROLL, loop, accs)
        for hk in range(SWA_KV):
            for j in range(G):
                tot = jnp.sum(accs[hk][:, j * BLK:(j + 1) * BLK], axis=1, keepdims=True)
                dsink_ref[G * hk + j:G * hk + j + 1, :] += jnp.broadcast_to(tot, (1, 128))

        def part_of(rs, tab):
            lo = lax.broadcasted_iota(jnp.int32, (EP_ROWS, 128), 1) < HEAD

            def kv_grad(c):
                g0, g1 = dq_ref.at[c][rs, :], dq_ref.at[c + 1][rs, :]
                return jnp.where(lo, g0 + pltpu.roll(g0, HEAD, 1), g1 + pltpu.roll(g1, HEAD, 1))
            dq = _lanes([dq_ref.at[c][rs, :] for c in range(SWA_Q // 2)])
            return _lanes([_rope(dq, tab, -1) * Q_SCALE, _rope(kv_grad(4), tab, -1), kv_grad(6)])
        _dh_epilogue(part_of, dh_ref, db_ref, inv_ref)

    stat = pl.BlockSpec((None, N_QBLK, 8, 128), lambda b: (b, 0, 0, 0))
    return pl.pallas_call(
        body, name="swa_bwd", grid=(B,),
        in_specs=[pl.BlockSpec(memory_space=pltpu.SMEM),
                  pl.BlockSpec((None, SWA_CHUNKS, SEQ, 128), lambda b: (b, 0, 0, 0)),
                  pl.BlockSpec((None, SWA_Q // 2, SEQ, 128), lambda b: (b, 0, 0, 0)), stat, stat,
                  pl.BlockSpec((1, 128), lambda b: (0, 0))],
        out_specs=[pl.BlockSpec((SEQ, W_A + 2 * W_KVA), lambda b: (b, 0)),
                   pl.BlockSpec((8, W_A + 2 * W_KVA), lambda b: (0, 0)),
                   pl.BlockSpec((8, 128), lambda b: (0, 0))],
        out_shape=[jax.ShapeDtypeStruct((B * SEQ, W_A + 2 * W_KVA), MXU),
                   jax.ShapeDtypeStruct((8, W_A + 2 * W_KVA), F32), jax.ShapeDtypeStruct((8, 128), F32)],
        scratch_shapes=[pltpu.VMEM((SWA_CHUNKS, SEQ, 128), F32)],
        compiler_params=_cparams(("arbitrary",)),
    )(sinks, qkva, do, lse, delta, inv)


DILATIONS = (1, 4, 16)
DIL_PAIRS_H = DIL_H // 2


def _stream_rows(d, r, i, n):
    if d == 1:
        return pl.ds(pl.multiple_of(i * BLK, BLK), n)
    return pl.ds(r + i * (BLK * d), n, stride=d)


def _spread_matrix():
    row = lax.broadcasted_iota(jnp.int32, (PICK_ROWS, 128), 0)
    lane = lax.broadcasted_iota(jnp.int32, (PICK_ROWS, 128), 1)
    return ((row < 6) & ((row % 2 == 1) == (lane >= HEAD))).astype(MXU)


def _lanes_to_tokens(v0, v1, spread):
    n = v0.shape[1]
    row = lax.broadcasted_iota(jnp.int32, (PICK_ROWS, n), 0)
    a = jnp.zeros((PICK_ROWS, n), F32)
    for i, (p0, p1) in enumerate(zip(_split3(v0), _split3(v1))):
        a = jnp.where(row == 2 * i, p0.astype(F32), a)
        a = jnp.where(row == 2 * i + 1, p1.astype(F32), a)
    return _dot_t0(a.astype(MXU), spread)


def _tokens_to_lanes(t):
    r = t.T
    return r[0:1, :], r[HEAD:HEAD + 1, :]


def _dil_schedule(body_first, body_next):
    for p, d in enumerate(DILATIONS):
        nblk = SEQ // d // BLK
        if d == 1:
            _interleave([body_first(p, d, 0)])
            def loop(j, c, p=p, d=d):
                _interleave([body_next(p, d, 0, 1 + 3 * j + u) for u in range(3)])
                return c
            lax.fori_loop(0, (nblk - 1) // 3, loop, 0)
        elif nblk > 1:
            def loop(r, c, p=p, d=d, nblk=nblk):
                _interleave([body_first(p, d, r)] + [body_next(p, d, r, i) for i in range(1, nblk)])
                return c
            lax.fori_loop(0, d, loop, 0)
        else:
            def loop(j, c, p=p, d=d):
                _interleave([body_first(p, d, 4 * j + u) for u in range(4)])
                return c
            lax.fori_loop(0, d // 4, loop, 0)


def _dil_fwd(qkvb):
    B = qkvb.shape[0]

    def body(qkv_ref, o_ref):
        band, first = _band_bias(BLK)
        spread = _spread_matrix()

        def block(p, d, rows_q, rows_k, bias):
            nk = bias.shape[0]
            sT = []
            for c in range(DIL_PAIRS_H):
                qc = qkv_ref.at[c][rows_q, :].astype(MXU)
                s2 = _dot_nt(_stack_pair(qkv_ref.at[DIL_PAIRS_H + c][rows_k, :]), qc)
                sT += [s2[0:nk], s2[nk:2 * nk]]
            yield
            sT = _lanes(sT) + bias
            m = jnp.max(sT, axis=0, keepdims=True)
            pT = jnp.exp(sT - m)
            l = jnp.sum(pT, axis=0, keepdims=True)
            pnT = (pT * (1.0 / l)).astype(MXU)
            lse = m + jnp.log(l)
            yield
            for c in range(DIL_PAIRS_H):
                q0, q1 = slice(2 * c * BLK, (2 * c + 1) * BLK), slice((2 * c + 1) * BLK, (2 * c + 2) * BLK)
                p2 = jnp.concatenate([pnT[:, q0], pnT[:, q1]], axis=0)
                o_ref.at[p, c][rows_q, :] = _dot_t0(p2, _stack_pair(qkv_ref.at[2 * DIL_PAIRS_H + c][rows_k, :]))
                o_ref.at[p, DIL_PAIRS_H + c][rows_q, :] = _lanes_to_tokens(lse[:, q0], lse[:, q1], spread)

        def body_first(p, d, r):
            rows = _stream_rows(d, r, 0, BLK)
            return block(p, d, rows, rows, first)

        def body_next(p, d, r, i):
            return block(p, d, _stream_rows(d, r, i, BLK), _stream_rows(d, r, i - 1, 2 * BLK), band)

        _dil_schedule(body_first, body_next)

    return pl.pallas_call(
        body, name="dil_fwd", grid=(B,),
        in_specs=[pl.BlockSpec((None, 6, SEQ, 128), lambda b: (b, 0, 0, 0))],
        out_specs=pl.BlockSpec((None, 3, 4, SEQ, 128), lambda b: (b, 0, 0, 0, 0)),
        out_shape=jax.ShapeDtypeStruct((B, 3, 4, SEQ, 128), F32),
        compiler_params=_cparams(("arbitrary",)),
    )(qkvb)


def _reduce_scatter_ops(send_refs, land_refs, send_sems, recv_sems):
    x, y, c = _my_pos()
    me = 4 * x + 2 * y + c
    n = len(send_refs)

    def to_peer(a, j):
        return pltpu.make_async_remote_copy(
            src_ref=send_refs[a].at[j], dst_ref=land_refs[a].at[me], send_sem=send_sems.at[a, j],
            recv_sem=recv_sems.at[a, me], device_id=_dev_coords(j), device_id_type=MESH)

    def from_peer(a, m):
        return pltpu.make_async_remote_copy(
            src_ref=land_refs[a].at[m], dst_ref=land_refs[a].at[m], send_sem=send_sems.at[a, m],
            recv_sem=recv_sems.at[a, m], device_id=_dev_coords(m), device_id_type=MESH)

    def start():
        for j in range(N_DEV):
            @pl.when(me != j)
            def _(j=j):
                for a in range(n):
                    to_peer(a, j).start()
        for a in range(n):
            land_refs[a][me] = jnp.zeros(land_refs[a].shape[1:], land_refs[a].dtype)

    def finish(own_refs, out_refs):
        for m in range(N_DEV):
            @pl.when(me != m)
            def _(m=m):
                for a in range(n):
                    from_peer(a, m).wait_recv()
        for j in range(N_DEV):
            @pl.when(me != j)
            def _(j=j):
                for a in range(n):
                    to_peer(a, j).wait_send()
        for a in range(n):
            def chunk(i, carry, a=a):
                rs = pl.ds(pl.multiple_of(i * REDUCE_ROWS, REDUCE_ROWS), REDUCE_ROWS)
                g = own_refs[a][rs, :]
                for m in range(N_DEV):
                    g = g + land_refs[a][m, rs, :].astype(F32)
                out_refs[a][rs, :] = g
                return carry
            lax.fori_loop(0, own_refs[a].shape[0] // REDUCE_ROWS, chunk, 0)

    return start, finish


def _dil_bwd(qkvb, dobb, inv, sends, owns):
    B = qkvb.shape[0]
    n_rs = len(sends)

    def body(qkv_ref, dob_ref, inv_ref, *rest):
        send_refs, own_refs = rest[:n_rs], rest[n_rs:2 * n_rs]
        dh_ref, db_ref = rest[2 * n_rs:2 * n_rs + 2]
        out_refs = rest[2 * n_rs + 2:3 * n_rs + 2]
        dq_ref = rest[3 * n_rs + 2]
        land_refs = rest[3 * n_rs + 3:4 * n_rs + 3]
        send_sems, recv_sems = rest[4 * n_rs + 3:]
        rs_start, rs_finish = _reduce_scatter_ops(send_refs, land_refs, send_sems, recv_sems)
        pl.when(pl.program_id(0) == 0)(rs_start)

        band, first = _band_bias(BLK)
        dq_ref[...] = jnp.zeros_like(dq_ref)

        def block(p, d, rows_q, rows_k, bias):
            nk = bias.shape[0]
            lo = lax.broadcasted_iota(jnp.int32, (nk, 128), 1) < HEAD
            qcs, docs, k2s, sT, dpT, lse, delta = [], [], [], [], [], [], []
            for c in range(DIL_PAIRS_H):
                qc = qkv_ref.at[c][rows_q, :].astype(MXU)
                doc = dob_ref.at[c][rows_q, :].astype(MXU)
                k2 = _stack_pair(qkv_ref.at[DIL_PAIRS_H + c][rows_k, :])
                s2 = _dot_nt(k2, qc)
                dp2 = _dot_nt(_stack_pair(qkv_ref.at[2 * DIL_PAIRS_H + c][rows_k, :]), doc)
                sT += [s2[0:nk], s2[nk:2 * nk]]
                dpT += [dp2[0:nk], dp2[nk:2 * nk]]
                lse += _tokens_to_lanes(dob_ref.at[DIL_PAIRS_H + c][rows_q, :])
                delta += _tokens_to_lanes(dob_ref.at[2 * DIL_PAIRS_H + c][rows_q, :])
                qcs.append(qc)
                docs.append(doc)
                k2s.append(k2)
            yield
            pT = jnp.exp(_lanes(sT) + bias - _lanes(lse))
            dsT = pT * (_lanes(dpT) - _lanes(delta))
            dsb, pb = dsT.astype(MXU), pT.astype(MXU)
            yield
            for c in range(DIL_PAIRS_H):
                q0, q1 = slice(2 * c * BLK, (2 * c + 1) * BLK), slice((2 * c + 1) * BLK, (2 * c + 2) * BLK)
                ds2 = jnp.concatenate([dsb[:, q0], dsb[:, q1]], axis=0)
                p2 = jnp.concatenate([pb[:, q0], pb[:, q1]], axis=0)
                dq_ref.at[c][rows_q, :] += _dot_t0(ds2, k2s[c])
                dk2, dv2 = _dot(ds2, qcs[c]), _dot(p2, docs[c])
                dq_ref.at[DIL_PAIRS_H + c][rows_k, :] += jnp.where(lo, dk2[0:nk], dk2[nk:2 * nk])
                dq_ref.at[2 * DIL_PAIRS_H + c][rows_k, :] += jnp.where(lo, dv2[0:nk], dv2[nk:2 * nk])

        def body_first(p, d, r):
            rows = _stream_rows(d, r, 0, BLK)
            return block(p, d, rows, rows, first)

        def body_next(p, d, r, i):
            return block(p, d, _stream_rows(d, r, i, BLK), _stream_rows(d, r, i - 1, 2 * BLK), band)

        _dil_schedule(body_first, body_next)

        def part_of(rs, tab):
            q, k, v = [_lanes([dq_ref.at[2 * j][rs, :], dq_ref.at[2 * j + 1][rs, :]]) for j in range(3)]
            return _lanes([_rope(q, tab, -1) * Q_SCALE, _rope(k, tab, -1), v])
        _dh_epilogue(part_of, dh_ref, db_ref, inv_ref)

        @pl.when(pl.program_id(0) == pl.num_programs(0) - 1)
        def _():
            rs_finish(own_refs, out_refs)

    spec = pl.BlockSpec((None, 6, SEQ, 128), lambda b: (b, 0, 0, 0))
    any_spec = pl.BlockSpec(memory_space=pl.ANY)
    vmem = pl.BlockSpec(memory_space=pltpu.VMEM)
    outs = pl.pallas_call(
        body, name="dil_bwd", grid=(B,),
        in_specs=[spec, spec, pl.BlockSpec((1, 128), lambda b: (0, 0))] + [any_spec] * n_rs + [vmem] * n_rs,
        out_specs=[pl.BlockSpec((SEQ, 3 * W_B), lambda b: (b, 0)), pl.BlockSpec((8, 3 * W_B), lambda b: (0, 0))]
        + [vmem] * n_rs,
        out_shape=[jax.ShapeDtypeStruct((B * SEQ, 3 * W_B), MXU), jax.ShapeDtypeStruct((8, 3 * W_B), F32)]
        + [jax.ShapeDtypeStruct(o.shape, F32) for o in owns],
        scratch_shapes=[pltpu.VMEM((6, SEQ, 128), F32)] + [pltpu.VMEM(s.shape, s.dtype) for s in sends]
        + [pltpu.SemaphoreType.DMA((n_rs, N_DEV)), pltpu.SemaphoreType.DMA((n_rs, N_DEV))],
        compiler_params=_cparams(("arbitrary",)),
    )(qkvb, dobb, inv, *sends, *owns)
    return outs[0], outs[1], outs[2:]


MEM_UNROLL = 4
MEM_PAIRS = MEM_H // 2


def _mem_attn_fwd(qc, mem, w_mem):
    B = qc.shape[0]

    def body(q_ref, mem_ref, w_ref, o_ref, lse_ref, mkv_ref, k2_ref, v2_ref):
        mkv = _dot(mem_ref[...].astype(MXU), w_ref[...])
        mkv_ref[...] = mkv.astype(MXU)
        for c in range(MEM_PAIRS):
            k2_ref[c] = _stack_pair(mkv[:, c * 128:(c + 1) * 128])
            v2_ref[c] = _stack_pair(mkv[:, W_C + c * 128:W_C + (c + 1) * 128])
        lse_ref[...] = jnp.zeros_like(lse_ref)

        def tile(blk):
            rows = pl.ds(pl.multiple_of(blk * BLK, BLK), BLK)
            sT = []
            for c in range(MEM_PAIRS):
                s2 = _dot_nt(k2_ref[c], q_ref.at[c][rows, :])
                sT += [s2[0:MEM_LEN], s2[MEM_LEN:2 * MEM_LEN]]
            yield
            pnT, lse = _softmax_cols(_lanes(sT))
            yield
            for c in range(MEM_PAIRS):
                p2 = jnp.concatenate([pnT[:, 2 * c * BLK:(2 * c + 1) * BLK],
                                      pnT[:, (2 * c + 1) * BLK:(2 * c + 2) * BLK]], axis=0)
                o_ref.at[c][rows, :] = _dot_t0(p2, v2_ref[c])
            for h in range(MEM_H):
                lse_ref.at[blk][h:h + 1, :] = lse[:, h * BLK:(h + 1) * BLK]

        def loop(j, carry):
            _interleave([tile(j * MEM_UNROLL + u) for u in range(MEM_UNROLL)])
            return carry
        lax.fori_loop(0, N_QBLK // MEM_UNROLL, loop, 0)

    return pl.pallas_call(
        body, name="mem_attn_fwd", grid=(B,),
        in_specs=[pl.BlockSpec((None, MEM_PAIRS, SEQ, 128), lambda b: (b, 0, 0, 0)),
                  pl.BlockSpec((None, MEM_LEN, D_MODEL), lambda b: (b, 0, 0)),
                  pl.BlockSpec((D_MODEL, 2 * W_C), lambda b: (0, 0))],
        out_specs=[pl.BlockSpec((None, MEM_PAIRS, SEQ, 128), lambda b: (b, 0, 0, 0)),
                   pl.BlockSpec((None, N_QBLK, 8, 128), lambda b: (b, 0, 0, 0)),
                   pl.BlockSpec((None, MEM_LEN, 2 * W_C), lambda b: (b, 0, 0))],
        out_shape=[jax.ShapeDtypeStruct((B, MEM_PAIRS, SEQ, 128), F32),
                   jax.ShapeDtypeStruct((B, N_QBLK, 8, 128), F32),
                   jax.ShapeDtypeStruct((B, MEM_LEN, 2 * W_C), MXU)],
        scratch_shapes=[pltpu.VMEM((MEM_PAIRS, 2 * MEM_LEN, 128), MXU), pltpu.VMEM((MEM_PAIRS, 2 * MEM_LEN, 128), MXU)],
        compiler_params=_cparams(("arbitrary",)),
    )(qc, mem, w_mem)


def _mem_attn_bwd(qc, mkv, do, lse, delta, mem):
    B = qc.shape[0]

    def body(q_ref, mkv_ref, do_ref, lse_ref, delta_ref, mem_ref, dh_ref, db_ref, dw_ref,
             dq_ref, dmkv_ref, k2_ref, v2_ref):
        @pl.when(pl.program_id(0) == 0)
        def _():
            dw_ref[...] = jnp.zeros_like(dw_ref)
        dmkv_ref[...] = jnp.zeros_like(dmkv_ref)
        for c in range(MEM_PAIRS):
            k2_ref[c] = _stack_pair(mkv_ref[:, c * 128:(c + 1) * 128])
            v2_ref[c] = _stack_pair(mkv_ref[:, W_C + c * 128:W_C + (c + 1) * 128])

        def tile(blk):
            rows = pl.ds(pl.multiple_of(blk * BLK, BLK), BLK)
            qcs, docs, sT, dpT = [], [], [], []
            for c in range(MEM_PAIRS):
                qc_, doc = q_ref.at[c][rows, :], do_ref.at[c][rows, :]
                s2, dp2 = _dot_nt(k2_ref[c], qc_), _dot_nt(v2_ref[c], doc)
                sT += [s2[0:MEM_LEN], s2[MEM_LEN:2 * MEM_LEN]]
                dpT += [dp2[0:MEM_LEN], dp2[MEM_LEN:2 * MEM_LEN]]
                qcs.append(qc_)
                docs.append(doc)
            lse_r = _lanes([lse_ref.at[blk][h:h + 1, :] for h in range(MEM_H)])
            delta_r = _lanes([delta_ref.at[blk][h:h + 1, :] for h in range(MEM_H)])
            yield
            pT = jnp.exp(_lanes(sT) - lse_r)
            dsT = pT * (_lanes(dpT) - delta_r)
            dsb, pb = dsT.astype(MXU), pT.astype(MXU)
            yield
            for c in range(MEM_PAIRS):
                q0, q1 = slice(2 * c * BLK, (2 * c + 1) * BLK), slice((2 * c + 1) * BLK, (2 * c + 2) * BLK)
                ds2 = jnp.concatenate([dsb[:, q0], dsb[:, q1]], axis=0)
                p2 = jnp.concatenate([pb[:, q0], pb[:, q1]], axis=0)
                dq_ref.at[c][rows, :] = _dot_t0(ds2, k2_ref[c])
                dmkv_ref[:, c * 128:(c + 1) * 128] += _pair_rows(_dot(ds2, qcs[c]), MEM_LEN)
                dmkv_ref[:, W_C + c * 128:W_C + (c + 1) * 128] += _pair_rows(_dot(p2, docs[c]), MEM_LEN)

        def loop(j, carry):
            _interleave([tile(j * MEM_UNROLL + u) for u in range(MEM_UNROLL)])
            return carry
        lax.fori_loop(0, N_QBLK // MEM_UNROLL, loop, 0)
        dw_ref[...] += _dot_tn(mem_ref[...], dmkv_ref[...].astype(MXU))
        _dh_epilogue(lambda rs, tab: _lanes([dq_ref.at[c][rs, :] for c in range(MEM_PAIRS)]) * Q_SCALE,
                     dh_ref, db_ref, None)

    stat = pl.BlockSpec((None, N_QBLK, 8, 128), lambda b: (b, 0, 0, 0))
    pairs = pl.BlockSpec((None, MEM_PAIRS, SEQ, 128), lambda b: (b, 0, 0, 0))
    return pl.pallas_call(
        body, name="mem_attn_bwd", grid=(B,),
        in_specs=[pairs, pl.BlockSpec((None, MEM_LEN, 2 * W_C), lambda b: (b, 0, 0)), pairs, stat, stat,
                  pl.BlockSpec((None, MEM_LEN, D_MODEL), lambda b: (b, 0, 0))],
        out_specs=[pl.BlockSpec((SEQ, W_C), lambda b: (b, 0)), pl.BlockSpec((8, W_C), lambda b: (0, 0)),
                   pl.BlockSpec((D_MODEL, 2 * W_C), lambda b: (0, 0))],
        out_shape=[jax.ShapeDtypeStruct((B * SEQ, W_C), MXU), jax.ShapeDtypeStruct((8, W_C), F32),
                   jax.ShapeDtypeStruct((D_MODEL, 2 * W_C), F32)],
        scratch_shapes=[pltpu.VMEM((MEM_PAIRS, SEQ, 128), F32), pltpu.VMEM((MEM_LEN, 2 * W_C), F32),
                        pltpu.VMEM((MEM_PAIRS, 2 * MEM_LEN, 128), MXU), pltpu.VMEM((MEM_PAIRS, 2 * MEM_LEN, 128), MXU)],
        compiler_params=_cparams(("arbitrary",)),
    )(qc, mkv, do, lse, delta, mem)


def _headsum(t, e):
    if MXU == F32:
        return _dot(t, e)
    hi = t.astype(MXU)
    lo = (t - hi.astype(F32)).astype(MXU)
    return _dot(hi, e) + _dot(lo, e)


def _heads_to_rows(t, e):
    return sum(_dot_nt(e, part) for part in _split3(t))


POST_ROWS = 256


def _post(o_a, olse_b, o_c, z, x2, tgt, g, gain, bias, w_out, hsum, hrows):
    T = x2.shape[0]
    tm = 256
    nt = SEQ // tm

    def body(oa_ref, ob_ref, oc_ref, z_ref, x_ref, t_ref, g_ref, gain_ref, bias_ref, w_ref, e_ref, er_ref,
             gx_ref, doa_ref, dela_ref, dobb_ref, doc_ref, delc_ref, dz_ref, dw_ref, small_ref, loss_ref):
        @pl.when(pl.program_id(0) == 0)
        def _():
            dw_ref[...] = jnp.zeros_like(dw_ref)
            small_ref[...] = jnp.zeros_like(small_ref)
            loss_ref[...] = jnp.zeros_like(loss_ref)

        gg = g_ref[...]
        gain_v = gain_ref[...]
        gain_s = gain_v * (1.0 / D_MODEL)
        bias_v = bias_ref[...]
        w = w_ref[...]

        def rms(o):
            rr = lax.rsqrt(jnp.mean(o * o, axis=1, keepdims=True) + RMS_EPS)
            return o * rr, rr

        def rows_of(rs):
            oa = _lanes([oa_ref.at[c][rs, :] for c in range(SWA_Q // 2)])
            (o1, l1), (o4, l4), (o16, l16) = [
                (_lanes([ob_ref.at[p, 0][rs, :], ob_ref.at[p, 1][rs, :]]),
                 _lanes([ob_ref.at[p, 2][rs, :], ob_ref.at[p, 3][rs, :]])) for p in range(3)]
            mx = jnp.maximum(jnp.maximum(l1, l4), l16)
            e1, e4, e16 = jnp.exp(l1 - mx), jnp.exp(l4 - mx), jnp.exp(l16 - mx)
            den = e1 + e4 + e16
            ob = (e1 * o1 + e4 * o4 + e16 * o16) / den
            lse_b = mx + jnp.log(den)
            oc = _lanes([oc_ref.at[c][rs, :] for c in range(MEM_PAIRS)])
            na, ra = rms(oa)
            nb, rb = rms(ob)
            nc, rc = rms(oc)
            n = jnp.concatenate([na, nb, nc], axis=1)
            zz = z_ref[rs, :]
            sig = 0.5 * jnp.tanh(0.5 * zz) + 0.5
            sz = zz * sig
            gs = gg * sz
            u = n * gs
            r = ALPHA * x_ref[rs, :] + _dot(u.astype(MXU), w)
            rc0 = r - jnp.mean(r, axis=1, keepdims=True)
            rstd = lax.rsqrt(jnp.mean(rc0 * rc0, axis=1, keepdims=True) + LN_EPS)
            xhat = rc0 * rstd
            err = xhat * gain_v + bias_v - t_ref[rs, :]
            dxh = err * gain_s
            dr = rstd * (dxh - jnp.mean(dxh, axis=1, keepdims=True)
                         - xhat * jnp.mean(dxh * xhat, axis=1, keepdims=True))
            gx_ref[rs, :] = ALPHA * dr
            drb = dr.astype(MXU)
            du = _dot_nt(drb, w)
            dun = du * n
            dz = dun * (gg * (sig + sz * (1.0 - sig)))
            dz_ref[rs, :] = dz.astype(MXU)
            dn = du * gs

            def branch(lo, hi, nbr, rr):
                dnb = dn[:, lo:hi]
                return rr * (dnb - nbr * jnp.mean(dnb * nbr, axis=1, keepdims=True))

            def to_kernel(dob, o, do_ref, delta_ref):
                wd = dob.shape[1]
                for c in range(wd // 128):
                    do_ref.at[c][rs, :] = dob[:, c * 128:(c + 1) * 128].astype(do_ref.dtype)
                dT = _heads_to_rows(dob * o, er_ref[:, 0:wd])
                for jb in range((rs.stop - rs.start) // BLK):
                    delta_ref[rs.start // BLK + jb] = dT[0:8, jb * BLK:(jb + 1) * BLK]

            to_kernel(branch(0, W_A, na, ra), oa, doa_ref, dela_ref)
            to_kernel(branch(W_A + W_B, D_MIX, nc, rc), oc, doc_ref, delc_ref)
            dob = branch(W_A, W_A + W_B, nb, rb)
            for j, t in enumerate((dob, lse_b, _headsum(dob * ob, e_ref[...]))):
                for c in range(W_B // 128):
                    dobb_ref.at[j * (W_B // 128) + c][rs, :] = t[:, c * 128:(c + 1) * 128]
            csum = lambda t: jnp.sum(t, axis=0, keepdims=True)
            return (u, drb, jnp.sum(err * err), csum(err * xhat), csum(err), csum(dun * sz), csum(dz))

        parts = [rows_of(slice(k * POST_ROWS, (k + 1) * POST_ROWS)) for k in range(tm // POST_ROWS)]
        tot = [sum(p[i] for p in parts) for i in range(2, 7)]
        dw_ref[...] += _dot_tn(jnp.concatenate([p[0] for p in parts], axis=0),
                               jnp.concatenate([p[1] for p in parts], axis=0))
        loss_ref[...] += 0.5 * tot[0] * (1.0 / D_MODEL)
        small_ref[0:1, :] += tot[1] * (1.0 / D_MODEL)
        small_ref[1:2, :] += tot[2] * (1.0 / D_MODEL)
        small_ref[2:3, :] += tot[3]
        small_ref[3:4, :] += tot[4]

    B = T // SEQ
    row = lambda w: pl.BlockSpec((tm, w), lambda i: (i, 0))
    full = lambda a, b: pl.BlockSpec((a, b), lambda i: (0, 0))
    chunked = lambda n: pl.BlockSpec((None, n, tm, 128), lambda i: (i // nt, 0, i % nt, 0))
    stat = pl.BlockSpec((None, tm // BLK, 8, 128), lambda i: (i // nt, i % nt, 0, 0))
    return pl.pallas_call(
        body, name="post_fwd_bwd", grid=(T // tm,),
        in_specs=[chunked(SWA_Q // 2), pl.BlockSpec((None, 3, 4, tm, 128), lambda i: (i // nt, 0, 0, i % nt, 0)),
                  chunked(MEM_PAIRS),
                  row(D_MIX), row(D_MODEL), row(D_MODEL),
                  full(1, D_MIX), full(1, D_MODEL), full(1, D_MODEL), full(D_MIX, D_MODEL), full(W_B, W_B),
                  full(PICK_ROWS, W_A)],
        out_specs=[row(D_MODEL), chunked(SWA_Q // 2), stat, chunked(6), chunked(MEM_PAIRS), stat, row(D_MIX),
                   full(D_MIX, D_MODEL), full(8, D_MODEL), full(8, 128)],
        out_shape=[jax.ShapeDtypeStruct((T, D_MODEL), F32),
                   jax.ShapeDtypeStruct((B, SWA_Q // 2, SEQ, 128), MXU),
                   jax.ShapeDtypeStruct((B, N_QBLK, 8, 128), F32),
                   jax.ShapeDtypeStruct((B, 6, SEQ, 128), F32),
                   jax.ShapeDtypeStruct((B, MEM_PAIRS, SEQ, 128), MXU),
                   jax.ShapeDtypeStruct((B, N_QBLK, 8, 128), F32),
                   jax.ShapeDtypeStruct((T, D_MIX), MXU),
                   jax.ShapeDtypeStruct((D_MIX, D_MODEL), F32),
                   jax.ShapeDtypeStruct((8, D_MODEL), F32),
                   jax.ShapeDtypeStruct((8, 128), F32)],
        compiler_params=_cparams(("arbitrary",)),
    )(o_a, olse_b, o_c, z, x2, tgt, g, gain, bias, w_out, hsum, hrows)


TAIL_TK = 512
TAIL_TN = D_IN // 2
TAIL_TM = 256
REDUCE_ROWS = 128


DH_SPLITS = (0, W_A + 2 * W_KVA, W_A + 2 * W_KVA + 3 * W_B, D_IN - D_MIX, D_IN)


def _tail(xt, dhs, gx1, w_in, small_g):
    T = xt.shape[1]
    dh = dhs[0]
    c0, c1, c2, c3, c4 = DH_SPLITS
    assert c1 < TAIL_TN < c2 and (TAIL_TN - c1) % 128 == 0
    kt = T // TAIL_TK
    ndw = (D_IN // TAIL_TN) * kt
    nsteps = ndw + T // TAIL_TM
    n_pass = D_IN // TAIL_TN
    assert n_pass == 2 and TAIL_TN == 4 * COLS_PER_DEV and kt >= 2
    pay = dh.dtype
    blk_shape = (D_MODEL, COLS_PER_DEV)
    n_half = 2 * n_pass
    n_chip = N_DEV // 2

    def body(xt_ref, a1_ref, b1_ref, b2_ref, c1_ref, z1_ref, a2_ref, b3_ref, c2_ref, z2_ref,
             gx_ref, w_hbm, sg_ref, dx_ref, gin_ref, gsm_ref,
             acc_ref, w_ref, mine_ref, stagea_ref, landa_ref, stageb_ref, landb_ref, own_ref, lsm_ref,
             sa_sems, ra_sems, sb_sems, rb_sems, ss_sems, rs_sems, w_sem):
        s = pl.program_id(0)
        x, y, c = _my_pos()
        me = 4 * x + 2 * y + c
        chip = 2 * x + y

        def to_sibling(q):
            return pltpu.make_async_remote_copy(
                src_ref=stagea_ref.at[q], dst_ref=landa_ref.at[q], send_sem=sa_sems.at[q], recv_sem=ra_sems.at[q],
                device_id=(x, y, 1 - c), device_id_type=MESH)

        def to_owner(q):
            return pltpu.make_async_remote_copy(
                src_ref=stageb_ref.at[q], dst_ref=landb_ref.at[chip], send_sem=sb_sems.at[q],
                recv_sem=rb_sems.at[chip], device_id=(q // 2, q % 2, c), device_id_type=MESH)

        def from_chip(m):
            return pltpu.make_async_remote_copy(
                src_ref=landb_ref.at[m], dst_ref=landb_ref.at[m], send_sem=sb_sems.at[m], recv_sem=rb_sems.at[m],
                device_id=(m // 2, m % 2, c), device_id_type=MESH)

        def is_me(q):
            return (x == q // 2) & (y == q % 2)

        def small_to(j):
            return pltpu.make_async_remote_copy(
                src_ref=sg_ref, dst_ref=lsm_ref.at[me], send_sem=ss_sems.at[j], recv_sem=rs_sems.at[me],
                device_id=_dev_coords(j), device_id_type=MESH)

        def small_from(m):
            return pltpu.make_async_remote_copy(
                src_ref=lsm_ref.at[m], dst_ref=lsm_ref.at[m], send_sem=ss_sems.at[m], recv_sem=rs_sems.at[m],
                device_id=_dev_coords(m), device_id_type=MESH)

        w_copy = pltpu.make_async_copy(w_hbm, w_ref, w_sem)

        @pl.when(s == 0)
        def _():
            w_copy.start()
            for j in range(N_DEV):
                pl.when(me != j)(small_to(j).start)
            lsm_ref[me] = sg_ref[...]
            landb_ref[chip] = jnp.zeros(blk_shape, pay)

        @pl.when(s < ndw)
        def _():
            @pl.when(s % kt == 0)
            def _():
                acc_ref[...] = jnp.zeros_like(acc_ref)
            xt_ = xt_ref[...]
            @pl.when(s < kt)
            def _():
                acc_ref[:, 0:c1] += _dot(xt_, a1_ref[...])
                acc_ref[:, c1:TAIL_TN] += _dot(xt_, b1_ref[...])

            @pl.when(s >= kt)
            def _():
                acc_ref[:, 0:c2 - TAIL_TN] += _dot(xt_, b2_ref[...])
                acc_ref[:, c2 - TAIL_TN:c3 - TAIL_TN] += _dot(xt_, c1_ref[...])
                acc_ref[:, c3 - TAIL_TN:c4 - TAIL_TN] += _dot(xt_, z1_ref[...])

        for p in range(n_pass):
            @pl.when(s == p * kt + kt - 1)
            def _(p=p):
                for cc in range(2):
                    @pl.when(c == cc)
                    def _(cc=cc):
                        for yo in range(2):
                            q = 2 * p + yo
                            same, other = 2 * yo + cc, 2 * yo + 1 - cc
                            mine_ref[q] = acc_ref[:, same * COLS_PER_DEV:(same + 1) * COLS_PER_DEV]
                            stagea_ref[q] = acc_ref[:, other * COLS_PER_DEV:(other + 1) * COLS_PER_DEV].astype(pay)
                for yo in range(2):
                    to_sibling(2 * p + yo).start()

            @pl.when(s == (p + 1) * kt + 1)
            def _(p=p):
                for yo in range(2):
                    q = 2 * p + yo
                    to_sibling(q).wait_recv()

                    def chunk(i, carry, q=q):
                        rs = pl.ds(pl.multiple_of(i * REDUCE_ROWS, REDUCE_ROWS), REDUCE_ROWS)
                        tot = mine_ref[q, rs, :] + landa_ref[q, rs, :].astype(F32)

                        @pl.when(is_me(q))
                        def _():
                            own_ref[rs, :] = tot

                        @pl.when(jnp.logical_not(is_me(q)))
                        def _():
                            stageb_ref[q, rs, :] = tot.astype(pay)
                        return carry
                    lax.fori_loop(0, D_MODEL // REDUCE_ROWS, chunk, 0)
                    pl.when(jnp.logical_not(is_me(q)))(to_owner(q).start)

        @pl.when(s >= ndw)
        def _():
            pl.when(s == ndw)(w_copy.wait)
            dx_ref[...] = (_dot_nt(a2_ref[...], w_ref[:, c0:c1]) + _dot_nt(b3_ref[...], w_ref[:, c1:c2])
                           + _dot_nt(c2_ref[...], w_ref[:, c2:c3]) + _dot_nt(z2_ref[...], w_ref[:, c3:c4])
                           + gx_ref[...])

        @pl.when(s == nsteps - 1)
        def _():
            for m in range(n_chip):
                pl.when(m != chip)(from_chip(m).wait_recv)
            for m in range(N_DEV):
                pl.when(me != m)(small_from(m).wait_recv)
            for q in range(n_half):
                to_sibling(q).wait_send()
                pl.when(jnp.logical_not(is_me(q)))(to_owner(q).wait_send)
            for j in range(N_DEV):
                pl.when(me != j)(small_to(j).wait_send)

            def chunk(i, carry):
                rs = pl.ds(pl.multiple_of(i * REDUCE_ROWS, REDUCE_ROWS), REDUCE_ROWS)
                g = own_ref[rs, :]
                for m in range(n_chip):
                    g = g + landb_ref[m, rs, :].astype(F32)
                gin_ref[rs, :] = g
                return carry
            lax.fori_loop(0, D_MODEL // REDUCE_ROWS, chunk, 0)
            g = lsm_ref[0]
            for m in range(1, N_DEV):
                g = g + lsm_ref[m]
            gsm_ref[...] = g

    dw_step = lambda s: jnp.minimum(s, ndw - 1)
    dx_step = lambda s: jnp.maximum(s - ndw, 0)
    pass0 = lambda s: jnp.minimum(s, kt - 1)
    pass1 = lambda s: jnp.clip(s - kt, 0, kt - 1)
    any_spec = pl.BlockSpec(memory_space=pl.ANY)
    vmem = pl.BlockSpec(memory_space=pltpu.VMEM)
    dma = pltpu.SemaphoreType.DMA
    scratch = [pltpu.VMEM((D_MODEL, TAIL_TN), F32), pltpu.VMEM((D_MODEL, D_IN), w_in.dtype),
               pltpu.VMEM((n_half,) + blk_shape, F32),
               pltpu.VMEM((n_half,) + blk_shape, pay), pltpu.VMEM((n_half,) + blk_shape, pay),
               pltpu.VMEM((n_half,) + blk_shape, pay), pltpu.VMEM((n_chip,) + blk_shape, pay),
               pltpu.VMEM(blk_shape, F32), pltpu.VMEM((N_DEV,) + small_g.shape, F32),
               dma((n_half,)), dma((n_half,)), dma((n_half,)), dma((n_chip,)), dma((N_DEV,)), dma((N_DEV,)), dma]
    return pl.pallas_call(
        body, name="tail_dw_dx_reduce", grid=(nsteps,),
        in_specs=[pl.BlockSpec((D_MODEL, TAIL_TK), lambda s: (0, dw_step(s) % kt)),
                  pl.BlockSpec((TAIL_TK, c1 - c0), lambda s: (pass0(s), 0)),
                  pl.BlockSpec((TAIL_TK, TAIL_TN - c1), lambda s: (pass0(s), 0)),
                  pl.BlockSpec((TAIL_TK, 128), lambda s: (pass1(s), (TAIL_TN - c1) // 128)),
                  pl.BlockSpec((TAIL_TK, c3 - c2), lambda s: (pass1(s), 0)),
                  pl.BlockSpec((TAIL_TK, c4 - c3), lambda s: (pass1(s), 0)),
                  pl.BlockSpec((TAIL_TM, c1 - c0), lambda s: (dx_step(s), 0)),
                  pl.BlockSpec((TAIL_TM, c2 - c1), lambda s: (dx_step(s), 0)),
                  pl.BlockSpec((TAIL_TM, c3 - c2), lambda s: (dx_step(s), 0)),
                  pl.BlockSpec((TAIL_TM, c4 - c3), lambda s: (dx_step(s), 0)),
                  pl.BlockSpec((TAIL_TM, D_MODEL), lambda s: (dx_step(s), 0)),
                  any_spec, vmem],
        out_specs=[pl.BlockSpec((TAIL_TM, D_MODEL), lambda s: (dx_step(s), 0)), vmem, vmem],
        out_shape=[jax.ShapeDtypeStruct((T, D_MODEL), F32), jax.ShapeDtypeStruct(blk_shape, F32),
                   jax.ShapeDtypeStruct(small_g.shape, F32)],
        scratch_shapes=scratch,
        compiler_params=_cparams(("arbitrary",)),
    )(xt, dhs[0], dhs[1], dhs[1], dhs[2], dhs[3], dhs[0], dhs[1], dhs[2], dhs[3], gx1, w_in, small_g)


def _adam_update(grads, params, carried):
    n = len(grads)

    def body(*refs):
        g_refs, p_refs, o_refs = refs[1:1 + n], refs[1 + n:1 + 4 * n], refs[2 + 4 * n:]
        for a in range(n):
            rows = g_refs[a].shape[0]
            cr = REDUCE_ROWS if rows % REDUCE_ROWS == 0 else rows
            flat2 = lambda r: r.at[0] if len(r.shape) == 3 else r
            w_ref, m_ref, v_ref = [flat2(r) for r in p_refs[3 * a:3 * a + 3]]
            go_ref, d_ref, nm_ref, nv_ref = [flat2(r) for r in o_refs[4 * a:4 * a + 4]]

            def chunk(i, carry, cr=cr, g_ref=g_refs[a], w_ref=w_ref, m_ref=m_ref, v_ref=v_ref,
                      go_ref=go_ref, d_ref=d_ref, nm_ref=nm_ref, nv_ref=nv_ref):
                rs = pl.ds(pl.multiple_of(i * cr, cr), cr)
                g = g_ref[rs, :]
                go_ref[rs, :] = g
                d_ref[rs, :], nm_ref[rs, :], nv_ref[rs, :] = _adamw(w_ref[rs, :], g, m_ref[rs, :], v_ref[rs, :])
                return carry
            lax.fori_loop(0, rows // cr, chunk, 0)

    vmem = pl.BlockSpec(memory_space=pltpu.VMEM)
    any_spec = pl.BlockSpec(memory_space=pl.ANY)
    flat = [p for grp in params for p in grp]
    outs = pl.pallas_call(
        body, name="adamw", in_specs=[any_spec] + [vmem] * (4 * n), out_specs=[any_spec] + [vmem] * (4 * n),
        out_shape=[jax.ShapeDtypeStruct(carried.shape, carried.dtype)]
        + [jax.ShapeDtypeStruct(grp[0].shape, F32) for grp in params for _ in range(4)],
        input_output_aliases={0: 0},
        compiler_params=pltpu.CompilerParams(vmem_limit_bytes=VMEM_LIMIT),
    )(carried, *grads, *flat)
    return [outs[1 + 4 * a:5 + 4 * a] for a in range(n)], outs[0]


def _step(x, mem, w_in_s, w_mem_s, w_out_s, b_in, sinks, g, gain, bias, tgt):
    B = x.shape[0]
    T = B * SEQ
    x2 = x.reshape(T, D_MODEL)
    t2 = tgt.reshape(T, D_MODEL)
    tab = _rope_inv()
    lane = jnp.arange(W_A)
    hsum = (lane[:W_B, None] // HEAD == lane[None, :W_B] // HEAD).astype(MXU)
    hrows = (jnp.arange(PICK_ROWS)[:, None] == lane[None, :] // HEAD).astype(MXU)
    me = 4 * lax.axis_index("x") + 2 * lax.axis_index("y") + lax.axis_index("c")

    (w_in_all,) = _gather_weights([w_in_s])
    qkva, qkvb, qc, z, w_in, xt, w_mem_all, w_out_all = _in_proj(x2, w_in_all, b_in, tab, [w_mem_s, w_out_s])
    w_mem = w_mem_all.reshape(D_MODEL, 2 * W_C)
    w_out = w_out_all.reshape(D_MIX, D_MODEL)

    o_a, lse_a = _swa_fwd(qkva, sinks)
    olse_b = _dil_fwd(qkvb)
    o_c, lse_c, mkv = _mem_attn_fwd(qc, mem, w_mem)

    gx1, do_a, delta_a, dobb, do_c, delta_c, dz, dw_out, small, loss = _post(
        o_a, olse_b, o_c, z, x2, t2, g, gain, bias, w_out, hsum, hrows)

    dh_c, db_c, dw_mem = _mem_attn_bwd(qc, mkv, do_c, lse_c, delta_c, mem)
    blocks = [dw_mem.reshape(N_DEV, ROWS_PER_DEV, 2 * W_C), dw_out.reshape(N_DEV, ROWS_PER_DEV, D_MODEL)]
    sends = [b.astype(MXU) for b in blocks]
    owns = [lax.dynamic_index_in_dim(b, me, axis=0, keepdims=False) for b in blocks]
    dh_b, db_b, (g_mem, g_out) = _dil_bwd(qkvb, dobb, tab, sends, owns)
    dh_a, db_a, dsink = _swa_bwd(qkva, do_a, lse_a, delta_a, sinks, tab)

    small_g = _pack_small(dict(b_in=jnp.concatenate([db_a[0], db_b[0], db_c[0], small[3]]), sinks=dsink[:, 0],
                               g=small[2], gain=small[0], bias=small[1], loss=loss[0, 0]))
    grad_x, g_in, g_small = _tail(xt, (dh_a, dh_b, dh_c, dz), gx1, w_in, small_g)
    return grad_x.reshape(B, SEQ, D_MODEL), g_in, g_mem, g_out, g_small


def _my_pos():
    return lax.axis_index("x"), lax.axis_index("y"), lax.axis_index("c")


def _gather_weights(shards):
    n_arr = len(shards)

    def body(*refs):
        ins, outs = refs[0:n_arr], refs[n_arr:2 * n_arr]
        send_sems, recv_sems, local_sems = refs[2 * n_arr:]
        x, y, c = _my_pos()
        me, sibling = (x, y, c), (x, y, 1 - c)
        chips = [(1 - x, y), (x, 1 - y), (1 - x, 1 - y)]

        def slot(a, pos):
            return outs[a].at[4 * pos[0] + 2 * pos[1] + pos[2]]

        def copy(a, k, block, to, src=None):
            return pltpu.make_async_remote_copy(
                src_ref=slot(a, block) if src is None else src, dst_ref=slot(a, block),
                send_sem=send_sems.at[a, k], recv_sem=recv_sems.at[a, k],
                device_id=to, device_id_type=MESH)

        mine = [pltpu.make_async_copy(ins[a], slot(a, me), local_sems.at[a]) for a in range(n_arr)]
        for cp in mine:
            cp.start()
        first = []
        for a in range(n_arr):
            first.append(copy(a, 0, me, sibling, src=ins[a]))
            first += [copy(a, 1 + j, me, (*chip, c), src=ins[a]) for j, chip in enumerate(chips)]
        for cp in first:
            cp.start()
        passed = []
        for j, chip in enumerate(chips):
            for a in range(n_arr):
                copy(a, 1 + j, (*chip, c), me).wait_recv()
                fwd = copy(a, 4 + j, (*chip, c), sibling)
                fwd.start()
                passed.append(fwd)
        for a in range(n_arr):
            copy(a, 0, sibling, me).wait_recv()
            for j, chip in enumerate(chips):
                copy(a, 4 + j, (*chip, 1 - c), me).wait_recv()
        for cp in first + passed:
            cp.wait_send()
        for cp in mine:
            cp.wait()

    any_spec = pl.BlockSpec(memory_space=pl.ANY)
    return pl.pallas_call(
        body, name="gather_weights",
        in_specs=[any_spec] * n_arr, out_specs=[any_spec] * n_arr,
        out_shape=[jax.ShapeDtypeStruct((N_DEV,) + s.shape, s.dtype) for s in shards],
        scratch_shapes=[pltpu.SemaphoreType.DMA((n_arr, 7)), pltpu.SemaphoreType.DMA((n_arr, 7)),
                        pltpu.SemaphoreType.DMA((n_arr,))],
    )(*shards)


def _adamw(w, g, m, v):
    m = ADAM_B1 * m + (1.0 - ADAM_B1) * g
    v = ADAM_B2 * v + (1.0 - ADAM_B2) * (g * g)
    m_hat = m / (1.0 - ADAM_B1 ** ADAM_STEP)
    v_hat = v / (1.0 - ADAM_B2 ** ADAM_STEP)
    delta = -ADAM_LR * (m_hat / (jnp.sqrt(v_hat) + ADAM_EPS) + ADAM_WD * w)
    return delta, m, v


_SMALL_SIZES = (("b_in", D_IN), ("g", D_MIX), ("gain", D_MODEL), ("bias", D_MODEL), ("sinks", SWA_Q), ("loss", 1))


def _pack_small(d):
    flat = jnp.concatenate([jnp.reshape(d[k], (-1,)).astype(F32) if k in d else jnp.zeros((n,), F32)
                            for k, n in _SMALL_SIZES])
    flat = jnp.pad(flat, (0, SMALL_ROWS * 128 - flat.shape[0]))
    return flat.reshape(SMALL_ROWS, 128)


def _unpack_small(p):
    flat = p.reshape(-1)
    out, off = {}, 0
    for k, n in _SMALL_SIZES:
        out[k] = flat[off:off + n].reshape(1, n)
        off += n
    return out


def kernel(x, mem, w_in, b_in, w_mem, attn_sinks, g_branch, w_out, ln_gain, ln_bias, loss_target, m_w_in, m_b_in, m_w_mem, m_attn_sinks, m_g_branch, m_w_out, m_ln_gain, m_ln_bias, v_w_in, v_b_in, v_w_mem, v_attn_sinks, v_g_branch, v_w_out, v_ln_gain, v_ln_bias):
    grad_x, g_in, g_mem, g_out, g_small = _step(
        x, mem, w_in[0].astype(MXU), w_mem[0].astype(MXU), w_out[0].astype(MXU), b_in, attn_sinks[0],
        g_branch, ln_gain, ln_bias, loss_target)

    small_w = _pack_small(dict(b_in=b_in, g=g_branch, gain=ln_gain, bias=ln_bias, sinks=attn_sinks))
    small_m = _pack_small(dict(b_in=m_b_in, g=m_g_branch, gain=m_ln_gain, bias=m_ln_bias, sinks=m_attn_sinks))
    small_v = _pack_small(dict(b_in=v_b_in, g=v_g_branch, gain=v_ln_gain, bias=v_ln_bias, sinks=v_attn_sinks))
    grads = [g_in, g_mem, g_out, g_small]
    params = [(w_in, m_w_in, v_w_in), (w_mem, m_w_mem, v_w_mem), (w_out, m_w_out, v_w_out),
              (small_w, small_m, small_v)]
    res, grad_x = _adam_update(grads, params, grad_x)
    big = res[:3]
    sm = [_unpack_small(r) for r in res[3]]

    def group(i):
        return (big[0][i], sm[i]["b_in"], big[1][i], sm[i]["sinks"], sm[i]["g"], big[2][i],
                sm[i]["gain"], sm[i]["bias"])

    loss = sm[0]["loss"].reshape(())
    return (loss, grad_x, *group(0), *group(1), *group(2), *group(3))
```

```python
import functools
import math

import jax
import jax.numpy as jnp
from jax import lax
from jax.experimental import pallas as pl
from jax.experimental.pallas import tpu as pltpu

F32 = jnp.float32
MXU = jnp.bfloat16

D_MODEL = 1024
SEQ = 2048
HEAD = 64
BLK = 128
SWA_Q, SWA_KV = 8, 2
DIL_H = 4
MEM_H = 4
MEM_LEN = 256
W_A, W_KVA, W_B, W_C = 512, 128, 256, 256
D_MIX = 1024
D_IN = 2816
N_DEV = 8
COLS_PER_DEV = D_IN // N_DEV
ROWS_PER_DEV = D_MODEL // N_DEV
ROPE_THETA = 10000.0
LN_EPS = 1e-5
RMS_EPS = 1e-6
ALPHA = 2.0 ** 0.25
Q_SCALE = HEAD ** -0.5
NEG = -1e30
SMALL_ROWS = 48
VMEM_LIMIT = 56 * 1024 * 1024

ADAM_LR = 0.001
ADAM_B1 = 0.9
ADAM_B2 = 0.999
ADAM_EPS = 1e-08
ADAM_WD = 0.01
ADAM_STEP = 10

MESH = pl.DeviceIdType.MESH


def _cparams(sem=None):
    return pltpu.CompilerParams(dimension_semantics=sem, vmem_limit_bytes=VMEM_LIMIT)


def _dot(a, b):
    return jnp.dot(a, b, preferred_element_type=F32)


def _dot_nt(a, b):
    return lax.dot_general(a, b, (((1,), (1,)), ((), ())), preferred_element_type=F32)


def _dot_t0(a, b):
    return lax.dot_general(a, b, (((0,), (0,)), ((), ())), preferred_element_type=F32)


def _dot_tn(a, b):
    return jnp.dot(a.T.astype(MXU), b, preferred_element_type=F32)


def _rope(t, tab, sign):
    cos, sa, sb = tab
    outs = []
    for c in range(t.shape[1] // 128):
        tc = t[:, c * 128:(c + 1) * 128]
        r = pltpu.roll(tc, 96, 1) * sa + pltpu.roll(tc, 32, 1) * sb
        outs.append(tc * cos + r if sign > 0 else tc * cos - r)
    return outs[0] if len(outs) == 1 else jnp.concatenate(outs, axis=1)


def _rope_inv():
    inv = ROPE_THETA ** (-jnp.arange(0, HEAD, 2, dtype=F32) / HEAD)
    return jnp.tile(inv, 2 * 128 // HEAD)[None, :]


def _rope_tab(pos0, rows, inv):
    pos = (lax.broadcasted_iota(jnp.int32, (rows, 128), 0) + pos0).astype(F32)
    ang = pos * inv
    cos, sin = jnp.cos(ang), jnp.sin(ang)
    first = lax.broadcasted_iota(jnp.int32, (rows, 128), 1) % HEAD < HEAD // 2
    return cos, jnp.where(first, -sin, 0.0), jnp.where(first, 0.0, sin)


def _dev_coords(j):
    return (j >> 2, (j >> 1) & 1, j & 1)


def _in_proj(x2, w_all, b_in, tab, late_shards):
    T = x2.shape[0]
    tm = 512
    n_late = len(late_shards)

    def body(x_ref, wall_ref, b_ref, tab_ref, *rest):
        late_in, rest = rest[:n_late], rest[n_late:]
        qkva_ref, qkvb_ref, qc_ref, z_ref, w_ref, xt_ref, tabo_ref = rest[:7]
        late_out = rest[7:7 + n_late]
        send_sems, recv_sems, local_sems = rest[7 + n_late:]
        step, last = pl.program_id(0), pl.num_programs(0) - 1
        x, y, c = _my_pos()
        me = 4 * x + 2 * y + c

        def to_peer(a, j):
            return pltpu.make_async_remote_copy(
                src_ref=late_in[a], dst_ref=late_out[a].at[me], send_sem=send_sems.at[a, j],
                recv_sem=recv_sems.at[a, me], device_id=_dev_coords(j), device_id_type=MESH)

        def from_peer(a, m):
            return pltpu.make_async_remote_copy(
                src_ref=late_out[a].at[m], dst_ref=late_out[a].at[m], send_sem=send_sems.at[a, m],
                recv_sem=recv_sems.at[a, m], device_id=_dev_coords(m), device_id_type=MESH)

        def mine(a):
            return pltpu.make_async_copy(late_in[a], late_out[a].at[me], local_sems.at[a])

        @pl.when(step == 0)
        def _():
            for a in range(n_late):
                mine(a).start()
                for j in range(N_DEV):
                    pl.when(me != j)(to_peer(a, j).start)
            for j in range(N_DEV):
                w_ref[:, j * COLS_PER_DEV:(j + 1) * COLS_PER_DEV] = wall_ref[j]

        xb = x_ref[...].astype(MXU)
        xt_ref[...] = x_ref[...].T.astype(MXU)
        tab = _rope_tab((step % nt) * tm, tm, tab_ref[...])
        for j in range(3):
            tabo_ref[:, j * 128:(j + 1) * 128] = tab[j]

        def seg(c0, c1):
            return _dot(xb, w_ref[:, c0:c1]) + b_ref[:, c0:c1]

        qa = (_rope(seg(0, 512), tab, 1) * Q_SCALE).astype(MXU)
        for c in range(SWA_Q // 2):
            qkva_ref[c] = qa[:, c * 128:(c + 1) * 128]
        lo = lax.broadcasted_iota(jnp.int32, (tm, 128), 1) < HEAD
        for j, t in enumerate((_rope(seg(512, 640), tab, 1), seg(640, 768))):
            other = pltpu.roll(t, HEAD, 1)
            qkva_ref[4 + 2 * j] = jnp.where(lo, t, other).astype(MXU)
            qkva_ref[5 + 2 * j] = jnp.where(lo, other, t).astype(MXU)
        qkvb = (_rope(seg(768, 1024), tab, 1) * Q_SCALE, _rope(seg(1024, 1280), tab, 1), seg(1280, 1536))
        for j, t in enumerate(qkvb):
            for c in range(2):
                qkvb_ref[2 * j + c] = t[:, c * 128:(c + 1) * 128]
        qc = (seg(1536, 1792) * Q_SCALE).astype(MXU)
        for c in range(MEM_H // 2):
            qc_ref[c] = qc[:, c * 128:(c + 1) * 128]
        z_ref[...] = seg(1792, 2816)

        @pl.when(step == last)
        def _():
            for a in range(n_late):
                mine(a).wait()
                for m in range(N_DEV):
                    pl.when(me != m)(from_peer(a, m).wait_recv)
                for j in range(N_DEV):
                    pl.when(me != j)(to_peer(a, j).wait_send)

    nt = SEQ // tm
    any_spec = pl.BlockSpec(memory_space=pl.ANY)
    chunked = lambda n: pl.BlockSpec((None, n, tm, 128), lambda i: (i // nt, 0, i % nt, 0))
    return pl.pallas_call(
        body, name="in_proj_fwd",
        grid=(T // tm,),
        in_specs=[pl.BlockSpec((tm, D_MODEL), lambda i: (i, 0)),
                  pl.BlockSpec((N_DEV, D_MODEL, COLS_PER_DEV), lambda i: (0, 0, 0)),
                  pl.BlockSpec((1, D_IN), lambda i: (0, 0)),
                  pl.BlockSpec((1, 128), lambda i: (0, 0))] + [any_spec] * n_late,
        out_specs=[chunked(SWA_CHUNKS), chunked(6), chunked(MEM_H // 2),
                   pl.BlockSpec((tm, D_MIX), lambda i: (i, 0)),
                   pl.BlockSpec((D_MODEL, D_IN), lambda i: (0, 0)),
                   pl.BlockSpec((D_MODEL, tm), lambda i: (0, i)),
                   pl.BlockSpec((tm, 384), lambda i: (i, 0))] + [any_spec] * n_late,
        out_shape=[jax.ShapeDtypeStruct((T // SEQ, SWA_CHUNKS, SEQ, 128), MXU),
                   jax.ShapeDtypeStruct((T // SEQ, 6, SEQ, 128), F32),
                   jax.ShapeDtypeStruct((T // SEQ, MEM_H // 2, SEQ, 128), MXU),
                   jax.ShapeDtypeStruct((T, D_MIX), F32),
                   jax.ShapeDtypeStruct((D_MODEL, D_IN), w_all.dtype),
                   jax.ShapeDtypeStruct((D_MODEL, T), MXU),
                   jax.ShapeDtypeStruct((T, 384), F32)]
        + [jax.ShapeDtypeStruct((N_DEV,) + s.shape, s.dtype) for s in late_shards],
        scratch_shapes=[pltpu.SemaphoreType.DMA((n_late, N_DEV)), pltpu.SemaphoreType.DMA((n_late, N_DEV)),
                        pltpu.SemaphoreType.DMA((n_late,))],
        compiler_params=_cparams(("arbitrary",)),
    )(x2, w_all, b_in, tab, *late_shards)


CHAIN = 4


def _band_bias(max_dist):
    kj = lax.broadcasted_iota(jnp.int32, (2 * BLK, BLK), 0)
    qi = lax.broadcasted_iota(jnp.int32, (2 * BLK, BLK), 1)
    dist = qi + BLK - kj
    band = jnp.where((dist >= 0) & (dist <= max_dist), 0.0, NEG).astype(F32)
    k1 = lax.broadcasted_iota(jnp.int32, (BLK, BLK), 0)
    q1 = lax.broadcasted_iota(jnp.int32, (BLK, BLK), 1)
    first = jnp.where((q1 - k1 >= 0) & (q1 - k1 <= max_dist), 0.0, NEG).astype(F32)
    return jnp.concatenate([band] * CHAIN, axis=1), jnp.concatenate([first] * CHAIN, axis=1)


def _lanes(parts):
    return jnp.concatenate(parts, axis=1)


PICK_ROWS = 16


def _stack_pair(t):
    lo = (lax.broadcasted_iota(jnp.int32, t.shape, 1) < HEAD).astype(F32)
    return jnp.concatenate([t * lo, t * (1.0 - lo)], axis=0).astype(MXU)


def _pair_rows(x, n):
    lo = lax.broadcasted_iota(jnp.int32, (n, 128), 1) < HEAD
    return jnp.where(lo, x[0:n], x[n:2 * n])


def _split3(t):
    if MXU == F32:
        return (t,)
    hi = t.astype(MXU)
    r = t - hi.astype(F32)
    mid = r.astype(MXU)
    return hi, mid, (r - mid.astype(F32)).astype(MXU)


def _interleave(tiles):
    tiles = list(tiles)
    while tiles:
        for t in list(tiles):
            try:
                next(t)
            except StopIteration:
                tiles.remove(t)


def _softmax_cols(sT, sinkrow=None):
    m = jnp.max(sT, axis=0, keepdims=True)
    if sinkrow is not None:
        m = jnp.maximum(m, sinkrow)
    pT = jnp.exp(sT - m)
    l = jnp.sum(pT, axis=0, keepdims=True)
    if sinkrow is not None:
        l = l + jnp.exp(sinkrow - m)
    return (pT * (1.0 / l)).astype(MXU), m + jnp.log(l)


SWA_CHUNKS = 8
SWA_UNROLL = 3
N_QBLK = SEQ // BLK


def _swa_fwd(qkva, sinks):
    B = qkva.shape[0]
    G = SWA_Q // SWA_KV

    def body(sink_ref, qkv_ref, o_ref, lse_ref):
        band, first = _band_bias(BLK - 1)
        sinkrows = [_lanes([jnp.full((1, BLK), sink_ref[G * hk + j], F32) for j in range(G)])
                    for hk in range(SWA_KV)]

        def tile(hk, blk, rows_q, rows_k, bias):
            nk = bias.shape[0]
            k2 = _stack_pair(qkv_ref.at[4 + hk][rows_k, :])
            sT = []
            for c in (2 * hk, 2 * hk + 1):
                s2 = _dot_nt(k2, qkv_ref.at[c][rows_q, :])
                sT += [s2[0:nk], s2[nk:2 * nk]]
            yield
            pnT, lse = _softmax_cols(_lanes(sT) + bias, sinkrows[hk])
            yield
            v2 = _stack_pair(qkv_ref.at[6 + hk][rows_k, :])
            for j, c in enumerate((2 * hk, 2 * hk + 1)):
                p2 = jnp.concatenate([pnT[:, 2 * j * BLK:(2 * j + 1) * BLK],
                                      pnT[:, (2 * j + 1) * BLK:(2 * j + 2) * BLK]], axis=0)
                o_ref.at[c][rows_q, :] = _dot_t0(p2, v2)
            for j in range(G):
                lse_ref.at[blk][G * hk + j:G * hk + j + 1, :] = lse[:, j * BLK:(j + 1) * BLK]

        def tiles_at(i):
            r0 = pl.multiple_of(i * BLK, BLK)
            rk = pl.multiple_of(i * BLK - BLK, BLK)
            return [tile(hk, i, pl.ds(r0, BLK), pl.ds(rk, 2 * BLK), band) for hk in range(SWA_KV)]

        _interleave([tile(hk, 0, pl.ds(0, BLK), pl.ds(0, BLK), first) for hk in range(SWA_KV)])

        def loop(j, carry):
            _interleave([t for u in range(SWA_UNROLL) for t in tiles_at(1 + j * SWA_UNROLL + u)])
            return carry
        lax.fori_loop(0, (N_QBLK - 1) // SWA_UNROLL, loop, 0)

    return pl.pallas_call(
        body, name="swa_fwd", grid=(B,),
        in_specs=[pl.BlockSpec(memory_space=pltpu.SMEM),
                  pl.BlockSpec((None, SWA_CHUNKS, SEQ, 128), lambda b: (b, 0, 0, 0))],
        out_specs=[pl.BlockSpec((None, SWA_Q // 2, SEQ, 128), lambda b: (b, 0, 0, 0)),
                   pl.BlockSpec((None, N_QBLK, 8, 128), lambda b: (b, 0, 0, 0))],
        out_shape=[jax.ShapeDtypeStruct((B, SWA_Q // 2, SEQ, 128), F32),
                   jax.ShapeDtypeStruct((B, N_QBLK, 8, 128), F32)],
        compiler_params=_cparams(("arbitrary",)),
    )(sinks, qkva)


EP_ROWS = BLK


def _dh_tile(part_of, dh_ref, db_ref, tab_ref, blk, live=None):
    rs = pl.ds(pl.multiple_of(blk * EP_ROWS, EP_ROWS), EP_ROWS)
    tab = None if tab_ref is None else tuple(tab_ref[rs, j * 128:(j + 1) * 128] for j in range(3))
    part = part_of(rs, tab)
    yield
    dh_ref[rs, :] = part.astype(dh_ref.dtype)
    psum = jnp.sum(part, axis=0, keepdims=True)
    db_ref[0:1, :] += psum if live is None else psum * live
    yield


def _dh_tiles_behind(part_of, dh_ref, db_ref, tab_ref, j, unroll):
    live = (j > 0).astype(F32)
    return [_dh_tile(part_of, dh_ref, db_ref, tab_ref, jnp.where(j > 0, unroll * (j - 1) + u, 0), live)
            for u in range(unroll)]


def _swa_bwd(qkva, do, lse, delta, sinks, inv):
    B = qkva.shape[0]
    G = SWA_Q // SWA_KV

    def body(sink_ref, qkv_ref, do_ref, lse_ref, delta_ref, inv_ref, dh_ref, db_ref, dsink_ref, dq_ref):
        band, first = _band_bias(BLK - 1)
        sinkrows = [_lanes([jnp.full((1, BLK), sink_ref[G * hk + j], F32) for j in range(G)])
                    for hk in range(SWA_KV)]

        @pl.when(pl.program_id(0) == 0)
        def _():
            dsink_ref[...] = jnp.zeros_like(dsink_ref)
            db_ref[...] = jnp.zeros_like(db_ref)
        for c in range(4, SWA_CHUNKS):
            dq_ref[c] = jnp.zeros((SEQ, 128), F32)

        def tile(hk, blk, rows_q, rows_k, bias, accs):
            nk = bias.shape[0]
            k2 = _stack_pair(qkv_ref.at[4 + hk][rows_k, :])
            v2 = _stack_pair(qkv_ref.at[6 + hk][rows_k, :])
            qcs, docs, sT, dpT = [], [], [], []
            for c in (2 * hk, 2 * hk + 1):
                qc, doc = qkv_ref.at[c][rows_q, :], do_ref.at[c][rows_q, :]
                s2, dp2 = _dot_nt(k2, qc), _dot_nt(v2, doc)
                sT += [s2[0:nk], s2[nk:2 * nk]]
                dpT += [dp2[0:nk], dp2[nk:2 * nk]]
                qcs.append(qc)
                docs.append(doc)
            heads = slice(G * hk, G * hk + G)
            lse_r = _lanes([lse_ref.at[blk][h:h + 1, :] for h in range(G * hk, G * hk + G)])
            delta_r = _lanes([delta_ref.at[blk][h:h + 1, :] for h in range(G * hk, G * hk + G)])
            yield
            pT = jnp.exp(_lanes(sT) + bias - lse_r)
            dsT = pT * (_lanes(dpT) - delta_r)
            dsb, pb = dsT.astype(MXU), pT.astype(MXU)
            accs[hk] = accs[hk] - jnp.exp(sinkrows[hk] - lse_r) * delta_r
            yield
            dk2 = dv2 = None
            for j, c in enumerate((2 * hk, 2 * hk + 1)):
                q0, q1 = slice(2 * j * BLK, (2 * j + 1) * BLK), slice((2 * j + 1) * BLK, (2 * j + 2) * BLK)
                ds2 = jnp.concatenate([dsb[:, q0], dsb[:, q1]], axis=0)
                p2 = jnp.concatenate([pb[:, q0], pb[:, q1]], axis=0)
                dq_ref.at[c][rows_q, :] = _dot_t0(ds2, k2)
                dk2 = _dot(ds2, qcs[j]) if dk2 is None else dk2 + _dot(ds2, qcs[j])
                dv2 = _dot(p2, docs[j]) if dv2 is None else dv2 + _dot(p2, docs[j])
            dq_ref.at[4 + hk][rows_k, :] += _pair_rows(dk2, nk)
            dq_ref.at[6 + hk][rows_k, :] += _pair_rows(dv2, nk)

        def run(tiles_of, accs):
            accs = list(accs)
            _interleave(tiles_of(accs))
            return tuple(accs)

        zero = jnp.zeros((1, G * BLK), F32)
        accs = run(lambda a: [tile(hk, 0, pl.ds(0, BLK), pl.ds(0, BLK), first, a) for hk in range(SWA_KV)],
                   (zero,) * SWA_KV)

        def part_of(rs, tab):
            lo = lax.broadcasted_iota(jnp.int32, (EP_ROWS, 128), 1) < HEAD

            def kv_grad(c):
                g0, g1 = dq_ref.at[c][rs, :], dq_ref.at[c + 1][rs, :]
                return jnp.where(lo, g0 + pltpu.roll(g0, HEAD, 1), g1 + pltpu.roll(g1, HEAD, 1))
            dq = _lanes([dq_ref.at[c][rs, :] for c in range(SWA_Q // 2)])
            return _lanes([_rope(dq, tab, -1) * Q_SCALE, _rope(kv_grad(4), tab, -1), kv_grad(6)])

        trips = (N_QBLK - 1) // SWA_UNROLL

        def loop(j, accs):
            def tiles_of(a):
                out = []
                for u in range(SWA_UNROLL):
                    i = 1 + j * SWA_UNROLL + u
                    r0 = pl.multiple_of(i * BLK, BLK)
                    rk = pl.multiple_of(i * BLK - BLK, BLK)
                    out += [tile(hk, i, pl.ds(r0, BLK), pl.ds(rk, 2 * BLK), band, a) for hk in range(SWA_KV)]
                return out + _dh_tiles_behind(part_of, dh_ref, db_ref, inv_ref, j, SWA_UNROLL)
            return run(tiles_of, accs)
        accs = lax.fori_loop(0, trips, loop, accs)
        _interleave([_dh_tile(part_of, dh_ref, db_ref, inv_ref, blk)
                     for blk in range(SWA_UNROLL * (trips - 1), N_QBLK)])
        for hk in range(SWA_KV):
            for j in range(G):
                tot = jnp.sum(accs[hk][:, j * BLK:(j + 1) * BLK], axis=1, keepdims=True)
                dsink_ref[G * hk + j:G * hk + j + 1, :] += jnp.broadcast_to(tot, (1, 128))

    stat = pl.BlockSpec((None, N_QBLK, 8, 128), lambda b: (b, 0, 0, 0))
    return pl.pallas_call(
        body, name="swa_bwd", grid=(B,),
        in_specs=[pl.BlockSpec(memory_space=pltpu.SMEM),
                  pl.BlockSpec((None, SWA_CHUNKS, SEQ, 128), lambda b: (b, 0, 0, 0)),
                  pl.BlockSpec((None, SWA_Q // 2, SEQ, 128), lambda b: (b, 0, 0, 0)), stat, stat,
                  pl.BlockSpec((SEQ, 384), lambda b: (0, 0))],
        out_specs=[pl.BlockSpec((SEQ, W_A + 2 * W_KVA), lambda b: (b, 0)),
                   pl.BlockSpec((8, W_A + 2 * W_KVA), lambda b: (0, 0)),
                   pl.BlockSpec((8, 128), lambda b: (0, 0))],
        out_shape=[jax.ShapeDtypeStruct((B * SEQ, W_A + 2 * W_KVA), MXU),
                   jax.ShapeDtypeStruct((8, W_A + 2 * W_KVA), F32), jax.ShapeDtypeStruct((8, 128), F32)],
        scratch_shapes=[pltpu.VMEM((SWA_CHUNKS, SEQ, 128), F32)],
        compiler_params=_cparams(("arbitrary",)),
    )(sinks, qkva, do, lse, delta, inv)


DILATIONS = (1, 4, 16)
DIL_PAIRS_H = DIL_H // 2


def _stream_rows(d, r, i, n):
    if d == 1:
        return pl.ds(pl.multiple_of(i * BLK, BLK), n)
    return pl.ds(r + i * (BLK * d), n, stride=d)


def _spread_matrix():
    row = lax.broadcasted_iota(jnp.int32, (PICK_ROWS, 128), 0)
    lane = lax.broadcasted_iota(jnp.int32, (PICK_ROWS, 128), 1)
    return ((row < 6) & ((row % 2 == 1) == (lane >= HEAD))).astype(MXU)


def _lanes_to_tokens(v0, v1, spread):
    n = v0.shape[1]
    row = lax.broadcasted_iota(jnp.int32, (PICK_ROWS, n), 0)
    a = jnp.zeros((PICK_ROWS, n), F32)
    for i, (p0, p1) in enumerate(zip(_split3(v0), _split3(v1))):
        a = jnp.where(row == 2 * i, p0.astype(F32), a)
        a = jnp.where(row == 2 * i + 1, p1.astype(F32), a)
    return _dot_t0(a.astype(MXU), spread)


def _tokens_to_lanes(t):
    r = t.T
    return r[0:1, :], r[HEAD:HEAD + 1, :]


DIL_UNROLL = 3


def _dil_schedule(body_first, body_next, behind=None):
    for p, d in sorted(enumerate(DILATIONS), key=lambda pd: -pd[1]):
        nblk = SEQ // d // BLK
        if d == 1:
            _interleave([body_first(p, d, 0)])
            def loop(j, c, p=p, d=d):
                _interleave([body_next(p, d, 0, 1 + DIL_UNROLL * j + u) for u in range(DIL_UNROLL)]
                            + (behind(j) if behind else []))
                return c
            lax.fori_loop(0, (nblk - 1) // DIL_UNROLL, loop, 0)
        elif nblk > 1:
            def loop(r, c, p=p, d=d, nblk=nblk):
                _interleave([body_first(p, d, r)] + [body_next(p, d, r, i) for i in range(1, nblk)])
                return c
            lax.fori_loop(0, d, loop, 0)
        else:
            def loop(j, c, p=p, d=d):
                _interleave([body_first(p, d, 4 * j + u) for u in range(4)])
                return c
            lax.fori_loop(0, d // 4, loop, 0)


def _dil_fwd(qkvb):
    B = qkvb.shape[0]

    def body(qkv_ref, o_ref):
        band, first = _band_bias(BLK)
        spread = _spread_matrix()

        def block(p, d, rows_q, rows_k, bias):
            nk = bias.shape[0]
            sT = []
            for c in range(DIL_PAIRS_H):
                qc = qkv_ref.at[c][rows_q, :].astype(MXU)
                s2 = _dot_nt(_stack_pair(qkv_ref.at[DIL_PAIRS_H + c][rows_k, :]), qc)
                sT += [s2[0:nk], s2[nk:2 * nk]]
            yield
            sT = _lanes(sT) + bias
            m = jnp.max(sT, axis=0, keepdims=True)
            pT = jnp.exp(sT - m)
            l = jnp.sum(pT, axis=0, keepdims=True)
            pnT = (pT * (1.0 / l)).astype(MXU)
            lse = m + jnp.log(l)
            yield
            for c in range(DIL_PAIRS_H):
                q0, q1 = slice(2 * c * BLK, (2 * c + 1) * BLK), slice((2 * c + 1) * BLK, (2 * c + 2) * BLK)
                p2 = jnp.concatenate([pnT[:, q0], pnT[:, q1]], axis=0)
                o_ref.at[p, c][rows_q, :] = _dot_t0(p2, _stack_pair(qkv_ref.at[2 * DIL_PAIRS_H + c][rows_k, :]))
                o_ref.at[p, DIL_PAIRS_H + c][rows_q, :] = _lanes_to_tokens(lse[:, q0], lse[:, q1], spread)

        def body_first(p, d, r):
            rows = _stream_rows(d, r, 0, BLK)
            return block(p, d, rows, rows, first)

        def body_next(p, d, r, i):
            return block(p, d, _stream_rows(d, r, i, BLK), _stream_rows(d, r, i - 1, 2 * BLK), band)

        _dil_schedule(body_first, body_next)

    return pl.pallas_call(
        body, name="dil_fwd", grid=(B,),
        in_specs=[pl.BlockSpec((None, 6, SEQ, 128), lambda b: (b, 0, 0, 0))],
        out_specs=pl.BlockSpec((None, 3, 4, SEQ, 128), lambda b: (b, 0, 0, 0, 0)),
        out_shape=jax.ShapeDtypeStruct((B, 3, 4, SEQ, 128), F32),
        compiler_params=_cparams(("arbitrary",)),
    )(qkvb)


def _reduce_scatter_ops(send_refs, land_refs, send_sems, recv_sems):
    x, y, c = _my_pos()
    me = 4 * x + 2 * y + c
    n = len(send_refs)

    def to_peer(a, j):
        return pltpu.make_async_remote_copy(
            src_ref=send_refs[a].at[j], dst_ref=land_refs[a].at[me], send_sem=send_sems.at[a, j],
            recv_sem=recv_sems.at[a, me], device_id=_dev_coords(j), device_id_type=MESH)

    def from_peer(a, m):
        return pltpu.make_async_remote_copy(
            src_ref=land_refs[a].at[m], dst_ref=land_refs[a].at[m], send_sem=send_sems.at[a, m],
            recv_sem=recv_sems.at[a, m], device_id=_dev_coords(m), device_id_type=MESH)

    def start():
        for j in range(N_DEV):
            @pl.when(me != j)
            def _(j=j):
                for a in range(n):
                    to_peer(a, j).start()
        for a in range(n):
            land_refs[a][me] = jnp.zeros(land_refs[a].shape[1:], land_refs[a].dtype)

    def finish(own_refs, out_refs):
        for m in range(N_DEV):
            @pl.when(me != m)
            def _(m=m):
                for a in range(n):
                    from_peer(a, m).wait_recv()
        for j in range(N_DEV):
            @pl.when(me != j)
            def _(j=j):
                for a in range(n):
                    to_peer(a, j).wait_send()
        for a in range(n):
            def chunk(i, carry, a=a):
                rs = pl.ds(pl.multiple_of(i * REDUCE_ROWS, REDUCE_ROWS), REDUCE_ROWS)
                g = own_refs[a][rs, :]
                for m in range(N_DEV):
                    g = g + land_refs[a][m, rs, :].astype(F32)
                out_refs[a][rs, :] = g
                return carry
            lax.fori_loop(0, own_refs[a].shape[0] // REDUCE_ROWS, chunk, 0)

    return start, finish


def _dil_bwd(qkvb, dobb, inv, sends, owns):
    B = qkvb.shape[0]
    n_rs = len(sends)

    def body(qkv_ref, dob_ref, inv_ref, *rest):
        send_refs, own_refs = rest[:n_rs], rest[n_rs:2 * n_rs]
        dh_ref, db_ref = rest[2 * n_rs:2 * n_rs + 2]
        out_refs = rest[2 * n_rs + 2:3 * n_rs + 2]
        dq_ref = rest[3 * n_rs + 2]
        land_refs = rest[3 * n_rs + 3:4 * n_rs + 3]
        send_sems, recv_sems = rest[4 * n_rs + 3:]
        rs_start, rs_finish = _reduce_scatter_ops(send_refs, land_refs, send_sems, recv_sems)
        pl.when(pl.program_id(0) == 0)(rs_start)

        band, first = _band_bias(BLK)
        dq_ref[...] = jnp.zeros_like(dq_ref)

        @pl.when(pl.program_id(0) == 0)
        def _():
            db_ref[...] = jnp.zeros_like(db_ref)

        def block(p, d, rows_q, rows_k, bias):
            nk = bias.shape[0]
            lo = lax.broadcasted_iota(jnp.int32, (nk, 128), 1) < HEAD
            qcs, docs, k2s, sT, dpT, lse, delta = [], [], [], [], [], [], []
            for c in range(DIL_PAIRS_H):
                qc = qkv_ref.at[c][rows_q, :].astype(MXU)
                doc = dob_ref.at[c][rows_q, :].astype(MXU)
                k2 = _stack_pair(qkv_ref.at[DIL_PAIRS_H + c][rows_k, :])
                s2 = _dot_nt(k2, qc)
                dp2 = _dot_nt(_stack_pair(qkv_ref.at[2 * DIL_PAIRS_H + c][rows_k, :]), doc)
                sT += [s2[0:nk], s2[nk:2 * nk]]
                dpT += [dp2[0:nk], dp2[nk:2 * nk]]
                lse += _tokens_to_lanes(dob_ref.at[DIL_PAIRS_H + c][rows_q, :])
                delta += _tokens_to_lanes(dob_ref.at[2 * DIL_PAIRS_H + c][rows_q, :])
                qcs.append(qc)
                docs.append(doc)
                k2s.append(k2)
            yield
            pT = jnp.exp(_lanes(sT) + bias - _lanes(lse))
            dsT = pT * (_lanes(dpT) - _lanes(delta))
            dsb, pb = dsT.astype(MXU), pT.astype(MXU)
            yield
            for c in range(DIL_PAIRS_H):
                q0, q1 = slice(2 * c * BLK, (2 * c + 1) * BLK), slice((2 * c + 1) * BLK, (2 * c + 2) * BLK)
                ds2 = jnp.concatenate([dsb[:, q0], dsb[:, q1]], axis=0)
                p2 = jnp.concatenate([pb[:, q0], pb[:, q1]], axis=0)
                dq_ref.at[c][rows_q, :] += _dot_t0(ds2, k2s[c])
                dk2, dv2 = _dot(ds2, qcs[c]), _dot(p2, docs[c])
                dq_ref.at[DIL_PAIRS_H + c][rows_k, :] += jnp.where(lo, dk2[0:nk], dk2[nk:2 * nk])
                dq_ref.at[2 * DIL_PAIRS_H + c][rows_k, :] += jnp.where(lo, dv2[0:nk], dv2[nk:2 * nk])

        def body_first(p, d, r):
            rows = _stream_rows(d, r, 0, BLK)
            return block(p, d, rows, rows, first)

        def body_next(p, d, r, i):
            return block(p, d, _stream_rows(d, r, i, BLK), _stream_rows(d, r, i - 1, 2 * BLK), band)

        def part_of(rs, tab):
            q, k, v = [_lanes([dq_ref.at[2 * j][rs, :], dq_ref.at[2 * j + 1][rs, :]]) for j in range(3)]
            return _lanes([_rope(q, tab, -1) * Q_SCALE, _rope(k, tab, -1), v])

        _dil_schedule(body_first, body_next,
                      lambda j: _dh_tiles_behind(part_of, dh_ref, db_ref, inv_ref, j, DIL_UNROLL))
        _interleave([_dh_tile(part_of, dh_ref, db_ref, inv_ref, blk)
                     for blk in range(DIL_UNROLL * ((N_QBLK - 1) // DIL_UNROLL - 1), N_QBLK)])

        @pl.when(pl.program_id(0) == pl.num_programs(0) - 1)
        def _():
            rs_finish(own_refs, out_refs)

    spec = pl.BlockSpec((None, 6, SEQ, 128), lambda b: (b, 0, 0, 0))
    any_spec = pl.BlockSpec(memory_space=pl.ANY)
    vmem = pl.BlockSpec(memory_space=pltpu.VMEM)
    outs = pl.pallas_call(
        body, name="dil_bwd", grid=(B,),
        in_specs=[spec, spec, pl.BlockSpec((SEQ, 384), lambda b: (0, 0))] + [any_spec] * n_rs + [vmem] * n_rs,
        out_specs=[pl.BlockSpec((SEQ, 3 * W_B), lambda b: (b, 0)), pl.BlockSpec((8, 3 * W_B), lambda b: (0, 0))]
        + [vmem] * n_rs,
        out_shape=[jax.ShapeDtypeStruct((B * SEQ, 3 * W_B), MXU), jax.ShapeDtypeStruct((8, 3 * W_B), F32)]
        + [jax.ShapeDtypeStruct(o.shape, F32) for o in owns],
        scratch_shapes=[pltpu.VMEM((6, SEQ, 128), F32)] + [pltpu.VMEM(s.shape, s.dtype) for s in sends]
        + [pltpu.SemaphoreType.DMA((n_rs, N_DEV)), pltpu.SemaphoreType.DMA((n_rs, N_DEV))],
        compiler_params=_cparams(("arbitrary",)),
    )(qkvb, dobb, inv, *sends, *owns)
    return outs[0], outs[1], outs[2:]


MEM_UNROLL = 4
MEM_PAIRS = MEM_H // 2


def _mem_attn_fwd(qc, mem, w_mem):
    B = qc.shape[0]

    def body(q_ref, mem_ref, w_ref, o_ref, lse_ref, mkv_ref, k2_ref, v2_ref):
        mkv = _dot(mem_ref[...].astype(MXU), w_ref[...])
        mkv_ref[...] = mkv.astype(MXU)
        for c in range(MEM_PAIRS):
            k2_ref[c] = _stack_pair(mkv[:, c * 128:(c + 1) * 128])
            v2_ref[c] = _stack_pair(mkv[:, W_C + c * 128:W_C + (c + 1) * 128])
        lse_ref[...] = jnp.zeros_like(lse_ref)

        def tile(blk):
            rows = pl.ds(pl.multiple_of(blk * BLK, BLK), BLK)
            sT = []
            for c in range(MEM_PAIRS):
                s2 = _dot_nt(k2_ref[c], q_ref.at[c][rows, :])
                sT += [s2[0:MEM_LEN], s2[MEM_LEN:2 * MEM_LEN]]
            yield
            pnT, lse = _softmax_cols(_lanes(sT))
            yield
            for c in range(MEM_PAIRS):
                p2 = jnp.concatenate([pnT[:, 2 * c * BLK:(2 * c + 1) * BLK],
                                      pnT[:, (2 * c + 1) * BLK:(2 * c + 2) * BLK]], axis=0)
                o_ref.at[c][rows, :] = _dot_t0(p2, v2_ref[c])
            for h in range(MEM_H):
                lse_ref.at[blk][h:h + 1, :] = lse[:, h * BLK:(h + 1) * BLK]

        def loop(j, carry):
            _interleave([tile(j * MEM_UNROLL + u) for u in range(MEM_UNROLL)])
            return carry
        lax.fori_loop(0, N_QBLK // MEM_UNROLL, loop, 0)

    return pl.pallas_call(
        body, name="mem_attn_fwd", grid=(B,),
        in_specs=[pl.BlockSpec((None, MEM_PAIRS, SEQ, 128), lambda b: (b, 0, 0, 0)),
                  pl.BlockSpec((None, MEM_LEN, D_MODEL), lambda b: (b, 0, 0)),
                  pl.BlockSpec((D_MODEL, 2 * W_C), lambda b: (0, 0))],
        out_specs=[pl.BlockSpec((None, MEM_PAIRS, SEQ, 128), lambda b: (b, 0, 0, 0)),
                   pl.BlockSpec((None, N_QBLK, 8, 128), lambda b: (b, 0, 0, 0)),
                   pl.BlockSpec((None, MEM_LEN, 2 * W_C), lambda b: (b, 0, 0))],
        out_shape=[jax.ShapeDtypeStruct((B, MEM_PAIRS, SEQ, 128), F32),
                   jax.ShapeDtypeStruct((B, N_QBLK, 8, 128), F32),
                   jax.ShapeDtypeStruct((B, MEM_LEN, 2 * W_C), MXU)],
        scratch_shapes=[pltpu.VMEM((MEM_PAIRS, 2 * MEM_LEN, 128), MXU), pltpu.VMEM((MEM_PAIRS, 2 * MEM_LEN, 128), MXU)],
        compiler_params=_cparams(("arbitrary",)),
    )(qc, mem, w_mem)


def _mem_attn_bwd(qc, mkv, do, lse, delta, mem):
    B = qc.shape[0]

    def body(q_ref, mkv_ref, do_ref, lse_ref, delta_ref, mem_ref, dh_ref, db_ref, dw_ref,
             dq_ref, dmkv_ref, k2_ref, v2_ref):
        @pl.when(pl.program_id(0) == 0)
        def _():
            dw_ref[...] = jnp.zeros_like(dw_ref)
            db_ref[...] = jnp.zeros_like(db_ref)
        dmkv_ref[...] = jnp.zeros_like(dmkv_ref)
        dq_ref[...] = jnp.zeros_like(dq_ref)
        for c in range(MEM_PAIRS):
            k2_ref[c] = _stack_pair(mkv_ref[:, c * 128:(c + 1) * 128])
            v2_ref[c] = _stack_pair(mkv_ref[:, W_C + c * 128:W_C + (c + 1) * 128])

        def tile(blk):
            rows = pl.ds(pl.multiple_of(blk * BLK, BLK), BLK)
            qcs, docs, sT, dpT = [], [], [], []
            for c in range(MEM_PAIRS):
                qc_, doc = q_ref.at[c][rows, :], do_ref.at[c][rows, :]
                s2, dp2 = _dot_nt(k2_ref[c], qc_), _dot_nt(v2_ref[c], doc)
                sT += [s2[0:MEM_LEN], s2[MEM_LEN:2 * MEM_LEN]]
                dpT += [dp2[0:MEM_LEN], dp2[MEM_LEN:2 * MEM_LEN]]
                qcs.append(qc_)
                docs.append(doc)
            lse_r = _lanes([lse_ref.at[blk][h:h + 1, :] for h in range(MEM_H)])
            delta_r = _lanes([delta_ref.at[blk][h:h + 1, :] for h in range(MEM_H)])
            yield
            pT = jnp.exp(_lanes(sT) - lse_r)
            dsT = pT * (_lanes(dpT) - delta_r)
            dsb, pb = dsT.astype(MXU), pT.astype(MXU)
            yield
            for c in range(MEM_PAIRS):
                q0, q1 = slice(2 * c * BLK, (2 * c + 1) * BLK), slice((2 * c + 1) * BLK, (2 * c + 2) * BLK)
                ds2 = jnp.concatenate([dsb[:, q0], dsb[:, q1]], axis=0)
                p2 = jnp.concatenate([pb[:, q0], pb[:, q1]], axis=0)
                dq_ref.at[c][rows, :] = _dot_t0(ds2, k2_ref[c])
                dmkv_ref[:, c * 128:(c + 1) * 128] += _pair_rows(_dot(ds2, qcs[c]), MEM_LEN)
                dmkv_ref[:, W_C + c * 128:W_C + (c + 1) * 128] += _pair_rows(_dot(p2, docs[c]), MEM_LEN)

        def part_of(rs, tab):
            return _lanes([dq_ref.at[c][rs, :] for c in range(MEM_PAIRS)]) * Q_SCALE

        trips = N_QBLK // MEM_UNROLL

        def loop(j, carry):
            _interleave([tile(j * MEM_UNROLL + u) for u in range(MEM_UNROLL)]
                        + _dh_tiles_behind(part_of, dh_ref, db_ref, None, j, MEM_UNROLL))
            return carry
        lax.fori_loop(0, trips, loop, 0)
        _interleave([_dh_tile(part_of, dh_ref, db_ref, None, blk) for blk in range(MEM_UNROLL * (trips - 1), N_QBLK)])
        dw_ref[...] += _dot_tn(mem_ref[...], dmkv_ref[...].astype(MXU))

    stat = pl.BlockSpec((None, N_QBLK, 8, 128), lambda b: (b, 0, 0, 0))
    pairs = pl.BlockSpec((None, MEM_PAIRS, SEQ, 128), lambda b: (b, 0, 0, 0))
    return pl.pallas_call(
        body, name="mem_attn_bwd", grid=(B,),
        in_specs=[pairs, pl.BlockSpec((None, MEM_LEN, 2 * W_C), lambda b: (b, 0, 0)), pairs, stat, stat,
                  pl.BlockSpec((None, MEM_LEN, D_MODEL), lambda b: (b, 0, 0))],
        out_specs=[pl.BlockSpec((SEQ, W_C), lambda b: (b, 0)), pl.BlockSpec((8, W_C), lambda b: (0, 0)),
                   pl.BlockSpec((D_MODEL, 2 * W_C), lambda b: (0, 0))],
        out_shape=[jax.ShapeDtypeStruct((B * SEQ, W_C), MXU), jax.ShapeDtypeStruct((8, W_C), F32),
                   jax.ShapeDtypeStruct((D_MODEL, 2 * W_C), F32)],
        scratch_shapes=[pltpu.VMEM((MEM_PAIRS, SEQ, 128), F32), pltpu.VMEM((MEM_LEN, 2 * W_C), F32),
                        pltpu.VMEM((MEM_PAIRS, 2 * MEM_LEN, 128), MXU), pltpu.VMEM((MEM_PAIRS, 2 * MEM_LEN, 128), MXU)],
        compiler_params=_cparams(("arbitrary",)),
    )(qc, mkv, do, lse, delta, mem)


def _headsum(t, e):
    if MXU == F32:
        return _dot(t, e)
    hi = t.astype(MXU)
    lo = (t - hi.astype(F32)).astype(MXU)
    return _dot(hi, e) + _dot(lo, e)


def _heads_to_rows(t, e):
    return sum(_dot_nt(e, part) for part in _split3(t))


POST_ROWS = 256


def _post(o_a, olse_b, o_c, z, x2, tgt, g, gain, bias, w_out, hsum, hrows):
    T = x2.shape[0]
    tm = 256
    nt = SEQ // tm

    def body(oa_ref, ob_ref, oc_ref, z_ref, x_ref, t_ref, g_ref, gain_ref, bias_ref, w_ref, e_ref, er_ref,
             gx_ref, doa_ref, dela_ref, dobb_ref, doc_ref, delc_ref, dz_ref, dw_ref, small_ref, loss_ref):
        @pl.when(pl.program_id(0) == 0)
        def _():
            dw_ref[...] = jnp.zeros_like(dw_ref)
            small_ref[...] = jnp.zeros_like(small_ref)
            loss_ref[...] = jnp.zeros_like(loss_ref)

        gg = g_ref[...]
        gain_v = gain_ref[...]
        gain_s = gain_v * (1.0 / D_MODEL)
        bias_v = bias_ref[...]
        w = w_ref[...]

        def rms(o):
            rr = lax.rsqrt(jnp.mean(o * o, axis=1, keepdims=True) + RMS_EPS)
            return o * rr, rr

        def rows_of(rs):
            oa = _lanes([oa_ref.at[c][rs, :] for c in range(SWA_Q // 2)])
            (o1, l1), (o4, l4), (o16, l16) = [
                (_lanes([ob_ref.at[p, 0][rs, :], ob_ref.at[p, 1][rs, :]]),
                 _lanes([ob_ref.at[p, 2][rs, :], ob_ref.at[p, 3][rs, :]])) for p in range(3)]
            mx = jnp.maximum(jnp.maximum(l1, l4), l16)
            e1, e4, e16 = jnp.exp(l1 - mx), jnp.exp(l4 - mx), jnp.exp(l16 - mx)
            den = e1 + e4 + e16
            ob = (e1 * o1 + e4 * o4 + e16 * o16) / den
            lse_b = mx + jnp.log(den)
            oc = _lanes([oc_ref.at[c][rs, :] for c in range(MEM_PAIRS)])
            na, ra = rms(oa)
            nb, rb = rms(ob)
            nc, rc = rms(oc)
            n = jnp.concatenate([na, nb, nc], axis=1)
            zz = z_ref[rs, :]
            sig = 0.5 * jnp.tanh(0.5 * zz) + 0.5
            sz = zz * sig
            gs = gg * sz
            u = n * gs
            r = ALPHA * x_ref[rs, :] + _dot(u.astype(MXU), w)
            rc0 = r - jnp.mean(r, axis=1, keepdims=True)
            rstd = lax.rsqrt(jnp.mean(rc0 * rc0, axis=1, keepdims=True) + LN_EPS)
            xhat = rc0 * rstd
            err = xhat * gain_v + bias_v - t_ref[rs, :]
            dxh = err * gain_s
            dr = rstd * (dxh - jnp.mean(dxh, axis=1, keepdims=True)
                         - xhat * jnp.mean(dxh * xhat, axis=1, keepdims=True))
            gx_ref[rs, :] = ALPHA * dr
            drb = dr.astype(MXU)
            du = _dot_nt(drb, w)
            dun = du * n
            dz = dun * (gg * (sig + sz * (1.0 - sig)))
            dz_ref[rs, :] = dz.astype(MXU)
            dn = du * gs

            def branch(lo, hi, nbr, rr):
                dnb = dn[:, lo:hi]
                return rr * (dnb - nbr * jnp.mean(dnb * nbr, axis=1, keepdims=True))

            def to_kernel(dob, o, do_ref, delta_ref):
                wd = dob.shape[1]
                for c in range(wd // 128):
                    do_ref.at[c][rs, :] = dob[:, c * 128:(c + 1) * 128].astype(do_ref.dtype)
                dT = _heads_to_rows(dob * o, er_ref[:, 0:wd])
                for jb in range((rs.stop - rs.start) // BLK):
                    delta_ref[rs.start // BLK + jb] = dT[0:8, jb * BLK:(jb + 1) * BLK]

            to_kernel(branch(0, W_A, na, ra), oa, doa_ref, dela_ref)
            to_kernel(branch(W_A + W_B, D_MIX, nc, rc), oc, doc_ref, delc_ref)
            dob = branch(W_A, W_A + W_B, nb, rb)
            for j, t in enumerate((dob, lse_b, _headsum(dob * ob, e_ref[...]))):
                for c in range(W_B // 128):
                    dobb_ref.at[j * (W_B // 128) + c][rs, :] = t[:, c * 128:(c + 1) * 128]
            csum = lambda t: jnp.sum(t, axis=0, keepdims=True)
            return (u, drb, jnp.sum(err * err), csum(err * xhat), csum(err), csum(dun * sz), csum(dz))

        parts = [rows_of(slice(k * POST_ROWS, (k + 1) * POST_ROWS)) for k in range(tm // POST_ROWS)]
        tot = [sum(p[i] for p in parts) for i in range(2, 7)]
        dw_ref[...] += _dot_tn(jnp.concatenate([p[0] for p in parts], axis=0),
                               jnp.concatenate([p[1] for p in parts], axis=0))
        loss_ref[...] += 0.5 * tot[0] * (1.0 / D_MODEL)
        small_ref[0:1, :] += tot[1] * (1.0 / D_MODEL)
        small_ref[1:2, :] += tot[2] * (1.0 / D_MODEL)
        small_ref[2:3, :] += tot[3]
        small_ref[3:4, :] += tot[4]

    B = T // SEQ
    row = lambda w: pl.BlockSpec((tm, w), lambda i: (i, 0))
    full = lambda a, b: pl.BlockSpec((a, b), lambda i: (0, 0))
    chunked = lambda n: pl.BlockSpec((None, n, tm, 128), lambda i: (i // nt, 0, i % nt, 0))
    stat = pl.BlockSpec((None, tm // BLK, 8, 128), lambda i: (i // nt, i % nt, 0, 0))
    return pl.pallas_call(
        body, name="post_fwd_bwd", grid=(T // tm,),
        in_specs=[chunked(SWA_Q // 2), pl.BlockSpec((None, 3, 4, tm, 128), lambda i: (i // nt, 0, 0, i % nt, 0)),
                  chunked(MEM_PAIRS),
                  row(D_MIX), row(D_MODEL), row(D_MODEL),
                  full(1, D_MIX), full(1, D_MODEL), full(1, D_MODEL), full(D_MIX, D_MODEL), full(W_B, W_B),
                  full(PICK_ROWS, W_A)],
        out_specs=[row(D_MODEL), chunked(SWA_Q // 2), stat, chunked(6), chunked(MEM_PAIRS), stat, row(D_MIX),
                   full(D_MIX, D_MODEL), full(8, D_MODEL), full(8, 128)],
        out_shape=[jax.ShapeDtypeStruct((T, D_MODEL), F32),
                   jax.ShapeDtypeStruct((B, SWA_Q // 2, SEQ, 128), MXU),
                   jax.ShapeDtypeStruct((B, N_QBLK, 8, 128), F32),
                   jax.ShapeDtypeStruct((B, 6, SEQ, 128), F32),
                   jax.ShapeDtypeStruct((B, MEM_PAIRS, SEQ, 128), MXU),
                   jax.ShapeDtypeStruct((B, N_QBLK, 8, 128), F32),
                   jax.ShapeDtypeStruct((T, D_MIX), MXU),
                   jax.ShapeDtypeStruct((D_MIX, D_MODEL), F32),
                   jax.ShapeDtypeStruct((8, D_MODEL), F32),
                   jax.ShapeDtypeStruct((8, 128), F32)],
        compiler_params=_cparams(("arbitrary",)),
    )(o_a, olse_b, o_c, z, x2, tgt, g, gain, bias, w_out, hsum, hrows)


TAIL_TK = 512
TAIL_TN = D_IN // 2
TAIL_TM = 256
REDUCE_ROWS = 128


DH_SPLITS = (0, W_A + 2 * W_KVA, W_A + 2 * W_KVA + 3 * W_B, D_IN - D_MIX, D_IN)


def _tail(xt, dhs, gx1, w_in, small_g):
    T = xt.shape[1]
    dh = dhs[0]
    c0, c1, c2, c3, c4 = DH_SPLITS
    assert c1 < TAIL_TN < c2 and (TAIL_TN - c1) % 128 == 0
    kt = T // TAIL_TK
    ndw = (D_IN // TAIL_TN) * kt
    nsteps = ndw + T // TAIL_TM
    n_pass = D_IN // TAIL_TN
    assert n_pass == 2 and TAIL_TN == 4 * COLS_PER_DEV and kt >= 2
    pay = dh.dtype
    blk_shape = (D_MODEL, COLS_PER_DEV)
    n_half = 2 * n_pass
    n_chip = N_DEV // 2

    def body(xt_ref, a1_ref, b1_ref, b2_ref, c1_ref, z1_ref, a2_ref, b3_ref, c2_ref, z2_ref,
             gx_ref, w_hbm, sg_ref, dx_ref, gin_ref, gsm_ref,
             acc_ref, w_ref, mine_ref, stagea_ref, landa_ref, stageb_ref, landb_ref, own_ref, lsm_ref,
             sa_sems, ra_sems, sb_sems, rb_sems, ss_sems, rs_sems, w_sem):
        s = pl.program_id(0)
        x, y, c = _my_pos()
        me = 4 * x + 2 * y + c
        chip = 2 * x + y

        def to_sibling(q):
            return pltpu.make_async_remote_copy(
                src_ref=stagea_ref.at[q], dst_ref=landa_ref.at[q], send_sem=sa_sems.at[q], recv_sem=ra_sems.at[q],
                device_id=(x, y, 1 - c), device_id_type=MESH)

        def to_owner(q):
            return pltpu.make_async_remote_copy(
                src_ref=stageb_ref.at[q], dst_ref=landb_ref.at[chip], send_sem=sb_sems.at[q],
                recv_sem=rb_sems.at[chip], device_id=(q // 2, q % 2, c), device_id_type=MESH)

        def from_chip(m):
            return pltpu.make_async_remote_copy(
                src_ref=landb_ref.at[m], dst_ref=landb_ref.at[m], send_sem=sb_sems.at[m], recv_sem=rb_sems.at[m],
                device_id=(m // 2, m % 2, c), device_id_type=MESH)

        def is_me(q):
            return (x == q // 2) & (y == q % 2)

        def small_to(j):
            return pltpu.make_async_remote_copy(
                src_ref=sg_ref, dst_ref=lsm_ref.at[me], send_sem=ss_sems.at[j], recv_sem=rs_sems.at[me],
                device_id=_dev_coords(j), device_id_type=MESH)

        def small_from(m):
            return pltpu.make_async_remote_copy(
                src_ref=lsm_ref.at[m], dst_ref=lsm_ref.at[m], send_sem=ss_sems.at[m], recv_sem=rs_sems.at[m],
                device_id=_dev_coords(m), device_id_type=MESH)

        w_copy = pltpu.make_async_copy(w_hbm, w_ref, w_sem)

        @pl.when(s == 0)
        def _():
            w_copy.start()
            for j in range(N_DEV):
                pl.when(me != j)(small_to(j).start)
            lsm_ref[me] = sg_ref[...]
            landb_ref[chip] = jnp.zeros(blk_shape, pay)

        @pl.when(s < ndw)
        def _():
            @pl.when(s % kt == 0)
            def _():
                acc_ref[...] = jnp.zeros_like(acc_ref)
            xt_ = xt_ref[...]
            @pl.when(s < kt)
            def _():
                acc_ref[:, 0:c1] += _dot(xt_, a1_ref[...])
                acc_ref[:, c1:TAIL_TN] += _dot(xt_, b1_ref[...])

            @pl.when(s >= kt)
            def _():
                acc_ref[:, 0:c2 - TAIL_TN] += _dot(xt_, b2_ref[...])
                acc_ref[:, c2 - TAIL_TN:c3 - TAIL_TN] += _dot(xt_, c1_ref[...])
                acc_ref[:, c3 - TAIL_TN:c4 - TAIL_TN] += _dot(xt_, z1_ref[...])

        for p in range(n_pass):
            @pl.when(s == p * kt + kt - 1)
            def _(p=p):
                for cc in range(2):
                    @pl.when(c == cc)
                    def _(cc=cc):
                        for yo in range(2):
                            q = 2 * p + yo
                            same, other = 2 * yo + cc, 2 * yo + 1 - cc
                            mine_ref[q] = acc_ref[:, same * COLS_PER_DEV:(same + 1) * COLS_PER_DEV]
                            stagea_ref[q] = acc_ref[:, other * COLS_PER_DEV:(other + 1) * COLS_PER_DEV].astype(pay)
                for yo in range(2):
                    to_sibling(2 * p + yo).start()

            @pl.when(s == (p + 1) * kt + 1)
            def _(p=p):
                for yo in range(2):
                    q = 2 * p + yo
                    to_sibling(q).wait_recv()

                    def chunk(i, carry, q=q):
                        rs = pl.ds(pl.multiple_of(i * REDUCE_ROWS, REDUCE_ROWS), REDUCE_ROWS)
                        tot = mine_ref[q, rs, :] + landa_ref[q, rs, :].astype(F32)

                        @pl.when(is_me(q))
                        def _():
                            own_ref[rs, :] = tot

                        @pl.when(jnp.logical_not(is_me(q)))
                        def _():
                            stageb_ref[q, rs, :] = tot.astype(pay)
                        return carry
                    lax.fori_loop(0, D_MODEL // REDUCE_ROWS, chunk, 0)
                    pl.when(jnp.logical_not(is_me(q)))(to_owner(q).start)

        @pl.when(s >= ndw)
        def _():
            pl.when(s == ndw)(w_copy.wait)
            dx_ref[...] = (_dot_nt(a2_ref[...], w_ref[:, c0:c1]) + _dot_nt(b3_ref[...], w_ref[:, c1:c2])
                           + _dot_nt(c2_ref[...], w_ref[:, c2:c3]) + _dot_nt(z2_ref[...], w_ref[:, c3:c4])
                           + gx_ref[...])

        @pl.when(s == nsteps - 1)
        def _():
            for m in range(n_chip):
                pl.when(m != chip)(from_chip(m).wait_recv)
            for m in range(N_DEV):
                pl.when(me != m)(small_from(m).wait_recv)
            for q in range(n_half):
                to_sibling(q).wait_send()
                pl.when(jnp.logical_not(is_me(q)))(to_owner(q).wait_send)
            for j in range(N_DEV):
                pl.when(me != j)(small_to(j).wait_send)

            def chunk(i, carry):
                rs = pl.ds(pl.multiple_of(i * REDUCE_ROWS, REDUCE_ROWS), REDUCE_ROWS)
                g = own_ref[rs, :]
                for m in range(n_chip):
                    g = g + landb_ref[m, rs, :].astype(F32)
                gin_ref[rs, :] = g
                return carry
            lax.fori_loop(0, D_MODEL // REDUCE_ROWS, chunk, 0)
            g = lsm_ref[0]
            for m in range(1, N_DEV):
                g = g + lsm_ref[m]
            gsm_ref[...] = g

    dw_step = lambda s: jnp.minimum(s, ndw - 1)
    dx_step = lambda s: jnp.maximum(s - ndw, 0)
    pass0 = lambda s: jnp.minimum(s, kt - 1)
    pass1 = lambda s: jnp.clip(s - kt, 0, kt - 1)
    any_spec = pl.BlockSpec(memory_space=pl.ANY)
    vmem = pl.BlockSpec(memory_space=pltpu.VMEM)
    dma = pltpu.SemaphoreType.DMA
    scratch = [pltpu.VMEM((D_MODEL, TAIL_TN), F32), pltpu.VMEM((D_MODEL, D_IN), w_in.dtype),
               pltpu.VMEM((n_half,) + blk_shape, F32),
               pltpu.VMEM((n_half,) + blk_shape, pay), pltpu.VMEM((n_half,) + blk_shape, pay),
               pltpu.VMEM((n_half,) + blk_shape, pay), pltpu.VMEM((n_chip,) + blk_shape, pay),
               pltpu.VMEM(blk_shape, F32), pltpu.VMEM((N_DEV,) + small_g.shape, F32),
               dma((n_half,)), dma((n_half,)), dma((n_half,)), dma((n_chip,)), dma((N_DEV,)), dma((N_DEV,)), dma]
    return pl.pallas_call(
        body, name="tail_dw_dx_reduce", grid=(nsteps,),
        in_specs=[pl.BlockSpec((D_MODEL, TAIL_TK), lambda s: (0, dw_step(s) % kt)),
                  pl.BlockSpec((TAIL_TK, c1 - c0), lambda s: (pass0(s), 0)),
                  pl.BlockSpec((TAIL_TK, TAIL_TN - c1), lambda s: (pass0(s), 0)),
                  pl.BlockSpec((TAIL_TK, 128), lambda s: (pass1(s), (TAIL_TN - c1) // 128)),
                  pl.BlockSpec((TAIL_TK, c3 - c2), lambda s: (pass1(s), 0)),
                  pl.BlockSpec((TAIL_TK, c4 - c3), lambda s: (pass1(s), 0)),
                  pl.BlockSpec((TAIL_TM, c1 - c0), lambda s: (dx_step(s), 0)),
                  pl.BlockSpec((TAIL_TM, c2 - c1), lambda s: (dx_step(s), 0)),
                  pl.BlockSpec((TAIL_TM, c3 - c2), lambda s: (dx_step(s), 0)),
                  pl.BlockSpec((TAIL_TM, c4 - c3), lambda s: (dx_step(s), 0)),
                  pl.BlockSpec((TAIL_TM, D_MODEL), lambda s: (dx_step(s), 0)),
                  any_spec, vmem],
        out_specs=[pl.BlockSpec((TAIL_TM, D_MODEL), lambda s: (dx_step(s), 0)), vmem, vmem],
        out_shape=[jax.ShapeDtypeStruct((T, D_MODEL), F32), jax.ShapeDtypeStruct(blk_shape, F32),
                   jax.ShapeDtypeStruct(small_g.shape, F32)],
        scratch_shapes=scratch,
        compiler_params=_cparams(("arbitrary",)),
    )(xt, dhs[0], dhs[1], dhs[1], dhs[2], dhs[3], dhs[0], dhs[1], dhs[2], dhs[3], gx1, w_in, small_g)


def _adam_update(grads, params, carried):
    n = len(grads)

    def body(*refs):
        g_refs, p_refs, o_refs = refs[1:1 + n], refs[1 + n:1 + 4 * n], refs[2 + 4 * n:]
        for a in range(n):
            rows = g_refs[a].shape[0]
            cr = REDUCE_ROWS if rows % REDUCE_ROWS == 0 else rows
            flat2 = lambda r: r.at[0] if len(r.shape) == 3 else r
            w_ref, m_ref, v_ref = [flat2(r) for r in p_refs[3 * a:3 * a + 3]]
            go_ref, d_ref, nm_ref, nv_ref = [flat2(r) for r in o_refs[4 * a:4 * a + 4]]

            def chunk(i, carry, cr=cr, g_ref=g_refs[a], w_ref=w_ref, m_ref=m_ref, v_ref=v_ref,
                      go_ref=go_ref, d_ref=d_ref, nm_ref=nm_ref, nv_ref=nv_ref):
                rs = pl.ds(pl.multiple_of(i * cr, cr), cr)
                g = g_ref[rs, :]
                go_ref[rs, :] = g
                d_ref[rs, :], nm_ref[rs, :], nv_ref[rs, :] = _adamw(w_ref[rs, :], g, m_ref[rs, :], v_ref[rs, :])
                return carry
            lax.fori_loop(0, rows // cr, chunk, 0)

    vmem = pl.BlockSpec(memory_space=pltpu.VMEM)
    any_spec = pl.BlockSpec(memory_space=pl.ANY)
    flat = [p for grp in params for p in grp]
    outs = pl.pallas_call(
        body, name="adamw", in_specs=[any_spec] + [vmem] * (4 * n), out_specs=[any_spec] + [vmem] * (4 * n),
        out_shape=[jax.ShapeDtypeStruct(carried.shape, carried.dtype)]
        + [jax.ShapeDtypeStruct(grp[0].shape, F32) for grp in params for _ in range(4)],
        input_output_aliases={0: 0},
        compiler_params=pltpu.CompilerParams(vmem_limit_bytes=VMEM_LIMIT),
    )(carried, *grads, *flat)
    return [outs[1 + 4 * a:5 + 4 * a] for a in range(n)], outs[0]


def _step(x, mem, w_in_s, w_mem_s, w_out_s, b_in, sinks, g, gain, bias, tgt):
    B = x.shape[0]
    T = B * SEQ
    x2 = x.reshape(T, D_MODEL)
    t2 = tgt.reshape(T, D_MODEL)
    tab = _rope_inv()
    lane = jnp.arange(W_A)
    hsum = (lane[:W_B, None] // HEAD == lane[None, :W_B] // HEAD).astype(MXU)
    hrows = (jnp.arange(PICK_ROWS)[:, None] == lane[None, :] // HEAD).astype(MXU)
    me = 4 * lax.axis_index("x") + 2 * lax.axis_index("y") + lax.axis_index("c")

    (w_in_all,) = _gather_weights([w_in_s])
    qkva, qkvb, qc, z, w_in, xt, rope_tab, w_mem_all, w_out_all = _in_proj(
        x2, w_in_all, b_in, tab, [w_mem_s, w_out_s])
    w_mem = w_mem_all.reshape(D_MODEL, 2 * W_C)
    w_out = w_out_all.reshape(D_MIX, D_MODEL)

    o_a, lse_a = _swa_fwd(qkva, sinks)
    olse_b = _dil_fwd(qkvb)
    o_c, lse_c, mkv = _mem_attn_fwd(qc, mem, w_mem)

    gx1, do_a, delta_a, dobb, do_c, delta_c, dz, dw_out, small, loss = _post(
        o_a, olse_b, o_c, z, x2, t2, g, gain, bias, w_out, hsum, hrows)

    dh_c, db_c, dw_mem = _mem_attn_bwd(qc, mkv, do_c, lse_c, delta_c, mem)
    blocks = [dw_mem.reshape(N_DEV, ROWS_PER_DEV, 2 * W_C), dw_out.reshape(N_DEV, ROWS_PER_DEV, D_MODEL)]
    sends = [b.astype(MXU) for b in blocks]
    owns = [lax.dynamic_index_in_dim(b, me, axis=0, keepdims=False) for b in blocks]
    dh_b, db_b, (g_mem, g_out) = _dil_bwd(qkvb, dobb, rope_tab, sends, owns)
    dh_a, db_a, dsink = _swa_bwd(qkva, do_a, lse_a, delta_a, sinks, rope_tab)

    small_g = _pack_small(dict(b_in=jnp.concatenate([db_a[0], db_b[0], db_c[0], small[3]]), sinks=dsink[:, 0],
                               g=small[2], gain=small[0], bias=small[1], loss=loss[0, 0]))
    grad_x, g_in, g_small = _tail(xt, (dh_a, dh_b, dh_c, dz), gx1, w_in, small_g)
    return grad_x.reshape(B, SEQ, D_MODEL), g_in, g_mem, g_out, g_small


def _my_pos():
    return lax.axis_index("x"), lax.axis_index("y"), lax.axis_index("c")


def _gather_weights(shards):
    n_arr = len(shards)

    def body(*refs):
        ins, outs = refs[0:n_arr], refs[n_arr:2 * n_arr]
        send_sems, recv_sems, local_sems = refs[2 * n_arr:]
        x, y, c = _my_pos()
        me, sibling = (x, y, c), (x, y, 1 - c)
        chips = [(1 - x, y), (x, 1 - y), (1 - x, 1 - y)]

        def slot(a, pos):
            return outs[a].at[4 * pos[0] + 2 * pos[1] + pos[2]]

        def copy(a, k, block, to, src=None):
            return pltpu.make_async_remote_copy(
                src_ref=slot(a, block) if src is None else src, dst_ref=slot(a, block),
                send_sem=send_sems.at[a, k], recv_sem=recv_sems.at[a, k],
                device_id=to, device_id_type=MESH)

        mine = [pltpu.make_async_copy(ins[a], slot(a, me), local_sems.at[a]) for a in range(n_arr)]
        for cp in mine:
            cp.start()
        first = []
        for a in range(n_arr):
            first.append(copy(a, 0, me, sibling, src=ins[a]))
            first += [copy(a, 1 + j, me, (*chip, c), src=ins[a]) for j, chip in enumerate(chips)]
        for cp in first:
            cp.start()
        passed = []
        for j, chip in enumerate(chips):
            for a in range(n_arr):
                copy(a, 1 + j, (*chip, c), me).wait_recv()
                fwd = copy(a, 4 + j, (*chip, c), sibling)
                fwd.start()
                passed.append(fwd)
        for a in range(n_arr):
            copy(a, 0, sibling, me).wait_recv()
            for j, chip in enumerate(chips):
                copy(a, 4 + j, (*chip, 1 - c), me).wait_recv()
        for cp in first + passed:
            cp.wait_send()
        for cp in mine:
            cp.wait()

    any_spec = pl.BlockSpec(memory_space=pl.ANY)
    return pl.pallas_call(
        body, name="gather_weights",
        in_specs=[any_spec] * n_arr, out_specs=[any_spec] * n_arr,
        out_shape=[jax.ShapeDtypeStruct((N_DEV,) + s.shape, s.dtype) for s in shards],
        scratch_shapes=[pltpu.SemaphoreType.DMA((n_arr, 7)), pltpu.SemaphoreType.DMA((n_arr, 7)),
                        pltpu.SemaphoreType.DMA((n_arr,))],
    )(*shards)


def _adamw(w, g, m, v):
    m = ADAM_B1 * m + (1.0 - ADAM_B1) * g
    v = ADAM_B2 * v + (1.0 - ADAM_B2) * (g * g)
    m_hat = m / (1.0 - ADAM_B1 ** ADAM_STEP)
    v_hat = v / (1.0 - ADAM_B2 ** ADAM_STEP)
    delta = -ADAM_LR * (m_hat / (jnp.sqrt(v_hat) + ADAM_EPS) + ADAM_WD * w)
    return delta, m, v


_SMALL_SIZES = (("b_in", D_IN), ("g", D_MIX), ("gain", D_MODEL), ("bias", D_MODEL), ("sinks", SWA_Q), ("loss", 1))


def _pack_small(d):
    flat = jnp.concatenate([jnp.reshape(d[k], (-1,)).astype(F32) if k in d else jnp.zeros((n,), F32)
                            for k, n in _SMALL_SIZES])
    flat = jnp.pad(flat, (0, SMALL_ROWS * 128 - flat.shape[0]))
    return flat.reshape(SMALL_ROWS, 128)


def _unpack_small(p):
    flat = p.reshape(-1)
    out, off = {}, 0
    for k, n in _SMALL_SIZES:
        out[k] = flat[off:off + n].reshape(1, n)
        off += n
    return out


def kernel(x, mem, w_in, b_in, w_mem, attn_sinks, g_branch, w_out, ln_gain, ln_bias, loss_target, m_w_in, m_b_in, m_w_mem, m_attn_sinks, m_g_branch, m_w_out, m_ln_gain, m_ln_bias, v_w_in, v_b_in, v_w_mem, v_attn_sinks, v_g_branch, v_w_out, v_ln_gain, v_ln_bias):
    grad_x, g_in, g_mem, g_out, g_small = _step(
        x, mem, w_in[0].astype(MXU), w_mem[0].astype(MXU), w_out[0].astype(MXU), b_in, attn_sinks[0],
        g_branch, ln_gain, ln_bias, loss_target)

    small_w = _pack_small(dict(b_in=b_in, g=g_branch, gain=ln_gain, bias=ln_bias, sinks=attn_sinks))
    small_m = _pack_small(dict(b_in=m_b_in, g=m_g_branch, gain=m_ln_gain, bias=m_ln_bias, sinks=m_attn_sinks))
    small_v = _pack_small(dict(b_in=v_b_in, g=v_g_branch, gain=v_ln_gain, bias=v_ln_bias, sinks=v_attn_sinks))
    grads = [g_in, g_mem, g_out, g_small]
    params = [(w_in, m_w_in, v_w_in), (w_mem, m_w_mem, v_w_mem), (w_out, m_w_out, v_w_out),
              (small_w, small_m, small_v)]
    res, grad_x = _adam_update(grads, params, grad_x)
    big = res[:3]
    sm = [_unpack_small(r) for r in res[3]]

    def group(i):
        return (big[0][i], sm[i]["b_in"], big[1][i], sm[i]["sinks"], sm[i]["g"], big[2][i],
                sm[i]["gain"], sm[i]["bias"])

    loss = sm[0]["loss"].reshape(())
    return (loss, grad_x, *group(0), *group(1), *group(2), *group(3))
```

```python
import functools
import math

import jax
import jax.numpy as jnp
from jax import lax
from jax.experimental import pallas as pl
from jax.experimental.pallas import tpu as pltpu

F32 = jnp.float32
MXU = jnp.bfloat16

D_MODEL = 1024
SEQ = 2048
HEAD = 64
BLK = 128
SWA_Q, SWA_KV = 8, 2
DIL_H = 4
MEM_H = 4
MEM_LEN = 256
W_A, W_KVA, W_B, W_C = 512, 128, 256, 256
D_MIX = 1024
D_IN = 2816
N_DEV = 8
COLS_PER_DEV = D_IN // N_DEV
ROWS_PER_DEV = D_MODEL // N_DEV
ROPE_THETA = 10000.0
LN_EPS = 1e-5
RMS_EPS = 1e-6
ALPHA = 2.0 ** 0.25
Q_SCALE = HEAD ** -0.5
NEG = -1e30
SMALL_ROWS = 48
VMEM_LIMIT = 56 * 1024 * 1024

ADAM_LR = 0.001
ADAM_B1 = 0.9
ADAM_B2 = 0.999
ADAM_EPS = 1e-08
ADAM_WD = 0.01
ADAM_STEP = 10

MESH = pl.DeviceIdType.MESH


def _cparams(sem=None):
    return pltpu.CompilerParams(dimension_semantics=sem, vmem_limit_bytes=VMEM_LIMIT)


def _dot(a, b):
    return jnp.dot(a, b, preferred_element_type=F32)


def _dot_nt(a, b):
    return lax.dot_general(a, b, (((1,), (1,)), ((), ())), preferred_element_type=F32)


def _dot_t0(a, b):
    return lax.dot_general(a, b, (((0,), (0,)), ((), ())), preferred_element_type=F32)


def _dot_tn(a, b):
    return jnp.dot(a.T.astype(MXU), b, preferred_element_type=F32)


def _rope(t, tab, sign):
    cos, sa, sb = tab
    outs = []
    for c in range(t.shape[1] // 128):
        tc = t[:, c * 128:(c + 1) * 128]
        r = pltpu.roll(tc, 96, 1) * sa + pltpu.roll(tc, 32, 1) * sb
        outs.append(tc * cos + r if sign > 0 else tc * cos - r)
    return outs[0] if len(outs) == 1 else jnp.concatenate(outs, axis=1)


def _rope_inv():
    inv = ROPE_THETA ** (-jnp.arange(0, HEAD, 2, dtype=F32) / HEAD)
    return jnp.tile(inv, 2 * 128 // HEAD)[None, :]


def _rope_tab(pos0, rows, inv):
    pos = (lax.broadcasted_iota(jnp.int32, (rows, 128), 0) + pos0).astype(F32)
    ang = pos * inv
    cos, sin = jnp.cos(ang), jnp.sin(ang)
    first = lax.broadcasted_iota(jnp.int32, (rows, 128), 1) % HEAD < HEAD // 2
    return cos, jnp.where(first, -sin, 0.0), jnp.where(first, 0.0, sin)


def _dev_coords(j):
    return (j >> 2, (j >> 1) & 1, j & 1)


def _in_proj(x2, w_all, b_in, tab, late_shards):
    T = x2.shape[0]
    tm = 512
    n_late = len(late_shards)

    def body(x_ref, wall_ref, b_ref, tab_ref, *rest):
        late_in, rest = rest[:n_late], rest[n_late:]
        qkva_ref, qkvb_ref, qc_ref, z_ref, w_ref, xt_ref, tabo_ref = rest[:7]
        late_out = rest[7:7 + n_late]
        send_sems, recv_sems, local_sems = rest[7 + n_late:]
        step, last = pl.program_id(0), pl.num_programs(0) - 1
        x, y, c = _my_pos()
        me = 4 * x + 2 * y + c

        def to_peer(a, j):
            return pltpu.make_async_remote_copy(
                src_ref=late_in[a], dst_ref=late_out[a].at[me], send_sem=send_sems.at[a, j],
                recv_sem=recv_sems.at[a, me], device_id=_dev_coords(j), device_id_type=MESH)

        def from_peer(a, m):
            return pltpu.make_async_remote_copy(
                src_ref=late_out[a].at[m], dst_ref=late_out[a].at[m], send_sem=send_sems.at[a, m],
                recv_sem=recv_sems.at[a, m], device_id=_dev_coords(m), device_id_type=MESH)

        def mine(a):
            return pltpu.make_async_copy(late_in[a], late_out[a].at[me], local_sems.at[a])

        @pl.when(step == 0)
        def _():
            for a in range(n_late):
                mine(a).start()
                for j in range(N_DEV):
                    pl.when(me != j)(to_peer(a, j).start)
            for j in range(N_DEV):
                w_ref[:, j * COLS_PER_DEV:(j + 1) * COLS_PER_DEV] = wall_ref[j]

        xb = x_ref[...].astype(MXU)
        xt_ref[...] = x_ref[...].T.astype(MXU)
        tab = _rope_tab((step % nt) * tm, tm, tab_ref[...])
        for j in range(3):
            tabo_ref[:, j * 128:(j + 1) * 128] = tab[j]

        def seg(c0, c1):
            return _dot(xb, w_ref[:, c0:c1]) + b_ref[:, c0:c1]

        qa = (_rope(seg(0, 512), tab, 1) * Q_SCALE).astype(MXU)
        for c in range(SWA_Q // 2):
            qkva_ref[c] = qa[:, c * 128:(c + 1) * 128]
        lo = lax.broadcasted_iota(jnp.int32, (tm, 128), 1) < HEAD
        for j, t in enumerate((_rope(seg(512, 640), tab, 1), seg(640, 768))):
            other = pltpu.roll(t, HEAD, 1)
            qkva_ref[4 + 2 * j] = jnp.where(lo, t, other).astype(MXU)
            qkva_ref[5 + 2 * j] = jnp.where(lo, other, t).astype(MXU)
        qkvb = (_rope(seg(768, 1024), tab, 1) * Q_SCALE, _rope(seg(1024, 1280), tab, 1), seg(1280, 1536))
        for j, t in enumerate(qkvb):
            for c in range(2):
                qkvb_ref[2 * j + c] = t[:, c * 128:(c + 1) * 128]
        qc = (seg(1536, 1792) * Q_SCALE).astype(MXU)
        for c in range(MEM_H // 2):
            qc_ref[c] = qc[:, c * 128:(c + 1) * 128]
        z_ref[...] = seg(1792, 2816)

        @pl.when(step == last)
        def _():
            for a in range(n_late):
                mine(a).wait()
                for m in range(N_DEV):
                    pl.when(me != m)(from_peer(a, m).wait_recv)
                for j in range(N_DEV):
                    pl.when(me != j)(to_peer(a, j).wait_send)

    nt = SEQ // tm
    any_spec = pl.BlockSpec(memory_space=pl.ANY)
    chunked = lambda n: pl.BlockSpec((None, n, tm, 128), lambda i: (i // nt, 0, i % nt, 0))
    return pl.pallas_call(
        body, name="in_proj_fwd",
        grid=(T // tm,),
        in_specs=[pl.BlockSpec((tm, D_MODEL), lambda i: (i, 0)),
                  pl.BlockSpec((N_DEV, D_MODEL, COLS_PER_DEV), lambda i: (0, 0, 0)),
                  pl.BlockSpec((1, D_IN), lambda i: (0, 0)),
                  pl.BlockSpec((1, 128), lambda i: (0, 0))] + [any_spec] * n_late,
        out_specs=[chunked(SWA_CHUNKS), chunked(6), chunked(MEM_H // 2),
                   pl.BlockSpec((tm, D_MIX), lambda i: (i, 0)),
                   pl.BlockSpec((D_MODEL, D_IN), lambda i: (0, 0)),
                   pl.BlockSpec((D_MODEL, tm), lambda i: (0, i)),
                   pl.BlockSpec((tm, 384), lambda i: (i, 0))] + [any_spec] * n_late,
        out_shape=[jax.ShapeDtypeStruct((T // SEQ, SWA_CHUNKS, SEQ, 128), MXU),
                   jax.ShapeDtypeStruct((T // SEQ, 6, SEQ, 128), F32),
                   jax.ShapeDtypeStruct((T // SEQ, MEM_H // 2, SEQ, 128), MXU),
                   jax.ShapeDtypeStruct((T, D_MIX), F32),
                   jax.ShapeDtypeStruct((D_MODEL, D_IN), w_all.dtype),
                   jax.ShapeDtypeStruct((D_MODEL, T), MXU),
                   jax.ShapeDtypeStruct((T, 384), F32)]
        + [jax.ShapeDtypeStruct((N_DEV,) + s.shape, s.dtype) for s in late_shards],
        scratch_shapes=[pltpu.SemaphoreType.DMA((n_late, N_DEV)), pltpu.SemaphoreType.DMA((n_late, N_DEV)),
                        pltpu.SemaphoreType.DMA((n_late,))],
        compiler_params=_cparams(("arbitrary",)),
    )(x2, w_all, b_in, tab, *late_shards)


CHAIN = 4


def _band_bias(max_dist):
    kj = lax.broadcasted_iota(jnp.int32, (2 * BLK, BLK), 0)
    qi = lax.broadcasted_iota(jnp.int32, (2 * BLK, BLK), 1)
    dist = qi + BLK - kj
    band = jnp.where((dist >= 0) & (dist <= max_dist), 0.0, NEG).astype(F32)
    k1 = lax.broadcasted_iota(jnp.int32, (BLK, BLK), 0)
    q1 = lax.broadcasted_iota(jnp.int32, (BLK, BLK), 1)
    first = jnp.where((q1 - k1 >= 0) & (q1 - k1 <= max_dist), 0.0, NEG).astype(F32)
    return jnp.concatenate([band] * CHAIN, axis=1), jnp.concatenate([first] * CHAIN, axis=1)


def _lanes(parts):
    return jnp.concatenate(parts, axis=1)


PICK_ROWS = 16


def _stack_pair(t):
    lo = (lax.broadcasted_iota(jnp.int32, t.shape, 1) < HEAD).astype(F32)
    return jnp.concatenate([t * lo, t * (1.0 - lo)], axis=0).astype(MXU)


def _pair_rows(x, n):
    lo = lax.broadcasted_iota(jnp.int32, (n, 128), 1) < HEAD
    return jnp.where(lo, x[0:n], x[n:2 * n])


def _split3(t):
    if MXU == F32:
        return (t,)
    hi = t.astype(MXU)
    r = t - hi.astype(F32)
    mid = r.astype(MXU)
    return hi, mid, (r - mid.astype(F32)).astype(MXU)


def _interleave(tiles):
    tiles = list(tiles)
    while tiles:
        for t in list(tiles):
            try:
                next(t)
            except StopIteration:
                tiles.remove(t)


def _softmax_cols(sT, sinkrow=None):
    m = jnp.max(sT, axis=0, keepdims=True)
    if sinkrow is not None:
        m = jnp.maximum(m, sinkrow)
    pT = jnp.exp(sT - m)
    l = jnp.sum(pT, axis=0, keepdims=True)
    if sinkrow is not None:
        l = l + jnp.exp(sinkrow - m)
    return (pT * (1.0 / l)).astype(MXU), m + jnp.log(l)


SWA_CHUNKS = 8
SWA_UNROLL = 3
N_QBLK = SEQ // BLK


def _swa_fwd(qkva, sinks):
    B = qkva.shape[0]
    G = SWA_Q // SWA_KV

    def body(sink_ref, qkv_ref, o_ref, lse_ref):
        band, first = _band_bias(BLK - 1)
        sinkrows = [_lanes([jnp.full((1, BLK), sink_ref[G * hk + j], F32) for j in range(G)])
                    for hk in range(SWA_KV)]

        def tile(hk, blk, rows_q, rows_k, bias):
            nk = bias.shape[0]
            k2 = _stack_pair(qkv_ref.at[4 + hk][rows_k, :])
            sT = []
            for c in (2 * hk, 2 * hk + 1):
                s2 = _dot_nt(k2, qkv_ref.at[c][rows_q, :])
                sT += [s2[0:nk], s2[nk:2 * nk]]
            yield
            pnT, lse = _softmax_cols(_lanes(sT) + bias, sinkrows[hk])
            yield
            v2 = _stack_pair(qkv_ref.at[6 + hk][rows_k, :])
            for j, c in enumerate((2 * hk, 2 * hk + 1)):
                p2 = jnp.concatenate([pnT[:, 2 * j * BLK:(2 * j + 1) * BLK],
                                      pnT[:, (2 * j + 1) * BLK:(2 * j + 2) * BLK]], axis=0)
                o_ref.at[c][rows_q, :] = _dot_t0(p2, v2)
            for j in range(G):
                lse_ref.at[blk][G * hk + j:G * hk + j + 1, :] = lse[:, j * BLK:(j + 1) * BLK]

        def tiles_at(i):
            r0 = pl.multiple_of(i * BLK, BLK)
            rk = pl.multiple_of(i * BLK - BLK, BLK)
            return [tile(hk, i, pl.ds(r0, BLK), pl.ds(rk, 2 * BLK), band) for hk in range(SWA_KV)]

        _interleave([tile(hk, 0, pl.ds(0, BLK), pl.ds(0, BLK), first) for hk in range(SWA_KV)])

        def loop(j, carry):
            _interleave([t for u in range(SWA_UNROLL) for t in tiles_at(1 + j * SWA_UNROLL + u)])
            return carry
        lax.fori_loop(0, (N_QBLK - 1) // SWA_UNROLL, loop, 0)

    return pl.pallas_call(
        body, name="swa_fwd", grid=(B,),
        in_specs=[pl.BlockSpec(memory_space=pltpu.SMEM),
                  pl.BlockSpec((None, SWA_CHUNKS, SEQ, 128), lambda b: (b, 0, 0, 0))],
        out_specs=[pl.BlockSpec((None, SWA_Q // 2, SEQ, 128), lambda b: (b, 0, 0, 0)),
                   pl.BlockSpec((None, N_QBLK, 8, 128), lambda b: (b, 0, 0, 0))],
        out_shape=[jax.ShapeDtypeStruct((B, SWA_Q // 2, SEQ, 128), F32),
                   jax.ShapeDtypeStruct((B, N_QBLK, 8, 128), F32)],
        compiler_params=_cparams(("arbitrary",)),
    )(sinks, qkva)


EP_ROWS = BLK


def _dh_tile(part_of, dh_ref, db_ref, tab_ref, blk, live=None):
    rs = pl.ds(pl.multiple_of(blk * EP_ROWS, EP_ROWS), EP_ROWS)
    tab = None if tab_ref is None else tuple(tab_ref[rs, j * 128:(j + 1) * 128] for j in range(3))
    part = part_of(rs, tab)
    yield
    dh_ref[rs, :] = part.astype(dh_ref.dtype)
    psum = jnp.sum(part, axis=0, keepdims=True)
    db_ref[0:1, :] += psum if live is None else psum * live
    yield


def _dh_tiles_behind(part_of, dh_ref, db_ref, tab_ref, j, unroll):
    live = (j > 0).astype(F32)
    return [_dh_tile(part_of, dh_ref, db_ref, tab_ref, jnp.where(j > 0, unroll * (j - 1) + u, 0), live)
            for u in range(unroll)]


def _swa_bwd(qkva, do, lse, delta, sinks, inv):
    B = qkva.shape[0]
    G = SWA_Q // SWA_KV

    def body(sink_ref, qkv_ref, do_ref, lse_ref, delta_ref, inv_ref, dh_ref, db_ref, dsink_ref, dq_ref):
        band, first = _band_bias(BLK - 1)
        sinkrows = [_lanes([jnp.full((1, BLK), sink_ref[G * hk + j], F32) for j in range(G)])
                    for hk in range(SWA_KV)]

        @pl.when(pl.program_id(0) == 0)
        def _():
            dsink_ref[...] = jnp.zeros_like(dsink_ref)
            db_ref[...] = jnp.zeros_like(db_ref)
        for c in range(4, SWA_CHUNKS):
            dq_ref[c] = jnp.zeros((SEQ, 128), F32)

        def tile(hk, blk, rows_q, rows_k, bias, accs):
            nk = bias.shape[0]
            k2 = _stack_pair(qkv_ref.at[4 + hk][rows_k, :])
            v2 = _stack_pair(qkv_ref.at[6 + hk][rows_k, :])
            qcs, docs, sT, dpT = [], [], [], []
            for c in (2 * hk, 2 * hk + 1):
                qc, doc = qkv_ref.at[c][rows_q, :], do_ref.at[c][rows_q, :]
                s2, dp2 = _dot_nt(k2, qc), _dot_nt(v2, doc)
                sT += [s2[0:nk], s2[nk:2 * nk]]
                dpT += [dp2[0:nk], dp2[nk:2 * nk]]
                qcs.append(qc)
                docs.append(doc)
            heads = slice(G * hk, G * hk + G)
            lse_r = _lanes([lse_ref.at[blk][h:h + 1, :] for h in range(G * hk, G * hk + G)])
            delta_r = _lanes([delta_ref.at[blk][h:h + 1, :] for h in range(G * hk, G * hk + G)])
            yield
            pT = jnp.exp(_lanes(sT) + bias - lse_r)
            dsT = pT * (_lanes(dpT) - delta_r)
            dsb, pb = dsT.astype(MXU), pT.astype(MXU)
            accs[hk] = accs[hk] - jnp.exp(sinkrows[hk] - lse_r) * delta_r
            yield
            dk2 = dv2 = None
            for j, c in enumerate((2 * hk, 2 * hk + 1)):
                q0, q1 = slice(2 * j * BLK, (2 * j + 1) * BLK), slice((2 * j + 1) * BLK, (2 * j + 2) * BLK)
                ds2 = jnp.concatenate([dsb[:, q0], dsb[:, q1]], axis=0)
                p2 = jnp.concatenate([pb[:, q0], pb[:, q1]], axis=0)
                dq_ref.at[c][rows_q, :] = _dot_t0(ds2, k2)
                dk2 = _dot(ds2, qcs[j]) if dk2 is None else dk2 + _dot(ds2, qcs[j])
                dv2 = _dot(p2, docs[j]) if dv2 is None else dv2 + _dot(p2, docs[j])
            dq_ref.at[4 + hk][rows_k, :] += _pair_rows(dk2, nk)
            dq_ref.at[6 + hk][rows_k, :] += _pair_rows(dv2, nk)

        def run(tiles_of, accs):
            accs = list(accs)
            _interleave(tiles_of(accs))
            return tuple(accs)

        zero = jnp.zeros((1, G * BLK), F32)
        accs = run(lambda a: [tile(hk, 0, pl.ds(0, BLK), pl.ds(0, BLK), first, a) for hk in range(SWA_KV)],
                   (zero,) * SWA_KV)

        def part_of(rs, tab):
            lo = lax.broadcasted_iota(jnp.int32, (EP_ROWS, 128), 1) < HEAD

            def kv_grad(c):
                g0, g1 = dq_ref.at[c][rs, :], dq_ref.at[c + 1][rs, :]
                return jnp.where(lo, g0 + pltpu.roll(g0, HEAD, 1), g1 + pltpu.roll(g1, HEAD, 1))
            dq = _lanes([dq_ref.at[c][rs, :] for c in range(SWA_Q // 2)])
            return _lanes([_rope(dq, tab, -1) * Q_SCALE, _rope(kv_grad(4), tab, -1), kv_grad(6)])

        trips = (N_QBLK - 1) // SWA_UNROLL

        def loop(j, accs):
            def tiles_of(a):
                out = []
                for u in range(SWA_UNROLL):
                    i = 1 + j * SWA_UNROLL + u
                    r0 = pl.multiple_of(i * BLK, BLK)
                    rk = pl.multiple_of(i * BLK - BLK, BLK)
                    out += [tile(hk, i, pl.ds(r0, BLK), pl.ds(rk, 2 * BLK), band, a) for hk in range(SWA_KV)]
                return out + _dh_tiles_behind(part_of, dh_ref, db_ref, inv_ref, j, SWA_UNROLL)
            return run(tiles_of, accs)
        accs = lax.fori_loop(0, trips, loop, accs)
        _interleave([_dh_tile(part_of, dh_ref, db_ref, inv_ref, blk)
                     for blk in range(SWA_UNROLL * (trips - 1), N_QBLK)])
        for hk in range(SWA_KV):
            for j in range(G):
                tot = jnp.sum(accs[hk][:, j * BLK:(j + 1) * BLK], axis=1, keepdims=True)
                dsink_ref[G * hk + j:G * hk + j + 1, :] += jnp.broadcast_to(tot, (1, 128))

    stat = pl.BlockSpec((None, N_QBLK, 8, 128), lambda b: (b, 0, 0, 0))
    return pl.pallas_call(
        body, name="swa_bwd", grid=(B,),
        in_specs=[pl.BlockSpec(memory_space=pltpu.SMEM),
                  pl.BlockSpec((None, SWA_CHUNKS, SEQ, 128), lambda b: (b, 0, 0, 0)),
                  pl.BlockSpec((None, SWA_Q // 2, SEQ, 128), lambda b: (b, 0, 0, 0)), stat, stat,
                  pl.BlockSpec((SEQ, 384), lambda b: (0, 0))],
        out_specs=[pl.BlockSpec((SEQ, W_A + 2 * W_KVA), lambda b: (b, 0)),
                   pl.BlockSpec((8, W_A + 2 * W_KVA), lambda b: (0, 0)),
                   pl.BlockSpec((8, 128), lambda b: (0, 0))],
        out_shape=[jax.ShapeDtypeStruct((B * SEQ, W_A + 2 * W_KVA), MXU),
                   jax.ShapeDtypeStruct((8, W_A + 2 * W_KVA), F32), jax.ShapeDtypeStruct((8, 128), F32)],
        scratch_shapes=[pltpu.VMEM((SWA_CHUNKS, SEQ, 128), F32)],
        compiler_params=_cparams(("arbitrary",)),
    )(sinks, qkva, do, lse, delta, inv)


DILATIONS = (1, 4, 16)
DIL_PAIRS_H = DIL_H // 2


def _stream_rows(d, r, i, n):
    if d == 1:
        return pl.ds(pl.multiple_of(i * BLK, BLK), n)
    return pl.ds(r + i * (BLK * d), n, stride=d)


def _spread_matrix():
    row = lax.broadcasted_iota(jnp.int32, (PICK_ROWS, 128), 0)
    lane = lax.broadcasted_iota(jnp.int32, (PICK_ROWS, 128), 1)
    return ((row < 6) & ((row % 2 == 1) == (lane >= HEAD))).astype(MXU)


def _lanes_to_tokens(v0, v1, spread):
    n = v0.shape[1]
    row = lax.broadcasted_iota(jnp.int32, (PICK_ROWS, n), 0)
    a = jnp.zeros((PICK_ROWS, n), F32)
    for i, (p0, p1) in enumerate(zip(_split3(v0), _split3(v1))):
        a = jnp.where(row == 2 * i, p0.astype(F32), a)
        a = jnp.where(row == 2 * i + 1, p1.astype(F32), a)
    return _dot_t0(a.astype(MXU), spread)


def _tokens_to_lanes(t):
    r = t.T
    return r[0:1, :], r[HEAD:HEAD + 1, :]


DIL_UNROLL = 3


def _dil_schedule(body_first, body_next, behind=None):
    for p, d in sorted(enumerate(DILATIONS), key=lambda pd: -pd[1]):
        nblk = SEQ // d // BLK
        if d == 1:
            _interleave([body_first(p, d, 0)])
            def loop(j, c, p=p, d=d):
                _interleave([body_next(p, d, 0, 1 + DIL_UNROLL * j + u) for u in range(DIL_UNROLL)]
                            + (behind(j) if behind else []))
                return c
            lax.fori_loop(0, (nblk - 1) // DIL_UNROLL, loop, 0)
        elif nblk > 1:
            def loop(r, c, p=p, d=d, nblk=nblk):
                _interleave([body_first(p, d, r)] + [body_next(p, d, r, i) for i in range(1, nblk)])
                return c
            lax.fori_loop(0, d, loop, 0)
        else:
            def loop(j, c, p=p, d=d):
                _interleave([body_first(p, d, 4 * j + u) for u in range(4)])
                return c
            lax.fori_loop(0, d // 4, loop, 0)


def _dil_fwd(qkvb):
    B = qkvb.shape[0]

    def body(qkv_ref, o_ref):
        band, first = _band_bias(BLK)
        spread = _spread_matrix()

        def block(p, d, rows_q, rows_k, bias):
            nk = bias.shape[0]
            sT = []
            for c in range(DIL_PAIRS_H):
                qc = qkv_ref.at[c][rows_q, :].astype(MXU)
                s2 = _dot_nt(_stack_pair(qkv_ref.at[DIL_PAIRS_H + c][rows_k, :]), qc)
                sT += [s2[0:nk], s2[nk:2 * nk]]
            yield
            sT = _lanes(sT) + bias
            m = jnp.max(sT, axis=0, keepdims=True)
            pT = jnp.exp(sT - m)
            l = jnp.sum(pT, axis=0, keepdims=True)
            pnT = (pT * (1.0 / l)).astype(MXU)
            lse = m + jnp.log(l)
            yield
            for c in range(DIL_PAIRS_H):
                q0, q1 = slice(2 * c * BLK, (2 * c + 1) * BLK), slice((2 * c + 1) * BLK, (2 * c + 2) * BLK)
                p2 = jnp.concatenate([pnT[:, q0], pnT[:, q1]], axis=0)
                o_ref.at[p, c][rows_q, :] = _dot_t0(p2, _stack_pair(qkv_ref.at[2 * DIL_PAIRS_H + c][rows_k, :]))
                o_ref.at[p, DIL_PAIRS_H + c][rows_q, :] = _lanes_to_tokens(lse[:, q0], lse[:, q1], spread)

        def body_first(p, d, r):
            rows = _stream_rows(d, r, 0, BLK)
            return block(p, d, rows, rows, first)

        def body_next(p, d, r, i):
            return block(p, d, _stream_rows(d, r, i, BLK), _stream_rows(d, r, i - 1, 2 * BLK), band)

        _dil_schedule(body_first, body_next)

    return pl.pallas_call(
        body, name="dil_fwd", grid=(B,),
        in_specs=[pl.BlockSpec((None, 6, SEQ, 128), lambda b: (b, 0, 0, 0))],
        out_specs=pl.BlockSpec((None, 3, 4, SEQ, 128), lambda b: (b, 0, 0, 0, 0)),
        out_shape=jax.ShapeDtypeStruct((B, 3, 4, SEQ, 128), F32),
        compiler_params=_cparams(("arbitrary",)),
    )(qkvb)


def _reduce_scatter_ops(send_refs, land_refs, send_sems, recv_sems):
    x, y, c = _my_pos()
    me = 4 * x + 2 * y + c
    n = len(send_refs)

    def to_peer(a, j):
        return pltpu.make_async_remote_copy(
            src_ref=send_refs[a].at[j], dst_ref=land_refs[a].at[me], send_sem=send_sems.at[a, j],
            recv_sem=recv_sems.at[a, me], device_id=_dev_coords(j), device_id_type=MESH)

    def from_peer(a, m):
        return pltpu.make_async_remote_copy(
            src_ref=land_refs[a].at[m], dst_ref=land_refs[a].at[m], send_sem=send_sems.at[a, m],
            recv_sem=recv_sems.at[a, m], device_id=_dev_coords(m), device_id_type=MESH)

    def start():
        for j in range(N_DEV):
            @pl.when(me != j)
            def _(j=j):
                for a in range(n):
                    to_peer(a, j).start()
        for a in range(n):
            land_refs[a][me] = jnp.zeros(land_refs[a].shape[1:], land_refs[a].dtype)

    def finish(own_refs, out_refs):
        for m in range(N_DEV):
            @pl.when(me != m)
            def _(m=m):
                for a in range(n):
                    from_peer(a, m).wait_recv()
        for j in range(N_DEV):
            @pl.when(me != j)
            def _(j=j):
                for a in range(n):
                    to_peer(a, j).wait_send()
        for a in range(n):
            def chunk(i, carry, a=a):
                rs = pl.ds(pl.multiple_of(i * REDUCE_ROWS, REDUCE_ROWS), REDUCE_ROWS)
                g = own_refs[a][rs, :]
                for m in range(N_DEV):
                    g = g + land_refs[a][m, rs, :].astype(F32)
                out_refs[a][rs, :] = g
                return carry
            lax.fori_loop(0, own_refs[a].shape[0] // REDUCE_ROWS, chunk, 0)

    return start, finish


def _dil_bwd(qkvb, dobb, inv, sends, owns):
    B = qkvb.shape[0]
    n_rs = len(sends)

    def body(qkv_ref, dob_ref, inv_ref, *rest):
        send_refs, own_refs = rest[:n_rs], rest[n_rs:2 * n_rs]
        dh_ref, db_ref = rest[2 * n_rs:2 * n_rs + 2]
        out_refs = rest[2 * n_rs + 2:3 * n_rs + 2]
        dq_ref = rest[3 * n_rs + 2]
        land_refs = rest[3 * n_rs + 3:4 * n_rs + 3]
        send_sems, recv_sems = rest[4 * n_rs + 3:]
        rs_start, rs_finish = _reduce_scatter_ops(send_refs, land_refs, send_sems, recv_sems)
        pl.when(pl.program_id(0) == 0)(rs_start)

        band, first = _band_bias(BLK)
        dq_ref[...] = jnp.zeros_like(dq_ref)

        @pl.when(pl.program_id(0) == 0)
        def _():
            db_ref[...] = jnp.zeros_like(db_ref)

        def block(p, d, rows_q, rows_k, bias):
            nk = bias.shape[0]
            lo = lax.broadcasted_iota(jnp.int32, (nk, 128), 1) < HEAD
            qcs, docs, k2s, sT, dpT, lse, delta = [], [], [], [], [], [], []
            for c in range(DIL_PAIRS_H):
                qc = qkv_ref.at[c][rows_q, :].astype(MXU)
                doc = dob_ref.at[c][rows_q, :].astype(MXU)
                k2 = _stack_pair(qkv_ref.at[DIL_PAIRS_H + c][rows_k, :])
                s2 = _dot_nt(k2, qc)
                dp2 = _dot_nt(_stack_pair(qkv_ref.at[2 * DIL_PAIRS_H + c][rows_k, :]), doc)
                sT += [s2[0:nk], s2[nk:2 * nk]]
                dpT += [dp2[0:nk], dp2[nk:2 * nk]]
                lse += _tokens_to_lanes(dob_ref.at[DIL_PAIRS_H + c][rows_q, :])
                delta += _tokens_to_lanes(dob_ref.at[2 * DIL_PAIRS_H + c][rows_q, :])
                qcs.append(qc)
                docs.append(doc)
                k2s.append(k2)
            yield
            pT = jnp.exp(_lanes(sT) + bias - _lanes(lse))
            dsT = pT * (_lanes(dpT) - _lanes(delta))
            dsb, pb = dsT.astype(MXU), pT.astype(MXU)
            yield
            for c in range(DIL_PAIRS_H):
                q0, q1 = slice(2 * c * BLK, (2 * c + 1) * BLK), slice((2 * c + 1) * BLK, (2 * c + 2) * BLK)
                ds2 = jnp.concatenate([dsb[:, q0], dsb[:, q1]], axis=0)
                p2 = jnp.concatenate([pb[:, q0], pb[:, q1]], axis=0)
                dq_ref.at[c][rows_q, :] += _dot_t0(ds2, k2s[c])
                dk2, dv2 = _dot(ds2, qcs[c]), _dot(p2, docs[c])
                dq_ref.at[DIL_PAIRS_H + c][rows_k, :] += jnp.where(lo, dk2[0:nk], dk2[nk:2 * nk])
                dq_ref.at[2 * DIL_PAIRS_H + c][rows_k, :] += jnp.where(lo, dv2[0:nk], dv2[nk:2 * nk])

        def body_first(p, d, r):
            rows = _stream_rows(d, r, 0, BLK)
            return block(p, d, rows, rows, first)

        def body_next(p, d, r, i):
            return block(p, d, _stream_rows(d, r, i, BLK), _stream_rows(d, r, i - 1, 2 * BLK), band)

        def part_of(rs, tab):
            q, k, v = [_lanes([dq_ref.at[2 * j][rs, :], dq_ref.at[2 * j + 1][rs, :]]) for j in range(3)]
            return _lanes([_rope(q, tab, -1) * Q_SCALE, _rope(k, tab, -1), v])

        _dil_schedule(body_first, body_next,
                      lambda j: _dh_tiles_behind(part_of, dh_ref, db_ref, inv_ref, j, DIL_UNROLL))
        _interleave([_dh_tile(part_of, dh_ref, db_ref, inv_ref, blk)
                     for blk in range(DIL_UNROLL * ((N_QBLK - 1) // DIL_UNROLL - 1), N_QBLK)])

        @pl.when(pl.program_id(0) == pl.num_programs(0) - 1)
        def _():
            rs_finish(own_refs, out_refs)

    spec = pl.BlockSpec((None, 6, SEQ, 128), lambda b: (b, 0, 0, 0))
    any_spec = pl.BlockSpec(memory_space=pl.ANY)
    vmem = pl.BlockSpec(memory_space=pltpu.VMEM)
    outs = pl.pallas_call(
        body, name="dil_bwd", grid=(B,),
        in_specs=[spec, spec, pl.BlockSpec((SEQ, 384), lambda b: (0, 0))] + [any_spec] * n_rs + [vmem] * n_rs,
        out_specs=[pl.BlockSpec((SEQ, 3 * W_B), lambda b: (b, 0)), pl.BlockSpec((8, 3 * W_B), lambda b: (0, 0))]
        + [vmem] * n_rs,
        out_shape=[jax.ShapeDtypeStruct((B * SEQ, 3 * W_B), MXU), jax.ShapeDtypeStruct((8, 3 * W_B), F32)]
        + [jax.ShapeDtypeStruct(o.shape, F32) for o in owns],
        scratch_shapes=[pltpu.VMEM((6, SEQ, 128), F32)] + [pltpu.VMEM(s.shape, s.dtype) for s in sends]
        + [pltpu.SemaphoreType.DMA((n_rs, N_DEV)), pltpu.SemaphoreType.DMA((n_rs, N_DEV))],
        compiler_params=_cparams(("arbitrary",)),
    )(qkvb, dobb, inv, *sends, *owns)
    return outs[0], outs[1], outs[2:]


MEM_UNROLL = 4
MEM_PAIRS = MEM_H // 2


def _mem_attn_fwd(qc, mem, w_mem):
    B = qc.shape[0]

    def body(q_ref, mem_ref, w_ref, o_ref, lse_ref, mkv_ref, k2_ref, v2_ref):
        mkv = _dot(mem_ref[...].astype(MXU), w_ref[...])
        mkv_ref[...] = mkv.astype(MXU)
        for c in range(MEM_PAIRS):
            k2_ref[c] = _stack_pair(mkv[:, c * 128:(c + 1) * 128])
            v2_ref[c] = _stack_pair(mkv[:, W_C + c * 128:W_C + (c + 1) * 128])
        lse_ref[...] = jnp.zeros_like(lse_ref)

        def tile(blk):
            rows = pl.ds(pl.multiple_of(blk * BLK, BLK), BLK)
            sT = []
            for c in range(MEM_PAIRS):
                s2 = _dot_nt(k2_ref[c], q_ref.at[c][rows, :])
                sT += [s2[0:MEM_LEN], s2[MEM_LEN:2 * MEM_LEN]]
            yield
            pnT, lse = _softmax_cols(_lanes(sT))
            yield
            for c in range(MEM_PAIRS):
                p2 = jnp.concatenate([pnT[:, 2 * c * BLK:(2 * c + 1) * BLK],
                                      pnT[:, (2 * c + 1) * BLK:(2 * c + 2) * BLK]], axis=0)
                o_ref.at[c][rows, :] = _dot_t0(p2, v2_ref[c])
            for h in range(MEM_H):
                lse_ref.at[blk][h:h + 1, :] = lse[:, h * BLK:(h + 1) * BLK]

        def loop(j, carry):
            _interleave([tile(j * MEM_UNROLL + u) for u in range(MEM_UNROLL)])
            return carry
        lax.fori_loop(0, N_QBLK // MEM_UNROLL, loop, 0)

    return pl.pallas_call(
        body, name="mem_attn_fwd", grid=(B,),
        in_specs=[pl.BlockSpec((None, MEM_PAIRS, SEQ, 128), lambda b: (b, 0, 0, 0)),
                  pl.BlockSpec((None, MEM_LEN, D_MODEL), lambda b: (b, 0, 0)),
                  pl.BlockSpec((D_MODEL, 2 * W_C), lambda b: (0, 0))],
        out_specs=[pl.BlockSpec((None, MEM_PAIRS, SEQ, 128), lambda b: (b, 0, 0, 0)),
                   pl.BlockSpec((None, N_QBLK, 8, 128), lambda b: (b, 0, 0, 0)),
                   pl.BlockSpec((None, MEM_LEN, 2 * W_C), lambda b: (b, 0, 0))],
        out_shape=[jax.ShapeDtypeStruct((B, MEM_PAIRS, SEQ, 128), F32),
                   jax.ShapeDtypeStruct((B, N_QBLK, 8, 128), F32),
                   jax.ShapeDtypeStruct((B, MEM_LEN, 2 * W_C), MXU)],
        scratch_shapes=[pltpu.VMEM((MEM_PAIRS, 2 * MEM_LEN, 128), MXU), pltpu.VMEM((MEM_PAIRS, 2 * MEM_LEN, 128), MXU)],
        compiler_params=_cparams(("arbitrary",)),
    )(qc, mem, w_mem)


def _mem_attn_bwd(qc, mkv, do, lse, delta, mem):
    B = qc.shape[0]

    def body(q_ref, mkv_ref, do_ref, lse_ref, delta_ref, mem_ref, dh_ref, db_ref, dw_ref,
             dq_ref, dmkv_ref, k2_ref, v2_ref):
        @pl.when(pl.program_id(0) == 0)
        def _():
            dw_ref[...] = jnp.zeros_like(dw_ref)
            db_ref[...] = jnp.zeros_like(db_ref)
        dmkv_ref[...] = jnp.zeros_like(dmkv_ref)
        dq_ref[...] = jnp.zeros_like(dq_ref)
        for c in range(MEM_PAIRS):
            k2_ref[c] = _stack_pair(mkv_ref[:, c * 128:(c + 1) * 128])
            v2_ref[c] = _stack_pair(mkv_ref[:, W_C + c * 128:W_C + (c + 1) * 128])

        def tile(blk):
            rows = pl.ds(pl.multiple_of(blk * BLK, BLK), BLK)
            qcs, docs, sT, dpT = [], [], [], []
            for c in range(MEM_PAIRS):
                qc_, doc = q_ref.at[c][rows, :], do_ref.at[c][rows, :]
                s2, dp2 = _dot_nt(k2_ref[c], qc_), _dot_nt(v2_ref[c], doc)
                sT += [s2[0:MEM_LEN], s2[MEM_LEN:2 * MEM_LEN]]
                dpT += [dp2[0:MEM_LEN], dp2[MEM_LEN:2 * MEM_LEN]]
                qcs.append(qc_)
                docs.append(doc)
            lse_r = _lanes([lse_ref.at[blk][h:h + 1, :] for h in range(MEM_H)])
            delta_r = _lanes([delta_ref.at[blk][h:h + 1, :] for h in range(MEM_H)])
            yield
            pT = jnp.exp(_lanes(sT) - lse_r)
            dsT = pT * (_lanes(dpT) - delta_r)
            dsb, pb = dsT.astype(MXU), pT.astype(MXU)
            yield
            for c in range(MEM_PAIRS):
                q0, q1 = slice(2 * c * BLK, (2 * c + 1) * BLK), slice((2 * c + 1) * BLK, (2 * c + 2) * BLK)
                ds2 = jnp.concatenate([dsb[:, q0], dsb[:, q1]], axis=0)
                p2 = jnp.concatenate([pb[:, q0], pb[:, q1]], axis=0)
                dq_ref.at[c][rows, :] = _dot_t0(ds2, k2_ref[c])
                dmkv_ref[:, c * 128:(c + 1) * 128] += _pair_rows(_dot(ds2, qcs[c]), MEM_LEN)
                dmkv_ref[:, W_C + c * 128:W_C + (c + 1) * 128] += _pair_rows(_dot(p2, docs[c]), MEM_LEN)

        def part_of(rs, tab):
            return _lanes([dq_ref.at[c][rs, :] for c in range(MEM_PAIRS)]) * Q_SCALE

        trips = N_QBLK // MEM_UNROLL

        def loop(j, carry):
            _interleave([tile(j * MEM_UNROLL + u) for u in range(MEM_UNROLL)]
                        + _dh_tiles_behind(part_of, dh_ref, db_ref, None, j, MEM_UNROLL))
            return carry
        lax.fori_loop(0, trips, loop, 0)
        _interleave([_dh_tile(part_of, dh_ref, db_ref, None, blk) for blk in range(MEM_UNROLL * (trips - 1), N_QBLK)])
        dw_ref[...] += _dot_tn(mem_ref[...], dmkv_ref[...].astype(MXU))

    stat = pl.BlockSpec((None, N_QBLK, 8, 128), lambda b: (b, 0, 0, 0))
    pairs = pl.BlockSpec((None, MEM_PAIRS, SEQ, 128), lambda b: (b, 0, 0, 0))
    return pl.pallas_call(
        body, name="mem_attn_bwd", grid=(B,),
        in_specs=[pairs, pl.BlockSpec((None, MEM_LEN, 2 * W_C), lambda b: (b, 0, 0)), pairs, stat, stat,
                  pl.BlockSpec((None, MEM_LEN, D_MODEL), lambda b: (b, 0, 0))],
        out_specs=[pl.BlockSpec((SEQ, W_C), lambda b: (b, 0)), pl.BlockSpec((8, W_C), lambda b: (0, 0)),
                   pl.BlockSpec((D_MODEL, 2 * W_C), lambda b: (0, 0))],
        out_shape=[jax.ShapeDtypeStruct((B * SEQ, W_C), MXU), jax.ShapeDtypeStruct((8, W_C), F32),
                   jax.ShapeDtypeStruct((D_MODEL, 2 * W_C), F32)],
        scratch_shapes=[pltpu.VMEM((MEM_PAIRS, SEQ, 128), F32), pltpu.VMEM((MEM_LEN, 2 * W_C), F32),
                        pltpu.VMEM((MEM_PAIRS, 2 * MEM_LEN, 128), MXU), pltpu.VMEM((MEM_PAIRS, 2 * MEM_LEN, 128), MXU)],
        compiler_params=_cparams(("arbitrary",)),
    )(qc, mkv, do, lse, delta, mem)


def _headsum(t, e):
    if MXU == F32:
        return _dot(t, e)
    hi = t.astype(MXU)
    lo = (t - hi.astype(F32)).astype(MXU)
    return _dot(hi, e) + _dot(lo, e)


def _heads_to_rows(t, e):
    return sum(_dot_nt(e, part) for part in _split3(t))


POST_TM = 512
POST_ROWS = 256


def _post(o_a, olse_b, o_c, z, x2, tgt, g, gain, bias, w_out, hsum, hrows):
    T = x2.shape[0]
    tm = POST_TM
    nt = SEQ // tm

    def body(oa_ref, ob_ref, oc_ref, z_ref, x_ref, t_ref, g_ref, gain_ref, bias_ref, w_ref, e_ref, er_ref,
             gx_ref, doa_ref, dela_ref, dobb_ref, doc_ref, delc_ref, dz_ref, dw_ref, small_ref, loss_ref):
        @pl.when(pl.program_id(0) == 0)
        def _():
            dw_ref[...] = jnp.zeros_like(dw_ref)
            small_ref[...] = jnp.zeros_like(small_ref)
            loss_ref[...] = jnp.zeros_like(loss_ref)

        gg = g_ref[...]
        gain_v = gain_ref[...]
        gain_s = gain_v * (1.0 / D_MODEL)
        bias_v = bias_ref[...]
        w = w_ref[...]

        def rms(o):
            rr = lax.rsqrt(jnp.mean(o * o, axis=1, keepdims=True) + RMS_EPS)
            return o * rr, rr

        def rows_of(rs, results):
            oa = _lanes([oa_ref.at[c][rs, :] for c in range(SWA_Q // 2)])
            (o1, l1), (o4, l4), (o16, l16) = [
                (_lanes([ob_ref.at[p, 0][rs, :], ob_ref.at[p, 1][rs, :]]),
                 _lanes([ob_ref.at[p, 2][rs, :], ob_ref.at[p, 3][rs, :]])) for p in range(3)]
            mx = jnp.maximum(jnp.maximum(l1, l4), l16)
            e1, e4, e16 = jnp.exp(l1 - mx), jnp.exp(l4 - mx), jnp.exp(l16 - mx)
            den = e1 + e4 + e16
            ob = (e1 * o1 + e4 * o4 + e16 * o16) / den
            lse_b = mx + jnp.log(den)
            oc = _lanes([oc_ref.at[c][rs, :] for c in range(MEM_PAIRS)])
            na, ra = rms(oa)
            nb, rb = rms(ob)
            nc, rc = rms(oc)
            n = jnp.concatenate([na, nb, nc], axis=1)
            zz = z_ref[rs, :]
            sig = 0.5 * jnp.tanh(0.5 * zz) + 0.5
            sz = zz * sig
            gs = gg * sz
            u = n * gs
            yo = _dot(u.astype(MXU), w)
            yield
            r = ALPHA * x_ref[rs, :] + yo
            rc0 = r - jnp.mean(r, axis=1, keepdims=True)
            rstd = lax.rsqrt(jnp.mean(rc0 * rc0, axis=1, keepdims=True) + LN_EPS)
            xhat = rc0 * rstd
            err = xhat * gain_v + bias_v - t_ref[rs, :]
            dxh = err * gain_s
            dr = rstd * (dxh - jnp.mean(dxh, axis=1, keepdims=True)
                         - xhat * jnp.mean(dxh * xhat, axis=1, keepdims=True))
            gx_ref[rs, :] = ALPHA * dr
            drb = dr.astype(MXU)
            du = _dot_nt(drb, w)
            yield
            dun = du * n
            dz = dun * (gg * (sig + sz * (1.0 - sig)))
            dz_ref[rs, :] = dz.astype(MXU)
            dn = du * gs

            def branch(lo, hi, nbr, rr):
                dnb = dn[:, lo:hi]
                return rr * (dnb - nbr * jnp.mean(dnb * nbr, axis=1, keepdims=True))

            def to_kernel(dob, o, do_ref, delta_ref):
                wd = dob.shape[1]
                for c in range(wd // 128):
                    do_ref.at[c][rs, :] = dob[:, c * 128:(c + 1) * 128].astype(do_ref.dtype)
                dT = _heads_to_rows(dob * o, er_ref[:, 0:wd])
                for jb in range((rs.stop - rs.start) // BLK):
                    delta_ref[rs.start // BLK + jb] = dT[0:8, jb * BLK:(jb + 1) * BLK]

            to_kernel(branch(0, W_A, na, ra), oa, doa_ref, dela_ref)
            to_kernel(branch(W_A + W_B, D_MIX, nc, rc), oc, doc_ref, delc_ref)
            dob = branch(W_A, W_A + W_B, nb, rb)
            for j, t in enumerate((dob, lse_b, _headsum(dob * ob, e_ref[...]))):
                for c in range(W_B // 128):
                    dobb_ref.at[j * (W_B // 128) + c][rs, :] = t[:, c * 128:(c + 1) * 128]
            csum = lambda t: jnp.sum(t, axis=0, keepdims=True)
            results.append((u, drb, jnp.sum(err * err), csum(err * xhat), csum(err), csum(dun * sz), csum(dz)))

        parts = []
        _interleave([rows_of(slice(k * POST_ROWS, (k + 1) * POST_ROWS), parts) for k in range(tm // POST_ROWS)])
        tot = [sum(p[i] for p in parts) for i in range(2, 7)]
        dw_ref[...] += _dot_tn(jnp.concatenate([p[0] for p in parts], axis=0),
                               jnp.concatenate([p[1] for p in parts], axis=0))
        loss_ref[...] += 0.5 * tot[0] * (1.0 / D_MODEL)
        small_ref[0:1, :] += tot[1] * (1.0 / D_MODEL)
        small_ref[1:2, :] += tot[2] * (1.0 / D_MODEL)
        small_ref[2:3, :] += tot[3]
        small_ref[3:4, :] += tot[4]

    B = T // SEQ
    row = lambda w: pl.BlockSpec((tm, w), lambda i: (i, 0))
    full = lambda a, b: pl.BlockSpec((a, b), lambda i: (0, 0))
    chunked = lambda n: pl.BlockSpec((None, n, tm, 128), lambda i: (i // nt, 0, i % nt, 0))
    stat = pl.BlockSpec((None, tm // BLK, 8, 128), lambda i: (i // nt, i % nt, 0, 0))
    return pl.pallas_call(
        body, name="post_fwd_bwd", grid=(T // tm,),
        in_specs=[chunked(SWA_Q // 2), pl.BlockSpec((None, 3, 4, tm, 128), lambda i: (i // nt, 0, 0, i % nt, 0)),
                  chunked(MEM_PAIRS),
                  row(D_MIX), row(D_MODEL), row(D_MODEL),
                  full(1, D_MIX), full(1, D_MODEL), full(1, D_MODEL), full(D_MIX, D_MODEL), full(W_B, W_B),
                  full(PICK_ROWS, W_A)],
        out_specs=[row(D_MODEL), chunked(SWA_Q // 2), stat, chunked(6), chunked(MEM_PAIRS), stat, row(D_MIX),
                   full(D_MIX, D_MODEL), full(8, D_MODEL), full(8, 128)],
        out_shape=[jax.ShapeDtypeStruct((T, D_MODEL), F32),
                   jax.ShapeDtypeStruct((B, SWA_Q // 2, SEQ, 128), MXU),
                   jax.ShapeDtypeStruct((B, N_QBLK, 8, 128), F32),
                   jax.ShapeDtypeStruct((B, 6, SEQ, 128), F32),
                   jax.ShapeDtypeStruct((B, MEM_PAIRS, SEQ, 128), MXU),
                   jax.ShapeDtypeStruct((B, N_QBLK, 8, 128), F32),
                   jax.ShapeDtypeStruct((T, D_MIX), MXU),
                   jax.ShapeDtypeStruct((D_MIX, D_MODEL), F32),
                   jax.ShapeDtypeStruct((8, D_MODEL), F32),
                   jax.ShapeDtypeStruct((8, 128), F32)],
        compiler_params=_cparams(("arbitrary",)),
    )(o_a, olse_b, o_c, z, x2, tgt, g, gain, bias, w_out, hsum, hrows)


TAIL_TK = 512
TAIL_TN = D_IN // 2
TAIL_TM = 256
REDUCE_ROWS = 128


DH_SPLITS = (0, W_A + 2 * W_KVA, W_A + 2 * W_KVA + 3 * W_B, D_IN - D_MIX, D_IN)


def _tail(xt, dhs, gx1, w_in, small_g):
    T = xt.shape[1]
    dh = dhs[0]
    c0, c1, c2, c3, c4 = DH_SPLITS
    assert c1 < TAIL_TN < c2 and (TAIL_TN - c1) % 128 == 0
    kt = T // TAIL_TK
    ndw = (D_IN // TAIL_TN) * kt
    nsteps = ndw + T // TAIL_TM
    n_pass = D_IN // TAIL_TN
    assert n_pass == 2 and TAIL_TN == 4 * COLS_PER_DEV and kt >= 2
    pay = dh.dtype
    blk_shape = (D_MODEL, COLS_PER_DEV)
    n_half = 2 * n_pass
    n_chip = N_DEV // 2

    def body(xt_ref, a1_ref, b1_ref, b2_ref, c1_ref, z1_ref, a2_ref, b3_ref, c2_ref, z2_ref,
             gx_ref, w_hbm, sg_ref, dx_ref, gin_ref, gsm_ref,
             acc_ref, w_ref, mine_ref, stagea_ref, landa_ref, stageb_ref, landb_ref, own_ref, lsm_ref,
             sa_sems, ra_sems, sb_sems, rb_sems, ss_sems, rs_sems, w_sem):
        s = pl.program_id(0)
        x, y, c = _my_pos()
        me = 4 * x + 2 * y + c
        chip = 2 * x + y

        def to_sibling(q):
            return pltpu.make_async_remote_copy(
                src_ref=stagea_ref.at[q], dst_ref=landa_ref.at[q], send_sem=sa_sems.at[q], recv_sem=ra_sems.at[q],
                device_id=(x, y, 1 - c), device_id_type=MESH)

        def to_owner(q):
            return pltpu.make_async_remote_copy(
                src_ref=stageb_ref.at[q], dst_ref=landb_ref.at[chip], send_sem=sb_sems.at[q],
                recv_sem=rb_sems.at[chip], device_id=(q // 2, q % 2, c), device_id_type=MESH)

        def from_chip(m):
            return pltpu.make_async_remote_copy(
                src_ref=landb_ref.at[m], dst_ref=landb_ref.at[m], send_sem=sb_sems.at[m], recv_sem=rb_sems.at[m],
                device_id=(m // 2, m % 2, c), device_id_type=MESH)

        def is_me(q):
            return (x == q // 2) & (y == q % 2)

        def small_to(j):
            return pltpu.make_async_remote_copy(
                src_ref=sg_ref, dst_ref=lsm_ref.at[me], send_sem=ss_sems.at[j], recv_sem=rs_sems.at[me],
                device_id=_dev_coords(j), device_id_type=MESH)

        def small_from(m):
            return pltpu.make_async_remote_copy(
                src_ref=lsm_ref.at[m], dst_ref=lsm_ref.at[m], send_sem=ss_sems.at[m], recv_sem=rs_sems.at[m],
                device_id=_dev_coords(m), device_id_type=MESH)

        w_copy = pltpu.make_async_copy(w_hbm, w_ref, w_sem)

        @pl.when(s == 0)
        def _():
            w_copy.start()
            for j in range(N_DEV):
                pl.when(me != j)(small_to(j).start)
            lsm_ref[me] = sg_ref[...]
            landb_ref[chip] = jnp.zeros(blk_shape, pay)

        @pl.when(s < ndw)
        def _():
            @pl.when(s % kt == 0)
            def _():
                acc_ref[...] = jnp.zeros_like(acc_ref)
            xt_ = xt_ref[...]
            @pl.when(s < kt)
            def _():
                acc_ref[:, 0:c1] += _dot(xt_, a1_ref[...])
                acc_ref[:, c1:TAIL_TN] += _dot(xt_, b1_ref[...])

            @pl.when(s >= kt)
            def _():
                acc_ref[:, 0:c2 - TAIL_TN] += _dot(xt_, b2_ref[...])
                acc_ref[:, c2 - TAIL_TN:c3 - TAIL_TN] += _dot(xt_, c1_ref[...])
                acc_ref[:, c3 - TAIL_TN:c4 - TAIL_TN] += _dot(xt_, z1_ref[...])

        for p in range(n_pass):
            @pl.when(s == p * kt + kt - 1)
            def _(p=p):
                for cc in range(2):
                    @pl.when(c == cc)
                    def _(cc=cc):
                        for yo in range(2):
                            q = 2 * p + yo
                            same, other = 2 * yo + cc, 2 * yo + 1 - cc
                            mine_ref[q] = acc_ref[:, same * COLS_PER_DEV:(same + 1) * COLS_PER_DEV]
                            stagea_ref[q] = acc_ref[:, other * COLS_PER_DEV:(other + 1) * COLS_PER_DEV].astype(pay)
                for yo in range(2):
                    to_sibling(2 * p + yo).start()

            @pl.when(s == (p + 1) * kt + 1)
            def _(p=p):
                for yo in range(2):
                    q = 2 * p + yo
                    to_sibling(q).wait_recv()

                    def chunk(i, carry, q=q):
                        rs = pl.ds(pl.multiple_of(i * REDUCE_ROWS, REDUCE_ROWS), REDUCE_ROWS)
                        tot = mine_ref[q, rs, :] + landa_ref[q, rs, :].astype(F32)

                        @pl.when(is_me(q))
                        def _():
                            own_ref[rs, :] = tot

                        @pl.when(jnp.logical_not(is_me(q)))
                        def _():
                            stageb_ref[q, rs, :] = tot.astype(pay)
                        return carry
                    lax.fori_loop(0, D_MODEL // REDUCE_ROWS, chunk, 0)
                    pl.when(jnp.logical_not(is_me(q)))(to_owner(q).start)

        @pl.when(s >= ndw)
        def _():
            pl.when(s == ndw)(w_copy.wait)
            dx_ref[...] = (_dot_nt(a2_ref[...], w_ref[:, c0:c1]) + _dot_nt(b3_ref[...], w_ref[:, c1:c2])
                           + _dot_nt(c2_ref[...], w_ref[:, c2:c3]) + _dot_nt(z2_ref[...], w_ref[:, c3:c4])
                           + gx_ref[...])

        @pl.when(s == nsteps - 1)
        def _():
            for m in range(n_chip):
                pl.when(m != chip)(from_chip(m).wait_recv)
            for m in range(N_DEV):
                pl.when(me != m)(small_from(m).wait_recv)
            for q in range(n_half):
                to_sibling(q).wait_send()
                pl.when(jnp.logical_not(is_me(q)))(to_owner(q).wait_send)
            for j in range(N_DEV):
                pl.when(me != j)(small_to(j).wait_send)

            def chunk(i, carry):
                rs = pl.ds(pl.multiple_of(i * REDUCE_ROWS, REDUCE_ROWS), REDUCE_ROWS)
                g = own_ref[rs, :]
                for m in range(n_chip):
                    g = g + landb_ref[m, rs, :].astype(F32)
                gin_ref[rs, :] = g
                return carry
            lax.fori_loop(0, D_MODEL // REDUCE_ROWS, chunk, 0)
            g = lsm_ref[0]
            for m in range(1, N_DEV):
                g = g + lsm_ref[m]
            gsm_ref[...] = g

    dw_step = lambda s: jnp.minimum(s, ndw - 1)
    dx_step = lambda s: jnp.maximum(s - ndw, 0)
    pass0 = lambda s: jnp.minimum(s, kt - 1)
    pass1 = lambda s: jnp.clip(s - kt, 0, kt - 1)
    any_spec = pl.BlockSpec(memory_space=pl.ANY)
    vmem = pl.BlockSpec(memory_space=pltpu.VMEM)
    dma = pltpu.SemaphoreType.DMA
    scratch = [pltpu.VMEM((D_MODEL, TAIL_TN), F32), pltpu.VMEM((D_MODEL, D_IN), w_in.dtype),
               pltpu.VMEM((n_half,) + blk_shape, F32),
               pltpu.VMEM((n_half,) + blk_shape, pay), pltpu.VMEM((n_half,) + blk_shape, pay),
               pltpu.VMEM((n_half,) + blk_shape, pay), pltpu.VMEM((n_chip,) + blk_shape, pay),
               pltpu.VMEM(blk_shape, F32), pltpu.VMEM((N_DEV,) + small_g.shape, F32),
               dma((n_half,)), dma((n_half,)), dma((n_half,)), dma((n_chip,)), dma((N_DEV,)), dma((N_DEV,)), dma]
    return pl.pallas_call(
        body, name="tail_dw_dx_reduce", grid=(nsteps,),
        in_specs=[pl.BlockSpec((D_MODEL, TAIL_TK), lambda s: (0, dw_step(s) % kt)),
                  pl.BlockSpec((TAIL_TK, c1 - c0), lambda s: (pass0(s), 0)),
                  pl.BlockSpec((TAIL_TK, TAIL_TN - c1), lambda s: (pass0(s), 0)),
                  pl.BlockSpec((TAIL_TK, 128), lambda s: (pass1(s), (TAIL_TN - c1) // 128)),
                  pl.BlockSpec((TAIL_TK, c3 - c2), lambda s: (pass1(s), 0)),
                  pl.BlockSpec((TAIL_TK, c4 - c3), lambda s: (pass1(s), 0)),
                  pl.BlockSpec((TAIL_TM, c1 - c0), lambda s: (dx_step(s), 0)),
                  pl.BlockSpec((TAIL_TM, c2 - c1), lambda s: (dx_step(s), 0)),
                  pl.BlockSpec((TAIL_TM, c3 - c2), lambda s: (dx_step(s), 0)),
                  pl.BlockSpec((TAIL_TM, c4 - c3), lambda s: (dx_step(s), 0)),
                  pl.BlockSpec((TAIL_TM, D_MODEL), lambda s: (dx_step(s), 0)),
                  any_spec, vmem],
        out_specs=[pl.BlockSpec((TAIL_TM, D_MODEL), lambda s: (dx_step(s), 0)), vmem, vmem],
        out_shape=[jax.ShapeDtypeStruct((T, D_MODEL), F32), jax.ShapeDtypeStruct(blk_shape, F32),
                   jax.ShapeDtypeStruct(small_g.shape, F32)],
        scratch_shapes=scratch,
        compiler_params=_cparams(("arbitrary",)),
    )(xt, dhs[0], dhs[1], dhs[1], dhs[2], dhs[3], dhs[0], dhs[1], dhs[2], dhs[3], gx1, w_in, small_g)


def _adam_update(grads, params, carried):
    n = len(grads)

    def body(*refs):
        g_refs, p_refs, o_refs = refs[1:1 + n], refs[1 + n:1 + 4 * n], refs[2 + 4 * n:]
        for a in range(n):
            rows = g_refs[a].shape[0]
            cr = REDUCE_ROWS if rows % REDUCE_ROWS == 0 else rows
            flat2 = lambda r: r.at[0] if len(r.shape) == 3 else r
            w_ref, m_ref, v_ref = [flat2(r) for r in p_refs[3 * a:3 * a + 3]]
            go_ref, d_ref, nm_ref, nv_ref = [flat2(r) for r in o_refs[4 * a:4 * a + 4]]

            def chunk(i, carry, cr=cr, g_ref=g_refs[a], w_ref=w_ref, m_ref=m_ref, v_ref=v_ref,
                      go_ref=go_ref, d_ref=d_ref, nm_ref=nm_ref, nv_ref=nv_ref):
                rs = pl.ds(pl.multiple_of(i * cr, cr), cr)
                g = g_ref[rs, :]
                go_ref[rs, :] = g
                d_ref[rs, :], nm_ref[rs, :], nv_ref[rs, :] = _adamw(w_ref[rs, :], g, m_ref[rs, :], v_ref[rs, :])
                return carry
            lax.fori_loop(0, rows // cr, chunk, 0)

    vmem = pl.BlockSpec(memory_space=pltpu.VMEM)
    any_spec = pl.BlockSpec(memory_space=pl.ANY)
    flat = [p for grp in params for p in grp]
    outs = pl.pallas_call(
        body, name="adamw", in_specs=[any_spec] + [vmem] * (4 * n), out_specs=[any_spec] + [vmem] * (4 * n),
        out_shape=[jax.ShapeDtypeStruct(carried.shape, carried.dtype)]
        + [jax.ShapeDtypeStruct(grp[0].shape, F32) for grp in params for _ in range(4)],
        input_output_aliases={0: 0},
        compiler_params=pltpu.CompilerParams(vmem_limit_bytes=VMEM_LIMIT),
    )(carried, *grads, *flat)
    return [outs[1 + 4 * a:5 + 4 * a] for a in range(n)], outs[0]


def _step(x, mem, w_in_s, w_mem_s, w_out_s, b_in, sinks, g, gain, bias, tgt):
    B = x.shape[0]
    T = B * SEQ
    x2 = x.reshape(T, D_MODEL)
    t2 = tgt.reshape(T, D_MODEL)
    tab = _rope_inv()
    lane = jnp.arange(W_A)
    hsum = (lane[:W_B, None] // HEAD == lane[None, :W_B] // HEAD).astype(MXU)
    hrows = (jnp.arange(PICK_ROWS)[:, None] == lane[None, :] // HEAD).astype(MXU)
    me = 4 * lax.axis_index("x") + 2 * lax.axis_index("y") + lax.axis_index("c")

    (w_in_all,) = _gather_weights([w_in_s])
    qkva, qkvb, qc, z, w_in, xt, rope_tab, w_mem_all, w_out_all = _in_proj(
        x2, w_in_all, b_in, tab, [w_mem_s, w_out_s])
    w_mem = w_mem_all.reshape(D_MODEL, 2 * W_C)
    w_out = w_out_all.reshape(D_MIX, D_MODEL)

    o_a, lse_a = _swa_fwd(qkva, sinks)
    olse_b = _dil_fwd(qkvb)
    o_c, lse_c, mkv = _mem_attn_fwd(qc, mem, w_mem)

    gx1, do_a, delta_a, dobb, do_c, delta_c, dz, dw_out, small, loss = _post(
        o_a, olse_b, o_c, z, x2, t2, g, gain, bias, w_out, hsum, hrows)

    dh_c, db_c, dw_mem = _mem_attn_bwd(qc, mkv, do_c, lse_c, delta_c, mem)
    blocks = [dw_mem.reshape(N_DEV, ROWS_PER_DEV, 2 * W_C), dw_out.reshape(N_DEV, ROWS_PER_DEV, D_MODEL)]
    sends = [b.astype(MXU) for b in blocks]
    owns = [lax.dynamic_index_in_dim(b, me, axis=0, keepdims=False) for b in blocks]
    dh_b, db_b, (g_mem, g_out) = _dil_bwd(qkvb, dobb, rope_tab, sends, owns)
    dh_a, db_a, dsink = _swa_bwd(qkva, do_a, lse_a, delta_a, sinks, rope_tab)

    small_g = _pack_small(dict(b_in=jnp.concatenate([db_a[0], db_b[0], db_c[0], small[3]]), sinks=dsink[:, 0],
                               g=small[2], gain=small[0], bias=small[1], loss=loss[0, 0]))
    grad_x, g_in, g_small = _tail(xt, (dh_a, dh_b, dh_c, dz), gx1, w_in, small_g)
    return grad_x.reshape(B, SEQ, D_MODEL), g_in, g_mem, g_out, g_small


def _my_pos():
    return lax.axis_index("x"), lax.axis_index("y"), lax.axis_index("c")


def _gather_weights(shards):
    n_arr = len(shards)

    def body(*refs):
        ins, outs = refs[0:n_arr], refs[n_arr:2 * n_arr]
        send_sems, recv_sems, local_sems = refs[2 * n_arr:]
        x, y, c = _my_pos()
        me, sibling = (x, y, c), (x, y, 1 - c)
        chips = [(1 - x, y), (x, 1 - y), (1 - x, 1 - y)]

        def slot(a, pos):
            return outs[a].at[4 * pos[0] + 2 * pos[1] + pos[2]]

        def copy(a, k, block, to, src=None):
            return pltpu.make_async_remote_copy(
                src_ref=slot(a, block) if src is None else src, dst_ref=slot(a, block),
                send_sem=send_sems.at[a, k], recv_sem=recv_sems.at[a, k],
                device_id=to, device_id_type=MESH)

        mine = [pltpu.make_async_copy(ins[a], slot(a, me), local_sems.at[a]) for a in range(n_arr)]
        for cp in mine:
            cp.start()
        first = []
        for a in range(n_arr):
            first.append(copy(a, 0, me, sibling, src=ins[a]))
            first += [copy(a, 1 + j, me, (*chip, c), src=ins[a]) for j, chip in enumerate(chips)]
        for cp in first:
            cp.start()
        passed = []
        for j, chip in enumerate(chips):
            for a in range(n_arr):
                copy(a, 1 + j, (*chip, c), me).wait_recv()
                fwd = copy(a, 4 + j, (*chip, c), sibling)
                fwd.start()
                passed.append(fwd)
        for a in range(n_arr):
            copy(a, 0, sibling, me).wait_recv()
            for j, chip in enumerate(chips):
                copy(a, 4 + j, (*chip, 1 - c), me).wait_recv()
        for cp in first + passed:
            cp.wait_send()
        for cp in mine:
            cp.wait()

    any_spec = pl.BlockSpec(memory_space=pl.ANY)
    return pl.pallas_call(
        body, name="gather_weights",
        in_specs=[any_spec] * n_arr, out_specs=[any_spec] * n_arr,
        out_shape=[jax.ShapeDtypeStruct((N_DEV,) + s.shape, s.dtype) for s in shards],
        scratch_shapes=[pltpu.SemaphoreType.DMA((n_arr, 7)), pltpu.SemaphoreType.DMA((n_arr, 7)),
                        pltpu.SemaphoreType.DMA((n_arr,))],
    )(*shards)


def _adamw(w, g, m, v):
    m = ADAM_B1 * m + (1.0 - ADAM_B1) * g
    v = ADAM_B2 * v + (1.0 - ADAM_B2) * (g * g)
    m_hat = m / (1.0 - ADAM_B1 ** ADAM_STEP)
    v_hat = v / (1.0 - ADAM_B2 ** ADAM_STEP)
    delta = -ADAM_LR * (m_hat / (jnp.sqrt(v_hat) + ADAM_EPS) + ADAM_WD * w)
    return delta, m, v


_SMALL_SIZES = (("b_in", D_IN), ("g", D_MIX), ("gain", D_MODEL), ("bias", D_MODEL), ("sinks", SWA_Q), ("loss", 1))


def _pack_small(d):
    flat = jnp.concatenate([jnp.reshape(d[k], (-1,)).astype(F32) if k in d else jnp.zeros((n,), F32)
                            for k, n in _SMALL_SIZES])
    flat = jnp.pad(flat, (0, SMALL_ROWS * 128 - flat.shape[0]))
    return flat.reshape(SMALL_ROWS, 128)


def _unpack_small(p):
    flat = p.reshape(-1)
    out, off = {}, 0
    for k, n in _SMALL_SIZES:
        out[k] = flat[off:off + n].reshape(1, n)
        off += n
    return out


def kernel(x, mem, w_in, b_in, w_mem, attn_sinks, g_branch, w_out, ln_gain, ln_bias, loss_target, m_w_in, m_b_in, m_w_mem, m_attn_sinks, m_g_branch, m_w_out, m_ln_gain, m_ln_bias, v_w_in, v_b_in, v_w_mem, v_attn_sinks, v_g_branch, v_w_out, v_ln_gain, v_ln_bias):
    grad_x, g_in, g_mem, g_out, g_small = _step(
        x, mem, w_in[0].astype(MXU), w_mem[0].astype(MXU), w_out[0].astype(MXU), b_in, attn_sinks[0],
        g_branch, ln_gain, ln_bias, loss_target)

    small_w = _pack_small(dict(b_in=b_in, g=g_branch, gain=ln_gain, bias=ln_bias, sinks=attn_sinks))
    small_m = _pack_small(dict(b_in=m_b_in, g=m_g_branch, gain=m_ln_gain, bias=m_ln_bias, sinks=m_attn_sinks))
    small_v = _pack_small(dict(b_in=v_b_in, g=v_g_branch, gain=v_ln_gain, bias=v_ln_bias, sinks=v_attn_sinks))
    grads = [g_in, g_mem, g_out, g_small]
    params = [(w_in, m_w_in, v_w_in), (w_mem, m_w_mem, v_w_mem), (w_out, m_w_out, v_w_out),
              (small_w, small_m, small_v)]
    res, grad_x = _adam_update(grads, params, grad_x)
    big = res[:3]
    sm = [_unpack_small(r) for r in res[3]]

    def group(i):
        return (big[0][i], sm[i]["b_in"], big[1][i], sm[i]["sinks"], sm[i]["g"], big[2][i],
                sm[i]["gain"], sm[i]["bias"])

    loss = sm[0]["loss"].reshape(())
    return (loss, grad_x, *group(0), *group(1), *group(2), *group(3))
```

```python
import jax
import jax.numpy as jnp
from jax import lax
from jax.experimental import pallas as pl
from jax.experimental.pallas import tpu as pltpu

F32 = jnp.float32
MXU = jnp.bfloat16

D_MODEL = 1024
SEQ = 2048
HEAD = 64
BLK = 128
SWA_Q, SWA_KV = 8, 2
DIL_H = 4
MEM_H = 4
MEM_LEN = 256
W_A, W_KVA, W_B, W_C = 512, 128, 256, 256
D_MIX = 1024
D_IN = 2816
N_DEV = 8
COLS_PER_DEV = D_IN // N_DEV
ROWS_PER_DEV = D_MODEL // N_DEV
ROPE_THETA = 10000.0
LN_EPS = 1e-5
RMS_EPS = 1e-6
ALPHA = 2.0 ** 0.25
Q_SCALE = HEAD ** -0.5
NEG = -1e30
SMALL_ROWS = 48
VMEM_LIMIT = 56 * 1024 * 1024

ADAM_LR = 0.001
ADAM_B1 = 0.9
ADAM_B2 = 0.999
ADAM_EPS = 1e-08
ADAM_WD = 0.01
ADAM_STEP = 10

MESH = pl.DeviceIdType.MESH


def _cparams(sem=None):
    return pltpu.CompilerParams(dimension_semantics=sem, vmem_limit_bytes=VMEM_LIMIT)


def _dot(a, b):
    return jnp.dot(a, b, preferred_element_type=F32)


def _dot_nt(a, b):
    return lax.dot_general(a, b, (((1,), (1,)), ((), ())), preferred_element_type=F32)


def _dot_t0(a, b):
    return lax.dot_general(a, b, (((0,), (0,)), ((), ())), preferred_element_type=F32)


def _dot_tn(a, b):
    return jnp.dot(a.T.astype(MXU), b, preferred_element_type=F32)


def _rope(t, tab, sign):
    cos, sa, sb = tab
    outs = []
    for c in range(t.shape[1] // 128):
        tc = t[:, c * 128:(c + 1) * 128]
        r = pltpu.roll(tc, 96, 1) * sa + pltpu.roll(tc, 32, 1) * sb
        outs.append(tc * cos + r if sign > 0 else tc * cos - r)
    return outs[0] if len(outs) == 1 else jnp.concatenate(outs, axis=1)


def _rope_inv():
    inv = ROPE_THETA ** (-jnp.arange(0, HEAD, 2, dtype=F32) / HEAD)
    return jnp.tile(inv, 2 * 128 // HEAD)[None, :]


def _rope_tab(pos0, rows, inv):
    pos = (lax.broadcasted_iota(jnp.int32, (rows, 128), 0) + pos0).astype(F32)
    ang = pos * inv
    cos, sin = jnp.cos(ang), jnp.sin(ang)
    first = lax.broadcasted_iota(jnp.int32, (rows, 128), 1) % HEAD < HEAD // 2
    return cos, jnp.where(first, -sin, 0.0), jnp.where(first, 0.0, sin)


def _dev_coords(j):
    return (j >> 2, (j >> 1) & 1, j & 1)


def _in_proj(x2, w_all, b_in, rope_inv, late_shards):
    T = x2.shape[0]
    tm = 512
    n_late = len(late_shards)

    def body(x_ref, wall_ref, b_ref, inv_ref, *rest):
        late_in, rest = rest[:n_late], rest[n_late:]
        qkva_ref, qkvb_ref, qc_ref, z_ref, w_ref, xt_ref, tabo_ref = rest[:7]
        late_out = rest[7:7 + n_late]
        send_sems, recv_sems, local_sems = rest[7 + n_late:]
        step, last = pl.program_id(0), pl.num_programs(0) - 1
        x, y, c = _my_pos()
        me = 4 * x + 2 * y + c

        def to_peer(a, j):
            return pltpu.make_async_remote_copy(
                src_ref=late_in[a], dst_ref=late_out[a].at[me], send_sem=send_sems.at[a, j],
                recv_sem=recv_sems.at[a, me], device_id=_dev_coords(j), device_id_type=MESH)

        def from_peer(a, m):
            return pltpu.make_async_remote_copy(
                src_ref=late_out[a].at[m], dst_ref=late_out[a].at[m], send_sem=send_sems.at[a, m],
                recv_sem=recv_sems.at[a, m], device_id=_dev_coords(m), device_id_type=MESH)

        def mine(a):
            return pltpu.make_async_copy(late_in[a], late_out[a].at[me], local_sems.at[a])

        @pl.when(step == 0)
        def _():
            for a in range(n_late):
                mine(a).start()
                for j in range(N_DEV):
                    pl.when(me != j)(to_peer(a, j).start)
            for j in range(N_DEV):
                w_ref[:, j * COLS_PER_DEV:(j + 1) * COLS_PER_DEV] = wall_ref[j]

        xb = x_ref[...].astype(MXU)
        xt_ref[...] = x_ref[...].T.astype(MXU)
        tab = _rope_tab((step % nt) * tm, tm, inv_ref[...])
        for j in range(3):
            tabo_ref[:, j * 128:(j + 1) * 128] = tab[j]

        def seg(c0, c1):
            return _dot(xb, w_ref[:, c0:c1]) + b_ref[:, c0:c1]

        qa = (_rope(seg(0, 512), tab, 1) * Q_SCALE).astype(MXU)
        for c in range(SWA_Q // 2):
            qkva_ref[c] = qa[:, c * 128:(c + 1) * 128]
        lo = lax.broadcasted_iota(jnp.int32, (tm, 128), 1) < HEAD
        for j, t in enumerate((_rope(seg(512, 640), tab, 1), seg(640, 768))):
            other = pltpu.roll(t, HEAD, 1)
            qkva_ref[4 + 2 * j] = jnp.where(lo, t, other).astype(MXU)
            qkva_ref[5 + 2 * j] = jnp.where(lo, other, t).astype(MXU)
        qkvb = (_rope(seg(768, 1024), tab, 1) * Q_SCALE, _rope(seg(1024, 1280), tab, 1), seg(1280, 1536))
        for j, t in enumerate(qkvb):
            for c in range(2):
                qkvb_ref[2 * j + c] = t[:, c * 128:(c + 1) * 128]
        qc = (seg(1536, 1792) * Q_SCALE).astype(MXU)
        for c in range(MEM_H // 2):
            qc_ref[c] = qc[:, c * 128:(c + 1) * 128]
        z_ref[...] = seg(1792, 2816)

        @pl.when(step == last)
        def _():
            for a in range(n_late):
                mine(a).wait()
                for m in range(N_DEV):
                    pl.when(me != m)(from_peer(a, m).wait_recv)
                for j in range(N_DEV):
                    pl.when(me != j)(to_peer(a, j).wait_send)

    nt = SEQ // tm
    any_spec = pl.BlockSpec(memory_space=pl.ANY)
    chunked = lambda n: pl.BlockSpec((None, n, tm, 128), lambda i: (i // nt, 0, i % nt, 0))
    return pl.pallas_call(
        body, name="in_proj_fwd",
        grid=(T // tm,),
        in_specs=[pl.BlockSpec((tm, D_MODEL), lambda i: (i, 0)),
                  pl.BlockSpec((N_DEV, D_MODEL, COLS_PER_DEV), lambda i: (0, 0, 0)),
                  pl.BlockSpec((1, D_IN), lambda i: (0, 0)),
                  pl.BlockSpec((1, 128), lambda i: (0, 0))] + [any_spec] * n_late,
        out_specs=[chunked(SWA_CHUNKS), chunked(6), chunked(MEM_H // 2),
                   pl.BlockSpec((tm, D_MIX), lambda i: (i, 0)),
                   pl.BlockSpec((D_MODEL, D_IN), lambda i: (0, 0)),
                   pl.BlockSpec((D_MODEL, tm), lambda i: (0, i)),
                   pl.BlockSpec((tm, 384), lambda i: (i, 0))] + [any_spec] * n_late,
        out_shape=[jax.ShapeDtypeStruct((T // SEQ, SWA_CHUNKS, SEQ, 128), MXU),
                   jax.ShapeDtypeStruct((T // SEQ, 6, SEQ, 128), F32),
                   jax.ShapeDtypeStruct((T // SEQ, MEM_H // 2, SEQ, 128), MXU),
                   jax.ShapeDtypeStruct((T, D_MIX), F32),
                   jax.ShapeDtypeStruct((D_MODEL, D_IN), w_all.dtype),
                   jax.ShapeDtypeStruct((D_MODEL, T), MXU),
                   jax.ShapeDtypeStruct((T, 384), F32)]
        + [jax.ShapeDtypeStruct((N_DEV,) + s.shape, s.dtype) for s in late_shards],
        scratch_shapes=[pltpu.SemaphoreType.DMA((n_late, N_DEV)), pltpu.SemaphoreType.DMA((n_late, N_DEV)),
                        pltpu.SemaphoreType.DMA((n_late,))],
        compiler_params=_cparams(("arbitrary",)),
    )(x2, w_all, b_in, rope_inv, *late_shards)


CHAIN = 4


def _band_bias(max_dist):
    kj = lax.broadcasted_iota(jnp.int32, (2 * BLK, BLK), 0)
    qi = lax.broadcasted_iota(jnp.int32, (2 * BLK, BLK), 1)
    dist = qi + BLK - kj
    band = jnp.where((dist >= 0) & (dist <= max_dist), 0.0, NEG).astype(F32)
    k1 = lax.broadcasted_iota(jnp.int32, (BLK, BLK), 0)
    q1 = lax.broadcasted_iota(jnp.int32, (BLK, BLK), 1)
    first = jnp.where((q1 - k1 >= 0) & (q1 - k1 <= max_dist), 0.0, NEG).astype(F32)
    return jnp.concatenate([band] * CHAIN, axis=1), jnp.concatenate([first] * CHAIN, axis=1)


def _lanes(parts):
    return jnp.concatenate(parts, axis=1)


PICK_ROWS = 16


def _stack_pair(t):
    lo = (lax.broadcasted_iota(jnp.int32, t.shape, 1) < HEAD).astype(F32)
    return jnp.concatenate([t * lo, t * (1.0 - lo)], axis=0).astype(MXU)


def _pair_rows(x, n):
    lo = lax.broadcasted_iota(jnp.int32, (n, 128), 1) < HEAD
    return jnp.where(lo, x[0:n], x[n:2 * n])


def _split3(t):
    if MXU == F32:
        return (t,)
    hi = t.astype(MXU)
    r = t - hi.astype(F32)
    mid = r.astype(MXU)
    return hi, mid, (r - mid.astype(F32)).astype(MXU)


def _interleave(tiles):
    tiles = list(tiles)
    while tiles:
        for t in list(tiles):
            try:
                next(t)
            except StopIteration:
                tiles.remove(t)


def _softmax_cols(sT, sinkrow=None):
    m = jnp.max(sT, axis=0, keepdims=True)
    if sinkrow is not None:
        m = jnp.maximum(m, sinkrow)
    pT = jnp.exp(sT - m)
    l = jnp.sum(pT, axis=0, keepdims=True)
    if sinkrow is not None:
        l = l + jnp.exp(sinkrow - m)
    return (pT * (1.0 / l)).astype(MXU), m + jnp.log(l)


SWA_CHUNKS = 8
SWA_UNROLL = 3
N_QBLK = SEQ // BLK


def _swa_fwd(qkva, sinks):
    B = qkva.shape[0]
    G = SWA_Q // SWA_KV

    def body(sink_ref, qkv_ref, o_ref, lse_ref):
        band, first = _band_bias(BLK - 1)
        sinkrows = [_lanes([jnp.full((1, BLK), sink_ref[G * hk + j], F32) for j in range(G)])
                    for hk in range(SWA_KV)]

        def tile(hk, blk, rows_q, rows_k, bias):
            nk = bias.shape[0]
            k2 = _stack_pair(qkv_ref.at[4 + hk][rows_k, :])
            sT = []
            for c in (2 * hk, 2 * hk + 1):
                s2 = _dot_nt(k2, qkv_ref.at[c][rows_q, :])
                sT += [s2[0:nk], s2[nk:2 * nk]]
            yield
            pnT, lse = _softmax_cols(_lanes(sT) + bias, sinkrows[hk])
            yield
            v2 = _stack_pair(qkv_ref.at[6 + hk][rows_k, :])
            for j, c in enumerate((2 * hk, 2 * hk + 1)):
                p2 = jnp.concatenate([pnT[:, 2 * j * BLK:(2 * j + 1) * BLK],
                                      pnT[:, (2 * j + 1) * BLK:(2 * j + 2) * BLK]], axis=0)
                o_ref.at[c][rows_q, :] = _dot_t0(p2, v2)
            for j in range(G):
                lse_ref.at[blk][G * hk + j:G * hk + j + 1, :] = lse[:, j * BLK:(j + 1) * BLK]

        def tiles_at(i):
            r0 = pl.multiple_of(i * BLK, BLK)
            rk = pl.multiple_of(i * BLK - BLK, BLK)
            return [tile(hk, i, pl.ds(r0, BLK), pl.ds(rk, 2 * BLK), band) for hk in range(SWA_KV)]

        _interleave([tile(hk, 0, pl.ds(0, BLK), pl.ds(0, BLK), first) for hk in range(SWA_KV)])

        def loop(j, carry):
            _interleave([t for u in range(SWA_UNROLL) for t in tiles_at(1 + j * SWA_UNROLL + u)])
            return carry
        lax.fori_loop(0, (N_QBLK - 1) // SWA_UNROLL, loop, 0)

    return pl.pallas_call(
        body, name="swa_fwd", grid=(B,),
        in_specs=[pl.BlockSpec(memory_space=pltpu.SMEM),
                  pl.BlockSpec((None, SWA_CHUNKS, SEQ, 128), lambda b: (b, 0, 0, 0))],
        out_specs=[pl.BlockSpec((None, SWA_Q // 2, SEQ, 128), lambda b: (b, 0, 0, 0)),
                   pl.BlockSpec((None, N_QBLK, 8, 128), lambda b: (b, 0, 0, 0))],
        out_shape=[jax.ShapeDtypeStruct((B, SWA_Q // 2, SEQ, 128), F32),
                   jax.ShapeDtypeStruct((B, N_QBLK, 8, 128), F32)],
        compiler_params=_cparams(("arbitrary",)),
    )(sinks, qkva)


EP_ROWS = BLK


def _dh_tile(part_of, dh_ref, db_ref, tab_ref, blk, live=None):
    rs = pl.ds(pl.multiple_of(blk * EP_ROWS, EP_ROWS), EP_ROWS)
    tab = None if tab_ref is None else tuple(tab_ref[rs, j * 128:(j + 1) * 128] for j in range(3))
    part = part_of(rs, tab)
    yield
    dh_ref[rs, :] = part.astype(dh_ref.dtype)
    psum = jnp.sum(part, axis=0, keepdims=True)
    db_ref[0:1, :] += psum if live is None else psum * live
    yield


def _dh_tiles_behind(part_of, dh_ref, db_ref, tab_ref, j, unroll):
    live = (j > 0).astype(F32)
    return [_dh_tile(part_of, dh_ref, db_ref, tab_ref, jnp.where(j > 0, unroll * (j - 1) + u, 0), live)
            for u in range(unroll)]


def _swa_bwd(qkva, do, lse, delta, sinks, rope_tab):
    B = qkva.shape[0]
    G = SWA_Q // SWA_KV

    def body(sink_ref, qkv_ref, do_ref, lse_ref, delta_ref, rtab_ref, dh_ref, db_ref, dsink_ref, dq_ref):
        band, first = _band_bias(BLK - 1)
        sinkrows = [_lanes([jnp.full((1, BLK), sink_ref[G * hk + j], F32) for j in range(G)])
                    for hk in range(SWA_KV)]

        @pl.when(pl.program_id(0) == 0)
        def _():
            dsink_ref[...] = jnp.zeros_like(dsink_ref)
            db_ref[...] = jnp.zeros_like(db_ref)
        for c in range(4, SWA_CHUNKS):
            dq_ref[c] = jnp.zeros((SEQ, 128), F32)

        def tile(hk, blk, rows_q, rows_k, bias, accs):
            nk = bias.shape[0]
            k2 = _stack_pair(qkv_ref.at[4 + hk][rows_k, :])
            v2 = _stack_pair(qkv_ref.at[6 + hk][rows_k, :])
            qcs, docs, sT, dpT = [], [], [], []
            for c in (2 * hk, 2 * hk + 1):
                qc, doc = qkv_ref.at[c][rows_q, :], do_ref.at[c][rows_q, :]
                s2, dp2 = _dot_nt(k2, qc), _dot_nt(v2, doc)
                sT += [s2[0:nk], s2[nk:2 * nk]]
                dpT += [dp2[0:nk], dp2[nk:2 * nk]]
                qcs.append(qc)
                docs.append(doc)
            lse_r = _lanes([lse_ref.at[blk][h:h + 1, :] for h in range(G * hk, G * hk + G)])
            delta_r = _lanes([delta_ref.at[blk][h:h + 1, :] for h in range(G * hk, G * hk + G)])
            yield
            pT = jnp.exp(_lanes(sT) + bias - lse_r)
            dsT = pT * (_lanes(dpT) - delta_r)
            dsb, pb = dsT.astype(MXU), pT.astype(MXU)
            accs[hk] = accs[hk] - jnp.exp(sinkrows[hk] - lse_r) * delta_r
            yield
            dk2 = dv2 = None
            for j, c in enumerate((2 * hk, 2 * hk + 1)):
                q0, q1 = slice(2 * j * BLK, (2 * j + 1) * BLK), slice((2 * j + 1) * BLK, (2 * j + 2) * BLK)
                ds2 = jnp.concatenate([dsb[:, q0], dsb[:, q1]], axis=0)
                p2 = jnp.concatenate([pb[:, q0], pb[:, q1]], axis=0)
                dq_ref.at[c][rows_q, :] = _dot_t0(ds2, k2)
                dk2 = _dot(ds2, qcs[j]) if dk2 is None else dk2 + _dot(ds2, qcs[j])
                dv2 = _dot(p2, docs[j]) if dv2 is None else dv2 + _dot(p2, docs[j])
            dq_ref.at[4 + hk][rows_k, :] += _pair_rows(dk2, nk)
            dq_ref.at[6 + hk][rows_k, :] += _pair_rows(dv2, nk)

        def run(tiles_of, accs):
            accs = list(accs)
            _interleave(tiles_of(accs))
            return tuple(accs)

        zero = jnp.zeros((1, G * BLK), F32)
        accs = run(lambda a: [tile(hk, 0, pl.ds(0, BLK), pl.ds(0, BLK), first, a) for hk in range(SWA_KV)],
                   (zero,) * SWA_KV)

        def part_of(rs, tab):
            lo = lax.broadcasted_iota(jnp.int32, (EP_ROWS, 128), 1) < HEAD

            def kv_grad(c):
                g0, g1 = dq_ref.at[c][rs, :], dq_ref.at[c + 1][rs, :]
                return jnp.where(lo, g0 + pltpu.roll(g0, HEAD, 1), g1 + pltpu.roll(g1, HEAD, 1))
            dq = _lanes([dq_ref.at[c][rs, :] for c in range(SWA_Q // 2)])
            return _lanes([_rope(dq, tab, -1) * Q_SCALE, _rope(kv_grad(4), tab, -1), kv_grad(6)])

        trips = (N_QBLK - 1) // SWA_UNROLL

        def loop(j, accs):
            def tiles_of(a):
                out = []
                for u in range(SWA_UNROLL):
                    i = 1 + j * SWA_UNROLL + u
                    r0 = pl.multiple_of(i * BLK, BLK)
                    rk = pl.multiple_of(i * BLK - BLK, BLK)
                    out += [tile(hk, i, pl.ds(r0, BLK), pl.ds(rk, 2 * BLK), band, a) for hk in range(SWA_KV)]
                return out + _dh_tiles_behind(part_of, dh_ref, db_ref, rtab_ref, j, SWA_UNROLL)
            return run(tiles_of, accs)
        accs = lax.fori_loop(0, trips, loop, accs)
        _interleave([_dh_tile(part_of, dh_ref, db_ref, rtab_ref, blk)
                     for blk in range(SWA_UNROLL * (trips - 1), N_QBLK)])
        for hk in range(SWA_KV):
            for j in range(G):
                tot = jnp.sum(accs[hk][:, j * BLK:(j + 1) * BLK], axis=1, keepdims=True)
                dsink_ref[G * hk + j:G * hk + j + 1, :] += jnp.broadcast_to(tot, (1, 128))

    stat = pl.BlockSpec((None, N_QBLK, 8, 128), lambda b: (b, 0, 0, 0))
    return pl.pallas_call(
        body, name="swa_bwd", grid=(B,),
        in_specs=[pl.BlockSpec(memory_space=pltpu.SMEM),
                  pl.BlockSpec((None, SWA_CHUNKS, SEQ, 128), lambda b: (b, 0, 0, 0)),
                  pl.BlockSpec((None, SWA_Q // 2, SEQ, 128), lambda b: (b, 0, 0, 0)), stat, stat,
                  pl.BlockSpec((SEQ, 384), lambda b: (0, 0))],
        out_specs=[pl.BlockSpec((SEQ, W_A + 2 * W_KVA), lambda b: (b, 0)),
                   pl.BlockSpec((8, W_A + 2 * W_KVA), lambda b: (0, 0)),
                   pl.BlockSpec((8, 128), lambda b: (0, 0))],
        out_shape=[jax.ShapeDtypeStruct((B * SEQ, W_A + 2 * W_KVA), MXU),
                   jax.ShapeDtypeStruct((8, W_A + 2 * W_KVA), F32), jax.ShapeDtypeStruct((8, 128), F32)],
        scratch_shapes=[pltpu.VMEM((SWA_CHUNKS, SEQ, 128), F32)],
        compiler_params=_cparams(("arbitrary",)),
    )(sinks, qkva, do, lse, delta, rope_tab)


DILATIONS = (1, 4, 16)
DIL_PAIRS_H = DIL_H // 2


def _stream_rows(d, r, i, n):
    if d == 1:
        return pl.ds(pl.multiple_of(i * BLK, BLK), n)
    return pl.ds(r + i * (BLK * d), n, stride=d)


def _spread_matrix():
    row = lax.broadcasted_iota(jnp.int32, (PICK_ROWS, 128), 0)
    lane = lax.broadcasted_iota(jnp.int32, (PICK_ROWS, 128), 1)
    return ((row < 6) & ((row % 2 == 1) == (lane >= HEAD))).astype(MXU)


def _lanes_to_tokens(v0, v1, spread):
    n = v0.shape[1]
    row = lax.broadcasted_iota(jnp.int32, (PICK_ROWS, n), 0)
    a = jnp.zeros((PICK_ROWS, n), F32)
    for i, (p0, p1) in enumerate(zip(_split3(v0), _split3(v1))):
        a = jnp.where(row == 2 * i, p0.astype(F32), a)
        a = jnp.where(row == 2 * i + 1, p1.astype(F32), a)
    return _dot_t0(a.astype(MXU), spread)


def _tokens_to_lanes(t):
    r = t.T
    return r[0:1, :], r[HEAD:HEAD + 1, :]


DIL_UNROLL = 3


def _dil_schedule(body_first, body_next, behind=None):
    for p, d in sorted(enumerate(DILATIONS), key=lambda pd: -pd[1]):
        nblk = SEQ // d // BLK
        if d == 1:
            _interleave([body_first(p, d, 0)])
            def loop(j, c, p=p, d=d):
                _interleave([body_next(p, d, 0, 1 + DIL_UNROLL * j + u) for u in range(DIL_UNROLL)]
                            + (behind(j) if behind else []))
                return c
            lax.fori_loop(0, (nblk - 1) // DIL_UNROLL, loop, 0)
        elif nblk > 1:
            def loop(r, c, p=p, d=d, nblk=nblk):
                _interleave([body_first(p, d, r)] + [body_next(p, d, r, i) for i in range(1, nblk)])
                return c
            lax.fori_loop(0, d, loop, 0)
        else:
            def loop(j, c, p=p, d=d):
                _interleave([body_first(p, d, 4 * j + u) for u in range(4)])
                return c
            lax.fori_loop(0, d // 4, loop, 0)


def _dil_fwd(qkvb):
    B = qkvb.shape[0]

    def body(qkv_ref, o_ref):
        band, first = _band_bias(BLK)
        spread = _spread_matrix()

        def block(p, d, rows_q, rows_k, bias):
            nk = bias.shape[0]
            sT = []
            for c in range(DIL_PAIRS_H):
                qc = qkv_ref.at[c][rows_q, :].astype(MXU)
                s2 = _dot_nt(_stack_pair(qkv_ref.at[DIL_PAIRS_H + c][rows_k, :]), qc)
                sT += [s2[0:nk], s2[nk:2 * nk]]
            yield
            sT = _lanes(sT) + bias
            m = jnp.max(sT, axis=0, keepdims=True)
            pT = jnp.exp(sT - m)
            l = jnp.sum(pT, axis=0, keepdims=True)
            pnT = (pT * (1.0 / l)).astype(MXU)
            lse = m + jnp.log(l)
            yield
            for c in range(DIL_PAIRS_H):
                q0, q1 = slice(2 * c * BLK, (2 * c + 1) * BLK), slice((2 * c + 1) * BLK, (2 * c + 2) * BLK)
                p2 = jnp.concatenate([pnT[:, q0], pnT[:, q1]], axis=0)
                o_ref.at[p, c][rows_q, :] = _dot_t0(p2, _stack_pair(qkv_ref.at[2 * DIL_PAIRS_H + c][rows_k, :]))
                o_ref.at[p, DIL_PAIRS_H + c][rows_q, :] = _lanes_to_tokens(lse[:, q0], lse[:, q1], spread)

        def body_first(p, d, r):
            rows = _stream_rows(d, r, 0, BLK)
            return block(p, d, rows, rows, first)

        def body_next(p, d, r, i):
            return block(p, d, _stream_rows(d, r, i, BLK), _stream_rows(d, r, i - 1, 2 * BLK), band)

        _dil_schedule(body_first, body_next)

    return pl.pallas_call(
        body, name="dil_fwd", grid=(B,),
        in_specs=[pl.BlockSpec((None, 6, SEQ, 128), lambda b: (b, 0, 0, 0))],
        out_specs=pl.BlockSpec((None, 3, 4, SEQ, 128), lambda b: (b, 0, 0, 0, 0)),
        out_shape=jax.ShapeDtypeStruct((B, 3, 4, SEQ, 128), F32),
        compiler_params=_cparams(("arbitrary",)),
    )(qkvb)


def _reduce_scatter_ops(send_refs, land_refs, send_sems, recv_sems):
    x, y, c = _my_pos()
    me = 4 * x + 2 * y + c
    n = len(send_refs)

    def to_peer(a, j):
        return pltpu.make_async_remote_copy(
            src_ref=send_refs[a].at[j], dst_ref=land_refs[a].at[me], send_sem=send_sems.at[a, j],
            recv_sem=recv_sems.at[a, me], device_id=_dev_coords(j), device_id_type=MESH)

    def from_peer(a, m):
        return pltpu.make_async_remote_copy(
            src_ref=land_refs[a].at[m], dst_ref=land_refs[a].at[m], send_sem=send_sems.at[a, m],
            recv_sem=recv_sems.at[a, m], device_id=_dev_coords(m), device_id_type=MESH)

    def start():
        for j in range(N_DEV):
            @pl.when(me != j)
            def _(j=j):
                for a in range(n):
                    to_peer(a, j).start()
        for a in range(n):
            land_refs[a][me] = jnp.zeros(land_refs[a].shape[1:], land_refs[a].dtype)

    def finish(own_refs, out_refs):
        for m in range(N_DEV):
            @pl.when(me != m)
            def _(m=m):
                for a in range(n):
                    from_peer(a, m).wait_recv()
        for j in range(N_DEV):
            @pl.when(me != j)
            def _(j=j):
                for a in range(n):
                    to_peer(a, j).wait_send()
        for a in range(n):
            def chunk(i, carry, a=a):
                rs = pl.ds(pl.multiple_of(i * REDUCE_ROWS, REDUCE_ROWS), REDUCE_ROWS)
                g = own_refs[a][rs, :]
                for m in range(N_DEV):
                    g = g + land_refs[a][m, rs, :].astype(F32)
                out_refs[a][rs, :] = g
                return carry
            lax.fori_loop(0, own_refs[a].shape[0] // REDUCE_ROWS, chunk, 0)

    return start, finish


def _dil_bwd(qkvb, dobb, rope_tab, sends, owns):
    B = qkvb.shape[0]
    n_rs = len(sends)

    def body(qkv_ref, dob_ref, rtab_ref, *rest):
        send_refs, own_refs = rest[:n_rs], rest[n_rs:2 * n_rs]
        dh_ref, db_ref = rest[2 * n_rs:2 * n_rs + 2]
        out_refs = rest[2 * n_rs + 2:3 * n_rs + 2]
        dq_ref = rest[3 * n_rs + 2]
        land_refs = rest[3 * n_rs + 3:4 * n_rs + 3]
        send_sems, recv_sems = rest[4 * n_rs + 3:]
        rs_start, rs_finish = _reduce_scatter_ops(send_refs, land_refs, send_sems, recv_sems)
        pl.when(pl.program_id(0) == 0)(rs_start)

        band, first = _band_bias(BLK)
        dq_ref[...] = jnp.zeros_like(dq_ref)

        @pl.when(pl.program_id(0) == 0)
        def _():
            db_ref[...] = jnp.zeros_like(db_ref)

        def block(p, d, rows_q, rows_k, bias):
            nk = bias.shape[0]
            lo = lax.broadcasted_iota(jnp.int32, (nk, 128), 1) < HEAD
            qcs, docs, k2s, sT, dpT, lse, delta = [], [], [], [], [], [], []
            for c in range(DIL_PAIRS_H):
                qc = qkv_ref.at[c][rows_q, :].astype(MXU)
                doc = dob_ref.at[c][rows_q, :].astype(MXU)
                k2 = _stack_pair(qkv_ref.at[DIL_PAIRS_H + c][rows_k, :])
                s2 = _dot_nt(k2, qc)
                dp2 = _dot_nt(_stack_pair(qkv_ref.at[2 * DIL_PAIRS_H + c][rows_k, :]), doc)
                sT += [s2[0:nk], s2[nk:2 * nk]]
                dpT += [dp2[0:nk], dp2[nk:2 * nk]]
                lse += _tokens_to_lanes(dob_ref.at[DIL_PAIRS_H + c][rows_q, :])
                delta += _tokens_to_lanes(dob_ref.at[2 * DIL_PAIRS_H + c][rows_q, :])
                qcs.append(qc)
                docs.append(doc)
                k2s.append(k2)
            yield
            pT = jnp.exp(_lanes(sT) + bias - _lanes(lse))
            dsT = pT * (_lanes(dpT) - _lanes(delta))
            dsb, pb = dsT.astype(MXU), pT.astype(MXU)
            yield
            for c in range(DIL_PAIRS_H):
                q0, q1 = slice(2 * c * BLK, (2 * c + 1) * BLK), slice((2 * c + 1) * BLK, (2 * c + 2) * BLK)
                ds2 = jnp.concatenate([dsb[:, q0], dsb[:, q1]], axis=0)
                p2 = jnp.concatenate([pb[:, q0], pb[:, q1]], axis=0)
                dq_ref.at[c][rows_q, :] += _dot_t0(ds2, k2s[c])
                dk2, dv2 = _dot(ds2, qcs[c]), _dot(p2, docs[c])
                dq_ref.at[DIL_PAIRS_H + c][rows_k, :] += jnp.where(lo, dk2[0:nk], dk2[nk:2 * nk])
                dq_ref.at[2 * DIL_PAIRS_H + c][rows_k, :] += jnp.where(lo, dv2[0:nk], dv2[nk:2 * nk])

        def body_first(p, d, r):
            rows = _stream_rows(d, r, 0, BLK)
            return block(p, d, rows, rows, first)

        def body_next(p, d, r, i):
            return block(p, d, _stream_rows(d, r, i, BLK), _stream_rows(d, r, i - 1, 2 * BLK), band)

        def part_of(rs, tab):
            q, k, v = [_lanes([dq_ref.at[2 * j][rs, :], dq_ref.at[2 * j + 1][rs, :]]) for j in range(3)]
            return _lanes([_rope(q, tab, -1) * Q_SCALE, _rope(k, tab, -1), v])

        _dil_schedule(body_first, body_next,
                      lambda j: _dh_tiles_behind(part_of, dh_ref, db_ref, rtab_ref, j, DIL_UNROLL))
        _interleave([_dh_tile(part_of, dh_ref, db_ref, rtab_ref, blk)
                     for blk in range(DIL_UNROLL * ((N_QBLK - 1) // DIL_UNROLL - 1), N_QBLK)])

        @pl.when(pl.program_id(0) == pl.num_programs(0) - 1)
        def _():
            rs_finish(own_refs, out_refs)

    spec = pl.BlockSpec((None, 6, SEQ, 128), lambda b: (b, 0, 0, 0))
    any_spec = pl.BlockSpec(memory_space=pl.ANY)
    vmem = pl.BlockSpec(memory_space=pltpu.VMEM)
    outs = pl.pallas_call(
        body, name="dil_bwd", grid=(B,),
        in_specs=[spec, spec, pl.BlockSpec((SEQ, 384), lambda b: (0, 0))] + [any_spec] * n_rs + [vmem] * n_rs,
        out_specs=[pl.BlockSpec((SEQ, 3 * W_B), lambda b: (b, 0)), pl.BlockSpec((8, 3 * W_B), lambda b: (0, 0))]
        + [vmem] * n_rs,
        out_shape=[jax.ShapeDtypeStruct((B * SEQ, 3 * W_B), MXU), jax.ShapeDtypeStruct((8, 3 * W_B), F32)]
        + [jax.ShapeDtypeStruct(o.shape, F32) for o in owns],
        scratch_shapes=[pltpu.VMEM((6, SEQ, 128), F32)] + [pltpu.VMEM(s.shape, s.dtype) for s in sends]
        + [pltpu.SemaphoreType.DMA((n_rs, N_DEV)), pltpu.SemaphoreType.DMA((n_rs, N_DEV))],
        compiler_params=_cparams(("arbitrary",)),
    )(qkvb, dobb, rope_tab, *sends, *owns)
    return outs[0], outs[1], outs[2:]


MEM_UNROLL = 4
MEM_PAIRS = MEM_H // 2


def _mem_attn_fwd(qc, mem, w_mem):
    B = qc.shape[0]

    def body(q_ref, mem_ref, w_ref, o_ref, lse_ref, mkv_ref, k2_ref, v2_ref):
        mkv = _dot(mem_ref[...].astype(MXU), w_ref[...])
        mkv_ref[...] = mkv.astype(MXU)
        for c in range(MEM_PAIRS):
            k2_ref[c] = _stack_pair(mkv[:, c * 128:(c + 1) * 128])
            v2_ref[c] = _stack_pair(mkv[:, W_C + c * 128:W_C + (c + 1) * 128])
        lse_ref[...] = jnp.zeros_like(lse_ref)

        def tile(blk):
            rows = pl.ds(pl.multiple_of(blk * BLK, BLK), BLK)
            sT = []
            for c in range(MEM_PAIRS):
                s2 = _dot_nt(k2_ref[c], q_ref.at[c][rows, :])
                sT += [s2[0:MEM_LEN], s2[MEM_LEN:2 * MEM_LEN]]
            yield
            pnT, lse = _softmax_cols(_lanes(sT))
            yield
            for c in range(MEM_PAIRS):
                p2 = jnp.concatenate([pnT[:, 2 * c * BLK:(2 * c + 1) * BLK],
                                      pnT[:, (2 * c + 1) * BLK:(2 * c + 2) * BLK]], axis=0)
                o_ref.at[c][rows, :] = _dot_t0(p2, v2_ref[c])
            for h in range(MEM_H):
                lse_ref.at[blk][h:h + 1, :] = lse[:, h * BLK:(h + 1) * BLK]

        def loop(j, carry):
            _interleave([tile(j * MEM_UNROLL + u) for u in range(MEM_UNROLL)])
            return carry
        lax.fori_loop(0, N_QBLK // MEM_UNROLL, loop, 0)

    return pl.pallas_call(
        body, name="mem_attn_fwd", grid=(B,),
        in_specs=[pl.BlockSpec((None, MEM_PAIRS, SEQ, 128), lambda b: (b, 0, 0, 0)),
                  pl.BlockSpec((None, MEM_LEN, D_MODEL), lambda b: (b, 0, 0)),
                  pl.BlockSpec((D_MODEL, 2 * W_C), lambda b: (0, 0))],
        out_specs=[pl.BlockSpec((None, MEM_PAIRS, SEQ, 128), lambda b: (b, 0, 0, 0)),
                   pl.BlockSpec((None, N_QBLK, 8, 128), lambda b: (b, 0, 0, 0)),
                   pl.BlockSpec((None, MEM_LEN, 2 * W_C), lambda b: (b, 0, 0))],
        out_shape=[jax.ShapeDtypeStruct((B, MEM_PAIRS, SEQ, 128), F32),
                   jax.ShapeDtypeStruct((B, N_QBLK, 8, 128), F32),
                   jax.ShapeDtypeStruct((B, MEM_LEN, 2 * W_C), MXU)],
        scratch_shapes=[pltpu.VMEM((MEM_PAIRS, 2 * MEM_LEN, 128), MXU), pltpu.VMEM((MEM_PAIRS, 2 * MEM_LEN, 128), MXU)],
        compiler_params=_cparams(("arbitrary",)),
    )(qc, mem, w_mem)


def _mem_attn_bwd(qc, mkv, do, lse, delta, mem):
    B = qc.shape[0]

    def body(q_ref, mkv_ref, do_ref, lse_ref, delta_ref, mem_ref, dh_ref, db_ref, dw_ref,
             dq_ref, dmkv_ref, k2_ref, v2_ref):
        @pl.when(pl.program_id(0) == 0)
        def _():
            dw_ref[...] = jnp.zeros_like(dw_ref)
            db_ref[...] = jnp.zeros_like(db_ref)
        dmkv_ref[...] = jnp.zeros_like(dmkv_ref)
        dq_ref[...] = jnp.zeros_like(dq_ref)
        for c in range(MEM_PAIRS):
            k2_ref[c] = _stack_pair(mkv_ref[:, c * 128:(c + 1) * 128])
            v2_ref[c] = _stack_pair(mkv_ref[:, W_C + c * 128:W_C + (c + 1) * 128])

        def tile(blk):
            rows = pl.ds(pl.multiple_of(blk * BLK, BLK), BLK)
            qcs, docs, sT, dpT = [], [], [], []
            for c in range(MEM_PAIRS):
                qc_, doc = q_ref.at[c][rows, :], do_ref.at[c][rows, :]
                s2, dp2 = _dot_nt(k2_ref[c], qc_), _dot_nt(v2_ref[c], doc)
                sT += [s2[0:MEM_LEN], s2[MEM_LEN:2 * MEM_LEN]]
                dpT += [dp2[0:MEM_LEN], dp2[MEM_LEN:2 * MEM_LEN]]
                qcs.append(qc_)
                docs.append(doc)
            lse_r = _lanes([lse_ref.at[blk][h:h + 1, :] for h in range(MEM_H)])
            delta_r = _lanes([delta_ref.at[blk][h:h + 1, :] for h in range(MEM_H)])
            yield
            pT = jnp.exp(_lanes(sT) - lse_r)
            dsT = pT * (_lanes(dpT) - delta_r)
            dsb, pb = dsT.astype(MXU), pT.astype(MXU)
            yield
            for c in range(MEM_PAIRS):
                q0, q1 = slice(2 * c * BLK, (2 * c + 1) * BLK), slice((2 * c + 1) * BLK, (2 * c + 2) * BLK)
                ds2 = jnp.concatenate([dsb[:, q0], dsb[:, q1]], axis=0)
                p2 = jnp.concatenate([pb[:, q0], pb[:, q1]], axis=0)
                dq_ref.at[c][rows, :] = _dot_t0(ds2, k2_ref[c])
                dmkv_ref[:, c * 128:(c + 1) * 128] += _pair_rows(_dot(ds2, qcs[c]), MEM_LEN)
                dmkv_ref[:, W_C + c * 128:W_C + (c + 1) * 128] += _pair_rows(_dot(p2, docs[c]), MEM_LEN)

        def part_of(rs, tab):
            return _lanes([dq_ref.at[c][rs, :] for c in range(MEM_PAIRS)]) * Q_SCALE

        trips = N_QBLK // MEM_UNROLL

        def loop(j, carry):
            _interleave([tile(j * MEM_UNROLL + u) for u in range(MEM_UNROLL)]
                        + _dh_tiles_behind(part_of, dh_ref, db_ref, None, j, MEM_UNROLL))
            return carry
        lax.fori_loop(0, trips, loop, 0)
        _interleave([_dh_tile(part_of, dh_ref, db_ref, None, blk) for blk in range(MEM_UNROLL * (trips - 1), N_QBLK)])
        dw_ref[...] += _dot_tn(mem_ref[...], dmkv_ref[...].astype(MXU))

    stat = pl.BlockSpec((None, N_QBLK, 8, 128), lambda b: (b, 0, 0, 0))
    pairs = pl.BlockSpec((None, MEM_PAIRS, SEQ, 128), lambda b: (b, 0, 0, 0))
    return pl.pallas_call(
        body, name="mem_attn_bwd", grid=(B,),
        in_specs=[pairs, pl.BlockSpec((None, MEM_LEN, 2 * W_C), lambda b: (b, 0, 0)), pairs, stat, stat,
                  pl.BlockSpec((None, MEM_LEN, D_MODEL), lambda b: (b, 0, 0))],
        out_specs=[pl.BlockSpec((SEQ, W_C), lambda b: (b, 0)), pl.BlockSpec((8, W_C), lambda b: (0, 0)),
                   pl.BlockSpec((D_MODEL, 2 * W_C), lambda b: (0, 0))],
        out_shape=[jax.ShapeDtypeStruct((B * SEQ, W_C), MXU), jax.ShapeDtypeStruct((8, W_C), F32),
                   jax.ShapeDtypeStruct((D_MODEL, 2 * W_C), F32)],
        scratch_shapes=[pltpu.VMEM((MEM_PAIRS, SEQ, 128), F32), pltpu.VMEM((MEM_LEN, 2 * W_C), F32),
                        pltpu.VMEM((MEM_PAIRS, 2 * MEM_LEN, 128), MXU), pltpu.VMEM((MEM_PAIRS, 2 * MEM_LEN, 128), MXU)],
        compiler_params=_cparams(("arbitrary",)),
    )(qc, mkv, do, lse, delta, mem)


def _headsum(t, e):
    if MXU == F32:
        return _dot(t, e)
    hi = t.astype(MXU)
    lo = (t - hi.astype(F32)).astype(MXU)
    return _dot(hi, e) + _dot(lo, e)


def _heads_to_rows(t, e):
    return sum(_dot_nt(e, part) for part in _split3(t))


POST_TM = 256
POST_ROWS = 256


def _post(o_a, olse_b, o_c, z, x2, tgt, g, gain, bias, w_out, hsum, hrows):
    T = x2.shape[0]
    tm = POST_TM
    nt = SEQ // tm

    def body(oa_ref, ob_ref, oc_ref, z_ref, x_ref, t_ref, g_ref, gain_ref, bias_ref, w_ref, e_ref, er_ref,
             gx_ref, doa_ref, dela_ref, dobb_ref, doc_ref, delc_ref, dz_ref, dw_ref, small_ref, loss_ref):
        @pl.when(pl.program_id(0) == 0)
        def _():
            dw_ref[...] = jnp.zeros_like(dw_ref)
            small_ref[...] = jnp.zeros_like(small_ref)
            loss_ref[...] = jnp.zeros_like(loss_ref)

        gg = g_ref[...]
        gain_v = gain_ref[...]
        gain_s = gain_v * (1.0 / D_MODEL)
        bias_v = bias_ref[...]
        w = w_ref[...]

        def rms(o):
            rr = lax.rsqrt(jnp.mean(o * o, axis=1, keepdims=True) + RMS_EPS)
            return o * rr, rr

        def rows_of(rs, results):
            oa = _lanes([oa_ref.at[c][rs, :] for c in range(SWA_Q // 2)])
            (o1, l1), (o4, l4), (o16, l16) = [
                (_lanes([ob_ref.at[p, 0][rs, :], ob_ref.at[p, 1][rs, :]]),
                 _lanes([ob_ref.at[p, 2][rs, :], ob_ref.at[p, 3][rs, :]])) for p in range(3)]
            mx = jnp.maximum(jnp.maximum(l1, l4), l16)
            e1, e4, e16 = jnp.exp(l1 - mx), jnp.exp(l4 - mx), jnp.exp(l16 - mx)
            den = e1 + e4 + e16
            ob = (e1 * o1 + e4 * o4 + e16 * o16) / den
            lse_b = mx + jnp.log(den)
            oc = _lanes([oc_ref.at[c][rs, :] for c in range(MEM_PAIRS)])
            na, ra = rms(oa)
            nb, rb = rms(ob)
            nc, rc = rms(oc)
            n = jnp.concatenate([na, nb, nc], axis=1)
            zz = z_ref[rs, :]
            sig = 0.5 * jnp.tanh(0.5 * zz) + 0.5
            sz = zz * sig
            gs = gg * sz
            u = n * gs
            yo = _dot(u.astype(MXU), w)
            yield
            r = ALPHA * x_ref[rs, :] + yo
            rc0 = r - jnp.mean(r, axis=1, keepdims=True)
            rstd = lax.rsqrt(jnp.mean(rc0 * rc0, axis=1, keepdims=True) + LN_EPS)
            xhat = rc0 * rstd
            err = xhat * gain_v + bias_v - t_ref[rs, :]
            dxh = err * gain_s
            dr = rstd * (dxh - jnp.mean(dxh, axis=1, keepdims=True)
                         - xhat * jnp.mean(dxh * xhat, axis=1, keepdims=True))
            gx_ref[rs, :] = ALPHA * dr
            drb = dr.astype(MXU)
            du = _dot_nt(drb, w)
            yield
            dun = du * n
            dz = dun * (gg * (sig + sz * (1.0 - sig)))
            dz_ref[rs, :] = dz.astype(MXU)
            dn = du * gs

            def branch(lo, hi, nbr, rr):
                dnb = dn[:, lo:hi]
                return rr * (dnb - nbr * jnp.mean(dnb * nbr, axis=1, keepdims=True))

            def to_kernel(dob, o, do_ref, delta_ref):
                wd = dob.shape[1]
                for c in range(wd // 128):
                    do_ref.at[c][rs, :] = dob[:, c * 128:(c + 1) * 128].astype(do_ref.dtype)
                dT = _heads_to_rows(dob * o, er_ref[:, 0:wd])
                for jb in range((rs.stop - rs.start) // BLK):
                    delta_ref[rs.start // BLK + jb] = dT[0:8, jb * BLK:(jb + 1) * BLK]

            to_kernel(branch(0, W_A, na, ra), oa, doa_ref, dela_ref)
            to_kernel(branch(W_A + W_B, D_MIX, nc, rc), oc, doc_ref, delc_ref)
            dob = branch(W_A, W_A + W_B, nb, rb)
            for j, t in enumerate((dob, lse_b, _headsum(dob * ob, e_ref[...]))):
                for c in range(W_B // 128):
                    dobb_ref.at[j * (W_B // 128) + c][rs, :] = t[:, c * 128:(c + 1) * 128]
            csum = lambda t: jnp.sum(t, axis=0, keepdims=True)
            results.append((u, drb, jnp.sum(err * err), csum(err * xhat), csum(err), csum(dun * sz), csum(dz)))

        parts = []
        _interleave([rows_of(slice(k * POST_ROWS, (k + 1) * POST_ROWS), parts) for k in range(tm // POST_ROWS)])
        tot = [sum(p[i] for p in parts) for i in range(2, 7)]
        dw_ref[...] += _dot_tn(jnp.concatenate([p[0] for p in parts], axis=0),
                               jnp.concatenate([p[1] for p in parts], axis=0))
        loss_ref[...] += 0.5 * tot[0] * (1.0 / D_MODEL)
        small_ref[0:1, :] += tot[1] * (1.0 / D_MODEL)
        small_ref[1:2, :] += tot[2] * (1.0 / D_MODEL)
        small_ref[2:3, :] += tot[3]
        small_ref[3:4, :] += tot[4]

    B = T // SEQ
    row = lambda w: pl.BlockSpec((tm, w), lambda i: (i, 0))
    full = lambda a, b: pl.BlockSpec((a, b), lambda i: (0, 0))
    chunked = lambda n: pl.BlockSpec((None, n, tm, 128), lambda i: (i // nt, 0, i % nt, 0))
    stat = pl.BlockSpec((None, tm // BLK, 8, 128), lambda i: (i // nt, i % nt, 0, 0))
    return pl.pallas_call(
        body, name="post_fwd_bwd", grid=(T // tm,),
        in_specs=[chunked(SWA_Q // 2), pl.BlockSpec((None, 3, 4, tm, 128), lambda i: (i // nt, 0, 0, i % nt, 0)),
                  chunked(MEM_PAIRS),
                  row(D_MIX), row(D_MODEL), row(D_MODEL),
                  full(1, D_MIX), full(1, D_MODEL), full(1, D_MODEL), full(D_MIX, D_MODEL), full(W_B, W_B),
                  full(PICK_ROWS, W_A)],
        out_specs=[row(D_MODEL), chunked(SWA_Q // 2), stat, chunked(6), chunked(MEM_PAIRS), stat, row(D_MIX),
                   full(D_MIX, D_MODEL), full(8, D_MODEL), full(8, 128)],
        out_shape=[jax.ShapeDtypeStruct((T, D_MODEL), F32),
                   jax.ShapeDtypeStruct((B, SWA_Q // 2, SEQ, 128), MXU),
                   jax.ShapeDtypeStruct((B, N_QBLK, 8, 128), F32),
                   jax.ShapeDtypeStruct((B, 6, SEQ, 128), F32),
                   jax.ShapeDtypeStruct((B, MEM_PAIRS, SEQ, 128), MXU),
                   jax.ShapeDtypeStruct((B, N_QBLK, 8, 128), F32),
                   jax.ShapeDtypeStruct((T, D_MIX), MXU),
                   jax.ShapeDtypeStruct((D_MIX, D_MODEL), F32),
                   jax.ShapeDtypeStruct((8, D_MODEL), F32),
                   jax.ShapeDtypeStruct((8, 128), F32)],
        compiler_params=_cparams(("arbitrary",)),
    )(o_a, olse_b, o_c, z, x2, tgt, g, gain, bias, w_out, hsum, hrows)


TAIL_TK = 512
TAIL_TN = D_IN // 2
TAIL_TM = 256
REDUCE_ROWS = 128


DH_SPLITS = (0, W_A + 2 * W_KVA, W_A + 2 * W_KVA + 3 * W_B, D_IN - D_MIX, D_IN)


def _tail(xt, dhs, gx1, w_in, small_g):
    T = xt.shape[1]
    dh = dhs[0]
    c0, c1, c2, c3, c4 = DH_SPLITS
    assert c1 < TAIL_TN < c2 and (TAIL_TN - c1) % 128 == 0
    kt = T // TAIL_TK
    ndw = (D_IN // TAIL_TN) * kt
    nsteps = ndw + T // TAIL_TM
    n_pass = D_IN // TAIL_TN
    assert n_pass == 2 and TAIL_TN == 4 * COLS_PER_DEV and kt >= 2
    pay = dh.dtype
    blk_shape = (D_MODEL, COLS_PER_DEV)
    n_half = 2 * n_pass
    n_chip = N_DEV // 2

    def body(xt_ref, a1_ref, b1_ref, b2_ref, c1_ref, z1_ref, a2_ref, b3_ref, c2_ref, z2_ref,
             gx_ref, w_hbm, sg_ref, dx_ref, gin_ref, gsm_ref,
             acc_ref, w_ref, mine_ref, stagea_ref, landa_ref, stageb_ref, landb_ref, own_ref, lsm_ref,
             sa_sems, ra_sems, sb_sems, rb_sems, ss_sems, rs_sems, w_sem):
        s = pl.program_id(0)
        x, y, c = _my_pos()
        me = 4 * x + 2 * y + c
        chip = 2 * x + y

        def to_sibling(q):
            return pltpu.make_async_remote_copy(
                src_ref=stagea_ref.at[q], dst_ref=landa_ref.at[q], send_sem=sa_sems.at[q], recv_sem=ra_sems.at[q],
                device_id=(x, y, 1 - c), device_id_type=MESH)

        def to_owner(q):
            return pltpu.make_async_remote_copy(
                src_ref=stageb_ref.at[q], dst_ref=landb_ref.at[chip], send_sem=sb_sems.at[q],
                recv_sem=rb_sems.at[chip], device_id=(q // 2, q % 2, c), device_id_type=MESH)

        def from_chip(m):
            return pltpu.make_async_remote_copy(
                src_ref=landb_ref.at[m], dst_ref=landb_ref.at[m], send_sem=sb_sems.at[m], recv_sem=rb_sems.at[m],
                device_id=(m // 2, m % 2, c), device_id_type=MESH)

        def is_me(q):
            return (x == q // 2) & (y == q % 2)

        def small_to(j):
            return pltpu.make_async_remote_copy(
                src_ref=sg_ref, dst_ref=lsm_ref.at[me], send_sem=ss_sems.at[j], recv_sem=rs_sems.at[me],
                device_id=_dev_coords(j), device_id_type=MESH)

        def small_from(m):
            return pltpu.make_async_remote_copy(
                src_ref=lsm_ref.at[m], dst_ref=lsm_ref.at[m], send_sem=ss_sems.at[m], recv_sem=rs_sems.at[m],
                device_id=_dev_coords(m), device_id_type=MESH)

        w_copy = pltpu.make_async_copy(w_hbm, w_ref, w_sem)

        @pl.when(s == 0)
        def _():
            w_copy.start()
            for j in range(N_DEV):
                pl.when(me != j)(small_to(j).start)
            lsm_ref[me] = sg_ref[...]
            landb_ref[chip] = jnp.zeros(blk_shape, pay)

        @pl.when(s < ndw)
        def _():
            @pl.when(s % kt == 0)
            def _():
                acc_ref[...] = jnp.zeros_like(acc_ref)
            xt_ = xt_ref[...]
            @pl.when(s < kt)
            def _():
                acc_ref[:, 0:c1] += _dot(xt_, a1_ref[...])
                acc_ref[:, c1:TAIL_TN] += _dot(xt_, b1_ref[...])

            @pl.when(s >= kt)
            def _():
                acc_ref[:, 0:c2 - TAIL_TN] += _dot(xt_, b2_ref[...])
                acc_ref[:, c2 - TAIL_TN:c3 - TAIL_TN] += _dot(xt_, c1_ref[...])
                acc_ref[:, c3 - TAIL_TN:c4 - TAIL_TN] += _dot(xt_, z1_ref[...])

        for p in range(n_pass):
            @pl.when(s == p * kt + kt - 1)
            def _(p=p):
                for cc in range(2):
                    @pl.when(c == cc)
                    def _(cc=cc):
                        for yo in range(2):
                            q = 2 * p + yo
                            same, other = 2 * yo + cc, 2 * yo + 1 - cc
                            mine_ref[q] = acc_ref[:, same * COLS_PER_DEV:(same + 1) * COLS_PER_DEV]
                            stagea_ref[q] = acc_ref[:, other * COLS_PER_DEV:(other + 1) * COLS_PER_DEV].astype(pay)
                for yo in range(2):
                    to_sibling(2 * p + yo).start()

            @pl.when(s == (p + 1) * kt + 1)
            def _(p=p):
                for yo in range(2):
                    q = 2 * p + yo
                    to_sibling(q).wait_recv()

                    def chunk(i, carry, q=q):
                        rs = pl.ds(pl.multiple_of(i * REDUCE_ROWS, REDUCE_ROWS), REDUCE_ROWS)
                        tot = mine_ref[q, rs, :] + landa_ref[q, rs, :].astype(F32)

                        @pl.when(is_me(q))
                        def _():
                            own_ref[rs, :] = tot

                        @pl.when(jnp.logical_not(is_me(q)))
                        def _():
                            stageb_ref[q, rs, :] = tot.astype(pay)
                        return carry
                    lax.fori_loop(0, D_MODEL // REDUCE_ROWS, chunk, 0)
                    pl.when(jnp.logical_not(is_me(q)))(to_owner(q).start)

        @pl.when(s >= ndw)
        def _():
            pl.when(s == ndw)(w_copy.wait)
            dx_ref[...] = (_dot_nt(a2_ref[...], w_ref[:, c0:c1]) + _dot_nt(b3_ref[...], w_ref[:, c1:c2])
                           + _dot_nt(c2_ref[...], w_ref[:, c2:c3]) + _dot_nt(z2_ref[...], w_ref[:, c3:c4])
                           + gx_ref[...])

        @pl.when(s == nsteps - 1)
        def _():
            for m in range(n_chip):
                pl.when(m != chip)(from_chip(m).wait_recv)
            for m in range(N_DEV):
                pl.when(me != m)(small_from(m).wait_recv)
            for q in range(n_half):
                to_sibling(q).wait_send()
                pl.when(jnp.logical_not(is_me(q)))(to_owner(q).wait_send)
            for j in range(N_DEV):
                pl.when(me != j)(small_to(j).wait_send)

            def chunk(i, carry):
                rs = pl.ds(pl.multiple_of(i * REDUCE_ROWS, REDUCE_ROWS), REDUCE_ROWS)
                g = own_ref[rs, :]
                for m in range(n_chip):
                    g = g + landb_ref[m, rs, :].astype(F32)
                gin_ref[rs, :] = g
                return carry
            lax.fori_loop(0, D_MODEL // REDUCE_ROWS, chunk, 0)
            g = lsm_ref[0]
            for m in range(1, N_DEV):
                g = g + lsm_ref[m]
            gsm_ref[...] = g

    dw_step = lambda s: jnp.minimum(s, ndw - 1)
    dx_step = lambda s: jnp.maximum(s - ndw, 0)
    pass0 = lambda s: jnp.minimum(s, kt - 1)
    pass1 = lambda s: jnp.clip(s - kt, 0, kt - 1)
    any_spec = pl.BlockSpec(memory_space=pl.ANY)
    vmem = pl.BlockSpec(memory_space=pltpu.VMEM)
    dma = pltpu.SemaphoreType.DMA
    scratch = [pltpu.VMEM((D_MODEL, TAIL_TN), F32), pltpu.VMEM((D_MODEL, D_IN), w_in.dtype),
               pltpu.VMEM((n_half,) + blk_shape, F32),
               pltpu.VMEM((n_half,) + blk_shape, pay), pltpu.VMEM((n_half,) + blk_shape, pay),
               pltpu.VMEM((n_half,) + blk_shape, pay), pltpu.VMEM((n_chip,) + blk_shape, pay),
               pltpu.VMEM(blk_shape, F32), pltpu.VMEM((N_DEV,) + small_g.shape, F32),
               dma((n_half,)), dma((n_half,)), dma((n_half,)), dma((n_chip,)), dma((N_DEV,)), dma((N_DEV,)), dma]
    return pl.pallas_call(
        body, name="tail_dw_dx_reduce", grid=(nsteps,),
        in_specs=[pl.BlockSpec((D_MODEL, TAIL_TK), lambda s: (0, dw_step(s) % kt)),
                  pl.BlockSpec((TAIL_TK, c1 - c0), lambda s: (pass0(s), 0)),
                  pl.BlockSpec((TAIL_TK, TAIL_TN - c1), lambda s: (pass0(s), 0)),
                  pl.BlockSpec((TAIL_TK, 128), lambda s: (pass1(s), (TAIL_TN - c1) // 128)),
                  pl.BlockSpec((TAIL_TK, c3 - c2), lambda s: (pass1(s), 0)),
                  pl.BlockSpec((TAIL_TK, c4 - c3), lambda s: (pass1(s), 0)),
                  pl.BlockSpec((TAIL_TM, c1 - c0), lambda s: (dx_step(s), 0)),
                  pl.BlockSpec((TAIL_TM, c2 - c1), lambda s: (dx_step(s), 0)),
                  pl.BlockSpec((TAIL_TM, c3 - c2), lambda s: (dx_step(s), 0)),
                  pl.BlockSpec((TAIL_TM, c4 - c3), lambda s: (dx_step(s), 0)),
                  pl.BlockSpec((TAIL_TM, D_MODEL), lambda s: (dx_step(s), 0)),
                  any_spec, vmem],
        out_specs=[pl.BlockSpec((TAIL_TM, D_MODEL), lambda s: (dx_step(s), 0)), vmem, vmem],
        out_shape=[jax.ShapeDtypeStruct((T, D_MODEL), F32), jax.ShapeDtypeStruct(blk_shape, F32),
                   jax.ShapeDtypeStruct(small_g.shape, F32)],
        scratch_shapes=scratch,
        compiler_params=_cparams(("arbitrary",)),
    )(xt, dhs[0], dhs[1], dhs[1], dhs[2], dhs[3], dhs[0], dhs[1], dhs[2], dhs[3], gx1, w_in, small_g)


def _adam_update(grads, params, carried):
    n = len(grads)

    def body(*refs):
        g_refs, p_refs, o_refs = refs[1:1 + n], refs[1 + n:1 + 4 * n], refs[2 + 4 * n:]
        for a in range(n):
            rows = g_refs[a].shape[0]
            cr = REDUCE_ROWS if rows % REDUCE_ROWS == 0 else rows
            flat2 = lambda r: r.at[0] if len(r.shape) == 3 else r
            w_ref, m_ref, v_ref = [flat2(r) for r in p_refs[3 * a:3 * a + 3]]
            go_ref, d_ref, nm_ref, nv_ref = [flat2(r) for r in o_refs[4 * a:4 * a + 4]]

            def chunk(i, carry, cr=cr, g_ref=g_refs[a], w_ref=w_ref, m_ref=m_ref, v_ref=v_ref,
                      go_ref=go_ref, d_ref=d_ref, nm_ref=nm_ref, nv_ref=nv_ref):
                rs = pl.ds(pl.multiple_of(i * cr, cr), cr)
                g = g_ref[rs, :]
                go_ref[rs, :] = g
                d_ref[rs, :], nm_ref[rs, :], nv_ref[rs, :] = _adamw(w_ref[rs, :], g, m_ref[rs, :], v_ref[rs, :])
                return carry
            lax.fori_loop(0, rows // cr, chunk, 0)

    vmem = pl.BlockSpec(memory_space=pltpu.VMEM)
    any_spec = pl.BlockSpec(memory_space=pl.ANY)
    flat = [p for grp in params for p in grp]
    outs = pl.pallas_call(
        body, name="adamw", in_specs=[any_spec] + [vmem] * (4 * n), out_specs=[any_spec] + [vmem] * (4 * n),
        out_shape=[jax.ShapeDtypeStruct(carried.shape, carried.dtype)]
        + [jax.ShapeDtypeStruct(grp[0].shape, F32) for grp in params for _ in range(4)],
        input_output_aliases={0: 0},
        compiler_params=pltpu.CompilerParams(vmem_limit_bytes=VMEM_LIMIT),
    )(carried, *grads, *flat)
    return [outs[1 + 4 * a:5 + 4 * a] for a in range(n)], outs[0]


def _step(x, mem, w_in_s, w_mem_s, w_out_s, b_in, sinks, g, gain, bias, tgt):
    B = x.shape[0]
    T = B * SEQ
    x2 = x.reshape(T, D_MODEL)
    t2 = tgt.reshape(T, D_MODEL)
    rope_inv = _rope_inv()
    lane = jnp.arange(W_A)
    hsum = (lane[:W_B, None] // HEAD == lane[None, :W_B] // HEAD).astype(MXU)
    hrows = (jnp.arange(PICK_ROWS)[:, None] == lane[None, :] // HEAD).astype(MXU)
    me = 4 * lax.axis_index("x") + 2 * lax.axis_index("y") + lax.axis_index("c")

    (w_in_all,) = _gather_weights([w_in_s])
    qkva, qkvb, qc, z, w_in, xt, rope_tab, w_mem_all, w_out_all = _in_proj(
        x2, w_in_all, b_in, rope_inv, [w_mem_s, w_out_s])
    w_mem = w_mem_all.reshape(D_MODEL, 2 * W_C)
    w_out = w_out_all.reshape(D_MIX, D_MODEL)

    o_a, lse_a = _swa_fwd(qkva, sinks)
    olse_b = _dil_fwd(qkvb)
    o_c, lse_c, mkv = _mem_attn_fwd(qc, mem, w_mem)

    gx1, do_a, delta_a, dobb, do_c, delta_c, dz, dw_out, small, loss = _post(
        o_a, olse_b, o_c, z, x2, t2, g, gain, bias, w_out, hsum, hrows)

    dh_c, db_c, dw_mem = _mem_attn_bwd(qc, mkv, do_c, lse_c, delta_c, mem)
    blocks = [dw_mem.reshape(N_DEV, ROWS_PER_DEV, 2 * W_C), dw_out.reshape(N_DEV, ROWS_PER_DEV, D_MODEL)]
    sends = [b.astype(MXU) for b in blocks]
    owns = [lax.dynamic_index_in_dim(b, me, axis=0, keepdims=False) for b in blocks]
    dh_b, db_b, (g_mem, g_out) = _dil_bwd(qkvb, dobb, rope_tab, sends, owns)
    dh_a, db_a, dsink = _swa_bwd(qkva, do_a, lse_a, delta_a, sinks, rope_tab)

    small_g = _pack_small(dict(b_in=jnp.concatenate([db_a[0], db_b[0], db_c[0], small[3]]), sinks=dsink[:, 0],
                               g=small[2], gain=small[0], bias=small[1], loss=loss[0, 0]))
    grad_x, g_in, g_small = _tail(xt, (dh_a, dh_b, dh_c, dz), gx1, w_in, small_g)
    return grad_x.reshape(B, SEQ, D_MODEL), g_in, g_mem, g_out, g_small


def _my_pos():
    return lax.axis_index("x"), lax.axis_index("y"), lax.axis_index("c")


def _gather_weights(shards):
    n_arr = len(shards)

    def body(*refs):
        ins, outs = refs[0:n_arr], refs[n_arr:2 * n_arr]
        send_sems, recv_sems, local_sems = refs[2 * n_arr:]
        x, y, c = _my_pos()
        me, sibling = (x, y, c), (x, y, 1 - c)
        chips = [(1 - x, y), (x, 1 - y), (1 - x, 1 - y)]

        def slot(a, pos):
            return outs[a].at[4 * pos[0] + 2 * pos[1] + pos[2]]

        def copy(a, k, block, to, src=None):
            return pltpu.make_async_remote_copy(
                src_ref=slot(a, block) if src is None else src, dst_ref=slot(a, block),
                send_sem=send_sems.at[a, k], recv_sem=recv_sems.at[a, k],
                device_id=to, device_id_type=MESH)

        mine = [pltpu.make_async_copy(ins[a], slot(a, me), local_sems.at[a]) for a in range(n_arr)]
        for cp in mine:
            cp.start()
        first = []
        for a in range(n_arr):
            first.append(copy(a, 0, me, sibling, src=ins[a]))
            first += [copy(a, 1 + j, me, (*chip, c), src=ins[a]) for j, chip in enumerate(chips)]
        for cp in first:
            cp.start()
        passed = []
        for j, chip in enumerate(chips):
            for a in range(n_arr):
                copy(a, 1 + j, (*chip, c), me).wait_recv()
                fwd = copy(a, 4 + j, (*chip, c), sibling)
                fwd.start()
                passed.append(fwd)
        for a in range(n_arr):
            copy(a, 0, sibling, me).wait_recv()
            for j, chip in enumerate(chips):
                copy(a, 4 + j, (*chip, 1 - c), me).wait_recv()
        for cp in first + passed:
            cp.wait_send()
        for cp in mine:
            cp.wait()

    any_spec = pl.BlockSpec(memory_space=pl.ANY)
    return pl.pallas_call(
        body, name="gather_weights",
        in_specs=[any_spec] * n_arr, out_specs=[any_spec] * n_arr,
        out_shape=[jax.ShapeDtypeStruct((N_DEV,) + s.shape, s.dtype) for s in shards],
        scratch_shapes=[pltpu.SemaphoreType.DMA((n_arr, 7)), pltpu.SemaphoreType.DMA((n_arr, 7)),
                        pltpu.SemaphoreType.DMA((n_arr,))],
    )(*shards)


def _adamw(w, g, m, v):
    m = ADAM_B1 * m + (1.0 - ADAM_B1) * g
    v = ADAM_B2 * v + (1.0 - ADAM_B2) * (g * g)
    m_hat = m / (1.0 - ADAM_B1 ** ADAM_STEP)
    v_hat = v / (1.0 - ADAM_B2 ** ADAM_STEP)
    delta = -ADAM_LR * (m_hat / (jnp.sqrt(v_hat) + ADAM_EPS) + ADAM_WD * w)
    return delta, m, v


_SMALL_SIZES = (("b_in", D_IN), ("g", D_MIX), ("gain", D_MODEL), ("bias", D_MODEL), ("sinks", SWA_Q), ("loss", 1))


def _pack_small(d):
    flat = jnp.concatenate([jnp.reshape(d[k], (-1,)).astype(F32) if k in d else jnp.zeros((n,), F32)
                            for k, n in _SMALL_SIZES])
    flat = jnp.pad(flat, (0, SMALL_ROWS * 128 - flat.shape[0]))
    return flat.reshape(SMALL_ROWS, 128)


def _unpack_small(p):
    flat = p.reshape(-1)
    out, off = {}, 0
    for k, n in _SMALL_SIZES:
        out[k] = flat[off:off + n].reshape(1, n)
        off += n
    return out


def kernel(x, mem, w_in, b_in, w_mem, attn_sinks, g_branch, w_out, ln_gain, ln_bias, loss_target, m_w_in, m_b_in, m_w_mem, m_attn_sinks, m_g_branch, m_w_out, m_ln_gain, m_ln_bias, v_w_in, v_b_in, v_w_mem, v_attn_sinks, v_g_branch, v_w_out, v_ln_gain, v_ln_bias):
    grad_x, g_in, g_mem, g_out, g_small = _step(
        x, mem, w_in[0].astype(MXU), w_mem[0].astype(MXU), w_out[0].astype(MXU), b_in, attn_sinks[0],
        g_branch, ln_gain, ln_bias, loss_target)

    small_w = _pack_small(dict(b_in=b_in, g=g_branch, gain=ln_gain, bias=ln_bias, sinks=attn_sinks))
    small_m = _pack_small(dict(b_in=m_b_in, g=m_g_branch, gain=m_ln_gain, bias=m_ln_bias, sinks=m_attn_sinks))
    small_v = _pack_small(dict(b_in=v_b_in, g=v_g_branch, gain=v_ln_gain, bias=v_ln_bias, sinks=v_attn_sinks))
    grads = [g_in, g_mem, g_out, g_small]
    params = [(w_in, m_w_in, v_w_in), (w_mem, m_w_mem, v_w_mem), (w_out, m_w_out, v_w_out),
              (small_w, small_m, small_v)]
    res, grad_x = _adam_update(grads, params, grad_x)
    big = res[:3]
    sm = [_unpack_small(r) for r in res[3]]

    def group(i):
        return (big[0][i], sm[i]["b_in"], big[1][i], sm[i]["sinks"], sm[i]["g"], big[2][i],
                sm[i]["gain"], sm[i]["bias"])

    loss = sm[0]["loss"].reshape(())
    return (loss, grad_x, *group(0), *group(1), *group(2), *group(3))
```

```python
import jax
import jax.numpy as jnp
from jax import lax
from jax.experimental import pallas as pl
from jax.experimental.pallas import tpu as pltpu

F32 = jnp.float32
MXU = jnp.bfloat16

D_MODEL = 1024
SEQ = 2048
HEAD = 64
BLK = 128
SWA_Q, SWA_KV = 8, 2
DIL_H = 4
MEM_H = 4
MEM_LEN = 256
W_A, W_KVA, W_B, W_C = 512, 128, 256, 256
D_MIX = 1024
D_IN = 2816
N_DEV = 8
COLS_PER_DEV = D_IN // N_DEV
ROWS_PER_DEV = D_MODEL // N_DEV
ROPE_THETA = 10000.0
LN_EPS = 1e-5
RMS_EPS = 1e-6
ALPHA = 2.0 ** 0.25
Q_SCALE = HEAD ** -0.5
NEG = -1e30
SMALL_ROWS = 48
VMEM_LIMIT = 56 * 1024 * 1024

ADAM_LR = 0.001
ADAM_B1 = 0.9
ADAM_B2 = 0.999
ADAM_EPS = 1e-08
ADAM_WD = 0.01
ADAM_STEP = 10

MESH = pl.DeviceIdType.MESH


def _cparams(sem=None):
    return pltpu.CompilerParams(dimension_semantics=sem, vmem_limit_bytes=VMEM_LIMIT)


def _dot(a, b):
    return jnp.dot(a, b, preferred_element_type=F32)


def _dot_nt(a, b):
    return lax.dot_general(a, b, (((1,), (1,)), ((), ())), preferred_element_type=F32)


def _dot_t0(a, b):
    return lax.dot_general(a, b, (((0,), (0,)), ((), ())), preferred_element_type=F32)


def _dot_tn(a, b):
    return jnp.dot(a.T.astype(MXU), b, preferred_element_type=F32)


def _rope(t, tab, sign):
    cos, sa, sb = tab
    outs = []
    for c in range(t.shape[1] // 128):
        tc = t[:, c * 128:(c + 1) * 128]
        r = pltpu.roll(tc, 96, 1) * sa + pltpu.roll(tc, 32, 1) * sb
        outs.append(tc * cos + r if sign > 0 else tc * cos - r)
    return outs[0] if len(outs) == 1 else jnp.concatenate(outs, axis=1)


def _rope_inv():
    inv = ROPE_THETA ** (-jnp.arange(0, HEAD, 2, dtype=F32) / HEAD)
    return jnp.tile(inv, 2 * 128 // HEAD)[None, :]


def _rope_tab(pos0, rows, inv):
    pos = (lax.broadcasted_iota(jnp.int32, (rows, 128), 0) + pos0).astype(F32)
    ang = pos * inv
    cos, sin = jnp.cos(ang), jnp.sin(ang)
    first = lax.broadcasted_iota(jnp.int32, (rows, 128), 1) % HEAD < HEAD // 2
    return cos, jnp.where(first, -sin, 0.0), jnp.where(first, 0.0, sin)


def _dev_coords(j):
    return (j >> 2, (j >> 1) & 1, j & 1)


def _in_proj(x2, w_all, b_in, rope_inv, late_shards):
    T = x2.shape[0]
    tm = 512
    n_late = len(late_shards)

    def body(x_ref, wall_ref, b_ref, inv_ref, *rest):
        late_in, rest = rest[:n_late], rest[n_late:]
        qkva_ref, qkvb_ref, qc_ref, z_ref, w_ref, xt_ref, tabo_ref = rest[:7]
        late_out = rest[7:7 + n_late]
        send_sems, recv_sems, local_sems = rest[7 + n_late:]
        step, last = pl.program_id(0), pl.num_programs(0) - 1
        x, y, c = _my_pos()
        me = 4 * x + 2 * y + c

        def to_peer(a, j):
            return pltpu.make_async_remote_copy(
                src_ref=late_in[a], dst_ref=late_out[a].at[me], send_sem=send_sems.at[a, j],
                recv_sem=recv_sems.at[a, me], device_id=_dev_coords(j), device_id_type=MESH)

        def from_peer(a, m):
            return pltpu.make_async_remote_copy(
                src_ref=late_out[a].at[m], dst_ref=late_out[a].at[m], send_sem=send_sems.at[a, m],
                recv_sem=recv_sems.at[a, m], device_id=_dev_coords(m), device_id_type=MESH)

        def mine(a):
            return pltpu.make_async_copy(late_in[a], late_out[a].at[me], local_sems.at[a])

        @pl.when(step == 0)
        def _():
            for a in range(n_late):
                mine(a).start()
                for j in range(N_DEV):
                    pl.when(me != j)(to_peer(a, j).start)
            for j in range(N_DEV):
                w_ref[:, j * COLS_PER_DEV:(j + 1) * COLS_PER_DEV] = wall_ref[j]

        xb = x_ref[...].astype(MXU)
        xt_ref[...] = x_ref[...].T.astype(MXU)
        tab = _rope_tab((step % nt) * tm, tm, inv_ref[...])
        for j in range(3):
            tabo_ref[:, j * 128:(j + 1) * 128] = tab[j]

        def seg(c0, c1):
            return _dot(xb, w_ref[:, c0:c1]) + b_ref[:, c0:c1]

        qa = (_rope(seg(0, 512), tab, 1) * Q_SCALE).astype(MXU)
        for c in range(SWA_Q // 2):
            qkva_ref[c] = qa[:, c * 128:(c + 1) * 128]
        lo = lax.broadcasted_iota(jnp.int32, (tm, 128), 1) < HEAD
        for j, t in enumerate((_rope(seg(512, 640), tab, 1), seg(640, 768))):
            other = pltpu.roll(t, HEAD, 1)
            qkva_ref[4 + 2 * j] = jnp.where(lo, t, other).astype(MXU)
            qkva_ref[5 + 2 * j] = jnp.where(lo, other, t).astype(MXU)
        qkvb = (_rope(seg(768, 1024), tab, 1) * Q_SCALE, _rope(seg(1024, 1280), tab, 1), seg(1280, 1536))
        for j, t in enumerate(qkvb):
            for c in range(2):
                qkvb_ref[2 * j + c] = t[:, c * 128:(c + 1) * 128]
        qc = (seg(1536, 1792) * Q_SCALE).astype(MXU)
        for c in range(MEM_H // 2):
            qc_ref[c] = qc[:, c * 128:(c + 1) * 128]
        z_ref[...] = seg(1792, 2816)

        @pl.when(step == last)
        def _():
            for a in range(n_late):
                mine(a).wait()
                for m in range(N_DEV):
                    pl.when(me != m)(from_peer(a, m).wait_recv)
                for j in range(N_DEV):
                    pl.when(me != j)(to_peer(a, j).wait_send)

    nt = SEQ // tm
    any_spec = pl.BlockSpec(memory_space=pl.ANY)
    chunked = lambda n: pl.BlockSpec((None, n, tm, 128), lambda i: (i // nt, 0, i % nt, 0))
    return pl.pallas_call(
        body, name="in_proj_fwd",
        grid=(T // tm,),
        in_specs=[pl.BlockSpec((tm, D_MODEL), lambda i: (i, 0)),
                  pl.BlockSpec((N_DEV, D_MODEL, COLS_PER_DEV), lambda i: (0, 0, 0)),
                  pl.BlockSpec((1, D_IN), lambda i: (0, 0)),
                  pl.BlockSpec((1, 128), lambda i: (0, 0))] + [any_spec] * n_late,
        out_specs=[chunked(SWA_CHUNKS), chunked(6), chunked(MEM_H // 2),
                   pl.BlockSpec((tm, D_MIX), lambda i: (i, 0)),
                   pl.BlockSpec((D_MODEL, D_IN), lambda i: (0, 0)),
                   pl.BlockSpec((D_MODEL, tm), lambda i: (0, i)),
                   pl.BlockSpec((tm, 384), lambda i: (i, 0))] + [any_spec] * n_late,
        out_shape=[jax.ShapeDtypeStruct((T // SEQ, SWA_CHUNKS, SEQ, 128), MXU),
                   jax.ShapeDtypeStruct((T // SEQ, 6, SEQ, 128), F32),
                   jax.ShapeDtypeStruct((T // SEQ, MEM_H // 2, SEQ, 128), MXU),
                   jax.ShapeDtypeStruct((T, D_MIX), F32),
                   jax.ShapeDtypeStruct((D_MODEL, D_IN), w_all.dtype),
                   jax.ShapeDtypeStruct((D_MODEL, T), MXU),
                   jax.ShapeDtypeStruct((T, 384), F32)]
        + [jax.ShapeDtypeStruct((N_DEV,) + s.shape, s.dtype) for s in late_shards],
        scratch_shapes=[pltpu.SemaphoreType.DMA((n_late, N_DEV)), pltpu.SemaphoreType.DMA((n_late, N_DEV)),
                        pltpu.SemaphoreType.DMA((n_late,))],
        compiler_params=_cparams(("arbitrary",)),
    )(x2, w_all, b_in, rope_inv, *late_shards)


CHAIN = 4


def _band_bias(max_dist):
    kj = lax.broadcasted_iota(jnp.int32, (2 * BLK, BLK), 0)
    qi = lax.broadcasted_iota(jnp.int32, (2 * BLK, BLK), 1)
    dist = qi + BLK - kj
    band = jnp.where((dist >= 0) & (dist <= max_dist), 0.0, NEG).astype(F32)
    k1 = lax.broadcasted_iota(jnp.int32, (BLK, BLK), 0)
    q1 = lax.broadcasted_iota(jnp.int32, (BLK, BLK), 1)
    first = jnp.where((q1 - k1 >= 0) & (q1 - k1 <= max_dist), 0.0, NEG).astype(F32)
    return jnp.concatenate([band] * CHAIN, axis=1), jnp.concatenate([first] * CHAIN, axis=1)


def _lanes(parts):
    return jnp.concatenate(parts, axis=1)


PICK_ROWS = 16


def _stack_pair(t):
    lo = (lax.broadcasted_iota(jnp.int32, t.shape, 1) < HEAD).astype(F32)
    return jnp.concatenate([t * lo, t * (1.0 - lo)], axis=0).astype(MXU)


def _pair_rows(x, n):
    lo = lax.broadcasted_iota(jnp.int32, (n, 128), 1) < HEAD
    return jnp.where(lo, x[0:n], x[n:2 * n])


def _split3(t):
    if MXU == F32:
        return (t,)
    hi = t.astype(MXU)
    r = t - hi.astype(F32)
    mid = r.astype(MXU)
    return hi, mid, (r - mid.astype(F32)).astype(MXU)


def _interleave(tiles):
    tiles = list(tiles)
    while tiles:
        for t in list(tiles):
            try:
                next(t)
            except StopIteration:
                tiles.remove(t)


def _softmax_cols(sT, sinkrow=None):
    m = jnp.max(sT, axis=0, keepdims=True)
    if sinkrow is not None:
        m = jnp.maximum(m, sinkrow)
    pT = jnp.exp(sT - m)
    l = jnp.sum(pT, axis=0, keepdims=True)
    if sinkrow is not None:
        l = l + jnp.exp(sinkrow - m)
    return (pT * (1.0 / l)).astype(MXU), m + jnp.log(l)


SWA_CHUNKS = 8
SWA_UNROLL = 3
FWD_UNROLL = 5
N_QBLK = SEQ // BLK


def _swa_fwd(qkva, sinks):
    B = qkva.shape[0]
    G = SWA_Q // SWA_KV

    def body(sink_ref, qkv_ref, o_ref, lse_ref):
        band, first = _band_bias(BLK - 1)
        sinkrows = [_lanes([jnp.full((1, BLK), sink_ref[G * hk + j], F32) for j in range(G)])
                    for hk in range(SWA_KV)]

        def tile(hk, blk, rows_q, rows_k, bias):
            nk = bias.shape[0]
            k2 = _stack_pair(qkv_ref.at[4 + hk][rows_k, :])
            sT = []
            for c in (2 * hk, 2 * hk + 1):
                s2 = _dot_nt(k2, qkv_ref.at[c][rows_q, :])
                sT += [s2[0:nk], s2[nk:2 * nk]]
            yield
            pnT, lse = _softmax_cols(_lanes(sT) + bias, sinkrows[hk])
            yield
            v2 = _stack_pair(qkv_ref.at[6 + hk][rows_k, :])
            for j, c in enumerate((2 * hk, 2 * hk + 1)):
                p2 = jnp.concatenate([pnT[:, 2 * j * BLK:(2 * j + 1) * BLK],
                                      pnT[:, (2 * j + 1) * BLK:(2 * j + 2) * BLK]], axis=0)
                o_ref.at[c][rows_q, :] = _dot_t0(p2, v2)
            for j in range(G):
                lse_ref.at[blk][G * hk + j:G * hk + j + 1, :] = lse[:, j * BLK:(j + 1) * BLK]

        def tiles_at(i):
            r0 = pl.multiple_of(i * BLK, BLK)
            rk = pl.multiple_of(i * BLK - BLK, BLK)
            return [tile(hk, i, pl.ds(r0, BLK), pl.ds(rk, 2 * BLK), band) for hk in range(SWA_KV)]

        _interleave([tile(hk, 0, pl.ds(0, BLK), pl.ds(0, BLK), first) for hk in range(SWA_KV)])

        def loop(j, carry):
            _interleave([t for u in range(FWD_UNROLL) for t in tiles_at(1 + j * FWD_UNROLL + u)])
            return carry
        lax.fori_loop(0, (N_QBLK - 1) // FWD_UNROLL, loop, 0)

    return pl.pallas_call(
        body, name="swa_fwd", grid=(B,),
        in_specs=[pl.BlockSpec(memory_space=pltpu.SMEM),
                  pl.BlockSpec((None, SWA_CHUNKS, SEQ, 128), lambda b: (b, 0, 0, 0))],
        out_specs=[pl.BlockSpec((None, SWA_Q // 2, SEQ, 128), lambda b: (b, 0, 0, 0)),
                   pl.BlockSpec((None, N_QBLK, 8, 128), lambda b: (b, 0, 0, 0))],
        out_shape=[jax.ShapeDtypeStruct((B, SWA_Q // 2, SEQ, 128), F32),
                   jax.ShapeDtypeStruct((B, N_QBLK, 8, 128), F32)],
        compiler_params=_cparams(("arbitrary",)),
    )(sinks, qkva)


EP_ROWS = BLK


def _dh_tile(part_of, dh_ref, db_ref, tab_ref, blk, live=None):
    rs = pl.ds(pl.multiple_of(blk * EP_ROWS, EP_ROWS), EP_ROWS)
    tab = None if tab_ref is None else tuple(tab_ref[rs, j * 128:(j + 1) * 128] for j in range(3))
    part = part_of(rs, tab)
    yield
    dh_ref[rs, :] = part.astype(dh_ref.dtype)
    psum = jnp.sum(part, axis=0, keepdims=True)
    db_ref[0:1, :] += psum if live is None else psum * live
    yield


def _dh_tiles_behind(part_of, dh_ref, db_ref, tab_ref, j, unroll):
    live = (j > 0).astype(F32)
    return [_dh_tile(part_of, dh_ref, db_ref, tab_ref, jnp.where(j > 0, unroll * (j - 1) + u, 0), live)
            for u in range(unroll)]


def _swa_bwd(qkva, do, lse, delta, sinks, rope_tab):
    B = qkva.shape[0]
    G = SWA_Q // SWA_KV

    def body(sink_ref, qkv_ref, do_ref, lse_ref, delta_ref, rtab_ref, dh_ref, db_ref, dsink_ref, dq_ref):
        band, first = _band_bias(BLK - 1)
        sinkrows = [_lanes([jnp.full((1, BLK), sink_ref[G * hk + j], F32) for j in range(G)])
                    for hk in range(SWA_KV)]

        @pl.when(pl.program_id(0) == 0)
        def _():
            dsink_ref[...] = jnp.zeros_like(dsink_ref)
            db_ref[...] = jnp.zeros_like(db_ref)
        for c in range(4, SWA_CHUNKS):
            dq_ref[c] = jnp.zeros((SEQ, 128), F32)

        def tile(hk, blk, rows_q, rows_k, bias, accs):
            nk = bias.shape[0]
            k2 = _stack_pair(qkv_ref.at[4 + hk][rows_k, :])
            v2 = _stack_pair(qkv_ref.at[6 + hk][rows_k, :])
            qcs, docs, sT, dpT = [], [], [], []
            for c in (2 * hk, 2 * hk + 1):
                qc, doc = qkv_ref.at[c][rows_q, :], do_ref.at[c][rows_q, :]
                s2, dp2 = _dot_nt(k2, qc), _dot_nt(v2, doc)
                sT += [s2[0:nk], s2[nk:2 * nk]]
                dpT += [dp2[0:nk], dp2[nk:2 * nk]]
                qcs.append(qc)
                docs.append(doc)
            lse_r = _lanes([lse_ref.at[blk][h:h + 1, :] for h in range(G * hk, G * hk + G)])
            delta_r = _lanes([delta_ref.at[blk][h:h + 1, :] for h in range(G * hk, G * hk + G)])
            yield
            pT = jnp.exp(_lanes(sT) + bias - lse_r)
            dsT = pT * (_lanes(dpT) - delta_r)
            dsb, pb = dsT.astype(MXU), pT.astype(MXU)
            accs[hk] = accs[hk] - jnp.exp(sinkrows[hk] - lse_r) * delta_r
            yield
            dk2 = dv2 = None
            for j, c in enumerate((2 * hk, 2 * hk + 1)):
                q0, q1 = slice(2 * j * BLK, (2 * j + 1) * BLK), slice((2 * j + 1) * BLK, (2 * j + 2) * BLK)
                ds2 = jnp.concatenate([dsb[:, q0], dsb[:, q1]], axis=0)
                p2 = jnp.concatenate([pb[:, q0], pb[:, q1]], axis=0)
                dq_ref.at[c][rows_q, :] = _dot_t0(ds2, k2)
                dk2 = _dot(ds2, qcs[j]) if dk2 is None else dk2 + _dot(ds2, qcs[j])
                dv2 = _dot(p2, docs[j]) if dv2 is None else dv2 + _dot(p2, docs[j])
            dq_ref.at[4 + hk][rows_k, :] += _pair_rows(dk2, nk)
            dq_ref.at[6 + hk][rows_k, :] += _pair_rows(dv2, nk)

        def run(tiles_of, accs):
            accs = list(accs)
            _interleave(tiles_of(accs))
            return tuple(accs)

        zero = jnp.zeros((1, G * BLK), F32)
        accs = run(lambda a: [tile(hk, 0, pl.ds(0, BLK), pl.ds(0, BLK), first, a) for hk in range(SWA_KV)],
                   (zero,) * SWA_KV)

        def part_of(rs, tab):
            lo = lax.broadcasted_iota(jnp.int32, (EP_ROWS, 128), 1) < HEAD

            def kv_grad(c):
                g0, g1 = dq_ref.at[c][rs, :], dq_ref.at[c + 1][rs, :]
                return jnp.where(lo, g0 + pltpu.roll(g0, HEAD, 1), g1 + pltpu.roll(g1, HEAD, 1))
            dq = _lanes([dq_ref.at[c][rs, :] for c in range(SWA_Q // 2)])
            return _lanes([_rope(dq, tab, -1) * Q_SCALE, _rope(kv_grad(4), tab, -1), kv_grad(6)])

        trips = (N_QBLK - 1) // SWA_UNROLL

        def loop(j, accs):
            def tiles_of(a):
                out = []
                for u in range(SWA_UNROLL):
                    i = 1 + j * SWA_UNROLL + u
                    r0 = pl.multiple_of(i * BLK, BLK)
                    rk = pl.multiple_of(i * BLK - BLK, BLK)
                    out += [tile(hk, i, pl.ds(r0, BLK), pl.ds(rk, 2 * BLK), band, a) for hk in range(SWA_KV)]
                return out + _dh_tiles_behind(part_of, dh_ref, db_ref, rtab_ref, j, SWA_UNROLL)
            return run(tiles_of, accs)
        accs = lax.fori_loop(0, trips, loop, accs)
        _interleave([_dh_tile(part_of, dh_ref, db_ref, rtab_ref, blk)
                     for blk in range(SWA_UNROLL * (trips - 1), N_QBLK)])
        for hk in range(SWA_KV):
            for j in range(G):
                tot = jnp.sum(accs[hk][:, j * BLK:(j + 1) * BLK], axis=1, keepdims=True)
                dsink_ref[G * hk + j:G * hk + j + 1, :] += jnp.broadcast_to(tot, (1, 128))

    stat = pl.BlockSpec((None, N_QBLK, 8, 128), lambda b: (b, 0, 0, 0))
    return pl.pallas_call(
        body, name="swa_bwd", grid=(B,),
        in_specs=[pl.BlockSpec(memory_space=pltpu.SMEM),
                  pl.BlockSpec((None, SWA_CHUNKS, SEQ, 128), lambda b: (b, 0, 0, 0)),
                  pl.BlockSpec((None, SWA_Q // 2, SEQ, 128), lambda b: (b, 0, 0, 0)), stat, stat,
                  pl.BlockSpec((SEQ, 384), lambda b: (0, 0))],
        out_specs=[pl.BlockSpec((SEQ, W_A + 2 * W_KVA), lambda b: (b, 0)),
                   pl.BlockSpec((8, W_A + 2 * W_KVA), lambda b: (0, 0)),
                   pl.BlockSpec((8, 128), lambda b: (0, 0))],
        out_shape=[jax.ShapeDtypeStruct((B * SEQ, W_A + 2 * W_KVA), MXU),
                   jax.ShapeDtypeStruct((8, W_A + 2 * W_KVA), F32), jax.ShapeDtypeStruct((8, 128), F32)],
        scratch_shapes=[pltpu.VMEM((SWA_CHUNKS, SEQ, 128), F32)],
        compiler_params=_cparams(("arbitrary",)),
    )(sinks, qkva, do, lse, delta, rope_tab)


DILATIONS = (1, 4, 16)
DIL_PAIRS_H = DIL_H // 2


def _stream_rows(d, r, i, n):
    if d == 1:
        return pl.ds(pl.multiple_of(i * BLK, BLK), n)
    return pl.ds(r + i * (BLK * d), n, stride=d)


def _spread_matrix():
    row = lax.broadcasted_iota(jnp.int32, (PICK_ROWS, 128), 0)
    lane = lax.broadcasted_iota(jnp.int32, (PICK_ROWS, 128), 1)
    return ((row < 6) & ((row % 2 == 1) == (lane >= HEAD))).astype(MXU)


def _lanes_to_tokens(v0, v1, spread):
    n = v0.shape[1]
    row = lax.broadcasted_iota(jnp.int32, (PICK_ROWS, n), 0)
    a = jnp.zeros((PICK_ROWS, n), F32)
    for i, (p0, p1) in enumerate(zip(_split3(v0), _split3(v1))):
        a = jnp.where(row == 2 * i, p0.astype(F32), a)
        a = jnp.where(row == 2 * i + 1, p1.astype(F32), a)
    return _dot_t0(a.astype(MXU), spread)


def _tokens_to_lanes(t):
    r = t.T
    return r[0:1, :], r[HEAD:HEAD + 1, :]


DIL_UNROLL = 3


def _dil_schedule(body_first, body_next, unroll, behind=None):
    for p, d in sorted(enumerate(DILATIONS), key=lambda pd: -pd[1]):
        nblk = SEQ // d // BLK
        if d == 1:
            _interleave([body_first(p, d, 0)])
            def loop(j, c, p=p, d=d):
                _interleave([body_next(p, d, 0, 1 + unroll * j + u) for u in range(unroll)]
                            + (behind(j) if behind else []))
                return c
            lax.fori_loop(0, (nblk - 1) // unroll, loop, 0)
        elif nblk > 1:
            def loop(r, c, p=p, d=d, nblk=nblk):
                _interleave([body_first(p, d, r)] + [body_next(p, d, r, i) for i in range(1, nblk)])
                return c
            lax.fori_loop(0, d, loop, 0)
        else:
            def loop(j, c, p=p, d=d):
                _interleave([body_first(p, d, 4 * j + u) for u in range(4)])
                return c
            lax.fori_loop(0, d // 4, loop, 0)


def _dil_fwd(qkvb):
    B = qkvb.shape[0]

    def body(qkv_ref, o_ref):
        band, first = _band_bias(BLK)
        spread = _spread_matrix()

        def block(p, d, rows_q, rows_k, bias):
            nk = bias.shape[0]
            sT = []
            for c in range(DIL_PAIRS_H):
                qc = qkv_ref.at[c][rows_q, :].astype(MXU)
                s2 = _dot_nt(_stack_pair(qkv_ref.at[DIL_PAIRS_H + c][rows_k, :]), qc)
                sT += [s2[0:nk], s2[nk:2 * nk]]
            yield
            sT = _lanes(sT) + bias
            m = jnp.max(sT, axis=0, keepdims=True)
            pT = jnp.exp(sT - m)
            l = jnp.sum(pT, axis=0, keepdims=True)
            pnT = (pT * (1.0 / l)).astype(MXU)
            lse = m + jnp.log(l)
            yield
            for c in range(DIL_PAIRS_H):
                q0, q1 = slice(2 * c * BLK, (2 * c + 1) * BLK), slice((2 * c + 1) * BLK, (2 * c + 2) * BLK)
                p2 = jnp.concatenate([pnT[:, q0], pnT[:, q1]], axis=0)
                o_ref.at[p, c][rows_q, :] = _dot_t0(p2, _stack_pair(qkv_ref.at[2 * DIL_PAIRS_H + c][rows_k, :]))
                o_ref.at[p, DIL_PAIRS_H + c][rows_q, :] = _lanes_to_tokens(lse[:, q0], lse[:, q1], spread)

        def body_first(p, d, r):
            rows = _stream_rows(d, r, 0, BLK)
            return block(p, d, rows, rows, first)

        def body_next(p, d, r, i):
            return block(p, d, _stream_rows(d, r, i, BLK), _stream_rows(d, r, i - 1, 2 * BLK), band)

        _dil_schedule(body_first, body_next, FWD_UNROLL)

    return pl.pallas_call(
        body, name="dil_fwd", grid=(B,),
        in_specs=[pl.BlockSpec((None, 6, SEQ, 128), lambda b: (b, 0, 0, 0))],
        out_specs=pl.BlockSpec((None, 3, 4, SEQ, 128), lambda b: (b, 0, 0, 0, 0)),
        out_shape=jax.ShapeDtypeStruct((B, 3, 4, SEQ, 128), F32),
        compiler_params=_cparams(("arbitrary",)),
    )(qkvb)


def _reduce_scatter_ops(send_refs, land_refs, send_sems, recv_sems):
    x, y, c = _my_pos()
    me = 4 * x + 2 * y + c
    n = len(send_refs)

    def to_peer(a, j):
        return pltpu.make_async_remote_copy(
            src_ref=send_refs[a].at[j], dst_ref=land_refs[a].at[me], send_sem=send_sems.at[a, j],
            recv_sem=recv_sems.at[a, me], device_id=_dev_coords(j), device_id_type=MESH)

    def from_peer(a, m):
        return pltpu.make_async_remote_copy(
            src_ref=land_refs[a].at[m], dst_ref=land_refs[a].at[m], send_sem=send_sems.at[a, m],
            recv_sem=recv_sems.at[a, m], device_id=_dev_coords(m), device_id_type=MESH)

    def start():
        for j in range(N_DEV):
            @pl.when(me != j)
            def _(j=j):
                for a in range(n):
                    to_peer(a, j).start()
        for a in range(n):
            land_refs[a][me] = jnp.zeros(land_refs[a].shape[1:], land_refs[a].dtype)

    def finish(own_refs, out_refs):
        for m in range(N_DEV):
            @pl.when(me != m)
            def _(m=m):
                for a in range(n):
                    from_peer(a, m).wait_recv()
        for j in range(N_DEV):
            @pl.when(me != j)
            def _(j=j):
                for a in range(n):
                    to_peer(a, j).wait_send()
        for a in range(n):
            def chunk(i, carry, a=a):
                rs = pl.ds(pl.multiple_of(i * REDUCE_ROWS, REDUCE_ROWS), REDUCE_ROWS)
                g = own_refs[a][rs, :]
                for m in range(N_DEV):
                    g = g + land_refs[a][m, rs, :].astype(F32)
                out_refs[a][rs, :] = g
                return carry
            lax.fori_loop(0, own_refs[a].shape[0] // REDUCE_ROWS, chunk, 0)

    return start, finish


def _dil_bwd(qkvb, dobb, rope_tab, sends, owns):
    B = qkvb.shape[0]
    n_rs = len(sends)

    def body(qkv_ref, dob_ref, rtab_ref, *rest):
        send_refs, own_refs = rest[:n_rs], rest[n_rs:2 * n_rs]
        dh_ref, db_ref = rest[2 * n_rs:2 * n_rs + 2]
        out_refs = rest[2 * n_rs + 2:3 * n_rs + 2]
        dq_ref = rest[3 * n_rs + 2]
        land_refs = rest[3 * n_rs + 3:4 * n_rs + 3]
        send_sems, recv_sems = rest[4 * n_rs + 3:]
        rs_start, rs_finish = _reduce_scatter_ops(send_refs, land_refs, send_sems, recv_sems)
        pl.when(pl.program_id(0) == 0)(rs_start)

        band, first = _band_bias(BLK)
        dq_ref[...] = jnp.zeros_like(dq_ref)

        @pl.when(pl.program_id(0) == 0)
        def _():
            db_ref[...] = jnp.zeros_like(db_ref)

        def block(p, d, rows_q, rows_k, bias):
            nk = bias.shape[0]
            lo = lax.broadcasted_iota(jnp.int32, (nk, 128), 1) < HEAD
            qcs, docs, k2s, sT, dpT, lse, delta = [], [], [], [], [], [], []
            for c in range(DIL_PAIRS_H):
                qc = qkv_ref.at[c][rows_q, :].astype(MXU)
                doc = dob_ref.at[c][rows_q, :].astype(MXU)
                k2 = _stack_pair(qkv_ref.at[DIL_PAIRS_H + c][rows_k, :])
                s2 = _dot_nt(k2, qc)
                dp2 = _dot_nt(_stack_pair(qkv_ref.at[2 * DIL_PAIRS_H + c][rows_k, :]), doc)
                sT += [s2[0:nk], s2[nk:2 * nk]]
                dpT += [dp2[0:nk], dp2[nk:2 * nk]]
                lse += _tokens_to_lanes(dob_ref.at[DIL_PAIRS_H + c][rows_q, :])
                delta += _tokens_to_lanes(dob_ref.at[2 * DIL_PAIRS_H + c][rows_q, :])
                qcs.append(qc)
                docs.append(doc)
                k2s.append(k2)
            yield
            pT = jnp.exp(_lanes(sT) + bias - _lanes(lse))
            dsT = pT * (_lanes(dpT) - _lanes(delta))
            dsb, pb = dsT.astype(MXU), pT.astype(MXU)
            yield
            for c in range(DIL_PAIRS_H):
                q0, q1 = slice(2 * c * BLK, (2 * c + 1) * BLK), slice((2 * c + 1) * BLK, (2 * c + 2) * BLK)
                ds2 = jnp.concatenate([dsb[:, q0], dsb[:, q1]], axis=0)
                p2 = jnp.concatenate([pb[:, q0], pb[:, q1]], axis=0)
                dq_ref.at[c][rows_q, :] += _dot_t0(ds2, k2s[c])
                dk2, dv2 = _dot(ds2, qcs[c]), _dot(p2, docs[c])
                dq_ref.at[DIL_PAIRS_H + c][rows_k, :] += jnp.where(lo, dk2[0:nk], dk2[nk:2 * nk])
                dq_ref.at[2 * DIL_PAIRS_H + c][rows_k, :] += jnp.where(lo, dv2[0:nk], dv2[nk:2 * nk])

        def body_first(p, d, r):
            rows = _stream_rows(d, r, 0, BLK)
            return block(p, d, rows, rows, first)

        def body_next(p, d, r, i):
            return block(p, d, _stream_rows(d, r, i, BLK), _stream_rows(d, r, i - 1, 2 * BLK), band)

        def part_of(rs, tab):
            q, k, v = [_lanes([dq_ref.at[2 * j][rs, :], dq_ref.at[2 * j + 1][rs, :]]) for j in range(3)]
            return _lanes([_rope(q, tab, -1) * Q_SCALE, _rope(k, tab, -1), v])

        _dil_schedule(body_first, body_next, DIL_UNROLL,
                      lambda j: _dh_tiles_behind(part_of, dh_ref, db_ref, rtab_ref, j, DIL_UNROLL))
        _interleave([_dh_tile(part_of, dh_ref, db_ref, rtab_ref, blk)
                     for blk in range(DIL_UNROLL * ((N_QBLK - 1) // DIL_UNROLL - 1), N_QBLK)])

        @pl.when(pl.program_id(0) == pl.num_programs(0) - 1)
        def _():
            rs_finish(own_refs, out_refs)

    spec = pl.BlockSpec((None, 6, SEQ, 128), lambda b: (b, 0, 0, 0))
    any_spec = pl.BlockSpec(memory_space=pl.ANY)
    vmem = pl.BlockSpec(memory_space=pltpu.VMEM)
    outs = pl.pallas_call(
        body, name="dil_bwd", grid=(B,),
        in_specs=[spec, spec, pl.BlockSpec((SEQ, 384), lambda b: (0, 0))] + [any_spec] * n_rs + [vmem] * n_rs,
        out_specs=[pl.BlockSpec((SEQ, 3 * W_B), lambda b: (b, 0)), pl.BlockSpec((8, 3 * W_B), lambda b: (0, 0))]
        + [vmem] * n_rs,
        out_shape=[jax.ShapeDtypeStruct((B * SEQ, 3 * W_B), MXU), jax.ShapeDtypeStruct((8, 3 * W_B), F32)]
        + [jax.ShapeDtypeStruct(o.shape, F32) for o in owns],
        scratch_shapes=[pltpu.VMEM((6, SEQ, 128), F32)] + [pltpu.VMEM(s.shape, s.dtype) for s in sends]
        + [pltpu.SemaphoreType.DMA((n_rs, N_DEV)), pltpu.SemaphoreType.DMA((n_rs, N_DEV))],
        compiler_params=_cparams(("arbitrary",)),
    )(qkvb, dobb, rope_tab, *sends, *owns)
    return outs[0], outs[1], outs[2:]


MEM_UNROLL = 4
MEM_PAIRS = MEM_H // 2


def _mem_attn_fwd(qc, mem, w_mem):
    B = qc.shape[0]

    def body(q_ref, mem_ref, w_ref, o_ref, lse_ref, mkv_ref, k2_ref, v2_ref):
        mkv = _dot(mem_ref[...].astype(MXU), w_ref[...])
        mkv_ref[...] = mkv.astype(MXU)
        for c in range(MEM_PAIRS):
            k2_ref[c] = _stack_pair(mkv[:, c * 128:(c + 1) * 128])
            v2_ref[c] = _stack_pair(mkv[:, W_C + c * 128:W_C + (c + 1) * 128])
        lse_ref[...] = jnp.zeros_like(lse_ref)

        def tile(blk):
            rows = pl.ds(pl.multiple_of(blk * BLK, BLK), BLK)
            sT = []
            for c in range(MEM_PAIRS):
                s2 = _dot_nt(k2_ref[c], q_ref.at[c][rows, :])
                sT += [s2[0:MEM_LEN], s2[MEM_LEN:2 * MEM_LEN]]
            yield
            pnT, lse = _softmax_cols(_lanes(sT))
            yield
            for c in range(MEM_PAIRS):
                p2 = jnp.concatenate([pnT[:, 2 * c * BLK:(2 * c + 1) * BLK],
                                      pnT[:, (2 * c + 1) * BLK:(2 * c + 2) * BLK]], axis=0)
                o_ref.at[c][rows, :] = _dot_t0(p2, v2_ref[c])
            for h in range(MEM_H):
                lse_ref.at[blk][h:h + 1, :] = lse[:, h * BLK:(h + 1) * BLK]

        def loop(j, carry):
            _interleave([tile(j * MEM_UNROLL + u) for u in range(MEM_UNROLL)])
            return carry
        lax.fori_loop(0, N_QBLK // MEM_UNROLL, loop, 0)

    return pl.pallas_call(
        body, name="mem_attn_fwd", grid=(B,),
        in_specs=[pl.BlockSpec((None, MEM_PAIRS, SEQ, 128), lambda b: (b, 0, 0, 0)),
                  pl.BlockSpec((None, MEM_LEN, D_MODEL), lambda b: (b, 0, 0)),
                  pl.BlockSpec((D_MODEL, 2 * W_C), lambda b: (0, 0))],
        out_specs=[pl.BlockSpec((None, MEM_PAIRS, SEQ, 128), lambda b: (b, 0, 0, 0)),
                   pl.BlockSpec((None, N_QBLK, 8, 128), lambda b: (b, 0, 0, 0)),
                   pl.BlockSpec((None, MEM_LEN, 2 * W_C), lambda b: (b, 0, 0))],
        out_shape=[jax.ShapeDtypeStruct((B, MEM_PAIRS, SEQ, 128), F32),
                   jax.ShapeDtypeStruct((B, N_QBLK, 8, 128), F32),
                   jax.ShapeDtypeStruct((B, MEM_LEN, 2 * W_C), MXU)],
        scratch_shapes=[pltpu.VMEM((MEM_PAIRS, 2 * MEM_LEN, 128), MXU), pltpu.VMEM((MEM_PAIRS, 2 * MEM_LEN, 128), MXU)],
        compiler_params=_cparams(("arbitrary",)),
    )(qc, mem, w_mem)


def _mem_attn_bwd(qc, mkv, do, lse, delta, mem):
    B = qc.shape[0]

    def body(q_ref, mkv_ref, do_ref, lse_ref, delta_ref, mem_ref, dh_ref, db_ref, dw_ref,
             dq_ref, dmkv_ref, k2_ref, v2_ref):
        @pl.when(pl.program_id(0) == 0)
        def _():
            dw_ref[...] = jnp.zeros_like(dw_ref)
            db_ref[...] = jnp.zeros_like(db_ref)
        dmkv_ref[...] = jnp.zeros_like(dmkv_ref)
        dq_ref[...] = jnp.zeros_like(dq_ref)
        for c in range(MEM_PAIRS):
            k2_ref[c] = _stack_pair(mkv_ref[:, c * 128:(c + 1) * 128])
            v2_ref[c] = _stack_pair(mkv_ref[:, W_C + c * 128:W_C + (c + 1) * 128])

        def tile(blk):
            rows = pl.ds(pl.multiple_of(blk * BLK, BLK), BLK)
            qcs, docs, sT, dpT = [], [], [], []
            for c in range(MEM_PAIRS):
                qc_, doc = q_ref.at[c][rows, :], do_ref.at[c][rows, :]
                s2, dp2 = _dot_nt(k2_ref[c], qc_), _dot_nt(v2_ref[c], doc)
                sT += [s2[0:MEM_LEN], s2[MEM_LEN:2 * MEM_LEN]]
                dpT += [dp2[0:MEM_LEN], dp2[MEM_LEN:2 * MEM_LEN]]
                qcs.append(qc_)
                docs.append(doc)
            lse_r = _lanes([lse_ref.at[blk][h:h + 1, :] for h in range(MEM_H)])
            delta_r = _lanes([delta_ref.at[blk][h:h + 1, :] for h in range(MEM_H)])
            yield
            pT = jnp.exp(_lanes(sT) - lse_r)
            dsT = pT * (_lanes(dpT) - delta_r)
            dsb, pb = dsT.astype(MXU), pT.astype(MXU)
            yield
            for c in range(MEM_PAIRS):
                q0, q1 = slice(2 * c * BLK, (2 * c + 1) * BLK), slice((2 * c + 1) * BLK, (2 * c + 2) * BLK)
                ds2 = jnp.concatenate([dsb[:, q0], dsb[:, q1]], axis=0)
                p2 = jnp.concatenate([pb[:, q0], pb[:, q1]], axis=0)
                dq_ref.at[c][rows, :] = _dot_t0(ds2, k2_ref[c])
                dmkv_ref[:, c * 128:(c + 1) * 128] += _pair_rows(_dot(ds2, qcs[c]), MEM_LEN)
                dmkv_ref[:, W_C + c * 128:W_C + (c + 1) * 128] += _pair_rows(_dot(p2, docs[c]), MEM_LEN)

        def part_of(rs, tab):
            return _lanes([dq_ref.at[c][rs, :] for c in range(MEM_PAIRS)]) * Q_SCALE

        trips = N_QBLK // MEM_UNROLL

        def loop(j, carry):
            _interleave([tile(j * MEM_UNROLL + u) for u in range(MEM_UNROLL)]
                        + _dh_tiles_behind(part_of, dh_ref, db_ref, None, j, MEM_UNROLL))
            return carry
        lax.fori_loop(0, trips, loop, 0)
        _interleave([_dh_tile(part_of, dh_ref, db_ref, None, blk) for blk in range(MEM_UNROLL * (trips - 1), N_QBLK)])
        dw_ref[...] += _dot_tn(mem_ref[...], dmkv_ref[...].astype(MXU))

    stat = pl.BlockSpec((None, N_QBLK, 8, 128), lambda b: (b, 0, 0, 0))
    pairs = pl.BlockSpec((None, MEM_PAIRS, SEQ, 128), lambda b: (b, 0, 0, 0))
    return pl.pallas_call(
        body, name="mem_attn_bwd", grid=(B,),
        in_specs=[pairs, pl.BlockSpec((None, MEM_LEN, 2 * W_C), lambda b: (b, 0, 0)), pairs, stat, stat,
                  pl.BlockSpec((None, MEM_LEN, D_MODEL), lambda b: (b, 0, 0))],
        out_specs=[pl.BlockSpec((SEQ, W_C), lambda b: (b, 0)), pl.BlockSpec((8, W_C), lambda b: (0, 0)),
                   pl.BlockSpec((D_MODEL, 2 * W_C), lambda b: (0, 0))],
        out_shape=[jax.ShapeDtypeStruct((B * SEQ, W_C), MXU), jax.ShapeDtypeStruct((8, W_C), F32),
                   jax.ShapeDtypeStruct((D_MODEL, 2 * W_C), F32)],
        scratch_shapes=[pltpu.VMEM((MEM_PAIRS, SEQ, 128), F32), pltpu.VMEM((MEM_LEN, 2 * W_C), F32),
                        pltpu.VMEM((MEM_PAIRS, 2 * MEM_LEN, 128), MXU), pltpu.VMEM((MEM_PAIRS, 2 * MEM_LEN, 128), MXU)],
        compiler_params=_cparams(("arbitrary",)),
    )(qc, mkv, do, lse, delta, mem)


def _headsum(t, e):
    if MXU == F32:
        return _dot(t, e)
    hi = t.astype(MXU)
    lo = (t - hi.astype(F32)).astype(MXU)
    return _dot(hi, e) + _dot(lo, e)


def _heads_to_rows(t, e):
    return sum(_dot_nt(e, part) for part in _split3(t))


POST_TM = 256
POST_ROWS = 256


def _post(o_a, olse_b, o_c, z, x2, tgt, g, gain, bias, w_out, hsum, hrows):
    T = x2.shape[0]
    tm = POST_TM
    nt = SEQ // tm

    def body(oa_ref, ob_ref, oc_ref, z_ref, x_ref, t_ref, g_ref, gain_ref, bias_ref, w_ref, e_ref, er_ref,
             gx_ref, doa_ref, dela_ref, dobb_ref, doc_ref, delc_ref, dz_ref, dw_ref, small_ref, loss_ref):
        @pl.when(pl.program_id(0) == 0)
        def _():
            dw_ref[...] = jnp.zeros_like(dw_ref)
            small_ref[...] = jnp.zeros_like(small_ref)
            loss_ref[...] = jnp.zeros_like(loss_ref)

        gg = g_ref[...]
        gain_v = gain_ref[...]
        gain_s = gain_v * (1.0 / D_MODEL)
        bias_v = bias_ref[...]
        w = w_ref[...]

        def rms(o):
            rr = lax.rsqrt(jnp.mean(o * o, axis=1, keepdims=True) + RMS_EPS)
            return o * rr, rr

        def rows_of(rs, results):
            oa = _lanes([oa_ref.at[c][rs, :] for c in range(SWA_Q // 2)])
            (o1, l1), (o4, l4), (o16, l16) = [
                (_lanes([ob_ref.at[p, 0][rs, :], ob_ref.at[p, 1][rs, :]]),
                 _lanes([ob_ref.at[p, 2][rs, :], ob_ref.at[p, 3][rs, :]])) for p in range(3)]
            mx = jnp.maximum(jnp.maximum(l1, l4), l16)
            e1, e4, e16 = jnp.exp(l1 - mx), jnp.exp(l4 - mx), jnp.exp(l16 - mx)
            den = e1 + e4 + e16
            ob = (e1 * o1 + e4 * o4 + e16 * o16) / den
            lse_b = mx + jnp.log(den)
            oc = _lanes([oc_ref.at[c][rs, :] for c in range(MEM_PAIRS)])
            na, ra = rms(oa)
            nb, rb = rms(ob)
            nc, rc = rms(oc)
            n = jnp.concatenate([na, nb, nc], axis=1)
            zz = z_ref[rs, :]
            sig = 0.5 * jnp.tanh(0.5 * zz) + 0.5
            sz = zz * sig
            gs = gg * sz
            u = n * gs
            yo = _dot(u.astype(MXU), w)
            yield
            r = ALPHA * x_ref[rs, :] + yo
            rc0 = r - jnp.mean(r, axis=1, keepdims=True)
            rstd = lax.rsqrt(jnp.mean(rc0 * rc0, axis=1, keepdims=True) + LN_EPS)
            xhat = rc0 * rstd
            err = xhat * gain_v + bias_v - t_ref[rs, :]
            dxh = err * gain_s
            dr = rstd * (dxh - jnp.mean(dxh, axis=1, keepdims=True)
                         - xhat * jnp.mean(dxh * xhat, axis=1, keepdims=True))
            gx_ref[rs, :] = ALPHA * dr
            drb = dr.astype(MXU)
            du = _dot_nt(drb, w)
            yield
            dun = du * n
            dz = dun * (gg * (sig + sz * (1.0 - sig)))
            dz_ref[rs, :] = dz.astype(MXU)
            dn = du * gs

            def branch(lo, hi, nbr, rr):
                dnb = dn[:, lo:hi]
                return rr * (dnb - nbr * jnp.mean(dnb * nbr, axis=1, keepdims=True))

            def to_kernel(dob, o, do_ref, delta_ref):
                wd = dob.shape[1]
                for c in range(wd // 128):
                    do_ref.at[c][rs, :] = dob[:, c * 128:(c + 1) * 128].astype(do_ref.dtype)
                dT = _heads_to_rows(dob * o, er_ref[:, 0:wd])
                for jb in range((rs.stop - rs.start) // BLK):
                    delta_ref[rs.start // BLK + jb] = dT[0:8, jb * BLK:(jb + 1) * BLK]

            to_kernel(branch(0, W_A, na, ra), oa, doa_ref, dela_ref)
            to_kernel(branch(W_A + W_B, D_MIX, nc, rc), oc, doc_ref, delc_ref)
            dob = branch(W_A, W_A + W_B, nb, rb)
            for j, t in enumerate((dob, lse_b, _headsum(dob * ob, e_ref[...]))):
                for c in range(W_B // 128):
                    dobb_ref.at[j * (W_B // 128) + c][rs, :] = t[:, c * 128:(c + 1) * 128]
            csum = lambda t: jnp.sum(t, axis=0, keepdims=True)
            results.append((u, drb, jnp.sum(err * err), csum(err * xhat), csum(err), csum(dun * sz), csum(dz)))

        parts = []
        _interleave([rows_of(slice(k * POST_ROWS, (k + 1) * POST_ROWS), parts) for k in range(tm // POST_ROWS)])
        tot = [sum(p[i] for p in parts) for i in range(2, 7)]
        dw_ref[...] += _dot_tn(jnp.concatenate([p[0] for p in parts], axis=0),
                               jnp.concatenate([p[1] for p in parts], axis=0))
        loss_ref[...] += 0.5 * tot[0] * (1.0 / D_MODEL)
        small_ref[0:1, :] += tot[1] * (1.0 / D_MODEL)
        small_ref[1:2, :] += tot[2] * (1.0 / D_MODEL)
        small_ref[2:3, :] += tot[3]
        small_ref[3:4, :] += tot[4]

    B = T // SEQ
    row = lambda w: pl.BlockSpec((tm, w), lambda i: (i, 0))
    full = lambda a, b: pl.BlockSpec((a, b), lambda i: (0, 0))
    chunked = lambda n: pl.BlockSpec((None, n, tm, 128), lambda i: (i // nt, 0, i % nt, 0))
    stat = pl.BlockSpec((None, tm // BLK, 8, 128), lambda i: (i // nt, i % nt, 0, 0))
    return pl.pallas_call(
        body, name="post_fwd_bwd", grid=(T // tm,),
        in_specs=[chunked(SWA_Q // 2), pl.BlockSpec((None, 3, 4, tm, 128), lambda i: (i // nt, 0, 0, i % nt, 0)),
                  chunked(MEM_PAIRS),
                  row(D_MIX), row(D_MODEL), row(D_MODEL),
                  full(1, D_MIX), full(1, D_MODEL), full(1, D_MODEL), full(D_MIX, D_MODEL), full(W_B, W_B),
                  full(PICK_ROWS, W_A)],
        out_specs=[row(D_MODEL), chunked(SWA_Q // 2), stat, chunked(6), chunked(MEM_PAIRS), stat, row(D_MIX),
                   full(D_MIX, D_MODEL), full(8, D_MODEL), full(8, 128)],
        out_shape=[jax.ShapeDtypeStruct((T, D_MODEL), F32),
                   jax.ShapeDtypeStruct((B, SWA_Q // 2, SEQ, 128), MXU),
                   jax.ShapeDtypeStruct((B, N_QBLK, 8, 128), F32),
                   jax.ShapeDtypeStruct((B, 6, SEQ, 128), F32),
                   jax.ShapeDtypeStruct((B, MEM_PAIRS, SEQ, 128), MXU),
                   jax.ShapeDtypeStruct((B, N_QBLK, 8, 128), F32),
                   jax.ShapeDtypeStruct((T, D_MIX), MXU),
                   jax.ShapeDtypeStruct((D_MIX, D_MODEL), F32),
                   jax.ShapeDtypeStruct((8, D_MODEL), F32),
                   jax.ShapeDtypeStruct((8, 128), F32)],
        compiler_params=_cparams(("arbitrary",)),
    )(o_a, olse_b, o_c, z, x2, tgt, g, gain, bias, w_out, hsum, hrows)


TAIL_TK = 512
TAIL_TN = D_IN // 2
TAIL_TM = 256
REDUCE_ROWS = 128


DH_SPLITS = (0, W_A + 2 * W_KVA, W_A + 2 * W_KVA + 3 * W_B, D_IN - D_MIX, D_IN)


def _tail(xt, dhs, gx1, w_in, small_g):
    T = xt.shape[1]
    dh = dhs[0]
    c0, c1, c2, c3, c4 = DH_SPLITS
    assert c1 < TAIL_TN < c2 and (TAIL_TN - c1) % 128 == 0
    kt = T // TAIL_TK
    ndw = (D_IN // TAIL_TN) * kt
    nsteps = ndw + T // TAIL_TM
    n_pass = D_IN // TAIL_TN
    assert n_pass == 2 and TAIL_TN == 4 * COLS_PER_DEV and kt >= 2
    pay = dh.dtype
    blk_shape = (D_MODEL, COLS_PER_DEV)
    n_half = 2 * n_pass
    n_chip = N_DEV // 2

    def body(xt_ref, a1_ref, b1_ref, b2_ref, c1_ref, z1_ref, a2_ref, b3_ref, c2_ref, z2_ref,
             gx_ref, w_hbm, sg_ref, dx_ref, gin_ref, gsm_ref,
             acc_ref, w_ref, mine_ref, stagea_ref, landa_ref, stageb_ref, landb_ref, own_ref, lsm_ref,
             sa_sems, ra_sems, sb_sems, rb_sems, ss_sems, rs_sems, w_sem):
        s = pl.program_id(0)
        x, y, c = _my_pos()
        me = 4 * x + 2 * y + c
        chip = 2 * x + y

        def to_sibling(q):
            return pltpu.make_async_remote_copy(
                src_ref=stagea_ref.at[q], dst_ref=landa_ref.at[q], send_sem=sa_sems.at[q], recv_sem=ra_sems.at[q],
                device_id=(x, y, 1 - c), device_id_type=MESH)

        def to_owner(q):
            return pltpu.make_async_remote_copy(
                src_ref=stageb_ref.at[q], dst_ref=landb_ref.at[chip], send_sem=sb_sems.at[q],
                recv_sem=rb_sems.at[chip], device_id=(q // 2, q % 2, c), device_id_type=MESH)

        def from_chip(m):
            return pltpu.make_async_remote_copy(
                src_ref=landb_ref.at[m], dst_ref=landb_ref.at[m], send_sem=sb_sems.at[m], recv_sem=rb_sems.at[m],
                device_id=(m // 2, m % 2, c), device_id_type=MESH)

        def is_me(q):
            return (x == q // 2) & (y == q % 2)

        def small_to(j):
            return pltpu.make_async_remote_copy(
                src_ref=sg_ref, dst_ref=lsm_ref.at[me], send_sem=ss_sems.at[j], recv_sem=rs_sems.at[me],
                device_id=_dev_coords(j), device_id_type=MESH)

        def small_from(m):
            return pltpu.make_async_remote_copy(
                src_ref=lsm_ref.at[m], dst_ref=lsm_ref.at[m], send_sem=ss_sems.at[m], recv_sem=rs_sems.at[m],
                device_id=_dev_coords(m), device_id_type=MESH)

        w_copy = pltpu.make_async_copy(w_hbm, w_ref, w_sem)

        @pl.when(s == 0)
        def _():
            w_copy.start()
            for j in range(N_DEV):
                pl.when(me != j)(small_to(j).start)
            lsm_ref[me] = sg_ref[...]
            landb_ref[chip] = jnp.zeros(blk_shape, pay)

        @pl.when(s < ndw)
        def _():
            @pl.when(s % kt == 0)
            def _():
                acc_ref[...] = jnp.zeros_like(acc_ref)
            xt_ = xt_ref[...]
            @pl.when(s < kt)
            def _():
                acc_ref[:, 0:c1] += _dot(xt_, a1_ref[...])
                acc_ref[:, c1:TAIL_TN] += _dot(xt_, b1_ref[...])

            @pl.when(s >= kt)
            def _():
                acc_ref[:, 0:c2 - TAIL_TN] += _dot(xt_, b2_ref[...])
                acc_ref[:, c2 - TAIL_TN:c3 - TAIL_TN] += _dot(xt_, c1_ref[...])
                acc_ref[:, c3 - TAIL_TN:c4 - TAIL_TN] += _dot(xt_, z1_ref[...])

        for p in range(n_pass):
            @pl.when(s == p * kt + kt - 1)
            def _(p=p):
                for cc in range(2):
                    @pl.when(c == cc)
                    def _(cc=cc):
                        for yo in range(2):
                            q = 2 * p + yo
                            same, other = 2 * yo + cc, 2 * yo + 1 - cc
                            mine_ref[q] = acc_ref[:, same * COLS_PER_DEV:(same + 1) * COLS_PER_DEV]
                            stagea_ref[q] = acc_ref[:, other * COLS_PER_DEV:(other + 1) * COLS_PER_DEV].astype(pay)
                for yo in range(2):
                    to_sibling(2 * p + yo).start()

            @pl.when(s == (p + 1) * kt + 1)
            def _(p=p):
                for yo in range(2):
                    q = 2 * p + yo
                    to_sibling(q).wait_recv()

                    def chunk(i, carry, q=q):
                        rs = pl.ds(pl.multiple_of(i * REDUCE_ROWS, REDUCE_ROWS), REDUCE_ROWS)
                        tot = mine_ref[q, rs, :] + landa_ref[q, rs, :].astype(F32)

                        @pl.when(is_me(q))
                        def _():
                            own_ref[rs, :] = tot

                        @pl.when(jnp.logical_not(is_me(q)))
                        def _():
                            stageb_ref[q, rs, :] = tot.astype(pay)
                        return carry
                    lax.fori_loop(0, D_MODEL // REDUCE_ROWS, chunk, 0)
                    pl.when(jnp.logical_not(is_me(q)))(to_owner(q).start)

        @pl.when(s >= ndw)
        def _():
            pl.when(s == ndw)(w_copy.wait)
            dx_ref[...] = (_dot_nt(a2_ref[...], w_ref[:, c0:c1]) + _dot_nt(b3_ref[...], w_ref[:, c1:c2])
                           + _dot_nt(c2_ref[...], w_ref[:, c2:c3]) + _dot_nt(z2_ref[...], w_ref[:, c3:c4])
                           + gx_ref[...])

        @pl.when(s == nsteps - 1)
        def _():
            for m in range(n_chip):
                pl.when(m != chip)(from_chip(m).wait_recv)
            for m in range(N_DEV):
                pl.when(me != m)(small_from(m).wait_recv)
            for q in range(n_half):
                to_sibling(q).wait_send()
                pl.when(jnp.logical_not(is_me(q)))(to_owner(q).wait_send)
            for j in range(N_DEV):
                pl.when(me != j)(small_to(j).wait_send)

            def chunk(i, carry):
                rs = pl.ds(pl.multiple_of(i * REDUCE_ROWS, REDUCE_ROWS), REDUCE_ROWS)
                g = own_ref[rs, :]
                for m in range(n_chip):
                    g = g + landb_ref[m, rs, :].astype(F32)
                gin_ref[rs, :] = g
                return carry
            lax.fori_loop(0, D_MODEL // REDUCE_ROWS, chunk, 0)
            g = lsm_ref[0]
            for m in range(1, N_DEV):
                g = g + lsm_ref[m]
            gsm_ref[...] = g

    dw_step = lambda s: jnp.minimum(s, ndw - 1)
    dx_step = lambda s: jnp.maximum(s - ndw, 0)
    pass0 = lambda s: jnp.minimum(s, kt - 1)
    pass1 = lambda s: jnp.clip(s - kt, 0, kt - 1)
    any_spec = pl.BlockSpec(memory_space=pl.ANY)
    vmem = pl.BlockSpec(memory_space=pltpu.VMEM)
    dma = pltpu.SemaphoreType.DMA
    scratch = [pltpu.VMEM((D_MODEL, TAIL_TN), F32), pltpu.VMEM((D_MODEL, D_IN), w_in.dtype),
               pltpu.VMEM((n_half,) + blk_shape, F32),
               pltpu.VMEM((n_half,) + blk_shape, pay), pltpu.VMEM((n_half,) + blk_shape, pay),
               pltpu.VMEM((n_half,) + blk_shape, pay), pltpu.VMEM((n_chip,) + blk_shape, pay),
               pltpu.VMEM(blk_shape, F32), pltpu.VMEM((N_DEV,) + small_g.shape, F32),
               dma((n_half,)), dma((n_half,)), dma((n_half,)), dma((n_chip,)), dma((N_DEV,)), dma((N_DEV,)), dma]
    return pl.pallas_call(
        body, name="tail_dw_dx_reduce", grid=(nsteps,),
        in_specs=[pl.BlockSpec((D_MODEL, TAIL_TK), lambda s: (0, dw_step(s) % kt)),
                  pl.BlockSpec((TAIL_TK, c1 - c0), lambda s: (pass0(s), 0)),
                  pl.BlockSpec((TAIL_TK, TAIL_TN - c1), lambda s: (pass0(s), 0)),
                  pl.BlockSpec((TAIL_TK, 128), lambda s: (pass1(s), (TAIL_TN - c1) // 128)),
                  pl.BlockSpec((TAIL_TK, c3 - c2), lambda s: (pass1(s), 0)),
                  pl.BlockSpec((TAIL_TK, c4 - c3), lambda s: (pass1(s), 0)),
                  pl.BlockSpec((TAIL_TM, c1 - c0), lambda s: (dx_step(s), 0)),
                  pl.BlockSpec((TAIL_TM, c2 - c1), lambda s: (dx_step(s), 0)),
                  pl.BlockSpec((TAIL_TM, c3 - c2), lambda s: (dx_step(s), 0)),
                  pl.BlockSpec((TAIL_TM, c4 - c3), lambda s: (dx_step(s), 0)),
                  pl.BlockSpec((TAIL_TM, D_MODEL), lambda s: (dx_step(s), 0)),
                  any_spec, vmem],
        out_specs=[pl.BlockSpec((TAIL_TM, D_MODEL), lambda s: (dx_step(s), 0)), vmem, vmem],
        out_shape=[jax.ShapeDtypeStruct((T, D_MODEL), F32), jax.ShapeDtypeStruct(blk_shape, F32),
                   jax.ShapeDtypeStruct(small_g.shape, F32)],
        scratch_shapes=scratch,
        compiler_params=_cparams(("arbitrary",)),
    )(xt, dhs[0], dhs[1], dhs[1], dhs[2], dhs[3], dhs[0], dhs[1], dhs[2], dhs[3], gx1, w_in, small_g)


def _adam_update(grads, params, carried):
    n = len(grads)

    def body(*refs):
        g_refs, p_refs, o_refs = refs[1:1 + n], refs[1 + n:1 + 4 * n], refs[2 + 4 * n:]
        for a in range(n):
            rows = g_refs[a].shape[0]
            cr = REDUCE_ROWS if rows % REDUCE_ROWS == 0 else rows
            flat2 = lambda r: r.at[0] if len(r.shape) == 3 else r
            w_ref, m_ref, v_ref = [flat2(r) for r in p_refs[3 * a:3 * a + 3]]
            go_ref, d_ref, nm_ref, nv_ref = [flat2(r) for r in o_refs[4 * a:4 * a + 4]]

            def chunk(i, carry, cr=cr, g_ref=g_refs[a], w_ref=w_ref, m_ref=m_ref, v_ref=v_ref,
                      go_ref=go_ref, d_ref=d_ref, nm_ref=nm_ref, nv_ref=nv_ref):
                rs = pl.ds(pl.multiple_of(i * cr, cr), cr)
                g = g_ref[rs, :]
                go_ref[rs, :] = g
                d_ref[rs, :], nm_ref[rs, :], nv_ref[rs, :] = _adamw(w_ref[rs, :], g, m_ref[rs, :], v_ref[rs, :])
                return carry
            lax.fori_loop(0, rows // cr, chunk, 0)

    vmem = pl.BlockSpec(memory_space=pltpu.VMEM)
    any_spec = pl.BlockSpec(memory_space=pl.ANY)
    flat = [p for grp in params for p in grp]
    outs = pl.pallas_call(
        body, name="adamw", in_specs=[any_spec] + [vmem] * (4 * n), out_specs=[any_spec] + [vmem] * (4 * n),
        out_shape=[jax.ShapeDtypeStruct(carried.shape, carried.dtype)]
        + [jax.ShapeDtypeStruct(grp[0].shape, F32) for grp in params for _ in range(4)],
        input_output_aliases={0: 0},
        compiler_params=pltpu.CompilerParams(vmem_limit_bytes=VMEM_LIMIT),
    )(carried, *grads, *flat)
    return [outs[1 + 4 * a:5 + 4 * a] for a in range(n)], outs[0]


def _step(x, mem, w_in_s, w_mem_s, w_out_s, b_in, sinks, g, gain, bias, tgt):
    B = x.shape[0]
    T = B * SEQ
    x2 = x.reshape(T, D_MODEL)
    t2 = tgt.reshape(T, D_MODEL)
    rope_inv = _rope_inv()
    lane = jnp.arange(W_A)
    hsum = (lane[:W_B, None] // HEAD == lane[None, :W_B] // HEAD).astype(MXU)
    hrows = (jnp.arange(PICK_ROWS)[:, None] == lane[None, :] // HEAD).astype(MXU)
    me = 4 * lax.axis_index("x") + 2 * lax.axis_index("y") + lax.axis_index("c")

    (w_in_all,) = _gather_weights([w_in_s])
    qkva, qkvb, qc, z, w_in, xt, rope_tab, w_mem_all, w_out_all = _in_proj(
        x2, w_in_all, b_in, rope_inv, [w_mem_s, w_out_s])
    w_mem = w_mem_all.reshape(D_MODEL, 2 * W_C)
    w_out = w_out_all.reshape(D_MIX, D_MODEL)

    o_a, lse_a = _swa_fwd(qkva, sinks)
    olse_b = _dil_fwd(qkvb)
    o_c, lse_c, mkv = _mem_attn_fwd(qc, mem, w_mem)

    gx1, do_a, delta_a, dobb, do_c, delta_c, dz, dw_out, small, loss = _post(
        o_a, olse_b, o_c, z, x2, t2, g, gain, bias, w_out, hsum, hrows)

    dh_c, db_c, dw_mem = _mem_attn_bwd(qc, mkv, do_c, lse_c, delta_c, mem)
    blocks = [dw_mem.reshape(N_DEV, ROWS_PER_DEV, 2 * W_C), dw_out.reshape(N_DEV, ROWS_PER_DEV, D_MODEL)]
    sends = [b.astype(MXU) for b in blocks]
    owns = [lax.dynamic_index_in_dim(b, me, axis=0, keepdims=False) for b in blocks]
    dh_b, db_b, (g_mem, g_out) = _dil_bwd(qkvb, dobb, rope_tab, sends, owns)
    dh_a, db_a, dsink = _swa_bwd(qkva, do_a, lse_a, delta_a, sinks, rope_tab)

    small_g = _pack_small(dict(b_in=jnp.concatenate([db_a[0], db_b[0], db_c[0], small[3]]), sinks=dsink[:, 0],
                               g=small[2], gain=small[0], bias=small[1], loss=loss[0, 0]))
    grad_x, g_in, g_small = _tail(xt, (dh_a, dh_b, dh_c, dz), gx1, w_in, small_g)
    return grad_x.reshape(B, SEQ, D_MODEL), g_in, g_mem, g_out, g_small


def _my_pos():
    return lax.axis_index("x"), lax.axis_index("y"), lax.axis_index("c")


def _gather_weights(shards):
    n_arr = len(shards)

    def body(*refs):
        ins, outs = refs[0:n_arr], refs[n_arr:2 * n_arr]
        send_sems, recv_sems, local_sems = refs[2 * n_arr:]
        x, y, c = _my_pos()
        me, sibling = (x, y, c), (x, y, 1 - c)
        chips = [(1 - x, y), (x, 1 - y), (1 - x, 1 - y)]

        def slot(a, pos):
            return outs[a].at[4 * pos[0] + 2 * pos[1] + pos[2]]

        def copy(a, k, block, to, src=None):
            return pltpu.make_async_remote_copy(
                src_ref=slot(a, block) if src is None else src, dst_ref=slot(a, block),
                send_sem=send_sems.at[a, k], recv_sem=recv_sems.at[a, k],
                device_id=to, device_id_type=MESH)

        mine = [pltpu.make_async_copy(ins[a], slot(a, me), local_sems.at[a]) for a in range(n_arr)]
        for cp in mine:
            cp.start()
        first = []
        for a in range(n_arr):
            first.append(copy(a, 0, me, sibling, src=ins[a]))
            first += [copy(a, 1 + j, me, (*chip, c), src=ins[a]) for j, chip in enumerate(chips)]
        for cp in first:
            cp.start()
        passed = []
        for j, chip in enumerate(chips):
            for a in range(n_arr):
                copy(a, 1 + j, (*chip, c), me).wait_recv()
                fwd = copy(a, 4 + j, (*chip, c), sibling)
                fwd.start()
                passed.append(fwd)
        for a in range(n_arr):
            copy(a, 0, sibling, me).wait_recv()
            for j, chip in enumerate(chips):
                copy(a, 4 + j, (*chip, 1 - c), me).wait_recv()
        for cp in first + passed:
            cp.wait_send()
        for cp in mine:
            cp.wait()

    any_spec = pl.BlockSpec(memory_space=pl.ANY)
    return pl.pallas_call(
        body, name="gather_weights",
        in_specs=[any_spec] * n_arr, out_specs=[any_spec] * n_arr,
        out_shape=[jax.ShapeDtypeStruct((N_DEV,) + s.shape, s.dtype) for s in shards],
        scratch_shapes=[pltpu.SemaphoreType.DMA((n_arr, 7)), pltpu.SemaphoreType.DMA((n_arr, 7)),
                        pltpu.SemaphoreType.DMA((n_arr,))],
    )(*shards)


def _adamw(w, g, m, v):
    m = ADAM_B1 * m + (1.0 - ADAM_B1) * g
    v = ADAM_B2 * v + (1.0 - ADAM_B2) * (g * g)
    m_hat = m / (1.0 - ADAM_B1 ** ADAM_STEP)
    v_hat = v / (1.0 - ADAM_B2 ** ADAM_STEP)
    delta = -ADAM_LR * (m_hat / (jnp.sqrt(v_hat) + ADAM_EPS) + ADAM_WD * w)
    return delta, m, v


_SMALL_SIZES = (("b_in", D_IN), ("g", D_MIX), ("gain", D_MODEL), ("bias", D_MODEL), ("sinks", SWA_Q), ("loss", 1))


def _pack_small(d):
    flat = jnp.concatenate([jnp.reshape(d[k], (-1,)).astype(F32) if k in d else jnp.zeros((n,), F32)
                            for k, n in _SMALL_SIZES])
    flat = jnp.pad(flat, (0, SMALL_ROWS * 128 - flat.shape[0]))
    return flat.reshape(SMALL_ROWS, 128)


def _unpack_small(p):
    flat = p.reshape(-1)
    out, off = {}, 0
    for k, n in _SMALL_SIZES:
        out[k] = flat[off:off + n].reshape(1, n)
        off += n
    return out


def kernel(x, mem, w_in, b_in, w_mem, attn_sinks, g_branch, w_out, ln_gain, ln_bias, loss_target, m_w_in, m_b_in, m_w_mem, m_attn_sinks, m_g_branch, m_w_out, m_ln_gain, m_ln_bias, v_w_in, v_b_in, v_w_mem, v_attn_sinks, v_g_branch, v_w_out, v_ln_gain, v_ln_bias):
    grad_x, g_in, g_mem, g_out, g_small = _step(
        x, mem, w_in[0].astype(MXU), w_mem[0].astype(MXU), w_out[0].astype(MXU), b_in, attn_sinks[0],
        g_branch, ln_gain, ln_bias, loss_target)

    small_w = _pack_small(dict(b_in=b_in, g=g_branch, gain=ln_gain, bias=ln_bias, sinks=attn_sinks))
    small_m = _pack_small(dict(b_in=m_b_in, g=m_g_branch, gain=m_ln_gain, bias=m_ln_bias, sinks=m_attn_sinks))
    small_v = _pack_small(dict(b_in=v_b_in, g=v_g_branch, gain=v_ln_gain, bias=v_ln_bias, sinks=v_attn_sinks))
    grads = [g_in, g_mem, g_out, g_small]
    params = [(w_in, m_w_in, v_w_in), (w_mem, m_w_mem, v_w_mem), (w_out, m_w_out, v_w_out),
              (small_w, small_m, small_v)]
    res, grad_x = _adam_update(grads, params, grad_x)
    big = res[:3]
    sm = [_unpack_small(r) for r in res[3]]

    def group(i):
        return (big[0][i], sm[i]["b_in"], big[1][i], sm[i]["sinks"], sm[i]["g"], big[2][i],
                sm[i]["gain"], sm[i]["bias"])

    loss = sm[0]["loss"].reshape(())
    return (loss, grad_x, *group(0), *group(1), *group(2), *group(3))
```

```python
import jax
import jax.numpy as jnp
from jax import lax
from jax.experimental import pallas as pl
from jax.experimental.pallas import tpu as pltpu

F32 = jnp.float32
MXU = jnp.bfloat16

D_MODEL = 1024
SEQ = 2048
HEAD = 64
BLK = 128
SWA_Q, SWA_KV = 8, 2
DIL_H = 4
MEM_H = 4
MEM_LEN = 256
W_A, W_KVA, W_B, W_C = 512, 128, 256, 256
D_MIX = 1024
D_IN = 2816
N_DEV = 8
COLS_PER_DEV = D_IN // N_DEV
ROWS_PER_DEV = D_MODEL // N_DEV
ROPE_THETA = 10000.0
LN_EPS = 1e-5
RMS_EPS = 1e-6
ALPHA = 2.0 ** 0.25
Q_SCALE = HEAD ** -0.5
NEG = -1e30
SMALL_ROWS = 48
VMEM_LIMIT = 56 * 1024 * 1024

ADAM_LR = 0.001
ADAM_B1 = 0.9
ADAM_B2 = 0.999
ADAM_EPS = 1e-08
ADAM_WD = 0.01
ADAM_STEP = 10

MESH = pl.DeviceIdType.MESH


def _cparams(sem=None):
    return pltpu.CompilerParams(dimension_semantics=sem, vmem_limit_bytes=VMEM_LIMIT)


def _dot(a, b):
    return jnp.dot(a, b, preferred_element_type=F32)


def _dot_nt(a, b):
    return lax.dot_general(a, b, (((1,), (1,)), ((), ())), preferred_element_type=F32)


def _dot_t0(a, b):
    return lax.dot_general(a, b, (((0,), (0,)), ((), ())), preferred_element_type=F32)


def _dot_tn(a, b):
    return jnp.dot(a.T.astype(MXU), b, preferred_element_type=F32)


def _rope(t, tab, sign):
    cos, sa, sb = tab
    outs = []
    for c in range(t.shape[1] // 128):
        tc = t[:, c * 128:(c + 1) * 128]
        r = pltpu.roll(tc, 96, 1) * sa + pltpu.roll(tc, 32, 1) * sb
        outs.append(tc * cos + r if sign > 0 else tc * cos - r)
    return outs[0] if len(outs) == 1 else jnp.concatenate(outs, axis=1)


def _rope_inv():
    inv = ROPE_THETA ** (-jnp.arange(0, HEAD, 2, dtype=F32) / HEAD)
    return jnp.tile(inv, 2 * 128 // HEAD)[None, :]


def _rope_tab(pos0, rows, inv):
    pos = (lax.broadcasted_iota(jnp.int32, (rows, 128), 0) + pos0).astype(F32)
    ang = pos * inv
    cos, sin = jnp.cos(ang), jnp.sin(ang)
    first = lax.broadcasted_iota(jnp.int32, (rows, 128), 1) % HEAD < HEAD // 2
    return cos, jnp.where(first, -sin, 0.0), jnp.where(first, 0.0, sin)


def _dev_coords(j):
    return (j >> 2, (j >> 1) & 1, j & 1)


def _in_proj(x2, w_all, b_in, rope_inv, late_shards):
    T = x2.shape[0]
    tm = 512
    n_late = len(late_shards)

    def body(x_ref, wall_ref, b_ref, inv_ref, *rest):
        late_in, rest = rest[:n_late], rest[n_late:]
        qkva_ref, qkvb_ref, qc_ref, z_ref, w_ref, xt_ref, tabo_ref = rest[:7]
        late_out = rest[7:7 + n_late]
        send_sems, recv_sems, local_sems = rest[7 + n_late:]
        step, last = pl.program_id(0), pl.num_programs(0) - 1
        x, y, c = _my_pos()
        me = 4 * x + 2 * y + c

        def to_peer(a, j):
            return pltpu.make_async_remote_copy(
                src_ref=late_in[a], dst_ref=late_out[a].at[me], send_sem=send_sems.at[a, j],
                recv_sem=recv_sems.at[a, me], device_id=_dev_coords(j), device_id_type=MESH)

        def from_peer(a, m):
            return pltpu.make_async_remote_copy(
                src_ref=late_out[a].at[m], dst_ref=late_out[a].at[m], send_sem=send_sems.at[a, m],
                recv_sem=recv_sems.at[a, m], device_id=_dev_coords(m), device_id_type=MESH)

        def mine(a):
            return pltpu.make_async_copy(late_in[a], late_out[a].at[me], local_sems.at[a])

        @pl.when(step == 0)
        def _():
            for a in range(n_late):
                mine(a).start()
                for j in range(N_DEV):
                    pl.when(me != j)(to_peer(a, j).start)
            for j in range(N_DEV):
                w_ref[:, j * COLS_PER_DEV:(j + 1) * COLS_PER_DEV] = wall_ref[j]

        xb = x_ref[...].astype(MXU)
        xt_ref[...] = x_ref[...].T.astype(MXU)
        tab = _rope_tab((step % nt) * tm, tm, inv_ref[...])
        for j in range(3):
            tabo_ref[:, j * 128:(j + 1) * 128] = tab[j]

        def seg(c0, c1):
            return _dot(xb, w_ref[:, c0:c1]) + b_ref[:, c0:c1]

        qa = (_rope(seg(0, 512), tab, 1) * Q_SCALE).astype(MXU)
        for c in range(SWA_Q // 2):
            qkva_ref[c] = qa[:, c * 128:(c + 1) * 128]
        lo = lax.broadcasted_iota(jnp.int32, (tm, 128), 1) < HEAD
        for j, t in enumerate((_rope(seg(512, 640), tab, 1), seg(640, 768))):
            other = pltpu.roll(t, HEAD, 1)
            qkva_ref[4 + 2 * j] = jnp.where(lo, t, other).astype(MXU)
            qkva_ref[5 + 2 * j] = jnp.where(lo, other, t).astype(MXU)
        qkvb = (_rope(seg(768, 1024), tab, 1) * Q_SCALE, _rope(seg(1024, 1280), tab, 1), seg(1280, 1536))
        for j, t in enumerate(qkvb):
            for c in range(2):
                qkvb_ref[2 * j + c] = t[:, c * 128:(c + 1) * 128]
        qc = (seg(1536, 1792) * Q_SCALE).astype(MXU)
        for c in range(MEM_H // 2):
            qc_ref[c] = qc[:, c * 128:(c + 1) * 128]
        z_ref[...] = seg(1792, 2816)

        @pl.when(step == last)
        def _():
            for a in range(n_late):
                mine(a).wait()
                for m in range(N_DEV):
                    pl.when(me != m)(from_peer(a, m).wait_recv)
                for j in range(N_DEV):
                    pl.when(me != j)(to_peer(a, j).wait_send)

    nt = SEQ // tm
    any_spec = pl.BlockSpec(memory_space=pl.ANY)
    chunked = lambda n: pl.BlockSpec((None, n, tm, 128), lambda i: (i // nt, 0, i % nt, 0))
    return pl.pallas_call(
        body, name="in_proj_fwd",
        grid=(T // tm,),
        in_specs=[pl.BlockSpec((tm, D_MODEL), lambda i: (i, 0)),
                  pl.BlockSpec((N_DEV, D_MODEL, COLS_PER_DEV), lambda i: (0, 0, 0)),
                  pl.BlockSpec((1, D_IN), lambda i: (0, 0)),
                  pl.BlockSpec((1, 128), lambda i: (0, 0))] + [any_spec] * n_late,
        out_specs=[chunked(SWA_CHUNKS), chunked(6), chunked(MEM_H // 2),
                   pl.BlockSpec((tm, D_MIX), lambda i: (i, 0)),
                   pl.BlockSpec((D_MODEL, D_IN), lambda i: (0, 0)),
                   pl.BlockSpec((D_MODEL, tm), lambda i: (0, i)),
                   pl.BlockSpec((tm, 384), lambda i: (i, 0))] + [any_spec] * n_late,
        out_shape=[jax.ShapeDtypeStruct((T // SEQ, SWA_CHUNKS, SEQ, 128), MXU),
                   jax.ShapeDtypeStruct((T // SEQ, 6, SEQ, 128), F32),
                   jax.ShapeDtypeStruct((T // SEQ, MEM_H // 2, SEQ, 128), MXU),
                   jax.ShapeDtypeStruct((T, D_MIX), F32),
                   jax.ShapeDtypeStruct((D_MODEL, D_IN), w_all.dtype),
                   jax.ShapeDtypeStruct((D_MODEL, T), MXU),
                   jax.ShapeDtypeStruct((T, 384), F32)]
        + [jax.ShapeDtypeStruct((N_DEV,) + s.shape, s.dtype) for s in late_shards],
        scratch_shapes=[pltpu.SemaphoreType.DMA((n_late, N_DEV)), pltpu.SemaphoreType.DMA((n_late, N_DEV)),
                        pltpu.SemaphoreType.DMA((n_late,))],
        compiler_params=_cparams(("arbitrary",)),
    )(x2, w_all, b_in, rope_inv, *late_shards)


CHAIN = 4


def _band_bias(max_dist):
    kj = lax.broadcasted_iota(jnp.int32, (2 * BLK, BLK), 0)
    qi = lax.broadcasted_iota(jnp.int32, (2 * BLK, BLK), 1)
    dist = qi + BLK - kj
    band = jnp.where((dist >= 0) & (dist <= max_dist), 0.0, NEG).astype(F32)
    k1 = lax.broadcasted_iota(jnp.int32, (BLK, BLK), 0)
    q1 = lax.broadcasted_iota(jnp.int32, (BLK, BLK), 1)
    first = jnp.where((q1 - k1 >= 0) & (q1 - k1 <= max_dist), 0.0, NEG).astype(F32)
    return jnp.concatenate([band] * CHAIN, axis=1), jnp.concatenate([first] * CHAIN, axis=1)


def _lanes(parts):
    return jnp.concatenate(parts, axis=1)


PICK_ROWS = 16


def _stack_pair(t):
    lo = (lax.broadcasted_iota(jnp.int32, t.shape, 1) < HEAD).astype(F32)
    return jnp.concatenate([t * lo, t * (1.0 - lo)], axis=0).astype(MXU)


def _pair_rows(x, n):
    lo = lax.broadcasted_iota(jnp.int32, (n, 128), 1) < HEAD
    return jnp.where(lo, x[0:n], x[n:2 * n])


def _split3(t):
    if MXU == F32:
        return (t,)
    hi = t.astype(MXU)
    r = t - hi.astype(F32)
    mid = r.astype(MXU)
    return hi, mid, (r - mid.astype(F32)).astype(MXU)


def _interleave(tiles):
    tiles = list(tiles)
    while tiles:
        for t in list(tiles):
            try:
                next(t)
            except StopIteration:
                tiles.remove(t)


def _softmax_cols(sT, sinkrow=None):
    m = jnp.max(sT, axis=0, keepdims=True)
    if sinkrow is not None:
        m = jnp.maximum(m, sinkrow)
    pT = jnp.exp(sT - m)
    l = jnp.sum(pT, axis=0, keepdims=True)
    if sinkrow is not None:
        l = l + jnp.exp(sinkrow - m)
    return (pT * (1.0 / l)).astype(MXU), m + jnp.log(l)


SWA_CHUNKS = 8
SWA_UNROLL = 3
FWD_UNROLL = 5
N_QBLK = SEQ // BLK


def _swa_fwd(qkva, sinks):
    B = qkva.shape[0]
    G = SWA_Q // SWA_KV

    def body(sink_ref, qkv_ref, o_ref, lse_ref):
        band, first = _band_bias(BLK - 1)
        sinkrows = [_lanes([jnp.full((1, BLK), sink_ref[G * hk + j], F32) for j in range(G)])
                    for hk in range(SWA_KV)]

        def tile(hk, blk, rows_q, rows_k, bias):
            nk = bias.shape[0]
            k2 = _stack_pair(qkv_ref.at[4 + hk][rows_k, :])
            sT = []
            for c in (2 * hk, 2 * hk + 1):
                s2 = _dot_nt(k2, qkv_ref.at[c][rows_q, :])
                sT += [s2[0:nk], s2[nk:2 * nk]]
            yield
            pnT, lse = _softmax_cols(_lanes(sT) + bias, sinkrows[hk])
            yield
            v2 = _stack_pair(qkv_ref.at[6 + hk][rows_k, :])
            for j, c in enumerate((2 * hk, 2 * hk + 1)):
                p2 = jnp.concatenate([pnT[:, 2 * j * BLK:(2 * j + 1) * BLK],
                                      pnT[:, (2 * j + 1) * BLK:(2 * j + 2) * BLK]], axis=0)
                o_ref.at[c][rows_q, :] = _dot_t0(p2, v2)
            for j in range(G):
                lse_ref.at[blk][G * hk + j:G * hk + j + 1, :] = lse[:, j * BLK:(j + 1) * BLK]

        def tiles_at(i):
            r0 = pl.multiple_of(i * BLK, BLK)
            rk = pl.multiple_of(i * BLK - BLK, BLK)
            return [tile(hk, i, pl.ds(r0, BLK), pl.ds(rk, 2 * BLK), band) for hk in range(SWA_KV)]

        _interleave([tile(hk, 0, pl.ds(0, BLK), pl.ds(0, BLK), first) for hk in range(SWA_KV)])

        def loop(j, carry):
            _interleave([t for u in range(FWD_UNROLL) for t in tiles_at(1 + j * FWD_UNROLL + u)])
            return carry
        lax.fori_loop(0, (N_QBLK - 1) // FWD_UNROLL, loop, 0)

    return pl.pallas_call(
        body, name="swa_fwd", grid=(B,),
        in_specs=[pl.BlockSpec(memory_space=pltpu.SMEM),
                  pl.BlockSpec((None, SWA_CHUNKS, SEQ, 128), lambda b: (b, 0, 0, 0))],
        out_specs=[pl.BlockSpec((None, SWA_Q // 2, SEQ, 128), lambda b: (b, 0, 0, 0)),
                   pl.BlockSpec((None, N_QBLK, 8, 128), lambda b: (b, 0, 0, 0))],
        out_shape=[jax.ShapeDtypeStruct((B, SWA_Q // 2, SEQ, 128), F32),
                   jax.ShapeDtypeStruct((B, N_QBLK, 8, 128), F32)],
        compiler_params=_cparams(("arbitrary",)),
    )(sinks, qkva)


EP_ROWS = BLK


def _dh_tile(part_of, dh_ref, db_ref, tab_ref, blk, live=None):
    rs = pl.ds(pl.multiple_of(blk * EP_ROWS, EP_ROWS), EP_ROWS)
    tab = None if tab_ref is None else tuple(tab_ref[rs, j * 128:(j + 1) * 128] for j in range(3))
    part = part_of(rs, tab)
    yield
    dh_ref[rs, :] = part.astype(dh_ref.dtype)
    psum = jnp.sum(part, axis=0, keepdims=True)
    db_ref[0:1, :] += psum if live is None else psum * live
    yield


def _dh_tiles_behind(part_of, dh_ref, db_ref, tab_ref, j, unroll):
    live = (j > 0).astype(F32)
    return [_dh_tile(part_of, dh_ref, db_ref, tab_ref, jnp.where(j > 0, unroll * (j - 1) + u, 0), live)
            for u in range(unroll)]


def _swa_bwd(qkva, do, lse, delta, sinks, rope_tab):
    B = qkva.shape[0]
    G = SWA_Q // SWA_KV

    def body(sink_ref, qkv_ref, do_ref, lse_ref, delta_ref, rtab_ref, dh_ref, db_ref, dsink_ref, dq_ref):
        band, first = _band_bias(BLK - 1)
        sinkrows = [_lanes([jnp.full((1, BLK), sink_ref[G * hk + j], F32) for j in range(G)])
                    for hk in range(SWA_KV)]

        @pl.when(pl.program_id(0) == 0)
        def _():
            dsink_ref[...] = jnp.zeros_like(dsink_ref)
            db_ref[...] = jnp.zeros_like(db_ref)
        for c in range(4, SWA_CHUNKS):
            dq_ref[c] = jnp.zeros((SEQ, 128), F32)

        def tile(hk, blk, rows_q, rows_k, bias, accs):
            nk = bias.shape[0]
            k2 = _stack_pair(qkv_ref.at[4 + hk][rows_k, :])
            v2 = _stack_pair(qkv_ref.at[6 + hk][rows_k, :])
            qcs, docs, sT, dpT = [], [], [], []
            for c in (2 * hk, 2 * hk + 1):
                qc, doc = qkv_ref.at[c][rows_q, :], do_ref.at[c][rows_q, :]
                s2, dp2 = _dot_nt(k2, qc), _dot_nt(v2, doc)
                sT += [s2[0:nk], s2[nk:2 * nk]]
                dpT += [dp2[0:nk], dp2[nk:2 * nk]]
                qcs.append(qc)
                docs.append(doc)
            lse_r = _lanes([lse_ref.at[blk][h:h + 1, :] for h in range(G * hk, G * hk + G)])
            delta_r = _lanes([delta_ref.at[blk][h:h + 1, :] for h in range(G * hk, G * hk + G)])
            yield
            pT = jnp.exp(_lanes(sT) + bias - lse_r)
            dsT = pT * (_lanes(dpT) - delta_r)
            dsb, pb = dsT.astype(MXU), pT.astype(MXU)
            accs[hk] = accs[hk] - jnp.exp(sinkrows[hk] - lse_r) * delta_r
            yield
            dk2 = dv2 = None
            for j, c in enumerate((2 * hk, 2 * hk + 1)):
                q0, q1 = slice(2 * j * BLK, (2 * j + 1) * BLK), slice((2 * j + 1) * BLK, (2 * j + 2) * BLK)
                ds2 = jnp.concatenate([dsb[:, q0], dsb[:, q1]], axis=0)
                p2 = jnp.concatenate([pb[:, q0], pb[:, q1]], axis=0)
                dq_ref.at[c][rows_q, :] = _dot_t0(ds2, k2)
                dk2 = _dot(ds2, qcs[j]) if dk2 is None else dk2 + _dot(ds2, qcs[j])
                dv2 = _dot(p2, docs[j]) if dv2 is None else dv2 + _dot(p2, docs[j])
            dq_ref.at[4 + hk][rows_k, :] += _pair_rows(dk2, nk)
            dq_ref.at[6 + hk][rows_k, :] += _pair_rows(dv2, nk)

        def run(tiles_of, accs):
            accs = list(accs)
            _interleave(tiles_of(accs))
            return tuple(accs)

        zero = jnp.zeros((1, G * BLK), F32)
        accs = run(lambda a: [tile(hk, 0, pl.ds(0, BLK), pl.ds(0, BLK), first, a) for hk in range(SWA_KV)],
                   (zero,) * SWA_KV)

        def part_of(rs, tab):
            lo = lax.broadcasted_iota(jnp.int32, (EP_ROWS, 128), 1) < HEAD

            def kv_grad(c):
                g0, g1 = dq_ref.at[c][rs, :], dq_ref.at[c + 1][rs, :]
                return jnp.where(lo, g0 + pltpu.roll(g0, HEAD, 1), g1 + pltpu.roll(g1, HEAD, 1))
            dq = _lanes([dq_ref.at[c][rs, :] for c in range(SWA_Q // 2)])
            return _lanes([_rope(dq, tab, -1) * Q_SCALE, _rope(kv_grad(4), tab, -1), kv_grad(6)])

        trips = (N_QBLK - 1) // SWA_UNROLL

        def loop(j, accs):
            def tiles_of(a):
                out = []
                for u in range(SWA_UNROLL):
                    i = 1 + j * SWA_UNROLL + u
                    r0 = pl.multiple_of(i * BLK, BLK)
                    rk = pl.multiple_of(i * BLK - BLK, BLK)
                    out += [tile(hk, i, pl.ds(r0, BLK), pl.ds(rk, 2 * BLK), band, a) for hk in range(SWA_KV)]
                return out + _dh_tiles_behind(part_of, dh_ref, db_ref, rtab_ref, j, SWA_UNROLL)
            return run(tiles_of, accs)
        accs = lax.fori_loop(0, trips, loop, accs)
        _interleave([_dh_tile(part_of, dh_ref, db_ref, rtab_ref, blk)
                     for blk in range(SWA_UNROLL * (trips - 1), N_QBLK)])
        for hk in range(SWA_KV):
            for j in range(G):
                tot = jnp.sum(accs[hk][:, j * BLK:(j + 1) * BLK], axis=1, keepdims=True)
                dsink_ref[G * hk + j:G * hk + j + 1, :] += jnp.broadcast_to(tot, (1, 128))

    stat = pl.BlockSpec((None, N_QBLK, 8, 128), lambda b: (b, 0, 0, 0))
    return pl.pallas_call(
        body, name="swa_bwd", grid=(B,),
        in_specs=[pl.BlockSpec(memory_space=pltpu.SMEM),
                  pl.BlockSpec((None, SWA_CHUNKS, SEQ, 128), lambda b: (b, 0, 0, 0)),
                  pl.BlockSpec((None, SWA_Q // 2, SEQ, 128), lambda b: (b, 0, 0, 0)), stat, stat,
                  pl.BlockSpec((SEQ, 384), lambda b: (0, 0))],
        out_specs=[pl.BlockSpec((SEQ, W_A + 2 * W_KVA), lambda b: (b, 0)),
                   pl.BlockSpec((8, W_A + 2 * W_KVA), lambda b: (0, 0)),
                   pl.BlockSpec((8, 128), lambda b: (0, 0))],
        out_shape=[jax.ShapeDtypeStruct((B * SEQ, W_A + 2 * W_KVA), MXU),
                   jax.ShapeDtypeStruct((8, W_A + 2 * W_KVA), F32), jax.ShapeDtypeStruct((8, 128), F32)],
        scratch_shapes=[pltpu.VMEM((SWA_CHUNKS, SEQ, 128), F32)],
        compiler_params=_cparams(("arbitrary",)),
    )(sinks, qkva, do, lse, delta, rope_tab)


DILATIONS = (1, 4, 16)
DIL_PAIRS_H = DIL_H // 2


def _stream_rows(d, r, i, n):
    if d == 1:
        return pl.ds(pl.multiple_of(i * BLK, BLK), n)
    return pl.ds(r + i * (BLK * d), n, stride=d)


def _spread_matrix():
    row = lax.broadcasted_iota(jnp.int32, (PICK_ROWS, 128), 0)
    lane = lax.broadcasted_iota(jnp.int32, (PICK_ROWS, 128), 1)
    return ((row < 6) & ((row % 2 == 1) == (lane >= HEAD))).astype(MXU)


def _lanes_to_tokens(v0, v1, spread):
    n = v0.shape[1]
    row = lax.broadcasted_iota(jnp.int32, (PICK_ROWS, n), 0)
    a = jnp.zeros((PICK_ROWS, n), F32)
    for i, (p0, p1) in enumerate(zip(_split3(v0), _split3(v1))):
        a = jnp.where(row == 2 * i, p0.astype(F32), a)
        a = jnp.where(row == 2 * i + 1, p1.astype(F32), a)
    return _dot_t0(a.astype(MXU), spread)


def _tokens_to_lanes(t):
    r = t.T
    return r[0:1, :], r[HEAD:HEAD + 1, :]


DIL_UNROLL = 3


def _dil_schedule(body_first, body_next, unroll, behind=None, wide=1):
    for p, d in sorted(enumerate(DILATIONS), key=lambda pd: -pd[1]):
        nblk = SEQ // d // BLK
        if d == 1:
            _interleave([body_first(p, d, 0)])
            def loop(j, c, p=p, d=d):
                _interleave([body_next(p, d, 0, 1 + unroll * j + u) for u in range(unroll)]
                            + (behind(j) if behind else []))
                return c
            lax.fori_loop(0, (nblk - 1) // unroll, loop, 0)
        elif nblk > 1:
            def loop(j, c, p=p, d=d, nblk=nblk):
                _interleave([t for u in range(wide) for t in
                             [body_first(p, d, wide * j + u)] + [body_next(p, d, wide * j + u, i) for i in range(1, nblk)]])
                return c
            lax.fori_loop(0, d // wide, loop, 0)
        else:
            def loop(j, c, p=p, d=d):
                _interleave([body_first(p, d, 4 * wide * j + u) for u in range(4 * wide)])
                return c
            lax.fori_loop(0, d // (4 * wide), loop, 0)


def _dil_fwd(qkvb):
    B = qkvb.shape[0]

    def body(qkv_ref, o_ref):
        band, first = _band_bias(BLK)
        spread = _spread_matrix()

        def block(p, d, rows_q, rows_k, bias):
            nk = bias.shape[0]
            sT = []
            for c in range(DIL_PAIRS_H):
                qc = qkv_ref.at[c][rows_q, :].astype(MXU)
                s2 = _dot_nt(_stack_pair(qkv_ref.at[DIL_PAIRS_H + c][rows_k, :]), qc)
                sT += [s2[0:nk], s2[nk:2 * nk]]
            yield
            sT = _lanes(sT) + bias
            m = jnp.max(sT, axis=0, keepdims=True)
            pT = jnp.exp(sT - m)
            l = jnp.sum(pT, axis=0, keepdims=True)
            pnT = (pT * (1.0 / l)).astype(MXU)
            lse = m + jnp.log(l)
            yield
            for c in range(DIL_PAIRS_H):
                q0, q1 = slice(2 * c * BLK, (2 * c + 1) * BLK), slice((2 * c + 1) * BLK, (2 * c + 2) * BLK)
                p2 = jnp.concatenate([pnT[:, q0], pnT[:, q1]], axis=0)
                o_ref.at[p, c][rows_q, :] = _dot_t0(p2, _stack_pair(qkv_ref.at[2 * DIL_PAIRS_H + c][rows_k, :]))
                o_ref.at[p, DIL_PAIRS_H + c][rows_q, :] = _lanes_to_tokens(lse[:, q0], lse[:, q1], spread)

        def body_first(p, d, r):
            rows = _stream_rows(d, r, 0, BLK)
            return block(p, d, rows, rows, first)

        def body_next(p, d, r, i):
            return block(p, d, _stream_rows(d, r, i, BLK), _stream_rows(d, r, i - 1, 2 * BLK), band)

        _dil_schedule(body_first, body_next, FWD_UNROLL, wide=2)

    return pl.pallas_call(
        body, name="dil_fwd", grid=(B,),
        in_specs=[pl.BlockSpec((None, 6, SEQ, 128), lambda b: (b, 0, 0, 0))],
        out_specs=pl.BlockSpec((None, 3, 4, SEQ, 128), lambda b: (b, 0, 0, 0, 0)),
        out_shape=jax.ShapeDtypeStruct((B, 3, 4, SEQ, 128), F32),
        compiler_params=_cparams(("arbitrary",)),
    )(qkvb)


def _reduce_scatter_ops(send_refs, land_refs, send_sems, recv_sems):
    x, y, c = _my_pos()
    me = 4 * x + 2 * y + c
    n = len(send_refs)

    def to_peer(a, j):
        return pltpu.make_async_remote_copy(
            src_ref=send_refs[a].at[j], dst_ref=land_refs[a].at[me], send_sem=send_sems.at[a, j],
            recv_sem=recv_sems.at[a, me], device_id=_dev_coords(j), device_id_type=MESH)

    def from_peer(a, m):
        return pltpu.make_async_remote_copy(
            src_ref=land_refs[a].at[m], dst_ref=land_refs[a].at[m], send_sem=send_sems.at[a, m],
            recv_sem=recv_sems.at[a, m], device_id=_dev_coords(m), device_id_type=MESH)

    def start():
        for j in range(N_DEV):
            @pl.when(me != j)
            def _(j=j):
                for a in range(n):
                    to_peer(a, j).start()
        for a in range(n):
            land_refs[a][me] = jnp.zeros(land_refs[a].shape[1:], land_refs[a].dtype)

    def finish(own_refs, out_refs):
        for m in range(N_DEV):
            @pl.when(me != m)
            def _(m=m):
                for a in range(n):
                    from_peer(a, m).wait_recv()
        for j in range(N_DEV):
            @pl.when(me != j)
            def _(j=j):
                for a in range(n):
                    to_peer(a, j).wait_send()
        for a in range(n):
            def chunk(i, carry, a=a):
                rs = pl.ds(pl.multiple_of(i * REDUCE_ROWS, REDUCE_ROWS), REDUCE_ROWS)
                g = own_refs[a][rs, :]
                for m in range(N_DEV):
                    g = g + land_refs[a][m, rs, :].astype(F32)
                out_refs[a][rs, :] = g
                return carry
            lax.fori_loop(0, own_refs[a].shape[0] // REDUCE_ROWS, chunk, 0)

    return start, finish


def _dil_bwd(qkvb, dobb, rope_tab, sends, owns):
    B = qkvb.shape[0]
    n_rs = len(sends)

    def body(qkv_ref, dob_ref, rtab_ref, *rest):
        send_refs, own_refs = rest[:n_rs], rest[n_rs:2 * n_rs]
        dh_ref, db_ref = rest[2 * n_rs:2 * n_rs + 2]
        out_refs = rest[2 * n_rs + 2:3 * n_rs + 2]
        dq_ref = rest[3 * n_rs + 2]
        land_refs = rest[3 * n_rs + 3:4 * n_rs + 3]
        send_sems, recv_sems = rest[4 * n_rs + 3:]
        rs_start, rs_finish = _reduce_scatter_ops(send_refs, land_refs, send_sems, recv_sems)
        pl.when(pl.program_id(0) == 0)(rs_start)

        band, first = _band_bias(BLK)
        dq_ref[...] = jnp.zeros_like(dq_ref)

        @pl.when(pl.program_id(0) == 0)
        def _():
            db_ref[...] = jnp.zeros_like(db_ref)

        def block(p, d, rows_q, rows_k, bias):
            nk = bias.shape[0]
            lo = lax.broadcasted_iota(jnp.int32, (nk, 128), 1) < HEAD
            qcs, docs, k2s, sT, dpT, lse, delta = [], [], [], [], [], [], []
            for c in range(DIL_PAIRS_H):
                qc = qkv_ref.at[c][rows_q, :].astype(MXU)
                doc = dob_ref.at[c][rows_q, :].astype(MXU)
                k2 = _stack_pair(qkv_ref.at[DIL_PAIRS_H + c][rows_k, :])
                s2 = _dot_nt(k2, qc)
                dp2 = _dot_nt(_stack_pair(qkv_ref.at[2 * DIL_PAIRS_H + c][rows_k, :]), doc)
                sT += [s2[0:nk], s2[nk:2 * nk]]
                dpT += [dp2[0:nk], dp2[nk:2 * nk]]
                lse += _tokens_to_lanes(dob_ref.at[DIL_PAIRS_H + c][rows_q, :])
                delta += _tokens_to_lanes(dob_ref.at[2 * DIL_PAIRS_H + c][rows_q, :])
                qcs.append(qc)
                docs.append(doc)
                k2s.append(k2)
            yield
            pT = jnp.exp(_lanes(sT) + bias - _lanes(lse))
            dsT = pT * (_lanes(dpT) - _lanes(delta))
            dsb, pb = dsT.astype(MXU), pT.astype(MXU)
            yield
            for c in range(DIL_PAIRS_H):
                q0, q1 = slice(2 * c * BLK, (2 * c + 1) * BLK), slice((2 * c + 1) * BLK, (2 * c + 2) * BLK)
                ds2 = jnp.concatenate([dsb[:, q0], dsb[:, q1]], axis=0)
                p2 = jnp.concatenate([pb[:, q0], pb[:, q1]], axis=0)
                dq_ref.at[c][rows_q, :] += _dot_t0(ds2, k2s[c])
                dk2, dv2 = _dot(ds2, qcs[c]), _dot(p2, docs[c])
                dq_ref.at[DIL_PAIRS_H + c][rows_k, :] += jnp.where(lo, dk2[0:nk], dk2[nk:2 * nk])
                dq_ref.at[2 * DIL_PAIRS_H + c][rows_k, :] += jnp.where(lo, dv2[0:nk], dv2[nk:2 * nk])

        def body_first(p, d, r):
            rows = _stream_rows(d, r, 0, BLK)
            return block(p, d, rows, rows, first)

        def body_next(p, d, r, i):
            return block(p, d, _stream_rows(d, r, i, BLK), _stream_rows(d, r, i - 1, 2 * BLK), band)

        def part_of(rs, tab):
            q, k, v = [_lanes([dq_ref.at[2 * j][rs, :], dq_ref.at[2 * j + 1][rs, :]]) for j in range(3)]
            return _lanes([_rope(q, tab, -1) * Q_SCALE, _rope(k, tab, -1), v])

        _dil_schedule(body_first, body_next, DIL_UNROLL,
                      lambda j: _dh_tiles_behind(part_of, dh_ref, db_ref, rtab_ref, j, DIL_UNROLL))
        _interleave([_dh_tile(part_of, dh_ref, db_ref, rtab_ref, blk)
                     for blk in range(DIL_UNROLL * ((N_QBLK - 1) // DIL_UNROLL - 1), N_QBLK)])

        @pl.when(pl.program_id(0) == pl.num_programs(0) - 1)
        def _():
            rs_finish(own_refs, out_refs)

    spec = pl.BlockSpec((None, 6, SEQ, 128), lambda b: (b, 0, 0, 0))
    any_spec = pl.BlockSpec(memory_space=pl.ANY)
    vmem = pl.BlockSpec(memory_space=pltpu.VMEM)
    outs = pl.pallas_call(
        body, name="dil_bwd", grid=(B,),
        in_specs=[spec, spec, pl.BlockSpec((SEQ, 384), lambda b: (0, 0))] + [any_spec] * n_rs + [vmem] * n_rs,
        out_specs=[pl.BlockSpec((SEQ, 3 * W_B), lambda b: (b, 0)), pl.BlockSpec((8, 3 * W_B), lambda b: (0, 0))]
        + [vmem] * n_rs,
        out_shape=[jax.ShapeDtypeStruct((B * SEQ, 3 * W_B), MXU), jax.ShapeDtypeStruct((8, 3 * W_B), F32)]
        + [jax.ShapeDtypeStruct(o.shape, F32) for o in owns],
        scratch_shapes=[pltpu.VMEM((6, SEQ, 128), F32)] + [pltpu.VMEM(s.shape, s.dtype) for s in sends]
        + [pltpu.SemaphoreType.DMA((n_rs, N_DEV)), pltpu.SemaphoreType.DMA((n_rs, N_DEV))],
        compiler_params=_cparams(("arbitrary",)),
    )(qkvb, dobb, rope_tab, *sends, *owns)
    return outs[0], outs[1], outs[2:]


MEM_UNROLL = 4
MEM_PAIRS = MEM_H // 2


def _mem_attn_fwd(qc, mem, w_mem):
    B = qc.shape[0]

    def body(q_ref, mem_ref, w_ref, o_ref, lse_ref, mkv_ref, k2_ref, v2_ref):
        mkv = _dot(mem_ref[...].astype(MXU), w_ref[...])
        mkv_ref[...] = mkv.astype(MXU)
        for c in range(MEM_PAIRS):
            k2_ref[c] = _stack_pair(mkv[:, c * 128:(c + 1) * 128])
            v2_ref[c] = _stack_pair(mkv[:, W_C + c * 128:W_C + (c + 1) * 128])
        lse_ref[...] = jnp.zeros_like(lse_ref)

        def tile(blk):
            rows = pl.ds(pl.multiple_of(blk * BLK, BLK), BLK)
            sT = []
            for c in range(MEM_PAIRS):
                s2 = _dot_nt(k2_ref[c], q_ref.at[c][rows, :])
                sT += [s2[0:MEM_LEN], s2[MEM_LEN:2 * MEM_LEN]]
            yield
            pnT, lse = _softmax_cols(_lanes(sT))
            yield
            for c in range(MEM_PAIRS):
                p2 = jnp.concatenate([pnT[:, 2 * c * BLK:(2 * c + 1) * BLK],
                                      pnT[:, (2 * c + 1) * BLK:(2 * c + 2) * BLK]], axis=0)
                o_ref.at[c][rows, :] = _dot_t0(p2, v2_ref[c])
            for h in range(MEM_H):
                lse_ref.at[blk][h:h + 1, :] = lse[:, h * BLK:(h + 1) * BLK]

        def loop(j, carry):
            _interleave([tile(j * 2 * MEM_UNROLL + u) for u in range(2 * MEM_UNROLL)])
            return carry
        lax.fori_loop(0, N_QBLK // (2 * MEM_UNROLL), loop, 0)

    return pl.pallas_call(
        body, name="mem_attn_fwd", grid=(B,),
        in_specs=[pl.BlockSpec((None, MEM_PAIRS, SEQ, 128), lambda b: (b, 0, 0, 0)),
                  pl.BlockSpec((None, MEM_LEN, D_MODEL), lambda b: (b, 0, 0)),
                  pl.BlockSpec((D_MODEL, 2 * W_C), lambda b: (0, 0))],
        out_specs=[pl.BlockSpec((None, MEM_PAIRS, SEQ, 128), lambda b: (b, 0, 0, 0)),
                   pl.BlockSpec((None, N_QBLK, 8, 128), lambda b: (b, 0, 0, 0)),
                   pl.BlockSpec((None, MEM_LEN, 2 * W_C), lambda b: (b, 0, 0))],
        out_shape=[jax.ShapeDtypeStruct((B, MEM_PAIRS, SEQ, 128), F32),
                   jax.ShapeDtypeStruct((B, N_QBLK, 8, 128), F32),
                   jax.ShapeDtypeStruct((B, MEM_LEN, 2 * W_C), MXU)],
        scratch_shapes=[pltpu.VMEM((MEM_PAIRS, 2 * MEM_LEN, 128), MXU), pltpu.VMEM((MEM_PAIRS, 2 * MEM_LEN, 128), MXU)],
        compiler_params=_cparams(("arbitrary",)),
    )(qc, mem, w_mem)


def _mem_attn_bwd(qc, mkv, do, lse, delta, mem):
    B = qc.shape[0]

    def body(q_ref, mkv_ref, do_ref, lse_ref, delta_ref, mem_ref, dh_ref, db_ref, dw_ref,
             dq_ref, dmkv_ref, k2_ref, v2_ref):
        @pl.when(pl.program_id(0) == 0)
        def _():
            dw_ref[...] = jnp.zeros_like(dw_ref)
            db_ref[...] = jnp.zeros_like(db_ref)
        dmkv_ref[...] = jnp.zeros_like(dmkv_ref)
        dq_ref[...] = jnp.zeros_like(dq_ref)
        for c in range(MEM_PAIRS):
            k2_ref[c] = _stack_pair(mkv_ref[:, c * 128:(c + 1) * 128])
            v2_ref[c] = _stack_pair(mkv_ref[:, W_C + c * 128:W_C + (c + 1) * 128])

        def tile(blk):
            rows = pl.ds(pl.multiple_of(blk * BLK, BLK), BLK)
            qcs, docs, sT, dpT = [], [], [], []
            for c in range(MEM_PAIRS):
                qc_, doc = q_ref.at[c][rows, :], do_ref.at[c][rows, :]
                s2, dp2 = _dot_nt(k2_ref[c], qc_), _dot_nt(v2_ref[c], doc)
                sT += [s2[0:MEM_LEN], s2[MEM_LEN:2 * MEM_LEN]]
                dpT += [dp2[0:MEM_LEN], dp2[MEM_LEN:2 * MEM_LEN]]
                qcs.append(qc_)
                docs.append(doc)
            lse_r = _lanes([lse_ref.at[blk][h:h + 1, :] for h in range(MEM_H)])
            delta_r = _lanes([delta_ref.at[blk][h:h + 1, :] for h in range(MEM_H)])
            yield
            pT = jnp.exp(_lanes(sT) - lse_r)
            dsT = pT * (_lanes(dpT) - delta_r)
            dsb, pb = dsT.astype(MXU), pT.astype(MXU)
            yield
            for c in range(MEM_PAIRS):
                q0, q1 = slice(2 * c * BLK, (2 * c + 1) * BLK), slice((2 * c + 1) * BLK, (2 * c + 2) * BLK)
                ds2 = jnp.concatenate([dsb[:, q0], dsb[:, q1]], axis=0)
                p2 = jnp.concatenate([pb[:, q0], pb[:, q1]], axis=0)
                dq_ref.at[c][rows, :] = _dot_t0(ds2, k2_ref[c])
                dmkv_ref[:, c * 128:(c + 1) * 128] += _pair_rows(_dot(ds2, qcs[c]), MEM_LEN)
                dmkv_ref[:, W_C + c * 128:W_C + (c + 1) * 128] += _pair_rows(_dot(p2, docs[c]), MEM_LEN)

        def part_of(rs, tab):
            return _lanes([dq_ref.at[c][rs, :] for c in range(MEM_PAIRS)]) * Q_SCALE

        trips = N_QBLK // MEM_UNROLL

        def loop(j, carry):
            _interleave([tile(j * MEM_UNROLL + u) for u in range(MEM_UNROLL)]
                        + _dh_tiles_behind(part_of, dh_ref, db_ref, None, j, MEM_UNROLL))
            return carry
        lax.fori_loop(0, trips, loop, 0)
        _interleave([_dh_tile(part_of, dh_ref, db_ref, None, blk) for blk in range(MEM_UNROLL * (trips - 1), N_QBLK)])
        dw_ref[...] += _dot_tn(mem_ref[...], dmkv_ref[...].astype(MXU))

    stat = pl.BlockSpec((None, N_QBLK, 8, 128), lambda b: (b, 0, 0, 0))
    pairs = pl.BlockSpec((None, MEM_PAIRS, SEQ, 128), lambda b: (b, 0, 0, 0))
    return pl.pallas_call(
        body, name="mem_attn_bwd", grid=(B,),
        in_specs=[pairs, pl.BlockSpec((None, MEM_LEN, 2 * W_C), lambda b: (b, 0, 0)), pairs, stat, stat,
                  pl.BlockSpec((None, MEM_LEN, D_MODEL), lambda b: (b, 0, 0))],
        out_specs=[pl.BlockSpec((SEQ, W_C), lambda b: (b, 0)), pl.BlockSpec((8, W_C), lambda b: (0, 0)),
                   pl.BlockSpec((D_MODEL, 2 * W_C), lambda b: (0, 0))],
        out_shape=[jax.ShapeDtypeStruct((B * SEQ, W_C), MXU), jax.ShapeDtypeStruct((8, W_C), F32),
                   jax.ShapeDtypeStruct((D_MODEL, 2 * W_C), F32)],
        scratch_shapes=[pltpu.VMEM((MEM_PAIRS, SEQ, 128), F32), pltpu.VMEM((MEM_LEN, 2 * W_C), F32),
                        pltpu.VMEM((MEM_PAIRS, 2 * MEM_LEN, 128), MXU), pltpu.VMEM((MEM_PAIRS, 2 * MEM_LEN, 128), MXU)],
        compiler_params=_cparams(("arbitrary",)),
    )(qc, mkv, do, lse, delta, mem)


def _headsum(t, e):
    if MXU == F32:
        return _dot(t, e)
    hi = t.astype(MXU)
    lo = (t - hi.astype(F32)).astype(MXU)
    return _dot(hi, e) + _dot(lo, e)


def _heads_to_rows(t, e):
    return sum(_dot_nt(e, part) for part in _split3(t))


POST_TM = 256
POST_ROWS = 256


def _post(o_a, olse_b, o_c, z, x2, tgt, g, gain, bias, w_out, hsum, hrows):
    T = x2.shape[0]
    tm = POST_TM
    nt = SEQ // tm

    def body(oa_ref, ob_ref, oc_ref, z_ref, x_ref, t_ref, g_ref, gain_ref, bias_ref, w_ref, e_ref, er_ref,
             gx_ref, doa_ref, dela_ref, dobb_ref, doc_ref, delc_ref, dz_ref, dw_ref, small_ref, loss_ref):
        @pl.when(pl.program_id(0) == 0)
        def _():
            dw_ref[...] = jnp.zeros_like(dw_ref)
            small_ref[...] = jnp.zeros_like(small_ref)
            loss_ref[...] = jnp.zeros_like(loss_ref)

        gg = g_ref[...]
        gain_v = gain_ref[...]
        gain_s = gain_v * (1.0 / D_MODEL)
        bias_v = bias_ref[...]
        w = w_ref[...]

        def rms(o):
            rr = lax.rsqrt(jnp.mean(o * o, axis=1, keepdims=True) + RMS_EPS)
            return o * rr, rr

        def rows_of(rs, results):
            oa = _lanes([oa_ref.at[c][rs, :] for c in range(SWA_Q // 2)])
            (o1, l1), (o4, l4), (o16, l16) = [
                (_lanes([ob_ref.at[p, 0][rs, :], ob_ref.at[p, 1][rs, :]]),
                 _lanes([ob_ref.at[p, 2][rs, :], ob_ref.at[p, 3][rs, :]])) for p in range(3)]
            mx = jnp.maximum(jnp.maximum(l1, l4), l16)
            e1, e4, e16 = jnp.exp(l1 - mx), jnp.exp(l4 - mx), jnp.exp(l16 - mx)
            den = e1 + e4 + e16
            ob = (e1 * o1 + e4 * o4 + e16 * o16) / den
            lse_b = mx + jnp.log(den)
            oc = _lanes([oc_ref.at[c][rs, :] for c in range(MEM_PAIRS)])
            na, ra = rms(oa)
            nb, rb = rms(ob)
            nc, rc = rms(oc)
            n = jnp.concatenate([na, nb, nc], axis=1)
            zz = z_ref[rs, :]
            sig = 0.5 * jnp.tanh(0.5 * zz) + 0.5
            sz = zz * sig
            gs = gg * sz
            u = n * gs
            yo = _dot(u.astype(MXU), w)
            yield
            r = ALPHA * x_ref[rs, :] + yo
            rc0 = r - jnp.mean(r, axis=1, keepdims=True)
            rstd = lax.rsqrt(jnp.mean(rc0 * rc0, axis=1, keepdims=True) + LN_EPS)
            xhat = rc0 * rstd
            err = xhat * gain_v + bias_v - t_ref[rs, :]
            dxh = err * gain_s
            dr = rstd * (dxh - jnp.mean(dxh, axis=1, keepdims=True)
                         - xhat * jnp.mean(dxh * xhat, axis=1, keepdims=True))
            gx_ref[rs, :] = ALPHA * dr
            drb = dr.astype(MXU)
            du = _dot_nt(drb, w)
            yield
            dun = du * n
            dz = dun * (gg * (sig + sz * (1.0 - sig)))
            dz_ref[rs, :] = dz.astype(MXU)
            dn = du * gs

            def branch(lo, hi, nbr, rr):
                dnb = dn[:, lo:hi]
                return rr * (dnb - nbr * jnp.mean(dnb * nbr, axis=1, keepdims=True))

            def to_kernel(dob, o, do_ref, delta_ref):
                wd = dob.shape[1]
                for c in range(wd // 128):
                    do_ref.at[c][rs, :] = dob[:, c * 128:(c + 1) * 128].astype(do_ref.dtype)
                dT = _heads_to_rows(dob * o, er_ref[:, 0:wd])
                for jb in range((rs.stop - rs.start) // BLK):
                    delta_ref[rs.start // BLK + jb] = dT[0:8, jb * BLK:(jb + 1) * BLK]

            to_kernel(branch(0, W_A, na, ra), oa, doa_ref, dela_ref)
            to_kernel(branch(W_A + W_B, D_MIX, nc, rc), oc, doc_ref, delc_ref)
            dob = branch(W_A, W_A + W_B, nb, rb)
            for j, t in enumerate((dob, lse_b, _headsum(dob * ob, e_ref[...]))):
                for c in range(W_B // 128):
                    dobb_ref.at[j * (W_B // 128) + c][rs, :] = t[:, c * 128:(c + 1) * 128]
            csum = lambda t: jnp.sum(t, axis=0, keepdims=True)
            results.append((u, drb, jnp.sum(err * err), csum(err * xhat), csum(err), csum(dun * sz), csum(dz)))

        parts = []
        _interleave([rows_of(slice(k * POST_ROWS, (k + 1) * POST_ROWS), parts) for k in range(tm // POST_ROWS)])
        tot = [sum(p[i] for p in parts) for i in range(2, 7)]
        dw_ref[...] += _dot_tn(jnp.concatenate([p[0] for p in parts], axis=0),
                               jnp.concatenate([p[1] for p in parts], axis=0))
        loss_ref[...] += 0.5 * tot[0] * (1.0 / D_MODEL)
        small_ref[0:1, :] += tot[1] * (1.0 / D_MODEL)
        small_ref[1:2, :] += tot[2] * (1.0 / D_MODEL)
        small_ref[2:3, :] += tot[3]
        small_ref[3:4, :] += tot[4]

    B = T // SEQ
    row = lambda w: pl.BlockSpec((tm, w), lambda i: (i, 0))
    full = lambda a, b: pl.BlockSpec((a, b), lambda i: (0, 0))
    chunked = lambda n: pl.BlockSpec((None, n, tm, 128), lambda i: (i // nt, 0, i % nt, 0))
    stat = pl.BlockSpec((None, tm // BLK, 8, 128), lambda i: (i // nt, i % nt, 0, 0))
    return pl.pallas_call(
        body, name="post_fwd_bwd", grid=(T // tm,),
        in_specs=[chunked(SWA_Q // 2), pl.BlockSpec((None, 3, 4, tm, 128), lambda i: (i // nt, 0, 0, i % nt, 0)),
                  chunked(MEM_PAIRS),
                  row(D_MIX), row(D_MODEL), row(D_MODEL),
                  full(1, D_MIX), full(1, D_MODEL), full(1, D_MODEL), full(D_MIX, D_MODEL), full(W_B, W_B),
                  full(PICK_ROWS, W_A)],
        out_specs=[row(D_MODEL), chunked(SWA_Q // 2), stat, chunked(6), chunked(MEM_PAIRS), stat, row(D_MIX),
                   full(D_MIX, D_MODEL), full(8, D_MODEL), full(8, 128)],
        out_shape=[jax.ShapeDtypeStruct((T, D_MODEL), F32),
                   jax.ShapeDtypeStruct((B, SWA_Q // 2, SEQ, 128), MXU),
                   jax.ShapeDtypeStruct((B, N_QBLK, 8, 128), F32),
                   jax.ShapeDtypeStruct((B, 6, SEQ, 128), F32),
                   jax.ShapeDtypeStruct((B, MEM_PAIRS, SEQ, 128), MXU),
                   jax.ShapeDtypeStruct((B, N_QBLK, 8, 128), F32),
                   jax.ShapeDtypeStruct((T, D_MIX), MXU),
                   jax.ShapeDtypeStruct((D_MIX, D_MODEL), F32),
                   jax.ShapeDtypeStruct((8, D_MODEL), F32),
                   jax.ShapeDtypeStruct((8, 128), F32)],
        compiler_params=_cparams(("arbitrary",)),
    )(o_a, olse_b, o_c, z, x2, tgt, g, gain, bias, w_out, hsum, hrows)


TAIL_TK = 512
TAIL_TN = D_IN // 2
TAIL_TM = 256
REDUCE_ROWS = 128


DH_SPLITS = (0, W_A + 2 * W_KVA, W_A + 2 * W_KVA + 3 * W_B, D_IN - D_MIX, D_IN)


def _tail(xt, dhs, gx1, w_in, small_g):
    T = xt.shape[1]
    dh = dhs[0]
    c0, c1, c2, c3, c4 = DH_SPLITS
    assert c1 < TAIL_TN < c2 and (TAIL_TN - c1) % 128 == 0
    kt = T // TAIL_TK
    ndw = (D_IN // TAIL_TN) * kt
    nsteps = ndw + T // TAIL_TM
    n_pass = D_IN // TAIL_TN
    assert n_pass == 2 and TAIL_TN == 4 * COLS_PER_DEV and kt >= 2
    pay = dh.dtype
    blk_shape = (D_MODEL, COLS_PER_DEV)
    n_half = 2 * n_pass
    n_chip = N_DEV // 2

    def body(xt_ref, a1_ref, b1_ref, b2_ref, c1_ref, z1_ref, a2_ref, b3_ref, c2_ref, z2_ref,
             gx_ref, w_hbm, sg_ref, dx_ref, gin_ref, gsm_ref,
             acc_ref, w_ref, mine_ref, stagea_ref, landa_ref, stageb_ref, landb_ref, own_ref, lsm_ref,
             sa_sems, ra_sems, sb_sems, rb_sems, ss_sems, rs_sems, w_sem):
        s = pl.program_id(0)
        x, y, c = _my_pos()
        me = 4 * x + 2 * y + c
        chip = 2 * x + y

        def to_sibling(q):
            return pltpu.make_async_remote_copy(
                src_ref=stagea_ref.at[q], dst_ref=landa_ref.at[q], send_sem=sa_sems.at[q], recv_sem=ra_sems.at[q],
                device_id=(x, y, 1 - c), device_id_type=MESH)

        def to_owner(q):
            return pltpu.make_async_remote_copy(
                src_ref=stageb_ref.at[q], dst_ref=landb_ref.at[chip], send_sem=sb_sems.at[q],
                recv_sem=rb_sems.at[chip], device_id=(q // 2, q % 2, c), device_id_type=MESH)

        def from_chip(m):
            return pltpu.make_async_remote_copy(
                src_ref=landb_ref.at[m], dst_ref=landb_ref.at[m], send_sem=sb_sems.at[m], recv_sem=rb_sems.at[m],
                device_id=(m // 2, m % 2, c), device_id_type=MESH)

        def is_me(q):
            return (x == q // 2) & (y == q % 2)

        def small_to(j):
            return pltpu.make_async_remote_copy(
                src_ref=sg_ref, dst_ref=lsm_ref.at[me], send_sem=ss_sems.at[j], recv_sem=rs_sems.at[me],
                device_id=_dev_coords(j), device_id_type=MESH)

        def small_from(m):
            return pltpu.make_async_remote_copy(
                src_ref=lsm_ref.at[m], dst_ref=lsm_ref.at[m], send_sem=ss_sems.at[m], recv_sem=rs_sems.at[m],
                device_id=_dev_coords(m), device_id_type=MESH)

        w_copy = pltpu.make_async_copy(w_hbm, w_ref, w_sem)

        @pl.when(s == 0)
        def _():
            w_copy.start()
            for j in range(N_DEV):
                pl.when(me != j)(small_to(j).start)
            lsm_ref[me] = sg_ref[...]
            landb_ref[chip] = jnp.zeros(blk_shape, pay)

        @pl.when(s < ndw)
        def _():
            @pl.when(s % kt == 0)
            def _():
                acc_ref[...] = jnp.zeros_like(acc_ref)
            xt_ = xt_ref[...]
            @pl.when(s < kt)
            def _():
                acc_ref[:, 0:c1] += _dot(xt_, a1_ref[...])
                acc_ref[:, c1:TAIL_TN] += _dot(xt_, b1_ref[...])

            @pl.when(s >= kt)
            def _():
                acc_ref[:, 0:c2 - TAIL_TN] += _dot(xt_, b2_ref[...])
                acc_ref[:, c2 - TAIL_TN:c3 - TAIL_TN] += _dot(xt_, c1_ref[...])
                acc_ref[:, c3 - TAIL_TN:c4 - TAIL_TN] += _dot(xt_, z1_ref[...])

        for p in range(n_pass):
            @pl.when(s == p * kt + kt - 1)
            def _(p=p):
                for cc in range(2):
                    @pl.when(c == cc)
                    def _(cc=cc):
                        for yo in range(2):
                            q = 2 * p + yo
                            same, other = 2 * yo + cc, 2 * yo + 1 - cc
                            mine_ref[q] = acc_ref[:, same * COLS_PER_DEV:(same + 1) * COLS_PER_DEV]
                            stagea_ref[q] = acc_ref[:, other * COLS_PER_DEV:(other + 1) * COLS_PER_DEV].astype(pay)
                for yo in range(2):
                    to_sibling(2 * p + yo).start()

            @pl.when(s == (p + 1) * kt + 1)
            def _(p=p):
                for yo in range(2):
                    q = 2 * p + yo
                    to_sibling(q).wait_recv()

                    def chunk(i, carry, q=q):
                        rs = pl.ds(pl.multiple_of(i * REDUCE_ROWS, REDUCE_ROWS), REDUCE_ROWS)
                        tot = mine_ref[q, rs, :] + landa_ref[q, rs, :].astype(F32)

                        @pl.when(is_me(q))
                        def _():
                            own_ref[rs, :] = tot

                        @pl.when(jnp.logical_not(is_me(q)))
                        def _():
                            stageb_ref[q, rs, :] = tot.astype(pay)
                        return carry
                    lax.fori_loop(0, D_MODEL // REDUCE_ROWS, chunk, 0)
                    pl.when(jnp.logical_not(is_me(q)))(to_owner(q).start)

        @pl.when(s >= ndw)
        def _():
            pl.when(s == ndw)(w_copy.wait)
            dx_ref[...] = (_dot_nt(a2_ref[...], w_ref[:, c0:c1]) + _dot_nt(b3_ref[...], w_ref[:, c1:c2])
                           + _dot_nt(c2_ref[...], w_ref[:, c2:c3]) + _dot_nt(z2_ref[...], w_ref[:, c3:c4])
                           + gx_ref[...])

        @pl.when(s == nsteps - 1)
        def _():
            for m in range(n_chip):
                pl.when(m != chip)(from_chip(m).wait_recv)
            for m in range(N_DEV):
                pl.when(me != m)(small_from(m).wait_recv)
            for q in range(n_half):
                to_sibling(q).wait_send()
                pl.when(jnp.logical_not(is_me(q)))(to_owner(q).wait_send)
            for j in range(N_DEV):
                pl.when(me != j)(small_to(j).wait_send)

            def chunk(i, carry):
                rs = pl.ds(pl.multiple_of(i * REDUCE_ROWS, REDUCE_ROWS), REDUCE_ROWS)
                g = own_ref[rs, :]
                for m in range(n_chip):
                    g = g + landb_ref[m, rs, :].astype(F32)
                gin_ref[rs, :] = g
                return carry
            lax.fori_loop(0, D_MODEL // REDUCE_ROWS, chunk, 0)
            g = lsm_ref[0]
            for m in range(1, N_DEV):
                g = g + lsm_ref[m]
            gsm_ref[...] = g

    dw_step = lambda s: jnp.minimum(s, ndw - 1)
    dx_step = lambda s: jnp.maximum(s - ndw, 0)
    pass0 = lambda s: jnp.minimum(s, kt - 1)
    pass1 = lambda s: jnp.clip(s - kt, 0, kt - 1)
    any_spec = pl.BlockSpec(memory_space=pl.ANY)
    vmem = pl.BlockSpec(memory_space=pltpu.VMEM)
    dma = pltpu.SemaphoreType.DMA
    scratch = [pltpu.VMEM((D_MODEL, TAIL_TN), F32), pltpu.VMEM((D_MODEL, D_IN), w_in.dtype),
               pltpu.VMEM((n_half,) + blk_shape, F32),
               pltpu.VMEM((n_half,) + blk_shape, pay), pltpu.VMEM((n_half,) + blk_shape, pay),
               pltpu.VMEM((n_half,) + blk_shape, pay), pltpu.VMEM((n_chip,) + blk_shape, pay),
               pltpu.VMEM(blk_shape, F32), pltpu.VMEM((N_DEV,) + small_g.shape, F32),
               dma((n_half,)), dma((n_half,)), dma((n_half,)), dma((n_chip,)), dma((N_DEV,)), dma((N_DEV,)), dma]
    return pl.pallas_call(
        body, name="tail_dw_dx_reduce", grid=(nsteps,),
        in_specs=[pl.BlockSpec((D_MODEL, TAIL_TK), lambda s: (0, dw_step(s) % kt)),
                  pl.BlockSpec((TAIL_TK, c1 - c0), lambda s: (pass0(s), 0)),
                  pl.BlockSpec((TAIL_TK, TAIL_TN - c1), lambda s: (pass0(s), 0)),
                  pl.BlockSpec((TAIL_TK, 128), lambda s: (pass1(s), (TAIL_TN - c1) // 128)),
                  pl.BlockSpec((TAIL_TK, c3 - c2), lambda s: (pass1(s), 0)),
                  pl.BlockSpec((TAIL_TK, c4 - c3), lambda s: (pass1(s), 0)),
                  pl.BlockSpec((TAIL_TM, c1 - c0), lambda s: (dx_step(s), 0)),
                  pl.BlockSpec((TAIL_TM, c2 - c1), lambda s: (dx_step(s), 0)),
                  pl.BlockSpec((TAIL_TM, c3 - c2), lambda s: (dx_step(s), 0)),
                  pl.BlockSpec((TAIL_TM, c4 - c3), lambda s: (dx_step(s), 0)),
                  pl.BlockSpec((TAIL_TM, D_MODEL), lambda s: (dx_step(s), 0)),
                  any_spec, vmem],
        out_specs=[pl.BlockSpec((TAIL_TM, D_MODEL), lambda s: (dx_step(s), 0)), vmem, vmem],
        out_shape=[jax.ShapeDtypeStruct((T, D_MODEL), F32), jax.ShapeDtypeStruct(blk_shape, F32),
                   jax.ShapeDtypeStruct(small_g.shape, F32)],
        scratch_shapes=scratch,
        compiler_params=_cparams(("arbitrary",)),
    )(xt, dhs[0], dhs[1], dhs[1], dhs[2], dhs[3], dhs[0], dhs[1], dhs[2], dhs[3], gx1, w_in, small_g)


def _adam_update(grads, params, carried):
    n = len(grads)

    def body(*refs):
        g_refs, p_refs, o_refs = refs[1:1 + n], refs[1 + n:1 + 4 * n], refs[2 + 4 * n:]
        for a in range(n):
            rows = g_refs[a].shape[0]
            cr = REDUCE_ROWS if rows % REDUCE_ROWS == 0 else rows
            flat2 = lambda r: r.at[0] if len(r.shape) == 3 else r
            w_ref, m_ref, v_ref = [flat2(r) for r in p_refs[3 * a:3 * a + 3]]
            go_ref, d_ref, nm_ref, nv_ref = [flat2(r) for r in o_refs[4 * a:4 * a + 4]]

            def chunk(i, carry, cr=cr, g_ref=g_refs[a], w_ref=w_ref, m_ref=m_ref, v_ref=v_ref,
                      go_ref=go_ref, d_ref=d_ref, nm_ref=nm_ref, nv_ref=nv_ref):
                rs = pl.ds(pl.multiple_of(i * cr, cr), cr)
                g = g_ref[rs, :]
                go_ref[rs, :] = g
                d_ref[rs, :], nm_ref[rs, :], nv_ref[rs, :] = _adamw(w_ref[rs, :], g, m_ref[rs, :], v_ref[rs, :])
                return carry
            lax.fori_loop(0, rows // cr, chunk, 0)

    vmem = pl.BlockSpec(memory_space=pltpu.VMEM)
    any_spec = pl.BlockSpec(memory_space=pl.ANY)
    flat = [p for grp in params for p in grp]
    outs = pl.pallas_call(
        body, name="adamw", in_specs=[any_spec] + [vmem] * (4 * n), out_specs=[any_spec] + [vmem] * (4 * n),
        out_shape=[jax.ShapeDtypeStruct(carried.shape, carried.dtype)]
        + [jax.ShapeDtypeStruct(grp[0].shape, F32) for grp in params for _ in range(4)],
        input_output_aliases={0: 0},
        compiler_params=pltpu.CompilerParams(vmem_limit_bytes=VMEM_LIMIT),
    )(carried, *grads, *flat)
    return [outs[1 + 4 * a:5 + 4 * a] for a in range(n)], outs[0]


def _step(x, mem, w_in_s, w_mem_s, w_out_s, b_in, sinks, g, gain, bias, tgt):
    B = x.shape[0]
    T = B * SEQ
    x2 = x.reshape(T, D_MODEL)
    t2 = tgt.reshape(T, D_MODEL)
    rope_inv = _rope_inv()
    lane = jnp.arange(W_A)
    hsum = (lane[:W_B, None] // HEAD == lane[None, :W_B] // HEAD).astype(MXU)
    hrows = (jnp.arange(PICK_ROWS)[:, None] == lane[None, :] // HEAD).astype(MXU)
    me = 4 * lax.axis_index("x") + 2 * lax.axis_index("y") + lax.axis_index("c")

    (w_in_all,) = _gather_weights([w_in_s])
    qkva, qkvb, qc, z, w_in, xt, rope_tab, w_mem_all, w_out_all = _in_proj(
        x2, w_in_all, b_in, rope_inv, [w_mem_s, w_out_s])
    w_mem = w_mem_all.reshape(D_MODEL, 2 * W_C)
    w_out = w_out_all.reshape(D_MIX, D_MODEL)

    o_a, lse_a = _swa_fwd(qkva, sinks)
    olse_b = _dil_fwd(qkvb)
    o_c, lse_c, mkv = _mem_attn_fwd(qc, mem, w_mem)

    gx1, do_a, delta_a, dobb, do_c, delta_c, dz, dw_out, small, loss = _post(
        o_a, olse_b, o_c, z, x2, t2, g, gain, bias, w_out, hsum, hrows)

    dh_c, db_c, dw_mem = _mem_attn_bwd(qc, mkv, do_c, lse_c, delta_c, mem)
    blocks = [dw_mem.reshape(N_DEV, ROWS_PER_DEV, 2 * W_C), dw_out.reshape(N_DEV, ROWS_PER_DEV, D_MODEL)]
    sends = [b.astype(MXU) for b in blocks]
    owns = [lax.dynamic_index_in_dim(b, me, axis=0, keepdims=False) for b in blocks]
    dh_b, db_b, (g_mem, g_out) = _dil_bwd(qkvb, dobb, rope_tab, sends, owns)
    dh_a, db_a, dsink = _swa_bwd(qkva, do_a, lse_a, delta_a, sinks, rope_tab)

    small_g = _pack_small(dict(b_in=jnp.concatenate([db_a[0], db_b[0], db_c[0], small[3]]), sinks=dsink[:, 0],
                               g=small[2], gain=small[0], bias=small[1], loss=loss[0, 0]))
    grad_x, g_in, g_small = _tail(xt, (dh_a, dh_b, dh_c, dz), gx1, w_in, small_g)
    return grad_x.reshape(B, SEQ, D_MODEL), g_in, g_mem, g_out, g_small


def _my_pos():
    return lax.axis_index("x"), lax.axis_index("y"), lax.axis_index("c")


def _gather_weights(shards):
    n_arr = len(shards)

    def body(*refs):
        ins, outs = refs[0:n_arr], refs[n_arr:2 * n_arr]
        send_sems, recv_sems, local_sems = refs[2 * n_arr:]
        x, y, c = _my_pos()
        me, sibling = (x, y, c), (x, y, 1 - c)
        chips = [(1 - x, y), (x, 1 - y), (1 - x, 1 - y)]

        def slot(a, pos):
            return outs[a].at[4 * pos[0] + 2 * pos[1] + pos[2]]

        def copy(a, k, block, to, src=None):
            return pltpu.make_async_remote_copy(
                src_ref=slot(a, block) if src is None else src, dst_ref=slot(a, block),
                send_sem=send_sems.at[a, k], recv_sem=recv_sems.at[a, k],
                device_id=to, device_id_type=MESH)

        mine = [pltpu.make_async_copy(ins[a], slot(a, me), local_sems.at[a]) for a in range(n_arr)]
        for cp in mine:
            cp.start()
        first = []
        for a in range(n_arr):
            first.append(copy(a, 0, me, sibling, src=ins[a]))
            first += [copy(a, 1 + j, me, (*chip, c), src=ins[a]) for j, chip in enumerate(chips)]
        for cp in first:
            cp.start()
        passed = []
        for j, chip in enumerate(chips):
            for a in range(n_arr):
                copy(a, 1 + j, (*chip, c), me).wait_recv()
                fwd = copy(a, 4 + j, (*chip, c), sibling)
                fwd.start()
                passed.append(fwd)
        for a in range(n_arr):
            copy(a, 0, sibling, me).wait_recv()
            for j, chip in enumerate(chips):
                copy(a, 4 + j, (*chip, 1 - c), me).wait_recv()
        for cp in first + passed:
            cp.wait_send()
        for cp in mine:
            cp.wait()

    any_spec = pl.BlockSpec(memory_space=pl.ANY)
    return pl.pallas_call(
        body, name="gather_weights",
        in_specs=[any_spec] * n_arr, out_specs=[any_spec] * n_arr,
        out_shape=[jax.ShapeDtypeStruct((N_DEV,) + s.shape, s.dtype) for s in shards],
        scratch_shapes=[pltpu.SemaphoreType.DMA((n_arr, 7)), pltpu.SemaphoreType.DMA((n_arr, 7)),
                        pltpu.SemaphoreType.DMA((n_arr,))],
    )(*shards)


def _adamw(w, g, m, v):
    m = ADAM_B1 * m + (1.0 - ADAM_B1) * g
    v = ADAM_B2 * v + (1.0 - ADAM_B2) * (g * g)
    m_hat = m / (1.0 - ADAM_B1 ** ADAM_STEP)
    v_hat = v / (1.0 - ADAM_B2 ** ADAM_STEP)
    delta = -ADAM_LR * (m_hat / (jnp.sqrt(v_hat) + ADAM_EPS) + ADAM_WD * w)
    return delta, m, v


_SMALL_SIZES = (("b_in", D_IN), ("g", D_MIX), ("gain", D_MODEL), ("bias", D_MODEL), ("sinks", SWA_Q), ("loss", 1))


def _pack_small(d):
    flat = jnp.concatenate([jnp.reshape(d[k], (-1,)).astype(F32) if k in d else jnp.zeros((n,), F32)
                            for k, n in _SMALL_SIZES])
    flat = jnp.pad(flat, (0, SMALL_ROWS * 128 - flat.shape[0]))
    return flat.reshape(SMALL_ROWS, 128)


def _unpack_small(p):
    flat = p.reshape(-1)
    out, off = {}, 0
    for k, n in _SMALL_SIZES:
        out[k] = flat[off:off + n].reshape(1, n)
        off += n
    return out


def kernel(x, mem, w_in, b_in, w_mem, attn_sinks, g_branch, w_out, ln_gain, ln_bias, loss_target, m_w_in, m_b_in, m_w_mem, m_attn_sinks, m_g_branch, m_w_out, m_ln_gain, m_ln_bias, v_w_in, v_b_in, v_w_mem, v_attn_sinks, v_g_branch, v_w_out, v_ln_gain, v_ln_bias):
    grad_x, g_in, g_mem, g_out, g_small = _step(
        x, mem, w_in[0].astype(MXU), w_mem[0].astype(MXU), w_out[0].astype(MXU), b_in, attn_sinks[0],
        g_branch, ln_gain, ln_bias, loss_target)

    small_w = _pack_small(dict(b_in=b_in, g=g_branch, gain=ln_gain, bias=ln_bias, sinks=attn_sinks))
    small_m = _pack_small(dict(b_in=m_b_in, g=m_g_branch, gain=m_ln_gain, bias=m_ln_bias, sinks=m_attn_sinks))
    small_v = _pack_small(dict(b_in=v_b_in, g=v_g_branch, gain=v_ln_gain, bias=v_ln_bias, sinks=v_attn_sinks))
    grads = [g_in, g_mem, g_out, g_small]
    params = [(w_in, m_w_in, v_w_in), (w_mem, m_w_mem, v_w_mem), (w_out, m_w_out, v_w_out),
              (small_w, small_m, small_v)]
    res, grad_x = _adam_update(grads, params, grad_x)
    big = res[:3]
    sm = [_unpack_small(r) for r in res[3]]

    def group(i):
        return (big[0][i], sm[i]["b_in"], big[1][i], sm[i]["sinks"], sm[i]["g"], big[2][i],
                sm[i]["gain"], sm[i]["bias"])

    loss = sm[0]["loss"].reshape(())
    return (loss, grad_x, *group(0), *group(1), *group(2), *group(3))
```

```python
import jax
import jax.numpy as jnp
from jax import lax
from jax.experimental import pallas as pl
from jax.experimental.pallas import tpu as pltpu

F32 = jnp.float32
MXU = jnp.bfloat16

D_MODEL = 1024
SEQ = 2048
HEAD = 64
BLK = 128
SWA_Q, SWA_KV = 8, 2
DIL_H = 4
MEM_H = 4
MEM_LEN = 256
W_A, W_KVA, W_B, W_C = 512, 128, 256, 256
D_MIX = 1024
D_IN = 2816
N_DEV = 8
COLS_PER_DEV = D_IN // N_DEV
ROWS_PER_DEV = D_MODEL // N_DEV
ROPE_THETA = 10000.0
LN_EPS = 1e-5
RMS_EPS = 1e-6
ALPHA = 2.0 ** 0.25
Q_SCALE = HEAD ** -0.5
NEG = -1e30
SMALL_ROWS = 48
VMEM_LIMIT = 56 * 1024 * 1024

ADAM_LR = 0.001
ADAM_B1 = 0.9
ADAM_B2 = 0.999
ADAM_EPS = 1e-08
ADAM_WD = 0.01
ADAM_STEP = 10

MESH = pl.DeviceIdType.MESH


def _cparams(sem=None):
    return pltpu.CompilerParams(dimension_semantics=sem, vmem_limit_bytes=VMEM_LIMIT)


def _dot(a, b):
    return jnp.dot(a, b, preferred_element_type=F32)


def _dot_nt(a, b):
    return lax.dot_general(a, b, (((1,), (1,)), ((), ())), preferred_element_type=F32)


def _dot_t0(a, b):
    return lax.dot_general(a, b, (((0,), (0,)), ((), ())), preferred_element_type=F32)


def _dot_tn(a, b):
    return jnp.dot(a.T.astype(MXU), b, preferred_element_type=F32)


def _rope(t, tab, sign):
    cos, sa, sb = tab
    outs = []
    for c in range(t.shape[1] // 128):
        tc = t[:, c * 128:(c + 1) * 128]
        r = pltpu.roll(tc, 96, 1) * sa + pltpu.roll(tc, 32, 1) * sb
        outs.append(tc * cos + r if sign > 0 else tc * cos - r)
    return outs[0] if len(outs) == 1 else jnp.concatenate(outs, axis=1)


def _rope_inv():
    inv = ROPE_THETA ** (-jnp.arange(0, HEAD, 2, dtype=F32) / HEAD)
    return jnp.tile(inv, 2 * 128 // HEAD)[None, :]


def _rope_tab(pos0, rows, inv):
    pos = (lax.broadcasted_iota(jnp.int32, (rows, 128), 0) + pos0).astype(F32)
    ang = pos * inv
    cos, sin = jnp.cos(ang), jnp.sin(ang)
    first = lax.broadcasted_iota(jnp.int32, (rows, 128), 1) % HEAD < HEAD // 2
    return cos, jnp.where(first, -sin, 0.0), jnp.where(first, 0.0, sin)


def _dev_coords(j):
    return (j >> 2, (j >> 1) & 1, j & 1)


def _in_proj(x2, w_all, b_in, rope_inv, late_shards):
    T = x2.shape[0]
    tm = 512
    n_late = len(late_shards)

    def body(x_ref, wall_ref, b_ref, inv_ref, *rest):
        late_in, rest = rest[:n_late], rest[n_late:]
        qkva_ref, qkvb_ref, qc_ref, z_ref, w_ref, xt_ref, tabo_ref = rest[:7]
        late_out = rest[7:7 + n_late]
        send_sems, recv_sems, local_sems = rest[7 + n_late:]
        step, last = pl.program_id(0), pl.num_programs(0) - 1
        x, y, c = _my_pos()
        me = 4 * x + 2 * y + c

        def to_peer(a, j):
            return pltpu.make_async_remote_copy(
                src_ref=late_in[a], dst_ref=late_out[a].at[me], send_sem=send_sems.at[a, j],
                recv_sem=recv_sems.at[a, me], device_id=_dev_coords(j), device_id_type=MESH)

        def from_peer(a, m):
            return pltpu.make_async_remote_copy(
                src_ref=late_out[a].at[m], dst_ref=late_out[a].at[m], send_sem=send_sems.at[a, m],
                recv_sem=recv_sems.at[a, m], device_id=_dev_coords(m), device_id_type=MESH)

        def mine(a):
            return pltpu.make_async_copy(late_in[a], late_out[a].at[me], local_sems.at[a])

        @pl.when(step == 0)
        def _():
            for a in range(n_late):
                mine(a).start()
                for j in range(N_DEV):
                    pl.when(me != j)(to_peer(a, j).start)
            for j in range(N_DEV):
                w_ref[:, j * COLS_PER_DEV:(j + 1) * COLS_PER_DEV] = wall_ref[j]

        xb = x_ref[...].astype(MXU)
        xt_ref[...] = x_ref[...].T.astype(MXU)
        tab = _rope_tab((step % nt) * tm, tm, inv_ref[...])
        for j in range(3):
            tabo_ref[:, j * 128:(j + 1) * 128] = tab[j]

        def seg(c0, c1):
            return _dot(xb, w_ref[:, c0:c1]) + b_ref[:, c0:c1]

        qa = (_rope(seg(0, 512), tab, 1) * Q_SCALE).astype(MXU)
        for c in range(SWA_Q // 2):
            qkva_ref[c] = qa[:, c * 128:(c + 1) * 128]
        lo = lax.broadcasted_iota(jnp.int32, (tm, 128), 1) < HEAD
        for j, t in enumerate((_rope(seg(512, 640), tab, 1), seg(640, 768))):
            other = pltpu.roll(t, HEAD, 1)
            qkva_ref[4 + 2 * j] = jnp.where(lo, t, other).astype(MXU)
            qkva_ref[5 + 2 * j] = jnp.where(lo, other, t).astype(MXU)
        qkvb = (_rope(seg(768, 1024), tab, 1) * Q_SCALE, _rope(seg(1024, 1280), tab, 1), seg(1280, 1536))
        for j, t in enumerate(qkvb):
            for c in range(2):
                qkvb_ref[2 * j + c] = t[:, c * 128:(c + 1) * 128]
        qc = (seg(1536, 1792) * Q_SCALE).astype(MXU)
        for c in range(MEM_H // 2):
            qc_ref[c] = qc[:, c * 128:(c + 1) * 128]
        z_ref[...] = seg(1792, 2816)

        @pl.when(step == last)
        def _():
            for a in range(n_late):
                mine(a).wait()
                for m in range(N_DEV):
                    pl.when(me != m)(from_peer(a, m).wait_recv)
                for j in range(N_DEV):
                    pl.when(me != j)(to_peer(a, j).wait_send)

    nt = SEQ // tm
    any_spec = pl.BlockSpec(memory_space=pl.ANY)
    chunked = lambda n: pl.BlockSpec((None, n, tm, 128), lambda i: (i // nt, 0, i % nt, 0))
    return pl.pallas_call(
        body, name="in_proj_fwd",
        grid=(T // tm,),
        in_specs=[pl.BlockSpec((tm, D_MODEL), lambda i: (i, 0)),
                  pl.BlockSpec((N_DEV, D_MODEL, COLS_PER_DEV), lambda i: (0, 0, 0)),
                  pl.BlockSpec((1, D_IN), lambda i: (0, 0)),
                  pl.BlockSpec((1, 128), lambda i: (0, 0))] + [any_spec] * n_late,
        out_specs=[chunked(SWA_CHUNKS), chunked(6), chunked(MEM_H // 2),
                   pl.BlockSpec((tm, D_MIX), lambda i: (i, 0)),
                   pl.BlockSpec((D_MODEL, D_IN), lambda i: (0, 0)),
                   pl.BlockSpec((D_MODEL, tm), lambda i: (0, i)),
                   pl.BlockSpec((tm, 384), lambda i: (i, 0))] + [any_spec] * n_late,
        out_shape=[jax.ShapeDtypeStruct((T // SEQ, SWA_CHUNKS, SEQ, 128), MXU),
                   jax.ShapeDtypeStruct((T // SEQ, 6, SEQ, 128), F32),
                   jax.ShapeDtypeStruct((T // SEQ, MEM_H // 2, SEQ, 128), MXU),
                   jax.ShapeDtypeStruct((T, D_MIX), F32),
                   jax.ShapeDtypeStruct((D_MODEL, D_IN), w_all.dtype),
                   jax.ShapeDtypeStruct((D_MODEL, T), MXU),
                   jax.ShapeDtypeStruct((T, 384), F32)]
        + [jax.ShapeDtypeStruct((N_DEV,) + s.shape, s.dtype) for s in late_shards],
        scratch_shapes=[pltpu.SemaphoreType.DMA((n_late, N_DEV)), pltpu.SemaphoreType.DMA((n_late, N_DEV)),
                        pltpu.SemaphoreType.DMA((n_late,))],
        compiler_params=_cparams(("arbitrary",)),
    )(x2, w_all, b_in, rope_inv, *late_shards)


CHAIN = 4


def _band_bias(max_dist):
    kj = lax.broadcasted_iota(jnp.int32, (2 * BLK, BLK), 0)
    qi = lax.broadcasted_iota(jnp.int32, (2 * BLK, BLK), 1)
    dist = qi + BLK - kj
    band = jnp.where((dist >= 0) & (dist <= max_dist), 0.0, NEG).astype(F32)
    k1 = lax.broadcasted_iota(jnp.int32, (BLK, BLK), 0)
    q1 = lax.broadcasted_iota(jnp.int32, (BLK, BLK), 1)
    first = jnp.where((q1 - k1 >= 0) & (q1 - k1 <= max_dist), 0.0, NEG).astype(F32)
    return jnp.concatenate([band] * CHAIN, axis=1), jnp.concatenate([first] * CHAIN, axis=1)


def _lanes(parts):
    return jnp.concatenate(parts, axis=1)


PICK_ROWS = 16


def _stack_pair(t):
    lo = (lax.broadcasted_iota(jnp.int32, t.shape, 1) < HEAD).astype(F32)
    return jnp.concatenate([t * lo, t * (1.0 - lo)], axis=0).astype(MXU)


def _pair_rows(x, n):
    lo = lax.broadcasted_iota(jnp.int32, (n, 128), 1) < HEAD
    return jnp.where(lo, x[0:n], x[n:2 * n])


def _split3(t):
    if MXU == F32:
        return (t,)
    hi = t.astype(MXU)
    r = t - hi.astype(F32)
    mid = r.astype(MXU)
    return hi, mid, (r - mid.astype(F32)).astype(MXU)


def _interleave(tiles):
    tiles = list(tiles)
    while tiles:
        for t in list(tiles):
            try:
                next(t)
            except StopIteration:
                tiles.remove(t)


def _softmax_cols(sT, sinkrow=None):
    m = jnp.max(sT, axis=0, keepdims=True)
    if sinkrow is not None:
        m = jnp.maximum(m, sinkrow)
    pT = jnp.exp(sT - m)
    l = jnp.sum(pT, axis=0, keepdims=True)
    if sinkrow is not None:
        l = l + jnp.exp(sinkrow - m)
    return (pT * (1.0 / l)).astype(MXU), m + jnp.log(l)


SWA_CHUNKS = 8
SWA_UNROLL = 3
FWD_UNROLL = 5
N_QBLK = SEQ // BLK


def _swa_fwd(qkva, sinks):
    B = qkva.shape[0]
    G = SWA_Q // SWA_KV

    def body(sink_ref, qkv_ref, o_ref, lse_ref):
        band, first = _band_bias(BLK - 1)
        sinkrows = [_lanes([jnp.full((1, BLK), sink_ref[G * hk + j], F32) for j in range(G)])
                    for hk in range(SWA_KV)]

        def tile(hk, blk, rows_q, rows_k, bias):
            nk = bias.shape[0]
            k2 = _stack_pair(qkv_ref.at[4 + hk][rows_k, :])
            sT = []
            for c in (2 * hk, 2 * hk + 1):
                s2 = _dot_nt(k2, qkv_ref.at[c][rows_q, :])
                sT += [s2[0:nk], s2[nk:2 * nk]]
            yield
            pnT, lse = _softmax_cols(_lanes(sT) + bias, sinkrows[hk])
            yield
            v2 = _stack_pair(qkv_ref.at[6 + hk][rows_k, :])
            for j, c in enumerate((2 * hk, 2 * hk + 1)):
                p2 = jnp.concatenate([pnT[:, 2 * j * BLK:(2 * j + 1) * BLK],
                                      pnT[:, (2 * j + 1) * BLK:(2 * j + 2) * BLK]], axis=0)
                o_ref.at[c][rows_q, :] = _dot_t0(p2, v2)
            for j in range(G):
                lse_ref.at[blk][G * hk + j:G * hk + j + 1, :] = lse[:, j * BLK:(j + 1) * BLK]

        def tiles_at(i):
            r0 = pl.multiple_of(i * BLK, BLK)
            rk = pl.multiple_of(i * BLK - BLK, BLK)
            return [tile(hk, i, pl.ds(r0, BLK), pl.ds(rk, 2 * BLK), band) for hk in range(SWA_KV)]

        _interleave([tile(hk, 0, pl.ds(0, BLK), pl.ds(0, BLK), first) for hk in range(SWA_KV)])

        def loop(j, carry):
            _interleave([t for u in range(FWD_UNROLL) for t in tiles_at(1 + j * FWD_UNROLL + u)])
            return carry
        lax.fori_loop(0, (N_QBLK - 1) // FWD_UNROLL, loop, 0)

    return pl.pallas_call(
        body, name="swa_fwd", grid=(B,),
        in_specs=[pl.BlockSpec(memory_space=pltpu.SMEM),
                  pl.BlockSpec((None, SWA_CHUNKS, SEQ, 128), lambda b: (b, 0, 0, 0))],
        out_specs=[pl.BlockSpec((None, SWA_Q // 2, SEQ, 128), lambda b: (b, 0, 0, 0)),
                   pl.BlockSpec((None, N_QBLK, 8, 128), lambda b: (b, 0, 0, 0))],
        out_shape=[jax.ShapeDtypeStruct((B, SWA_Q // 2, SEQ, 128), F32),
                   jax.ShapeDtypeStruct((B, N_QBLK, 8, 128), F32)],
        compiler_params=_cparams(("arbitrary",)),
    )(sinks, qkva)


EP_ROWS = BLK


def _dh_tile(part_of, dh_ref, db_ref, tab_ref, blk, live=None):
    rs = pl.ds(pl.multiple_of(blk * EP_ROWS, EP_ROWS), EP_ROWS)
    tab = None if tab_ref is None else tuple(tab_ref[rs, j * 128:(j + 1) * 128] for j in range(3))
    part = part_of(rs, tab)
    yield
    dh_ref[rs, :] = part.astype(dh_ref.dtype)
    psum = jnp.sum(part, axis=0, keepdims=True)
    db_ref[0:1, :] += psum if live is None else psum * live
    yield


def _dh_tiles_behind(part_of, dh_ref, db_ref, tab_ref, j, unroll):
    live = (j > 0).astype(F32)
    return [_dh_tile(part_of, dh_ref, db_ref, tab_ref, jnp.where(j > 0, unroll * (j - 1) + u, 0), live)
            for u in range(unroll)]


def _swa_bwd(qkva, do, lse, delta, sinks, rope_tab):
    B = qkva.shape[0]
    G = SWA_Q // SWA_KV

    def body(sink_ref, qkv_ref, do_ref, lse_ref, delta_ref, rtab_ref, dh_ref, db_ref, dsink_ref, dq_ref):
        band, first = _band_bias(BLK - 1)
        sinkrows = [_lanes([jnp.full((1, BLK), sink_ref[G * hk + j], F32) for j in range(G)])
                    for hk in range(SWA_KV)]

        @pl.when(pl.program_id(0) == 0)
        def _():
            dsink_ref[...] = jnp.zeros_like(dsink_ref)
            db_ref[...] = jnp.zeros_like(db_ref)
        for c in range(4, SWA_CHUNKS):
            dq_ref[c] = jnp.zeros((SEQ, 128), F32)

        def tile(hk, blk, rows_q, rows_k, bias, accs):
            nk = bias.shape[0]
            k2 = _stack_pair(qkv_ref.at[4 + hk][rows_k, :])
            v2 = _stack_pair(qkv_ref.at[6 + hk][rows_k, :])
            qcs, docs, sT, dpT = [], [], [], []
            for c in (2 * hk, 2 * hk + 1):
                qc, doc = qkv_ref.at[c][rows_q, :], do_ref.at[c][rows_q, :]
                s2, dp2 = _dot_nt(k2, qc), _dot_nt(v2, doc)
                sT += [s2[0:nk], s2[nk:2 * nk]]
                dpT += [dp2[0:nk], dp2[nk:2 * nk]]
                qcs.append(qc)
                docs.append(doc)
            lse_r = _lanes([lse_ref.at[blk][h:h + 1, :] for h in range(G * hk, G * hk + G)])
            delta_r = _lanes([delta_ref.at[blk][h:h + 1, :] for h in range(G * hk, G * hk + G)])
            yield
            pT = jnp.exp(_lanes(sT) + bias - lse_r)
            dsT = pT * (_lanes(dpT) - delta_r)
            dsb, pb = dsT.astype(MXU), pT.astype(MXU)
            accs[hk] = accs[hk] - jnp.exp(sinkrows[hk] - lse_r) * delta_r
            yield
            dk2 = dv2 = None
            for j, c in enumerate((2 * hk, 2 * hk + 1)):
                q0, q1 = slice(2 * j * BLK, (2 * j + 1) * BLK), slice((2 * j + 1) * BLK, (2 * j + 2) * BLK)
                ds2 = jnp.concatenate([dsb[:, q0], dsb[:, q1]], axis=0)
                p2 = jnp.concatenate([pb[:, q0], pb[:, q1]], axis=0)
                dq_ref.at[c][rows_q, :] = _dot_t0(ds2, k2)
                dk2 = _dot(ds2, qcs[j]) if dk2 is None else dk2 + _dot(ds2, qcs[j])
                dv2 = _dot(p2, docs[j]) if dv2 is None else dv2 + _dot(p2, docs[j])
            dq_ref.at[4 + hk][rows_k, :] += _pair_rows(dk2, nk)
            dq_ref.at[6 + hk][rows_k, :] += _pair_rows(dv2, nk)

        def run(tiles_of, accs):
            accs = list(accs)
            _interleave(tiles_of(accs))
            return tuple(accs)

        zero = jnp.zeros((1, G * BLK), F32)
        accs = run(lambda a: [tile(hk, 0, pl.ds(0, BLK), pl.ds(0, BLK), first, a) for hk in range(SWA_KV)],
                   (zero,) * SWA_KV)

        def part_of(rs, tab):
            lo = lax.broadcasted_iota(jnp.int32, (EP_ROWS, 128), 1) < HEAD

            def kv_grad(c):
                g0, g1 = dq_ref.at[c][rs, :], dq_ref.at[c + 1][rs, :]
                return jnp.where(lo, g0 + pltpu.roll(g0, HEAD, 1), g1 + pltpu.roll(g1, HEAD, 1))
            dq = _lanes([dq_ref.at[c][rs, :] for c in range(SWA_Q // 2)])
            return _lanes([_rope(dq, tab, -1) * Q_SCALE, _rope(kv_grad(4), tab, -1), kv_grad(6)])

        trips = (N_QBLK - 1) // SWA_UNROLL

        def loop(j, accs):
            def tiles_of(a):
                out = []
                for u in range(SWA_UNROLL):
                    i = 1 + j * SWA_UNROLL + u
                    r0 = pl.multiple_of(i * BLK, BLK)
                    rk = pl.multiple_of(i * BLK - BLK, BLK)
                    out += [tile(hk, i, pl.ds(r0, BLK), pl.ds(rk, 2 * BLK), band, a) for hk in range(SWA_KV)]
                return out + _dh_tiles_behind(part_of, dh_ref, db_ref, rtab_ref, j, SWA_UNROLL)
            return run(tiles_of, accs)
        accs = lax.fori_loop(0, trips, loop, accs)
        _interleave([_dh_tile(part_of, dh_ref, db_ref, rtab_ref, blk)
                     for blk in range(SWA_UNROLL * (trips - 1), N_QBLK)])
        for hk in range(SWA_KV):
            for j in range(G):
                tot = jnp.sum(accs[hk][:, j * BLK:(j + 1) * BLK], axis=1, keepdims=True)
                dsink_ref[G * hk + j:G * hk + j + 1, :] += jnp.broadcast_to(tot, (1, 128))

    stat = pl.BlockSpec((None, N_QBLK, 8, 128), lambda b: (b, 0, 0, 0))
    return pl.pallas_call(
        body, name="swa_bwd", grid=(B,),
        in_specs=[pl.BlockSpec(memory_space=pltpu.SMEM),
                  pl.BlockSpec((None, SWA_CHUNKS, SEQ, 128), lambda b: (b, 0, 0, 0)),
                  pl.BlockSpec((None, SWA_Q // 2, SEQ, 128), lambda b: (b, 0, 0, 0)), stat, stat,
                  pl.BlockSpec((SEQ, 384), lambda b: (0, 0))],
        out_specs=[pl.BlockSpec((SEQ, W_A + 2 * W_KVA), lambda b: (b, 0)),
                   pl.BlockSpec((8, W_A + 2 * W_KVA), lambda b: (0, 0)),
                   pl.BlockSpec((8, 128), lambda b: (0, 0))],
        out_shape=[jax.ShapeDtypeStruct((B * SEQ, W_A + 2 * W_KVA), MXU),
                   jax.ShapeDtypeStruct((8, W_A + 2 * W_KVA), F32), jax.ShapeDtypeStruct((8, 128), F32)],
        scratch_shapes=[pltpu.VMEM((SWA_CHUNKS, SEQ, 128), F32)],
        compiler_params=_cparams(("arbitrary",)),
    )(sinks, qkva, do, lse, delta, rope_tab)


DILATIONS = (1, 4, 16)
DIL_PAIRS_H = DIL_H // 2


def _stream_rows(d, r, i, n):
    if d == 1:
        return pl.ds(pl.multiple_of(i * BLK, BLK), n)
    return pl.ds(r + i * (BLK * d), n, stride=d)


def _spread_matrix():
    row = lax.broadcasted_iota(jnp.int32, (PICK_ROWS, 128), 0)
    lane = lax.broadcasted_iota(jnp.int32, (PICK_ROWS, 128), 1)
    return ((row < 6) & ((row % 2 == 1) == (lane >= HEAD))).astype(MXU)


def _lanes_to_tokens(v0, v1, spread):
    n = v0.shape[1]
    row = lax.broadcasted_iota(jnp.int32, (PICK_ROWS, n), 0)
    a = jnp.zeros((PICK_ROWS, n), F32)
    for i, (p0, p1) in enumerate(zip(_split3(v0), _split3(v1))):
        a = jnp.where(row == 2 * i, p0.astype(F32), a)
        a = jnp.where(row == 2 * i + 1, p1.astype(F32), a)
    return _dot_t0(a.astype(MXU), spread)


def _tokens_to_lanes(t):
    r = t.T
    return r[0:1, :], r[HEAD:HEAD + 1, :]


DIL_UNROLL = 3


def _dil_schedule(body_first, body_next, unroll, behind=None):
    for p, d in sorted(enumerate(DILATIONS), key=lambda pd: -pd[1]):
        nblk = SEQ // d // BLK
        if d == 1:
            _interleave([body_first(p, d, 0)])
            def loop(j, c, p=p, d=d):
                _interleave([body_next(p, d, 0, 1 + unroll * j + u) for u in range(unroll)]
                            + (behind(j) if behind else []))
                return c
            lax.fori_loop(0, (nblk - 1) // unroll, loop, 0)
        elif nblk > 1:
            def loop(r, c, p=p, d=d, nblk=nblk):
                _interleave([body_first(p, d, r)] + [body_next(p, d, r, i) for i in range(1, nblk)])
                return c
            lax.fori_loop(0, d, loop, 0)
        else:
            def loop(j, c, p=p, d=d):
                _interleave([body_first(p, d, 4 * j + u) for u in range(4)])
                return c
            lax.fori_loop(0, d // 4, loop, 0)


def _dil_fwd(qkvb):
    B = qkvb.shape[0]

    def body(qkv_ref, o_ref):
        band, first = _band_bias(BLK)
        spread = _spread_matrix()

        def block(p, d, rows_q, rows_k, bias):
            nk = bias.shape[0]
            sT = []
            for c in range(DIL_PAIRS_H):
                qc = qkv_ref.at[c][rows_q, :].astype(MXU)
                s2 = _dot_nt(_stack_pair(qkv_ref.at[DIL_PAIRS_H + c][rows_k, :]), qc)
                sT += [s2[0:nk], s2[nk:2 * nk]]
            yield
            sT = _lanes(sT) + bias
            m = jnp.max(sT, axis=0, keepdims=True)
            pT = jnp.exp(sT - m)
            l = jnp.sum(pT, axis=0, keepdims=True)
            pnT = (pT * (1.0 / l)).astype(MXU)
            lse = m + jnp.log(l)
            yield
            for c in range(DIL_PAIRS_H):
                q0, q1 = slice(2 * c * BLK, (2 * c + 1) * BLK), slice((2 * c + 1) * BLK, (2 * c + 2) * BLK)
                p2 = jnp.concatenate([pnT[:, q0], pnT[:, q1]], axis=0)
                o_ref.at[p, c][rows_q, :] = _dot_t0(p2, _stack_pair(qkv_ref.at[2 * DIL_PAIRS_H + c][rows_k, :]))
                o_ref.at[p, DIL_PAIRS_H + c][rows_q, :] = _lanes_to_tokens(lse[:, q0], lse[:, q1], spread)

        def body_first(p, d, r):
            rows = _stream_rows(d, r, 0, BLK)
            return block(p, d, rows, rows, first)

        def body_next(p, d, r, i):
            return block(p, d, _stream_rows(d, r, i, BLK), _stream_rows(d, r, i - 1, 2 * BLK), band)

        _dil_schedule(body_first, body_next, FWD_UNROLL)

    return pl.pallas_call(
        body, name="dil_fwd", grid=(B,),
        in_specs=[pl.BlockSpec((None, 6, SEQ, 128), lambda b: (b, 0, 0, 0))],
        out_specs=pl.BlockSpec((None, 3, 4, SEQ, 128), lambda b: (b, 0, 0, 0, 0)),
        out_shape=jax.ShapeDtypeStruct((B, 3, 4, SEQ, 128), F32),
        compiler_params=_cparams(("arbitrary",)),
    )(qkvb)


def _reduce_scatter_ops(send_refs, land_refs, send_sems, recv_sems):
    x, y, c = _my_pos()
    me = 4 * x + 2 * y + c
    n = len(send_refs)

    def to_peer(a, j):
        return pltpu.make_async_remote_copy(
            src_ref=send_refs[a].at[j], dst_ref=land_refs[a].at[me], send_sem=send_sems.at[a, j],
            recv_sem=recv_sems.at[a, me], device_id=_dev_coords(j), device_id_type=MESH)

    def from_peer(a, m):
        return pltpu.make_async_remote_copy(
            src_ref=land_refs[a].at[m], dst_ref=land_refs[a].at[m], send_sem=send_sems.at[a, m],
            recv_sem=recv_sems.at[a, m], device_id=_dev_coords(m), device_id_type=MESH)

    def start():
        for j in range(N_DEV):
            @pl.when(me != j)
            def _(j=j):
                for a in range(n):
                    to_peer(a, j).start()
        for a in range(n):
            land_refs[a][me] = jnp.zeros(land_refs[a].shape[1:], land_refs[a].dtype)

    def finish(own_refs, out_refs):
        for m in range(N_DEV):
            @pl.when(me != m)
            def _(m=m):
                for a in range(n):
                    from_peer(a, m).wait_recv()
        for j in range(N_DEV):
            @pl.when(me != j)
            def _(j=j):
                for a in range(n):
                    to_peer(a, j).wait_send()
        for a in range(n):
            def chunk(i, carry, a=a):
                rs = pl.ds(pl.multiple_of(i * REDUCE_ROWS, REDUCE_ROWS), REDUCE_ROWS)
                g = own_refs[a][rs, :]
                for m in range(N_DEV):
                    g = g + land_refs[a][m, rs, :].astype(F32)
                out_refs[a][rs, :] = g
                return carry
            lax.fori_loop(0, own_refs[a].shape[0] // REDUCE_ROWS, chunk, 0)

    return start, finish


def _dil_bwd(qkvb, dobb, rope_tab, sends, owns):
    B = qkvb.shape[0]
    n_rs = len(sends)

    def body(qkv_ref, dob_ref, rtab_ref, *rest):
        send_refs, own_refs = rest[:n_rs], rest[n_rs:2 * n_rs]
        dh_ref, db_ref = rest[2 * n_rs:2 * n_rs + 2]
        out_refs = rest[2 * n_rs + 2:3 * n_rs + 2]
        dq_ref = rest[3 * n_rs + 2]
        land_refs = rest[3 * n_rs + 3:4 * n_rs + 3]
        send_sems, recv_sems = rest[4 * n_rs + 3:]
        rs_start, rs_finish = _reduce_scatter_ops(send_refs, land_refs, send_sems, recv_sems)
        pl.when(pl.program_id(0) == 0)(rs_start)

        band, first = _band_bias(BLK)
        dq_ref[...] = jnp.zeros_like(dq_ref)

        @pl.when(pl.program_id(0) == 0)
        def _():
            db_ref[...] = jnp.zeros_like(db_ref)

        def block(p, d, rows_q, rows_k, bias):
            nk = bias.shape[0]
            lo = lax.broadcasted_iota(jnp.int32, (nk, 128), 1) < HEAD
            qcs, docs, k2s, sT, dpT, lse, delta = [], [], [], [], [], [], []
            for c in range(DIL_PAIRS_H):
                qc = qkv_ref.at[c][rows_q, :].astype(MXU)
                doc = dob_ref.at[c][rows_q, :].astype(MXU)
                k2 = _stack_pair(qkv_ref.at[DIL_PAIRS_H + c][rows_k, :])
                s2 = _dot_nt(k2, qc)
                dp2 = _dot_nt(_stack_pair(qkv_ref.at[2 * DIL_PAIRS_H + c][rows_k, :]), doc)
                sT += [s2[0:nk], s2[nk:2 * nk]]
                dpT += [dp2[0:nk], dp2[nk:2 * nk]]
                lse += _tokens_to_lanes(dob_ref.at[DIL_PAIRS_H + c][rows_q, :])
                delta += _tokens_to_lanes(dob_ref.at[2 * DIL_PAIRS_H + c][rows_q, :])
                qcs.append(qc)
                docs.append(doc)
                k2s.append(k2)
            yield
            pT = jnp.exp(_lanes(sT) + bias - _lanes(lse))
            dsT = pT * (_lanes(dpT) - _lanes(delta))
            dsb, pb = dsT.astype(MXU), pT.astype(MXU)
            yield
            for c in range(DIL_PAIRS_H):
                q0, q1 = slice(2 * c * BLK, (2 * c + 1) * BLK), slice((2 * c + 1) * BLK, (2 * c + 2) * BLK)
                ds2 = jnp.concatenate([dsb[:, q0], dsb[:, q1]], axis=0)
                p2 = jnp.concatenate([pb[:, q0], pb[:, q1]], axis=0)
                dq_ref.at[c][rows_q, :] += _dot_t0(ds2, k2s[c])
                dk2, dv2 = _dot(ds2, qcs[c]), _dot(p2, docs[c])
                dq_ref.at[DIL_PAIRS_H + c][rows_k, :] += jnp.where(lo, dk2[0:nk], dk2[nk:2 * nk])
                dq_ref.at[2 * DIL_PAIRS_H + c][rows_k, :] += jnp.where(lo, dv2[0:nk], dv2[nk:2 * nk])

        def body_first(p, d, r):
            rows = _stream_rows(d, r, 0, BLK)
            return block(p, d, rows, rows, first)

        def body_next(p, d, r, i):
            return block(p, d, _stream_rows(d, r, i, BLK), _stream_rows(d, r, i - 1, 2 * BLK), band)

        def part_of(rs, tab):
            q, k, v = [_lanes([dq_ref.at[2 * j][rs, :], dq_ref.at[2 * j + 1][rs, :]]) for j in range(3)]
            return _lanes([_rope(q, tab, -1) * Q_SCALE, _rope(k, tab, -1), v])

        _dil_schedule(body_first, body_next, DIL_UNROLL,
                      lambda j: _dh_tiles_behind(part_of, dh_ref, db_ref, rtab_ref, j, DIL_UNROLL))
        _interleave([_dh_tile(part_of, dh_ref, db_ref, rtab_ref, blk)
                     for blk in range(DIL_UNROLL * ((N_QBLK - 1) // DIL_UNROLL - 1), N_QBLK)])

        @pl.when(pl.program_id(0) == pl.num_programs(0) - 1)
        def _():
            rs_finish(own_refs, out_refs)

    spec = pl.BlockSpec((None, 6, SEQ, 128), lambda b: (b, 0, 0, 0))
    any_spec = pl.BlockSpec(memory_space=pl.ANY)
    vmem = pl.BlockSpec(memory_space=pltpu.VMEM)
    outs = pl.pallas_call(
        body, name="dil_bwd", grid=(B,),
        in_specs=[spec, spec, pl.BlockSpec((SEQ, 384), lambda b: (0, 0))] + [any_spec] * n_rs + [vmem] * n_rs,
        out_specs=[pl.BlockSpec((SEQ, 3 * W_B), lambda b: (b, 0)), pl.BlockSpec((8, 3 * W_B), lambda b: (0, 0))]
        + [vmem] * n_rs,
        out_shape=[jax.ShapeDtypeStruct((B * SEQ, 3 * W_B), MXU), jax.ShapeDtypeStruct((8, 3 * W_B), F32)]
        + [jax.ShapeDtypeStruct(o.shape, F32) for o in owns],
        scratch_shapes=[pltpu.VMEM((6, SEQ, 128), F32)] + [pltpu.VMEM(s.shape, s.dtype) for s in sends]
        + [pltpu.SemaphoreType.DMA((n_rs, N_DEV)), pltpu.SemaphoreType.DMA((n_rs, N_DEV))],
        compiler_params=_cparams(("arbitrary",)),
    )(qkvb, dobb, rope_tab, *sends, *owns)
    return outs[0], outs[1], outs[2:]


MEM_UNROLL = 4
MEM_PAIRS = MEM_H // 2


def _mem_attn_fwd(qc, mem, w_mem):
    B = qc.shape[0]

    def body(q_ref, mem_ref, w_ref, o_ref, lse_ref, mkv_ref, k2_ref, v2_ref):
        mkv = _dot(mem_ref[...].astype(MXU), w_ref[...])
        mkv_ref[...] = mkv.astype(MXU)
        for c in range(MEM_PAIRS):
            k2_ref[c] = _stack_pair(mkv[:, c * 128:(c + 1) * 128])
            v2_ref[c] = _stack_pair(mkv[:, W_C + c * 128:W_C + (c + 1) * 128])
        lse_ref[...] = jnp.zeros_like(lse_ref)

        def tile(blk):
            rows = pl.ds(pl.multiple_of(blk * BLK, BLK), BLK)
            sT = []
            for c in range(MEM_PAIRS):
                s2 = _dot_nt(k2_ref[c], q_ref.at[c][rows, :])
                sT += [s2[0:MEM_LEN], s2[MEM_LEN:2 * MEM_LEN]]
            yield
            pnT, lse = _softmax_cols(_lanes(sT))
            yield
            for c in range(MEM_PAIRS):
                p2 = jnp.concatenate([pnT[:, 2 * c * BLK:(2 * c + 1) * BLK],
                                      pnT[:, (2 * c + 1) * BLK:(2 * c + 2) * BLK]], axis=0)
                o_ref.at[c][rows, :] = _dot_t0(p2, v2_ref[c])
            for h in range(MEM_H):
                lse_ref.at[blk][h:h + 1, :] = lse[:, h * BLK:(h + 1) * BLK]

        def loop(j, carry):
            _interleave([tile(j * 2 * MEM_UNROLL + u) for u in range(2 * MEM_UNROLL)])
            return carry
        lax.fori_loop(0, N_QBLK // (2 * MEM_UNROLL), loop, 0)

    return pl.pallas_call(
        body, name="mem_attn_fwd", grid=(B,),
        in_specs=[pl.BlockSpec((None, MEM_PAIRS, SEQ, 128), lambda b: (b, 0, 0, 0)),
                  pl.BlockSpec((None, MEM_LEN, D_MODEL), lambda b: (b, 0, 0)),
                  pl.BlockSpec((D_MODEL, 2 * W_C), lambda b: (0, 0))],
        out_specs=[pl.BlockSpec((None, MEM_PAIRS, SEQ, 128), lambda b: (b, 0, 0, 0)),
                   pl.BlockSpec((None, N_QBLK, 8, 128), lambda b: (b, 0, 0, 0)),
                   pl.BlockSpec((None, MEM_LEN, 2 * W_C), lambda b: (b, 0, 0))],
        out_shape=[jax.ShapeDtypeStruct((B, MEM_PAIRS, SEQ, 128), F32),
                   jax.ShapeDtypeStruct((B, N_QBLK, 8, 128), F32),
                   jax.ShapeDtypeStruct((B, MEM_LEN, 2 * W_C), MXU)],
        scratch_shapes=[pltpu.VMEM((MEM_PAIRS, 2 * MEM_LEN, 128), MXU), pltpu.VMEM((MEM_PAIRS, 2 * MEM_LEN, 128), MXU)],
        compiler_params=_cparams(("arbitrary",)),
    )(qc, mem, w_mem)


def _mem_attn_bwd(qc, mkv, do, lse, delta, mem):
    B = qc.shape[0]

    def body(q_ref, mkv_ref, do_ref, lse_ref, delta_ref, mem_ref, dh_ref, db_ref, dw_ref,
             dq_ref, dmkv_ref, k2_ref, v2_ref):
        @pl.when(pl.program_id(0) == 0)
        def _():
            dw_ref[...] = jnp.zeros_like(dw_ref)
            db_ref[...] = jnp.zeros_like(db_ref)
        dmkv_ref[...] = jnp.zeros_like(dmkv_ref)
        dq_ref[...] = jnp.zeros_like(dq_ref)
        for c in range(MEM_PAIRS):
            k2_ref[c] = _stack_pair(mkv_ref[:, c * 128:(c + 1) * 128])
            v2_ref[c] = _stack_pair(mkv_ref[:, W_C + c * 128:W_C + (c + 1) * 128])

        def tile(blk):
            rows = pl.ds(pl.multiple_of(blk * BLK, BLK), BLK)
            qcs, docs, sT, dpT = [], [], [], []
            for c in range(MEM_PAIRS):
                qc_, doc = q_ref.at[c][rows, :], do_ref.at[c][rows, :]
                s2, dp2 = _dot_nt(k2_ref[c], qc_), _dot_nt(v2_ref[c], doc)
                sT += [s2[0:MEM_LEN], s2[MEM_LEN:2 * MEM_LEN]]
                dpT += [dp2[0:MEM_LEN], dp2[MEM_LEN:2 * MEM_LEN]]
                qcs.append(qc_)
                docs.append(doc)
            lse_r = _lanes([lse_ref.at[blk][h:h + 1, :] for h in range(MEM_H)])
            delta_r = _lanes([delta_ref.at[blk][h:h + 1, :] for h in range(MEM_H)])
            yield
            pT = jnp.exp(_lanes(sT) - lse_r)
            dsT = pT * (_lanes(dpT) - delta_r)
            dsb, pb = dsT.astype(MXU), pT.astype(MXU)
            yield
            for c in range(MEM_PAIRS):
                q0, q1 = slice(2 * c * BLK, (2 * c + 1) * BLK), slice((2 * c + 1) * BLK, (2 * c + 2) * BLK)
                ds2 = jnp.concatenate([dsb[:, q0], dsb[:, q1]], axis=0)
                p2 = jnp.concatenate([pb[:, q0], pb[:, q1]], axis=0)
                dq_ref.at[c][rows, :] = _dot_t0(ds2, k2_ref[c])
                dmkv_ref[:, c * 128:(c + 1) * 128] += _pair_rows(_dot(ds2, qcs[c]), MEM_LEN)
                dmkv_ref[:, W_C + c * 128:W_C + (c + 1) * 128] += _pair_rows(_dot(p2, docs[c]), MEM_LEN)

        def part_of(rs, tab):
            return _lanes([dq_ref.at[c][rs, :] for c in range(MEM_PAIRS)]) * Q_SCALE

        trips = N_QBLK // MEM_UNROLL

        def loop(j, carry):
            _interleave([tile(j * MEM_UNROLL + u) for u in range(MEM_UNROLL)]
                        + _dh_tiles_behind(part_of, dh_ref, db_ref, None, j, MEM_UNROLL))
            return carry
        lax.fori_loop(0, trips, loop, 0)
        _interleave([_dh_tile(part_of, dh_ref, db_ref, None, blk) for blk in range(MEM_UNROLL * (trips - 1), N_QBLK)])
        dw_ref[...] += _dot_tn(mem_ref[...], dmkv_ref[...].astype(MXU))

    stat = pl.BlockSpec((None, N_QBLK, 8, 128), lambda b: (b, 0, 0, 0))
    pairs = pl.BlockSpec((None, MEM_PAIRS, SEQ, 128), lambda b: (b, 0, 0, 0))
    return pl.pallas_call(
        body, name="mem_attn_bwd", grid=(B,),
        in_specs=[pairs, pl.BlockSpec((None, MEM_LEN, 2 * W_C), lambda b: (b, 0, 0)), pairs, stat, stat,
                  pl.BlockSpec((None, MEM_LEN, D_MODEL), lambda b: (b, 0, 0))],
        out_specs=[pl.BlockSpec((SEQ, W_C), lambda b: (b, 0)), pl.BlockSpec((8, W_C), lambda b: (0, 0)),
                   pl.BlockSpec((D_MODEL, 2 * W_C), lambda b: (0, 0))],
        out_shape=[jax.ShapeDtypeStruct((B * SEQ, W_C), MXU), jax.ShapeDtypeStruct((8, W_C), F32),
                   jax.ShapeDtypeStruct((D_MODEL, 2 * W_C), F32)],
        scratch_shapes=[pltpu.VMEM((MEM_PAIRS, SEQ, 128), F32), pltpu.VMEM((MEM_LEN, 2 * W_C), F32),
                        pltpu.VMEM((MEM_PAIRS, 2 * MEM_LEN, 128), MXU), pltpu.VMEM((MEM_PAIRS, 2 * MEM_LEN, 128), MXU)],
        compiler_params=_cparams(("arbitrary",)),
    )(qc, mkv, do, lse, delta, mem)


def _headsum(t, e):
    if MXU == F32:
        return _dot(t, e)
    hi = t.astype(MXU)
    lo = (t - hi.astype(F32)).astype(MXU)
    return _dot(hi, e) + _dot(lo, e)


def _heads_to_rows(t, e):
    return sum(_dot_nt(e, part) for part in _split3(t))


POST_TM = 256
POST_ROWS = 256


def _post(o_a, olse_b, o_c, z, x2, tgt, g, gain, bias, w_out, hsum, hrows):
    T = x2.shape[0]
    tm = POST_TM
    nt = SEQ // tm

    def body(oa_ref, ob_ref, oc_ref, z_ref, x_ref, t_ref, g_ref, gain_ref, bias_ref, w_ref, e_ref, er_ref,
             gx_ref, doa_ref, dela_ref, dobb_ref, doc_ref, delc_ref, dz_ref, dw_ref, small_ref, loss_ref):
        @pl.when(pl.program_id(0) == 0)
        def _():
            dw_ref[...] = jnp.zeros_like(dw_ref)
            small_ref[...] = jnp.zeros_like(small_ref)
            loss_ref[...] = jnp.zeros_like(loss_ref)

        gg = g_ref[...]
        gain_v = gain_ref[...]
        gain_s = gain_v * (1.0 / D_MODEL)
        bias_v = bias_ref[...]
        w = w_ref[...]

        def rms(o):
            rr = lax.rsqrt(jnp.mean(o * o, axis=1, keepdims=True) + RMS_EPS)
            return o * rr, rr

        def rows_of(rs, results):
            oa = _lanes([oa_ref.at[c][rs, :] for c in range(SWA_Q // 2)])
            (o1, l1), (o4, l4), (o16, l16) = [
                (_lanes([ob_ref.at[p, 0][rs, :], ob_ref.at[p, 1][rs, :]]),
                 _lanes([ob_ref.at[p, 2][rs, :], ob_ref.at[p, 3][rs, :]])) for p in range(3)]
            mx = jnp.maximum(jnp.maximum(l1, l4), l16)
            e1, e4, e16 = jnp.exp(l1 - mx), jnp.exp(l4 - mx), jnp.exp(l16 - mx)
            den = e1 + e4 + e16
            ob = (e1 * o1 + e4 * o4 + e16 * o16) / den
            lse_b = mx + jnp.log(den)
            oc = _lanes([oc_ref.at[c][rs, :] for c in range(MEM_PAIRS)])
            na, ra = rms(oa)
            nb, rb = rms(ob)
            nc, rc = rms(oc)
            n = jnp.concatenate([na, nb, nc], axis=1)
            zz = z_ref[rs, :]
            sig = 0.5 * jnp.tanh(0.5 * zz) + 0.5
            sz = zz * sig
            gs = gg * sz
            u = n * gs
            yo = _dot(u.astype(MXU), w)
            yield
            r = ALPHA * x_ref[rs, :] + yo
            rc0 = r - jnp.mean(r, axis=1, keepdims=True)
            rstd = lax.rsqrt(jnp.mean(rc0 * rc0, axis=1, keepdims=True) + LN_EPS)
            xhat = rc0 * rstd
            err = xhat * gain_v + bias_v - t_ref[rs, :]
            dxh = err * gain_s
            dr = rstd * (dxh - jnp.mean(dxh, axis=1, keepdims=True)
                         - xhat * jnp.mean(dxh * xhat, axis=1, keepdims=True))
            gx_ref[rs, :] = ALPHA * dr
            drb = dr.astype(MXU)
            du = _dot_nt(drb, w)
            yield
            dun = du * n
            dz = dun * (gg * (sig + sz * (1.0 - sig)))
            dz_ref[rs, :] = dz.astype(MXU)
            dn = du * gs

            def branch(lo, hi, nbr, rr):
                dnb = dn[:, lo:hi]
                return rr * (dnb - nbr * jnp.mean(dnb * nbr, axis=1, keepdims=True))

            def to_kernel(dob, o, do_ref, delta_ref):
                wd = dob.shape[1]
                for c in range(wd // 128):
                    do_ref.at[c][rs, :] = dob[:, c * 128:(c + 1) * 128].astype(do_ref.dtype)
                dT = _heads_to_rows(dob * o, er_ref[:, 0:wd])
                for jb in range((rs.stop - rs.start) // BLK):
                    delta_ref[rs.start // BLK + jb] = dT[0:8, jb * BLK:(jb + 1) * BLK]

            to_kernel(branch(0, W_A, na, ra), oa, doa_ref, dela_ref)
            to_kernel(branch(W_A + W_B, D_MIX, nc, rc), oc, doc_ref, delc_ref)
            dob = branch(W_A, W_A + W_B, nb, rb)
            for j, t in enumerate((dob, lse_b, _headsum(dob * ob, e_ref[...]))):
                for c in range(W_B // 128):
                    dobb_ref.at[j * (W_B // 128) + c][rs, :] = t[:, c * 128:(c + 1) * 128]
            csum = lambda t: jnp.sum(t, axis=0, keepdims=True)
            results.append((u, drb, jnp.sum(err * err), csum(err * xhat), csum(err), csum(dun * sz), csum(dz)))

        parts = []
        _interleave([rows_of(slice(k * POST_ROWS, (k + 1) * POST_ROWS), parts) for k in range(tm // POST_ROWS)])
        tot = [sum(p[i] for p in parts) for i in range(2, 7)]
        dw_ref[...] += _dot_tn(jnp.concatenate([p[0] for p in parts], axis=0),
                               jnp.concatenate([p[1] for p in parts], axis=0))
        loss_ref[...] += 0.5 * tot[0] * (1.0 / D_MODEL)
        small_ref[0:1, :] += tot[1] * (1.0 / D_MODEL)
        small_ref[1:2, :] += tot[2] * (1.0 / D_MODEL)
        small_ref[2:3, :] += tot[3]
        small_ref[3:4, :] += tot[4]

    B = T // SEQ
    row = lambda w: pl.BlockSpec((tm, w), lambda i: (i, 0))
    full = lambda a, b: pl.BlockSpec((a, b), lambda i: (0, 0))
    chunked = lambda n: pl.BlockSpec((None, n, tm, 128), lambda i: (i // nt, 0, i % nt, 0))
    stat = pl.BlockSpec((None, tm // BLK, 8, 128), lambda i: (i // nt, i % nt, 0, 0))
    return pl.pallas_call(
        body, name="post_fwd_bwd", grid=(T // tm,),
        in_specs=[chunked(SWA_Q // 2), pl.BlockSpec((None, 3, 4, tm, 128), lambda i: (i // nt, 0, 0, i % nt, 0)),
                  chunked(MEM_PAIRS),
                  row(D_MIX), row(D_MODEL), row(D_MODEL),
                  full(1, D_MIX), full(1, D_MODEL), full(1, D_MODEL), full(D_MIX, D_MODEL), full(W_B, W_B),
                  full(PICK_ROWS, W_A)],
        out_specs=[row(D_MODEL), chunked(SWA_Q // 2), stat, chunked(6), chunked(MEM_PAIRS), stat, row(D_MIX),
                   full(D_MIX, D_MODEL), full(8, D_MODEL), full(8, 128)],
        out_shape=[jax.ShapeDtypeStruct((T, D_MODEL), F32),
                   jax.ShapeDtypeStruct((B, SWA_Q // 2, SEQ, 128), MXU),
                   jax.ShapeDtypeStruct((B, N_QBLK, 8, 128), F32),
                   jax.ShapeDtypeStruct((B, 6, SEQ, 128), F32),
                   jax.ShapeDtypeStruct((B, MEM_PAIRS, SEQ, 128), MXU),
                   jax.ShapeDtypeStruct((B, N_QBLK, 8, 128), F32),
                   jax.ShapeDtypeStruct((T, D_MIX), MXU),
                   jax.ShapeDtypeStruct((D_MIX, D_MODEL), F32),
                   jax.ShapeDtypeStruct((8, D_MODEL), F32),
                   jax.ShapeDtypeStruct((8, 128), F32)],
        compiler_params=_cparams(("arbitrary",)),
    )(o_a, olse_b, o_c, z, x2, tgt, g, gain, bias, w_out, hsum, hrows)


TAIL_TK = 512
TAIL_TN = D_IN // 2
TAIL_TM = 256
REDUCE_ROWS = 128


DH_SPLITS = (0, W_A + 2 * W_KVA, W_A + 2 * W_KVA + 3 * W_B, D_IN - D_MIX, D_IN)


def _tail(xt, dhs, gx1, w_in, small_g):
    T = xt.shape[1]
    dh = dhs[0]
    c0, c1, c2, c3, c4 = DH_SPLITS
    assert c1 < TAIL_TN < c2 and (TAIL_TN - c1) % 128 == 0
    kt = T // TAIL_TK
    ndw = (D_IN // TAIL_TN) * kt
    nsteps = ndw + T // TAIL_TM
    n_pass = D_IN // TAIL_TN
    assert n_pass == 2 and TAIL_TN == 4 * COLS_PER_DEV and kt >= 2
    pay = dh.dtype
    blk_shape = (D_MODEL, COLS_PER_DEV)
    n_half = 2 * n_pass
    n_chip = N_DEV // 2

    def body(xt_ref, a1_ref, b1_ref, b2_ref, c1_ref, z1_ref, a2_ref, b3_ref, c2_ref, z2_ref,
             gx_ref, w_hbm, sg_ref, dx_ref, gin_ref, gsm_ref,
             acc_ref, w_ref, mine_ref, stagea_ref, landa_ref, stageb_ref, landb_ref, own_ref, lsm_ref,
             sa_sems, ra_sems, sb_sems, rb_sems, ss_sems, rs_sems, w_sem):
        s = pl.program_id(0)
        x, y, c = _my_pos()
        me = 4 * x + 2 * y + c
        chip = 2 * x + y

        def to_sibling(q):
            return pltpu.make_async_remote_copy(
                src_ref=stagea_ref.at[q], dst_ref=landa_ref.at[q], send_sem=sa_sems.at[q], recv_sem=ra_sems.at[q],
                device_id=(x, y, 1 - c), device_id_type=MESH)

        def to_owner(q):
            return pltpu.make_async_remote_copy(
                src_ref=stageb_ref.at[q], dst_ref=landb_ref.at[chip], send_sem=sb_sems.at[q],
                recv_sem=rb_sems.at[chip], device_id=(q // 2, q % 2, c), device_id_type=MESH)

        def from_chip(m):
            return pltpu.make_async_remote_copy(
                src_ref=landb_ref.at[m], dst_ref=landb_ref.at[m], send_sem=sb_sems.at[m], recv_sem=rb_sems.at[m],
                device_id=(m // 2, m % 2, c), device_id_type=MESH)

        def is_me(q):
            return (x == q // 2) & (y == q % 2)

        def small_to(j):
            return pltpu.make_async_remote_copy(
                src_ref=sg_ref, dst_ref=lsm_ref.at[me], send_sem=ss_sems.at[j], recv_sem=rs_sems.at[me],
                device_id=_dev_coords(j), device_id_type=MESH)

        def small_from(m):
            return pltpu.make_async_remote_copy(
                src_ref=lsm_ref.at[m], dst_ref=lsm_ref.at[m], send_sem=ss_sems.at[m], recv_sem=rs_sems.at[m],
                device_id=_dev_coords(m), device_id_type=MESH)

        w_copy = pltpu.make_async_copy(w_hbm, w_ref, w_sem)

        @pl.when(s == 0)
        def _():
            w_copy.start()
            for j in range(N_DEV):
                pl.when(me != j)(small_to(j).start)
            lsm_ref[me] = sg_ref[...]
            landb_ref[chip] = jnp.zeros(blk_shape, pay)

        @pl.when(s < ndw)
        def _():
            @pl.when(s % kt == 0)
            def _():
                acc_ref[...] = jnp.zeros_like(acc_ref)
            xt_ = xt_ref[...]
            @pl.when(s < kt)
            def _():
                acc_ref[:, 0:c1] += _dot(xt_, a1_ref[...])
                acc_ref[:, c1:TAIL_TN] += _dot(xt_, b1_ref[...])

            @pl.when(s >= kt)
            def _():
                acc_ref[:, 0:c2 - TAIL_TN] += _dot(xt_, b2_ref[...])
                acc_ref[:, c2 - TAIL_TN:c3 - TAIL_TN] += _dot(xt_, c1_ref[...])
                acc_ref[:, c3 - TAIL_TN:c4 - TAIL_TN] += _dot(xt_, z1_ref[...])

        for p in range(n_pass):
            @pl.when(s == p * kt + kt - 1)
            def _(p=p):
                for cc in range(2):
                    @pl.when(c == cc)
                    def _(cc=cc):
                        for yo in range(2):
                            q = 2 * p + yo
                            same, other = 2 * yo + cc, 2 * yo + 1 - cc
                            mine_ref[q] = acc_ref[:, same * COLS_PER_DEV:(same + 1) * COLS_PER_DEV]
                            stagea_ref[q] = acc_ref[:, other * COLS_PER_DEV:(other + 1) * COLS_PER_DEV].astype(pay)
                for yo in range(2):
                    to_sibling(2 * p + yo).start()

            @pl.when(s == (p + 1) * kt + 1)
            def _(p=p):
                for yo in range(2):
                    q = 2 * p + yo
                    to_sibling(q).wait_recv()

                    def chunk(i, carry, q=q):
                        rs = pl.ds(pl.multiple_of(i * REDUCE_ROWS, REDUCE_ROWS), REDUCE_ROWS)
                        tot = mine_ref[q, rs, :] + landa_ref[q, rs, :].astype(F32)

                        @pl.when(is_me(q))
                        def _():
                            own_ref[rs, :] = tot

                        @pl.when(jnp.logical_not(is_me(q)))
                        def _():
                            stageb_ref[q, rs, :] = tot.astype(pay)
                        return carry
                    lax.fori_loop(0, D_MODEL // REDUCE_ROWS, chunk, 0)
                    pl.when(jnp.logical_not(is_me(q)))(to_owner(q).start)

        @pl.when(s >= ndw)
        def _():
            pl.when(s == ndw)(w_copy.wait)
            dx_ref[...] = (_dot_nt(a2_ref[...], w_ref[:, c0:c1]) + _dot_nt(b3_ref[...], w_ref[:, c1:c2])
                           + _dot_nt(c2_ref[...], w_ref[:, c2:c3]) + _dot_nt(z2_ref[...], w_ref[:, c3:c4])
                           + gx_ref[...])

        @pl.when(s == nsteps - 1)
        def _():
            for m in range(n_chip):
                pl.when(m != chip)(from_chip(m).wait_recv)
            for m in range(N_DEV):
                pl.when(me != m)(small_from(m).wait_recv)
            for q in range(n_half):
                to_sibling(q).wait_send()
                pl.when(jnp.logical_not(is_me(q)))(to_owner(q).wait_send)
            for j in range(N_DEV):
                pl.when(me != j)(small_to(j).wait_send)

            def chunk(i, carry):
                rs = pl.ds(pl.multiple_of(i * REDUCE_ROWS, REDUCE_ROWS), REDUCE_ROWS)
                g = own_ref[rs, :]
                for m in range(n_chip):
                    g = g + landb_ref[m, rs, :].astype(F32)
                gin_ref[rs, :] = g
                return carry
            lax.fori_loop(0, D_MODEL // REDUCE_ROWS, chunk, 0)
            g = lsm_ref[0]
            for m in range(1, N_DEV):
                g = g + lsm_ref[m]
            gsm_ref[...] = g

    dw_step = lambda s: jnp.minimum(s, ndw - 1)
    dx_step = lambda s: jnp.maximum(s - ndw, 0)
    pass0 = lambda s: jnp.minimum(s, kt - 1)
    pass1 = lambda s: jnp.clip(s - kt, 0, kt - 1)
    any_spec = pl.BlockSpec(memory_space=pl.ANY)
    vmem = pl.BlockSpec(memory_space=pltpu.VMEM)
    dma = pltpu.SemaphoreType.DMA
    scratch = [pltpu.VMEM((D_MODEL, TAIL_TN), F32), pltpu.VMEM((D_MODEL, D_IN), w_in.dtype),
               pltpu.VMEM((n_half,) + blk_shape, F32),
               pltpu.VMEM((n_half,) + blk_shape, pay), pltpu.VMEM((n_half,) + blk_shape, pay),
               pltpu.VMEM((n_half,) + blk_shape, pay), pltpu.VMEM((n_chip,) + blk_shape, pay),
               pltpu.VMEM(blk_shape, F32), pltpu.VMEM((N_DEV,) + small_g.shape, F32),
               dma((n_half,)), dma((n_half,)), dma((n_half,)), dma((n_chip,)), dma((N_DEV,)), dma((N_DEV,)), dma]
    return pl.pallas_call(
        body, name="tail_dw_dx_reduce", grid=(nsteps,),
        in_specs=[pl.BlockSpec((D_MODEL, TAIL_TK), lambda s: (0, dw_step(s) % kt)),
                  pl.BlockSpec((TAIL_TK, c1 - c0), lambda s: (pass0(s), 0)),
                  pl.BlockSpec((TAIL_TK, TAIL_TN - c1), lambda s: (pass0(s), 0)),
                  pl.BlockSpec((TAIL_TK, 128), lambda s: (pass1(s), (TAIL_TN - c1) // 128)),
                  pl.BlockSpec((TAIL_TK, c3 - c2), lambda s: (pass1(s), 0)),
                  pl.BlockSpec((TAIL_TK, c4 - c3), lambda s: (pass1(s), 0)),
                  pl.BlockSpec((TAIL_TM, c1 - c0), lambda s: (dx_step(s), 0)),
                  pl.BlockSpec((TAIL_TM, c2 - c1), lambda s: (dx_step(s), 0)),
                  pl.BlockSpec((TAIL_TM, c3 - c2), lambda s: (dx_step(s), 0)),
                  pl.BlockSpec((TAIL_TM, c4 - c3), lambda s: (dx_step(s), 0)),
                  pl.BlockSpec((TAIL_TM, D_MODEL), lambda s: (dx_step(s), 0)),
                  any_spec, vmem],
        out_specs=[pl.BlockSpec((TAIL_TM, D_MODEL), lambda s: (dx_step(s), 0)), vmem, vmem],
        out_shape=[jax.ShapeDtypeStruct((T, D_MODEL), F32), jax.ShapeDtypeStruct(blk_shape, F32),
                   jax.ShapeDtypeStruct(small_g.shape, F32)],
        scratch_shapes=scratch,
        compiler_params=_cparams(("arbitrary",)),
    )(xt, dhs[0], dhs[1], dhs[1], dhs[2], dhs[3], dhs[0], dhs[1], dhs[2], dhs[3], gx1, w_in, small_g)


def _adam_update(grads, params, carried):
    n = len(grads)

    def body(*refs):
        g_refs, p_refs, o_refs = refs[1:1 + n], refs[1 + n:1 + 4 * n], refs[2 + 4 * n:]
        for a in range(n):
            rows = g_refs[a].shape[0]
            cr = REDUCE_ROWS if rows % REDUCE_ROWS == 0 else rows
            flat2 = lambda r: r.at[0] if len(r.shape) == 3 else r
            w_ref, m_ref, v_ref = [flat2(r) for r in p_refs[3 * a:3 * a + 3]]
            go_ref, d_ref, nm_ref, nv_ref = [flat2(r) for r in o_refs[4 * a:4 * a + 4]]

            def chunk(i, carry, cr=cr, g_ref=g_refs[a], w_ref=w_ref, m_ref=m_ref, v_ref=v_ref,
                      go_ref=go_ref, d_ref=d_ref, nm_ref=nm_ref, nv_ref=nv_ref):
                rs = pl.ds(pl.multiple_of(i * cr, cr), cr)
                g = g_ref[rs, :]
                go_ref[rs, :] = g
                d_ref[rs, :], nm_ref[rs, :], nv_ref[rs, :] = _adamw(w_ref[rs, :], g, m_ref[rs, :], v_ref[rs, :])
                return carry
            lax.fori_loop(0, rows // cr, chunk, 0)

    vmem = pl.BlockSpec(memory_space=pltpu.VMEM)
    any_spec = pl.BlockSpec(memory_space=pl.ANY)
    flat = [p for grp in params for p in grp]
    outs = pl.pallas_call(
        body, name="adamw", in_specs=[any_spec] + [vmem] * (4 * n), out_specs=[any_spec] + [vmem] * (4 * n),
        out_shape=[jax.ShapeDtypeStruct(carried.shape, carried.dtype)]
        + [jax.ShapeDtypeStruct(grp[0].shape, F32) for grp in params for _ in range(4)],
        input_output_aliases={0: 0},
        compiler_params=pltpu.CompilerParams(vmem_limit_bytes=VMEM_LIMIT),
    )(carried, *grads, *flat)
    return [outs[1 + 4 * a:5 + 4 * a] for a in range(n)], outs[0]


def _step(x, mem, w_in_s, w_mem_s, w_out_s, b_in, sinks, g, gain, bias, tgt):
    B = x.shape[0]
    T = B * SEQ
    x2 = x.reshape(T, D_MODEL)
    t2 = tgt.reshape(T, D_MODEL)
    rope_inv = _rope_inv()
    lane = jnp.arange(W_A)
    hsum = (lane[:W_B, None] // HEAD == lane[None, :W_B] // HEAD).astype(MXU)
    hrows = (jnp.arange(PICK_ROWS)[:, None] == lane[None, :] // HEAD).astype(MXU)
    me = 4 * lax.axis_index("x") + 2 * lax.axis_index("y") + lax.axis_index("c")

    (w_in_all,) = _gather_weights([w_in_s])
    qkva, qkvb, qc, z, w_in, xt, rope_tab, w_mem_all, w_out_all = _in_proj(
        x2, w_in_all, b_in, rope_inv, [w_mem_s, w_out_s])
    w_mem = w_mem_all.reshape(D_MODEL, 2 * W_C)
    w_out = w_out_all.reshape(D_MIX, D_MODEL)

    o_a, lse_a = _swa_fwd(qkva, sinks)
    olse_b = _dil_fwd(qkvb)
    o_c, lse_c, mkv = _mem_attn_fwd(qc, mem, w_mem)

    gx1, do_a, delta_a, dobb, do_c, delta_c, dz, dw_out, small, loss = _post(
        o_a, olse_b, o_c, z, x2, t2, g, gain, bias, w_out, hsum, hrows)

    dh_c, db_c, dw_mem = _mem_attn_bwd(qc, mkv, do_c, lse_c, delta_c, mem)
    blocks = [dw_mem.reshape(N_DEV, ROWS_PER_DEV, 2 * W_C), dw_out.reshape(N_DEV, ROWS_PER_DEV, D_MODEL)]
    sends = [b.astype(MXU) for b in blocks]
    owns = [lax.dynamic_index_in_dim(b, me, axis=0, keepdims=False) for b in blocks]
    dh_b, db_b, (g_mem, g_out) = _dil_bwd(qkvb, dobb, rope_tab, sends, owns)
    dh_a, db_a, dsink = _swa_bwd(qkva, do_a, lse_a, delta_a, sinks, rope_tab)

    small_g = _pack_small(dict(b_in=jnp.concatenate([db_a[0], db_b[0], db_c[0], small[3]]), sinks=dsink[:, 0],
                               g=small[2], gain=small[0], bias=small[1], loss=loss[0, 0]))
    grad_x, g_in, g_small = _tail(xt, (dh_a, dh_b, dh_c, dz), gx1, w_in, small_g)
    return grad_x.reshape(B, SEQ, D_MODEL), g_in, g_mem, g_out, g_small


def _my_pos():
    return lax.axis_index("x"), lax.axis_index("y"), lax.axis_index("c")


def _gather_weights(shards):
    n_arr = len(shards)

    def body(*refs):
        ins, outs = refs[0:n_arr], refs[n_arr:2 * n_arr]
        send_sems, recv_sems, local_sems = refs[2 * n_arr:]
        x, y, c = _my_pos()
        me, sibling = (x, y, c), (x, y, 1 - c)
        chips = [(1 - x, y), (x, 1 - y), (1 - x, 1 - y)]

        def slot(a, pos):
            return outs[a].at[4 * pos[0] + 2 * pos[1] + pos[2]]

        def copy(a, k, block, to, src=None):
            return pltpu.make_async_remote_copy(
                src_ref=slot(a, block) if src is None else src, dst_ref=slot(a, block),
                send_sem=send_sems.at[a, k], recv_sem=recv_sems.at[a, k],
                device_id=to, device_id_type=MESH)

        mine = [pltpu.make_async_copy(ins[a], slot(a, me), local_sems.at[a]) for a in range(n_arr)]
        for cp in mine:
            cp.start()
        first = []
        for a in range(n_arr):
            first.append(copy(a, 0, me, sibling, src=ins[a]))
            first += [copy(a, 1 + j, me, (*chip, c), src=ins[a]) for j, chip in enumerate(chips)]
        for cp in first:
            cp.start()
        passed = []
        for j, chip in enumerate(chips):
            for a in range(n_arr):
                copy(a, 1 + j, (*chip, c), me).wait_recv()
                fwd = copy(a, 4 + j, (*chip, c), sibling)
                fwd.start()
                passed.append(fwd)
        for a in range(n_arr):
            copy(a, 0, sibling, me).wait_recv()
            for j, chip in enumerate(chips):
                copy(a, 4 + j, (*chip, 1 - c), me).wait_recv()
        for cp in first + passed:
            cp.wait_send()
        for cp in mine:
            cp.wait()

    any_spec = pl.BlockSpec(memory_space=pl.ANY)
    return pl.pallas_call(
        body, name="gather_weights",
        in_specs=[any_spec] * n_arr, out_specs=[any_spec] * n_arr,
        out_shape=[jax.ShapeDtypeStruct((N_DEV,) + s.shape, s.dtype) for s in shards],
        scratch_shapes=[pltpu.SemaphoreType.DMA((n_arr, 7)), pltpu.SemaphoreType.DMA((n_arr, 7)),
                        pltpu.SemaphoreType.DMA((n_arr,))],
    )(*shards)


def _adamw(w, g, m, v):
    m = ADAM_B1 * m + (1.0 - ADAM_B1) * g
    v = ADAM_B2 * v + (1.0 - ADAM_B2) * (g * g)
    m_hat = m / (1.0 - ADAM_B1 ** ADAM_STEP)
    v_hat = v / (1.0 - ADAM_B2 ** ADAM_STEP)
    delta = -ADAM_LR * (m_hat / (jnp.sqrt(v_hat) + ADAM_EPS) + ADAM_WD * w)
    return delta, m, v


_SMALL_SIZES = (("b_in", D_IN), ("g", D_MIX), ("gain", D_MODEL), ("bias", D_MODEL), ("sinks", SWA_Q), ("loss", 1))


def _pack_small(d):
    flat = jnp.concatenate([jnp.reshape(d[k], (-1,)).astype(F32) if k in d else jnp.zeros((n,), F32)
                            for k, n in _SMALL_SIZES])
    flat = jnp.pad(flat, (0, SMALL_ROWS * 128 - flat.shape[0]))
    return flat.reshape(SMALL_ROWS, 128)


def _unpack_small(p):
    flat = p.reshape(-1)
    out, off = {}, 0
    for k, n in _SMALL_SIZES:
        out[k] = flat[off:off + n].reshape(1, n)
        off += n
    return out


def kernel(x, mem, w_in, b_in, w_mem, attn_sinks, g_branch, w_out, ln_gain, ln_bias, loss_target, m_w_in, m_b_in, m_w_mem, m_attn_sinks, m_g_branch, m_w_out, m_ln_gain, m_ln_bias, v_w_in, v_b_in, v_w_mem, v_attn_sinks, v_g_branch, v_w_out, v_ln_gain, v_ln_bias):
    grad_x, g_in, g_mem, g_out, g_small = _step(
        x, mem, w_in[0].astype(MXU), w_mem[0].astype(MXU), w_out[0].astype(MXU), b_in, attn_sinks[0],
        g_branch, ln_gain, ln_bias, loss_target)

    small_w = _pack_small(dict(b_in=b_in, g=g_branch, gain=ln_gain, bias=ln_bias, sinks=attn_sinks))
    small_m = _pack_small(dict(b_in=m_b_in, g=m_g_branch, gain=m_ln_gain, bias=m_ln_bias, sinks=m_attn_sinks))
    small_v = _pack_small(dict(b_in=v_b_in, g=v_g_branch, gain=v_ln_gain, bias=v_ln_bias, sinks=v_attn_sinks))
    grads = [g_in, g_mem, g_out, g_small]
    params = [(w_in, m_w_in, v_w_in), (w_mem, m_w_mem, v_w_mem), (w_out, m_w_out, v_w_out),
              (small_w, small_m, small_v)]
    res, grad_x = _adam_update(grads, params, grad_x)
    big = res[:3]
    sm = [_unpack_small(r) for r in res[3]]

    def group(i):
        return (big[0][i], sm[i]["b_in"], big[1][i], sm[i]["sinks"], sm[i]["g"], big[2][i],
                sm[i]["gain"], sm[i]["bias"])

    loss = sm[0]["loss"].reshape(())
    return (loss, grad_x, *group(0), *group(1), *group(2), *group(3))
```
